```python
import math
import numpy as np
import jax
import jax.numpy as jnp
from jax import lax

D_MODEL = 2048
BATCH = 1
SEQ = 16384
DEPTH = 2

GRID_W = 64
CTX_LEN = 256
N_MOD = 6
EPS = 1e-6

POOL_WINDOWS = (2, 4, 8, 16)
N_POOL_GROUPS = 4
POOL_GROUP = 3 * D_MODEL // 16
W_POOL = N_POOL_GROUPS * POOL_GROUP
W_S5 = D_MODEL // 4
S5_CH = 16
S5_STATE = 64
S5_GROUPS = W_S5 // S5_CH
S5_DT_MIN = 1e-3
S5_DT_MAX = 1e-1
W_IN_EVEN = W_POOL + W_S5
W_OUT_EVEN = W_POOL + W_S5

GLA_HEADS = 4
GLA_DK = D_MODEL // 16
GLA_DV = D_MODEL // 8
GLA_RANK = 16
GLA_TAU = 16.0
GLA_CHUNK = 64
W_GLA_K = GLA_HEADS * GLA_DK
W_GLA_V = GLA_HEADS * GLA_DV
ATT_HEAD_DIM = 128
ATT_HEADS = D_MODEL // 2 // ATT_HEAD_DIM
ATT_KV_HEADS = ATT_HEADS // 4
ATT_GROUP = ATT_HEADS // ATT_KV_HEADS
W_ATT = ATT_HEADS * ATT_HEAD_DIM
W_ATT_KV = ATT_KV_HEADS * ATT_HEAD_DIM
Q_BLOCK = 128
ROPE_THETA = 10000.0
ODD_WIDTHS = (W_GLA_K, W_GLA_K, W_GLA_V, GLA_RANK, W_GLA_V, W_ATT, W_ATT_KV, W_ATT_KV)
W_IN_ODD = 2 * W_GLA_K + 2 * W_GLA_V + GLA_RANK + W_ATT + 2 * W_ATT_KV
W_OUT_ODD = W_GLA_V + W_ATT

D_FF = 7 * D_MODEL // 2
N_EXPERTS = 8
TOP_K = 2

N_EVEN = (DEPTH + 1) // 2
N_ODD = DEPTH // 2

kernel_name = 'hybrid_pool_s5_gla_gqa_moe_dit_trunk'


def rms_norm(x, g):
    xf = x.astype(jnp.float32)
    y = xf * lax.rsqrt(jnp.mean(xf * xf, axis=-1, keepdims=True) + EPS)
    return (y * g.astype(jnp.float32)).astype(x.dtype)


def modulation(v, w_mod, b_mod):
    m = jax.nn.silu(v) @ w_mod + b_mod
    return jnp.split(m[..., None, :], N_MOD, axis=-1)


def split_widths(z, widths):
    return jnp.split(z, np.cumsum(widths)[:-1].tolist(), axis=-1)


def identity(t):
    return t


def flip_seq(t):
    return jnp.flip(t, axis=1)


def centred_window_mean(z, w):
    L = z.shape[1]
    cs = jnp.pad(jnp.cumsum(z.astype(jnp.float32), axis=1), ((0, 0), (1, 0), (0, 0)))
    t = jnp.arange(L)
    lo = jnp.maximum(t - w // 2, 0)
    hi = jnp.minimum(t + (w - w // 2), L)
    cnt = (hi - lo).astype(jnp.float32)[None, :, None]
    return ((cs[:, hi] - cs[:, lo]) / cnt).astype(z.dtype)


def pool_mixer(z, pool_w, pool_scale):
    groups = jnp.split(z, N_POOL_GROUPS, axis=-1)
    d = jnp.stack([centred_window_mean(g, w) - g for g, w in zip(groups, POOL_WINDOWS)], axis=2)
    y = jnp.einsum('blgc,gcd->blgd', d, pool_w)
    return y.reshape(z.shape) * pool_scale


def s5_discretise(lam_re, lam_im, log_dt, b_re, b_im):
    dt = jnp.exp(log_dt.astype(jnp.float32))[:, None]
    lr, li = lam_re.astype(jnp.float32), lam_im.astype(jnp.float32)
    mag = jnp.exp(lr * dt)
    a_re, a_im = mag * jnp.cos(li * dt), mag * jnp.sin(li * dt)
    den = lr * lr + li * li
    nr, ni = a_re - 1.0, a_im
    k_re = (nr * lr + ni * li) / den
    k_im = (ni * lr - nr * li) / den
    br = k_re[..., None] * b_re - k_im[..., None] * b_im
    bi = k_re[..., None] * b_im + k_im[..., None] * b_re
    return a_re, a_im, br, bi


def complex_affine_combine(e1, e2):
    a1r, a1i, b1r, b1i = e1
    a2r, a2i, b2r, b2i = e2
    return (a2r * a1r - a2i * a1i, a2r * a1i + a2i * a1r,
            a2r * b1r - a2i * b1i + b2r, a2r * b1i + a2i * b1r + b2i)


def s5_scan(u, a_re, a_im, br, bi, h0_re, h0_im):
    x_re = jnp.einsum('blgn,gpn->blgp', u, br)
    x_im = jnp.einsum('blgn,gpn->blgp', u, bi)
    x_re = x_re.at[:, 0].add(a_re * h0_re - a_im * h0_im)
    x_im = x_im.at[:, 0].add(a_re * h0_im + a_im * h0_re)
    ar = jnp.broadcast_to(a_re, x_re.shape)
    ai = jnp.broadcast_to(a_im, x_re.shape)
    _, _, h_re, h_im = lax.associative_scan(complex_affine_combine, (ar, ai, x_re, x_im), axis=1)
    return h_re, h_im


def s5_readout(h_re, h_im, c_re, c_im):
    return (jnp.einsum('blgp,gnp->blgn', h_re, c_re.astype(jnp.float32))
            - jnp.einsum('blgp,gnp->blgn', h_im, c_im.astype(jnp.float32)))


def s5_mixer(z_ctx, z_lat, lam_re, lam_im, log_dt, b_re, b_im, c_re, c_im, d_skip, w_glu, need_ctx):
    def groups(z):
        return z.astype(jnp.float32).reshape(z.shape[0], z.shape[1], S5_GROUPS, S5_CH)
    uc, ul = groups(z_ctx), groups(z_lat)
    zero = jnp.zeros((ul.shape[0], S5_GROUPS, S5_STATE), jnp.float32)
    y_lat, y_ctx = 0.0, 0.0
    for d in range(2):
        f = flip_seq if d == 1 else identity
        a_re, a_im, br, bi = s5_discretise(lam_re[d], lam_im[d], log_dt[d], b_re[d], b_im[d])
        hc_re, hc_im = s5_scan(f(uc), a_re, a_im, br, bi, zero, zero)
        hl_re, hl_im = s5_scan(f(ul), a_re, a_im, br, bi, hc_re[:, -1], hc_im[:, -1])
        y_lat = y_lat + f(s5_readout(hl_re, hl_im, c_re[d], c_im[d]))
        if need_ctx:
            y_ctx = y_ctx + f(s5_readout(hc_re, hc_im, c_re[d], c_im[d]))
    dsk = d_skip.astype(jnp.float32).reshape(S5_GROUPS, S5_CH)

    def finish(y, u, z):
        y = jax.nn.gelu((y + u * dsk).reshape(z.shape))
        return (y * jax.nn.sigmoid(y @ w_glu.astype(jnp.float32))).astype(z.dtype)
    return finish(y_lat, ul, z_lat), (finish(y_ctx, uc, z_ctx) if need_ctx else None)


def pool_ssm_mixer(u_ctx, u_lat, w_in, w_out, pool_w, pool_scale, lam_re, lam_im, log_dt,
                   b_re, b_im, c_re, c_im, d_skip, w_glu, need_ctx):
    p_lat, s_lat = split_widths(u_lat @ w_in, (W_POOL, W_S5))
    p_ctx, s_ctx = split_widths(u_ctx @ w_in, (W_POOL, W_S5))
    s5_lat, s5_ctx = s5_mixer(s_ctx, s_lat, lam_re, lam_im, log_dt, b_re, b_im, c_re, c_im,
                              d_skip, w_glu, need_ctx)
    y_lat = jnp.concatenate([pool_mixer(p_lat, pool_w, pool_scale), s5_lat], axis=-1) @ w_out
    y_ctx = None
    if need_ctx:
        y_ctx = jnp.concatenate([pool_mixer(p_ctx, pool_w, pool_scale), s5_ctx], axis=-1) @ w_out
    return y_lat, y_ctx


def gla_heads(q, k, v):
    B_, L, _ = q.shape
    q = q.astype(jnp.float32).reshape(B_, L, GLA_HEADS, GLA_DK) * (GLA_DK ** -0.5)
    k = k.astype(jnp.float32).reshape(B_, L, GLA_HEADS, GLA_DK)
    v = v.astype(jnp.float32).reshape(B_, L, GLA_HEADS, GLA_DV)
    return q, k, v


def gla_log_gate(a, w_a, b_a):
    B_, L, _ = a.shape
    z = a.astype(jnp.float32) @ w_a.astype(jnp.float32) + b_a.astype(jnp.float32)
    return (jax.nn.log_sigmoid(z) / GLA_TAU).reshape(B_, L, GLA_HEADS, GLA_DK)


def gla_chunked(q, k, v, g, s0):
    B_, L, H, _ = q.shape
    n = L // GLA_CHUNK

    def to_chunks(t):
        return t.reshape(B_, n, GLA_CHUNK, H, t.shape[-1]).transpose(1, 0, 3, 2, 4)
    qc, kc, vc = to_chunks(q), to_chunks(k), to_chunks(v)
    bc = jnp.cumsum(to_chunks(g), axis=-2)
    mask = jnp.tril(jnp.ones((GLA_CHUNK, GLA_CHUNK), bool))

    def step(S, inp):
        qi, ki, vi, bi = inp
        qe = qi * jnp.exp(bi)
        ke = ki * jnp.exp(-bi)
        att = jnp.where(mask, jnp.einsum('bhik,bhjk->bhij', qe, ke), 0.0)
        o = jnp.einsum('bhij,bhjv->bhiv', att, vi) + jnp.einsum('bhik,bhkv->bhiv', qe, S)
        b_last = bi[:, :, -1:, :]
        S = (jnp.exp(b_last[:, :, 0, :, None]) * S
             + jnp.einsum('bhjk,bhjv->bhkv', ki * jnp.exp(b_last - bi), vi))
        return S, o
    s_fin, o = lax.scan(step, s0, (qc, kc, vc, bc))
    o = o.transpose(1, 0, 3, 2, 4).reshape(B_, L, H, v.shape[-1])
    return o, s_fin


def gla_output(o, r, norm_g):
    B_, L = o.shape[:2]
    o = rms_norm(o, norm_g.reshape(GLA_HEADS, GLA_DV)).reshape(B_, L, W_GLA_V)
    return (o * jax.nn.silu(r.astype(jnp.float32))).astype(r.dtype)


def gla_mixer(ctx_in, lat_in, w_a, b_a, norm_g, need_ctx):
    qc, kc, vc = gla_heads(*ctx_in[:3])
    ql, kl, vl = gla_heads(*lat_in[:3])
    s0 = jnp.zeros((ql.shape[0], GLA_HEADS, GLA_DK, GLA_DV), jnp.float32)
    o_lat, o_ctx = [], []
    for d in range(2):
        f = flip_seq if d == 1 else identity
        gc = gla_log_gate(ctx_in[3], w_a[d], b_a[d])
        gl = gla_log_gate(lat_in[3], w_a[d], b_a[d])
        oc, s_ctx = gla_chunked(f(qc), f(kc), f(vc), f(gc), s0)
        ol, _ = gla_chunked(f(ql), f(kl), f(vl), f(gl), s_ctx)
        o_lat.append(f(ol))
        if need_ctx:
            o_ctx.append(f(oc))
    y_lat = gla_output(o_lat[0] + o_lat[1], lat_in[4], norm_g)
    y_ctx = gla_output(o_ctx[0] + o_ctx[1], ctx_in[4], norm_g) if need_ctx else None
    return y_lat, y_ctx


def rope_1d(x, pos):
    half = x.shape[-1] // 2
    freqs = ROPE_THETA ** (-jnp.arange(half, dtype=jnp.float32) / half)
    ang = pos.astype(jnp.float32)[:, None] * freqs
    cos, sin = jnp.cos(ang)[:, None, :], jnp.sin(ang)[:, None, :]
    x1, x2 = x[..., :half].astype(jnp.float32), x[..., half:].astype(jnp.float32)
    return jnp.concatenate([x1 * cos - x2 * sin, x1 * sin + x2 * cos], axis=-1).astype(x.dtype)


def rope_2d(x, rows, cols):
    half = x.shape[-1] // 2
    return jnp.concatenate([rope_1d(x[..., :half], rows), rope_1d(x[..., half:], cols)], axis=-1)


def block_attention(q, k, v):
    B_, Lq = q.shape[:2]
    nb = Lq // Q_BLOCK
    qb = q.reshape(B_, nb, Q_BLOCK, ATT_KV_HEADS, ATT_GROUP, ATT_HEAD_DIM).transpose(1, 0, 2, 3, 4, 5)
    scale = ATT_HEAD_DIM ** -0.5

    def one_block(qi):
        s = jnp.einsum('bqhgd,bkhd->bhgqk', qi, k).astype(jnp.float32) * scale
        p = jax.nn.softmax(s, axis=-1).astype(v.dtype)
        return jnp.einsum('bhgqk,bkhd->bqhgd', p, v)
    o = lax.map(one_block, qb)
    return o.transpose(1, 0, 2, 3, 4, 5).reshape(B_, Lq, W_ATT)


def gqa_mixer(ctx_in, lat_in, q_norm, k_norm, rows, cols, need_ctx):
    def q_heads(q):
        return rms_norm(q.reshape(q.shape[0], q.shape[1], ATT_HEADS, ATT_HEAD_DIM), q_norm)

    def kv_heads(k, v):
        B_, L, _ = k.shape
        return (rms_norm(k.reshape(B_, L, ATT_KV_HEADS, ATT_HEAD_DIM), k_norm),
                v.reshape(B_, L, ATT_KV_HEADS, ATT_HEAD_DIM))
    kc, vc = kv_heads(ctx_in[1], ctx_in[2])
    kl, vl = kv_heads(lat_in[1], lat_in[2])
    ql = rope_2d(q_heads(lat_in[0]), rows, cols)
    kl = rope_2d(kl, rows, cols)
    y_lat = block_attention(ql, jnp.concatenate([kc, kl], axis=1), jnp.concatenate([vc, vl], axis=1))
    y_ctx = block_attention(q_heads(ctx_in[0]), kc, vc) if need_ctx else None
    return y_lat, y_ctx


def gla_gqa_mixer(u_ctx, u_lat, w_in, w_out, gla_w_a, gla_b_a, gla_norm, q_norm, k_norm,
                  rows, cols, need_ctx):
    zl = split_widths(u_lat @ w_in, ODD_WIDTHS)
    zc = split_widths(u_ctx @ w_in, ODD_WIDTHS)
    g_lat, g_ctx = gla_mixer(zc[:5], zl[:5], gla_w_a, gla_b_a, gla_norm, need_ctx)
    a_lat, a_ctx = gqa_mixer(zc[5:], zl[5:], q_norm, k_norm, rows, cols, need_ctx)
    y_lat = jnp.concatenate([g_lat, a_lat], axis=-1) @ w_out
    y_ctx = (jnp.concatenate([g_ctx, a_ctx], axis=-1) @ w_out) if need_ctx else None
    return y_lat, y_ctx


def swiglu(u, w_gate, w_up, w_down):
    return (jax.nn.silu(u @ w_gate) * (u @ w_up)) @ w_down


def moe_swiglu(u, w_router, b_router, w_gate, w_up, w_down):
    shp = u.shape
    t = u.reshape(-1, shp[-1])
    logits = (t @ w_router).astype(jnp.float32) + b_router.astype(jnp.float32)
    top_val, top_idx = lax.top_k(logits, TOP_K)
    gates = jax.nn.softmax(top_val, axis=-1)
    out = jnp.zeros(t.shape, jnp.float32)
    for e in range(N_EXPERTS):
        g_e = jnp.sum(jnp.where(top_idx == e, gates, 0.0), axis=-1)[:, None]
        out = out + g_e * swiglu(t, w_gate[e], w_up[e], w_down[e]).astype(jnp.float32)
    return out.astype(u.dtype).reshape(shp)


def setup_inputs(seed: int = 0) -> dict:
    key = jax.random.key(seed)
    ks = iter(jax.random.split(key, 48))
    f32 = jnp.float32

    def nrm(shape, scale):
        return jax.random.normal(next(ks), shape, f32) * scale
    D, NE, NO = D_MODEL, N_EVEN, N_ODD
    G, P, N = S5_GROUPS, S5_STATE, S5_CH
    lam_im_base = math.pi * jnp.arange(P, dtype=f32)
    return {
        'x': nrm((BATCH, SEQ, D), 1.0),
        'c': nrm((BATCH, D), 1.0),
        'ctx': nrm((BATCH, CTX_LEN, D), 1.0),
        'c_ctx': nrm((D,), 1.0),
        'w_mod': nrm((DEPTH, D, N_MOD * D), 0.5 * D ** -0.5),
        'b_mod': nrm((DEPTH, N_MOD * D), 0.02),
        'norms': 1.0 + nrm((DEPTH, 4, D), 0.05),
        'e_w_in': nrm((NE, D, W_IN_EVEN), D ** -0.5),
        'e_pool_w': nrm((NE, N_POOL_GROUPS, POOL_GROUP, POOL_GROUP), POOL_GROUP ** -0.5),
        'e_pool_scale': 1.0 + nrm((NE, W_POOL), 0.1),
        'e_s5_lam_re': -0.5 + nrm((NE, 2, G, P), 0.01),
        'e_s5_lam_im': lam_im_base + nrm((NE, 2, G, P), 0.01),
        'e_s5_log_dt': jax.random.uniform(next(ks), (NE, 2, G), f32,
                                          math.log(S5_DT_MIN), math.log(S5_DT_MAX)),
        'e_s5_b_re': nrm((NE, 2, G, P, N), (2 * N) ** -0.5),
        'e_s5_b_im': nrm((NE, 2, G, P, N), (2 * N) ** -0.5),
        'e_s5_c_re': nrm((NE, 2, G, N, P), (2 * P) ** -0.5),
        'e_s5_c_im': nrm((NE, 2, G, N, P), (2 * P) ** -0.5),
        'e_s5_d': nrm((NE, W_S5), 1.0),
        'e_s5_w_glu': nrm((NE, W_S5, W_S5), W_S5 ** -0.5),
        'e_w_out': nrm((NE, W_OUT_EVEN, D), W_OUT_EVEN ** -0.5),
        'e_ffn_gate': nrm((NE, D, D_FF), D ** -0.5),
        'e_ffn_up': nrm((NE, D, D_FF), D ** -0.5),
        'e_ffn_down': nrm((NE, D_FF, D), D_FF ** -0.5),
        'o_w_in': nrm((NO, D, W_IN_ODD), D ** -0.5),
        'o_gla_w_a': nrm((NO, 2, GLA_RANK, W_GLA_K), GLA_RANK ** -0.5),
        'o_gla_b_a': nrm((NO, 2, W_GLA_K), 0.1),
        'o_gla_norm': 1.0 + nrm((NO, W_GLA_V), 0.05),
        'o_q_norm': 1.0 + nrm((NO, ATT_HEAD_DIM), 0.05),
        'o_k_norm': 1.0 + nrm((NO, ATT_HEAD_DIM), 0.05),
        'o_w_out': nrm((NO, W_OUT_ODD, D), W_OUT_ODD ** -0.5),
        'o_router': nrm((NO, D, N_EXPERTS), D ** -0.5),
        'o_router_b': nrm((NO, N_EXPERTS), 0.01),
        'o_moe_gate': nrm((NO, N_EXPERTS, D, D_FF), D ** -0.5),
        'o_moe_up': nrm((NO, N_EXPERTS, D, D_FF), D ** -0.5),
        'o_moe_down': nrm((NO, N_EXPERTS, D_FF, D), D_FF ** -0.5),
    }


def reference(x, c, ctx, c_ctx, w_mod, b_mod, norms, e_w_in, e_pool_w, e_pool_scale,
              e_s5_lam_re, e_s5_lam_im, e_s5_log_dt, e_s5_b_re, e_s5_b_im, e_s5_c_re, e_s5_c_im,
              e_s5_d, e_s5_w_glu, e_w_out, e_ffn_gate, e_ffn_up, e_ffn_down,
              o_w_in, o_gla_w_a, o_gla_b_a, o_gla_norm, o_q_norm, o_k_norm, o_w_out,
              o_router, o_router_b, o_moe_gate, o_moe_up, o_moe_down):
    L = x.shape[1]
    n_ctx = ctx.shape[1]
    n_rows = L // GRID_W
    rows = jnp.repeat(jnp.arange(n_rows, dtype=jnp.int32), GRID_W, total_repeat_length=L)
    cols = jnp.arange(L, dtype=jnp.int32) % GRID_W
    h_lat, h_ctx = x, ctx
    for i in range(DEPTH):
        need_ctx = i < DEPTH - 1
        j = i // 2
        sh1, sc1, g1, sh2, sc2, g2 = modulation(c, w_mod[i], b_mod[i])
        sh1c, sc1c, g1c, sh2c, sc2c, g2c = modulation(c_ctx, w_mod[i], b_mod[i])
        u_lat = rms_norm(h_lat, norms[i, 0]) * (1.0 + sc1) + sh1
        u_ctx = rms_norm(h_ctx, norms[i, 0]) * (1.0 + sc1c) + sh1c
        if i % 2 == 0:
            y_lat, y_ctx = pool_ssm_mixer(u_ctx, u_lat, e_w_in[j], e_w_out[j], e_pool_w[j], e_pool_scale[j],
                                          e_s5_lam_re[j], e_s5_lam_im[j], e_s5_log_dt[j],
                                          e_s5_b_re[j], e_s5_b_im[j], e_s5_c_re[j], e_s5_c_im[j],
                                          e_s5_d[j], e_s5_w_glu[j], need_ctx)
        else:
            y_lat, y_ctx = gla_gqa_mixer(u_ctx, u_lat, o_w_in[j], o_w_out[j], o_gla_w_a[j], o_gla_b_a[j],
                                         o_gla_norm[j], o_q_norm[j], o_k_norm[j], rows, cols, need_ctx)
        h_lat = h_lat + g1 * rms_norm(y_lat, norms[i, 1])
        v_lat = rms_norm(h_lat, norms[i, 2]) * (1.0 + sc2) + sh2
        if need_ctx:
            h_ctx = h_ctx + g1c * rms_norm(y_ctx, norms[i, 1])
            v_ctx = rms_norm(h_ctx, norms[i, 2]) * (1.0 + sc2c) + sh2c
            v_all = jnp.concatenate([v_ctx, v_lat], axis=1)
        else:
            v_all = v_lat
        if i % 2 == 0:
            f_all = swiglu(v_all, e_ffn_gate[j], e_ffn_up[j], e_ffn_down[j])
        else:
            f_all = moe_swiglu(v_all, o_router[j], o_router_b[j], o_moe_gate[j], o_moe_up[j], o_moe_down[j])
        f_all = rms_norm(f_all, norms[i, 3])
        if need_ctx:
            h_ctx = h_ctx + g2c * f_all[:, :n_ctx]
            h_lat = h_lat + g2 * f_all[:, n_ctx:]
        else:
            h_lat = h_lat + g2 * f_all
    return h_lat
```

```python
import functools
import math

import jax
import jax.numpy as jnp
from jax import lax
from jax.experimental import pallas as pl
from jax.experimental.pallas import tpu as pltpu

F32 = jnp.float32
BF16 = jnp.bfloat16
HI = lax.Precision.HIGHEST
EPS = 1e-6

D_MODEL = 2048
GRID_W = 64
N_MOD = 6

POOL_WINDOWS = (2, 4, 8, 16)
POOL_GROUP = 384
W_POOL = 1536
W_S5 = 512
S5_CH = 16
S5_STATE = 64
S5_GROUPS = 32
S5_T = 32

GLA_HEADS = 4
GLA_DK = 128
GLA_DV = 256
GLA_RANK = 16
GLA_TAU = 16.0
GLA_CHUNK = 64
W_GLA_K = 512
W_GLA_V = 1024
ATT_HEAD_DIM = 128
ATT_HEADS = 8
ATT_KV_HEADS = 2
ATT_GROUP = 4
W_ATT = 1024
W_ATT_KV = 256
ROPE_THETA = 10000.0

D_FF = 7168
N_EXPERTS = 8

OQ, OK_, OV, OR, OAQ, OAK, OAV = 0, 512, 1024, 2048, 3072, 4096, 4352
W_ODD_MAIN = 4608

MIB = 2 ** 20


def _params(sem, vmem_mib):
    return pltpu.CompilerParams(dimension_semantics=sem, vmem_limit_bytes=vmem_mib * MIB)


def _pick(n, cands):
    for c in cands:
        if n % c == 0:
            return c
    raise ValueError(f"no tile for {n} in {cands}")


def _sigmoid(x):
    return 1.0 / (1.0 + jnp.exp(-x))


def _rms(x, w):
    return x * lax.rsqrt(jnp.mean(x * x, axis=-1, keepdims=True) + EPS) * w


def _norm_mod(x, nw, mod_ref, row0, n_ctx):
    xn = _rms(x, nw)
    rows = row0 + lax.broadcasted_iota(jnp.int32, (x.shape[0], 1), 0)
    is_ctx = rows < n_ctx
    sc = jnp.where(is_ctx, mod_ref[2:3, :], mod_ref[0:1, :])
    sh = jnp.where(is_ctx, mod_ref[3:4, :], mod_ref[1:2, :])
    return xn * (1.0 + sc) + sh


def _gate_rows(mod_ref, row0, tm, n_ctx):
    rows = row0 + lax.broadcasted_iota(jnp.int32, (tm, 1), 0)
    return jnp.where(rows < n_ctx, mod_ref[5:6, :], mod_ref[4:5, :])


def _mod_kernel(v_ref, w_ref, b_ref, o_ref):
    v = v_ref[...]
    s = v * _sigmoid(v)
    o_ref[...] = jnp.dot(s, w_ref[...], precision=HI, preferred_element_type=F32) + b_ref[...]


def _modulation(vecs, w_mod, b_mod):
    depth, d, n6 = w_mod.shape
    tn = 1024
    return pl.pallas_call(
        _mod_kernel,
        grid=(depth, n6 // tn),
        in_specs=[pl.BlockSpec((8, d), lambda l, j: (0, 0)),
                  pl.BlockSpec((None, d, tn), lambda l, j: (l, 0, j)),
                  pl.BlockSpec((None, 1, tn), lambda l, j: (l, 0, j))],
        out_specs=pl.BlockSpec((None, 8, tn), lambda l, j: (l, 0, j)),
        out_shape=jax.ShapeDtypeStruct((depth, 8, n6), F32),
        compiler_params=_params(("parallel", "parallel"), 40),
        name="modulation",
    )(vecs, w_mod, b_mod.reshape(depth, 1, n6))


def _mod_rows(m, sub):
    d = m.shape[1] // N_MOD
    m6 = m.reshape(8, N_MOD, d)
    sh, sc, g = m6[:, 3 * sub + 0], m6[:, 3 * sub + 1], m6[:, 3 * sub + 2]
    z = jnp.zeros((d,), F32)
    return jnp.stack([sc[0], sh[0], sc[1], sh[1], g[0], g[1], z, z])


def _nmm_kernel(h_ref, nw_ref, mod_ref, w_ref, o_ref, u_ref, *, n_ctx, tm):
    @pl.when(pl.program_id(1) == 0)
    def _():
        u = _norm_mod(h_ref[...], nw_ref[...], mod_ref, pl.program_id(0) * tm, n_ctx)
        u_ref[...] = u.astype(BF16)
    o_ref[...] = jnp.dot(u_ref[...], w_ref[...], preferred_element_type=F32).astype(o_ref.dtype)


def _nmm_aux_kernel(h_ref, nw_ref, mod_ref, w_ref, wa_ref, o_ref, oa_ref, u_ref, *, n_ctx, tm):
    @pl.when(pl.program_id(1) == 0)
    def _():
        u = _norm_mod(h_ref[...], nw_ref[...], mod_ref, pl.program_id(0) * tm, n_ctx)
        ub = u.astype(BF16)
        u_ref[...] = ub
        oa_ref[...] = jnp.dot(ub, wa_ref[...], preferred_element_type=F32)
    o_ref[...] = jnp.dot(u_ref[...], w_ref[...], preferred_element_type=F32).astype(o_ref.dtype)


def _normed_matmul(h, nw, mod, w, n_ctx, w_aux=None):
    n, d = h.shape
    nout = w.shape[1]
    tm = _pick(n, (640, 512, 256, 128))
    tn = _pick(nout, (512, 256, 128))
    in_specs = [pl.BlockSpec((tm, d), lambda i, j: (i, 0)),
                pl.BlockSpec((1, d), lambda i, j: (0, 0)),
                pl.BlockSpec((8, d), lambda i, j: (0, 0)),
                pl.BlockSpec((d, tn), lambda i, j: (0, j))]
    out_specs = pl.BlockSpec((tm, tn), lambda i, j: (i, j))
    out_shape = jax.ShapeDtypeStruct((n, nout), BF16)
    args = [h, nw.reshape(1, d), mod, w]
    if w_aux is None:
        kern = functools.partial(_nmm_kernel, n_ctx=n_ctx, tm=tm)
    else:
        na = w_aux.shape[1]
        kern = functools.partial(_nmm_aux_kernel, n_ctx=n_ctx, tm=tm)
        in_specs.append(pl.BlockSpec((d, na), lambda i, j: (0, 0)))
        out_specs = [out_specs, pl.BlockSpec((tm, na), lambda i, j: (i, 0))]
        out_shape = [out_shape, jax.ShapeDtypeStruct((n, na), F32)]
        args.append(w_aux)
    return pl.pallas_call(
        kern, grid=(n // tm, nout // tn), in_specs=in_specs, out_specs=out_specs, out_shape=out_shape,
        scratch_shapes=[pltpu.VMEM((tm, d), BF16)],
        compiler_params=_params(("parallel", "arbitrary"), 40),
        name="normed_proj",
    )(*args)


_POOL_HALO = 16


def _pool_kernel(z_ref, zp_ref, zn_ref, pw_ref, ps_ref, o_ref, ext_ref, *, n_ctx, n_all, tm):
    i = pl.program_id(0)
    row0 = i * tm
    in_ctx = row0 < n_ctx
    seq_start = jnp.where(in_ctx, 0, n_ctx)
    seq_end = jnp.where(in_ctx, n_ctx, n_all)
    has_prev = row0 > seq_start
    has_next = row0 + tm < seq_end
    hl = _POOL_HALO
    ext_ref[0:hl, :] = jnp.where(has_prev, zp_ref[...].astype(F32), 0.0)
    ext_ref[hl:hl + tm, :] = z_ref[...].astype(F32)
    ext_ref[hl + tm:hl + tm + hl, :] = jnp.where(has_next, zn_ref[...].astype(F32), 0.0)
    t = row0 - seq_start + lax.broadcasted_iota(jnp.int32, (tm, 1), 0)
    seq_len = seq_end - seq_start
    for g, w in enumerate(POOL_WINDOWS):
        c0 = g * POOL_GROUP
        acc = None
        for k in range(-(w // 2), w - w // 2):
            v = ext_ref[hl + k:hl + k + tm, c0:c0 + POOL_GROUP]
            acc = v if acc is None else acc + v
        lo = jnp.maximum(t - w // 2, 0)
        hi = jnp.minimum(t + (w - w // 2), seq_len)
        cnt = (hi - lo).astype(F32)
        dlt = acc / cnt - ext_ref[hl:hl + tm, c0:c0 + POOL_GROUP]
        y = jnp.dot(dlt.astype(BF16), pw_ref[g], preferred_element_type=F32)
        o_ref[:, c0:c0 + POOL_GROUP] = (y * ps_ref[:, c0:c0 + POOL_GROUP]).astype(o_ref.dtype)


def _pool_mixer(z, pool_w, pool_scale, n_ctx):
    n = z.shape[0]
    tm = 256
    assert n_ctx % tm == 0 and n % tm == 0
    hl = _POOL_HALO
    nh = n // hl
    per = tm // hl
    kern = functools.partial(_pool_kernel, n_ctx=n_ctx, n_all=n, tm=tm)
    return pl.pallas_call(
        kern, grid=(n // tm,),
        in_specs=[pl.BlockSpec((tm, W_POOL), lambda i: (i, 0)),
                  pl.BlockSpec((hl, W_POOL), lambda i: (jnp.maximum(i * per - 1, 0), 0)),
                  pl.BlockSpec((hl, W_POOL), lambda i: (jnp.minimum((i + 1) * per, nh - 1), 0)),
                  pl.BlockSpec((4, POOL_GROUP, POOL_GROUP), lambda i: (0, 0, 0)),
                  pl.BlockSpec((1, W_POOL), lambda i: (0, 0))],
        out_specs=pl.BlockSpec((tm, W_POOL), lambda i: (i, 0)),
        out_shape=jax.ShapeDtypeStruct((n, W_POOL), BF16),
        scratch_shapes=[pltpu.VMEM((tm + 2 * hl, W_POOL), F32)],
        compiler_params=_params(("parallel",), 32),
        name="pool_mixer",
    )(z, z, z, pool_w, pool_scale.reshape(1, W_POOL))


def _dot_hi(a, b):
    return jnp.dot(a, b, precision=HI, preferred_element_type=F32)


def _dot_t_hi(a, b):
    return lax.dot_general(a, b, (((0,), (0,)), ((), ())), precision=HI, preferred_element_type=F32)


def _s5_param_kernel(lr_ref, li_ref, lrc_ref, lic_ref, ldt_ref, br_ref, bi_ref, cr_ref, ci_ref,
                     kt_ref, qre_ref, qim_ref, pre_ref, pim_ref, at_ref):
    d = pl.program_id(0)
    t_len = S5_T
    tp = t_len + 8
    wid = t_len * S5_CH
    dt = jnp.exp(ldt_ref[...])
    lr, li = lr_ref[...], li_ref[...]
    kk = lax.broadcasted_iota(jnp.int32, (tp, S5_STATE), 0).astype(F32)
    mag = jnp.exp(kk * (lr * dt))
    ang = kk * (li * dt)
    pr, pi_ = mag * jnp.cos(ang), mag * jnp.sin(ang)
    lrc, lic = lrc_ref[...], lic_ref[...]
    magc = jnp.exp(lrc * dt)
    arc, aic = magc * jnp.cos(lic * dt), magc * jnp.sin(lic * dt)
    den = lrc * lrc + lic * lic
    nr, ni = arc - 1.0, aic
    kre = (nr * lrc + ni * lic) / den
    kim = (ni * lrc - nr * lic) / den
    bbr = kre * br_ref[...] - kim * bi_ref[...]
    bbi = kre * bi_ref[...] + kim * br_ref[...]
    col = lax.broadcasted_iota(jnp.int32, (tp, wid), 1)
    kid = lax.broadcasted_iota(jnp.int32, (tp, wid), 0)
    cq = col >> 4
    qidx = jnp.where(d == 0, t_len - 1 - cq, cq)
    pidx = jnp.where(d == 0, cq + 1, t_len - cq)
    rep0 = (kid == cq).astype(F32)
    repq = (kid == qidx).astype(F32)
    repp = (kid == pidx).astype(F32)
    til = (lax.broadcasted_iota(jnp.int32, (S5_CH, wid), 0)
           == (lax.broadcasted_iota(jnp.int32, (S5_CH, wid), 1) & (S5_CH - 1))).astype(F32)
    bt_r, bt_i = _dot_hi(bbr, til), _dot_hi(bbi, til)
    e0r, e0i = _dot_t_hi(pr, rep0), _dot_t_hi(pi_, rep0)
    x_re = e0r * bt_r - e0i * bt_i
    x_im = e0r * bt_i + e0i * bt_r
    kt_ref[...] = _dot_hi(cr_ref[...], x_re) - _dot_hi(ci_ref[...], x_im)
    eqr, eqi = _dot_t_hi(pr, repq), _dot_t_hi(pi_, repq)
    qre_ref[...] = (eqr * bt_r - eqi * bt_i).astype(qre_ref.dtype)
    qim_ref[...] = (eqr * bt_i + eqi * bt_r).astype(qim_ref.dtype)
    epr, epi = _dot_t_hi(pr, repp), _dot_t_hi(pi_, repp)
    ct_r, ct_i = _dot_t_hi(cr_ref[...], til), _dot_t_hi(ci_ref[...], til)
    pre_ref[...] = (ct_r * epr - ct_i * epi).astype(pre_ref.dtype)
    pim_ref[...] = (-(ct_r * epi + ct_i * epr)).astype(pim_ref.dtype)
    at_ref[0:1, :] = pr[t_len:t_len + 1, :]
    at_ref[1:2, :] = pi_[t_len:t_len + 1, :]


def _s5_params(lam_re, lam_im, log_dt, b_re, b_im, c_re, c_im):
    g, p, n = S5_GROUPS, S5_STATE, S5_CH
    wid = S5_T * S5_CH

    def spec(*shape):
        return pl.BlockSpec((None, None) + shape, lambda d, gi: (d, gi) + (0,) * len(shape))

    def ospec(*shape):
        return pl.BlockSpec((None, None) + shape, lambda d, gi: (gi, d) + (0,) * len(shape))
    return pl.pallas_call(
        _s5_param_kernel, grid=(2, g),
        in_specs=[spec(1, p), spec(1, p), spec(p, 1), spec(p, 1), spec(1, 1),
                  spec(p, n), spec(p, n), spec(n, p), spec(n, p)],
        out_specs=[ospec(n, wid), ospec(p, wid), ospec(p, wid), ospec(p, wid), ospec(p, wid), ospec(2, p)],
        out_shape=[jax.ShapeDtypeStruct((g, 2, n, wid), F32),
                   jax.ShapeDtypeStruct((g, 2, p, wid), BF16),
                   jax.ShapeDtypeStruct((g, 2, p, wid), BF16),
                   jax.ShapeDtypeStruct((g, 2, p, wid), BF16),
                   jax.ShapeDtypeStruct((g, 2, p, wid), BF16),
                   jax.ShapeDtypeStruct((g, 2, 2, p), F32)],
        compiler_params=_params(("parallel", "parallel"), 32),
        name="s5_params",
    )(lam_re.reshape(2, g, 1, p), lam_im.reshape(2, g, 1, p), lam_re.reshape(2, g, p, 1),
      lam_im.reshape(2, g, p, 1), log_dt.reshape(2, g, 1, 1), b_re, b_im, c_re, c_im)


def _s5_toeplitz(kt):
    g, t_len, n = S5_GROUPS, S5_T, S5_CH
    k = kt.reshape(g, 2, n, t_len, n)
    k = jnp.concatenate([k, jnp.zeros_like(k[:, :, :, :1])], axis=3)
    s = jnp.arange(t_len)[:, None]
    t = jnp.arange(t_len)[None, :]
    idx_f = jnp.where(t >= s, t - s, t_len)
    idx_b = jnp.where(s >= t, s - t, t_len)
    m = k[:, 0][:, :, idx_f, :] + k[:, 1][:, :, idx_b, :]
    return m.transpose(0, 2, 4, 3, 1).reshape(g, t_len * n, t_len * n).astype(BF16)


def _s5_state_kernel(u_ref, qre_ref, qim_ref, s_ref):
    u = u_ref[...]
    dn = (((1,), (1,)), ((), ()))
    s_ref[:, 0:128] = lax.dot_general(u, qre_ref[...], dn, preferred_element_type=F32)
    s_ref[:, 128:256] = lax.dot_general(u, qim_ref[...], dn, preferred_element_type=F32)


def _s5_scan_kernel(s_ref, are_ref, aim_ref, h_ref, *, nc, ncc):
    ar, ai = are_ref[...], aim_ref[...]
    is_f = lax.broadcasted_iota(jnp.int32, ar.shape, 1) < S5_STATE

    def body(i, carry):
        hr, hi = carry
        cf = i
        cb = jnp.where(i < ncc, ncc - 1 - i, nc + ncc - 1 - i)
        h_ref[cf, :, 0:64] = hr[:, 0:64]
        h_ref[cb, :, 64:128] = hr[:, 64:128]
        h_ref[cf, :, 128:192] = hi[:, 0:64]
        h_ref[cb, :, 192:256] = hi[:, 64:128]
        sf, sb = s_ref[cf], s_ref[cb]
        sr = jnp.where(is_f, sf[:, 0:128], sb[:, 0:128])
        si = jnp.where(is_f, sf[:, 128:256], sb[:, 128:256])
        return ar * hr - ai * hi + sr, ar * hi + ai * hr + si

    zero = jnp.zeros(ar.shape, F32)
    lax.fori_loop(0, nc, body, (zero, zero))


def _s5_out_kernel(u_ref, m_ref, h_ref, p_ref, y_ref):
    y = jnp.dot(u_ref[...], m_ref[...], preferred_element_type=F32)
    y_ref[...] = y + jnp.dot(h_ref[...].astype(BF16), p_ref[...], preferred_element_type=F32)


def _s5_mixer(s_all, kt, qre, qim, pre, pim, at, n_ctx):
    n = s_all.shape[0]
    g, t_len, wid = S5_GROUPS, S5_T, S5_T * S5_CH
    nc, ncc = n // t_len, n_ctx // t_len
    assert n % t_len == 0 and n_ctx % t_len == 0
    u = s_all.reshape(nc, t_len, g, S5_CH).transpose(2, 0, 1, 3).reshape(g, nc, wid)
    m = _s5_toeplitz(kt)
    p_all = jnp.concatenate([pre.reshape(g, 128, wid), pim.reshape(g, 128, wid)], axis=1)
    a_re = at[:, :, 0, :].reshape(g, 128)
    a_im = at[:, :, 1, :].reshape(g, 128)
    s_t = pl.pallas_call(
        _s5_state_kernel, grid=(g,),
        in_specs=[pl.BlockSpec((None, nc, wid), lambda i: (i, 0, 0)),
                  pl.BlockSpec((None, 128, wid), lambda i: (i, 0, 0)),
                  pl.BlockSpec((None, 128, wid), lambda i: (i, 0, 0))],
        out_specs=pl.BlockSpec((nc, 256), lambda i: (0, i)),
        out_shape=jax.ShapeDtypeStruct((nc, g * 256), F32),
        compiler_params=_params(("parallel",), 32),
        name="s5_chunk_state",
    )(u, qre.reshape(g, 128, wid), qim.reshape(g, 128, wid))
    h_t = pl.pallas_call(
        functools.partial(_s5_scan_kernel, nc=nc, ncc=ncc),
        out_shape=jax.ShapeDtypeStruct((nc, g, 256), F32),
        compiler_params=pltpu.CompilerParams(vmem_limit_bytes=48 * MIB),
        name="s5_chunk_scan",
    )(s_t.reshape(nc, g, 256), a_re, a_im).reshape(nc, g * 256)
    y = pl.pallas_call(
        _s5_out_kernel, grid=(g,),
        in_specs=[pl.BlockSpec((None, nc, wid), lambda i: (i, 0, 0)),
                  pl.BlockSpec((None, wid, wid), lambda i: (i, 0, 0)),
                  pl.BlockSpec((nc, 256), lambda i: (0, i)),
                  pl.BlockSpec((None, 256, wid), lambda i: (i, 0, 0))],
        out_specs=pl.BlockSpec((None, nc, wid), lambda i: (i, 0, 0)),
        out_shape=jax.ShapeDtypeStruct((g, nc, wid), F32),
        compiler_params=_params(("parallel",), 32),
        name="s5_chunk_out",
    )(u, m, h_t, p_all)
    return y.reshape(g, nc, t_len, S5_CH).transpose(1, 2, 0, 3).reshape(n, W_S5)


def _gelu_tanh(x):
    return 0.5 * x * (1.0 + jnp.tanh(math.sqrt(2.0 / math.pi) * (x + 0.044715 * (x * x * x))))


def _even_out_kernel(pool_ref, y_ref, s_ref, dsk_ref, wglu_ref, wo_ref, h_ref, nw_ref, mod_ref, o_ref,
                     *, n_ctx, tm):
    y = _gelu_tanh(y_ref[...] + s_ref[...].astype(F32) * dsk_ref[...])
    gate = jnp.dot(y.astype(BF16), wglu_ref[...], preferred_element_type=F32)
    s5 = (y * _sigmoid(gate)).astype(BF16)
    mix = jnp.dot(pool_ref[...], wo_ref[0:W_POOL, :], preferred_element_type=F32)
    mix = mix + jnp.dot(s5, wo_ref[W_POOL:W_POOL + W_S5, :], preferred_element_type=F32)
    g = _gate_rows(mod_ref, pl.program_id(0) * tm, tm, n_ctx)
    o_ref[...] = h_ref[...] + g * _rms(mix, nw_ref[...])


def _even_out(pool_out, y_s5, z, dsk, w_glu, w_out, h, nw, mod, n_ctx):
    n, d = h.shape
    tm = _pick(n, (320, 256, 128))
    return pl.pallas_call(
        functools.partial(_even_out_kernel, n_ctx=n_ctx, tm=tm), grid=(n // tm,),
        in_specs=[pl.BlockSpec((tm, W_POOL), lambda i: (i, 0)),
                  pl.BlockSpec((tm, W_S5), lambda i: (i, 0)),
                  pl.BlockSpec((tm, W_S5), lambda i: (i, W_POOL // W_S5)),
                  pl.BlockSpec((1, W_S5), lambda i: (0, 0)),
                  pl.BlockSpec((W_S5, W_S5), lambda i: (0, 0)),
                  pl.BlockSpec((d, d), lambda i: (0, 0)),
                  pl.BlockSpec((tm, d), lambda i: (i, 0)),
                  pl.BlockSpec((1, d), lambda i: (0, 0)),
                  pl.BlockSpec((8, d), lambda i: (0, 0))],
        out_specs=pl.BlockSpec((tm, d), lambda i: (i, 0)),
        out_shape=jax.ShapeDtypeStruct((n, d), F32),
        compiler_params=_params(("parallel",), 48),
        name="even_mixer_out",
    )(pool_out, y_s5, z, dsk.reshape(1, W_S5), w_glu, w_out, h, nw.reshape(1, d), mod)


def _ffn_kernel(h_ref, nw2_ref, nw3_ref, mod_ref, wg_ref, wu_ref, wd_ref, o_ref, v_ref, acc_ref,
                *, n_ctx, tm):
    f = pl.program_id(1)

    @pl.when(f == 0)
    def _():
        v = _norm_mod(h_ref[...], nw2_ref[...], mod_ref, pl.program_id(0) * tm, n_ctx)
        v_ref[...] = v.astype(BF16)
        acc_ref[...] = jnp.zeros_like(acc_ref)
    v = v_ref[...]
    a = jnp.dot(v, wg_ref[...], preferred_element_type=F32)
    b = jnp.dot(v, wu_ref[...], preferred_element_type=F32)
    act = (a * _sigmoid(a) * b).astype(BF16)
    acc_ref[...] += jnp.dot(act, wd_ref[...], preferred_element_type=F32)

    @pl.when(f == pl.num_programs(1) - 1)
    def _():
        g = _gate_rows(mod_ref, pl.program_id(0) * tm, tm, n_ctx)
        o_ref[...] = h_ref[...] + g * _rms(acc_ref[...], nw3_ref[...])


def _dense_ffn(h, nw2, nw3, mod, wg, wu, wd, n_ctx):
    n, d = h.shape
    dff = wg.shape[1]
    tm = _pick(n, (640, 512, 256, 128))
    tf = 256
    return pl.pallas_call(
        functools.partial(_ffn_kernel, n_ctx=n_ctx, tm=tm), grid=(n // tm, dff // tf),
        in_specs=[pl.BlockSpec((tm, d), lambda i, f: (i, 0)),
                  pl.BlockSpec((1, d), lambda i, f: (0, 0)),
                  pl.BlockSpec((1, d), lambda i, f: (0, 0)),
                  pl.BlockSpec((8, d), lambda i, f: (0, 0)),
                  pl.BlockSpec((d, tf), lambda i, f: (0, f)),
                  pl.BlockSpec((d, tf), lambda i, f: (0, f)),
                  pl.BlockSpec((tf, d), lambda i, f: (f, 0))],
        out_specs=pl.BlockSpec((tm, d), lambda i, f: (i, 0)),
        out_shape=jax.ShapeDtypeStruct((n, d), F32),
        scratch_shapes=[pltpu.VMEM((tm, d), BF16), pltpu.VMEM((tm, d), F32)],
        compiler_params=_params(("parallel", "arbitrary"), 52),
        name="dense_swiglu",
    )(h, nw2.reshape(1, d), nw3.reshape(1, d), mod, wg, wu, wd)


def _gla_kernel(qf_ref, kf_ref, vf_ref, af_ref, qb_ref, kb_ref, vb_ref, ab_ref, wa_ref, ba_ref,
                of_ref, ob_ref, st_ref, *, tb):
    c = GLA_CHUNK
    nsub = tb // c

    @pl.when(pl.program_id(0) == 0)
    def _():
        st_ref[...] = jnp.zeros_like(st_ref)
    ri = lax.broadcasted_iota(jnp.int32, (c, c), 0)
    ci = lax.broadcasted_iota(jnp.int32, (c, c), 1)
    ones = jnp.ones((c, 128), F32)
    scale = GLA_DK ** -0.5
    nt = (((1,), (1,)), ((), ()))
    tn = (((0,), (0,)), ((), ()))
    dirs = ((qf_ref, kf_ref, vf_ref, af_ref, of_ref), (qb_ref, kb_ref, vb_ref, ab_ref, ob_ref))
    for d, (q_ref, k_ref, v_ref, a_ref, o_ref) in enumerate(dirs):
        mask = (ri >= ci) if d == 0 else (ci >= ri)
        cum = mask.astype(F32)
        z = _dot_hi(a_ref[...], wa_ref[d]) + ba_ref[d]
        glog = (jnp.minimum(z, 0.0) - jnp.log(1.0 + jnp.exp(-jnp.abs(z)))) * (1.0 / GLA_TAU)
        order = range(nsub) if d == 0 else range(nsub - 1, -1, -1)
        for sc in order:
            r0 = sc * c
            gc = glog[r0:r0 + c, :]
            b = _dot_hi(cum, gc)
            tot = lax.dot_general(gc, ones, tn, precision=HI, preferred_element_type=F32)
            b_last = b[c - 1:c, :] if d == 0 else b[0:1, :]
            q = q_ref[r0:r0 + c, :].astype(F32) * scale
            k = k_ref[r0:r0 + c, :].astype(F32)
            qe = (q * jnp.exp(b)).astype(BF16)
            ke = (k * jnp.exp(-b)).astype(BF16)
            kd = (k * jnp.exp(b_last - b)).astype(BF16)
            for h in range(GLA_HEADS):
                ks = slice(h * GLA_DK, (h + 1) * GLA_DK)
                vs = slice(h * GLA_DV, (h + 1) * GLA_DV)
                att = lax.dot_general(qe[:, ks], ke[:, ks], nt, preferred_element_type=F32)
                att = jnp.where(mask, att, 0.0).astype(BF16)
                vh = v_ref[r0:r0 + c, vs]
                s_old = st_ref[d, h]
                o = jnp.dot(att, vh, preferred_element_type=F32)
                o = o + jnp.dot(qe[:, ks], s_old.astype(BF16), preferred_element_type=F32)
                o_ref[r0:r0 + c, vs] = o
                dec = jnp.exp(tot[ks, :])
                upd = lax.dot_general(kd[:, ks], vh, tn, preferred_element_type=F32)
                st_ref[d, h] = jnp.concatenate([s_old[:, 0:128] * dec, s_old[:, 128:256] * dec], axis=1) + upd


def _gla_mixer(zmain, a_aux, w_a, b_a, n_ctx):
    n = zmain.shape[0]
    tb = 256
    assert n % tb == 0 and n_ctx % tb == 0
    nb, ncb = n // tb, n_ctx // tb

    def bwd(s):
        return jnp.where(s < ncb, ncb - 1 - s, nb + ncb - 1 - s)
    wa = jnp.pad(w_a, ((0, 0), (0, 128 - GLA_RANK), (0, 0)))
    in_specs = []
    for order in (lambda s: s, bwd):
        in_specs += [pl.BlockSpec((tb, W_GLA_K), lambda s, o=order: (o(s), OQ // W_GLA_K)),
                     pl.BlockSpec((tb, W_GLA_K), lambda s, o=order: (o(s), OK_ // W_GLA_K)),
                     pl.BlockSpec((tb, W_GLA_V), lambda s, o=order: (o(s), OV // W_GLA_V)),
                     pl.BlockSpec((tb, 128), lambda s, o=order: (o(s), 0))]
    in_specs += [pl.BlockSpec((2, 128, W_GLA_K), lambda s: (0, 0, 0)),
                 pl.BlockSpec((2, 1, W_GLA_K), lambda s: (0, 0, 0))]
    return pl.pallas_call(
        functools.partial(_gla_kernel, tb=tb), grid=(nb,),
        in_specs=in_specs,
        out_specs=[pl.BlockSpec((tb, W_GLA_V), lambda s: (s, 0)),
                   pl.BlockSpec((tb, W_GLA_V), lambda s: (bwd(s), 0))],
        out_shape=[jax.ShapeDtypeStruct((n, W_GLA_V), F32)] * 2,
        scratch_shapes=[pltpu.VMEM((2, GLA_HEADS, GLA_DK, GLA_DV), F32)],
        compiler_params=_params(("arbitrary",), 32),
        name="gla_chunked",
    )(zmain, zmain, zmain, a_aux, zmain, zmain, zmain, a_aux, wa, b_a.reshape(2, 1, W_GLA_K))


def _qk_prep_kernel(q_ref, k_ref, qw_ref, kw_ref, qn_ref, kn_ref, *, n_ctx, tm):
    hd = ATT_HEAD_DIM
    row = pl.program_id(0) * tm + lax.broadcasted_iota(jnp.int32, (tm, hd), 0)
    lane = lax.broadcasted_iota(jnp.int32, (tm, hd), 1)
    t = row - n_ctx
    pos = jnp.where(lane < hd // 2, t // GRID_W, t % GRID_W).astype(F32)
    quarter = hd // 4
    freq = jnp.exp((lane % quarter).astype(F32) * (-math.log(ROPE_THETA) / quarter))
    ang = pos * freq
    first = (lane & quarter) == 0
    cosv = jnp.cos(ang)
    sin_s = jnp.where(first, -jnp.sin(ang), jnp.sin(ang))
    is_lat = row >= n_ctx

    def rope(x):
        sw = jnp.where(first, pltpu.roll(x, hd - quarter, 1), pltpu.roll(x, quarter, 1))
        return x * cosv + sw * sin_s
    scale = hd ** -0.5
    for h in range(ATT_HEADS):
        sl = slice(h * hd, (h + 1) * hd)
        xn = _rms(q_ref[:, sl].astype(F32), qw_ref[...])
        qn_ref[:, sl] = (rope(xn) * scale).astype(qn_ref.dtype)
    for h in range(ATT_KV_HEADS):
        sl = slice(h * hd, (h + 1) * hd)
        xn = _rms(k_ref[:, sl].astype(F32), kw_ref[...])
        kn_ref[:, sl] = jnp.where(is_lat, rope(xn), xn).astype(kn_ref.dtype)


def _qk_prep(zmain, q_norm, k_norm, n_ctx):
    n = zmain.shape[0]
    tm = 256
    assert n_ctx % tm == 0
    ncb = n_ctx // tm
    return pl.pallas_call(
        functools.partial(_qk_prep_kernel, n_ctx=n_ctx, tm=tm), grid=(n // tm,),
        in_specs=[pl.BlockSpec((tm, W_ATT), lambda i: (i, OAQ // W_ATT)),
                  pl.BlockSpec((tm, W_ATT_KV), lambda i: (i, OAK // W_ATT_KV)),
                  pl.BlockSpec((1, ATT_HEAD_DIM), lambda i: (0, 0)),
                  pl.BlockSpec((1, ATT_HEAD_DIM), lambda i: (0, 0))],
        out_specs=[pl.BlockSpec((tm, W_ATT), lambda i: (jnp.maximum(i - ncb, 0), 0)),
                   pl.BlockSpec((tm, W_ATT_KV), lambda i: (i, 0))],
        out_shape=[jax.ShapeDtypeStruct((n - n_ctx, W_ATT), BF16), jax.ShapeDtypeStruct((n, W_ATT_KV), BF16)],
        compiler_params=_params(("arbitrary",), 32),
        name="qk_norm_rope",
    )(zmain, zmain, q_norm.reshape(1, -1), k_norm.reshape(1, -1))


def _flash_kernel(q_ref, k_ref, v_ref, o_ref, m_ref, l_ref, acc_ref):
    ki = pl.program_id(2)
    hd = ATT_HEAD_DIM

    @pl.when(ki == 0)
    def _():
        m_ref[...] = jnp.full(m_ref.shape, -jnp.inf, F32)
        l_ref[...] = jnp.zeros_like(l_ref)
        acc_ref[...] = jnp.zeros_like(acc_ref)
    k = k_ref[...]
    v = v_ref[...]
    nt = (((1,), (1,)), ((), ()))
    for g in range(ATT_GROUP):
        q = q_ref[:, g * hd:(g + 1) * hd]
        s = lax.dot_general(q, k, nt, preferred_element_type=F32)
        m_prev = m_ref[g]
        m_new = jnp.maximum(m_prev, jnp.max(s, axis=-1, keepdims=True))
        alpha = jnp.exp(m_prev - m_new)
        p = jnp.exp(s - m_new)
        l_ref[g] = alpha * l_ref[g] + jnp.sum(p, axis=-1, keepdims=True)
        acc_ref[g] = alpha * acc_ref[g] + jnp.dot(p.astype(BF16), v, preferred_element_type=F32)
        m_ref[g] = m_new

    @pl.when(ki == pl.num_programs(2) - 1)
    def _():
        for g in range(ATT_GROUP):
            o_ref[:, g * hd:(g + 1) * hd] = (acc_ref[g] / l_ref[g]).astype(o_ref.dtype)


def _attention(qn, kn, zmain, n_ctx):
    lq = qn.shape[0]
    n = kn.shape[0]
    tq = _pick(lq, (512, 256, 128))
    tk = _pick(n, (640, 512, 256, 128))
    hd = ATT_HEAD_DIM
    gw = ATT_GROUP * hd
    return pl.pallas_call(
        _flash_kernel, grid=(ATT_KV_HEADS, lq // tq, n // tk),
        in_specs=[pl.BlockSpec((tq, gw), lambda h, i, j: (i, h)),
                  pl.BlockSpec((tk, hd), lambda h, i, j: (j, h)),
                  pl.BlockSpec((tk, hd), lambda h, i, j: (j, OAV // hd + h))],
        out_specs=pl.BlockSpec((tq, gw), lambda h, i, j: (i, h)),
        out_shape=jax.ShapeDtypeStruct((lq, W_ATT), BF16),
        scratch_shapes=[pltpu.VMEM((ATT_GROUP, tq, 1), F32), pltpu.VMEM((ATT_GROUP, tq, 1), F32),
                        pltpu.VMEM((ATT_GROUP, tq, hd), F32)],
        compiler_params=_params(("parallel", "parallel", "arbitrary"), 40),
        name="gqa_flash",
    )(qn, kn, zmain)


def _odd_out_kernel(of_ref, ob_ref, r_ref, att_ref, gn_ref, wo_ref, h_ref, nw_ref, mod_ref, o_ref):
    o = of_ref[...] + ob_ref[...]
    r = r_ref[...].astype(F32)
    silu_r = r * _sigmoid(r)
    parts = []
    for h in range(GLA_HEADS):
        vs = slice(h * GLA_DV, (h + 1) * GLA_DV)
        parts.append((_rms(o[:, vs], gn_ref[:, vs]) * silu_r[:, vs]).astype(BF16))
    gla = jnp.concatenate(parts, axis=1)
    mix = jnp.dot(gla, wo_ref[0:W_GLA_V, :], preferred_element_type=F32)
    mix = mix + jnp.dot(att_ref[...], wo_ref[W_GLA_V:W_GLA_V + W_ATT, :], preferred_element_type=F32)
    o_ref[...] = h_ref[...] + mod_ref[4:5, :] * _rms(mix, nw_ref[...])


def _odd_out(o_f, o_b, zmain, att, gla_norm, w_out, h_all, nw, mod, n_ctx):
    n, d = h_all.shape
    lq = n - n_ctx
    tm = _pick(lq, (256, 128))
    assert n_ctx % tm == 0
    off = n_ctx // tm
    return pl.pallas_call(
        _odd_out_kernel, grid=(lq // tm,),
        in_specs=[pl.BlockSpec((tm, W_GLA_V), lambda i: (i + off, 0)),
                  pl.BlockSpec((tm, W_GLA_V), lambda i: (i + off, 0)),
                  pl.BlockSpec((tm, W_GLA_V), lambda i: (i + off, OR // W_GLA_V)),
                  pl.BlockSpec((tm, W_ATT), lambda i: (i, 0)),
                  pl.BlockSpec((1, W_GLA_V), lambda i: (0, 0)),
                  pl.BlockSpec((d, d), lambda i: (0, 0)),
                  pl.BlockSpec((tm, d), lambda i: (i + off, 0)),
                  pl.BlockSpec((1, d), lambda i: (0, 0)),
                  pl.BlockSpec((8, d), lambda i: (0, 0))],
        out_specs=pl.BlockSpec((tm, d), lambda i: (i, 0)),
        out_shape=jax.ShapeDtypeStruct((lq, d), F32),
        compiler_params=_params(("parallel",), 48),
        name="odd_mixer_out",
    )(o_f, o_b, zmain, att, gla_norm.reshape(1, W_GLA_V), w_out, h_all, nw.reshape(1, d), mod)


def _router_kernel(h_ref, nw_ref, mod_ref, wr_ref, br_ref, v_ref, r_ref):
    x = h_ref[...]
    v = _rms(x, nw_ref[...]) * (1.0 + mod_ref[0:1, :]) + mod_ref[1:2, :]
    v_ref[...] = v.astype(BF16)
    logits = _dot_hi(v, wr_ref[...]) + br_ref[...]
    lane = lax.broadcasted_iota(jnp.int32, logits.shape, 1)
    m1 = jnp.max(logits, axis=-1, keepdims=True)
    i1 = jnp.min(jnp.where(logits == m1, lane, 128), axis=-1, keepdims=True)
    rest = jnp.where(lane == i1, -jnp.inf, logits)
    m2 = jnp.max(rest, axis=-1, keepdims=True)
    i2 = jnp.min(jnp.where(rest == m2, lane, 128), axis=-1, keepdims=True)
    e2 = jnp.exp(m2 - m1)
    g1 = 1.0 / (1.0 + e2)
    g2 = e2 / (1.0 + e2)
    out = jnp.where(lane == 0, i1.astype(F32), 0.0)
    out = jnp.where(lane == 1, i2.astype(F32), out)
    out = jnp.where(lane == 2, g1, out)
    out = jnp.where(lane == 3, g2, out)
    r_ref[...] = out


def _router(h_lat, nw, mod, w_router, b_router):
    lq, d = h_lat.shape
    tm = _pick(lq, (512, 256, 128))
    wr = jnp.pad(w_router, ((0, 0), (0, 128 - N_EXPERTS)))
    br = jnp.pad(b_router, (0, 128 - N_EXPERTS), constant_values=-1e30).reshape(1, 128)
    return pl.pallas_call(
        _router_kernel, grid=(lq // tm,),
        in_specs=[pl.BlockSpec((tm, d), lambda i: (i, 0)),
                  pl.BlockSpec((1, d), lambda i: (0, 0)),
                  pl.BlockSpec((8, d), lambda i: (0, 0)),
                  pl.BlockSpec((d, 128), lambda i: (0, 0)),
                  pl.BlockSpec((1, 128), lambda i: (0, 0))],
        out_specs=[pl.BlockSpec((tm, d), lambda i: (i, 0)), pl.BlockSpec((tm, 128), lambda i: (i, 0))],
        out_shape=[jax.ShapeDtypeStruct((lq, d), BF16), jax.ShapeDtypeStruct((lq, 128), F32)],
        compiler_params=_params(("parallel",), 32),
        name="router_top2",
    )(h_lat, nw.reshape(1, d), mod, wr, br)


def _dispatch_kernel(j_ref, b_ref, fl_ref, tok_ref, v_ref, o_ref, *, sblk):
    s = pl.program_id(0)
    fl = fl_ref[s]

    @pl.when((fl & 2) != 0)
    def _():
        o_ref[...] = jnp.zeros_like(o_ref)

    @pl.when((fl & 1) != 0)
    def _():
        cols = b_ref[s] * sblk + lax.broadcasted_iota(jnp.int32, (1, sblk), 1)
        onehot = (tok_ref[...] == cols).astype(BF16)
        o_ref[...] += jnp.dot(onehot, v_ref[...], preferred_element_type=F32).astype(o_ref.dtype)


def _expert_kernel(te_ref, ts_ref, tv_ref, x_ref, gt_ref, wg_ref, wu_ref, wd_ref, o_ref, acc_ref):
    j = pl.program_id(0)
    f = pl.program_id(1)
    valid = tv_ref[j] == 1

    @pl.when(f == 0)
    def _():
        acc_ref[...] = jnp.zeros_like(acc_ref)

    @pl.when(valid)
    def _():
        x = x_ref[...]
        a = jnp.dot(x, wg_ref[...], preferred_element_type=F32)
        b = jnp.dot(x, wu_ref[...], preferred_element_type=F32)
        act = (a * _sigmoid(a) * b).astype(BF16)
        acc_ref[...] += jnp.dot(act, wd_ref[...], preferred_element_type=F32)

    @pl.when(f == pl.num_programs(1) - 1)
    def _():
        o_ref[...] = (acc_ref[...] * gt_ref[...]).astype(o_ref.dtype)


def _combine_kernel(ci_ref, cy_ref, fl_ref, tok_ref, y_ref, h_ref, nw_ref, mod_ref, o_ref, acc_ref, *, sblk):
    s = pl.program_id(0)
    fl = fl_ref[s]

    @pl.when((fl & 2) != 0)
    def _():
        acc_ref[...] = jnp.zeros_like(acc_ref)

    @pl.when((fl & 1) != 0)
    def _():
        rows = ci_ref[s] * sblk + lax.broadcasted_iota(jnp.int32, (sblk, 1), 0)
        onehot = (tok_ref[...] == rows).astype(BF16)
        acc_ref[...] += jnp.dot(onehot, y_ref[...], preferred_element_type=F32)

    @pl.when((fl & 4) != 0)
    def _():
        o_ref[...] = h_ref[...] + mod_ref[4:5, :] * _rms(acc_ref[...], nw_ref[...])


def _count_le(sorted_vals, queries):
    return jnp.sum((sorted_vals[None, :] <= queries[:, None]).astype(jnp.int32), axis=1)


def _moe(h_lat, nw2, nw3, mod, w_router, b_router, wg, wu, wd):
    lq, d = h_lat.shape
    e = N_EXPERTS
    dff = wg.shape[2]
    tm = _pick(lq, (512, 256))
    tf = 512
    v, route = _router(h_lat, nw2, mod, w_router, b_router)
    idx = route[:, 0:2].astype(jnp.int32)
    gates = route[:, 2:4]

    npairs = 2 * lq
    nt = npairs // tm + e
    nblk = lq // tm
    flat_e = idx.reshape(-1)
    tok = jnp.arange(npairs, dtype=jnp.int32) // 2
    onehot = (flat_e[:, None] == jnp.arange(e, dtype=jnp.int32)[None, :]).astype(jnp.int32)
    csum = jnp.cumsum(onehot, axis=0)
    rank = jnp.sum(csum * onehot, axis=1) - 1
    counts = csum[-1]
    padded = ((counts + tm - 1) // tm) * tm
    ends = jnp.cumsum(padded)
    starts = ends - padded
    pos = starts[flat_e] + rank
    sorted_tok = jnp.full((nt * tm,), -1, jnp.int32).at[pos].set(tok)
    sorted_gate = jnp.zeros((nt * tm,), F32).at[pos].set(gates.reshape(-1))
    n_used = ends[-1] // tm
    tile_ids = jnp.arange(nt, dtype=jnp.int32)
    tile_valid = (tile_ids < n_used).astype(jnp.int32)
    tile_src = jnp.minimum(tile_ids, n_used - 1).astype(jnp.int32)
    tile_exp = jnp.minimum(_count_le(ends, tile_src * tm), e - 1)

    st2 = sorted_tok.reshape(nt, tm)
    blo = st2[:, 0] // tm
    bhi = jnp.max(st2, axis=1) // tm
    nb = jnp.where(tile_valid == 1, bhi - blo + 1, 0)
    cs = jnp.cumsum(nb)
    w_d = nt + e * (nblk - 1)
    sidx = jnp.arange(w_d, dtype=jnp.int32)
    sc = jnp.minimum(sidx, cs[-1] - 1)
    dj = _count_le(cs, sc)
    dstart = cs[dj] - nb[dj]
    db = (blo[dj] + sc - dstart).astype(jnp.int32)
    dact = sidx < cs[-1]
    dfl = (dact.astype(jnp.int32) + 2 * (dact & (sc == dstart)).astype(jnp.int32)).astype(jnp.int32)

    xs = pl.pallas_call(
        functools.partial(_dispatch_kernel, sblk=tm),
        grid_spec=pltpu.PrefetchScalarGridSpec(
            num_scalar_prefetch=3, grid=(w_d,),
            in_specs=[pl.BlockSpec((tm, 1), lambda s, j, b, fl: (j[s], 0)),
                      pl.BlockSpec((tm, d), lambda s, j, b, fl: (b[s], 0))],
            out_specs=pl.BlockSpec((tm, d), lambda s, j, b, fl: (j[s], 0))),
        out_shape=jax.ShapeDtypeStruct((nt * tm, d), BF16),
        compiler_params=_params(("arbitrary",), 32),
        name="moe_dispatch",
    )(dj, db, dfl, sorted_tok.reshape(nt * tm, 1), v)

    nf = dff // tf
    ys = pl.pallas_call(
        _expert_kernel,
        grid_spec=pltpu.PrefetchScalarGridSpec(
            num_scalar_prefetch=3, grid=(nt, nf),
            in_specs=[pl.BlockSpec((tm, d), lambda j, f, te, ts, tv: (ts[j], 0)),
                      pl.BlockSpec((tm, 1), lambda j, f, te, ts, tv: (ts[j], 0)),
                      pl.BlockSpec((None, d, tf), lambda j, f, te, ts, tv: (te[j], 0, jnp.where(tv[j] == 1, f, nf - 1))),
                      pl.BlockSpec((None, d, tf), lambda j, f, te, ts, tv: (te[j], 0, jnp.where(tv[j] == 1, f, nf - 1))),
                      pl.BlockSpec((None, tf, d), lambda j, f, te, ts, tv: (te[j], jnp.where(tv[j] == 1, f, nf - 1), 0))],
            out_specs=pl.BlockSpec((tm, d), lambda j, f, te, ts, tv: (j, 0)),
            scratch_shapes=[pltpu.VMEM((tm, d), F32)]),
        out_shape=jax.ShapeDtypeStruct((nt * tm, d), BF16),
        compiler_params=_params(("arbitrary", "arbitrary"), 48),
        name="moe_experts",
    )(tile_exp, tile_src, tile_valid, xs, sorted_gate.reshape(nt * tm, 1), wg, wu, wd)

    key = (tok // tm) * e + flat_e
    yj = (pos // tm).astype(jnp.int32)
    jlo = jnp.full((nblk * e,), nt, jnp.int32).at[key].min(yj)
    jhi = jnp.full((nblk * e,), -1, jnp.int32).at[key].max(yj)
    nbc = jnp.where(jhi >= 0, jhi - jlo + 1, 0)
    cs3 = jnp.cumsum(nbc)
    w_c = nt + e * (nblk - 1)
    sidx = jnp.arange(w_c, dtype=jnp.int32)
    sc = jnp.minimum(sidx, cs3[-1] - 1)
    kidx = _count_le(cs3, sc)
    ci = (kidx // e).astype(jnp.int32)
    cy = (jlo[kidx] + sc - (cs3[kidx] - nbc[kidx])).astype(jnp.int32)
    cact = sidx < cs3[-1]
    blk_first = cs3[ci * e] - nbc[ci * e]
    blk_last = cs3[ci * e + e - 1] - 1
    cfl = (cact.astype(jnp.int32) + 2 * (cact & (sc == blk_first)).astype(jnp.int32)
           + 4 * (cact & (sc == blk_last)).astype(jnp.int32)).astype(jnp.int32)

    return pl.pallas_call(
        functools.partial(_combine_kernel, sblk=tm),
        grid_spec=pltpu.PrefetchScalarGridSpec(
            num_scalar_prefetch=3, grid=(w_c,),
            in_specs=[pl.BlockSpec((None, 1, tm), lambda s, ci_, cy_, fl: (cy_[s], 0, 0)),
                      pl.BlockSpec((tm, d), lambda s, ci_, cy_, fl: (cy_[s], 0)),
                      pl.BlockSpec((tm, d), lambda s, ci_, cy_, fl: (ci_[s], 0)),
                      pl.BlockSpec((1, d), lambda s, ci_, cy_, fl: (0, 0)),
                      pl.BlockSpec((8, d), lambda s, ci_, cy_, fl: (0, 0))],
            out_specs=pl.BlockSpec((tm, d), lambda s, ci_, cy_, fl: (ci_[s], 0)),
            scratch_shapes=[pltpu.VMEM((tm, d), F32)]),
        out_shape=jax.ShapeDtypeStruct((lq, d), F32),
        compiler_params=_params(("arbitrary",), 40),
        name="moe_combine",
    )(ci, cy, cfl, sorted_tok.reshape(nt, 1, tm), ys, h_lat, nw3.reshape(1, d), mod)


def kernel(x, c, ctx, c_ctx, w_mod, b_mod, norms, e_w_in, e_pool_w, e_pool_scale, e_s5_lam_re, e_s5_lam_im, e_s5_log_dt, e_s5_b_re, e_s5_b_im, e_s5_c_re, e_s5_c_im, e_s5_d, e_s5_w_glu, e_w_out, e_ffn_gate, e_ffn_up, e_ffn_down, o_w_in, o_gla_w_a, o_gla_b_a, o_gla_norm, o_q_norm, o_k_norm, o_w_out, o_router, o_router_b, o_moe_gate, o_moe_up, o_moe_down):
    assert x.shape[0] == 1 and w_mod.shape[0] == 2 and e_w_in.shape[0] == 1 and o_w_in.shape[0] == 1
    d = x.shape[2]
    n_ctx = ctx.shape[1]
    h = jnp.concatenate([ctx[0], x[0]], axis=0)

    vecs = jnp.zeros((8, d), F32).at[0].set(c[0]).at[1].set(c_ctx)
    mods = _modulation(vecs, w_mod, b_mod)

    mod1, mod2 = _mod_rows(mods[0], 0), _mod_rows(mods[0], 1)
    z = _normed_matmul(h, norms[0, 0], mod1, e_w_in[0].astype(BF16), n_ctx)
    pool_out = _pool_mixer(z, e_pool_w[0].astype(BF16), e_pool_scale[0], n_ctx)
    kt, qre, qim, pre, pim, at = _s5_params(e_s5_lam_re[0], e_s5_lam_im[0], e_s5_log_dt[0],
                                            e_s5_b_re[0], e_s5_b_im[0], e_s5_c_re[0], e_s5_c_im[0])
    y_s5 = _s5_mixer(z[:, W_POOL:], kt, qre, qim, pre, pim, at, n_ctx)
    h = _even_out(pool_out, y_s5, z, e_s5_d[0], e_s5_w_glu[0].astype(BF16), e_w_out[0].astype(BF16),
                  h, norms[0, 1], mod1, n_ctx)
    h = _dense_ffn(h, norms[0, 2], norms[0, 3], mod2, e_ffn_gate[0].astype(BF16),
                   e_ffn_up[0].astype(BF16), e_ffn_down[0].astype(BF16), n_ctx)

    mod1, mod2 = _mod_rows(mods[1], 0), _mod_rows(mods[1], 1)
    w_in = o_w_in[0]
    a0 = 2 * W_GLA_K + W_GLA_V
    w_main = jnp.concatenate([w_in[:, :a0], w_in[:, a0 + GLA_RANK:]], axis=1).astype(BF16)
    w_aux = jnp.pad(w_in[:, a0:a0 + GLA_RANK], ((0, 0), (0, 128 - GLA_RANK))).astype(BF16)
    zmain, a_aux = _normed_matmul(h, norms[1, 0], mod1, w_main, n_ctx, w_aux=w_aux)
    o_f, o_b = _gla_mixer(zmain, a_aux, o_gla_w_a[0], o_gla_b_a[0], n_ctx)
    qn, kn = _qk_prep(zmain, o_q_norm[0], o_k_norm[0], n_ctx)
    att = _attention(qn, kn, zmain, n_ctx)
    h_lat = _odd_out(o_f, o_b, zmain, att, o_gla_norm[0], o_w_out[0].astype(BF16), h, norms[1, 1], mod1, n_ctx)
    out = _moe(h_lat, norms[1, 2], norms[1, 3], mod2, o_router[0], o_router_b[0],
               o_moe_gate[0].astype(BF16), o_moe_up[0].astype(BF16), o_moe_down[0].astype(BF16))
    return out[None]
```

```python
import functools
import math

import jax
import jax.numpy as jnp
from jax import lax
from jax.experimental import pallas as pl
from jax.experimental.pallas import tpu as pltpu

F32 = jnp.float32
BF16 = jnp.bfloat16
HI = lax.Precision.HIGHEST
EPS = 1e-6

D_MODEL = 2048
GRID_W = 64
N_MOD = 6

POOL_WINDOWS = (2, 4, 8, 16)
POOL_GROUP = 384
W_POOL = 1536
W_S5 = 512
S5_CH = 16
S5_STATE = 64
S5_GROUPS = 32
S5_T = 32

GLA_HEADS = 4
GLA_DK = 128
GLA_DV = 256
GLA_RANK = 16
GLA_TAU = 16.0
GLA_CHUNK = 64
W_GLA_K = 512
W_GLA_V = 1024
ATT_HEAD_DIM = 128
ATT_HEADS = 8
ATT_KV_HEADS = 2
ATT_GROUP = 4
W_ATT = 1024
W_ATT_KV = 256
ROPE_THETA = 10000.0

D_FF = 7168
N_EXPERTS = 8

OQ, OK_, OV, OR, OAQ, OAK, OAV = 0, 512, 1024, 2048, 3072, 4096, 4352
W_ODD_MAIN = 4608

MIB = 2 ** 20


def _params(sem, vmem_mib):
    return pltpu.CompilerParams(dimension_semantics=sem, vmem_limit_bytes=vmem_mib * MIB)


def _pick(n, cands):
    for c in cands:
        if n % c == 0:
            return c
    raise ValueError(f"no tile for {n} in {cands}")


def _sigmoid(x):
    return 1.0 / (1.0 + jnp.exp(-x))


def _rms(x, w):
    return x * lax.rsqrt(jnp.mean(x * x, axis=-1, keepdims=True) + EPS) * w


def _norm_mod(x, nw, mod_ref, row0, n_ctx):
    xn = _rms(x, nw)
    rows = row0 + lax.broadcasted_iota(jnp.int32, (x.shape[0], 1), 0)
    is_ctx = rows < n_ctx
    sc = jnp.where(is_ctx, mod_ref[2:3, :], mod_ref[0:1, :])
    sh = jnp.where(is_ctx, mod_ref[3:4, :], mod_ref[1:2, :])
    return xn * (1.0 + sc) + sh


def _gate_rows(mod_ref, row0, tm, n_ctx):
    rows = row0 + lax.broadcasted_iota(jnp.int32, (tm, 1), 0)
    return jnp.where(rows < n_ctx, mod_ref[5:6, :], mod_ref[4:5, :])


def _mod_kernel(v_ref, w_ref, b_ref, o_ref):
    v = v_ref[...]
    s = v * _sigmoid(v)
    o_ref[...] = jnp.dot(s, w_ref[...], precision=HI, preferred_element_type=F32) + b_ref[...]


def _modulation(vecs, w_mod, b_mod):
    depth, d, n6 = w_mod.shape
    tn = 1024
    return pl.pallas_call(
        _mod_kernel,
        grid=(depth, n6 // tn),
        in_specs=[pl.BlockSpec((8, d), lambda l, j: (0, 0)),
                  pl.BlockSpec((None, d, tn), lambda l, j: (l, 0, j)),
                  pl.BlockSpec((None, 1, tn), lambda l, j: (l, 0, j))],
        out_specs=pl.BlockSpec((None, 8, tn), lambda l, j: (l, 0, j)),
        out_shape=jax.ShapeDtypeStruct((depth, 8, n6), F32),
        compiler_params=_params(("parallel", "parallel"), 40),
        name="modulation",
    )(vecs, w_mod, b_mod.reshape(depth, 1, n6))


def _mod_rows(m, sub):
    d = m.shape[1] // N_MOD
    m6 = m.reshape(8, N_MOD, d)
    sh, sc, g = m6[:, 3 * sub + 0], m6[:, 3 * sub + 1], m6[:, 3 * sub + 2]
    z = jnp.zeros((d,), F32)
    return jnp.stack([sc[0], sh[0], sc[1], sh[1], g[0], g[1], z, z])


def _nmm_kernel(h_ref, nw_ref, mod_ref, w_ref, o_ref, u_ref, *, n_ctx, tm):
    @pl.when(pl.program_id(1) == 0)
    def _():
        u = _norm_mod(h_ref[...], nw_ref[...], mod_ref, pl.program_id(0) * tm, n_ctx)
        u_ref[...] = u.astype(BF16)
    o_ref[...] = jnp.dot(u_ref[...], w_ref[...], preferred_element_type=F32).astype(o_ref.dtype)


def _nmm_aux_kernel(h_ref, nw_ref, mod_ref, w_ref, wa_ref, o_ref, oa_ref, u_ref, *, n_ctx, tm):
    @pl.when(pl.program_id(1) == 0)
    def _():
        u = _norm_mod(h_ref[...], nw_ref[...], mod_ref, pl.program_id(0) * tm, n_ctx)
        ub = u.astype(BF16)
        u_ref[...] = ub
        oa_ref[...] = jnp.dot(ub, wa_ref[...], preferred_element_type=F32)
    o_ref[...] = jnp.dot(u_ref[...], w_ref[...], preferred_element_type=F32).astype(o_ref.dtype)


def _normed_matmul(h, nw, mod, w, n_ctx, w_aux=None):
    n, d = h.shape
    nout = w.shape[1]
    tm = _pick(n, (640, 512, 256, 128))
    tn = _pick(nout, (512, 256, 128))
    in_specs = [pl.BlockSpec((tm, d), lambda i, j: (i, 0)),
                pl.BlockSpec((1, d), lambda i, j: (0, 0)),
                pl.BlockSpec((8, d), lambda i, j: (0, 0)),
                pl.BlockSpec((d, tn), lambda i, j: (0, j))]
    out_specs = pl.BlockSpec((tm, tn), lambda i, j: (i, j))
    out_shape = jax.ShapeDtypeStruct((n, nout), BF16)
    args = [h, nw.reshape(1, d), mod, w]
    if w_aux is None:
        kern = functools.partial(_nmm_kernel, n_ctx=n_ctx, tm=tm)
    else:
        na = w_aux.shape[1]
        kern = functools.partial(_nmm_aux_kernel, n_ctx=n_ctx, tm=tm)
        in_specs.append(pl.BlockSpec((d, na), lambda i, j: (0, 0)))
        out_specs = [out_specs, pl.BlockSpec((tm, na), lambda i, j: (i, 0))]
        out_shape = [out_shape, jax.ShapeDtypeStruct((n, na), F32)]
        args.append(w_aux)
    return pl.pallas_call(
        kern, grid=(n // tm, nout // tn), in_specs=in_specs, out_specs=out_specs, out_shape=out_shape,
        scratch_shapes=[pltpu.VMEM((tm, d), BF16)],
        compiler_params=_params(("parallel", "arbitrary"), 40),
        name="normed_proj",
    )(*args)


_POOL_HALO = 16


def _pool_kernel(z_ref, zp_ref, zn_ref, pw_ref, ps_ref, o_ref, ext_ref, *, n_ctx, n_all, tm):
    i = pl.program_id(0)
    row0 = i * tm
    in_ctx = row0 < n_ctx
    seq_start = jnp.where(in_ctx, 0, n_ctx)
    seq_end = jnp.where(in_ctx, n_ctx, n_all)
    has_prev = row0 > seq_start
    has_next = row0 + tm < seq_end
    hl = _POOL_HALO
    ext_ref[0:hl, :] = jnp.where(has_prev, zp_ref[...].astype(F32), 0.0)
    ext_ref[hl:hl + tm, :] = z_ref[...].astype(F32)
    ext_ref[hl + tm:hl + tm + hl, :] = jnp.where(has_next, zn_ref[...].astype(F32), 0.0)
    t = row0 - seq_start + lax.broadcasted_iota(jnp.int32, (tm, 1), 0)
    seq_len = seq_end - seq_start
    for g, w in enumerate(POOL_WINDOWS):
        c0 = g * POOL_GROUP
        acc = None
        for k in range(-(w // 2), w - w // 2):
            v = ext_ref[hl + k:hl + k + tm, c0:c0 + POOL_GROUP]
            acc = v if acc is None else acc + v
        lo = jnp.maximum(t - w // 2, 0)
        hi = jnp.minimum(t + (w - w // 2), seq_len)
        cnt = (hi - lo).astype(F32)
        dlt = acc / cnt - ext_ref[hl:hl + tm, c0:c0 + POOL_GROUP]
        y = jnp.dot(dlt.astype(BF16), pw_ref[g], preferred_element_type=F32)
        o_ref[:, c0:c0 + POOL_GROUP] = (y * ps_ref[:, c0:c0 + POOL_GROUP]).astype(o_ref.dtype)


def _pool_mixer(z, pool_w, pool_scale, n_ctx):
    n = z.shape[0]
    tm = 256
    assert n_ctx % tm == 0 and n % tm == 0
    hl = _POOL_HALO
    nh = n // hl
    per = tm // hl
    kern = functools.partial(_pool_kernel, n_ctx=n_ctx, n_all=n, tm=tm)
    return pl.pallas_call(
        kern, grid=(n // tm,),
        in_specs=[pl.BlockSpec((tm, W_POOL), lambda i: (i, 0)),
                  pl.BlockSpec((hl, W_POOL), lambda i: (jnp.maximum(i * per - 1, 0), 0)),
                  pl.BlockSpec((hl, W_POOL), lambda i: (jnp.minimum((i + 1) * per, nh - 1), 0)),
                  pl.BlockSpec((4, POOL_GROUP, POOL_GROUP), lambda i: (0, 0, 0)),
                  pl.BlockSpec((1, W_POOL), lambda i: (0, 0))],
        out_specs=pl.BlockSpec((tm, W_POOL), lambda i: (i, 0)),
        out_shape=jax.ShapeDtypeStruct((n, W_POOL), BF16),
        scratch_shapes=[pltpu.VMEM((tm + 2 * hl, W_POOL), F32)],
        compiler_params=_params(("parallel",), 32),
        name="pool_mixer",
    )(z, z, z, pool_w, pool_scale.reshape(1, W_POOL))


def _dot_hi(a, b):
    return jnp.dot(a, b, precision=HI, preferred_element_type=F32)


def _dot_t_hi(a, b):
    return lax.dot_general(a, b, (((0,), (0,)), ((), ())), precision=HI, preferred_element_type=F32)


def _s5_param_kernel(lr_ref, li_ref, lrc_ref, lic_ref, ldt_ref, br_ref, bi_ref, cr_ref, ci_ref,
                     kt_ref, qre_ref, qim_ref, pre_ref, pim_ref, at_ref):
    d = pl.program_id(0)
    t_len = S5_T
    tp = t_len + 8
    wid = t_len * S5_CH
    dt = jnp.exp(ldt_ref[...])
    lr, li = lr_ref[...], li_ref[...]
    kk = lax.broadcasted_iota(jnp.int32, (tp, S5_STATE), 0).astype(F32)
    mag = jnp.exp(kk * (lr * dt))
    ang = kk * (li * dt)
    pr, pi_ = mag * jnp.cos(ang), mag * jnp.sin(ang)
    lrc, lic = lrc_ref[...], lic_ref[...]
    magc = jnp.exp(lrc * dt)
    arc, aic = magc * jnp.cos(lic * dt), magc * jnp.sin(lic * dt)
    den = lrc * lrc + lic * lic
    nr, ni = arc - 1.0, aic
    kre = (nr * lrc + ni * lic) / den
    kim = (ni * lrc - nr * lic) / den
    bbr = kre * br_ref[...] - kim * bi_ref[...]
    bbi = kre * bi_ref[...] + kim * br_ref[...]
    col = lax.broadcasted_iota(jnp.int32, (tp, wid), 1)
    kid = lax.broadcasted_iota(jnp.int32, (tp, wid), 0)
    cq = col >> 4
    qidx = jnp.where(d == 0, t_len - 1 - cq, cq)
    pidx = jnp.where(d == 0, cq + 1, t_len - cq)
    rep0 = (kid == cq).astype(F32)
    repq = (kid == qidx).astype(F32)
    repp = (kid == pidx).astype(F32)
    til = (lax.broadcasted_iota(jnp.int32, (S5_CH, wid), 0)
           == (lax.broadcasted_iota(jnp.int32, (S5_CH, wid), 1) & (S5_CH - 1))).astype(F32)
    bt_r, bt_i = _dot_hi(bbr, til), _dot_hi(bbi, til)
    e0r, e0i = _dot_t_hi(pr, rep0), _dot_t_hi(pi_, rep0)
    x_re = e0r * bt_r - e0i * bt_i
    x_im = e0r * bt_i + e0i * bt_r
    kt_ref[...] = _dot_hi(cr_ref[...], x_re) - _dot_hi(ci_ref[...], x_im)
    eqr, eqi = _dot_t_hi(pr, repq), _dot_t_hi(pi_, repq)
    qre_ref[...] = (eqr * bt_r - eqi * bt_i).astype(qre_ref.dtype)
    qim_ref[...] = (eqr * bt_i + eqi * bt_r).astype(qim_ref.dtype)
    epr, epi = _dot_t_hi(pr, repp), _dot_t_hi(pi_, repp)
    ct_r, ct_i = _dot_t_hi(cr_ref[...], til), _dot_t_hi(ci_ref[...], til)
    pre_ref[...] = (ct_r * epr - ct_i * epi).astype(pre_ref.dtype)
    pim_ref[...] = (-(ct_r * epi + ct_i * epr)).astype(pim_ref.dtype)
    at_ref[0:1, :] = pr[t_len:t_len + 1, :]
    at_ref[1:2, :] = pi_[t_len:t_len + 1, :]


def _s5_params(lam_re, lam_im, log_dt, b_re, b_im, c_re, c_im):
    g, p, n = S5_GROUPS, S5_STATE, S5_CH
    wid = S5_T * S5_CH

    def spec(*shape):
        return pl.BlockSpec((None, None) + shape, lambda d, gi: (d, gi) + (0,) * len(shape))

    def ospec(*shape):
        return pl.BlockSpec((None, None) + shape, lambda d, gi: (gi, d) + (0,) * len(shape))
    return pl.pallas_call(
        _s5_param_kernel, grid=(2, g),
        in_specs=[spec(1, p), spec(1, p), spec(p, 1), spec(p, 1), spec(1, 1),
                  spec(p, n), spec(p, n), spec(n, p), spec(n, p)],
        out_specs=[ospec(n, wid), ospec(p, wid), ospec(p, wid), ospec(p, wid), ospec(p, wid), ospec(2, p)],
        out_shape=[jax.ShapeDtypeStruct((g, 2, n, wid), F32),
                   jax.ShapeDtypeStruct((g, 2, p, wid), BF16),
                   jax.ShapeDtypeStruct((g, 2, p, wid), BF16),
                   jax.ShapeDtypeStruct((g, 2, p, wid), BF16),
                   jax.ShapeDtypeStruct((g, 2, p, wid), BF16),
                   jax.ShapeDtypeStruct((g, 2, 2, p), F32)],
        compiler_params=_params(("parallel", "parallel"), 32),
        name="s5_params",
    )(lam_re.reshape(2, g, 1, p), lam_im.reshape(2, g, 1, p), lam_re.reshape(2, g, p, 1),
      lam_im.reshape(2, g, p, 1), log_dt.reshape(2, g, 1, 1), b_re, b_im, c_re, c_im)


def _s5_toeplitz(kt):
    g, t_len, n = S5_GROUPS, S5_T, S5_CH
    k = kt.reshape(g, 2, n, t_len, n)
    k = jnp.concatenate([k, jnp.zeros_like(k[:, :, :, :1])], axis=3)
    s = jnp.arange(t_len)[:, None]
    t = jnp.arange(t_len)[None, :]
    idx_f = jnp.where(t >= s, t - s, t_len)
    idx_b = jnp.where(s >= t, s - t, t_len)
    m = k[:, 0][:, :, idx_f, :] + k[:, 1][:, :, idx_b, :]
    return m.transpose(0, 2, 4, 3, 1).reshape(g, t_len * n, t_len * n).astype(BF16)


def _s5_state_kernel(u_ref, qre_ref, qim_ref, s_ref):
    u = u_ref[...]
    dn = (((1,), (1,)), ((), ()))
    s_ref[:, 0:128] = lax.dot_general(u, qre_ref[...], dn, preferred_element_type=F32)
    s_ref[:, 128:256] = lax.dot_general(u, qim_ref[...], dn, preferred_element_type=F32)


def _s5_scan_kernel(s_ref, are_ref, aim_ref, h_ref, *, nc, ncc):
    ar, ai = are_ref[...], aim_ref[...]
    is_f = lax.broadcasted_iota(jnp.int32, ar.shape, 1) < S5_STATE

    def body(i, carry):
        hr, hi = carry
        cf = i
        cb = jnp.where(i < ncc, ncc - 1 - i, nc + ncc - 1 - i)
        h_ref[cf, :, 0:64] = hr[:, 0:64]
        h_ref[cb, :, 64:128] = hr[:, 64:128]
        h_ref[cf, :, 128:192] = hi[:, 0:64]
        h_ref[cb, :, 192:256] = hi[:, 64:128]
        sf, sb = s_ref[cf], s_ref[cb]
        sr = jnp.where(is_f, sf[:, 0:128], sb[:, 0:128])
        si = jnp.where(is_f, sf[:, 128:256], sb[:, 128:256])
        return ar * hr - ai * hi + sr, ar * hi + ai * hr + si

    zero = jnp.zeros(ar.shape, F32)
    lax.fori_loop(0, nc, body, (zero, zero))


def _s5_out_kernel(u_ref, m_ref, h_ref, p_ref, y_ref):
    y = jnp.dot(u_ref[...], m_ref[...], preferred_element_type=F32)
    y_ref[...] = y + jnp.dot(h_ref[...].astype(BF16), p_ref[...], preferred_element_type=F32)


def _s5_mixer(s_all, kt, qre, qim, pre, pim, at, n_ctx):
    n = s_all.shape[0]
    g, t_len, wid = S5_GROUPS, S5_T, S5_T * S5_CH
    nc, ncc = n // t_len, n_ctx // t_len
    assert n % t_len == 0 and n_ctx % t_len == 0
    u = s_all.reshape(nc, t_len, g, S5_CH).transpose(2, 0, 1, 3).reshape(g, nc, wid)
    m = _s5_toeplitz(kt)
    p_all = jnp.concatenate([pre.reshape(g, 128, wid), pim.reshape(g, 128, wid)], axis=1)
    a_re = at[:, :, 0, :].reshape(g, 128)
    a_im = at[:, :, 1, :].reshape(g, 128)
    s_t = pl.pallas_call(
        _s5_state_kernel, grid=(g,),
        in_specs=[pl.BlockSpec((None, nc, wid), lambda i: (i, 0, 0)),
                  pl.BlockSpec((None, 128, wid), lambda i: (i, 0, 0)),
                  pl.BlockSpec((None, 128, wid), lambda i: (i, 0, 0))],
        out_specs=pl.BlockSpec((nc, 256), lambda i: (0, i)),
        out_shape=jax.ShapeDtypeStruct((nc, g * 256), F32),
        compiler_params=_params(("parallel",), 32),
        name="s5_chunk_state",
    )(u, qre.reshape(g, 128, wid), qim.reshape(g, 128, wid))
    h_t = pl.pallas_call(
        functools.partial(_s5_scan_kernel, nc=nc, ncc=ncc),
        out_shape=jax.ShapeDtypeStruct((nc, g, 256), F32),
        compiler_params=pltpu.CompilerParams(vmem_limit_bytes=48 * MIB),
        name="s5_chunk_scan",
    )(s_t.reshape(nc, g, 256), a_re, a_im).reshape(nc, g * 256)
    y = pl.pallas_call(
        _s5_out_kernel, grid=(g,),
        in_specs=[pl.BlockSpec((None, nc, wid), lambda i: (i, 0, 0)),
                  pl.BlockSpec((None, wid, wid), lambda i: (i, 0, 0)),
                  pl.BlockSpec((nc, 256), lambda i: (0, i)),
                  pl.BlockSpec((None, 256, wid), lambda i: (i, 0, 0))],
        out_specs=pl.BlockSpec((None, nc, wid), lambda i: (i, 0, 0)),
        out_shape=jax.ShapeDtypeStruct((g, nc, wid), F32),
        compiler_params=_params(("parallel",), 32),
        name="s5_chunk_out",
    )(u, m, h_t, p_all)
    return y.reshape(g, nc, t_len, S5_CH).transpose(1, 2, 0, 3).reshape(n, W_S5)


def _gelu_tanh(x):
    return 0.5 * x * (1.0 + jnp.tanh(math.sqrt(2.0 / math.pi) * (x + 0.044715 * (x * x * x))))


def _even_out_kernel(pool_ref, y_ref, s_ref, dsk_ref, wglu_ref, wo_ref, h_ref, nw_ref, mod_ref, o_ref,
                     *, n_ctx, tm):
    y = _gelu_tanh(y_ref[...] + s_ref[...].astype(F32) * dsk_ref[...])
    gate = jnp.dot(y.astype(BF16), wglu_ref[...], preferred_element_type=F32)
    s5 = (y * _sigmoid(gate)).astype(BF16)
    mix = jnp.dot(pool_ref[...], wo_ref[0:W_POOL, :], preferred_element_type=F32)
    mix = mix + jnp.dot(s5, wo_ref[W_POOL:W_POOL + W_S5, :], preferred_element_type=F32)
    g = _gate_rows(mod_ref, pl.program_id(0) * tm, tm, n_ctx)
    o_ref[...] = h_ref[...] + g * _rms(mix, nw_ref[...])


def _even_out(pool_out, y_s5, z, dsk, w_glu, w_out, h, nw, mod, n_ctx):
    n, d = h.shape
    tm = _pick(n, (320, 256, 128))
    return pl.pallas_call(
        functools.partial(_even_out_kernel, n_ctx=n_ctx, tm=tm), grid=(n // tm,),
        in_specs=[pl.BlockSpec((tm, W_POOL), lambda i: (i, 0)),
                  pl.BlockSpec((tm, W_S5), lambda i: (i, 0)),
                  pl.BlockSpec((tm, W_S5), lambda i: (i, W_POOL // W_S5)),
                  pl.BlockSpec((1, W_S5), lambda i: (0, 0)),
                  pl.BlockSpec((W_S5, W_S5), lambda i: (0, 0)),
                  pl.BlockSpec((d, d), lambda i: (0, 0)),
                  pl.BlockSpec((tm, d), lambda i: (i, 0)),
                  pl.BlockSpec((1, d), lambda i: (0, 0)),
                  pl.BlockSpec((8, d), lambda i: (0, 0))],
        out_specs=pl.BlockSpec((tm, d), lambda i: (i, 0)),
        out_shape=jax.ShapeDtypeStruct((n, d), F32),
        compiler_params=_params(("parallel",), 48),
        name="even_mixer_out",
    )(pool_out, y_s5, z, dsk.reshape(1, W_S5), w_glu, w_out, h, nw.reshape(1, d), mod)


def _ffn_kernel(h_ref, nw2_ref, nw3_ref, mod_ref, wg_ref, wu_ref, wd_ref, o_ref, v_ref, *, n_ctx, tm):
    f = pl.program_id(1)

    @pl.when(f == 0)
    def _():
        v = _norm_mod(h_ref[...], nw2_ref[...], mod_ref, pl.program_id(0) * tm, n_ctx)
        v_ref[...] = v.astype(BF16)
        o_ref[...] = jnp.zeros_like(o_ref)
    v = v_ref[...]
    a = jnp.dot(v, wg_ref[...], preferred_element_type=F32)
    b = jnp.dot(v, wu_ref[...], preferred_element_type=F32)
    act = (a * _sigmoid(a) * b).astype(BF16)
    o_ref[...] += jnp.dot(act, wd_ref[...], preferred_element_type=F32)

    @pl.when(f == pl.num_programs(1) - 1)
    def _():
        g = _gate_rows(mod_ref, pl.program_id(0) * tm, tm, n_ctx)
        o_ref[...] = h_ref[...] + g * _rms(o_ref[...], nw3_ref[...])


def _dense_ffn(h, nw2, nw3, mod, wg, wu, wd, n_ctx):
    n, d = h.shape
    dff = wg.shape[1]
    tm = _pick(n, (640, 512, 256, 128))
    tf = 512
    return pl.pallas_call(
        functools.partial(_ffn_kernel, n_ctx=n_ctx, tm=tm), grid=(n // tm, dff // tf),
        in_specs=[pl.BlockSpec((tm, d), lambda i, f: (i, 0)),
                  pl.BlockSpec((1, d), lambda i, f: (0, 0)),
                  pl.BlockSpec((1, d), lambda i, f: (0, 0)),
                  pl.BlockSpec((8, d), lambda i, f: (0, 0)),
                  pl.BlockSpec((d, tf), lambda i, f: (0, f)),
                  pl.BlockSpec((d, tf), lambda i, f: (0, f)),
                  pl.BlockSpec((tf, d), lambda i, f: (f, 0))],
        out_specs=pl.BlockSpec((tm, d), lambda i, f: (i, 0)),
        out_shape=jax.ShapeDtypeStruct((n, d), F32),
        scratch_shapes=[pltpu.VMEM((tm, d), BF16)],
        compiler_params=_params(("parallel", "arbitrary"), 56),
        name="dense_swiglu",
    )(h, nw2.reshape(1, d), nw3.reshape(1, d), mod, wg, wu, wd)


def _gla_kernel(qf_ref, kf_ref, vf_ref, af_ref, qb_ref, kb_ref, vb_ref, ab_ref, wa_ref, ba_ref,
                of_ref, ob_ref, st_ref, *, tb):
    c = GLA_CHUNK
    nsub = tb // c

    @pl.when(pl.program_id(0) == 0)
    def _():
        st_ref[...] = jnp.zeros_like(st_ref)
    ri = lax.broadcasted_iota(jnp.int32, (c, c), 0)
    ci = lax.broadcasted_iota(jnp.int32, (c, c), 1)
    ones = jnp.ones((c, 128), F32)
    scale = GLA_DK ** -0.5
    nt = (((1,), (1,)), ((), ()))
    tn = (((0,), (0,)), ((), ()))
    dirs = ((qf_ref, kf_ref, vf_ref, af_ref, of_ref), (qb_ref, kb_ref, vb_ref, ab_ref, ob_ref))
    for d, (q_ref, k_ref, v_ref, a_ref, o_ref) in enumerate(dirs):
        mask = (ri >= ci) if d == 0 else (ci >= ri)
        cum = mask.astype(F32)
        z = _dot_hi(a_ref[...], wa_ref[d]) + ba_ref[d]
        glog = (jnp.minimum(z, 0.0) - jnp.log(1.0 + jnp.exp(-jnp.abs(z)))) * (1.0 / GLA_TAU)
        order = range(nsub) if d == 0 else range(nsub - 1, -1, -1)
        for sc in order:
            r0 = sc * c
            gc = glog[r0:r0 + c, :]
            b = _dot_hi(cum, gc)
            tot = lax.dot_general(gc, ones, tn, precision=HI, preferred_element_type=F32)
            b_last = b[c - 1:c, :] if d == 0 else b[0:1, :]
            q = q_ref[r0:r0 + c, :].astype(F32) * scale
            k = k_ref[r0:r0 + c, :].astype(F32)
            qe = (q * jnp.exp(b)).astype(BF16)
            ke = (k * jnp.exp(-b)).astype(BF16)
            kd = (k * jnp.exp(b_last - b)).astype(BF16)
            for h in range(GLA_HEADS):
                ks = slice(h * GLA_DK, (h + 1) * GLA_DK)
                vs = slice(h * GLA_DV, (h + 1) * GLA_DV)
                att = lax.dot_general(qe[:, ks], ke[:, ks], nt, preferred_element_type=F32)
                att = jnp.where(mask, att, 0.0).astype(BF16)
                vh = v_ref[r0:r0 + c, vs]
                s_old = st_ref[d, h]
                o = jnp.dot(att, vh, preferred_element_type=F32)
                o = o + jnp.dot(qe[:, ks], s_old.astype(BF16), preferred_element_type=F32)
                o_ref[r0:r0 + c, vs] = o
                dec = jnp.exp(tot[ks, :])
                upd = lax.dot_general(kd[:, ks], vh, tn, preferred_element_type=F32)
                st_ref[d, h] = jnp.concatenate([s_old[:, 0:128] * dec, s_old[:, 128:256] * dec], axis=1) + upd


def _gla_mixer(zmain, a_aux, w_a, b_a, n_ctx):
    n = zmain.shape[0]
    tb = 256
    assert n % tb == 0 and n_ctx % tb == 0
    nb, ncb = n // tb, n_ctx // tb

    def bwd(s):
        return jnp.where(s < ncb, ncb - 1 - s, nb + ncb - 1 - s)
    wa = jnp.pad(w_a, ((0, 0), (0, 128 - GLA_RANK), (0, 0)))
    in_specs = []
    for order in (lambda s: s, bwd):
        in_specs += [pl.BlockSpec((tb, W_GLA_K), lambda s, o=order: (o(s), OQ // W_GLA_K)),
                     pl.BlockSpec((tb, W_GLA_K), lambda s, o=order: (o(s), OK_ // W_GLA_K)),
                     pl.BlockSpec((tb, W_GLA_V), lambda s, o=order: (o(s), OV // W_GLA_V)),
                     pl.BlockSpec((tb, 128), lambda s, o=order: (o(s), 0))]
    in_specs += [pl.BlockSpec((2, 128, W_GLA_K), lambda s: (0, 0, 0)),
                 pl.BlockSpec((2, 1, W_GLA_K), lambda s: (0, 0, 0))]
    return pl.pallas_call(
        functools.partial(_gla_kernel, tb=tb), grid=(nb,),
        in_specs=in_specs,
        out_specs=[pl.BlockSpec((tb, W_GLA_V), lambda s: (s, 0)),
                   pl.BlockSpec((tb, W_GLA_V), lambda s: (bwd(s), 0))],
        out_shape=[jax.ShapeDtypeStruct((n, W_GLA_V), F32)] * 2,
        scratch_shapes=[pltpu.VMEM((2, GLA_HEADS, GLA_DK, GLA_DV), F32)],
        compiler_params=_params(("arbitrary",), 32),
        name="gla_chunked",
    )(zmain, zmain, zmain, a_aux, zmain, zmain, zmain, a_aux, wa, b_a.reshape(2, 1, W_GLA_K))


_ATT_PAD = 2 * ATT_HEAD_DIM
_LOG2E = 1.4426950408889634
_FIXED_SHIFT_MAX_BOUND = 40.0


def _rope_fn(row0, tm, n_ctx):
    hd = ATT_HEAD_DIM
    row = row0 + lax.broadcasted_iota(jnp.int32, (tm, hd), 0)
    lane = lax.broadcasted_iota(jnp.int32, (tm, hd), 1)
    t = row - n_ctx
    pos = jnp.where(lane < hd // 2, t // GRID_W, t % GRID_W).astype(F32)
    quarter = hd // 4
    freq = jnp.exp((lane % quarter).astype(F32) * (-math.log(ROPE_THETA) / quarter))
    ang = pos * freq
    first = (lane & quarter) == 0
    cosv = jnp.cos(ang)
    sin_s = jnp.where(first, -jnp.sin(ang), jnp.sin(ang))

    def rope(x):
        sw = jnp.where(first, pltpu.roll(x, hd - quarter, 1), pltpu.roll(x, quarter, 1))
        return x * cosv + sw * sin_s
    return rope, row >= n_ctx


def _k_prep_kernel(k_ref, v_ref, kw_ref, kx_ref, vx_ref, kmax_ref, *, n_ctx, tm):
    hd = ATT_HEAD_DIM
    rope, is_lat = _rope_fn(pl.program_id(0) * tm, tm, n_ctx)
    lane = lax.broadcasted_iota(jnp.int32, (tm, hd), 1)
    one_col = jnp.where(lane == 0, 1.0, 0.0).astype(BF16)
    ones = jnp.ones((tm, hd), BF16)
    nmax = jnp.zeros((tm, 1), F32)
    for h in range(ATT_KV_HEADS):
        sl = slice(h * hd, (h + 1) * hd)
        xn = _rms(k_ref[:, sl].astype(F32), kw_ref[...])
        kr = jnp.where(is_lat, rope(xn), xn).astype(BF16)
        kx_ref[:, h * _ATT_PAD:h * _ATT_PAD + hd] = kr
        kx_ref[:, h * _ATT_PAD + hd:(h + 1) * _ATT_PAD] = one_col
        vx_ref[:, h * _ATT_PAD:h * _ATT_PAD + hd] = v_ref[:, sl]
        vx_ref[:, h * _ATT_PAD + hd:(h + 1) * _ATT_PAD] = ones
        krf = kr.astype(F32)
        nmax = jnp.maximum(nmax, jnp.sum(krf * krf, axis=-1, keepdims=True))
    kmax_ref[...] = jnp.broadcast_to(jnp.max(nmax, axis=0, keepdims=True), kmax_ref.shape)


def _q_prep_kernel(q_ref, qw_ref, kn_ref, qx_ref, bmax_ref, *, n_ctx, tm):
    hd = ATT_HEAD_DIM
    rope, _ = _rope_fn(n_ctx + pl.program_id(0) * tm, tm, n_ctx)
    lane = lax.broadcasted_iota(jnp.int32, (tm, hd), 1)
    scale = hd ** -0.5 * _LOG2E
    bmax = jnp.zeros((tm, 1), F32)
    for h in range(ATT_HEADS):
        sl = slice(h * hd, (h + 1) * hd)
        xn = _rms(q_ref[:, sl].astype(F32), qw_ref[...])
        qr = (rope(xn) * scale).astype(BF16)
        qf = qr.astype(F32)
        bound = jnp.sqrt(jnp.sum(qf * qf, axis=-1, keepdims=True)) * kn_ref[...]
        qx_ref[:, h * _ATT_PAD:h * _ATT_PAD + hd] = qr
        qx_ref[:, h * _ATT_PAD + hd:(h + 1) * _ATT_PAD] = jnp.where(lane == 0, -bound, 0.0).astype(BF16)
        bmax = jnp.maximum(bmax, bound)
    bmax_ref[...] = jnp.broadcast_to(jnp.max(bmax, axis=0, keepdims=True), bmax_ref.shape)


def _qk_prep(zmain, q_norm, k_norm, n_ctx):
    n = zmain.shape[0]
    lq = n - n_ctx
    tm = 256
    hd = ATT_HEAD_DIM
    assert n_ctx % tm == 0
    ncb = n_ctx // tm
    kx, vx, kmax = pl.pallas_call(
        functools.partial(_k_prep_kernel, n_ctx=n_ctx, tm=tm), grid=(n // tm,),
        in_specs=[pl.BlockSpec((tm, W_ATT_KV), lambda i: (i, OAK // W_ATT_KV)),
                  pl.BlockSpec((tm, W_ATT_KV), lambda i: (i, OAV // W_ATT_KV)),
                  pl.BlockSpec((1, hd), lambda i: (0, 0))],
        out_specs=[pl.BlockSpec((tm, ATT_KV_HEADS * _ATT_PAD), lambda i: (i, 0)),
                   pl.BlockSpec((tm, ATT_KV_HEADS * _ATT_PAD), lambda i: (i, 0)),
                   pl.BlockSpec((None, 8, 128), lambda i: (i, 0, 0))],
        out_shape=[jax.ShapeDtypeStruct((n, ATT_KV_HEADS * _ATT_PAD), BF16),
                   jax.ShapeDtypeStruct((n, ATT_KV_HEADS * _ATT_PAD), BF16),
                   jax.ShapeDtypeStruct((n // tm, 8, 128), F32)],
        compiler_params=_params(("parallel",), 32),
        name="k_norm_rope",
    )(zmain, zmain, k_norm.reshape(1, hd))
    knorm = jnp.sqrt(jnp.max(kmax)).reshape(1, 1)
    qx, bmax = pl.pallas_call(
        functools.partial(_q_prep_kernel, n_ctx=n_ctx, tm=tm), grid=(lq // tm,),
        in_specs=[pl.BlockSpec((tm, W_ATT), lambda i: (i + ncb, OAQ // W_ATT)),
                  pl.BlockSpec((1, hd), lambda i: (0, 0)),
                  pl.BlockSpec((1, 1), lambda i: (0, 0))],
        out_specs=[pl.BlockSpec((tm, ATT_HEADS * _ATT_PAD), lambda i: (i, 0)),
                   pl.BlockSpec((None, 8, 128), lambda i: (i, 0, 0))],
        out_shape=[jax.ShapeDtypeStruct((lq, ATT_HEADS * _ATT_PAD), BF16),
                   jax.ShapeDtypeStruct((lq // tm, 8, 128), F32)],
        compiler_params=_params(("parallel",), 32),
        name="q_norm_rope",
    )(zmain, q_norm.reshape(1, hd), knorm)
    return qx, kx, vx, jnp.max(bmax)


def _flash_fixed_kernel(q_ref, kt_ref, v_ref, o_ref, acc_ref):
    c = pl.program_id(2)
    hd = ATT_HEAD_DIM

    @pl.when(c == 0)
    def _():
        acc_ref[...] = jnp.zeros_like(acc_ref)
    kt = kt_ref[...]
    v = v_ref[...]
    for g in range(ATT_GROUP):
        s = jnp.dot(q_ref[:, g * _ATT_PAD:(g + 1) * _ATT_PAD], kt, preferred_element_type=F32)
        acc_ref[g] += jnp.dot(jnp.exp2(s).astype(BF16), v, preferred_element_type=F32)

    @pl.when(c == pl.num_programs(2) - 1)
    def _():
        for g in range(ATT_GROUP):
            a = acc_ref[g]
            o_ref[:, g * hd:(g + 1) * hd] = (a[:, 0:hd] / a[:, hd:2 * hd]).astype(o_ref.dtype)


def _flash_online_kernel(q_ref, kt_ref, v_ref, o_ref, m_ref, acc_ref):
    c = pl.program_id(2)
    hd = ATT_HEAD_DIM

    @pl.when(c == 0)
    def _():
        m_ref[...] = jnp.full(m_ref.shape, -jnp.inf, F32)
        acc_ref[...] = jnp.zeros_like(acc_ref)
    kt = kt_ref[...]
    v = v_ref[...]
    for g in range(ATT_GROUP):
        s = jnp.dot(q_ref[:, g * _ATT_PAD:(g + 1) * _ATT_PAD], kt, preferred_element_type=F32)
        m_prev = m_ref[g]
        m_new = jnp.maximum(m_prev, jnp.max(s, axis=-1, keepdims=True))
        p = jnp.exp2(s - m_new).astype(BF16)
        acc_ref[g] = jnp.exp2(m_prev - m_new) * acc_ref[g] + jnp.dot(p, v, preferred_element_type=F32)
        m_ref[g] = m_new

    @pl.when(c == pl.num_programs(2) - 1)
    def _():
        for g in range(ATT_GROUP):
            a = acc_ref[g]
            o_ref[:, g * hd:(g + 1) * hd] = (a[:, 0:hd] / a[:, hd:2 * hd]).astype(o_ref.dtype)


def _attention(qx, kx, vx, bound_log2):
    lq = qx.shape[0]
    n = kx.shape[0]
    kt = kx.T
    gw = ATT_GROUP * _ATT_PAD
    ow = ATT_GROUP * ATT_HEAD_DIM
    tq = _pick(lq, (512, 256, 128))

    def call(kern, tk, scratch, name):
        return pl.pallas_call(
            kern, grid=(ATT_KV_HEADS, lq // tq, n // tk),
            in_specs=[pl.BlockSpec((tq, gw), lambda h, i, j: (i, h)),
                      pl.BlockSpec((_ATT_PAD, tk), lambda h, i, j: (h, j)),
                      pl.BlockSpec((tk, _ATT_PAD), lambda h, i, j: (j, h))],
            out_specs=pl.BlockSpec((tq, ow), lambda h, i, j: (i, h)),
            out_shape=jax.ShapeDtypeStruct((lq, W_ATT), BF16),
            scratch_shapes=scratch + [pltpu.VMEM((ATT_GROUP, tq, _ATT_PAD), F32)],
            compiler_params=_params(("parallel", "parallel", "arbitrary"), 48),
            name=name,
        )(qx, kt, vx)

    def fixed(_):
        return call(_flash_fixed_kernel, _pick(n, (3328, 1280, 512, 256)), [], "gqa_flash_fixed_shift")

    def online(_):
        return call(_flash_online_kernel, _pick(n, (640, 512, 256, 128)),
                    [pltpu.VMEM((ATT_GROUP, tq, 1), F32)], "gqa_flash_online")
    return lax.cond(bound_log2 <= _FIXED_SHIFT_MAX_BOUND * _LOG2E, fixed, online, None)


def _odd_out_kernel(of_ref, ob_ref, r_ref, att_ref, gn_ref, wo_ref, h_ref, nw_ref, mod_ref, o_ref):
    o = of_ref[...] + ob_ref[...]
    r = r_ref[...].astype(F32)
    silu_r = r * _sigmoid(r)
    parts = []
    for h in range(GLA_HEADS):
        vs = slice(h * GLA_DV, (h + 1) * GLA_DV)
        parts.append((_rms(o[:, vs], gn_ref[:, vs]) * silu_r[:, vs]).astype(BF16))
    gla = jnp.concatenate(parts, axis=1)
    mix = jnp.dot(gla, wo_ref[0:W_GLA_V, :], preferred_element_type=F32)
    mix = mix + jnp.dot(att_ref[...], wo_ref[W_GLA_V:W_GLA_V + W_ATT, :], preferred_element_type=F32)
    o_ref[...] = h_ref[...] + mod_ref[4:5, :] * _rms(mix, nw_ref[...])


def _odd_out(o_f, o_b, zmain, att, gla_norm, w_out, h_all, nw, mod, n_ctx):
    n, d = h_all.shape
    lq = n - n_ctx
    tm = _pick(lq, (256, 128))
    assert n_ctx % tm == 0
    off = n_ctx // tm
    return pl.pallas_call(
        _odd_out_kernel, grid=(lq // tm,),
        in_specs=[pl.BlockSpec((tm, W_GLA_V), lambda i: (i + off, 0)),
                  pl.BlockSpec((tm, W_GLA_V), lambda i: (i + off, 0)),
                  pl.BlockSpec((tm, W_GLA_V), lambda i: (i + off, OR // W_GLA_V)),
                  pl.BlockSpec((tm, W_ATT), lambda i: (i, 0)),
                  pl.BlockSpec((1, W_GLA_V), lambda i: (0, 0)),
                  pl.BlockSpec((d, d), lambda i: (0, 0)),
                  pl.BlockSpec((tm, d), lambda i: (i + off, 0)),
                  pl.BlockSpec((1, d), lambda i: (0, 0)),
                  pl.BlockSpec((8, d), lambda i: (0, 0))],
        out_specs=pl.BlockSpec((tm, d), lambda i: (i, 0)),
        out_shape=jax.ShapeDtypeStruct((lq, d), F32),
        compiler_params=_params(("parallel",), 48),
        name="odd_mixer_out",
    )(o_f, o_b, zmain, att, gla_norm.reshape(1, W_GLA_V), w_out, h_all, nw.reshape(1, d), mod)


def _router_kernel(h_ref, nw_ref, mod_ref, wr_ref, br_ref, v_ref, r_ref):
    x = h_ref[...]
    v = _rms(x, nw_ref[...]) * (1.0 + mod_ref[0:1, :]) + mod_ref[1:2, :]
    v_ref[...] = v.astype(BF16)
    logits = _dot_hi(v, wr_ref[...]) + br_ref[...]
    lane = lax.broadcasted_iota(jnp.int32, logits.shape, 1)
    m1 = jnp.max(logits, axis=-1, keepdims=True)
    i1 = jnp.min(jnp.where(logits == m1, lane, 128), axis=-1, keepdims=True)
    rest = jnp.where(lane == i1, -jnp.inf, logits)
    m2 = jnp.max(rest, axis=-1, keepdims=True)
    i2 = jnp.min(jnp.where(rest == m2, lane, 128), axis=-1, keepdims=True)
    e2 = jnp.exp(m2 - m1)
    g1 = 1.0 / (1.0 + e2)
    g2 = e2 / (1.0 + e2)
    out = jnp.where(lane == 0, i1.astype(F32), 0.0)
    out = jnp.where(lane == 1, i2.astype(F32), out)
    out = jnp.where(lane == 2, g1, out)
    out = jnp.where(lane == 3, g2, out)
    r_ref[...] = out


def _router(h_lat, nw, mod, w_router, b_router):
    lq, d = h_lat.shape
    tm = _pick(lq, (512, 256, 128))
    wr = jnp.pad(w_router, ((0, 0), (0, 128 - N_EXPERTS)))
    br = jnp.pad(b_router, (0, 128 - N_EXPERTS), constant_values=-1e30).reshape(1, 128)
    return pl.pallas_call(
        _router_kernel, grid=(lq // tm,),
        in_specs=[pl.BlockSpec((tm, d), lambda i: (i, 0)),
                  pl.BlockSpec((1, d), lambda i: (0, 0)),
                  pl.BlockSpec((8, d), lambda i: (0, 0)),
                  pl.BlockSpec((d, 128), lambda i: (0, 0)),
                  pl.BlockSpec((1, 128), lambda i: (0, 0))],
        out_specs=[pl.BlockSpec((tm, d), lambda i: (i, 0)), pl.BlockSpec((tm, 128), lambda i: (i, 0))],
        out_shape=[jax.ShapeDtypeStruct((lq, d), BF16), jax.ShapeDtypeStruct((lq, 128), F32)],
        compiler_params=_params(("parallel",), 32),
        name="router_top2",
    )(h_lat, nw.reshape(1, d), mod, wr, br)


def _dispatch_kernel(j_ref, b_ref, fl_ref, tok_ref, v_ref, o_ref, *, sblk):
    s = pl.program_id(0)
    fl = fl_ref[s]

    @pl.when((fl & 2) != 0)
    def _():
        o_ref[...] = jnp.zeros_like(o_ref)

    @pl.when((fl & 1) != 0)
    def _():
        cols = b_ref[s] * sblk + lax.broadcasted_iota(jnp.int32, (1, sblk), 1)
        onehot = (tok_ref[...] == cols).astype(BF16)
        o_ref[...] += jnp.dot(onehot, v_ref[...], preferred_element_type=F32).astype(o_ref.dtype)


def _expert_kernel(te_ref, ts_ref, tv_ref, x_ref, gt_ref, wg_ref, wu_ref, wd_ref, o_ref, acc_ref):
    j = pl.program_id(0)
    f = pl.program_id(1)
    valid = tv_ref[j] == 1

    @pl.when(f == 0)
    def _():
        acc_ref[...] = jnp.zeros_like(acc_ref)

    @pl.when(valid)
    def _():
        x = x_ref[...]
        a = jnp.dot(x, wg_ref[...], preferred_element_type=F32)
        b = jnp.dot(x, wu_ref[...], preferred_element_type=F32)
        act = (a * _sigmoid(a) * b).astype(BF16)
        acc_ref[...] += jnp.dot(act, wd_ref[...], preferred_element_type=F32)

    @pl.when(f == pl.num_programs(1) - 1)
    def _():
        o_ref[...] = (acc_ref[...] * gt_ref[...]).astype(o_ref.dtype)


def _combine_kernel(ci_ref, cy_ref, fl_ref, tok_ref, y_ref, h_ref, nw_ref, mod_ref, o_ref, acc_ref, *, sblk):
    s = pl.program_id(0)
    fl = fl_ref[s]

    @pl.when((fl & 2) != 0)
    def _():
        acc_ref[...] = jnp.zeros_like(acc_ref)

    @pl.when((fl & 1) != 0)
    def _():
        rows = ci_ref[s] * sblk + lax.broadcasted_iota(jnp.int32, (sblk, 1), 0)
        onehot = (tok_ref[...] == rows).astype(BF16)
        acc_ref[...] += jnp.dot(onehot, y_ref[...], preferred_element_type=F32)

    @pl.when((fl & 4) != 0)
    def _():
        o_ref[...] = h_ref[...] + mod_ref[4:5, :] * _rms(acc_ref[...], nw_ref[...])


def _count_le(sorted_vals, queries):
    return jnp.sum((sorted_vals[None, :] <= queries[:, None]).astype(jnp.int32), axis=1)


def _moe(h_lat, nw2, nw3, mod, w_router, b_router, wg, wu, wd):
    lq, d = h_lat.shape
    e = N_EXPERTS
    dff = wg.shape[2]
    tm = _pick(lq, (512, 256))
    tf = 512
    v, route = _router(h_lat, nw2, mod, w_router, b_router)
    idx = route[:, 0:2].astype(jnp.int32)
    gates = route[:, 2:4]

    npairs = 2 * lq
    nt = npairs // tm + e
    nblk = lq // tm
    flat_e = idx.reshape(-1)
    tok = jnp.arange(npairs, dtype=jnp.int32) // 2
    onehot = (flat_e[:, None] == jnp.arange(e, dtype=jnp.int32)[None, :]).astype(jnp.int32)
    csum = jnp.cumsum(onehot, axis=0)
    rank = jnp.sum(csum * onehot, axis=1) - 1
    counts = csum[-1]
    padded = ((counts + tm - 1) // tm) * tm
    ends = jnp.cumsum(padded)
    starts = ends - padded
    pos = starts[flat_e] + rank
    sorted_tok = jnp.full((nt * tm,), -1, jnp.int32).at[pos].set(tok)
    sorted_gate = jnp.zeros((nt * tm,), F32).at[pos].set(gates.reshape(-1))
    n_used = ends[-1] // tm
    tile_ids = jnp.arange(nt, dtype=jnp.int32)
    tile_valid = (tile_ids < n_used).astype(jnp.int32)
    tile_src = jnp.minimum(tile_ids, n_used - 1).astype(jnp.int32)
    tile_exp = jnp.minimum(_count_le(ends, tile_src * tm), e - 1)

    st2 = sorted_tok.reshape(nt, tm)
    blo = st2[:, 0] // tm
    bhi = jnp.max(st2, axis=1) // tm
    nb = jnp.where(tile_valid == 1, bhi - blo + 1, 0)
    cs = jnp.cumsum(nb)
    w_d = nt + e * (nblk - 1)
    sidx = jnp.arange(w_d, dtype=jnp.int32)
    sc = jnp.minimum(sidx, cs[-1] - 1)
    dj = _count_le(cs, sc)
    dstart = cs[dj] - nb[dj]
    db = (blo[dj] + sc - dstart).astype(jnp.int32)
    dact = sidx < cs[-1]
    dfl = (dact.astype(jnp.int32) + 2 * (dact & (sc == dstart)).astype(jnp.int32)).astype(jnp.int32)

    xs = pl.pallas_call(
        functools.partial(_dispatch_kernel, sblk=tm),
        grid_spec=pltpu.PrefetchScalarGridSpec(
            num_scalar_prefetch=3, grid=(w_d,),
            in_specs=[pl.BlockSpec((tm, 1), lambda s, j, b, fl: (j[s], 0)),
                      pl.BlockSpec((tm, d), lambda s, j, b, fl: (b[s], 0))],
            out_specs=pl.BlockSpec((tm, d), lambda s, j, b, fl: (j[s], 0))),
        out_shape=jax.ShapeDtypeStruct((nt * tm, d), BF16),
        compiler_params=_params(("arbitrary",), 32),
        name="moe_dispatch",
    )(dj, db, dfl, sorted_tok.reshape(nt * tm, 1), v)

    nf = dff // tf
    ys = pl.pallas_call(
        _expert_kernel,
        grid_spec=pltpu.PrefetchScalarGridSpec(
            num_scalar_prefetch=3, grid=(nt, nf),
            in_specs=[pl.BlockSpec((tm, d), lambda j, f, te, ts, tv: (ts[j], 0)),
                      pl.BlockSpec((tm, 1), lambda j, f, te, ts, tv: (ts[j], 0)),
                      pl.BlockSpec((None, d, tf), lambda j, f, te, ts, tv: (te[j], 0, jnp.where(tv[j] == 1, f, nf - 1))),
                      pl.BlockSpec((None, d, tf), lambda j, f, te, ts, tv: (te[j], 0, jnp.where(tv[j] == 1, f, nf - 1))),
                      pl.BlockSpec((None, tf, d), lambda j, f, te, ts, tv: (te[j], jnp.where(tv[j] == 1, f, nf - 1), 0))],
            out_specs=pl.BlockSpec((tm, d), lambda j, f, te, ts, tv: (j, 0)),
            scratch_shapes=[pltpu.VMEM((tm, d), F32)]),
        out_shape=jax.ShapeDtypeStruct((nt * tm, d), BF16),
        compiler_params=_params(("arbitrary", "arbitrary"), 48),
        name="moe_experts",
    )(tile_exp, tile_src, tile_valid, xs, sorted_gate.reshape(nt * tm, 1), wg, wu, wd)

    key = (tok // tm) * e + flat_e
    yj = (pos // tm).astype(jnp.int32)
    jlo = jnp.full((nblk * e,), nt, jnp.int32).at[key].min(yj)
    jhi = jnp.full((nblk * e,), -1, jnp.int32).at[key].max(yj)
    nbc = jnp.where(jhi >= 0, jhi - jlo + 1, 0)
    cs3 = jnp.cumsum(nbc)
    w_c = nt + e * (nblk - 1)
    sidx = jnp.arange(w_c, dtype=jnp.int32)
    sc = jnp.minimum(sidx, cs3[-1] - 1)
    kidx = _count_le(cs3, sc)
    ci = (kidx // e).astype(jnp.int32)
    cy = (jlo[kidx] + sc - (cs3[kidx] - nbc[kidx])).astype(jnp.int32)
    cact = sidx < cs3[-1]
    blk_first = cs3[ci * e] - nbc[ci * e]
    blk_last = cs3[ci * e + e - 1] - 1
    cfl = (cact.astype(jnp.int32) + 2 * (cact & (sc == blk_first)).astype(jnp.int32)
           + 4 * (cact & (sc == blk_last)).astype(jnp.int32)).astype(jnp.int32)

    return pl.pallas_call(
        functools.partial(_combine_kernel, sblk=tm),
        grid_spec=pltpu.PrefetchScalarGridSpec(
            num_scalar_prefetch=3, grid=(w_c,),
            in_specs=[pl.BlockSpec((None, 1, tm), lambda s, ci_, cy_, fl: (cy_[s], 0, 0)),
                      pl.BlockSpec((tm, d), lambda s, ci_, cy_, fl: (cy_[s], 0)),
                      pl.BlockSpec((tm, d), lambda s, ci_, cy_, fl: (ci_[s], 0)),
                      pl.BlockSpec((1, d), lambda s, ci_, cy_, fl: (0, 0)),
                      pl.BlockSpec((8, d), lambda s, ci_, cy_, fl: (0, 0))],
            out_specs=pl.BlockSpec((tm, d), lambda s, ci_, cy_, fl: (ci_[s], 0)),
            scratch_shapes=[pltpu.VMEM((tm, d), F32)]),
        out_shape=jax.ShapeDtypeStruct((lq, d), F32),
        compiler_params=_params(("arbitrary",), 40),
        name="moe_combine",
    )(ci, cy, cfl, sorted_tok.reshape(nt, 1, tm), ys, h_lat, nw3.reshape(1, d), mod)


def kernel(x, c, ctx, c_ctx, w_mod, b_mod, norms, e_w_in, e_pool_w, e_pool_scale, e_s5_lam_re, e_s5_lam_im, e_s5_log_dt, e_s5_b_re, e_s5_b_im, e_s5_c_re, e_s5_c_im, e_s5_d, e_s5_w_glu, e_w_out, e_ffn_gate, e_ffn_up, e_ffn_down, o_w_in, o_gla_w_a, o_gla_b_a, o_gla_norm, o_q_norm, o_k_norm, o_w_out, o_router, o_router_b, o_moe_gate, o_moe_up, o_moe_down):
    assert x.shape[0] == 1 and w_mod.shape[0] == 2 and e_w_in.shape[0] == 1 and o_w_in.shape[0] == 1
    d = x.shape[2]
    n_ctx = ctx.shape[1]
    h = jnp.concatenate([ctx[0], x[0]], axis=0)

    vecs = jnp.zeros((8, d), F32).at[0].set(c[0]).at[1].set(c_ctx)
    mods = _modulation(vecs, w_mod, b_mod)

    mod1, mod2 = _mod_rows(mods[0], 0), _mod_rows(mods[0], 1)
    z = _normed_matmul(h, norms[0, 0], mod1, e_w_in[0].astype(BF16), n_ctx)
    pool_out = _pool_mixer(z, e_pool_w[0].astype(BF16), e_pool_scale[0], n_ctx)
    kt, qre, qim, pre, pim, at = _s5_params(e_s5_lam_re[0], e_s5_lam_im[0], e_s5_log_dt[0],
                                            e_s5_b_re[0], e_s5_b_im[0], e_s5_c_re[0], e_s5_c_im[0])
    y_s5 = _s5_mixer(z[:, W_POOL:], kt, qre, qim, pre, pim, at, n_ctx)
    h = _even_out(pool_out, y_s5, z, e_s5_d[0], e_s5_w_glu[0].astype(BF16), e_w_out[0].astype(BF16),
                  h, norms[0, 1], mod1, n_ctx)
    h = _dense_ffn(h, norms[0, 2], norms[0, 3], mod2, e_ffn_gate[0].astype(BF16),
                   e_ffn_up[0].astype(BF16), e_ffn_down[0].astype(BF16), n_ctx)

    mod1, mod2 = _mod_rows(mods[1], 0), _mod_rows(mods[1], 1)
    w_in = o_w_in[0]
    a0 = 2 * W_GLA_K + W_GLA_V
    w_main = jnp.concatenate([w_in[:, :a0], w_in[:, a0 + GLA_RANK:]], axis=1).astype(BF16)
    w_aux = jnp.pad(w_in[:, a0:a0 + GLA_RANK], ((0, 0), (0, 128 - GLA_RANK))).astype(BF16)
    zmain, a_aux = _normed_matmul(h, norms[1, 0], mod1, w_main, n_ctx, w_aux=w_aux)
    o_f, o_b = _gla_mixer(zmain, a_aux, o_gla_w_a[0], o_gla_b_a[0], n_ctx)
    qx, kx, vx, bound_log2 = _qk_prep(zmain, o_q_norm[0], o_k_norm[0], n_ctx)
    att = _attention(qx, kx, vx, bound_log2)
    h_lat = _odd_out(o_f, o_b, zmain, att, o_gla_norm[0], o_w_out[0].astype(BF16), h, norms[1, 1], mod1, n_ctx)
    out = _moe(h_lat, norms[1, 2], norms[1, 3], mod2, o_router[0], o_router_b[0],
               o_moe_gate[0].astype(BF16), o_moe_up[0].astype(BF16), o_moe_down[0].astype(BF16))
    return out[None]
```

```python
import functools
import math

import jax
import jax.numpy as jnp
from jax import lax
from jax.experimental import pallas as pl
from jax.experimental.pallas import tpu as pltpu

F32 = jnp.float32
BF16 = jnp.bfloat16
HI = lax.Precision.HIGHEST
EPS = 1e-6

D_MODEL = 2048
GRID_W = 64
N_MOD = 6

POOL_WINDOWS = (2, 4, 8, 16)
POOL_GROUP = 384
W_POOL = 1536
W_S5 = 512
S5_CH = 16
S5_STATE = 64
S5_GROUPS = 32
S5_T = 32

GLA_HEADS = 4
GLA_DK = 128
GLA_DV = 256
GLA_RANK = 16
GLA_TAU = 16.0
GLA_CHUNK = 64
W_GLA_K = 512
W_GLA_V = 1024
ATT_HEAD_DIM = 128
ATT_HEADS = 8
ATT_KV_HEADS = 2
ATT_GROUP = 4
W_ATT = 1024
W_ATT_KV = 256
ROPE_THETA = 10000.0

D_FF = 7168
N_EXPERTS = 8

OQ, OK_, OV, OR, OAQ, OAK, OAV = 0, 512, 1024, 2048, 3072, 4096, 4352
W_ODD_MAIN = 4608

MIB = 2 ** 20


def _params(sem, vmem_mib):
    return pltpu.CompilerParams(dimension_semantics=sem, vmem_limit_bytes=vmem_mib * MIB)


def _pick(n, cands):
    for c in cands:
        if n % c == 0:
            return c
    raise ValueError(f"no tile for {n} in {cands}")


def _sigmoid(x):
    return 1.0 / (1.0 + jnp.exp(-x))


def _rms(x, w):
    return x * lax.rsqrt(jnp.mean(x * x, axis=-1, keepdims=True) + EPS) * w


def _norm_mod(x, nw, mod_ref, row0, n_ctx):
    xn = _rms(x, nw)
    rows = row0 + lax.broadcasted_iota(jnp.int32, (x.shape[0], 1), 0)
    is_ctx = rows < n_ctx
    sc = jnp.where(is_ctx, mod_ref[2:3, :], mod_ref[0:1, :])
    sh = jnp.where(is_ctx, mod_ref[3:4, :], mod_ref[1:2, :])
    return xn * (1.0 + sc) + sh


def _gate_rows(mod_ref, row0, tm, n_ctx):
    rows = row0 + lax.broadcasted_iota(jnp.int32, (tm, 1), 0)
    return jnp.where(rows < n_ctx, mod_ref[5:6, :], mod_ref[4:5, :])


def _mod_kernel(v_ref, w_ref, b_ref, o_ref):
    v = v_ref[...]
    s = v * _sigmoid(v)
    o_ref[...] = jnp.dot(s, w_ref[...], precision=HI, preferred_element_type=F32) + b_ref[...]


def _modulation(vecs, w_mod, b_mod):
    depth, d, n6 = w_mod.shape
    tn = 1024
    return pl.pallas_call(
        _mod_kernel,
        grid=(depth, n6 // tn),
        in_specs=[pl.BlockSpec((8, d), lambda l, j: (0, 0)),
                  pl.BlockSpec((None, d, tn), lambda l, j: (l, 0, j)),
                  pl.BlockSpec((None, 1, tn), lambda l, j: (l, 0, j))],
        out_specs=pl.BlockSpec((None, 8, tn), lambda l, j: (l, 0, j)),
        out_shape=jax.ShapeDtypeStruct((depth, 8, n6), F32),
        compiler_params=_params(("parallel", "parallel"), 40),
        name="modulation",
    )(vecs, w_mod, b_mod.reshape(depth, 1, n6))


def _mod_rows(m, sub):
    d = m.shape[1] // N_MOD
    m6 = m.reshape(8, N_MOD, d)
    sh, sc, g = m6[:, 3 * sub + 0], m6[:, 3 * sub + 1], m6[:, 3 * sub + 2]
    z = jnp.zeros((d,), F32)
    return jnp.stack([sc[0], sh[0], sc[1], sh[1], g[0], g[1], z, z])


def _nmm_kernel(h_ref, nw_ref, mod_ref, w_ref, o_ref, u_ref, *, n_ctx, tm):
    @pl.when(pl.program_id(1) == 0)
    def _():
        u = _norm_mod(h_ref[...], nw_ref[...], mod_ref, pl.program_id(0) * tm, n_ctx)
        u_ref[...] = u.astype(BF16)
    o_ref[...] = jnp.dot(u_ref[...], w_ref[...], preferred_element_type=F32).astype(o_ref.dtype)


def _nmm_aux_kernel(h_ref, nw_ref, mod_ref, w_ref, wa_ref, o_ref, oa_ref, u_ref, *, n_ctx, tm):
    @pl.when(pl.program_id(1) == 0)
    def _():
        u = _norm_mod(h_ref[...], nw_ref[...], mod_ref, pl.program_id(0) * tm, n_ctx)
        ub = u.astype(BF16)
        u_ref[...] = ub
        oa_ref[...] = jnp.dot(ub, wa_ref[...], preferred_element_type=F32)
    o_ref[...] = jnp.dot(u_ref[...], w_ref[...], preferred_element_type=F32).astype(o_ref.dtype)


def _normed_matmul(h, nw, mod, w, n_ctx, w_aux=None):
    n, d = h.shape
    nout = w.shape[1]
    tm = _pick(n, (640, 512, 256, 128))
    tn = _pick(nout, (1536, 1024, 512, 256, 128))
    in_specs = [pl.BlockSpec((tm, d), lambda i, j: (i, 0)),
                pl.BlockSpec((1, d), lambda i, j: (0, 0)),
                pl.BlockSpec((8, d), lambda i, j: (0, 0)),
                pl.BlockSpec((d, tn), lambda i, j: (0, j))]
    out_specs = pl.BlockSpec((tm, tn), lambda i, j: (i, j))
    out_shape = jax.ShapeDtypeStruct((n, nout), BF16)
    args = [h, nw.reshape(1, d), mod, w]
    if w_aux is None:
        kern = functools.partial(_nmm_kernel, n_ctx=n_ctx, tm=tm)
    else:
        na = w_aux.shape[1]
        kern = functools.partial(_nmm_aux_kernel, n_ctx=n_ctx, tm=tm)
        in_specs.append(pl.BlockSpec((d, na), lambda i, j: (0, 0)))
        out_specs = [out_specs, pl.BlockSpec((tm, na), lambda i, j: (i, 0))]
        out_shape = [out_shape, jax.ShapeDtypeStruct((n, na), F32)]
        args.append(w_aux)
    return pl.pallas_call(
        kern, grid=(n // tm, nout // tn), in_specs=in_specs, out_specs=out_specs, out_shape=out_shape,
        scratch_shapes=[pltpu.VMEM((tm, d), BF16)],
        compiler_params=_params(("parallel", "arbitrary"), 48),
        name="normed_proj",
    )(*args)


_POOL_HALO = 16


def _pool_kernel(z_ref, zp_ref, zn_ref, pw_ref, ps_ref, o_ref, ext_ref, *, n_ctx, n_all, tm):
    i = pl.program_id(0)
    row0 = i * tm
    in_ctx = row0 < n_ctx
    seq_start = jnp.where(in_ctx, 0, n_ctx)
    seq_end = jnp.where(in_ctx, n_ctx, n_all)
    has_prev = row0 > seq_start
    has_next = row0 + tm < seq_end
    hl = _POOL_HALO
    ext_ref[0:hl, :] = jnp.where(has_prev, zp_ref[...].astype(F32), 0.0)
    ext_ref[hl:hl + tm, :] = z_ref[...].astype(F32)
    ext_ref[hl + tm:hl + tm + hl, :] = jnp.where(has_next, zn_ref[...].astype(F32), 0.0)
    t = row0 - seq_start + lax.broadcasted_iota(jnp.int32, (tm, 1), 0)
    seq_len = seq_end - seq_start
    for g, w in enumerate(POOL_WINDOWS):
        c0 = g * POOL_GROUP
        acc = None
        for k in range(-(w // 2), w - w // 2):
            v = ext_ref[hl + k:hl + k + tm, c0:c0 + POOL_GROUP]
            acc = v if acc is None else acc + v
        lo = jnp.maximum(t - w // 2, 0)
        hi = jnp.minimum(t + (w - w // 2), seq_len)
        cnt = (hi - lo).astype(F32)
        dlt = acc / cnt - ext_ref[hl:hl + tm, c0:c0 + POOL_GROUP]
        y = jnp.dot(dlt.astype(BF16), pw_ref[g], preferred_element_type=F32)
        o_ref[:, c0:c0 + POOL_GROUP] = (y * ps_ref[:, c0:c0 + POOL_GROUP]).astype(o_ref.dtype)


def _pool_mixer(z, pool_w, pool_scale, n_ctx):
    n = z.shape[0]
    tm = 256
    assert n_ctx % tm == 0 and n % tm == 0
    hl = _POOL_HALO
    nh = n // hl
    per = tm // hl
    kern = functools.partial(_pool_kernel, n_ctx=n_ctx, n_all=n, tm=tm)
    return pl.pallas_call(
        kern, grid=(n // tm,),
        in_specs=[pl.BlockSpec((tm, W_POOL), lambda i: (i, 0)),
                  pl.BlockSpec((hl, W_POOL), lambda i: (jnp.maximum(i * per - 1, 0), 0)),
                  pl.BlockSpec((hl, W_POOL), lambda i: (jnp.minimum((i + 1) * per, nh - 1), 0)),
                  pl.BlockSpec((4, POOL_GROUP, POOL_GROUP), lambda i: (0, 0, 0)),
                  pl.BlockSpec((1, W_POOL), lambda i: (0, 0))],
        out_specs=pl.BlockSpec((tm, W_POOL), lambda i: (i, 0)),
        out_shape=jax.ShapeDtypeStruct((n, W_POOL), BF16),
        scratch_shapes=[pltpu.VMEM((tm + 2 * hl, W_POOL), F32)],
        compiler_params=_params(("parallel",), 32),
        name="pool_mixer",
    )(z, z, z, pool_w, pool_scale.reshape(1, W_POOL))


def _dot_hi(a, b):
    return jnp.dot(a, b, precision=HI, preferred_element_type=F32)


def _dot_t_hi(a, b):
    return lax.dot_general(a, b, (((0,), (0,)), ((), ())), precision=HI, preferred_element_type=F32)


def _s5_param_kernel(lr_ref, li_ref, lrc_ref, lic_ref, ldt_ref, br_ref, bi_ref, cr_ref, ci_ref,
                     kt_ref, qre_ref, qim_ref, pre_ref, pim_ref, at_ref):
    d = pl.program_id(0)
    t_len = S5_T
    tp = t_len + 8
    wid = t_len * S5_CH
    dt = jnp.exp(ldt_ref[...])
    lr, li = lr_ref[...], li_ref[...]
    kk = lax.broadcasted_iota(jnp.int32, (tp, S5_STATE), 0).astype(F32)
    mag = jnp.exp(kk * (lr * dt))
    ang = kk * (li * dt)
    pr, pi_ = mag * jnp.cos(ang), mag * jnp.sin(ang)
    lrc, lic = lrc_ref[...], lic_ref[...]
    magc = jnp.exp(lrc * dt)
    arc, aic = magc * jnp.cos(lic * dt), magc * jnp.sin(lic * dt)
    den = lrc * lrc + lic * lic
    nr, ni = arc - 1.0, aic
    kre = (nr * lrc + ni * lic) / den
    kim = (ni * lrc - nr * lic) / den
    bbr = kre * br_ref[...] - kim * bi_ref[...]
    bbi = kre * bi_ref[...] + kim * br_ref[...]
    col = lax.broadcasted_iota(jnp.int32, (tp, wid), 1)
    kid = lax.broadcasted_iota(jnp.int32, (tp, wid), 0)
    cq = col >> 4
    qidx = jnp.where(d == 0, t_len - 1 - cq, cq)
    pidx = jnp.where(d == 0, cq + 1, t_len - cq)
    rep0 = (kid == cq).astype(F32)
    repq = (kid == qidx).astype(F32)
    repp = (kid == pidx).astype(F32)
    til = (lax.broadcasted_iota(jnp.int32, (S5_CH, wid), 0)
           == (lax.broadcasted_iota(jnp.int32, (S5_CH, wid), 1) & (S5_CH - 1))).astype(F32)
    bt_r, bt_i = _dot_hi(bbr, til), _dot_hi(bbi, til)
    e0r, e0i = _dot_t_hi(pr, rep0), _dot_t_hi(pi_, rep0)
    x_re = e0r * bt_r - e0i * bt_i
    x_im = e0r * bt_i + e0i * bt_r
    kt_ref[...] = _dot_hi(cr_ref[...], x_re) - _dot_hi(ci_ref[...], x_im)
    eqr, eqi = _dot_t_hi(pr, repq), _dot_t_hi(pi_, repq)
    qre_ref[...] = (eqr * bt_r - eqi * bt_i).astype(qre_ref.dtype)
    qim_ref[...] = (eqr * bt_i + eqi * bt_r).astype(qim_ref.dtype)
    epr, epi = _dot_t_hi(pr, repp), _dot_t_hi(pi_, repp)
    ct_r, ct_i = _dot_t_hi(cr_ref[...], til), _dot_t_hi(ci_ref[...], til)
    pre_ref[...] = (ct_r * epr - ct_i * epi).astype(pre_ref.dtype)
    pim_ref[...] = (-(ct_r * epi + ct_i * epr)).astype(pim_ref.dtype)
    at_ref[0:1, :] = pr[t_len:t_len + 1, :]
    at_ref[1:2, :] = pi_[t_len:t_len + 1, :]


def _s5_params(lam_re, lam_im, log_dt, b_re, b_im, c_re, c_im):
    g, p, n = S5_GROUPS, S5_STATE, S5_CH
    wid = S5_T * S5_CH

    def spec(*shape):
        return pl.BlockSpec((None, None) + shape, lambda d, gi: (d, gi) + (0,) * len(shape))

    def ospec(*shape):
        return pl.BlockSpec((None, None) + shape, lambda d, gi: (gi, d) + (0,) * len(shape))
    return pl.pallas_call(
        _s5_param_kernel, grid=(2, g),
        in_specs=[spec(1, p), spec(1, p), spec(p, 1), spec(p, 1), spec(1, 1),
                  spec(p, n), spec(p, n), spec(n, p), spec(n, p)],
        out_specs=[ospec(n, wid), ospec(p, wid), ospec(p, wid), ospec(p, wid), ospec(p, wid), ospec(2, p)],
        out_shape=[jax.ShapeDtypeStruct((g, 2, n, wid), F32),
                   jax.ShapeDtypeStruct((g, 2, p, wid), BF16),
                   jax.ShapeDtypeStruct((g, 2, p, wid), BF16),
                   jax.ShapeDtypeStruct((g, 2, p, wid), BF16),
                   jax.ShapeDtypeStruct((g, 2, p, wid), BF16),
                   jax.ShapeDtypeStruct((g, 2, 2, p), F32)],
        compiler_params=_params(("parallel", "parallel"), 32),
        name="s5_params",
    )(lam_re.reshape(2, g, 1, p), lam_im.reshape(2, g, 1, p), lam_re.reshape(2, g, p, 1),
      lam_im.reshape(2, g, p, 1), log_dt.reshape(2, g, 1, 1), b_re, b_im, c_re, c_im)


def _s5_toeplitz(kt):
    g, t_len, n = S5_GROUPS, S5_T, S5_CH
    k = kt.reshape(g, 2, n, t_len, n)
    k = jnp.concatenate([k, jnp.zeros_like(k[:, :, :, :1])], axis=3)
    s = jnp.arange(t_len)[:, None]
    t = jnp.arange(t_len)[None, :]
    idx_f = jnp.where(t >= s, t - s, t_len)
    idx_b = jnp.where(s >= t, s - t, t_len)
    m = k[:, 0][:, :, idx_f, :] + k[:, 1][:, :, idx_b, :]
    return m.transpose(0, 2, 4, 3, 1).reshape(g, t_len * n, t_len * n).astype(BF16)


def _s5_state_kernel(u_ref, qre_ref, qim_ref, s_ref):
    u = u_ref[...]
    dn = (((1,), (1,)), ((), ()))
    s_ref[:, 0:128] = lax.dot_general(u, qre_ref[...], dn, preferred_element_type=F32)
    s_ref[:, 128:256] = lax.dot_general(u, qim_ref[...], dn, preferred_element_type=F32)


def _s5_scan_kernel(s_ref, are_ref, aim_ref, h_ref, *, nc, ncc):
    ar, ai = are_ref[...], aim_ref[...]
    is_f = lax.broadcasted_iota(jnp.int32, ar.shape, 1) < S5_STATE

    def body(i, carry):
        hr, hi = carry
        cf = i
        cb = jnp.where(i < ncc, ncc - 1 - i, nc + ncc - 1 - i)
        h_ref[cf, :, 0:64] = hr[:, 0:64]
        h_ref[cb, :, 64:128] = hr[:, 64:128]
        h_ref[cf, :, 128:192] = hi[:, 0:64]
        h_ref[cb, :, 192:256] = hi[:, 64:128]
        sf, sb = s_ref[cf], s_ref[cb]
        sr = jnp.where(is_f, sf[:, 0:128], sb[:, 0:128])
        si = jnp.where(is_f, sf[:, 128:256], sb[:, 128:256])
        return ar * hr - ai * hi + sr, ar * hi + ai * hr + si

    zero = jnp.zeros(ar.shape, F32)
    lax.fori_loop(0, nc, body, (zero, zero))


def _s5_out_kernel(u_ref, m_ref, h_ref, p_ref, y_ref):
    y = jnp.dot(u_ref[...], m_ref[...], preferred_element_type=F32)
    y_ref[...] = y + jnp.dot(h_ref[...].astype(BF16), p_ref[...], preferred_element_type=F32)


def _s5_mixer(s_all, kt, qre, qim, pre, pim, at, n_ctx):
    n = s_all.shape[0]
    g, t_len, wid = S5_GROUPS, S5_T, S5_T * S5_CH
    nc, ncc = n // t_len, n_ctx // t_len
    assert n % t_len == 0 and n_ctx % t_len == 0
    u = s_all.reshape(nc, t_len, g, S5_CH).transpose(2, 0, 1, 3).reshape(g, nc, wid)
    m = _s5_toeplitz(kt)
    p_all = jnp.concatenate([pre.reshape(g, 128, wid), pim.reshape(g, 128, wid)], axis=1)
    a_re = at[:, :, 0, :].reshape(g, 128)
    a_im = at[:, :, 1, :].reshape(g, 128)
    s_t = pl.pallas_call(
        _s5_state_kernel, grid=(g,),
        in_specs=[pl.BlockSpec((None, nc, wid), lambda i: (i, 0, 0)),
                  pl.BlockSpec((None, 128, wid), lambda i: (i, 0, 0)),
                  pl.BlockSpec((None, 128, wid), lambda i: (i, 0, 0))],
        out_specs=pl.BlockSpec((nc, 256), lambda i: (0, i)),
        out_shape=jax.ShapeDtypeStruct((nc, g * 256), F32),
        compiler_params=_params(("parallel",), 32),
        name="s5_chunk_state",
    )(u, qre.reshape(g, 128, wid), qim.reshape(g, 128, wid))
    h_t = pl.pallas_call(
        functools.partial(_s5_scan_kernel, nc=nc, ncc=ncc),
        out_shape=jax.ShapeDtypeStruct((nc, g, 256), F32),
        compiler_params=pltpu.CompilerParams(vmem_limit_bytes=48 * MIB),
        name="s5_chunk_scan",
    )(s_t.reshape(nc, g, 256), a_re, a_im).reshape(nc, g * 256)
    y = pl.pallas_call(
        _s5_out_kernel, grid=(g,),
        in_specs=[pl.BlockSpec((None, nc, wid), lambda i: (i, 0, 0)),
                  pl.BlockSpec((None, wid, wid), lambda i: (i, 0, 0)),
                  pl.BlockSpec((nc, 256), lambda i: (0, i)),
                  pl.BlockSpec((None, 256, wid), lambda i: (i, 0, 0))],
        out_specs=pl.BlockSpec((None, nc, wid), lambda i: (i, 0, 0)),
        out_shape=jax.ShapeDtypeStruct((g, nc, wid), F32),
        compiler_params=_params(("parallel",), 32),
        name="s5_chunk_out",
    )(u, m, h_t, p_all)
    return y.reshape(g, nc, t_len, S5_CH).transpose(1, 2, 0, 3).reshape(n, W_S5)


def _gelu_tanh(x):
    return 0.5 * x * (1.0 + jnp.tanh(math.sqrt(2.0 / math.pi) * (x + 0.044715 * (x * x * x))))


def _even_out_kernel(pool_ref, y_ref, s_ref, dsk_ref, wglu_ref, wo_ref, h_ref, nw_ref, mod_ref, o_ref,
                     *, n_ctx, tm):
    y = _gelu_tanh(y_ref[...] + s_ref[...].astype(F32) * dsk_ref[...])
    gate = jnp.dot(y.astype(BF16), wglu_ref[...], preferred_element_type=F32)
    s5 = (y * _sigmoid(gate)).astype(BF16)
    mix = jnp.dot(pool_ref[...], wo_ref[0:W_POOL, :], preferred_element_type=F32)
    mix = mix + jnp.dot(s5, wo_ref[W_POOL:W_POOL + W_S5, :], preferred_element_type=F32)
    g = _gate_rows(mod_ref, pl.program_id(0) * tm, tm, n_ctx)
    o_ref[...] = h_ref[...] + g * _rms(mix, nw_ref[...])


def _even_out(pool_out, y_s5, z, dsk, w_glu, w_out, h, nw, mod, n_ctx):
    n, d = h.shape
    tm = _pick(n, (320, 256, 128))
    return pl.pallas_call(
        functools.partial(_even_out_kernel, n_ctx=n_ctx, tm=tm), grid=(n // tm,),
        in_specs=[pl.BlockSpec((tm, W_POOL), lambda i: (i, 0)),
                  pl.BlockSpec((tm, W_S5), lambda i: (i, 0)),
                  pl.BlockSpec((tm, W_S5), lambda i: (i, W_POOL // W_S5)),
                  pl.BlockSpec((1, W_S5), lambda i: (0, 0)),
                  pl.BlockSpec((W_S5, W_S5), lambda i: (0, 0)),
                  pl.BlockSpec((d, d), lambda i: (0, 0)),
                  pl.BlockSpec((tm, d), lambda i: (i, 0)),
                  pl.BlockSpec((1, d), lambda i: (0, 0)),
                  pl.BlockSpec((8, d), lambda i: (0, 0))],
        out_specs=pl.BlockSpec((tm, d), lambda i: (i, 0)),
        out_shape=jax.ShapeDtypeStruct((n, d), F32),
        compiler_params=_params(("parallel",), 48),
        name="even_mixer_out",
    )(pool_out, y_s5, z, dsk.reshape(1, W_S5), w_glu, w_out, h, nw.reshape(1, d), mod)


def _ffn_kernel(h_ref, nw2_ref, nw3_ref, mod_ref, wg_ref, wu_ref, wd_ref, o_ref, v_ref, *, n_ctx, tm):
    f = pl.program_id(1)

    @pl.when(f == 0)
    def _():
        v = _norm_mod(h_ref[...], nw2_ref[...], mod_ref, pl.program_id(0) * tm, n_ctx)
        v_ref[...] = v.astype(BF16)
        o_ref[...] = jnp.zeros_like(o_ref)
    v = v_ref[...]
    a = jnp.dot(v, wg_ref[...], preferred_element_type=F32)
    b = jnp.dot(v, wu_ref[...], preferred_element_type=F32)
    act = (a * _sigmoid(a) * b).astype(BF16)
    o_ref[...] += jnp.dot(act, wd_ref[...], preferred_element_type=F32)

    @pl.when(f == pl.num_programs(1) - 1)
    def _():
        g = _gate_rows(mod_ref, pl.program_id(0) * tm, tm, n_ctx)
        o_ref[...] = h_ref[...] + g * _rms(o_ref[...], nw3_ref[...])


def _dense_ffn(h, nw2, nw3, mod, wg, wu, wd, n_ctx):
    n, d = h.shape
    dff = wg.shape[1]
    tm = _pick(n, (640, 512, 256, 128))
    tf = 512
    return pl.pallas_call(
        functools.partial(_ffn_kernel, n_ctx=n_ctx, tm=tm), grid=(n // tm, dff // tf),
        in_specs=[pl.BlockSpec((tm, d), lambda i, f: (i, 0)),
                  pl.BlockSpec((1, d), lambda i, f: (0, 0)),
                  pl.BlockSpec((1, d), lambda i, f: (0, 0)),
                  pl.BlockSpec((8, d), lambda i, f: (0, 0)),
                  pl.BlockSpec((d, tf), lambda i, f: (0, f)),
                  pl.BlockSpec((d, tf), lambda i, f: (0, f)),
                  pl.BlockSpec((tf, d), lambda i, f: (f, 0))],
        out_specs=pl.BlockSpec((tm, d), lambda i, f: (i, 0)),
        out_shape=jax.ShapeDtypeStruct((n, d), F32),
        scratch_shapes=[pltpu.VMEM((tm, d), BF16)],
        compiler_params=_params(("parallel", "arbitrary"), 56),
        name="dense_swiglu",
    )(h, nw2.reshape(1, d), nw3.reshape(1, d), mod, wg, wu, wd)


def _split_bf16(x):
    hi = x.astype(BF16)
    return hi, (x - hi.astype(F32)).astype(BF16)


def _gla_kernel(qf_ref, kf_ref, vf_ref, af_ref, qb_ref, kb_ref, vb_ref, ab_ref, wa_ref, ba_ref,
                of_ref, ob_ref, st_ref, *, tb):
    c = GLA_CHUNK
    nsub = tb // c

    @pl.when(pl.program_id(0) == 0)
    def _():
        st_ref[...] = jnp.zeros_like(st_ref)
    ri = lax.broadcasted_iota(jnp.int32, (c, c), 0)
    ci = lax.broadcasted_iota(jnp.int32, (c, c), 1)
    scale = GLA_DK ** -0.5
    nt = (((1,), (1,)), ((), ()))
    tn = (((0,), (0,)), ((), ()))
    dirs = ((qf_ref, kf_ref, vf_ref, af_ref, of_ref), (qb_ref, kb_ref, vb_ref, ab_ref, ob_ref))
    for d, (q_ref, k_ref, v_ref, a_ref, o_ref) in enumerate(dirs):
        mask = (ri >= ci) if d == 0 else (ci >= ri)
        cum = mask.astype(BF16)
        a_hi, a_lo = _split_bf16(a_ref[...])
        w_hi, w_lo = _split_bf16(wa_ref[d])
        z = (jnp.dot(a_hi, w_hi, preferred_element_type=F32) + jnp.dot(a_hi, w_lo, preferred_element_type=F32)
             + jnp.dot(a_lo, w_hi, preferred_element_type=F32)) + ba_ref[d]
        glog = (jnp.minimum(z, 0.0) - jnp.log(1.0 + jnp.exp(-jnp.abs(z)))) * (1.0 / GLA_TAU)
        order = range(nsub) if d == 0 else range(nsub - 1, -1, -1)
        for sc in order:
            r0 = sc * c
            g_hi, g_lo = _split_bf16(glog[r0:r0 + c, :])
            b = jnp.dot(cum, g_hi, preferred_element_type=F32) + jnp.dot(cum, g_lo, preferred_element_type=F32)
            b_last = b[c - 1:c, :] if d == 0 else b[0:1, :]
            dec = jnp.exp(b_last)
            q = q_ref[r0:r0 + c, :].astype(F32) * scale
            k = k_ref[r0:r0 + c, :].astype(F32)
            qe = (q * jnp.exp(b)).astype(BF16)
            ke = (k * jnp.exp(-b)).astype(BF16)
            kd = (k * jnp.exp(b_last - b)).astype(BF16)
            for h in range(GLA_HEADS):
                ks = slice(h * GLA_DK, (h + 1) * GLA_DK)
                vs = slice(h * GLA_DV, (h + 1) * GLA_DV)
                att = lax.dot_general(qe[:, ks], ke[:, ks], nt, preferred_element_type=F32)
                att = jnp.where(mask, att, 0.0).astype(BF16)
                vh = v_ref[r0:r0 + c, vs]
                s_old = st_ref[d, h]
                o = jnp.dot(att, vh, preferred_element_type=F32)
                o = o + lax.dot_general(qe[:, ks], s_old.astype(BF16), nt, preferred_element_type=F32)
                o_ref[r0:r0 + c, vs] = o
                upd = lax.dot_general(vh, kd[:, ks], tn, preferred_element_type=F32)
                st_ref[d, h] = s_old * dec[:, ks] + upd


def _gla_mixer(zmain, a_aux, w_a, b_a, n_ctx):
    n = zmain.shape[0]
    tb = 256
    assert n % tb == 0 and n_ctx % tb == 0
    nb, ncb = n // tb, n_ctx // tb

    def bwd(s):
        return jnp.where(s < ncb, ncb - 1 - s, nb + ncb - 1 - s)
    wa = jnp.pad(w_a, ((0, 0), (0, 128 - GLA_RANK), (0, 0)))
    in_specs = []
    for order in (lambda s: s, bwd):
        in_specs += [pl.BlockSpec((tb, W_GLA_K), lambda s, o=order: (o(s), OQ // W_GLA_K)),
                     pl.BlockSpec((tb, W_GLA_K), lambda s, o=order: (o(s), OK_ // W_GLA_K)),
                     pl.BlockSpec((tb, W_GLA_V), lambda s, o=order: (o(s), OV // W_GLA_V)),
                     pl.BlockSpec((tb, 128), lambda s, o=order: (o(s), 0))]
    in_specs += [pl.BlockSpec((2, 128, W_GLA_K), lambda s: (0, 0, 0)),
                 pl.BlockSpec((2, 1, W_GLA_K), lambda s: (0, 0, 0))]
    return pl.pallas_call(
        functools.partial(_gla_kernel, tb=tb), grid=(nb,),
        in_specs=in_specs,
        out_specs=[pl.BlockSpec((tb, W_GLA_V), lambda s: (s, 0)),
                   pl.BlockSpec((tb, W_GLA_V), lambda s: (bwd(s), 0))],
        out_shape=[jax.ShapeDtypeStruct((n, W_GLA_V), F32)] * 2,
        scratch_shapes=[pltpu.VMEM((2, GLA_HEADS, GLA_DV, GLA_DK), F32)],
        compiler_params=_params(("arbitrary",), 32),
        name="gla_chunked",
    )(zmain, zmain, zmain, a_aux, zmain, zmain, zmain, a_aux, wa, b_a.reshape(2, 1, W_GLA_K))


_ATT_PAD = 2 * ATT_HEAD_DIM
_LOG2E = 1.4426950408889634
_FIXED_SHIFT_MAX_BOUND = 40.0


def _rope_fn(row0, tm, n_ctx):
    hd = ATT_HEAD_DIM
    row = row0 + lax.broadcasted_iota(jnp.int32, (tm, hd), 0)
    lane = lax.broadcasted_iota(jnp.int32, (tm, hd), 1)
    t = row - n_ctx
    pos = jnp.where(lane < hd // 2, t // GRID_W, t % GRID_W).astype(F32)
    quarter = hd // 4
    freq = jnp.exp((lane % quarter).astype(F32) * (-math.log(ROPE_THETA) / quarter))
    ang = pos * freq
    first = (lane & quarter) == 0
    cosv = jnp.cos(ang)
    sin_s = jnp.where(first, -jnp.sin(ang), jnp.sin(ang))

    def rope(x):
        sw = jnp.where(first, pltpu.roll(x, hd - quarter, 1), pltpu.roll(x, quarter, 1))
        return x * cosv + sw * sin_s
    return rope, row >= n_ctx


def _k_prep_kernel(k_ref, v_ref, kw_ref, kx_ref, vx_ref, kmax_ref, *, n_ctx, tm):
    hd = ATT_HEAD_DIM
    rope, is_lat = _rope_fn(pl.program_id(0) * tm, tm, n_ctx)
    lane = lax.broadcasted_iota(jnp.int32, (tm, hd), 1)
    one_col = jnp.where(lane == 0, 1.0, 0.0).astype(BF16)
    ones = jnp.ones((tm, hd), BF16)
    nmax = jnp.zeros((tm, 1), F32)
    for h in range(ATT_KV_HEADS):
        sl = slice(h * hd, (h + 1) * hd)
        xn = _rms(k_ref[:, sl].astype(F32), kw_ref[...])
        kr = jnp.where(is_lat, rope(xn), xn).astype(BF16)
        kx_ref[:, h * _ATT_PAD:h * _ATT_PAD + hd] = kr
        kx_ref[:, h * _ATT_PAD + hd:(h + 1) * _ATT_PAD] = one_col
        vx_ref[:, h * _ATT_PAD:h * _ATT_PAD + hd] = v_ref[:, sl]
        vx_ref[:, h * _ATT_PAD + hd:(h + 1) * _ATT_PAD] = ones
        krf = kr.astype(F32)
        nmax = jnp.maximum(nmax, jnp.sum(krf * krf, axis=-1, keepdims=True))
    kmax_ref[...] = jnp.broadcast_to(jnp.max(nmax, axis=0, keepdims=True), kmax_ref.shape)


def _q_prep_kernel(q_ref, qw_ref, kn_ref, qx_ref, bmax_ref, *, n_ctx, tm):
    hd = ATT_HEAD_DIM
    rope, _ = _rope_fn(n_ctx + pl.program_id(0) * tm, tm, n_ctx)
    lane = lax.broadcasted_iota(jnp.int32, (tm, hd), 1)
    scale = hd ** -0.5 * _LOG2E
    bmax = jnp.zeros((tm, 1), F32)
    for h in range(ATT_HEADS):
        sl = slice(h * hd, (h + 1) * hd)
        xn = _rms(q_ref[:, sl].astype(F32), qw_ref[...])
        qr = (rope(xn) * scale).astype(BF16)
        qf = qr.astype(F32)
        bound = jnp.sqrt(jnp.sum(qf * qf, axis=-1, keepdims=True)) * kn_ref[...]
        qx_ref[:, h * _ATT_PAD:h * _ATT_PAD + hd] = qr
        qx_ref[:, h * _ATT_PAD + hd:(h + 1) * _ATT_PAD] = jnp.where(lane == 0, -bound, 0.0).astype(BF16)
        bmax = jnp.maximum(bmax, bound)
    bmax_ref[...] = jnp.broadcast_to(jnp.max(bmax, axis=0, keepdims=True), bmax_ref.shape)


def _qk_prep(zmain, q_norm, k_norm, n_ctx):
    n = zmain.shape[0]
    lq = n - n_ctx
    tm = 256
    hd = ATT_HEAD_DIM
    assert n_ctx % tm == 0
    ncb = n_ctx // tm
    kx, vx, kmax = pl.pallas_call(
        functools.partial(_k_prep_kernel, n_ctx=n_ctx, tm=tm), grid=(n // tm,),
        in_specs=[pl.BlockSpec((tm, W_ATT_KV), lambda i: (i, OAK // W_ATT_KV)),
                  pl.BlockSpec((tm, W_ATT_KV), lambda i: (i, OAV // W_ATT_KV)),
                  pl.BlockSpec((1, hd), lambda i: (0, 0))],
        out_specs=[pl.BlockSpec((tm, ATT_KV_HEADS * _ATT_PAD), lambda i: (i, 0)),
                   pl.BlockSpec((tm, ATT_KV_HEADS * _ATT_PAD), lambda i: (i, 0)),
                   pl.BlockSpec((None, 8, 128), lambda i: (i, 0, 0))],
        out_shape=[jax.ShapeDtypeStruct((n, ATT_KV_HEADS * _ATT_PAD), BF16),
                   jax.ShapeDtypeStruct((n, ATT_KV_HEADS * _ATT_PAD), BF16),
                   jax.ShapeDtypeStruct((n // tm, 8, 128), F32)],
        compiler_params=_params(("parallel",), 32),
        name="k_norm_rope",
    )(zmain, zmain, k_norm.reshape(1, hd))
    knorm = jnp.sqrt(jnp.max(kmax)).reshape(1, 1)
    qx, bmax = pl.pallas_call(
        functools.partial(_q_prep_kernel, n_ctx=n_ctx, tm=tm), grid=(lq // tm,),
        in_specs=[pl.BlockSpec((tm, W_ATT), lambda i: (i + ncb, OAQ // W_ATT)),
                  pl.BlockSpec((1, hd), lambda i: (0, 0)),
                  pl.BlockSpec((1, 1), lambda i: (0, 0))],
        out_specs=[pl.BlockSpec((tm, ATT_HEADS * _ATT_PAD), lambda i: (i, 0)),
                   pl.BlockSpec((None, 8, 128), lambda i: (i, 0, 0))],
        out_shape=[jax.ShapeDtypeStruct((lq, ATT_HEADS * _ATT_PAD), BF16),
                   jax.ShapeDtypeStruct((lq // tm, 8, 128), F32)],
        compiler_params=_params(("parallel",), 32),
        name="q_norm_rope",
    )(zmain, q_norm.reshape(1, hd), knorm)
    return qx, kx, vx, jnp.max(bmax)


def _flash_fixed_kernel(q_ref, kt_ref, v_ref, o_ref, acc_ref):
    c = pl.program_id(2)
    hd = ATT_HEAD_DIM

    @pl.when(c == 0)
    def _():
        acc_ref[...] = jnp.zeros_like(acc_ref)
    kt = kt_ref[...]
    v = v_ref[...]
    for g in range(ATT_GROUP):
        s = jnp.dot(q_ref[:, g * _ATT_PAD:(g + 1) * _ATT_PAD], kt, preferred_element_type=F32)
        acc_ref[g] += jnp.dot(jnp.exp2(s).astype(BF16), v, preferred_element_type=F32)

    @pl.when(c == pl.num_programs(2) - 1)
    def _():
        for g in range(ATT_GROUP):
            a = acc_ref[g]
            o_ref[:, g * hd:(g + 1) * hd] = (a[:, 0:hd] / a[:, hd:2 * hd]).astype(o_ref.dtype)


def _flash_online_kernel(q_ref, kt_ref, v_ref, o_ref, m_ref, acc_ref):
    c = pl.program_id(2)
    hd = ATT_HEAD_DIM

    @pl.when(c == 0)
    def _():
        m_ref[...] = jnp.full(m_ref.shape, -jnp.inf, F32)
        acc_ref[...] = jnp.zeros_like(acc_ref)
    kt = kt_ref[...]
    v = v_ref[...]
    for g in range(ATT_GROUP):
        s = jnp.dot(q_ref[:, g * _ATT_PAD:(g + 1) * _ATT_PAD], kt, preferred_element_type=F32)
        m_prev = m_ref[g]
        m_new = jnp.maximum(m_prev, jnp.max(s, axis=-1, keepdims=True))
        p = jnp.exp2(s - m_new).astype(BF16)
        acc_ref[g] = jnp.exp2(m_prev - m_new) * acc_ref[g] + jnp.dot(p, v, preferred_element_type=F32)
        m_ref[g] = m_new

    @pl.when(c == pl.num_programs(2) - 1)
    def _():
        for g in range(ATT_GROUP):
            a = acc_ref[g]
            o_ref[:, g * hd:(g + 1) * hd] = (a[:, 0:hd] / a[:, hd:2 * hd]).astype(o_ref.dtype)


def _attention(qx, kx, vx, bound_log2):
    lq = qx.shape[0]
    n = kx.shape[0]
    kt = kx.T
    gw = ATT_GROUP * _ATT_PAD
    ow = ATT_GROUP * ATT_HEAD_DIM
    tq = _pick(lq, (512, 256, 128))

    def call(kern, tk, scratch, name):
        return pl.pallas_call(
            kern, grid=(ATT_KV_HEADS, lq // tq, n // tk),
            in_specs=[pl.BlockSpec((tq, gw), lambda h, i, j: (i, h)),
                      pl.BlockSpec((_ATT_PAD, tk), lambda h, i, j: (h, j)),
                      pl.BlockSpec((tk, _ATT_PAD), lambda h, i, j: (j, h))],
            out_specs=pl.BlockSpec((tq, ow), lambda h, i, j: (i, h)),
            out_shape=jax.ShapeDtypeStruct((lq, W_ATT), BF16),
            scratch_shapes=scratch + [pltpu.VMEM((ATT_GROUP, tq, _ATT_PAD), F32)],
            compiler_params=_params(("parallel", "parallel", "arbitrary"), 48),
            name=name,
        )(qx, kt, vx)

    def fixed(_):
        return call(_flash_fixed_kernel, _pick(n, (3328, 1280, 512, 256)), [], "gqa_flash_fixed_shift")

    def online(_):
        return call(_flash_online_kernel, _pick(n, (640, 512, 256, 128)),
                    [pltpu.VMEM((ATT_GROUP, tq, 1), F32)], "gqa_flash_online")
    return lax.cond(bound_log2 <= _FIXED_SHIFT_MAX_BOUND * _LOG2E, fixed, online, None)


def _odd_out_kernel(of_ref, ob_ref, r_ref, att_ref, gn_ref, wo_ref, h_ref, nw_ref, mod_ref, o_ref):
    o = of_ref[...] + ob_ref[...]
    r = r_ref[...].astype(F32)
    silu_r = r * _sigmoid(r)
    parts = []
    for h in range(GLA_HEADS):
        vs = slice(h * GLA_DV, (h + 1) * GLA_DV)
        parts.append((_rms(o[:, vs], gn_ref[:, vs]) * silu_r[:, vs]).astype(BF16))
    gla = jnp.concatenate(parts, axis=1)
    mix = jnp.dot(gla, wo_ref[0:W_GLA_V, :], preferred_element_type=F32)
    mix = mix + jnp.dot(att_ref[...], wo_ref[W_GLA_V:W_GLA_V + W_ATT, :], preferred_element_type=F32)
    o_ref[...] = h_ref[...] + mod_ref[4:5, :] * _rms(mix, nw_ref[...])


def _odd_out(o_f, o_b, zmain, att, gla_norm, w_out, h_all, nw, mod, n_ctx):
    n, d = h_all.shape
    lq = n - n_ctx
    tm = _pick(lq, (256, 128))
    assert n_ctx % tm == 0
    off = n_ctx // tm
    return pl.pallas_call(
        _odd_out_kernel, grid=(lq // tm,),
        in_specs=[pl.BlockSpec((tm, W_GLA_V), lambda i: (i + off, 0)),
                  pl.BlockSpec((tm, W_GLA_V), lambda i: (i + off, 0)),
                  pl.BlockSpec((tm, W_GLA_V), lambda i: (i + off, OR // W_GLA_V)),
                  pl.BlockSpec((tm, W_ATT), lambda i: (i, 0)),
                  pl.BlockSpec((1, W_GLA_V), lambda i: (0, 0)),
                  pl.BlockSpec((d, d), lambda i: (0, 0)),
                  pl.BlockSpec((tm, d), lambda i: (i + off, 0)),
                  pl.BlockSpec((1, d), lambda i: (0, 0)),
                  pl.BlockSpec((8, d), lambda i: (0, 0))],
        out_specs=pl.BlockSpec((tm, d), lambda i: (i, 0)),
        out_shape=jax.ShapeDtypeStruct((lq, d), F32),
        compiler_params=_params(("parallel",), 48),
        name="odd_mixer_out",
    )(o_f, o_b, zmain, att, gla_norm.reshape(1, W_GLA_V), w_out, h_all, nw.reshape(1, d), mod)


def _router_kernel(h_ref, nw_ref, mod_ref, wr_ref, br_ref, v_ref, r_ref):
    x = h_ref[...]
    v = _rms(x, nw_ref[...]) * (1.0 + mod_ref[0:1, :]) + mod_ref[1:2, :]
    v_ref[...] = v.astype(BF16)
    logits = _dot_hi(v, wr_ref[...]) + br_ref[...]
    lane = lax.broadcasted_iota(jnp.int32, logits.shape, 1)
    m1 = jnp.max(logits, axis=-1, keepdims=True)
    i1 = jnp.min(jnp.where(logits == m1, lane, 128), axis=-1, keepdims=True)
    rest = jnp.where(lane == i1, -jnp.inf, logits)
    m2 = jnp.max(rest, axis=-1, keepdims=True)
    i2 = jnp.min(jnp.where(rest == m2, lane, 128), axis=-1, keepdims=True)
    e2 = jnp.exp(m2 - m1)
    g1 = 1.0 / (1.0 + e2)
    g2 = e2 / (1.0 + e2)
    out = jnp.where(lane == 0, i1.astype(F32), 0.0)
    out = jnp.where(lane == 1, i2.astype(F32), out)
    out = jnp.where(lane == 2, g1, out)
    out = jnp.where(lane == 3, g2, out)
    r_ref[...] = out


def _router(h_lat, nw, mod, w_router, b_router):
    lq, d = h_lat.shape
    tm = _pick(lq, (512, 256, 128))
    wr = jnp.pad(w_router, ((0, 0), (0, 128 - N_EXPERTS)))
    br = jnp.pad(b_router, (0, 128 - N_EXPERTS), constant_values=-1e30).reshape(1, 128)
    return pl.pallas_call(
        _router_kernel, grid=(lq // tm,),
        in_specs=[pl.BlockSpec((tm, d), lambda i: (i, 0)),
                  pl.BlockSpec((1, d), lambda i: (0, 0)),
                  pl.BlockSpec((8, d), lambda i: (0, 0)),
                  pl.BlockSpec((d, 128), lambda i: (0, 0)),
                  pl.BlockSpec((1, 128), lambda i: (0, 0))],
        out_specs=[pl.BlockSpec((tm, d), lambda i: (i, 0)), pl.BlockSpec((tm, 128), lambda i: (i, 0))],
        out_shape=[jax.ShapeDtypeStruct((lq, d), BF16), jax.ShapeDtypeStruct((lq, 128), F32)],
        compiler_params=_params(("parallel",), 32),
        name="router_top2",
    )(h_lat, nw.reshape(1, d), mod, wr, br)


def _dispatch_kernel(j_ref, b_ref, fl_ref, tok_ref, v_ref, o_ref, *, sblk):
    s = pl.program_id(0)
    fl = fl_ref[s]

    @pl.when((fl & 2) != 0)
    def _():
        o_ref[...] = jnp.zeros_like(o_ref)

    @pl.when((fl & 1) != 0)
    def _():
        cols = b_ref[s] * sblk + lax.broadcasted_iota(jnp.int32, (1, sblk), 1)
        onehot = (tok_ref[...] == cols).astype(BF16)
        o_ref[...] += jnp.dot(onehot, v_ref[...], preferred_element_type=F32).astype(o_ref.dtype)


def _expert_kernel(te_ref, ts_ref, tv_ref, x_ref, gt_ref, wg_ref, wu_ref, wd_ref, o_ref, acc_ref):
    j = pl.program_id(0)
    f = pl.program_id(1)
    valid = tv_ref[j] == 1

    @pl.when(f == 0)
    def _():
        acc_ref[...] = jnp.zeros_like(acc_ref)

    @pl.when(valid)
    def _():
        x = x_ref[...]
        a = jnp.dot(x, wg_ref[...], preferred_element_type=F32)
        b = jnp.dot(x, wu_ref[...], preferred_element_type=F32)
        act = (a * _sigmoid(a) * b).astype(BF16)
        acc_ref[...] += jnp.dot(act, wd_ref[...], preferred_element_type=F32)

    @pl.when(f == pl.num_programs(1) - 1)
    def _():
        o_ref[...] = (acc_ref[...] * gt_ref[...]).astype(o_ref.dtype)


def _combine_kernel(ci_ref, cy_ref, fl_ref, tok_ref, y_ref, h_ref, nw_ref, mod_ref, o_ref, acc_ref, *, sblk):
    s = pl.program_id(0)
    fl = fl_ref[s]

    @pl.when((fl & 2) != 0)
    def _():
        acc_ref[...] = jnp.zeros_like(acc_ref)

    @pl.when((fl & 1) != 0)
    def _():
        rows = ci_ref[s] * sblk + lax.broadcasted_iota(jnp.int32, (sblk, 1), 0)
        onehot = (tok_ref[...] == rows).astype(BF16)
        acc_ref[...] += jnp.dot(onehot, y_ref[...], preferred_element_type=F32)

    @pl.when((fl & 4) != 0)
    def _():
        o_ref[...] = h_ref[...] + mod_ref[4:5, :] * _rms(acc_ref[...], nw_ref[...])


def _count_le(sorted_vals, queries):
    return jnp.sum((sorted_vals[None, :] <= queries[:, None]).astype(jnp.int32), axis=1)


def _moe(h_lat, nw2, nw3, mod, w_router, b_router, wg, wu, wd):
    lq, d = h_lat.shape
    e = N_EXPERTS
    dff = wg.shape[2]
    tm = _pick(lq, (512, 256))
    tf = 1024
    v, route = _router(h_lat, nw2, mod, w_router, b_router)
    idx = route[:, 0:2].astype(jnp.int32)
    gates = route[:, 2:4]

    npairs = 2 * lq
    nt = npairs // tm + e
    nblk = lq // tm
    flat_e = idx.reshape(-1)
    tok = jnp.arange(npairs, dtype=jnp.int32) // 2
    onehot = (flat_e[:, None] == jnp.arange(e, dtype=jnp.int32)[None, :]).astype(jnp.int32)
    csum = jnp.cumsum(onehot, axis=0)
    rank = jnp.sum(csum * onehot, axis=1) - 1
    counts = csum[-1]
    padded = ((counts + tm - 1) // tm) * tm
    ends = jnp.cumsum(padded)
    starts = ends - padded
    pos = jnp.sum(onehot * starts[None, :], axis=1) + rank
    pair = jnp.stack([tok, lax.bitcast_convert_type(gates.reshape(-1), jnp.int32)], axis=1)
    empty = jnp.broadcast_to(jnp.array([-1, 0], jnp.int32), (nt * tm, 2))
    slots = empty.at[pos].set(pair)
    sorted_tok = slots[:, 0]
    sorted_gate = lax.bitcast_convert_type(slots[:, 1], F32)
    n_used = ends[-1] // tm
    tile_ids = jnp.arange(nt, dtype=jnp.int32)
    tile_valid = (tile_ids < n_used).astype(jnp.int32)
    tile_src = jnp.minimum(tile_ids, n_used - 1).astype(jnp.int32)
    tile_exp = jnp.minimum(_count_le(ends, tile_src * tm), e - 1)

    st2 = sorted_tok.reshape(nt, tm)
    blo = st2[:, 0] // tm
    bhi = jnp.max(st2, axis=1) // tm
    nb = jnp.where(tile_valid == 1, bhi - blo + 1, 0)
    cs = jnp.cumsum(nb)
    w_d = nt + e * (nblk - 1)
    sidx = jnp.arange(w_d, dtype=jnp.int32)
    sc = jnp.minimum(sidx, cs[-1] - 1)
    dj = _count_le(cs, sc)
    dstart = cs[dj] - nb[dj]
    db = (blo[dj] + sc - dstart).astype(jnp.int32)
    dact = sidx < cs[-1]
    dfl = (dact.astype(jnp.int32) + 2 * (dact & (sc == dstart)).astype(jnp.int32)).astype(jnp.int32)

    xs = pl.pallas_call(
        functools.partial(_dispatch_kernel, sblk=tm),
        grid_spec=pltpu.PrefetchScalarGridSpec(
            num_scalar_prefetch=3, grid=(w_d,),
            in_specs=[pl.BlockSpec((tm, 1), lambda s, j, b, fl: (j[s], 0)),
                      pl.BlockSpec((tm, d), lambda s, j, b, fl: (b[s], 0))],
            out_specs=pl.BlockSpec((tm, d), lambda s, j, b, fl: (j[s], 0))),
        out_shape=jax.ShapeDtypeStruct((nt * tm, d), BF16),
        compiler_params=_params(("arbitrary",), 32),
        name="moe_dispatch",
    )(dj, db, dfl, sorted_tok.reshape(nt * tm, 1), v)

    nf = dff // tf
    ys = pl.pallas_call(
        _expert_kernel,
        grid_spec=pltpu.PrefetchScalarGridSpec(
            num_scalar_prefetch=3, grid=(nt, nf),
            in_specs=[pl.BlockSpec((tm, d), lambda j, f, te, ts, tv: (ts[j], 0)),
                      pl.BlockSpec((tm, 1), lambda j, f, te, ts, tv: (ts[j], 0)),
                      pl.BlockSpec((None, d, tf), lambda j, f, te, ts, tv: (te[j], 0, jnp.where(tv[j] == 1, f, nf - 1))),
                      pl.BlockSpec((None, d, tf), lambda j, f, te, ts, tv: (te[j], 0, jnp.where(tv[j] == 1, f, nf - 1))),
                      pl.BlockSpec((None, tf, d), lambda j, f, te, ts, tv: (te[j], jnp.where(tv[j] == 1, f, nf - 1), 0))],
            out_specs=pl.BlockSpec((tm, d), lambda j, f, te, ts, tv: (j, 0)),
            scratch_shapes=[pltpu.VMEM((tm, d), F32)]),
        out_shape=jax.ShapeDtypeStruct((nt * tm, d), BF16),
        compiler_params=_params(("arbitrary", "arbitrary"), 56),
        name="moe_experts",
    )(tile_exp, tile_src, tile_valid, xs, sorted_gate.reshape(nt * tm, 1), wg, wu, wd)

    key = (tok // tm) * e + flat_e
    yj = (pos // tm).astype(jnp.int32)
    jlo = jnp.full((nblk * e,), nt, jnp.int32).at[key].min(yj)
    jhi = jnp.full((nblk * e,), -1, jnp.int32).at[key].max(yj)
    nbc = jnp.where(jhi >= 0, jhi - jlo + 1, 0)
    cs3 = jnp.cumsum(nbc)
    w_c = nt + e * (nblk - 1)
    sidx = jnp.arange(w_c, dtype=jnp.int32)
    sc = jnp.minimum(sidx, cs3[-1] - 1)
    kidx = _count_le(cs3, sc)
    ci = (kidx // e).astype(jnp.int32)
    cy = (jlo[kidx] + sc - (cs3[kidx] - nbc[kidx])).astype(jnp.int32)
    cact = sidx < cs3[-1]
    blk_first = cs3[ci * e] - nbc[ci * e]
    blk_last = cs3[ci * e + e - 1] - 1
    cfl = (cact.astype(jnp.int32) + 2 * (cact & (sc == blk_first)).astype(jnp.int32)
           + 4 * (cact & (sc == blk_last)).astype(jnp.int32)).astype(jnp.int32)

    return pl.pallas_call(
        functools.partial(_combine_kernel, sblk=tm),
        grid_spec=pltpu.PrefetchScalarGridSpec(
            num_scalar_prefetch=3, grid=(w_c,),
            in_specs=[pl.BlockSpec((None, 1, tm), lambda s, ci_, cy_, fl: (cy_[s], 0, 0)),
                      pl.BlockSpec((tm, d), lambda s, ci_, cy_, fl: (cy_[s], 0)),
                      pl.BlockSpec((tm, d), lambda s, ci_, cy_, fl: (ci_[s], 0)),
                      pl.BlockSpec((1, d), lambda s, ci_, cy_, fl: (0, 0)),
                      pl.BlockSpec((8, d), lambda s, ci_, cy_, fl: (0, 0))],
            out_specs=pl.BlockSpec((tm, d), lambda s, ci_, cy_, fl: (ci_[s], 0)),
            scratch_shapes=[pltpu.VMEM((tm, d), F32)]),
        out_shape=jax.ShapeDtypeStruct((lq, d), F32),
        compiler_params=_params(("arbitrary",), 40),
        name="moe_combine",
    )(ci, cy, cfl, sorted_tok.reshape(nt, 1, tm), ys, h_lat, nw3.reshape(1, d), mod)


def kernel(x, c, ctx, c_ctx, w_mod, b_mod, norms, e_w_in, e_pool_w, e_pool_scale, e_s5_lam_re, e_s5_lam_im, e_s5_log_dt, e_s5_b_re, e_s5_b_im, e_s5_c_re, e_s5_c_im, e_s5_d, e_s5_w_glu, e_w_out, e_ffn_gate, e_ffn_up, e_ffn_down, o_w_in, o_gla_w_a, o_gla_b_a, o_gla_norm, o_q_norm, o_k_norm, o_w_out, o_router, o_router_b, o_moe_gate, o_moe_up, o_moe_down):
    assert x.shape[0] == 1 and w_mod.shape[0] == 2 and e_w_in.shape[0] == 1 and o_w_in.shape[0] == 1
    d = x.shape[2]
    n_ctx = ctx.shape[1]
    h = jnp.concatenate([ctx[0], x[0]], axis=0)

    vecs = jnp.zeros((8, d), F32).at[0].set(c[0]).at[1].set(c_ctx)
    mods = _modulation(vecs, w_mod, b_mod)

    mod1, mod2 = _mod_rows(mods[0], 0), _mod_rows(mods[0], 1)
    z = _normed_matmul(h, norms[0, 0], mod1, e_w_in[0].astype(BF16), n_ctx)
    pool_out = _pool_mixer(z, e_pool_w[0].astype(BF16), e_pool_scale[0], n_ctx)
    kt, qre, qim, pre, pim, at = _s5_params(e_s5_lam_re[0], e_s5_lam_im[0], e_s5_log_dt[0],
                                            e_s5_b_re[0], e_s5_b_im[0], e_s5_c_re[0], e_s5_c_im[0])
    y_s5 = _s5_mixer(z[:, W_POOL:], kt, qre, qim, pre, pim, at, n_ctx)
    h = _even_out(pool_out, y_s5, z, e_s5_d[0], e_s5_w_glu[0].astype(BF16), e_w_out[0].astype(BF16),
                  h, norms[0, 1], mod1, n_ctx)
    h = _dense_ffn(h, norms[0, 2], norms[0, 3], mod2, e_ffn_gate[0].astype(BF16),
                   e_ffn_up[0].astype(BF16), e_ffn_down[0].astype(BF16), n_ctx)

    mod1, mod2 = _mod_rows(mods[1], 0), _mod_rows(mods[1], 1)
    w_in = o_w_in[0]
    a0 = 2 * W_GLA_K + W_GLA_V
    w_main = jnp.concatenate([w_in[:, :a0], w_in[:, a0 + GLA_RANK:]], axis=1).astype(BF16)
    w_aux = jnp.pad(w_in[:, a0:a0 + GLA_RANK], ((0, 0), (0, 128 - GLA_RANK))).astype(BF16)
    zmain, a_aux = _normed_matmul(h, norms[1, 0], mod1, w_main, n_ctx, w_aux=w_aux)
    o_f, o_b = _gla_mixer(zmain, a_aux, o_gla_w_a[0], o_gla_b_a[0], n_ctx)
    qx, kx, vx, bound_log2 = _qk_prep(zmain, o_q_norm[0], o_k_norm[0], n_ctx)
    att = _attention(qx, kx, vx, bound_log2)
    h_lat = _odd_out(o_f, o_b, zmain, att, o_gla_norm[0], o_w_out[0].astype(BF16), h, norms[1, 1], mod1, n_ctx)
    out = _moe(h_lat, norms[1, 2], norms[1, 3], mod2, o_router[0], o_router_b[0],
               o_moe_gate[0].astype(BF16), o_moe_up[0].astype(BF16), o_moe_down[0].astype(BF16))
    return out[None]
```

```python
import functools
import math

import jax
import jax.numpy as jnp
from jax import lax
from jax.experimental import pallas as pl
from jax.experimental.pallas import tpu as pltpu

F32 = jnp.float32
BF16 = jnp.bfloat16
HI = lax.Precision.HIGHEST
EPS = 1e-6

D_MODEL = 2048
GRID_W = 64
N_MOD = 6

POOL_WINDOWS = (2, 4, 8, 16)
POOL_GROUP = 384
W_POOL = 1536
W_S5 = 512
S5_CH = 16
S5_STATE = 64
S5_GROUPS = 32
S5_T = 32

GLA_HEADS = 4
GLA_DK = 128
GLA_DV = 256
GLA_RANK = 16
GLA_TAU = 16.0
GLA_CHUNK = 64
W_GLA_K = 512
W_GLA_V = 1024
ATT_HEAD_DIM = 128
ATT_HEADS = 8
ATT_KV_HEADS = 2
ATT_GROUP = 4
W_ATT = 1024
W_ATT_KV = 256
ROPE_THETA = 10000.0

D_FF = 7168
N_EXPERTS = 8

OQ, OK_, OV, OR, OAQ, OAK, OAV = 0, 512, 1024, 2048, 3072, 4096, 4352
W_ODD_MAIN = 4608

MIB = 2 ** 20


def _params(sem, vmem_mib):
    return pltpu.CompilerParams(dimension_semantics=sem, vmem_limit_bytes=vmem_mib * MIB)


def _pick(n, cands):
    for c in cands:
        if n % c == 0:
            return c
    raise ValueError(f"no tile for {n} in {cands}")


def _sigmoid(x):
    return 1.0 / (1.0 + jnp.exp(-x))


def _rms(x, w):
    return x * lax.rsqrt(jnp.mean(x * x, axis=-1, keepdims=True) + EPS) * w


def _norm_mod(x, nw, mod_ref, row0, n_ctx):
    xn = _rms(x, nw)
    rows = row0 + lax.broadcasted_iota(jnp.int32, (x.shape[0], 1), 0)
    is_ctx = rows < n_ctx
    sc = jnp.where(is_ctx, mod_ref[2:3, :], mod_ref[0:1, :])
    sh = jnp.where(is_ctx, mod_ref[3:4, :], mod_ref[1:2, :])
    return xn * (1.0 + sc) + sh


def _gate_rows(mod_ref, row0, tm, n_ctx):
    rows = row0 + lax.broadcasted_iota(jnp.int32, (tm, 1), 0)
    return jnp.where(rows < n_ctx, mod_ref[5:6, :], mod_ref[4:5, :])


def _mod_kernel(v_ref, w_ref, b_ref, o_ref):
    v = v_ref[...]
    s = v * _sigmoid(v)
    o_ref[...] = jnp.dot(s, w_ref[...], precision=HI, preferred_element_type=F32) + b_ref[...]


def _modulation(vecs, w_mod, b_mod):
    depth, d, n6 = w_mod.shape
    tn = 1024
    return pl.pallas_call(
        _mod_kernel,
        grid=(depth, n6 // tn),
        in_specs=[pl.BlockSpec((8, d), lambda l, j: (0, 0)),
                  pl.BlockSpec((None, d, tn), lambda l, j: (l, 0, j)),
                  pl.BlockSpec((None, 1, tn), lambda l, j: (l, 0, j))],
        out_specs=pl.BlockSpec((None, 8, tn), lambda l, j: (l, 0, j)),
        out_shape=jax.ShapeDtypeStruct((depth, 8, n6), F32),
        compiler_params=_params(("parallel", "parallel"), 40),
        name="modulation",
    )(vecs, w_mod, b_mod.reshape(depth, 1, n6))


def _mod_rows(m, sub):
    d = m.shape[1] // N_MOD
    m6 = m.reshape(8, N_MOD, d)
    sh, sc, g = m6[:, 3 * sub + 0], m6[:, 3 * sub + 1], m6[:, 3 * sub + 2]
    z = jnp.zeros((d,), F32)
    return jnp.stack([sc[0], sh[0], sc[1], sh[1], g[0], g[1], z, z])


def _nmm_kernel(h_ref, nw_ref, mod_ref, w_ref, o_ref, u_ref, *, n_ctx, tm):
    @pl.when(pl.program_id(1) == 0)
    def _():
        u = _norm_mod(h_ref[...], nw_ref[...], mod_ref, pl.program_id(0) * tm, n_ctx)
        u_ref[...] = u.astype(BF16)
    o_ref[...] = jnp.dot(u_ref[...], w_ref[...], preferred_element_type=F32).astype(o_ref.dtype)


def _nmm_aux_kernel(h_ref, nw_ref, mod_ref, w_ref, wa_ref, o_ref, oa_ref, u_ref, *, n_ctx, tm):
    @pl.when(pl.program_id(1) == 0)
    def _():
        u = _norm_mod(h_ref[...], nw_ref[...], mod_ref, pl.program_id(0) * tm, n_ctx)
        ub = u.astype(BF16)
        u_ref[...] = ub
        oa_ref[...] = jnp.dot(ub, wa_ref[...], preferred_element_type=F32)
    o_ref[...] = jnp.dot(u_ref[...], w_ref[...], preferred_element_type=F32).astype(o_ref.dtype)


def _normed_matmul(h, nw, mod, w, n_ctx, w_aux=None):
    n, d = h.shape
    nout = w.shape[1]
    tm = _pick(n, (640, 512, 256, 128))
    tn = _pick(nout, (1536, 1024, 512, 256, 128))
    in_specs = [pl.BlockSpec((tm, d), lambda i, j: (i, 0)),
                pl.BlockSpec((1, d), lambda i, j: (0, 0)),
                pl.BlockSpec((8, d), lambda i, j: (0, 0)),
                pl.BlockSpec((d, tn), lambda i, j: (0, j))]
    out_specs = pl.BlockSpec((tm, tn), lambda i, j: (i, j))
    out_shape = jax.ShapeDtypeStruct((n, nout), BF16)
    args = [h, nw.reshape(1, d), mod, w]
    if w_aux is None:
        kern = functools.partial(_nmm_kernel, n_ctx=n_ctx, tm=tm)
    else:
        na = w_aux.shape[1]
        kern = functools.partial(_nmm_aux_kernel, n_ctx=n_ctx, tm=tm)
        in_specs.append(pl.BlockSpec((d, na), lambda i, j: (0, 0)))
        out_specs = [out_specs, pl.BlockSpec((tm, na), lambda i, j: (i, 0))]
        out_shape = [out_shape, jax.ShapeDtypeStruct((n, na), F32)]
        args.append(w_aux)
    return pl.pallas_call(
        kern, grid=(n // tm, nout // tn), in_specs=in_specs, out_specs=out_specs, out_shape=out_shape,
        scratch_shapes=[pltpu.VMEM((tm, d), BF16)],
        compiler_params=_params(("parallel", "arbitrary"), 48),
        name="normed_proj",
    )(*args)


_POOL_HALO = 16


def _pool_kernel(z_ref, zp_ref, zn_ref, pw_ref, ps_ref, o_ref, ext_ref, *, n_ctx, n_all, tm):
    i = pl.program_id(0)
    row0 = i * tm
    in_ctx = row0 < n_ctx
    seq_start = jnp.where(in_ctx, 0, n_ctx)
    seq_end = jnp.where(in_ctx, n_ctx, n_all)
    has_prev = row0 > seq_start
    has_next = row0 + tm < seq_end
    hl = _POOL_HALO
    ext_ref[0:hl, :] = jnp.where(has_prev, zp_ref[...].astype(F32), 0.0)
    ext_ref[hl:hl + tm, :] = z_ref[...].astype(F32)
    ext_ref[hl + tm:hl + tm + hl, :] = jnp.where(has_next, zn_ref[...].astype(F32), 0.0)
    t = row0 - seq_start + lax.broadcasted_iota(jnp.int32, (tm, 1), 0)
    seq_len = seq_end - seq_start
    for g, w in enumerate(POOL_WINDOWS):
        c0 = g * POOL_GROUP
        acc = None
        for k in range(-(w // 2), w - w // 2):
            v = ext_ref[hl + k:hl + k + tm, c0:c0 + POOL_GROUP]
            acc = v if acc is None else acc + v
        lo = jnp.maximum(t - w // 2, 0)
        hi = jnp.minimum(t + (w - w // 2), seq_len)
        cnt = (hi - lo).astype(F32)
        dlt = acc / cnt - ext_ref[hl:hl + tm, c0:c0 + POOL_GROUP]
        y = jnp.dot(dlt.astype(BF16), pw_ref[g], preferred_element_type=F32)
        o_ref[:, c0:c0 + POOL_GROUP] = (y * ps_ref[:, c0:c0 + POOL_GROUP]).astype(o_ref.dtype)


def _pool_mixer(z, pool_w, pool_scale, n_ctx):
    n = z.shape[0]
    tm = 256
    assert n_ctx % tm == 0 and n % tm == 0
    hl = _POOL_HALO
    nh = n // hl
    per = tm // hl
    kern = functools.partial(_pool_kernel, n_ctx=n_ctx, n_all=n, tm=tm)
    return pl.pallas_call(
        kern, grid=(n // tm,),
        in_specs=[pl.BlockSpec((tm, W_POOL), lambda i: (i, 0)),
                  pl.BlockSpec((hl, W_POOL), lambda i: (jnp.maximum(i * per - 1, 0), 0)),
                  pl.BlockSpec((hl, W_POOL), lambda i: (jnp.minimum((i + 1) * per, nh - 1), 0)),
                  pl.BlockSpec((4, POOL_GROUP, POOL_GROUP), lambda i: (0, 0, 0)),
                  pl.BlockSpec((1, W_POOL), lambda i: (0, 0))],
        out_specs=pl.BlockSpec((tm, W_POOL), lambda i: (i, 0)),
        out_shape=jax.ShapeDtypeStruct((n, W_POOL), BF16),
        scratch_shapes=[pltpu.VMEM((tm + 2 * hl, W_POOL), F32)],
        compiler_params=_params(("parallel",), 32),
        name="pool_mixer",
    )(z, z, z, pool_w, pool_scale.reshape(1, W_POOL))


def _dot_hi(a, b):
    return jnp.dot(a, b, precision=HI, preferred_element_type=F32)


def _dot_t_hi(a, b):
    return lax.dot_general(a, b, (((0,), (0,)), ((), ())), precision=HI, preferred_element_type=F32)


def _s5_param_kernel(lr_ref, li_ref, lrc_ref, lic_ref, ldt_ref, br_ref, bi_ref, cr_ref, ci_ref,
                     m_ref, qre_ref, qim_ref, pre_ref, pim_ref, at_ref):
    t_len = S5_T
    tp = t_len + 8
    wid = t_len * S5_CH
    col = lax.broadcasted_iota(jnp.int32, (tp, wid), 1)
    kid = lax.broadcasted_iota(jnp.int32, (tp, wid), 0)
    cq = col >> 4
    rep_nat = (kid == cq).astype(F32)
    rep_rev = (kid == t_len - 1 - cq).astype(F32)
    til = (lax.broadcasted_iota(jnp.int32, (S5_CH, wid), 0)
           == (lax.broadcasted_iota(jnp.int32, (S5_CH, wid), 1) & (S5_CH - 1))).astype(F32)
    colb = lax.broadcasted_iota(jnp.int32, (S5_CH, wid), 1) >> 4
    kk = lax.broadcasted_iota(jnp.int32, (tp, S5_STATE), 0).astype(F32)
    rows = [None] * t_len
    for d in range(2):
        dt = jnp.exp(ldt_ref[d])
        lr, li = lr_ref[d], li_ref[d]
        mag = jnp.exp(kk * (lr * dt))
        ang = kk * (li * dt)
        pr, pi_ = mag * jnp.cos(ang), mag * jnp.sin(ang)
        lrc, lic = lrc_ref[d], lic_ref[d]
        magc = jnp.exp(lrc * dt)
        arc, aic = magc * jnp.cos(lic * dt), magc * jnp.sin(lic * dt)
        den = lrc * lrc + lic * lic
        nr, ni = arc - 1.0, aic
        kre = (nr * lrc + ni * lic) / den
        kim = (ni * lrc - nr * lic) / den
        bbr = kre * br_ref[d] - kim * bi_ref[d]
        bbi = kre * bi_ref[d] + kim * br_ref[d]
        repq = rep_rev if d == 0 else rep_nat
        repp = ((kid == cq + 1) if d == 0 else (kid == t_len - cq)).astype(F32)
        repm = rep_nat if d == 0 else rep_rev
        bt_r, bt_i = _dot_hi(bbr, til), _dot_hi(bbi, til)
        eqr, eqi = _dot_t_hi(pr, repq), _dot_t_hi(pi_, repq)
        qre_ref[d] = (eqr * bt_r - eqi * bt_i).astype(qre_ref.dtype)
        qim_ref[d] = (eqr * bt_i + eqi * bt_r).astype(qim_ref.dtype)
        epr, epi = _dot_t_hi(pr, repp), _dot_t_hi(pi_, repp)
        ct_r, ct_i = _dot_t_hi(cr_ref[d], til), _dot_t_hi(ci_ref[d], til)
        pre_ref[d] = (ct_r * epr - ct_i * epi).astype(pre_ref.dtype)
        pim_ref[d] = (-(ct_r * epi + ct_i * epr)).astype(pim_ref.dtype)
        at_ref[d, 0:1, :] = pr[t_len:t_len + 1, :]
        at_ref[d, 1:2, :] = pi_[t_len:t_len + 1, :]
        emr, emi = _dot_t_hi(pr, repm), _dot_t_hi(pi_, repm)
        y_re = emr * ct_r - emi * ct_i
        y_im = emr * ct_i + emi * ct_r
        r = _dot_t_hi(bbr, y_re) - _dot_t_hi(bbi, y_im)
        for s in range(t_len):
            if d == 0:
                blk = jnp.where(colb >= s, pltpu.roll(r, S5_CH * s, 1), 0.0)
            else:
                blk = jnp.where(colb <= s, pltpu.roll(r, (S5_CH * (s + 1)) % wid, 1), 0.0)
            rows[s] = blk if rows[s] is None else rows[s] + blk
    for s in range(t_len):
        m_ref[s * S5_CH:(s + 1) * S5_CH, :] = rows[s].astype(m_ref.dtype)


def _s5_params(lam_re, lam_im, log_dt, b_re, b_im, c_re, c_im):
    g, p, n = S5_GROUPS, S5_STATE, S5_CH
    wid = S5_T * S5_CH

    def spec(*shape):
        return pl.BlockSpec((2, None) + shape, lambda gi: (0, gi) + (0,) * len(shape))

    def ospec(*shape):
        return pl.BlockSpec((None,) + shape, lambda gi: (gi,) + (0,) * len(shape))
    return pl.pallas_call(
        _s5_param_kernel, grid=(g,),
        in_specs=[spec(1, p), spec(1, p), spec(p, 1), spec(p, 1), spec(1, 1),
                  spec(p, n), spec(p, n), spec(n, p), spec(n, p)],
        out_specs=[ospec(wid, wid), ospec(2, p, wid), ospec(2, p, wid), ospec(2, p, wid), ospec(2, p, wid),
                   ospec(2, 2, p)],
        out_shape=[jax.ShapeDtypeStruct((g, wid, wid), BF16),
                   jax.ShapeDtypeStruct((g, 2, p, wid), BF16),
                   jax.ShapeDtypeStruct((g, 2, p, wid), BF16),
                   jax.ShapeDtypeStruct((g, 2, p, wid), BF16),
                   jax.ShapeDtypeStruct((g, 2, p, wid), BF16),
                   jax.ShapeDtypeStruct((g, 2, 2, p), F32)],
        compiler_params=_params(("parallel",), 32),
        name="s5_params",
    )(lam_re.reshape(2, g, 1, p), lam_im.reshape(2, g, 1, p), lam_re.reshape(2, g, p, 1),
      lam_im.reshape(2, g, p, 1), log_dt.reshape(2, g, 1, 1), b_re, b_im, c_re, c_im)


def _s5_state_kernel(u_ref, qre_ref, qim_ref, s_ref):
    u = u_ref[...]
    dn = (((1,), (1,)), ((), ()))
    s_ref[:, 0:128] = lax.dot_general(u, qre_ref[...], dn, preferred_element_type=F32)
    s_ref[:, 128:256] = lax.dot_general(u, qim_ref[...], dn, preferred_element_type=F32)


def _s5_scan_kernel(s_ref, are_ref, aim_ref, h_ref, *, nc, ncc):
    ar, ai = are_ref[...], aim_ref[...]
    is_f = lax.broadcasted_iota(jnp.int32, ar.shape, 1) < S5_STATE

    def body(i, carry):
        hr, hi = carry
        cf = i
        cb = jnp.where(i < ncc, ncc - 1 - i, nc + ncc - 1 - i)
        h_ref[cf, :, 0:64] = hr[:, 0:64]
        h_ref[cb, :, 64:128] = hr[:, 64:128]
        h_ref[cf, :, 128:192] = hi[:, 0:64]
        h_ref[cb, :, 192:256] = hi[:, 64:128]
        sf, sb = s_ref[cf], s_ref[cb]
        sr = jnp.where(is_f, sf[:, 0:128], sb[:, 0:128])
        si = jnp.where(is_f, sf[:, 128:256], sb[:, 128:256])
        return ar * hr - ai * hi + sr, ar * hi + ai * hr + si

    zero = jnp.zeros(ar.shape, F32)
    lax.fori_loop(0, nc, body, (zero, zero))


def _s5_out_kernel(u_ref, m_ref, h_ref, p_ref, y_ref):
    y = jnp.dot(u_ref[...], m_ref[...], preferred_element_type=F32)
    y_ref[...] = y + jnp.dot(h_ref[...].astype(BF16), p_ref[...], preferred_element_type=F32)


def _s5_mixer(s_all, m, qre, qim, pre, pim, at, n_ctx):
    n = s_all.shape[0]
    g, t_len, wid = S5_GROUPS, S5_T, S5_T * S5_CH
    nc, ncc = n // t_len, n_ctx // t_len
    assert n % t_len == 0 and n_ctx % t_len == 0
    u = s_all.reshape(nc, t_len, g, S5_CH).transpose(2, 0, 1, 3).reshape(g, nc, wid)
    p_all = jnp.concatenate([pre.reshape(g, 128, wid), pim.reshape(g, 128, wid)], axis=1)
    a_re = at[:, :, 0, :].reshape(g, 128)
    a_im = at[:, :, 1, :].reshape(g, 128)
    s_t = pl.pallas_call(
        _s5_state_kernel, grid=(g,),
        in_specs=[pl.BlockSpec((None, nc, wid), lambda i: (i, 0, 0)),
                  pl.BlockSpec((None, 128, wid), lambda i: (i, 0, 0)),
                  pl.BlockSpec((None, 128, wid), lambda i: (i, 0, 0))],
        out_specs=pl.BlockSpec((nc, 256), lambda i: (0, i)),
        out_shape=jax.ShapeDtypeStruct((nc, g * 256), F32),
        compiler_params=_params(("parallel",), 32),
        name="s5_chunk_state",
    )(u, qre.reshape(g, 128, wid), qim.reshape(g, 128, wid))
    h_t = pl.pallas_call(
        functools.partial(_s5_scan_kernel, nc=nc, ncc=ncc),
        out_shape=jax.ShapeDtypeStruct((nc, g, 256), F32),
        compiler_params=pltpu.CompilerParams(vmem_limit_bytes=48 * MIB),
        name="s5_chunk_scan",
    )(s_t.reshape(nc, g, 256), a_re, a_im).reshape(nc, g * 256)
    y = pl.pallas_call(
        _s5_out_kernel, grid=(g,),
        in_specs=[pl.BlockSpec((None, nc, wid), lambda i: (i, 0, 0)),
                  pl.BlockSpec((None, wid, wid), lambda i: (i, 0, 0)),
                  pl.BlockSpec((nc, 256), lambda i: (0, i)),
                  pl.BlockSpec((None, 256, wid), lambda i: (i, 0, 0))],
        out_specs=pl.BlockSpec((None, nc, wid), lambda i: (i, 0, 0)),
        out_shape=jax.ShapeDtypeStruct((g, nc, wid), F32),
        compiler_params=_params(("parallel",), 32),
        name="s5_chunk_out",
    )(u, m, h_t, p_all)
    return y.reshape(g, nc, t_len, S5_CH).transpose(1, 2, 0, 3).reshape(n, W_S5)


def _gelu_tanh(x):
    return 0.5 * x * (1.0 + jnp.tanh(math.sqrt(2.0 / math.pi) * (x + 0.044715 * (x * x * x))))


def _even_out_kernel(pool_ref, y_ref, s_ref, dsk_ref, wglu_ref, wo_ref, h_ref, nw_ref, mod_ref, o_ref,
                     *, n_ctx, tm):
    y = _gelu_tanh(y_ref[...] + s_ref[...].astype(F32) * dsk_ref[...])
    gate = jnp.dot(y.astype(BF16), wglu_ref[...], preferred_element_type=F32)
    s5 = (y * _sigmoid(gate)).astype(BF16)
    mix = jnp.dot(pool_ref[...], wo_ref[0:W_POOL, :], preferred_element_type=F32)
    mix = mix + jnp.dot(s5, wo_ref[W_POOL:W_POOL + W_S5, :], preferred_element_type=F32)
    g = _gate_rows(mod_ref, pl.program_id(0) * tm, tm, n_ctx)
    o_ref[...] = h_ref[...] + g * _rms(mix, nw_ref[...])


def _even_out(pool_out, y_s5, z, dsk, w_glu, w_out, h, nw, mod, n_ctx):
    n, d = h.shape
    tm = _pick(n, (320, 256, 128))
    return pl.pallas_call(
        functools.partial(_even_out_kernel, n_ctx=n_ctx, tm=tm), grid=(n // tm,),
        in_specs=[pl.BlockSpec((tm, W_POOL), lambda i: (i, 0)),
                  pl.BlockSpec((tm, W_S5), lambda i: (i, 0)),
                  pl.BlockSpec((tm, W_S5), lambda i: (i, W_POOL // W_S5)),
                  pl.BlockSpec((1, W_S5), lambda i: (0, 0)),
                  pl.BlockSpec((W_S5, W_S5), lambda i: (0, 0)),
                  pl.BlockSpec((d, d), lambda i: (0, 0)),
                  pl.BlockSpec((tm, d), lambda i: (i, 0)),
                  pl.BlockSpec((1, d), lambda i: (0, 0)),
                  pl.BlockSpec((8, d), lambda i: (0, 0))],
        out_specs=pl.BlockSpec((tm, d), lambda i: (i, 0)),
        out_shape=jax.ShapeDtypeStruct((n, d), F32),
        compiler_params=_params(("parallel",), 48),
        name="even_mixer_out",
    )(pool_out, y_s5, z, dsk.reshape(1, W_S5), w_glu, w_out, h, nw.reshape(1, d), mod)


def _ffn_kernel(h_ref, nw2_ref, nw3_ref, mod_ref, wg_ref, wu_ref, wd_ref, o_ref, v_ref, *, n_ctx, tm):
    f = pl.program_id(1)

    @pl.when(f == 0)
    def _():
        v = _norm_mod(h_ref[...], nw2_ref[...], mod_ref, pl.program_id(0) * tm, n_ctx)
        v_ref[...] = v.astype(BF16)
        o_ref[...] = jnp.zeros_like(o_ref)
    v = v_ref[...]
    a = jnp.dot(v, wg_ref[...], preferred_element_type=F32)
    b = jnp.dot(v, wu_ref[...], preferred_element_type=F32)
    act = (a * _sigmoid(a) * b).astype(BF16)
    o_ref[...] += jnp.dot(act, wd_ref[...], preferred_element_type=F32)

    @pl.when(f == pl.num_programs(1) - 1)
    def _():
        g = _gate_rows(mod_ref, pl.program_id(0) * tm, tm, n_ctx)
        o_ref[...] = h_ref[...] + g * _rms(o_ref[...], nw3_ref[...])


def _dense_ffn(h, nw2, nw3, mod, wg, wu, wd, n_ctx):
    n, d = h.shape
    dff = wg.shape[1]
    tm = _pick(n, (640, 512, 256, 128))
    tf = 512
    return pl.pallas_call(
        functools.partial(_ffn_kernel, n_ctx=n_ctx, tm=tm), grid=(n // tm, dff // tf),
        in_specs=[pl.BlockSpec((tm, d), lambda i, f: (i, 0)),
                  pl.BlockSpec((1, d), lambda i, f: (0, 0)),
                  pl.BlockSpec((1, d), lambda i, f: (0, 0)),
                  pl.BlockSpec((8, d), lambda i, f: (0, 0)),
                  pl.BlockSpec((d, tf), lambda i, f: (0, f)),
                  pl.BlockSpec((d, tf), lambda i, f: (0, f)),
                  pl.BlockSpec((tf, d), lambda i, f: (f, 0))],
        out_specs=pl.BlockSpec((tm, d), lambda i, f: (i, 0)),
        out_shape=jax.ShapeDtypeStruct((n, d), F32),
        scratch_shapes=[pltpu.VMEM((tm, d), BF16)],
        compiler_params=_params(("parallel", "arbitrary"), 56),
        name="dense_swiglu",
    )(h, nw2.reshape(1, d), nw3.reshape(1, d), mod, wg, wu, wd)


def _split_bf16(x):
    hi = x.astype(BF16)
    return hi, (x - hi.astype(F32)).astype(BF16)


def _gla_kernel(qf_ref, kf_ref, vf_ref, af_ref, qb_ref, kb_ref, vb_ref, ab_ref, wa_ref, ba_ref,
                of_ref, ob_ref, st_ref, *, tb):
    c = GLA_CHUNK
    nsub = tb // c

    @pl.when(pl.program_id(0) == 0)
    def _():
        st_ref[...] = jnp.zeros_like(st_ref)
    ri = lax.broadcasted_iota(jnp.int32, (c, c), 0)
    ci = lax.broadcasted_iota(jnp.int32, (c, c), 1)
    scale = GLA_DK ** -0.5
    nt = (((1,), (1,)), ((), ()))
    tn = (((0,), (0,)), ((), ()))
    dirs = ((qf_ref, kf_ref, vf_ref, af_ref, of_ref), (qb_ref, kb_ref, vb_ref, ab_ref, ob_ref))
    for d, (q_ref, k_ref, v_ref, a_ref, o_ref) in enumerate(dirs):
        mask = (ri >= ci) if d == 0 else (ci >= ri)
        cum = mask.astype(BF16)
        a_hi, a_lo = _split_bf16(a_ref[...])
        w_hi, w_lo = _split_bf16(wa_ref[d])
        z = (jnp.dot(a_hi, w_hi, preferred_element_type=F32) + jnp.dot(a_hi, w_lo, preferred_element_type=F32)
             + jnp.dot(a_lo, w_hi, preferred_element_type=F32)) + ba_ref[d]
        glog = (jnp.minimum(z, 0.0) - jnp.log(1.0 + jnp.exp(-jnp.abs(z)))) * (1.0 / GLA_TAU)
        order = range(nsub) if d == 0 else range(nsub - 1, -1, -1)
        for sc in order:
            r0 = sc * c
            g_hi, g_lo = _split_bf16(glog[r0:r0 + c, :])
            b = jnp.dot(cum, g_hi, preferred_element_type=F32) + jnp.dot(cum, g_lo, preferred_element_type=F32)
            b_last = b[c - 1:c, :] if d == 0 else b[0:1, :]
            dec = jnp.exp(b_last)
            q = q_ref[r0:r0 + c, :].astype(F32) * scale
            k = k_ref[r0:r0 + c, :].astype(F32)
            qe = (q * jnp.exp(b)).astype(BF16)
            ke = (k * jnp.exp(-b)).astype(BF16)
            kd = (k * jnp.exp(b_last - b)).astype(BF16)
            for h in range(GLA_HEADS):
                ks = slice(h * GLA_DK, (h + 1) * GLA_DK)
                vs = slice(h * GLA_DV, (h + 1) * GLA_DV)
                att = lax.dot_general(qe[:, ks], ke[:, ks], nt, preferred_element_type=F32)
                att = jnp.where(mask, att, 0.0).astype(BF16)
                vh = v_ref[r0:r0 + c, vs]
                s_old = st_ref[d, h]
                o = jnp.dot(att, vh, preferred_element_type=F32)
                o = o + lax.dot_general(qe[:, ks], s_old.astype(BF16), nt, preferred_element_type=F32)
                o_ref[r0:r0 + c, vs] = o
                upd = lax.dot_general(vh, kd[:, ks], tn, preferred_element_type=F32)
                st_ref[d, h] = s_old * dec[:, ks] + upd


def _gla_mixer(zmain, a_aux, w_a, b_a, n_ctx):
    n = zmain.shape[0]
    tb = 256
    assert n % tb == 0 and n_ctx % tb == 0
    nb, ncb = n // tb, n_ctx // tb

    def bwd(s):
        return jnp.where(s < ncb, ncb - 1 - s, nb + ncb - 1 - s)
    wa = jnp.pad(w_a, ((0, 0), (0, 128 - GLA_RANK), (0, 0)))
    in_specs = []
    for order in (lambda s: s, bwd):
        in_specs += [pl.BlockSpec((tb, W_GLA_K), lambda s, o=order: (o(s), OQ // W_GLA_K)),
                     pl.BlockSpec((tb, W_GLA_K), lambda s, o=order: (o(s), OK_ // W_GLA_K)),
                     pl.BlockSpec((tb, W_GLA_V), lambda s, o=order: (o(s), OV // W_GLA_V)),
                     pl.BlockSpec((tb, 128), lambda s, o=order: (o(s), 0))]
    in_specs += [pl.BlockSpec((2, 128, W_GLA_K), lambda s: (0, 0, 0)),
                 pl.BlockSpec((2, 1, W_GLA_K), lambda s: (0, 0, 0))]
    return pl.pallas_call(
        functools.partial(_gla_kernel, tb=tb), grid=(nb,),
        in_specs=in_specs,
        out_specs=[pl.BlockSpec((tb, W_GLA_V), lambda s: (s, 0)),
                   pl.BlockSpec((tb, W_GLA_V), lambda s: (bwd(s), 0))],
        out_shape=[jax.ShapeDtypeStruct((n, W_GLA_V), F32)] * 2,
        scratch_shapes=[pltpu.VMEM((2, GLA_HEADS, GLA_DV, GLA_DK), F32)],
        compiler_params=_params(("arbitrary",), 32),
        name="gla_chunked",
    )(zmain, zmain, zmain, a_aux, zmain, zmain, zmain, a_aux, wa, b_a.reshape(2, 1, W_GLA_K))


_ATT_PAD = 2 * ATT_HEAD_DIM
_LOG2E = 1.4426950408889634
_FIXED_SHIFT_MAX_BOUND = 40.0


def _rope_fn(row0, tm, n_ctx):
    hd = ATT_HEAD_DIM
    row = row0 + lax.broadcasted_iota(jnp.int32, (tm, hd), 0)
    lane = lax.broadcasted_iota(jnp.int32, (tm, hd), 1)
    t = row - n_ctx
    pos = jnp.where(lane < hd // 2, t // GRID_W, t % GRID_W).astype(F32)
    quarter = hd // 4
    freq = jnp.exp((lane % quarter).astype(F32) * (-math.log(ROPE_THETA) / quarter))
    ang = pos * freq
    first = (lane & quarter) == 0
    cosv = jnp.cos(ang)
    sin_s = jnp.where(first, -jnp.sin(ang), jnp.sin(ang))

    def rope(x):
        sw = jnp.where(first, pltpu.roll(x, hd - quarter, 1), pltpu.roll(x, quarter, 1))
        return x * cosv + sw * sin_s
    return rope, row >= n_ctx


def _k_prep_kernel(k_ref, v_ref, kw_ref, kx_ref, vx_ref, kmax_ref, *, n_ctx, tm):
    hd = ATT_HEAD_DIM
    rope, is_lat = _rope_fn(pl.program_id(0) * tm, tm, n_ctx)
    lane = lax.broadcasted_iota(jnp.int32, (tm, hd), 1)
    one_col = jnp.where(lane == 0, 1.0, 0.0).astype(BF16)
    ones = jnp.ones((tm, hd), BF16)
    nmax = jnp.zeros((tm, 1), F32)
    for h in range(ATT_KV_HEADS):
        sl = slice(h * hd, (h + 1) * hd)
        xn = _rms(k_ref[:, sl].astype(F32), kw_ref[...])
        kr = jnp.where(is_lat, rope(xn), xn).astype(BF16)
        kx_ref[:, h * _ATT_PAD:h * _ATT_PAD + hd] = kr
        kx_ref[:, h * _ATT_PAD + hd:(h + 1) * _ATT_PAD] = one_col
        vx_ref[:, h * _ATT_PAD:h * _ATT_PAD + hd] = v_ref[:, sl]
        vx_ref[:, h * _ATT_PAD + hd:(h + 1) * _ATT_PAD] = ones
        krf = kr.astype(F32)
        nmax = jnp.maximum(nmax, jnp.sum(krf * krf, axis=-1, keepdims=True))
    kmax_ref[...] = jnp.broadcast_to(jnp.max(nmax, axis=0, keepdims=True), kmax_ref.shape)


def _q_prep_kernel(q_ref, qw_ref, kn_ref, qx_ref, bmax_ref, *, n_ctx, tm):
    hd = ATT_HEAD_DIM
    rope, _ = _rope_fn(n_ctx + pl.program_id(0) * tm, tm, n_ctx)
    lane = lax.broadcasted_iota(jnp.int32, (tm, hd), 1)
    scale = hd ** -0.5 * _LOG2E
    bmax = jnp.zeros((tm, 1), F32)
    for h in range(ATT_HEADS):
        sl = slice(h * hd, (h + 1) * hd)
        xn = _rms(q_ref[:, sl].astype(F32), qw_ref[...])
        qr = (rope(xn) * scale).astype(BF16)
        qf = qr.astype(F32)
        bound = jnp.sqrt(jnp.sum(qf * qf, axis=-1, keepdims=True)) * kn_ref[...]
        qx_ref[:, h * _ATT_PAD:h * _ATT_PAD + hd] = qr
        qx_ref[:, h * _ATT_PAD + hd:(h + 1) * _ATT_PAD] = jnp.where(lane == 0, -bound, 0.0).astype(BF16)
        bmax = jnp.maximum(bmax, bound)
    bmax_ref[...] = jnp.broadcast_to(jnp.max(bmax, axis=0, keepdims=True), bmax_ref.shape)


def _qk_prep(zmain, q_norm, k_norm, n_ctx):
    n = zmain.shape[0]
    lq = n - n_ctx
    tm = 256
    hd = ATT_HEAD_DIM
    assert n_ctx % tm == 0
    ncb = n_ctx // tm
    kx, vx, kmax = pl.pallas_call(
        functools.partial(_k_prep_kernel, n_ctx=n_ctx, tm=tm), grid=(n // tm,),
        in_specs=[pl.BlockSpec((tm, W_ATT_KV), lambda i: (i, OAK // W_ATT_KV)),
                  pl.BlockSpec((tm, W_ATT_KV), lambda i: (i, OAV // W_ATT_KV)),
                  pl.BlockSpec((1, hd), lambda i: (0, 0))],
        out_specs=[pl.BlockSpec((tm, ATT_KV_HEADS * _ATT_PAD), lambda i: (i, 0)),
                   pl.BlockSpec((tm, ATT_KV_HEADS * _ATT_PAD), lambda i: (i, 0)),
                   pl.BlockSpec((None, 8, 128), lambda i: (i, 0, 0))],
        out_shape=[jax.ShapeDtypeStruct((n, ATT_KV_HEADS * _ATT_PAD), BF16),
                   jax.ShapeDtypeStruct((n, ATT_KV_HEADS * _ATT_PAD), BF16),
                   jax.ShapeDtypeStruct((n // tm, 8, 128), F32)],
        compiler_params=_params(("parallel",), 32),
        name="k_norm_rope",
    )(zmain, zmain, k_norm.reshape(1, hd))
    knorm = jnp.sqrt(jnp.max(kmax)).reshape(1, 1)
    qx, bmax = pl.pallas_call(
        functools.partial(_q_prep_kernel, n_ctx=n_ctx, tm=tm), grid=(lq // tm,),
        in_specs=[pl.BlockSpec((tm, W_ATT), lambda i: (i + ncb, OAQ // W_ATT)),
                  pl.BlockSpec((1, hd), lambda i: (0, 0)),
                  pl.BlockSpec((1, 1), lambda i: (0, 0))],
        out_specs=[pl.BlockSpec((tm, ATT_HEADS * _ATT_PAD), lambda i: (i, 0)),
                   pl.BlockSpec((None, 8, 128), lambda i: (i, 0, 0))],
        out_shape=[jax.ShapeDtypeStruct((lq, ATT_HEADS * _ATT_PAD), BF16),
                   jax.ShapeDtypeStruct((lq // tm, 8, 128), F32)],
        compiler_params=_params(("parallel",), 32),
        name="q_norm_rope",
    )(zmain, q_norm.reshape(1, hd), knorm)
    return qx, kx, vx, jnp.max(bmax)


def _flash_fixed_kernel(q_ref, kt_ref, v_ref, o_ref, acc_ref):
    c = pl.program_id(2)
    hd = ATT_HEAD_DIM

    @pl.when(c == 0)
    def _():
        acc_ref[...] = jnp.zeros_like(acc_ref)
    kt = kt_ref[...]
    v = v_ref[...]
    for g in range(ATT_GROUP):
        s = jnp.dot(q_ref[:, g * _ATT_PAD:(g + 1) * _ATT_PAD], kt, preferred_element_type=F32)
        acc_ref[g] += jnp.dot(jnp.exp2(s).astype(BF16), v, preferred_element_type=F32)

    @pl.when(c == pl.num_programs(2) - 1)
    def _():
        for g in range(ATT_GROUP):
            a = acc_ref[g]
            o_ref[:, g * hd:(g + 1) * hd] = (a[:, 0:hd] / a[:, hd:2 * hd]).astype(o_ref.dtype)


def _flash_online_kernel(q_ref, kt_ref, v_ref, o_ref, m_ref, acc_ref):
    c = pl.program_id(2)
    hd = ATT_HEAD_DIM

    @pl.when(c == 0)
    def _():
        m_ref[...] = jnp.full(m_ref.shape, -jnp.inf, F32)
        acc_ref[...] = jnp.zeros_like(acc_ref)
    kt = kt_ref[...]
    v = v_ref[...]
    for g in range(ATT_GROUP):
        s = jnp.dot(q_ref[:, g * _ATT_PAD:(g + 1) * _ATT_PAD], kt, preferred_element_type=F32)
        m_prev = m_ref[g]
        m_new = jnp.maximum(m_prev, jnp.max(s, axis=-1, keepdims=True))
        p = jnp.exp2(s - m_new).astype(BF16)
        acc_ref[g] = jnp.exp2(m_prev - m_new) * acc_ref[g] + jnp.dot(p, v, preferred_element_type=F32)
        m_ref[g] = m_new

    @pl.when(c == pl.num_programs(2) - 1)
    def _():
        for g in range(ATT_GROUP):
            a = acc_ref[g]
            o_ref[:, g * hd:(g + 1) * hd] = (a[:, 0:hd] / a[:, hd:2 * hd]).astype(o_ref.dtype)


def _attention(qx, kx, vx, bound_log2):
    lq = qx.shape[0]
    n = kx.shape[0]
    kt = kx.T
    gw = ATT_GROUP * _ATT_PAD
    ow = ATT_GROUP * ATT_HEAD_DIM
    tq = _pick(lq, (512, 256, 128))

    def call(kern, tk, scratch, name):
        return pl.pallas_call(
            kern, grid=(ATT_KV_HEADS, lq // tq, n // tk),
            in_specs=[pl.BlockSpec((tq, gw), lambda h, i, j: (i, h)),
                      pl.BlockSpec((_ATT_PAD, tk), lambda h, i, j: (h, j)),
                      pl.BlockSpec((tk, _ATT_PAD), lambda h, i, j: (j, h))],
            out_specs=pl.BlockSpec((tq, ow), lambda h, i, j: (i, h)),
            out_shape=jax.ShapeDtypeStruct((lq, W_ATT), BF16),
            scratch_shapes=scratch + [pltpu.VMEM((ATT_GROUP, tq, _ATT_PAD), F32)],
            compiler_params=_params(("parallel", "parallel", "arbitrary"), 48),
            name=name,
        )(qx, kt, vx)

    def fixed(_):
        return call(_flash_fixed_kernel, _pick(n, (3328, 1280, 512, 256)), [], "gqa_flash_fixed_shift")

    def online(_):
        return call(_flash_online_kernel, _pick(n, (640, 512, 256, 128)),
                    [pltpu.VMEM((ATT_GROUP, tq, 1), F32)], "gqa_flash_online")
    return lax.cond(bound_log2 <= _FIXED_SHIFT_MAX_BOUND * _LOG2E, fixed, online, None)


def _odd_out_kernel(of_ref, ob_ref, r_ref, att_ref, gn_ref, wo_ref, h_ref, nw_ref, mod_ref, o_ref):
    o = of_ref[...] + ob_ref[...]
    r = r_ref[...].astype(F32)
    silu_r = r * _sigmoid(r)
    parts = []
    for h in range(GLA_HEADS):
        vs = slice(h * GLA_DV, (h + 1) * GLA_DV)
        parts.append((_rms(o[:, vs], gn_ref[:, vs]) * silu_r[:, vs]).astype(BF16))
    gla = jnp.concatenate(parts, axis=1)
    mix = jnp.dot(gla, wo_ref[0:W_GLA_V, :], preferred_element_type=F32)
    mix = mix + jnp.dot(att_ref[...], wo_ref[W_GLA_V:W_GLA_V + W_ATT, :], preferred_element_type=F32)
    o_ref[...] = h_ref[...] + mod_ref[4:5, :] * _rms(mix, nw_ref[...])


def _odd_out(o_f, o_b, zmain, att, gla_norm, w_out, h_all, nw, mod, n_ctx):
    n, d = h_all.shape
    lq = n - n_ctx
    tm = _pick(lq, (256, 128))
    assert n_ctx % tm == 0
    off = n_ctx // tm
    return pl.pallas_call(
        _odd_out_kernel, grid=(lq // tm,),
        in_specs=[pl.BlockSpec((tm, W_GLA_V), lambda i: (i + off, 0)),
                  pl.BlockSpec((tm, W_GLA_V), lambda i: (i + off, 0)),
                  pl.BlockSpec((tm, W_GLA_V), lambda i: (i + off, OR // W_GLA_V)),
                  pl.BlockSpec((tm, W_ATT), lambda i: (i, 0)),
                  pl.BlockSpec((1, W_GLA_V), lambda i: (0, 0)),
                  pl.BlockSpec((d, d), lambda i: (0, 0)),
                  pl.BlockSpec((tm, d), lambda i: (i + off, 0)),
                  pl.BlockSpec((1, d), lambda i: (0, 0)),
                  pl.BlockSpec((8, d), lambda i: (0, 0))],
        out_specs=pl.BlockSpec((tm, d), lambda i: (i, 0)),
        out_shape=jax.ShapeDtypeStruct((lq, d), F32),
        compiler_params=_params(("parallel",), 48),
        name="odd_mixer_out",
    )(o_f, o_b, zmain, att, gla_norm.reshape(1, W_GLA_V), w_out, h_all, nw.reshape(1, d), mod)


def _router_kernel(h_ref, nw_ref, mod_ref, wr_ref, br_ref, v_ref, r_ref):
    x = h_ref[...]
    v = _rms(x, nw_ref[...]) * (1.0 + mod_ref[0:1, :]) + mod_ref[1:2, :]
    v_ref[...] = v.astype(BF16)
    logits = _dot_hi(v, wr_ref[...]) + br_ref[...]
    lane = lax.broadcasted_iota(jnp.int32, logits.shape, 1)
    m1 = jnp.max(logits, axis=-1, keepdims=True)
    i1 = jnp.min(jnp.where(logits == m1, lane, 128), axis=-1, keepdims=True)
    rest = jnp.where(lane == i1, -jnp.inf, logits)
    m2 = jnp.max(rest, axis=-1, keepdims=True)
    i2 = jnp.min(jnp.where(rest == m2, lane, 128), axis=-1, keepdims=True)
    e2 = jnp.exp(m2 - m1)
    g1 = 1.0 / (1.0 + e2)
    g2 = e2 / (1.0 + e2)
    out = jnp.where(lane == 0, i1.astype(F32), 0.0)
    out = jnp.where(lane == 1, i2.astype(F32), out)
    out = jnp.where(lane == 2, g1, out)
    out = jnp.where(lane == 3, g2, out)
    r_ref[...] = out


def _router(h_lat, nw, mod, w_router, b_router):
    lq, d = h_lat.shape
    tm = _pick(lq, (512, 256, 128))
    wr = jnp.pad(w_router, ((0, 0), (0, 128 - N_EXPERTS)))
    br = jnp.pad(b_router, (0, 128 - N_EXPERTS), constant_values=-1e30).reshape(1, 128)
    return pl.pallas_call(
        _router_kernel, grid=(lq // tm,),
        in_specs=[pl.BlockSpec((tm, d), lambda i: (i, 0)),
                  pl.BlockSpec((1, d), lambda i: (0, 0)),
                  pl.BlockSpec((8, d), lambda i: (0, 0)),
                  pl.BlockSpec((d, 128), lambda i: (0, 0)),
                  pl.BlockSpec((1, 128), lambda i: (0, 0))],
        out_specs=[pl.BlockSpec((tm, d), lambda i: (i, 0)), pl.BlockSpec((tm, 128), lambda i: (i, 0))],
        out_shape=[jax.ShapeDtypeStruct((lq, d), BF16), jax.ShapeDtypeStruct((lq, 128), F32)],
        compiler_params=_params(("parallel",), 32),
        name="router_top2",
    )(h_lat, nw.reshape(1, d), mod, wr, br)


def _dispatch_kernel(j_ref, b_ref, fl_ref, tok_ref, v_ref, o_ref, *, sblk):
    s = pl.program_id(0)
    fl = fl_ref[s]

    @pl.when((fl & 2) != 0)
    def _():
        o_ref[...] = jnp.zeros_like(o_ref)

    @pl.when((fl & 1) != 0)
    def _():
        cols = b_ref[s] * sblk + lax.broadcasted_iota(jnp.int32, (1, sblk), 1)
        onehot = (tok_ref[...] == cols).astype(BF16)
        o_ref[...] += jnp.dot(onehot, v_ref[...], preferred_element_type=F32).astype(o_ref.dtype)


def _expert_kernel(te_ref, ts_ref, tv_ref, x_ref, gt_ref, wg_ref, wu_ref, wd_ref, o_ref, acc_ref):
    j = pl.program_id(0)
    f = pl.program_id(1)
    valid = tv_ref[j] == 1

    @pl.when(f == 0)
    def _():
        acc_ref[...] = jnp.zeros_like(acc_ref)

    @pl.when(valid)
    def _():
        x = x_ref[...]
        a = jnp.dot(x, wg_ref[...], preferred_element_type=F32)
        b = jnp.dot(x, wu_ref[...], preferred_element_type=F32)
        act = (a * _sigmoid(a) * b).astype(BF16)
        acc_ref[...] += jnp.dot(act, wd_ref[...], preferred_element_type=F32)

    @pl.when(f == pl.num_programs(1) - 1)
    def _():
        o_ref[...] = (acc_ref[...] * gt_ref[...]).astype(o_ref.dtype)


def _combine_kernel(ci_ref, cy_ref, fl_ref, tok_ref, y_ref, h_ref, nw_ref, mod_ref, o_ref, acc_ref, *, sblk):
    s = pl.program_id(0)
    fl = fl_ref[s]

    @pl.when((fl & 2) != 0)
    def _():
        acc_ref[...] = jnp.zeros_like(acc_ref)

    @pl.when((fl & 1) != 0)
    def _():
        rows = ci_ref[s] * sblk + lax.broadcasted_iota(jnp.int32, (sblk, 1), 0)
        onehot = (tok_ref[...] == rows).astype(BF16)
        acc_ref[...] += jnp.dot(onehot, y_ref[...], preferred_element_type=F32)

    @pl.when((fl & 4) != 0)
    def _():
        o_ref[...] = h_ref[...] + mod_ref[4:5, :] * _rms(acc_ref[...], nw_ref[...])


def _count_le(sorted_vals, queries):
    return jnp.sum((sorted_vals[None, :] <= queries[:, None]).astype(jnp.int32), axis=1)


def _moe(h_lat, nw2, nw3, mod, w_router, b_router, wg, wu, wd):
    lq, d = h_lat.shape
    e = N_EXPERTS
    dff = wg.shape[2]
    tm = _pick(lq, (512, 256))
    tf = 1024
    v, route = _router(h_lat, nw2, mod, w_router, b_router)
    idx = route[:, 0:2].astype(jnp.int32)
    gates = route[:, 2:4]

    npairs = 2 * lq
    nt = npairs // tm + e
    nblk = lq // tm
    flat_e = idx.reshape(-1)
    tok = jnp.arange(npairs, dtype=jnp.int32) // 2
    onehot = (flat_e[:, None] == jnp.arange(e, dtype=jnp.int32)[None, :]).astype(jnp.int32)
    csum = jnp.cumsum(onehot, axis=0)
    rank = jnp.sum(csum * onehot, axis=1) - 1
    counts = csum[-1]
    padded = ((counts + tm - 1) // tm) * tm
    ends = jnp.cumsum(padded)
    starts = ends - padded
    pos = jnp.sum(onehot * starts[None, :], axis=1) + rank
    pair = jnp.stack([tok, lax.bitcast_convert_type(gates.reshape(-1), jnp.int32)], axis=1)
    empty = jnp.broadcast_to(jnp.array([-1, 0], jnp.int32), (nt * tm, 2))
    slots = empty.at[pos].set(pair)
    sorted_tok = slots[:, 0]
    sorted_gate = lax.bitcast_convert_type(slots[:, 1], F32)
    n_used = ends[-1] // tm
    tile_ids = jnp.arange(nt, dtype=jnp.int32)
    tile_valid = (tile_ids < n_used).astype(jnp.int32)
    tile_src = jnp.minimum(tile_ids, n_used - 1).astype(jnp.int32)
    tile_exp = jnp.minimum(_count_le(ends, tile_src * tm), e - 1)

    st2 = sorted_tok.reshape(nt, tm)
    blo = st2[:, 0] // tm
    bhi = jnp.max(st2, axis=1) // tm
    nb = jnp.where(tile_valid == 1, bhi - blo + 1, 0)
    cs = jnp.cumsum(nb)
    w_d = nt + e * (nblk - 1)
    sidx = jnp.arange(w_d, dtype=jnp.int32)
    sc = jnp.minimum(sidx, cs[-1] - 1)
    dj = _count_le(cs, sc)
    dstart = cs[dj] - nb[dj]
    db = (blo[dj] + sc - dstart).astype(jnp.int32)
    dact = sidx < cs[-1]
    dfl = (dact.astype(jnp.int32) + 2 * (dact & (sc == dstart)).astype(jnp.int32)).astype(jnp.int32)

    xs = pl.pallas_call(
        functools.partial(_dispatch_kernel, sblk=tm),
        grid_spec=pltpu.PrefetchScalarGridSpec(
            num_scalar_prefetch=3, grid=(w_d,),
            in_specs=[pl.BlockSpec((tm, 1), lambda s, j, b, fl: (j[s], 0)),
                      pl.BlockSpec((tm, d), lambda s, j, b, fl: (b[s], 0))],
            out_specs=pl.BlockSpec((tm, d), lambda s, j, b, fl: (j[s], 0))),
        out_shape=jax.ShapeDtypeStruct((nt * tm, d), BF16),
        compiler_params=_params(("arbitrary",), 32),
        name="moe_dispatch",
    )(dj, db, dfl, sorted_tok.reshape(nt * tm, 1), v)

    nf = dff // tf
    ys = pl.pallas_call(
        _expert_kernel,
        grid_spec=pltpu.PrefetchScalarGridSpec(
            num_scalar_prefetch=3, grid=(nt, nf),
            in_specs=[pl.BlockSpec((tm, d), lambda j, f, te, ts, tv: (ts[j], 0)),
                      pl.BlockSpec((tm, 1), lambda j, f, te, ts, tv: (ts[j], 0)),
                      pl.BlockSpec((None, d, tf), lambda j, f, te, ts, tv: (te[j], 0, jnp.where(tv[j] == 1, f, nf - 1))),
                      pl.BlockSpec((None, d, tf), lambda j, f, te, ts, tv: (te[j], 0, jnp.where(tv[j] == 1, f, nf - 1))),
                      pl.BlockSpec((None, tf, d), lambda j, f, te, ts, tv: (te[j], jnp.where(tv[j] == 1, f, nf - 1), 0))],
            out_specs=pl.BlockSpec((tm, d), lambda j, f, te, ts, tv: (j, 0)),
            scratch_shapes=[pltpu.VMEM((tm, d), F32)]),
        out_shape=jax.ShapeDtypeStruct((nt * tm, d), BF16),
        compiler_params=_params(("arbitrary", "arbitrary"), 56),
        name="moe_experts",
    )(tile_exp, tile_src, tile_valid, xs, sorted_gate.reshape(nt * tm, 1), wg, wu, wd)

    key = (tok // tm) * e + flat_e
    yj = (pos // tm).astype(jnp.int32)
    jlo = jnp.full((nblk * e,), nt, jnp.int32).at[key].min(yj)
    jhi = jnp.full((nblk * e,), -1, jnp.int32).at[key].max(yj)
    nbc = jnp.where(jhi >= 0, jhi - jlo + 1, 0)
    cs3 = jnp.cumsum(nbc)
    w_c = nt + e * (nblk - 1)
    sidx = jnp.arange(w_c, dtype=jnp.int32)
    sc = jnp.minimum(sidx, cs3[-1] - 1)
    kidx = _count_le(cs3, sc)
    ci = (kidx // e).astype(jnp.int32)
    cy = (jlo[kidx] + sc - (cs3[kidx] - nbc[kidx])).astype(jnp.int32)
    cact = sidx < cs3[-1]
    blk_first = cs3[ci * e] - nbc[ci * e]
    blk_last = cs3[ci * e + e - 1] - 1
    cfl = (cact.astype(jnp.int32) + 2 * (cact & (sc == blk_first)).astype(jnp.int32)
           + 4 * (cact & (sc == blk_last)).astype(jnp.int32)).astype(jnp.int32)

    return pl.pallas_call(
        functools.partial(_combine_kernel, sblk=tm),
        grid_spec=pltpu.PrefetchScalarGridSpec(
            num_scalar_prefetch=3, grid=(w_c,),
            in_specs=[pl.BlockSpec((None, 1, tm), lambda s, ci_, cy_, fl: (cy_[s], 0, 0)),
                      pl.BlockSpec((tm, d), lambda s, ci_, cy_, fl: (cy_[s], 0)),
                      pl.BlockSpec((tm, d), lambda s, ci_, cy_, fl: (ci_[s], 0)),
                      pl.BlockSpec((1, d), lambda s, ci_, cy_, fl: (0, 0)),
                      pl.BlockSpec((8, d), lambda s, ci_, cy_, fl: (0, 0))],
            out_specs=pl.BlockSpec((tm, d), lambda s, ci_, cy_, fl: (ci_[s], 0)),
            scratch_shapes=[pltpu.VMEM((tm, d), F32)]),
        out_shape=jax.ShapeDtypeStruct((lq, d), F32),
        compiler_params=_params(("arbitrary",), 40),
        name="moe_combine",
    )(ci, cy, cfl, sorted_tok.reshape(nt, 1, tm), ys, h_lat, nw3.reshape(1, d), mod)


def kernel(x, c, ctx, c_ctx, w_mod, b_mod, norms, e_w_in, e_pool_w, e_pool_scale, e_s5_lam_re, e_s5_lam_im, e_s5_log_dt, e_s5_b_re, e_s5_b_im, e_s5_c_re, e_s5_c_im, e_s5_d, e_s5_w_glu, e_w_out, e_ffn_gate, e_ffn_up, e_ffn_down, o_w_in, o_gla_w_a, o_gla_b_a, o_gla_norm, o_q_norm, o_k_norm, o_w_out, o_router, o_router_b, o_moe_gate, o_moe_up, o_moe_down):
    assert x.shape[0] == 1 and w_mod.shape[0] == 2 and e_w_in.shape[0] == 1 and o_w_in.shape[0] == 1
    d = x.shape[2]
    n_ctx = ctx.shape[1]
    h = jnp.concatenate([ctx[0], x[0]], axis=0)

    vecs = jnp.zeros((8, d), F32).at[0].set(c[0]).at[1].set(c_ctx)
    mods = _modulation(vecs, w_mod, b_mod)

    mod1, mod2 = _mod_rows(mods[0], 0), _mod_rows(mods[0], 1)
    z = _normed_matmul(h, norms[0, 0], mod1, e_w_in[0].astype(BF16), n_ctx)
    pool_out = _pool_mixer(z, e_pool_w[0].astype(BF16), e_pool_scale[0], n_ctx)
    m_op, qre, qim, pre, pim, at = _s5_params(e_s5_lam_re[0], e_s5_lam_im[0], e_s5_log_dt[0],
                                              e_s5_b_re[0], e_s5_b_im[0], e_s5_c_re[0], e_s5_c_im[0])
    y_s5 = _s5_mixer(z[:, W_POOL:], m_op, qre, qim, pre, pim, at, n_ctx)
    h = _even_out(pool_out, y_s5, z, e_s5_d[0], e_s5_w_glu[0].astype(BF16), e_w_out[0].astype(BF16),
                  h, norms[0, 1], mod1, n_ctx)
    h = _dense_ffn(h, norms[0, 2], norms[0, 3], mod2, e_ffn_gate[0].astype(BF16),
                   e_ffn_up[0].astype(BF16), e_ffn_down[0].astype(BF16), n_ctx)

    mod1, mod2 = _mod_rows(mods[1], 0), _mod_rows(mods[1], 1)
    w_in = o_w_in[0]
    a0 = 2 * W_GLA_K + W_GLA_V
    w_main = jnp.concatenate([w_in[:, :a0], w_in[:, a0 + GLA_RANK:]], axis=1).astype(BF16)
    w_aux = jnp.pad(w_in[:, a0:a0 + GLA_RANK], ((0, 0), (0, 128 - GLA_RANK))).astype(BF16)
    zmain, a_aux = _normed_matmul(h, norms[1, 0], mod1, w_main, n_ctx, w_aux=w_aux)
    o_f, o_b = _gla_mixer(zmain, a_aux, o_gla_w_a[0], o_gla_b_a[0], n_ctx)
    qx, kx, vx, bound_log2 = _qk_prep(zmain, o_q_norm[0], o_k_norm[0], n_ctx)
    att = _attention(qx, kx, vx, bound_log2)
    h_lat = _odd_out(o_f, o_b, zmain, att, o_gla_norm[0], o_w_out[0].astype(BF16), h, norms[1, 1], mod1, n_ctx)
    out = _moe(h_lat, norms[1, 2], norms[1, 3], mod2, o_router[0], o_router_b[0],
               o_moe_gate[0].astype(BF16), o_moe_up[0].astype(BF16), o_moe_down[0].astype(BF16))
    return out[None]
```

```python
import functools
import math

import jax
import jax.numpy as jnp
from jax import lax
from jax.experimental import pallas as pl
from jax.experimental.pallas import tpu as pltpu

F32 = jnp.float32
BF16 = jnp.bfloat16
HI = lax.Precision.HIGHEST
EPS = 1e-6

D_MODEL = 2048
GRID_W = 64
N_MOD = 6

POOL_WINDOWS = (2, 4, 8, 16)
POOL_GROUP = 384
W_POOL = 1536
W_S5 = 512
S5_CH = 16
S5_STATE = 64
S5_GROUPS = 32
S5_T = 32

GLA_HEADS = 4
GLA_DK = 128
GLA_DV = 256
GLA_RANK = 16
GLA_TAU = 16.0
GLA_CHUNK = 64
W_GLA_K = 512
W_GLA_V = 1024
ATT_HEAD_DIM = 128
ATT_HEADS = 8
ATT_KV_HEADS = 2
ATT_GROUP = 4
W_ATT = 1024
W_ATT_KV = 256
ROPE_THETA = 10000.0

D_FF = 7168
N_EXPERTS = 8

OQ, OK_, OV, OR, OAQ, OAK, OAV = 0, 512, 1024, 2048, 3072, 4096, 4352
W_ODD_MAIN = 4608

MIB = 2 ** 20


def _params(sem, vmem_mib):
    return pltpu.CompilerParams(dimension_semantics=sem, vmem_limit_bytes=vmem_mib * MIB)


def _pick(n, cands):
    for c in cands:
        if n % c == 0:
            return c
    raise ValueError(f"no tile for {n} in {cands}")


def _sigmoid(x):
    return 1.0 / (1.0 + jnp.exp(-x))


def _rms(x, w):
    return x * lax.rsqrt(jnp.mean(x * x, axis=-1, keepdims=True) + EPS) * w


def _norm_mod(x, nw, mod_ref, row0, n_ctx):
    xn = _rms(x, nw)
    rows = row0 + lax.broadcasted_iota(jnp.int32, (x.shape[0], 1), 0)
    is_ctx = rows < n_ctx
    sc = jnp.where(is_ctx, mod_ref[2:3, :], mod_ref[0:1, :])
    sh = jnp.where(is_ctx, mod_ref[3:4, :], mod_ref[1:2, :])
    return xn * (1.0 + sc) + sh


def _gate_rows(mod_ref, row0, tm, n_ctx):
    rows = row0 + lax.broadcasted_iota(jnp.int32, (tm, 1), 0)
    return jnp.where(rows < n_ctx, mod_ref[5:6, :], mod_ref[4:5, :])


def _mod_kernel(v_ref, w_ref, b_ref, o_ref):
    v = v_ref[...]
    s = v * _sigmoid(v)
    o_ref[...] = jnp.dot(s, w_ref[...], precision=HI, preferred_element_type=F32) + b_ref[...]


def _modulation(vecs, w_mod, b_mod):
    depth, d, n6 = w_mod.shape
    tn = 1024
    return pl.pallas_call(
        _mod_kernel,
        grid=(depth, n6 // tn),
        in_specs=[pl.BlockSpec((8, d), lambda l, j: (0, 0)),
                  pl.BlockSpec((None, d, tn), lambda l, j: (l, 0, j)),
                  pl.BlockSpec((None, 1, tn), lambda l, j: (l, 0, j))],
        out_specs=pl.BlockSpec((None, 8, tn), lambda l, j: (l, 0, j)),
        out_shape=jax.ShapeDtypeStruct((depth, 8, n6), F32),
        compiler_params=_params(("parallel", "parallel"), 40),
        name="modulation",
    )(vecs, w_mod, b_mod.reshape(depth, 1, n6))


def _mod_rows(m, sub):
    d = m.shape[1] // N_MOD
    m6 = m.reshape(8, N_MOD, d)
    sh, sc, g = m6[:, 3 * sub + 0], m6[:, 3 * sub + 1], m6[:, 3 * sub + 2]
    z = jnp.zeros((d,), F32)
    return jnp.stack([sc[0], sh[0], sc[1], sh[1], g[0], g[1], z, z])


def _nmm_kernel(h_ref, nw_ref, mod_ref, w_ref, o_ref, u_ref, *, n_ctx, tm):
    @pl.when(pl.program_id(1) == 0)
    def _():
        u = _norm_mod(h_ref[...], nw_ref[...], mod_ref, pl.program_id(0) * tm, n_ctx)
        u_ref[...] = u.astype(BF16)
    o_ref[...] = jnp.dot(u_ref[...], w_ref[...], preferred_element_type=F32).astype(o_ref.dtype)


def _nmm_aux_kernel(h_ref, nw_ref, mod_ref, w_ref, wa_ref, o_ref, oa_ref, u_ref, *, n_ctx, tm):
    @pl.when(pl.program_id(1) == 0)
    def _():
        u = _norm_mod(h_ref[...], nw_ref[...], mod_ref, pl.program_id(0) * tm, n_ctx)
        ub = u.astype(BF16)
        u_ref[...] = ub
        oa_ref[...] = jnp.dot(ub, wa_ref[...], preferred_element_type=F32)
    o_ref[...] = jnp.dot(u_ref[...], w_ref[...], preferred_element_type=F32).astype(o_ref.dtype)


def _normed_matmul(h, nw, mod, w, n_ctx, w_aux=None):
    n, d = h.shape
    nout = w.shape[1]
    tm = _pick(n, (640, 512, 256, 128))
    tn = _pick(nout, (1536, 1024, 512, 256, 128))
    in_specs = [pl.BlockSpec((tm, d), lambda i, j: (i, 0)),
                pl.BlockSpec((1, d), lambda i, j: (0, 0)),
                pl.BlockSpec((8, d), lambda i, j: (0, 0)),
                pl.BlockSpec((d, tn), lambda i, j: (0, j))]
    out_specs = pl.BlockSpec((tm, tn), lambda i, j: (i, j))
    out_shape = jax.ShapeDtypeStruct((n, nout), BF16)
    args = [h, nw.reshape(1, d), mod, w]
    if w_aux is None:
        kern = functools.partial(_nmm_kernel, n_ctx=n_ctx, tm=tm)
    else:
        na = w_aux.shape[1]
        kern = functools.partial(_nmm_aux_kernel, n_ctx=n_ctx, tm=tm)
        in_specs.append(pl.BlockSpec((d, na), lambda i, j: (0, 0)))
        out_specs = [out_specs, pl.BlockSpec((tm, na), lambda i, j: (i, 0))]
        out_shape = [out_shape, jax.ShapeDtypeStruct((n, na), F32)]
        args.append(w_aux)
    return pl.pallas_call(
        kern, grid=(n // tm, nout // tn), in_specs=in_specs, out_specs=out_specs, out_shape=out_shape,
        scratch_shapes=[pltpu.VMEM((tm, d), BF16)],
        compiler_params=_params(("parallel", "arbitrary"), 48),
        name="normed_proj",
    )(*args)


_POOL_HALO = 16


def _pool_kernel(z_ref, zp_ref, zn_ref, pw_ref, ps_ref, o_ref, ext_ref, *, n_ctx, n_all, tm):
    i = pl.program_id(0)
    row0 = i * tm
    in_ctx = row0 < n_ctx
    seq_start = jnp.where(in_ctx, 0, n_ctx)
    seq_end = jnp.where(in_ctx, n_ctx, n_all)
    has_prev = row0 > seq_start
    has_next = row0 + tm < seq_end
    hl = _POOL_HALO
    ext_ref[0:hl, :] = jnp.where(has_prev, zp_ref[...].astype(F32), 0.0)
    ext_ref[hl:hl + tm, :] = z_ref[...].astype(F32)
    ext_ref[hl + tm:hl + tm + hl, :] = jnp.where(has_next, zn_ref[...].astype(F32), 0.0)
    t = row0 - seq_start + lax.broadcasted_iota(jnp.int32, (tm, 1), 0)
    seq_len = seq_end - seq_start
    for g, w in enumerate(POOL_WINDOWS):
        c0 = g * POOL_GROUP
        acc = None
        for k in range(-(w // 2), w - w // 2):
            v = ext_ref[hl + k:hl + k + tm, c0:c0 + POOL_GROUP]
            acc = v if acc is None else acc + v
        lo = jnp.maximum(t - w // 2, 0)
        hi = jnp.minimum(t + (w - w // 2), seq_len)
        cnt = (hi - lo).astype(F32)
        dlt = acc / cnt - ext_ref[hl:hl + tm, c0:c0 + POOL_GROUP]
        y = jnp.dot(dlt.astype(BF16), pw_ref[g], preferred_element_type=F32)
        o_ref[:, c0:c0 + POOL_GROUP] = (y * ps_ref[:, c0:c0 + POOL_GROUP]).astype(o_ref.dtype)


def _pool_mixer(z, pool_w, pool_scale, n_ctx):
    n = z.shape[0]
    tm = 256
    assert n_ctx % tm == 0 and n % tm == 0
    hl = _POOL_HALO
    nh = n // hl
    per = tm // hl
    kern = functools.partial(_pool_kernel, n_ctx=n_ctx, n_all=n, tm=tm)
    return pl.pallas_call(
        kern, grid=(n // tm,),
        in_specs=[pl.BlockSpec((tm, W_POOL), lambda i: (i, 0)),
                  pl.BlockSpec((hl, W_POOL), lambda i: (jnp.maximum(i * per - 1, 0), 0)),
                  pl.BlockSpec((hl, W_POOL), lambda i: (jnp.minimum((i + 1) * per, nh - 1), 0)),
                  pl.BlockSpec((4, POOL_GROUP, POOL_GROUP), lambda i: (0, 0, 0)),
                  pl.BlockSpec((1, W_POOL), lambda i: (0, 0))],
        out_specs=pl.BlockSpec((tm, W_POOL), lambda i: (i, 0)),
        out_shape=jax.ShapeDtypeStruct((n, W_POOL), BF16),
        scratch_shapes=[pltpu.VMEM((tm + 2 * hl, W_POOL), F32)],
        compiler_params=_params(("parallel",), 32),
        name="pool_mixer",
    )(z, z, z, pool_w, pool_scale.reshape(1, W_POOL))


def _dot_hi(a, b):
    return jnp.dot(a, b, precision=HI, preferred_element_type=F32)


def _dot_t_hi(a, b):
    return lax.dot_general(a, b, (((0,), (0,)), ((), ())), precision=HI, preferred_element_type=F32)


def _s5_param_kernel(lr_ref, li_ref, lrc_ref, lic_ref, ldt_ref, br_ref, bi_ref, cr_ref, ci_ref,
                     m_ref, qre_ref, qim_ref, pre_ref, pim_ref, at_ref):
    t_len = S5_T
    tp = t_len + 8
    wid = t_len * S5_CH
    col = lax.broadcasted_iota(jnp.int32, (tp, wid), 1)
    kid = lax.broadcasted_iota(jnp.int32, (tp, wid), 0)
    cq = col >> 4
    rep_nat = (kid == cq).astype(F32)
    rep_rev = (kid == t_len - 1 - cq).astype(F32)
    til = (lax.broadcasted_iota(jnp.int32, (S5_CH, wid), 0)
           == (lax.broadcasted_iota(jnp.int32, (S5_CH, wid), 1) & (S5_CH - 1))).astype(F32)
    colb = lax.broadcasted_iota(jnp.int32, (S5_CH, wid), 1) >> 4
    kk = lax.broadcasted_iota(jnp.int32, (tp, S5_STATE), 0).astype(F32)
    rows = [None] * t_len
    for d in range(2):
        dt = jnp.exp(ldt_ref[d])
        lr, li = lr_ref[d], li_ref[d]
        mag = jnp.exp(kk * (lr * dt))
        ang = kk * (li * dt)
        pr, pi_ = mag * jnp.cos(ang), mag * jnp.sin(ang)
        lrc, lic = lrc_ref[d], lic_ref[d]
        magc = jnp.exp(lrc * dt)
        arc, aic = magc * jnp.cos(lic * dt), magc * jnp.sin(lic * dt)
        den = lrc * lrc + lic * lic
        nr, ni = arc - 1.0, aic
        kre = (nr * lrc + ni * lic) / den
        kim = (ni * lrc - nr * lic) / den
        bbr = kre * br_ref[d] - kim * bi_ref[d]
        bbi = kre * bi_ref[d] + kim * br_ref[d]
        repq = rep_rev if d == 0 else rep_nat
        repp = ((kid == cq + 1) if d == 0 else (kid == t_len - cq)).astype(F32)
        repm = rep_nat if d == 0 else rep_rev
        bt_r, bt_i = _dot_hi(bbr, til), _dot_hi(bbi, til)
        eqr, eqi = _dot_t_hi(pr, repq), _dot_t_hi(pi_, repq)
        qre_ref[d] = (eqr * bt_r - eqi * bt_i).astype(qre_ref.dtype)
        qim_ref[d] = (eqr * bt_i + eqi * bt_r).astype(qim_ref.dtype)
        epr, epi = _dot_t_hi(pr, repp), _dot_t_hi(pi_, repp)
        ct_r, ct_i = _dot_t_hi(cr_ref[d], til), _dot_t_hi(ci_ref[d], til)
        pre_ref[d] = (ct_r * epr - ct_i * epi).astype(pre_ref.dtype)
        pim_ref[d] = (-(ct_r * epi + ct_i * epr)).astype(pim_ref.dtype)
        at_ref[d, 0:1, :] = pr[t_len:t_len + 1, :]
        at_ref[d, 1:2, :] = pi_[t_len:t_len + 1, :]
        emr, emi = _dot_t_hi(pr, repm), _dot_t_hi(pi_, repm)
        y_re = emr * ct_r - emi * ct_i
        y_im = emr * ct_i + emi * ct_r
        r = _dot_t_hi(bbr, y_re) - _dot_t_hi(bbi, y_im)
        for s in range(t_len):
            if d == 0:
                blk = jnp.where(colb >= s, pltpu.roll(r, S5_CH * s, 1), 0.0)
            else:
                blk = jnp.where(colb <= s, pltpu.roll(r, (S5_CH * (s + 1)) % wid, 1), 0.0)
            rows[s] = blk if rows[s] is None else rows[s] + blk
    for s in range(t_len):
        m_ref[s * S5_CH:(s + 1) * S5_CH, :] = rows[s].astype(m_ref.dtype)


def _s5_params(lam_re, lam_im, log_dt, b_re, b_im, c_re, c_im):
    g, p, n = S5_GROUPS, S5_STATE, S5_CH
    wid = S5_T * S5_CH

    def spec(*shape):
        return pl.BlockSpec((2, None) + shape, lambda gi: (0, gi) + (0,) * len(shape))

    def ospec(*shape):
        return pl.BlockSpec((None,) + shape, lambda gi: (gi,) + (0,) * len(shape))
    return pl.pallas_call(
        _s5_param_kernel, grid=(g,),
        in_specs=[spec(1, p), spec(1, p), spec(p, 1), spec(p, 1), spec(1, 1),
                  spec(p, n), spec(p, n), spec(n, p), spec(n, p)],
        out_specs=[ospec(wid, wid), ospec(2, p, wid), ospec(2, p, wid), ospec(2, p, wid), ospec(2, p, wid),
                   ospec(2, 2, p)],
        out_shape=[jax.ShapeDtypeStruct((g, wid, wid), BF16),
                   jax.ShapeDtypeStruct((g, 2, p, wid), BF16),
                   jax.ShapeDtypeStruct((g, 2, p, wid), BF16),
                   jax.ShapeDtypeStruct((g, 2, p, wid), BF16),
                   jax.ShapeDtypeStruct((g, 2, p, wid), BF16),
                   jax.ShapeDtypeStruct((g, 2, 2, p), F32)],
        compiler_params=_params(("parallel",), 32),
        name="s5_params",
    )(lam_re.reshape(2, g, 1, p), lam_im.reshape(2, g, 1, p), lam_re.reshape(2, g, p, 1),
      lam_im.reshape(2, g, p, 1), log_dt.reshape(2, g, 1, 1), b_re, b_im, c_re, c_im)


def _s5_state_kernel(u_ref, qre_ref, qim_ref, s_ref):
    u = u_ref[...]
    dn = (((1,), (1,)), ((), ()))
    s_ref[:, 0:128] = lax.dot_general(u, qre_ref[...], dn, preferred_element_type=F32)
    s_ref[:, 128:256] = lax.dot_general(u, qim_ref[...], dn, preferred_element_type=F32)


def _s5_scan_kernel(s_ref, are_ref, aim_ref, h_ref, *, nc, ncc):
    ar, ai = are_ref[...], aim_ref[...]
    is_f = lax.broadcasted_iota(jnp.int32, ar.shape, 1) < S5_STATE

    def body(i, carry):
        hr, hi = carry
        cf = i
        cb = jnp.where(i < ncc, ncc - 1 - i, nc + ncc - 1 - i)
        h_ref[cf, :, 0:64] = hr[:, 0:64]
        h_ref[cb, :, 64:128] = hr[:, 64:128]
        h_ref[cf, :, 128:192] = hi[:, 0:64]
        h_ref[cb, :, 192:256] = hi[:, 64:128]
        sf, sb = s_ref[cf], s_ref[cb]
        sr = jnp.where(is_f, sf[:, 0:128], sb[:, 0:128])
        si = jnp.where(is_f, sf[:, 128:256], sb[:, 128:256])
        return ar * hr - ai * hi + sr, ar * hi + ai * hr + si

    zero = jnp.zeros(ar.shape, F32)
    lax.fori_loop(0, nc, body, (zero, zero))


def _s5_out_kernel(u_ref, m_ref, h_ref, p_ref, y_ref):
    y = jnp.dot(u_ref[...], m_ref[...], preferred_element_type=F32)
    y_ref[...] = y + jnp.dot(h_ref[...].astype(BF16), p_ref[...], preferred_element_type=F32)


def _s5_mixer(s_all, m, qre, qim, pre, pim, at, n_ctx):
    n = s_all.shape[0]
    g, t_len, wid = S5_GROUPS, S5_T, S5_T * S5_CH
    nc, ncc = n // t_len, n_ctx // t_len
    assert n % t_len == 0 and n_ctx % t_len == 0
    u = s_all.reshape(nc, t_len, g, S5_CH).transpose(2, 0, 1, 3).reshape(g, nc, wid)
    p_all = jnp.concatenate([pre.reshape(g, 128, wid), pim.reshape(g, 128, wid)], axis=1)
    a_re = at[:, :, 0, :].reshape(g, 128)
    a_im = at[:, :, 1, :].reshape(g, 128)
    s_t = pl.pallas_call(
        _s5_state_kernel, grid=(g,),
        in_specs=[pl.BlockSpec((None, nc, wid), lambda i: (i, 0, 0)),
                  pl.BlockSpec((None, 128, wid), lambda i: (i, 0, 0)),
                  pl.BlockSpec((None, 128, wid), lambda i: (i, 0, 0))],
        out_specs=pl.BlockSpec((nc, 256), lambda i: (0, i)),
        out_shape=jax.ShapeDtypeStruct((nc, g * 256), F32),
        compiler_params=_params(("parallel",), 32),
        name="s5_chunk_state",
    )(u, qre.reshape(g, 128, wid), qim.reshape(g, 128, wid))
    h_t = pl.pallas_call(
        functools.partial(_s5_scan_kernel, nc=nc, ncc=ncc),
        out_shape=jax.ShapeDtypeStruct((nc, g, 256), F32),
        compiler_params=pltpu.CompilerParams(vmem_limit_bytes=48 * MIB),
        name="s5_chunk_scan",
    )(s_t.reshape(nc, g, 256), a_re, a_im).reshape(nc, g * 256)
    y = pl.pallas_call(
        _s5_out_kernel, grid=(g,),
        in_specs=[pl.BlockSpec((None, nc, wid), lambda i: (i, 0, 0)),
                  pl.BlockSpec((None, wid, wid), lambda i: (i, 0, 0)),
                  pl.BlockSpec((nc, 256), lambda i: (0, i)),
                  pl.BlockSpec((None, 256, wid), lambda i: (i, 0, 0))],
        out_specs=pl.BlockSpec((None, nc, wid), lambda i: (i, 0, 0)),
        out_shape=jax.ShapeDtypeStruct((g, nc, wid), F32),
        compiler_params=_params(("parallel",), 32),
        name="s5_chunk_out",
    )(u, m, h_t, p_all)
    return y.reshape(g, nc, t_len, S5_CH).transpose(1, 2, 0, 3).reshape(n, W_S5)


def _gelu_tanh(x):
    return 0.5 * x * (1.0 + jnp.tanh(math.sqrt(2.0 / math.pi) * (x + 0.044715 * (x * x * x))))


def _even_out_kernel(pool_ref, y_ref, s_ref, dsk_ref, wglu_ref, wo_ref, h_ref, nw_ref, mod_ref, o_ref,
                     *, n_ctx, tm):
    y = _gelu_tanh(y_ref[...] + s_ref[...].astype(F32) * dsk_ref[...])
    gate = jnp.dot(y.astype(BF16), wglu_ref[...], preferred_element_type=F32)
    s5 = (y * _sigmoid(gate)).astype(BF16)
    mix = jnp.dot(pool_ref[...], wo_ref[0:W_POOL, :], preferred_element_type=F32)
    mix = mix + jnp.dot(s5, wo_ref[W_POOL:W_POOL + W_S5, :], preferred_element_type=F32)
    g = _gate_rows(mod_ref, pl.program_id(0) * tm, tm, n_ctx)
    o_ref[...] = h_ref[...] + g * _rms(mix, nw_ref[...])


def _even_out(pool_out, y_s5, z, dsk, w_glu, w_out, h, nw, mod, n_ctx):
    n, d = h.shape
    tm = _pick(n, (320, 256, 128))
    return pl.pallas_call(
        functools.partial(_even_out_kernel, n_ctx=n_ctx, tm=tm), grid=(n // tm,),
        in_specs=[pl.BlockSpec((tm, W_POOL), lambda i: (i, 0)),
                  pl.BlockSpec((tm, W_S5), lambda i: (i, 0)),
                  pl.BlockSpec((tm, W_S5), lambda i: (i, W_POOL // W_S5)),
                  pl.BlockSpec((1, W_S5), lambda i: (0, 0)),
                  pl.BlockSpec((W_S5, W_S5), lambda i: (0, 0)),
                  pl.BlockSpec((d, d), lambda i: (0, 0)),
                  pl.BlockSpec((tm, d), lambda i: (i, 0)),
                  pl.BlockSpec((1, d), lambda i: (0, 0)),
                  pl.BlockSpec((8, d), lambda i: (0, 0))],
        out_specs=pl.BlockSpec((tm, d), lambda i: (i, 0)),
        out_shape=jax.ShapeDtypeStruct((n, d), F32),
        compiler_params=_params(("parallel",), 48),
        name="even_mixer_out",
    )(pool_out, y_s5, z, dsk.reshape(1, W_S5), w_glu, w_out, h, nw.reshape(1, d), mod)


def _ffn_kernel(h_ref, nw2_ref, nw3_ref, mod_ref, wg_ref, wu_ref, wd_ref, o_ref, v_ref, *, n_ctx, tm):
    f = pl.program_id(1)

    @pl.when(f == 0)
    def _():
        v = _norm_mod(h_ref[...], nw2_ref[...], mod_ref, pl.program_id(0) * tm, n_ctx)
        v_ref[...] = v.astype(BF16)
        o_ref[...] = jnp.zeros_like(o_ref)
    v = v_ref[...]
    a = jnp.dot(v, wg_ref[...], preferred_element_type=F32)
    b = jnp.dot(v, wu_ref[...], preferred_element_type=F32)
    act = (a * _sigmoid(a) * b).astype(BF16)
    o_ref[...] += jnp.dot(act, wd_ref[...], preferred_element_type=F32)

    @pl.when(f == pl.num_programs(1) - 1)
    def _():
        g = _gate_rows(mod_ref, pl.program_id(0) * tm, tm, n_ctx)
        o_ref[...] = h_ref[...] + g * _rms(o_ref[...], nw3_ref[...])


def _dense_ffn(h, nw2, nw3, mod, wg, wu, wd, n_ctx):
    n, d = h.shape
    dff = wg.shape[1]
    tm = _pick(n, (640, 512, 256, 128))
    tf = 512
    return pl.pallas_call(
        functools.partial(_ffn_kernel, n_ctx=n_ctx, tm=tm), grid=(n // tm, dff // tf),
        in_specs=[pl.BlockSpec((tm, d), lambda i, f: (i, 0)),
                  pl.BlockSpec((1, d), lambda i, f: (0, 0)),
                  pl.BlockSpec((1, d), lambda i, f: (0, 0)),
                  pl.BlockSpec((8, d), lambda i, f: (0, 0)),
                  pl.BlockSpec((d, tf), lambda i, f: (0, f)),
                  pl.BlockSpec((d, tf), lambda i, f: (0, f)),
                  pl.BlockSpec((tf, d), lambda i, f: (f, 0))],
        out_specs=pl.BlockSpec((tm, d), lambda i, f: (i, 0)),
        out_shape=jax.ShapeDtypeStruct((n, d), F32),
        scratch_shapes=[pltpu.VMEM((tm, d), BF16)],
        compiler_params=_params(("parallel", "arbitrary"), 56),
        name="dense_swiglu",
    )(h, nw2.reshape(1, d), nw3.reshape(1, d), mod, wg, wu, wd)


def _split_bf16(x):
    hi = x.astype(BF16)
    return hi, (x - hi.astype(F32)).astype(BF16)


def _gla_kernel(qf_ref, kf_ref, vf_ref, af_ref, qb_ref, kb_ref, vb_ref, ab_ref, wa_ref, ba_ref,
                of_ref, ob_ref, st_ref, *, tb):
    c = GLA_CHUNK
    nsub = tb // c

    @pl.when(pl.program_id(0) == 0)
    def _():
        st_ref[...] = jnp.zeros_like(st_ref)
    ri = lax.broadcasted_iota(jnp.int32, (c, c), 0)
    ci = lax.broadcasted_iota(jnp.int32, (c, c), 1)
    scale = GLA_DK ** -0.5
    nt = (((1,), (1,)), ((), ()))
    tn = (((0,), (0,)), ((), ()))
    dirs = ((qf_ref, kf_ref, vf_ref, af_ref, of_ref), (qb_ref, kb_ref, vb_ref, ab_ref, ob_ref))
    for d, (q_ref, k_ref, v_ref, a_ref, o_ref) in enumerate(dirs):
        mask = (ri >= ci) if d == 0 else (ci >= ri)
        cum = mask.astype(BF16)
        a_hi, a_lo = _split_bf16(a_ref[...])
        w_hi, w_lo = _split_bf16(wa_ref[d])
        z = (jnp.dot(a_hi, w_hi, preferred_element_type=F32) + jnp.dot(a_hi, w_lo, preferred_element_type=F32)
             + jnp.dot(a_lo, w_hi, preferred_element_type=F32)) + ba_ref[d]
        glog = (jnp.minimum(z, 0.0) - jnp.log(1.0 + jnp.exp(-jnp.abs(z)))) * (1.0 / GLA_TAU)
        order = range(nsub) if d == 0 else range(nsub - 1, -1, -1)
        for sc in order:
            r0 = sc * c
            g_hi, g_lo = _split_bf16(glog[r0:r0 + c, :])
            b = jnp.dot(cum, g_hi, preferred_element_type=F32) + jnp.dot(cum, g_lo, preferred_element_type=F32)
            b_last = b[c - 1:c, :] if d == 0 else b[0:1, :]
            dec = jnp.exp(b_last)
            q = q_ref[r0:r0 + c, :].astype(F32) * scale
            k = k_ref[r0:r0 + c, :].astype(F32)
            qe = (q * jnp.exp(b)).astype(BF16)
            ke = (k * jnp.exp(-b)).astype(BF16)
            kd = (k * jnp.exp(b_last - b)).astype(BF16)
            for h in range(GLA_HEADS):
                ks = slice(h * GLA_DK, (h + 1) * GLA_DK)
                vs = slice(h * GLA_DV, (h + 1) * GLA_DV)
                att = lax.dot_general(qe[:, ks], ke[:, ks], nt, preferred_element_type=F32)
                att = jnp.where(mask, att, 0.0).astype(BF16)
                vh = v_ref[r0:r0 + c, vs]
                s_old = st_ref[d, h]
                o = jnp.dot(att, vh, preferred_element_type=F32)
                o = o + lax.dot_general(qe[:, ks], s_old.astype(BF16), nt, preferred_element_type=F32)
                o_ref[r0:r0 + c, vs] = o
                upd = lax.dot_general(vh, kd[:, ks], tn, preferred_element_type=F32)
                st_ref[d, h] = s_old * dec[:, ks] + upd


def _gla_mixer(zmain, a_aux, w_a, b_a, n_ctx):
    n = zmain.shape[0]
    tb = 256
    assert n % tb == 0 and n_ctx % tb == 0
    nb, ncb = n // tb, n_ctx // tb

    def bwd(s):
        return jnp.where(s < ncb, ncb - 1 - s, nb + ncb - 1 - s)
    wa = jnp.pad(w_a, ((0, 0), (0, 128 - GLA_RANK), (0, 0)))
    in_specs = []
    for order in (lambda s: s, bwd):
        in_specs += [pl.BlockSpec((tb, W_GLA_K), lambda s, o=order: (o(s), OQ // W_GLA_K)),
                     pl.BlockSpec((tb, W_GLA_K), lambda s, o=order: (o(s), OK_ // W_GLA_K)),
                     pl.BlockSpec((tb, W_GLA_V), lambda s, o=order: (o(s), OV // W_GLA_V)),
                     pl.BlockSpec((tb, 128), lambda s, o=order: (o(s), 0))]
    in_specs += [pl.BlockSpec((2, 128, W_GLA_K), lambda s: (0, 0, 0)),
                 pl.BlockSpec((2, 1, W_GLA_K), lambda s: (0, 0, 0))]
    return pl.pallas_call(
        functools.partial(_gla_kernel, tb=tb), grid=(nb,),
        in_specs=in_specs,
        out_specs=[pl.BlockSpec((tb, W_GLA_V), lambda s: (s, 0)),
                   pl.BlockSpec((tb, W_GLA_V), lambda s: (bwd(s), 0))],
        out_shape=[jax.ShapeDtypeStruct((n, W_GLA_V), F32)] * 2,
        scratch_shapes=[pltpu.VMEM((2, GLA_HEADS, GLA_DV, GLA_DK), F32)],
        compiler_params=_params(("arbitrary",), 32),
        name="gla_chunked",
    )(zmain, zmain, zmain, a_aux, zmain, zmain, zmain, a_aux, wa, b_a.reshape(2, 1, W_GLA_K))


_ATT_PAD = 2 * ATT_HEAD_DIM
_LOG2E = 1.4426950408889634
_FIXED_SHIFT_MAX_BOUND = 40.0


def _rope_fn(row0, tm, n_ctx):
    hd = ATT_HEAD_DIM
    row = row0 + lax.broadcasted_iota(jnp.int32, (tm, hd), 0)
    lane = lax.broadcasted_iota(jnp.int32, (tm, hd), 1)
    t = row - n_ctx
    pos = jnp.where(lane < hd // 2, t // GRID_W, t % GRID_W).astype(F32)
    quarter = hd // 4
    freq = jnp.exp((lane % quarter).astype(F32) * (-math.log(ROPE_THETA) / quarter))
    ang = pos * freq
    first = (lane & quarter) == 0
    cosv = jnp.cos(ang)
    sin_s = jnp.where(first, -jnp.sin(ang), jnp.sin(ang))

    def rope(x):
        sw = jnp.where(first, pltpu.roll(x, hd - quarter, 1), pltpu.roll(x, quarter, 1))
        return x * cosv + sw * sin_s
    return rope, row >= n_ctx


def _k_prep_kernel(k_ref, v_ref, kw_ref, kx_ref, vx_ref, kmax_ref, *, n_ctx, tm):
    hd = ATT_HEAD_DIM
    rope, is_lat = _rope_fn(pl.program_id(0) * tm, tm, n_ctx)
    lane = lax.broadcasted_iota(jnp.int32, (tm, hd), 1)
    one_col = jnp.where(lane == 0, 1.0, 0.0).astype(BF16)
    ones = jnp.ones((tm, hd), BF16)
    nmax = jnp.zeros((tm, 1), F32)
    for h in range(ATT_KV_HEADS):
        sl = slice(h * hd, (h + 1) * hd)
        xn = _rms(k_ref[:, sl].astype(F32), kw_ref[...])
        kr = jnp.where(is_lat, rope(xn), xn).astype(BF16)
        kx_ref[:, h * _ATT_PAD:h * _ATT_PAD + hd] = kr
        kx_ref[:, h * _ATT_PAD + hd:(h + 1) * _ATT_PAD] = one_col
        vx_ref[:, h * _ATT_PAD:h * _ATT_PAD + hd] = v_ref[:, sl]
        vx_ref[:, h * _ATT_PAD + hd:(h + 1) * _ATT_PAD] = ones
        krf = kr.astype(F32)
        nmax = jnp.maximum(nmax, jnp.sum(krf * krf, axis=-1, keepdims=True))
    kmax_ref[...] = jnp.broadcast_to(jnp.max(nmax, axis=0, keepdims=True), kmax_ref.shape)


def _q_prep_kernel(q_ref, qw_ref, kn_ref, qx_ref, bmax_ref, *, n_ctx, tm):
    hd = ATT_HEAD_DIM
    rope, _ = _rope_fn(n_ctx + pl.program_id(0) * tm, tm, n_ctx)
    lane = lax.broadcasted_iota(jnp.int32, (tm, hd), 1)
    scale = hd ** -0.5 * _LOG2E
    bmax = jnp.zeros((tm, 1), F32)
    for h in range(ATT_HEADS):
        sl = slice(h * hd, (h + 1) * hd)
        xn = _rms(q_ref[:, sl].astype(F32), qw_ref[...])
        qr = (rope(xn) * scale).astype(BF16)
        qf = qr.astype(F32)
        bound = jnp.sqrt(jnp.sum(qf * qf, axis=-1, keepdims=True)) * kn_ref[...]
        qx_ref[:, h * _ATT_PAD:h * _ATT_PAD + hd] = qr
        qx_ref[:, h * _ATT_PAD + hd:(h + 1) * _ATT_PAD] = jnp.where(lane == 0, -bound, 0.0).astype(BF16)
        bmax = jnp.maximum(bmax, bound)
    bmax_ref[...] = jnp.broadcast_to(jnp.max(bmax, axis=0, keepdims=True), bmax_ref.shape)


def _qk_prep(zmain, q_norm, k_norm, n_ctx):
    n = zmain.shape[0]
    lq = n - n_ctx
    tm = 256
    hd = ATT_HEAD_DIM
    assert n_ctx % tm == 0
    ncb = n_ctx // tm
    kx, vx, kmax = pl.pallas_call(
        functools.partial(_k_prep_kernel, n_ctx=n_ctx, tm=tm), grid=(n // tm,),
        in_specs=[pl.BlockSpec((tm, W_ATT_KV), lambda i: (i, OAK // W_ATT_KV)),
                  pl.BlockSpec((tm, W_ATT_KV), lambda i: (i, OAV // W_ATT_KV)),
                  pl.BlockSpec((1, hd), lambda i: (0, 0))],
        out_specs=[pl.BlockSpec((tm, ATT_KV_HEADS * _ATT_PAD), lambda i: (i, 0)),
                   pl.BlockSpec((tm, ATT_KV_HEADS * _ATT_PAD), lambda i: (i, 0)),
                   pl.BlockSpec((None, 8, 128), lambda i: (i, 0, 0))],
        out_shape=[jax.ShapeDtypeStruct((n, ATT_KV_HEADS * _ATT_PAD), BF16),
                   jax.ShapeDtypeStruct((n, ATT_KV_HEADS * _ATT_PAD), BF16),
                   jax.ShapeDtypeStruct((n // tm, 8, 128), F32)],
        compiler_params=_params(("parallel",), 32),
        name="k_norm_rope",
    )(zmain, zmain, k_norm.reshape(1, hd))
    knorm = jnp.sqrt(jnp.max(kmax)).reshape(1, 1)
    qx, bmax = pl.pallas_call(
        functools.partial(_q_prep_kernel, n_ctx=n_ctx, tm=tm), grid=(lq // tm,),
        in_specs=[pl.BlockSpec((tm, W_ATT), lambda i: (i + ncb, OAQ // W_ATT)),
                  pl.BlockSpec((1, hd), lambda i: (0, 0)),
                  pl.BlockSpec((1, 1), lambda i: (0, 0))],
        out_specs=[pl.BlockSpec((tm, ATT_HEADS * _ATT_PAD), lambda i: (i, 0)),
                   pl.BlockSpec((None, 8, 128), lambda i: (i, 0, 0))],
        out_shape=[jax.ShapeDtypeStruct((lq, ATT_HEADS * _ATT_PAD), BF16),
                   jax.ShapeDtypeStruct((lq // tm, 8, 128), F32)],
        compiler_params=_params(("parallel",), 32),
        name="q_norm_rope",
    )(zmain, q_norm.reshape(1, hd), knorm)
    return qx, kx, vx, jnp.max(bmax)


def _flash_fixed_kernel(q_ref, kt_ref, v_ref, o_ref, acc_ref):
    c = pl.program_id(2)
    hd = ATT_HEAD_DIM

    @pl.when(c == 0)
    def _():
        acc_ref[...] = jnp.zeros_like(acc_ref)
    kt = kt_ref[...]
    v = v_ref[...]
    for g in range(ATT_GROUP):
        s = jnp.dot(q_ref[:, g * _ATT_PAD:(g + 1) * _ATT_PAD], kt, preferred_element_type=F32)
        acc_ref[g] += jnp.dot(jnp.exp2(s).astype(BF16), v, preferred_element_type=F32)

    @pl.when(c == pl.num_programs(2) - 1)
    def _():
        for g in range(ATT_GROUP):
            a = acc_ref[g]
            o_ref[:, g * hd:(g + 1) * hd] = (a[:, 0:hd] / a[:, hd:2 * hd]).astype(o_ref.dtype)


def _flash_online_kernel(q_ref, kt_ref, v_ref, o_ref, m_ref, acc_ref):
    c = pl.program_id(2)
    hd = ATT_HEAD_DIM

    @pl.when(c == 0)
    def _():
        m_ref[...] = jnp.full(m_ref.shape, -jnp.inf, F32)
        acc_ref[...] = jnp.zeros_like(acc_ref)
    kt = kt_ref[...]
    v = v_ref[...]
    for g in range(ATT_GROUP):
        s = jnp.dot(q_ref[:, g * _ATT_PAD:(g + 1) * _ATT_PAD], kt, preferred_element_type=F32)
        m_prev = m_ref[g]
        m_new = jnp.maximum(m_prev, jnp.max(s, axis=-1, keepdims=True))
        p = jnp.exp2(s - m_new).astype(BF16)
        acc_ref[g] = jnp.exp2(m_prev - m_new) * acc_ref[g] + jnp.dot(p, v, preferred_element_type=F32)
        m_ref[g] = m_new

    @pl.when(c == pl.num_programs(2) - 1)
    def _():
        for g in range(ATT_GROUP):
            a = acc_ref[g]
            o_ref[:, g * hd:(g + 1) * hd] = (a[:, 0:hd] / a[:, hd:2 * hd]).astype(o_ref.dtype)


def _attention(qx, kx, vx, bound_log2):
    lq = qx.shape[0]
    n = kx.shape[0]
    kt = kx.T
    gw = ATT_GROUP * _ATT_PAD
    ow = ATT_GROUP * ATT_HEAD_DIM
    tq = _pick(lq, (512, 256, 128))

    def call(kern, tk, scratch, name):
        return pl.pallas_call(
            kern, grid=(ATT_KV_HEADS, lq // tq, n // tk),
            in_specs=[pl.BlockSpec((tq, gw), lambda h, i, j: (i, h)),
                      pl.BlockSpec((_ATT_PAD, tk), lambda h, i, j: (h, j)),
                      pl.BlockSpec((tk, _ATT_PAD), lambda h, i, j: (j, h))],
            out_specs=pl.BlockSpec((tq, ow), lambda h, i, j: (i, h)),
            out_shape=jax.ShapeDtypeStruct((lq, W_ATT), BF16),
            scratch_shapes=scratch + [pltpu.VMEM((ATT_GROUP, tq, _ATT_PAD), F32)],
            compiler_params=_params(("parallel", "parallel", "arbitrary"), 48),
            name=name,
        )(qx, kt, vx)

    def fixed(_):
        return call(_flash_fixed_kernel, _pick(n, (3328, 1280, 512, 256)), [], "gqa_flash_fixed_shift")

    def online(_):
        return call(_flash_online_kernel, _pick(n, (640, 512, 256, 128)),
                    [pltpu.VMEM((ATT_GROUP, tq, 1), F32)], "gqa_flash_online")
    return lax.cond(bound_log2 <= _FIXED_SHIFT_MAX_BOUND * _LOG2E, fixed, online, None)


def _odd_out_kernel(of_ref, ob_ref, r_ref, att_ref, gn_ref, wo_ref, h_ref, nw_ref, mod_ref, o_ref):
    o = of_ref[...] + ob_ref[...]
    r = r_ref[...].astype(F32)
    silu_r = r * _sigmoid(r)
    parts = []
    for h in range(GLA_HEADS):
        vs = slice(h * GLA_DV, (h + 1) * GLA_DV)
        parts.append((_rms(o[:, vs], gn_ref[:, vs]) * silu_r[:, vs]).astype(BF16))
    gla = jnp.concatenate(parts, axis=1)
    mix = jnp.dot(gla, wo_ref[0:W_GLA_V, :], preferred_element_type=F32)
    mix = mix + jnp.dot(att_ref[...], wo_ref[W_GLA_V:W_GLA_V + W_ATT, :], preferred_element_type=F32)
    o_ref[...] = h_ref[...] + mod_ref[4:5, :] * _rms(mix, nw_ref[...])


def _odd_out(o_f, o_b, zmain, att, gla_norm, w_out, h_all, nw, mod, n_ctx):
    n, d = h_all.shape
    lq = n - n_ctx
    tm = _pick(lq, (256, 128))
    assert n_ctx % tm == 0
    off = n_ctx // tm
    return pl.pallas_call(
        _odd_out_kernel, grid=(lq // tm,),
        in_specs=[pl.BlockSpec((tm, W_GLA_V), lambda i: (i + off, 0)),
                  pl.BlockSpec((tm, W_GLA_V), lambda i: (i + off, 0)),
                  pl.BlockSpec((tm, W_GLA_V), lambda i: (i + off, OR // W_GLA_V)),
                  pl.BlockSpec((tm, W_ATT), lambda i: (i, 0)),
                  pl.BlockSpec((1, W_GLA_V), lambda i: (0, 0)),
                  pl.BlockSpec((d, d), lambda i: (0, 0)),
                  pl.BlockSpec((tm, d), lambda i: (i + off, 0)),
                  pl.BlockSpec((1, d), lambda i: (0, 0)),
                  pl.BlockSpec((8, d), lambda i: (0, 0))],
        out_specs=pl.BlockSpec((tm, d), lambda i: (i, 0)),
        out_shape=jax.ShapeDtypeStruct((lq, d), F32),
        compiler_params=_params(("parallel",), 48),
        name="odd_mixer_out",
    )(o_f, o_b, zmain, att, gla_norm.reshape(1, W_GLA_V), w_out, h_all, nw.reshape(1, d), mod)


def _router_kernel(h_ref, nw_ref, mod_ref, wr_ref, br_ref, v_ref, r_ref):
    x = h_ref[...]
    v = _rms(x, nw_ref[...]) * (1.0 + mod_ref[0:1, :]) + mod_ref[1:2, :]
    v_ref[...] = v
    logits = _dot_hi(v, wr_ref[...]) + br_ref[...]
    lane = lax.broadcasted_iota(jnp.int32, logits.shape, 1)
    m1 = jnp.max(logits, axis=-1, keepdims=True)
    i1 = jnp.min(jnp.where(logits == m1, lane, 128), axis=-1, keepdims=True)
    rest = jnp.where(lane == i1, -jnp.inf, logits)
    m2 = jnp.max(rest, axis=-1, keepdims=True)
    i2 = jnp.min(jnp.where(rest == m2, lane, 128), axis=-1, keepdims=True)
    e2 = jnp.exp(m2 - m1)
    g1 = 1.0 / (1.0 + e2)
    g2 = e2 / (1.0 + e2)
    out = jnp.where(lane == 0, i1.astype(F32), 0.0)
    out = jnp.where(lane == 1, i2.astype(F32), out)
    out = jnp.where(lane == 2, g1, out)
    out = jnp.where(lane == 3, g2, out)
    r_ref[...] = out


def _router(h_lat, nw, mod, w_router, b_router):
    lq, d = h_lat.shape
    tm = _pick(lq, (512, 256, 128))
    wr = jnp.pad(w_router, ((0, 0), (0, 128 - N_EXPERTS)))
    br = jnp.pad(b_router, (0, 128 - N_EXPERTS), constant_values=-1e30).reshape(1, 128)
    return pl.pallas_call(
        _router_kernel, grid=(lq // tm,),
        in_specs=[pl.BlockSpec((tm, d), lambda i: (i, 0)),
                  pl.BlockSpec((1, d), lambda i: (0, 0)),
                  pl.BlockSpec((8, d), lambda i: (0, 0)),
                  pl.BlockSpec((d, 128), lambda i: (0, 0)),
                  pl.BlockSpec((1, 128), lambda i: (0, 0))],
        out_specs=[pl.BlockSpec((tm, d), lambda i: (i, 0)), pl.BlockSpec((tm, 128), lambda i: (i, 0))],
        out_shape=[jax.ShapeDtypeStruct((lq, d), F32), jax.ShapeDtypeStruct((lq, 128), F32)],
        compiler_params=_params(("parallel",), 40),
        name="router_top2",
    )(h_lat, nw.reshape(1, d), mod, wr, br)


def _expert_kernel(te_ref, tv_ref, src_ref, dst_ref, v_hbm, gt_ref, wg_ref, wu_ref, wd_ref, y_hbm,
                   xg_ref, xb_ref, ys_ref, acc_ref, gsem, ssem, *, tm, nt, nf):
    j = pl.program_id(0)
    f = pl.program_id(1)
    slot = j % 2
    per_step = tm // nf

    def gather_rows(tile, slot_, lo, count):
        def body(r, carry):
            row = lo + r
            tok = src_ref[tile * tm + row]
            pltpu.make_async_copy(v_hbm.at[pl.ds(tok, 1)], xg_ref.at[slot_, pl.ds(row, 1)], gsem.at[slot_]).start()
            return carry
        lax.fori_loop(0, count, body, 0, unroll=8)

    def wait_scatter():
        pltpu.make_async_copy(ys_ref, y_hbm.at[pl.ds(0, tm)], ssem.at[0]).wait()

    @pl.when((j == 0) & (f == 0))
    def _():
        gather_rows(0, 0, 0, tm)

    @pl.when(f == 0)
    def _():
        pltpu.make_async_copy(v_hbm.at[pl.ds(0, tm)], xg_ref.at[slot], gsem.at[slot]).wait()
        xb_ref[...] = xg_ref[slot].astype(BF16)
        acc_ref[...] = jnp.zeros_like(acc_ref)

    @pl.when(j + 1 < nt)
    def _():
        gather_rows(j + 1, 1 - slot, f * per_step, per_step)

    @pl.when(tv_ref[j] == 1)
    def _():
        x = xb_ref[...]
        a = jnp.dot(x, wg_ref[...], preferred_element_type=F32)
        b = jnp.dot(x, wu_ref[...], preferred_element_type=F32)
        act = (a * _sigmoid(a) * b).astype(BF16)
        acc_ref[...] += jnp.dot(act, wd_ref[...], preferred_element_type=F32)

    @pl.when(f == nf - 1)
    def _():
        @pl.when(j > 0)
        def _():
            wait_scatter()
        ys_ref[...] = acc_ref[...] * gt_ref[...]

        def body(r, carry):
            dst = dst_ref[j * tm + r]
            pltpu.make_async_copy(ys_ref.at[pl.ds(r, 1)], y_hbm.at[pl.ds(dst, 1)], ssem.at[0]).start()
            return carry
        lax.fori_loop(0, tm, body, 0, unroll=8)

        @pl.when(j == nt - 1)
        def _():
            wait_scatter()


def _combine_kernel(y0_ref, y1_ref, h_ref, nw_ref, mod_ref, o_ref):
    o_ref[...] = h_ref[...] + mod_ref[4:5, :] * _rms(y0_ref[...] + y1_ref[...], nw_ref[...])


def _count_le(sorted_vals, queries):
    return jnp.sum((sorted_vals[None, :] <= queries[:, None]).astype(jnp.int32), axis=1)


def _moe(h_lat, nw2, nw3, mod, w_router, b_router, wg, wu, wd):
    lq, d = h_lat.shape
    e = N_EXPERTS
    dff = wg.shape[2]
    tm = _pick(lq, (512, 256))
    tf = 896
    v, route = _router(h_lat, nw2, mod, w_router, b_router)
    idx = route[:, 0:2].astype(jnp.int32)
    gates = route[:, 2:4]

    npairs = 2 * lq
    nt = npairs // tm + e
    flat_e = idx.reshape(-1)
    pair_id = jnp.arange(npairs, dtype=jnp.int32)
    onehot = (flat_e[:, None] == jnp.arange(e, dtype=jnp.int32)[None, :]).astype(jnp.int32)
    csum = jnp.cumsum(onehot, axis=0)
    rank = jnp.sum(csum * onehot, axis=1) - 1
    counts = csum[-1]
    padded = ((counts + tm - 1) // tm) * tm
    ends = jnp.cumsum(padded)
    starts = ends - padded
    pos = jnp.sum(onehot * starts[None, :], axis=1) + rank
    pair = jnp.stack([pair_id, lax.bitcast_convert_type(gates.reshape(-1), jnp.int32)], axis=1)
    empty = jnp.broadcast_to(jnp.array([-1, 0], jnp.int32), (nt * tm, 2))
    slots = empty.at[pos].set(pair)
    sorted_pair = slots[:, 0]
    sorted_gate = lax.bitcast_convert_type(slots[:, 1], F32)
    is_pad = sorted_pair < 0
    src_row = jnp.where(is_pad, 0, sorted_pair >> 1).astype(jnp.int32)
    pad_rank = jnp.cumsum(is_pad.astype(jnp.int32)) - 1
    dst_row = jnp.where(is_pad, npairs + pad_rank, (sorted_pair & 1) * lq + (sorted_pair >> 1)).astype(jnp.int32)
    n_used = ends[-1] // tm
    tile_ids = jnp.arange(nt, dtype=jnp.int32)
    tile_valid = (tile_ids < n_used).astype(jnp.int32)
    tile_exp = jnp.minimum(_count_le(ends, jnp.minimum(tile_ids, n_used - 1) * tm), e - 1)

    nf = dff // tf
    assert tm % nf == 0
    pinned = lambda j, f, tv: jnp.where(tv[j] == 1, f, nf - 1)
    y = pl.pallas_call(
        functools.partial(_expert_kernel, tm=tm, nt=nt, nf=nf),
        grid_spec=pltpu.PrefetchScalarGridSpec(
            num_scalar_prefetch=4, grid=(nt, nf),
            in_specs=[pl.BlockSpec(memory_space=pl.ANY),
                      pl.BlockSpec((tm, 1), lambda j, f, te, tv, src, dst: (j, 0)),
                      pl.BlockSpec((None, d, tf), lambda j, f, te, tv, src, dst: (te[j], 0, pinned(j, f, tv))),
                      pl.BlockSpec((None, d, tf), lambda j, f, te, tv, src, dst: (te[j], 0, pinned(j, f, tv))),
                      pl.BlockSpec((None, tf, d), lambda j, f, te, tv, src, dst: (te[j], pinned(j, f, tv), 0))],
            out_specs=pl.BlockSpec(memory_space=pl.ANY),
            scratch_shapes=[pltpu.VMEM((2, tm, d), F32), pltpu.VMEM((tm, d), BF16), pltpu.VMEM((tm, d), F32),
                            pltpu.VMEM((tm, d), F32), pltpu.SemaphoreType.DMA((2,)),
                            pltpu.SemaphoreType.DMA((1,))]),
        out_shape=jax.ShapeDtypeStruct((nt * tm, d), F32),
        compiler_params=pltpu.CompilerParams(dimension_semantics=("arbitrary", "arbitrary"),
                                             vmem_limit_bytes=56 * MIB, disable_bounds_checks=True),
        name="moe_experts",
    )(tile_exp, tile_valid, src_row, dst_row, v, sorted_gate.reshape(nt * tm, 1), wg, wu, wd)

    tc = _pick(lq, (256, 128))
    return pl.pallas_call(
        _combine_kernel, grid=(lq // tc,),
        in_specs=[pl.BlockSpec((tc, d), lambda i: (i, 0)),
                  pl.BlockSpec((tc, d), lambda i: (i + lq // tc, 0)),
                  pl.BlockSpec((tc, d), lambda i: (i, 0)),
                  pl.BlockSpec((1, d), lambda i: (0, 0)),
                  pl.BlockSpec((8, d), lambda i: (0, 0))],
        out_specs=pl.BlockSpec((tc, d), lambda i: (i, 0)),
        out_shape=jax.ShapeDtypeStruct((lq, d), F32),
        compiler_params=_params(("parallel",), 32),
        name="moe_combine",
    )(y, y, h_lat, nw3.reshape(1, d), mod)


def kernel(x, c, ctx, c_ctx, w_mod, b_mod, norms, e_w_in, e_pool_w, e_pool_scale, e_s5_lam_re, e_s5_lam_im, e_s5_log_dt, e_s5_b_re, e_s5_b_im, e_s5_c_re, e_s5_c_im, e_s5_d, e_s5_w_glu, e_w_out, e_ffn_gate, e_ffn_up, e_ffn_down, o_w_in, o_gla_w_a, o_gla_b_a, o_gla_norm, o_q_norm, o_k_norm, o_w_out, o_router, o_router_b, o_moe_gate, o_moe_up, o_moe_down):
    assert x.shape[0] == 1 and w_mod.shape[0] == 2 and e_w_in.shape[0] == 1 and o_w_in.shape[0] == 1
    d = x.shape[2]
    n_ctx = ctx.shape[1]
    h = jnp.concatenate([ctx[0], x[0]], axis=0)

    vecs = jnp.zeros((8, d), F32).at[0].set(c[0]).at[1].set(c_ctx)
    mods = _modulation(vecs, w_mod, b_mod)

    mod1, mod2 = _mod_rows(mods[0], 0), _mod_rows(mods[0], 1)
    z = _normed_matmul(h, norms[0, 0], mod1, e_w_in[0].astype(BF16), n_ctx)
    pool_out = _pool_mixer(z, e_pool_w[0].astype(BF16), e_pool_scale[0], n_ctx)
    m_op, qre, qim, pre, pim, at = _s5_params(e_s5_lam_re[0], e_s5_lam_im[0], e_s5_log_dt[0],
                                              e_s5_b_re[0], e_s5_b_im[0], e_s5_c_re[0], e_s5_c_im[0])
    y_s5 = _s5_mixer(z[:, W_POOL:], m_op, qre, qim, pre, pim, at, n_ctx)
    h = _even_out(pool_out, y_s5, z, e_s5_d[0], e_s5_w_glu[0].astype(BF16), e_w_out[0].astype(BF16),
                  h, norms[0, 1], mod1, n_ctx)
    h = _dense_ffn(h, norms[0, 2], norms[0, 3], mod2, e_ffn_gate[0].astype(BF16),
                   e_ffn_up[0].astype(BF16), e_ffn_down[0].astype(BF16), n_ctx)

    mod1, mod2 = _mod_rows(mods[1], 0), _mod_rows(mods[1], 1)
    w_in = o_w_in[0]
    a0 = 2 * W_GLA_K + W_GLA_V
    w_main = jnp.concatenate([w_in[:, :a0], w_in[:, a0 + GLA_RANK:]], axis=1).astype(BF16)
    w_aux = jnp.pad(w_in[:, a0:a0 + GLA_RANK], ((0, 0), (0, 128 - GLA_RANK))).astype(BF16)
    zmain, a_aux = _normed_matmul(h, norms[1, 0], mod1, w_main, n_ctx, w_aux=w_aux)
    o_f, o_b = _gla_mixer(zmain, a_aux, o_gla_w_a[0], o_gla_b_a[0], n_ctx)
    qx, kx, vx, bound_log2 = _qk_prep(zmain, o_q_norm[0], o_k_norm[0], n_ctx)
    att = _attention(qx, kx, vx, bound_log2)
    h_lat = _odd_out(o_f, o_b, zmain, att, o_gla_norm[0], o_w_out[0].astype(BF16), h, norms[1, 1], mod1, n_ctx)
    out = _moe(h_lat, norms[1, 2], norms[1, 3], mod2, o_router[0], o_router_b[0],
               o_moe_gate[0].astype(BF16), o_moe_up[0].astype(BF16), o_moe_down[0].astype(BF16))
    return out[None]
```

```python
import functools
import math

import jax
import jax.numpy as jnp
from jax import lax
from jax.experimental import pallas as pl
from jax.experimental.pallas import tpu as pltpu

F32 = jnp.float32
BF16 = jnp.bfloat16
HI = lax.Precision.HIGHEST
EPS = 1e-6

D_MODEL = 2048
GRID_W = 64
N_MOD = 6

POOL_WINDOWS = (2, 4, 8, 16)
POOL_GROUP = 384
W_POOL = 1536
W_S5 = 512
S5_CH = 16
S5_STATE = 64
S5_GROUPS = 32
S5_T = 32

GLA_HEADS = 4
GLA_DK = 128
GLA_DV = 256
GLA_RANK = 16
GLA_TAU = 16.0
GLA_CHUNK = 64
W_GLA_K = 512
W_GLA_V = 1024
ATT_HEAD_DIM = 128
ATT_HEADS = 8
ATT_KV_HEADS = 2
ATT_GROUP = 4
W_ATT = 1024
W_ATT_KV = 256
ROPE_THETA = 10000.0

D_FF = 7168
N_EXPERTS = 8

OQ, OK_, OV, OR, OAQ, OAK, OAV = 0, 512, 1024, 2048, 3072, 4096, 4352
W_ODD_MAIN = 4608

MIB = 2 ** 20


def _params(sem, vmem_mib):
    return pltpu.CompilerParams(dimension_semantics=sem, vmem_limit_bytes=vmem_mib * MIB)


def _pick(n, cands):
    for c in cands:
        if n % c == 0:
            return c
    raise ValueError(f"no tile for {n} in {cands}")


def _sigmoid(x):
    return 1.0 / (1.0 + jnp.exp(-x))


def _rms(x, w):
    return x * lax.rsqrt(jnp.mean(x * x, axis=-1, keepdims=True) + EPS) * w


def _norm_mod(x, nw, mod_ref, row0, n_ctx):
    xn = _rms(x, nw)
    rows = row0 + lax.broadcasted_iota(jnp.int32, (x.shape[0], 1), 0)
    is_ctx = rows < n_ctx
    sc = jnp.where(is_ctx, mod_ref[2:3, :], mod_ref[0:1, :])
    sh = jnp.where(is_ctx, mod_ref[3:4, :], mod_ref[1:2, :])
    return xn * (1.0 + sc) + sh


def _gate_rows(mod_ref, row0, tm, n_ctx):
    rows = row0 + lax.broadcasted_iota(jnp.int32, (tm, 1), 0)
    return jnp.where(rows < n_ctx, mod_ref[5:6, :], mod_ref[4:5, :])


def _mod_kernel(v_ref, w_ref, b_ref, o_ref):
    v = v_ref[...]
    s = v * _sigmoid(v)
    o_ref[...] = jnp.dot(s, w_ref[...], precision=HI, preferred_element_type=F32) + b_ref[...]


def _modulation(vecs, w_mod, b_mod):
    depth, d, n6 = w_mod.shape
    tn = 1024
    return pl.pallas_call(
        _mod_kernel,
        grid=(depth, n6 // tn),
        in_specs=[pl.BlockSpec((8, d), lambda l, j: (0, 0)),
                  pl.BlockSpec((None, d, tn), lambda l, j: (l, 0, j)),
                  pl.BlockSpec((None, 1, tn), lambda l, j: (l, 0, j))],
        out_specs=pl.BlockSpec((None, 8, tn), lambda l, j: (l, 0, j)),
        out_shape=jax.ShapeDtypeStruct((depth, 8, n6), F32),
        compiler_params=_params(("parallel", "parallel"), 40),
        name="modulation",
    )(vecs, w_mod, b_mod.reshape(depth, 1, n6))


def _mod_rows(m, sub):
    d = m.shape[1] // N_MOD
    m6 = m.reshape(8, N_MOD, d)
    sh, sc, g = m6[:, 3 * sub + 0], m6[:, 3 * sub + 1], m6[:, 3 * sub + 2]
    z = jnp.zeros((d,), F32)
    return jnp.stack([sc[0], sh[0], sc[1], sh[1], g[0], g[1], z, z])


def _nmm_kernel(h_ref, nw_ref, mod_ref, w_ref, o_ref, u_ref, *, n_ctx, tm):
    @pl.when(pl.program_id(1) == 0)
    def _():
        u = _norm_mod(h_ref[...], nw_ref[...], mod_ref, pl.program_id(0) * tm, n_ctx)
        u_ref[...] = u.astype(BF16)
    o_ref[...] = jnp.dot(u_ref[...], w_ref[...], preferred_element_type=F32).astype(o_ref.dtype)


def _nmm_aux_kernel(h_ref, nw_ref, mod_ref, w_ref, wa_ref, o_ref, oa_ref, u_ref, *, n_ctx, tm):
    @pl.when(pl.program_id(1) == 0)
    def _():
        u = _norm_mod(h_ref[...], nw_ref[...], mod_ref, pl.program_id(0) * tm, n_ctx)
        ub = u.astype(BF16)
        u_ref[...] = ub
        oa_ref[...] = jnp.dot(ub, wa_ref[...], preferred_element_type=F32)
    o_ref[...] = jnp.dot(u_ref[...], w_ref[...], preferred_element_type=F32).astype(o_ref.dtype)


def _normed_matmul(h, nw, mod, w, n_ctx, w_aux=None):
    n, d = h.shape
    nout = w.shape[1]
    tm = _pick(n, (640, 512, 256, 128))
    tn = _pick(nout, (1536, 1024, 512, 256, 128))
    in_specs = [pl.BlockSpec((tm, d), lambda i, j: (i, 0)),
                pl.BlockSpec((1, d), lambda i, j: (0, 0)),
                pl.BlockSpec((8, d), lambda i, j: (0, 0)),
                pl.BlockSpec((d, tn), lambda i, j: (0, j))]
    out_specs = pl.BlockSpec((tm, tn), lambda i, j: (i, j))
    out_shape = jax.ShapeDtypeStruct((n, nout), BF16)
    args = [h, nw.reshape(1, d), mod, w]
    if w_aux is None:
        kern = functools.partial(_nmm_kernel, n_ctx=n_ctx, tm=tm)
    else:
        na = w_aux.shape[1]
        kern = functools.partial(_nmm_aux_kernel, n_ctx=n_ctx, tm=tm)
        in_specs.append(pl.BlockSpec((d, na), lambda i, j: (0, 0)))
        out_specs = [out_specs, pl.BlockSpec((tm, na), lambda i, j: (i, 0))]
        out_shape = [out_shape, jax.ShapeDtypeStruct((n, na), F32)]
        args.append(w_aux)
    return pl.pallas_call(
        kern, grid=(n // tm, nout // tn), in_specs=in_specs, out_specs=out_specs, out_shape=out_shape,
        scratch_shapes=[pltpu.VMEM((tm, d), BF16)],
        compiler_params=_params(("parallel", "arbitrary"), 48),
        name="normed_proj",
    )(*args)


_POOL_HALO = 16


def _pool_kernel(z_ref, zp_ref, zn_ref, pw_ref, ps_ref, o_ref, ext_ref, *, n_ctx, n_all, tm):
    i = pl.program_id(0)
    row0 = i * tm
    in_ctx = row0 < n_ctx
    seq_start = jnp.where(in_ctx, 0, n_ctx)
    seq_end = jnp.where(in_ctx, n_ctx, n_all)
    has_prev = row0 > seq_start
    has_next = row0 + tm < seq_end
    hl = _POOL_HALO
    ext_ref[0:hl, :] = jnp.where(has_prev, zp_ref[...].astype(F32), 0.0)
    ext_ref[hl:hl + tm, :] = z_ref[...].astype(F32)
    ext_ref[hl + tm:hl + tm + hl, :] = jnp.where(has_next, zn_ref[...].astype(F32), 0.0)
    t = row0 - seq_start + lax.broadcasted_iota(jnp.int32, (tm, 1), 0)
    seq_len = seq_end - seq_start
    for g, w in enumerate(POOL_WINDOWS):
        c0 = g * POOL_GROUP
        acc = None
        for k in range(-(w // 2), w - w // 2):
            v = ext_ref[hl + k:hl + k + tm, c0:c0 + POOL_GROUP]
            acc = v if acc is None else acc + v
        lo = jnp.maximum(t - w // 2, 0)
        hi = jnp.minimum(t + (w - w // 2), seq_len)
        cnt = (hi - lo).astype(F32)
        dlt = acc / cnt - ext_ref[hl:hl + tm, c0:c0 + POOL_GROUP]
        y = jnp.dot(dlt.astype(BF16), pw_ref[g], preferred_element_type=F32)
        o_ref[:, c0:c0 + POOL_GROUP] = (y * ps_ref[:, c0:c0 + POOL_GROUP]).astype(o_ref.dtype)


def _pool_mixer(z, pool_w, pool_scale, n_ctx):
    n = z.shape[0]
    tm = 256
    assert n_ctx % tm == 0 and n % tm == 0
    hl = _POOL_HALO
    nh = n // hl
    per = tm // hl
    kern = functools.partial(_pool_kernel, n_ctx=n_ctx, n_all=n, tm=tm)
    return pl.pallas_call(
        kern, grid=(n // tm,),
        in_specs=[pl.BlockSpec((tm, W_POOL), lambda i: (i, 0)),
                  pl.BlockSpec((hl, W_POOL), lambda i: (jnp.maximum(i * per - 1, 0), 0)),
                  pl.BlockSpec((hl, W_POOL), lambda i: (jnp.minimum((i + 1) * per, nh - 1), 0)),
                  pl.BlockSpec((4, POOL_GROUP, POOL_GROUP), lambda i: (0, 0, 0)),
                  pl.BlockSpec((1, W_POOL), lambda i: (0, 0))],
        out_specs=pl.BlockSpec((tm, W_POOL), lambda i: (i, 0)),
        out_shape=jax.ShapeDtypeStruct((n, W_POOL), BF16),
        scratch_shapes=[pltpu.VMEM((tm + 2 * hl, W_POOL), F32)],
        compiler_params=_params(("parallel",), 32),
        name="pool_mixer",
    )(z, z, z, pool_w, pool_scale.reshape(1, W_POOL))


def _dot_hi(a, b):
    return jnp.dot(a, b, precision=HI, preferred_element_type=F32)


def _dot_t_hi(a, b):
    return lax.dot_general(a, b, (((0,), (0,)), ((), ())), precision=HI, preferred_element_type=F32)


def _s5_param_kernel(lr_ref, li_ref, lrc_ref, lic_ref, ldt_ref, br_ref, bi_ref, cr_ref, ci_ref,
                     m_ref, qre_ref, qim_ref, pre_ref, pim_ref, at_ref):
    t_len = S5_T
    tp = t_len + 8
    wid = t_len * S5_CH
    col = lax.broadcasted_iota(jnp.int32, (tp, wid), 1)
    kid = lax.broadcasted_iota(jnp.int32, (tp, wid), 0)
    cq = col >> 4
    rep_nat = (kid == cq).astype(F32)
    rep_rev = (kid == t_len - 1 - cq).astype(F32)
    til = (lax.broadcasted_iota(jnp.int32, (S5_CH, wid), 0)
           == (lax.broadcasted_iota(jnp.int32, (S5_CH, wid), 1) & (S5_CH - 1))).astype(F32)
    colb = lax.broadcasted_iota(jnp.int32, (S5_CH, wid), 1) >> 4
    kk = lax.broadcasted_iota(jnp.int32, (tp, S5_STATE), 0).astype(F32)
    rows = [None] * t_len
    for d in range(2):
        dt = jnp.exp(ldt_ref[d])
        lr, li = lr_ref[d], li_ref[d]
        mag = jnp.exp(kk * (lr * dt))
        ang = kk * (li * dt)
        pr, pi_ = mag * jnp.cos(ang), mag * jnp.sin(ang)
        lrc, lic = lrc_ref[d], lic_ref[d]
        magc = jnp.exp(lrc * dt)
        arc, aic = magc * jnp.cos(lic * dt), magc * jnp.sin(lic * dt)
        den = lrc * lrc + lic * lic
        nr, ni = arc - 1.0, aic
        kre = (nr * lrc + ni * lic) / den
        kim = (ni * lrc - nr * lic) / den
        bbr = kre * br_ref[d] - kim * bi_ref[d]
        bbi = kre * bi_ref[d] + kim * br_ref[d]
        repq = rep_rev if d == 0 else rep_nat
        repp = ((kid == cq + 1) if d == 0 else (kid == t_len - cq)).astype(F32)
        repm = rep_nat if d == 0 else rep_rev
        bt_r, bt_i = _dot_hi(bbr, til), _dot_hi(bbi, til)
        eqr, eqi = _dot_t_hi(pr, repq), _dot_t_hi(pi_, repq)
        qre_ref[d] = (eqr * bt_r - eqi * bt_i).astype(qre_ref.dtype)
        qim_ref[d] = (eqr * bt_i + eqi * bt_r).astype(qim_ref.dtype)
        epr, epi = _dot_t_hi(pr, repp), _dot_t_hi(pi_, repp)
        ct_r, ct_i = _dot_t_hi(cr_ref[d], til), _dot_t_hi(ci_ref[d], til)
        pre_ref[d] = (ct_r * epr - ct_i * epi).astype(pre_ref.dtype)
        pim_ref[d] = (-(ct_r * epi + ct_i * epr)).astype(pim_ref.dtype)
        at_ref[d, 0:1, :] = pr[t_len:t_len + 1, :]
        at_ref[d, 1:2, :] = pi_[t_len:t_len + 1, :]
        emr, emi = _dot_t_hi(pr, repm), _dot_t_hi(pi_, repm)
        y_re = emr * ct_r - emi * ct_i
        y_im = emr * ct_i + emi * ct_r
        r = _dot_t_hi(bbr, y_re) - _dot_t_hi(bbi, y_im)
        for s in range(t_len):
            if d == 0:
                blk = jnp.where(colb >= s, pltpu.roll(r, S5_CH * s, 1), 0.0)
            else:
                blk = jnp.where(colb <= s, pltpu.roll(r, (S5_CH * (s + 1)) % wid, 1), 0.0)
            rows[s] = blk if rows[s] is None else rows[s] + blk
    for s in range(t_len):
        m_ref[s * S5_CH:(s + 1) * S5_CH, :] = rows[s].astype(m_ref.dtype)


def _s5_params(lam_re, lam_im, log_dt, b_re, b_im, c_re, c_im):
    g, p, n = S5_GROUPS, S5_STATE, S5_CH
    wid = S5_T * S5_CH

    def spec(*shape):
        return pl.BlockSpec((2, None) + shape, lambda gi: (0, gi) + (0,) * len(shape))

    def ospec(*shape):
        return pl.BlockSpec((None,) + shape, lambda gi: (gi,) + (0,) * len(shape))
    return pl.pallas_call(
        _s5_param_kernel, grid=(g,),
        in_specs=[spec(1, p), spec(1, p), spec(p, 1), spec(p, 1), spec(1, 1),
                  spec(p, n), spec(p, n), spec(n, p), spec(n, p)],
        out_specs=[ospec(wid, wid), ospec(2, p, wid), ospec(2, p, wid), ospec(2, p, wid), ospec(2, p, wid),
                   ospec(2, 2, p)],
        out_shape=[jax.ShapeDtypeStruct((g, wid, wid), BF16),
                   jax.ShapeDtypeStruct((g, 2, p, wid), BF16),
                   jax.ShapeDtypeStruct((g, 2, p, wid), BF16),
                   jax.ShapeDtypeStruct((g, 2, p, wid), BF16),
                   jax.ShapeDtypeStruct((g, 2, p, wid), BF16),
                   jax.ShapeDtypeStruct((g, 2, 2, p), F32)],
        compiler_params=_params(("parallel",), 32),
        name="s5_params",
    )(lam_re.reshape(2, g, 1, p), lam_im.reshape(2, g, 1, p), lam_re.reshape(2, g, p, 1),
      lam_im.reshape(2, g, p, 1), log_dt.reshape(2, g, 1, 1), b_re, b_im, c_re, c_im)


def _s5_state_kernel(u_ref, qre_ref, qim_ref, s_ref):
    u = u_ref[...]
    dn = (((1,), (1,)), ((), ()))
    s_ref[:, 0:128] = lax.dot_general(u, qre_ref[...], dn, preferred_element_type=F32)
    s_ref[:, 128:256] = lax.dot_general(u, qim_ref[...], dn, preferred_element_type=F32)


def _s5_scan_kernel(s_ref, are_ref, aim_ref, h_ref, *, nc, ncc):
    ar, ai = are_ref[...], aim_ref[...]
    is_f = lax.broadcasted_iota(jnp.int32, ar.shape, 1) < S5_STATE

    def body(i, carry):
        hr, hi = carry
        cf = i
        cb = jnp.where(i < ncc, ncc - 1 - i, nc + ncc - 1 - i)
        h_ref[cf, :, 0:64] = hr[:, 0:64]
        h_ref[cb, :, 64:128] = hr[:, 64:128]
        h_ref[cf, :, 128:192] = hi[:, 0:64]
        h_ref[cb, :, 192:256] = hi[:, 64:128]
        sf, sb = s_ref[cf], s_ref[cb]
        sr = jnp.where(is_f, sf[:, 0:128], sb[:, 0:128])
        si = jnp.where(is_f, sf[:, 128:256], sb[:, 128:256])
        return ar * hr - ai * hi + sr, ar * hi + ai * hr + si

    zero = jnp.zeros(ar.shape, F32)
    lax.fori_loop(0, nc, body, (zero, zero))


def _s5_out_kernel(u_ref, m_ref, h_ref, p_ref, y_ref):
    y = jnp.dot(u_ref[...], m_ref[...], preferred_element_type=F32)
    y_ref[...] = y + jnp.dot(h_ref[...].astype(BF16), p_ref[...], preferred_element_type=F32)


def _s5_mixer(s_all, m, qre, qim, pre, pim, at, n_ctx):
    n = s_all.shape[0]
    g, t_len, wid = S5_GROUPS, S5_T, S5_T * S5_CH
    nc, ncc = n // t_len, n_ctx // t_len
    assert n % t_len == 0 and n_ctx % t_len == 0
    u = s_all.reshape(nc, t_len, g, S5_CH).transpose(2, 0, 1, 3).reshape(g, nc, wid)
    p_all = jnp.concatenate([pre.reshape(g, 128, wid), pim.reshape(g, 128, wid)], axis=1)
    a_re = at[:, :, 0, :].reshape(g, 128)
    a_im = at[:, :, 1, :].reshape(g, 128)
    s_t = pl.pallas_call(
        _s5_state_kernel, grid=(g,),
        in_specs=[pl.BlockSpec((None, nc, wid), lambda i: (i, 0, 0)),
                  pl.BlockSpec((None, 128, wid), lambda i: (i, 0, 0)),
                  pl.BlockSpec((None, 128, wid), lambda i: (i, 0, 0))],
        out_specs=pl.BlockSpec((nc, 256), lambda i: (0, i)),
        out_shape=jax.ShapeDtypeStruct((nc, g * 256), F32),
        compiler_params=_params(("parallel",), 32),
        name="s5_chunk_state",
    )(u, qre.reshape(g, 128, wid), qim.reshape(g, 128, wid))
    h_t = pl.pallas_call(
        functools.partial(_s5_scan_kernel, nc=nc, ncc=ncc),
        out_shape=jax.ShapeDtypeStruct((nc, g, 256), F32),
        compiler_params=pltpu.CompilerParams(vmem_limit_bytes=48 * MIB),
        name="s5_chunk_scan",
    )(s_t.reshape(nc, g, 256), a_re, a_im).reshape(nc, g * 256)
    y = pl.pallas_call(
        _s5_out_kernel, grid=(g,),
        in_specs=[pl.BlockSpec((None, nc, wid), lambda i: (i, 0, 0)),
                  pl.BlockSpec((None, wid, wid), lambda i: (i, 0, 0)),
                  pl.BlockSpec((nc, 256), lambda i: (0, i)),
                  pl.BlockSpec((None, 256, wid), lambda i: (i, 0, 0))],
        out_specs=pl.BlockSpec((None, nc, wid), lambda i: (i, 0, 0)),
        out_shape=jax.ShapeDtypeStruct((g, nc, wid), F32),
        compiler_params=_params(("parallel",), 32),
        name="s5_chunk_out",
    )(u, m, h_t, p_all)
    return y.reshape(g, nc, t_len, S5_CH).transpose(1, 2, 0, 3).reshape(n, W_S5)


def _gelu_tanh(x):
    return 0.5 * x * (1.0 + jnp.tanh(math.sqrt(2.0 / math.pi) * (x + 0.044715 * (x * x * x))))


def _even_out_kernel(pool_ref, y_ref, s_ref, dsk_ref, wglu_ref, wo_ref, h_ref, nw_ref, mod_ref, o_ref,
                     *, n_ctx, tm):
    y = _gelu_tanh(y_ref[...] + s_ref[...].astype(F32) * dsk_ref[...])
    gate = jnp.dot(y.astype(BF16), wglu_ref[...], preferred_element_type=F32)
    s5 = (y * _sigmoid(gate)).astype(BF16)
    mix = jnp.dot(pool_ref[...], wo_ref[0:W_POOL, :], preferred_element_type=F32)
    mix = mix + jnp.dot(s5, wo_ref[W_POOL:W_POOL + W_S5, :], preferred_element_type=F32)
    g = _gate_rows(mod_ref, pl.program_id(0) * tm, tm, n_ctx)
    o_ref[...] = h_ref[...] + g * _rms(mix, nw_ref[...])


def _even_out(pool_out, y_s5, z, dsk, w_glu, w_out, h, nw, mod, n_ctx):
    n, d = h.shape
    tm = _pick(n, (320, 256, 128))
    return pl.pallas_call(
        functools.partial(_even_out_kernel, n_ctx=n_ctx, tm=tm), grid=(n // tm,),
        in_specs=[pl.BlockSpec((tm, W_POOL), lambda i: (i, 0)),
                  pl.BlockSpec((tm, W_S5), lambda i: (i, 0)),
                  pl.BlockSpec((tm, W_S5), lambda i: (i, W_POOL // W_S5)),
                  pl.BlockSpec((1, W_S5), lambda i: (0, 0)),
                  pl.BlockSpec((W_S5, W_S5), lambda i: (0, 0)),
                  pl.BlockSpec((d, d), lambda i: (0, 0)),
                  pl.BlockSpec((tm, d), lambda i: (i, 0)),
                  pl.BlockSpec((1, d), lambda i: (0, 0)),
                  pl.BlockSpec((8, d), lambda i: (0, 0))],
        out_specs=pl.BlockSpec((tm, d), lambda i: (i, 0)),
        out_shape=jax.ShapeDtypeStruct((n, d), F32),
        compiler_params=_params(("parallel",), 48),
        name="even_mixer_out",
    )(pool_out, y_s5, z, dsk.reshape(1, W_S5), w_glu, w_out, h, nw.reshape(1, d), mod)


def _ffn_kernel(h_ref, nw2_ref, nw3_ref, mod_ref, wg_ref, wu_ref, wd_ref, o_ref, v_ref, *, n_ctx, tm):
    f = pl.program_id(1)

    @pl.when(f == 0)
    def _():
        v = _norm_mod(h_ref[...], nw2_ref[...], mod_ref, pl.program_id(0) * tm, n_ctx)
        v_ref[...] = v.astype(BF16)
        o_ref[...] = jnp.zeros_like(o_ref)
    v = v_ref[...]
    a = jnp.dot(v, wg_ref[...], preferred_element_type=F32)
    b = jnp.dot(v, wu_ref[...], preferred_element_type=F32)
    act = (a * _sigmoid(a) * b).astype(BF16)
    o_ref[...] += jnp.dot(act, wd_ref[...], preferred_element_type=F32)

    @pl.when(f == pl.num_programs(1) - 1)
    def _():
        g = _gate_rows(mod_ref, pl.program_id(0) * tm, tm, n_ctx)
        o_ref[...] = h_ref[...] + g * _rms(o_ref[...], nw3_ref[...])


def _dense_ffn(h, nw2, nw3, mod, wg, wu, wd, n_ctx):
    n, d = h.shape
    dff = wg.shape[1]
    tm = _pick(n, (640, 512, 256, 128))
    tf = 512
    return pl.pallas_call(
        functools.partial(_ffn_kernel, n_ctx=n_ctx, tm=tm), grid=(n // tm, dff // tf),
        in_specs=[pl.BlockSpec((tm, d), lambda i, f: (i, 0)),
                  pl.BlockSpec((1, d), lambda i, f: (0, 0)),
                  pl.BlockSpec((1, d), lambda i, f: (0, 0)),
                  pl.BlockSpec((8, d), lambda i, f: (0, 0)),
                  pl.BlockSpec((d, tf), lambda i, f: (0, f)),
                  pl.BlockSpec((d, tf), lambda i, f: (0, f)),
                  pl.BlockSpec((tf, d), lambda i, f: (f, 0))],
        out_specs=pl.BlockSpec((tm, d), lambda i, f: (i, 0)),
        out_shape=jax.ShapeDtypeStruct((n, d), F32),
        scratch_shapes=[pltpu.VMEM((tm, d), BF16)],
        compiler_params=_params(("parallel", "arbitrary"), 56),
        name="dense_swiglu",
    )(h, nw2.reshape(1, d), nw3.reshape(1, d), mod, wg, wu, wd)


def _split_bf16(x):
    hi = x.astype(BF16)
    return hi, (x - hi.astype(F32)).astype(BF16)


def _gla_kernel(qf_ref, kf_ref, vf_ref, af_ref, qb_ref, kb_ref, vb_ref, ab_ref, wa_ref, ba_ref,
                of_ref, ob_ref, st_ref, *, tb):
    c = GLA_CHUNK
    nsub = tb // c

    @pl.when(pl.program_id(0) == 0)
    def _():
        st_ref[...] = jnp.zeros_like(st_ref)
    ri = lax.broadcasted_iota(jnp.int32, (c, c), 0)
    ci = lax.broadcasted_iota(jnp.int32, (c, c), 1)
    scale = GLA_DK ** -0.5
    nt = (((1,), (1,)), ((), ()))
    tn = (((0,), (0,)), ((), ()))
    dirs = ((qf_ref, kf_ref, vf_ref, af_ref, of_ref), (qb_ref, kb_ref, vb_ref, ab_ref, ob_ref))
    for d, (q_ref, k_ref, v_ref, a_ref, o_ref) in enumerate(dirs):
        mask = (ri >= ci) if d == 0 else (ci >= ri)
        cum = mask.astype(BF16)
        a_hi, a_lo = _split_bf16(a_ref[...])
        w_hi, w_lo = _split_bf16(wa_ref[d])
        z = (jnp.dot(a_hi, w_hi, preferred_element_type=F32) + jnp.dot(a_hi, w_lo, preferred_element_type=F32)
             + jnp.dot(a_lo, w_hi, preferred_element_type=F32)) + ba_ref[d]
        glog = (jnp.minimum(z, 0.0) - jnp.log(1.0 + jnp.exp(-jnp.abs(z)))) * (1.0 / GLA_TAU)
        order = range(nsub) if d == 0 else range(nsub - 1, -1, -1)
        for sc in order:
            r0 = sc * c
            g_hi, g_lo = _split_bf16(glog[r0:r0 + c, :])
            b = jnp.dot(cum, g_hi, preferred_element_type=F32) + jnp.dot(cum, g_lo, preferred_element_type=F32)
            b_last = b[c - 1:c, :] if d == 0 else b[0:1, :]
            dec = jnp.exp(b_last)
            q = q_ref[r0:r0 + c, :].astype(F32) * scale
            k = k_ref[r0:r0 + c, :].astype(F32)
            qe = (q * jnp.exp(b)).astype(BF16)
            ke = (k * jnp.exp(-b)).astype(BF16)
            kd = (k * jnp.exp(b_last - b)).astype(BF16)
            for h in range(GLA_HEADS):
                ks = slice(h * GLA_DK, (h + 1) * GLA_DK)
                vs = slice(h * GLA_DV, (h + 1) * GLA_DV)
                att = lax.dot_general(qe[:, ks], ke[:, ks], nt, preferred_element_type=F32)
                att = jnp.where(mask, att, 0.0).astype(BF16)
                vh = v_ref[r0:r0 + c, vs]
                s_old = st_ref[d, h]
                o = jnp.dot(att, vh, preferred_element_type=F32)
                o = o + lax.dot_general(qe[:, ks], s_old.astype(BF16), nt, preferred_element_type=F32)
                o_ref[r0:r0 + c, vs] = o
                upd = lax.dot_general(vh, kd[:, ks], tn, preferred_element_type=F32)
                st_ref[d, h] = s_old * dec[:, ks] + upd


def _gla_mixer(zmain, a_aux, w_a, b_a, n_ctx):
    n = zmain.shape[0]
    tb = 256
    assert n % tb == 0 and n_ctx % tb == 0
    nb, ncb = n // tb, n_ctx // tb

    def bwd(s):
        return jnp.where(s < ncb, ncb - 1 - s, nb + ncb - 1 - s)
    wa = jnp.pad(w_a, ((0, 0), (0, 128 - GLA_RANK), (0, 0)))
    in_specs = []
    for order in (lambda s: s, bwd):
        in_specs += [pl.BlockSpec((tb, W_GLA_K), lambda s, o=order: (o(s), OQ // W_GLA_K)),
                     pl.BlockSpec((tb, W_GLA_K), lambda s, o=order: (o(s), OK_ // W_GLA_K)),
                     pl.BlockSpec((tb, W_GLA_V), lambda s, o=order: (o(s), OV // W_GLA_V)),
                     pl.BlockSpec((tb, 128), lambda s, o=order: (o(s), 0))]
    in_specs += [pl.BlockSpec((2, 128, W_GLA_K), lambda s: (0, 0, 0)),
                 pl.BlockSpec((2, 1, W_GLA_K), lambda s: (0, 0, 0))]
    return pl.pallas_call(
        functools.partial(_gla_kernel, tb=tb), grid=(nb,),
        in_specs=in_specs,
        out_specs=[pl.BlockSpec((tb, W_GLA_V), lambda s: (s, 0)),
                   pl.BlockSpec((tb, W_GLA_V), lambda s: (bwd(s), 0))],
        out_shape=[jax.ShapeDtypeStruct((n, W_GLA_V), F32)] * 2,
        scratch_shapes=[pltpu.VMEM((2, GLA_HEADS, GLA_DV, GLA_DK), F32)],
        compiler_params=_params(("arbitrary",), 32),
        name="gla_chunked",
    )(zmain, zmain, zmain, a_aux, zmain, zmain, zmain, a_aux, wa, b_a.reshape(2, 1, W_GLA_K))


_ATT_PAD = 2 * ATT_HEAD_DIM
_LOG2E = 1.4426950408889634
_FIXED_SHIFT_MAX_BOUND = 40.0


def _rope_fn(row0, tm, n_ctx):
    hd = ATT_HEAD_DIM
    row = row0 + lax.broadcasted_iota(jnp.int32, (tm, hd), 0)
    lane = lax.broadcasted_iota(jnp.int32, (tm, hd), 1)
    t = row - n_ctx
    pos = jnp.where(lane < hd // 2, t // GRID_W, t % GRID_W).astype(F32)
    quarter = hd // 4
    freq = jnp.exp((lane % quarter).astype(F32) * (-math.log(ROPE_THETA) / quarter))
    ang = pos * freq
    first = (lane & quarter) == 0
    cosv = jnp.cos(ang)
    sin_s = jnp.where(first, -jnp.sin(ang), jnp.sin(ang))

    def rope(x):
        sw = jnp.where(first, pltpu.roll(x, hd - quarter, 1), pltpu.roll(x, quarter, 1))
        return x * cosv + sw * sin_s
    return rope, row >= n_ctx


def _k_prep_kernel(k_ref, v_ref, kw_ref, kx_ref, vx_ref, kmax_ref, *, n_ctx, tm):
    hd = ATT_HEAD_DIM
    rope, is_lat = _rope_fn(pl.program_id(0) * tm, tm, n_ctx)
    lane = lax.broadcasted_iota(jnp.int32, (tm, hd), 1)
    one_col = jnp.where(lane == 0, 1.0, 0.0).astype(BF16)
    ones = jnp.ones((tm, hd), BF16)
    nmax = jnp.zeros((tm, 1), F32)
    for h in range(ATT_KV_HEADS):
        sl = slice(h * hd, (h + 1) * hd)
        xn = _rms(k_ref[:, sl].astype(F32), kw_ref[...])
        kr = jnp.where(is_lat, rope(xn), xn).astype(BF16)
        kx_ref[:, h * _ATT_PAD:h * _ATT_PAD + hd] = kr
        kx_ref[:, h * _ATT_PAD + hd:(h + 1) * _ATT_PAD] = one_col
        vx_ref[:, h * _ATT_PAD:h * _ATT_PAD + hd] = v_ref[:, sl]
        vx_ref[:, h * _ATT_PAD + hd:(h + 1) * _ATT_PAD] = ones
        krf = kr.astype(F32)
        nmax = jnp.maximum(nmax, jnp.sum(krf * krf, axis=-1, keepdims=True))
    kmax_ref[...] = jnp.broadcast_to(jnp.max(nmax, axis=0, keepdims=True), kmax_ref.shape)


def _q_prep_kernel(q_ref, qw_ref, kn_ref, qx_ref, bmax_ref, *, n_ctx, tm):
    hd = ATT_HEAD_DIM
    rope, _ = _rope_fn(n_ctx + pl.program_id(0) * tm, tm, n_ctx)
    lane = lax.broadcasted_iota(jnp.int32, (tm, hd), 1)
    scale = hd ** -0.5 * _LOG2E
    bmax = jnp.zeros((tm, 1), F32)
    for h in range(ATT_HEADS):
        sl = slice(h * hd, (h + 1) * hd)
        xn = _rms(q_ref[:, sl].astype(F32), qw_ref[...])
        qr = (rope(xn) * scale).astype(BF16)
        qf = qr.astype(F32)
        bound = jnp.sqrt(jnp.sum(qf * qf, axis=-1, keepdims=True)) * kn_ref[...]
        qx_ref[:, h * _ATT_PAD:h * _ATT_PAD + hd] = qr
        qx_ref[:, h * _ATT_PAD + hd:(h + 1) * _ATT_PAD] = jnp.where(lane == 0, -bound, 0.0).astype(BF16)
        bmax = jnp.maximum(bmax, bound)
    bmax_ref[...] = jnp.broadcast_to(jnp.max(bmax, axis=0, keepdims=True), bmax_ref.shape)


def _qk_prep(zmain, q_norm, k_norm, n_ctx):
    n = zmain.shape[0]
    lq = n - n_ctx
    tm = 256
    hd = ATT_HEAD_DIM
    assert n_ctx % tm == 0
    ncb = n_ctx // tm
    kx, vx, kmax = pl.pallas_call(
        functools.partial(_k_prep_kernel, n_ctx=n_ctx, tm=tm), grid=(n // tm,),
        in_specs=[pl.BlockSpec((tm, W_ATT_KV), lambda i: (i, OAK // W_ATT_KV)),
                  pl.BlockSpec((tm, W_ATT_KV), lambda i: (i, OAV // W_ATT_KV)),
                  pl.BlockSpec((1, hd), lambda i: (0, 0))],
        out_specs=[pl.BlockSpec((tm, ATT_KV_HEADS * _ATT_PAD), lambda i: (i, 0)),
                   pl.BlockSpec((tm, ATT_KV_HEADS * _ATT_PAD), lambda i: (i, 0)),
                   pl.BlockSpec((None, 8, 128), lambda i: (i, 0, 0))],
        out_shape=[jax.ShapeDtypeStruct((n, ATT_KV_HEADS * _ATT_PAD), BF16),
                   jax.ShapeDtypeStruct((n, ATT_KV_HEADS * _ATT_PAD), BF16),
                   jax.ShapeDtypeStruct((n // tm, 8, 128), F32)],
        compiler_params=_params(("parallel",), 32),
        name="k_norm_rope",
    )(zmain, zmain, k_norm.reshape(1, hd))
    knorm = jnp.sqrt(jnp.max(kmax)).reshape(1, 1)
    qx, bmax = pl.pallas_call(
        functools.partial(_q_prep_kernel, n_ctx=n_ctx, tm=tm), grid=(lq // tm,),
        in_specs=[pl.BlockSpec((tm, W_ATT), lambda i: (i + ncb, OAQ // W_ATT)),
                  pl.BlockSpec((1, hd), lambda i: (0, 0)),
                  pl.BlockSpec((1, 1), lambda i: (0, 0))],
        out_specs=[pl.BlockSpec((tm, ATT_HEADS * _ATT_PAD), lambda i: (i, 0)),
                   pl.BlockSpec((None, 8, 128), lambda i: (i, 0, 0))],
        out_shape=[jax.ShapeDtypeStruct((lq, ATT_HEADS * _ATT_PAD), BF16),
                   jax.ShapeDtypeStruct((lq // tm, 8, 128), F32)],
        compiler_params=_params(("parallel",), 32),
        name="q_norm_rope",
    )(zmain, q_norm.reshape(1, hd), knorm)
    return qx, kx, vx, jnp.max(bmax)


def _flash_fixed_kernel(q_ref, kt_ref, v_ref, o_ref, acc_ref):
    c = pl.program_id(2)
    hd = ATT_HEAD_DIM

    @pl.when(c == 0)
    def _():
        acc_ref[...] = jnp.zeros_like(acc_ref)
    kt = kt_ref[...]
    v = v_ref[...]
    for g in range(ATT_GROUP):
        s = jnp.dot(q_ref[:, g * _ATT_PAD:(g + 1) * _ATT_PAD], kt, preferred_element_type=F32)
        acc_ref[g] += jnp.dot(jnp.exp2(s).astype(BF16), v, preferred_element_type=F32)

    @pl.when(c == pl.num_programs(2) - 1)
    def _():
        for g in range(ATT_GROUP):
            a = acc_ref[g]
            o_ref[:, g * hd:(g + 1) * hd] = (a[:, 0:hd] / a[:, hd:2 * hd]).astype(o_ref.dtype)


def _flash_online_kernel(q_ref, kt_ref, v_ref, o_ref, m_ref, acc_ref):
    c = pl.program_id(2)
    hd = ATT_HEAD_DIM

    @pl.when(c == 0)
    def _():
        m_ref[...] = jnp.full(m_ref.shape, -jnp.inf, F32)
        acc_ref[...] = jnp.zeros_like(acc_ref)
    kt = kt_ref[...]
    v = v_ref[...]
    for g in range(ATT_GROUP):
        s = jnp.dot(q_ref[:, g * _ATT_PAD:(g + 1) * _ATT_PAD], kt, preferred_element_type=F32)
        m_prev = m_ref[g]
        m_new = jnp.maximum(m_prev, jnp.max(s, axis=-1, keepdims=True))
        p = jnp.exp2(s - m_new).astype(BF16)
        acc_ref[g] = jnp.exp2(m_prev - m_new) * acc_ref[g] + jnp.dot(p, v, preferred_element_type=F32)
        m_ref[g] = m_new

    @pl.when(c == pl.num_programs(2) - 1)
    def _():
        for g in range(ATT_GROUP):
            a = acc_ref[g]
            o_ref[:, g * hd:(g + 1) * hd] = (a[:, 0:hd] / a[:, hd:2 * hd]).astype(o_ref.dtype)


def _attention(qx, kx, vx, bound_log2):
    lq = qx.shape[0]
    n = kx.shape[0]
    kt = kx.T
    gw = ATT_GROUP * _ATT_PAD
    ow = ATT_GROUP * ATT_HEAD_DIM
    tq = _pick(lq, (512, 256, 128))

    def call(kern, tk, scratch, name):
        return pl.pallas_call(
            kern, grid=(ATT_KV_HEADS, lq // tq, n // tk),
            in_specs=[pl.BlockSpec((tq, gw), lambda h, i, j: (i, h)),
                      pl.BlockSpec((_ATT_PAD, tk), lambda h, i, j: (h, j)),
                      pl.BlockSpec((tk, _ATT_PAD), lambda h, i, j: (j, h))],
            out_specs=pl.BlockSpec((tq, ow), lambda h, i, j: (i, h)),
            out_shape=jax.ShapeDtypeStruct((lq, W_ATT), BF16),
            scratch_shapes=scratch + [pltpu.VMEM((ATT_GROUP, tq, _ATT_PAD), F32)],
            compiler_params=_params(("parallel", "parallel", "arbitrary"), 48),
            name=name,
        )(qx, kt, vx)

    def fixed(_):
        return call(_flash_fixed_kernel, _pick(n, (3328, 1280, 512, 256)), [], "gqa_flash_fixed_shift")

    def online(_):
        return call(_flash_online_kernel, _pick(n, (640, 512, 256, 128)),
                    [pltpu.VMEM((ATT_GROUP, tq, 1), F32)], "gqa_flash_online")
    return lax.cond(bound_log2 <= _FIXED_SHIFT_MAX_BOUND * _LOG2E, fixed, online, None)


def _odd_out_kernel(of_ref, ob_ref, r_ref, att_ref, gn_ref, wo_ref, h_ref, nw_ref, mod_ref, o_ref):
    o = of_ref[...] + ob_ref[...]
    r = r_ref[...].astype(F32)
    silu_r = r * _sigmoid(r)
    parts = []
    for h in range(GLA_HEADS):
        vs = slice(h * GLA_DV, (h + 1) * GLA_DV)
        parts.append((_rms(o[:, vs], gn_ref[:, vs]) * silu_r[:, vs]).astype(BF16))
    gla = jnp.concatenate(parts, axis=1)
    mix = jnp.dot(gla, wo_ref[0:W_GLA_V, :], preferred_element_type=F32)
    mix = mix + jnp.dot(att_ref[...], wo_ref[W_GLA_V:W_GLA_V + W_ATT, :], preferred_element_type=F32)
    o_ref[...] = h_ref[...] + mod_ref[4:5, :] * _rms(mix, nw_ref[...])


def _odd_out(o_f, o_b, zmain, att, gla_norm, w_out, h_all, nw, mod, n_ctx):
    n, d = h_all.shape
    lq = n - n_ctx
    tm = _pick(lq, (256, 128))
    assert n_ctx % tm == 0
    off = n_ctx // tm
    return pl.pallas_call(
        _odd_out_kernel, grid=(lq // tm,),
        in_specs=[pl.BlockSpec((tm, W_GLA_V), lambda i: (i + off, 0)),
                  pl.BlockSpec((tm, W_GLA_V), lambda i: (i + off, 0)),
                  pl.BlockSpec((tm, W_GLA_V), lambda i: (i + off, OR // W_GLA_V)),
                  pl.BlockSpec((tm, W_ATT), lambda i: (i, 0)),
                  pl.BlockSpec((1, W_GLA_V), lambda i: (0, 0)),
                  pl.BlockSpec((d, d), lambda i: (0, 0)),
                  pl.BlockSpec((tm, d), lambda i: (i + off, 0)),
                  pl.BlockSpec((1, d), lambda i: (0, 0)),
                  pl.BlockSpec((8, d), lambda i: (0, 0))],
        out_specs=pl.BlockSpec((tm, d), lambda i: (i, 0)),
        out_shape=jax.ShapeDtypeStruct((lq, d), F32),
        compiler_params=_params(("parallel",), 48),
        name="odd_mixer_out",
    )(o_f, o_b, zmain, att, gla_norm.reshape(1, W_GLA_V), w_out, h_all, nw.reshape(1, d), mod)


def _router_kernel(h_ref, nw_ref, mod_ref, wr_ref, br_ref, v_ref, r_ref):
    x = h_ref[...]
    v = _rms(x, nw_ref[...]) * (1.0 + mod_ref[0:1, :]) + mod_ref[1:2, :]
    v_ref[...] = v
    logits = _dot_hi(v, wr_ref[...]) + br_ref[...]
    lane = lax.broadcasted_iota(jnp.int32, logits.shape, 1)
    m1 = jnp.max(logits, axis=-1, keepdims=True)
    i1 = jnp.min(jnp.where(logits == m1, lane, 128), axis=-1, keepdims=True)
    rest = jnp.where(lane == i1, -jnp.inf, logits)
    m2 = jnp.max(rest, axis=-1, keepdims=True)
    i2 = jnp.min(jnp.where(rest == m2, lane, 128), axis=-1, keepdims=True)
    e2 = jnp.exp(m2 - m1)
    g1 = 1.0 / (1.0 + e2)
    g2 = e2 / (1.0 + e2)
    out = jnp.where(lane == 0, i1.astype(F32), 0.0)
    out = jnp.where(lane == 1, i2.astype(F32), out)
    out = jnp.where(lane == 2, g1, out)
    out = jnp.where(lane == 3, g2, out)
    r_ref[...] = out


def _router(h_lat, nw, mod, w_router, b_router):
    lq, d = h_lat.shape
    tm = _pick(lq, (512, 256, 128))
    wr = jnp.pad(w_router, ((0, 0), (0, 128 - N_EXPERTS)))
    br = jnp.pad(b_router, (0, 128 - N_EXPERTS), constant_values=-1e30).reshape(1, 128)
    return pl.pallas_call(
        _router_kernel, grid=(lq // tm,),
        in_specs=[pl.BlockSpec((tm, d), lambda i: (i, 0)),
                  pl.BlockSpec((1, d), lambda i: (0, 0)),
                  pl.BlockSpec((8, d), lambda i: (0, 0)),
                  pl.BlockSpec((d, 128), lambda i: (0, 0)),
                  pl.BlockSpec((1, 128), lambda i: (0, 0))],
        out_specs=[pl.BlockSpec((tm, d), lambda i: (i, 0)), pl.BlockSpec((tm, 128), lambda i: (i, 0))],
        out_shape=[jax.ShapeDtypeStruct((lq, d), F32), jax.ShapeDtypeStruct((lq, 128), F32)],
        compiler_params=_params(("parallel",), 40),
        name="router_top2",
    )(h_lat, nw.reshape(1, d), mod, wr, br)


def _expert_kernel(te_ref, tv_ref, src_ref, dst_ref, v_hbm, gt_ref, wg_ref, wu_ref, wd_ref, y_hbm,
                   xg_ref, xb_ref, ys_ref, acc_ref, gsem, ssem, *, tm, nt, nf):
    j = pl.program_id(0)
    f = pl.program_id(1)
    slot = j % 2
    per_step = tm // nf
    lo = f * per_step
    valid = tv_ref[j] == 1

    def start_gather(tile, slot_, row):
        tok = src_ref[tile * tm + row]
        pltpu.make_async_copy(v_hbm.at[pl.ds(tok, 1)], xg_ref.at[slot_, pl.ds(row, 1)], gsem.at[slot_]).start()

    def start_scatter(row):
        dst = dst_ref[j * tm + row]
        pltpu.make_async_copy(ys_ref.at[pl.ds(row, 1)], y_hbm.at[pl.ds(dst, 1)], ssem.at[0]).start()

    def gather_loop(tile, slot_, first, count):
        def body(r, carry):
            start_gather(tile, slot_, first + r)
            return carry
        lax.fori_loop(0, count, body, 0, unroll=8)

    @pl.when((j == 0) & (f == 0))
    def _():
        gather_loop(0, 0, 0, tm)
        ys_ref[...] = jnp.zeros_like(ys_ref)

    @pl.when(f == 0)
    def _():
        pltpu.make_async_copy(v_hbm.at[pl.ds(0, tm)], xg_ref.at[slot], gsem.at[slot]).wait()
        xb_ref[...] = xg_ref[slot].astype(BF16)
        acc_ref[...] = jnp.zeros_like(acc_ref)

    @pl.when(valid)
    def _():
        for r in range(per_step):
            start_gather(j + 1, 1 - slot, lo + r)
        for r in range(per_step):
            start_scatter(lo + r)
        x = xb_ref[...]
        a = jnp.dot(x, wg_ref[...], preferred_element_type=F32)
        b = jnp.dot(x, wu_ref[...], preferred_element_type=F32)
        act = (a * _sigmoid(a) * b).astype(BF16)
        acc_ref[...] += jnp.dot(act, wd_ref[...], preferred_element_type=F32)

    @pl.when(jnp.logical_not(valid))
    def _():
        @pl.when(j + 1 < nt)
        def _():
            gather_loop(j + 1, 1 - slot, lo, per_step)

        def body(r, carry):
            start_scatter(lo + r)
            return carry
        lax.fori_loop(0, per_step, body, 0, unroll=8)

    @pl.when(f == nf - 1)
    def _():
        pltpu.make_async_copy(ys_ref, y_hbm.at[pl.ds(0, tm)], ssem.at[0]).wait()
        ys_ref[...] = acc_ref[...] * gt_ref[...]


def _combine_kernel(y0_ref, y1_ref, h_ref, nw_ref, mod_ref, o_ref):
    o_ref[...] = h_ref[...] + mod_ref[4:5, :] * _rms(y0_ref[...] + y1_ref[...], nw_ref[...])


def _count_le(sorted_vals, queries):
    return jnp.sum((sorted_vals[None, :] <= queries[:, None]).astype(jnp.int32), axis=1)


def _moe(h_lat, nw2, nw3, mod, w_router, b_router, wg, wu, wd):
    lq, d = h_lat.shape
    e = N_EXPERTS
    dff = wg.shape[2]
    tm = _pick(lq, (512, 256))
    tf = 896
    v, route = _router(h_lat, nw2, mod, w_router, b_router)
    idx = route[:, 0:2].astype(jnp.int32)
    gates = route[:, 2:4]

    npairs = 2 * lq
    nt = npairs // tm + e + 1
    flat_e = idx.reshape(-1)
    pair_id = jnp.arange(npairs, dtype=jnp.int32)
    onehot = (flat_e[:, None] == jnp.arange(e, dtype=jnp.int32)[None, :]).astype(jnp.int32)
    csum = jnp.cumsum(onehot, axis=0)
    rank = jnp.sum(csum * onehot, axis=1) - 1
    counts = csum[-1]
    padded = ((counts + tm - 1) // tm) * tm
    ends = jnp.cumsum(padded)
    starts = ends - padded
    pos = jnp.sum(onehot * starts[None, :], axis=1) + rank
    pair = jnp.stack([pair_id, lax.bitcast_convert_type(gates.reshape(-1), jnp.int32)], axis=1)
    empty = jnp.broadcast_to(jnp.array([-1, 0], jnp.int32), (nt * tm, 2))
    slots = empty.at[pos].set(pair)
    sorted_pair = slots[:, 0]
    sorted_gate = lax.bitcast_convert_type(slots[:, 1], F32)
    is_pad = sorted_pair < 0
    src_row = jnp.where(is_pad, 0, sorted_pair >> 1).astype(jnp.int32)
    pad_rank = jnp.cumsum(is_pad.astype(jnp.int32)) - 1
    dst_row = jnp.where(is_pad, npairs + pad_rank, (sorted_pair & 1) * lq + (sorted_pair >> 1)).astype(jnp.int32)
    dst_row = jnp.concatenate([nt * tm + jnp.arange(tm, dtype=jnp.int32), dst_row])
    n_used = ends[-1] // tm
    tile_ids = jnp.arange(nt, dtype=jnp.int32)
    tile_valid = (tile_ids < n_used).astype(jnp.int32)
    tile_exp = jnp.minimum(_count_le(ends, jnp.minimum(tile_ids, n_used - 1) * tm), e - 1)

    nf = dff // tf
    assert tm % nf == 0
    pinned = lambda j, f, tv: jnp.where(tv[j] == 1, f, nf - 1)
    y = pl.pallas_call(
        functools.partial(_expert_kernel, tm=tm, nt=nt, nf=nf),
        grid_spec=pltpu.PrefetchScalarGridSpec(
            num_scalar_prefetch=4, grid=(nt, nf),
            in_specs=[pl.BlockSpec(memory_space=pl.ANY),
                      pl.BlockSpec((tm, 1), lambda j, f, te, tv, src, dst: (j, 0)),
                      pl.BlockSpec((None, d, tf), lambda j, f, te, tv, src, dst: (te[j], 0, pinned(j, f, tv))),
                      pl.BlockSpec((None, d, tf), lambda j, f, te, tv, src, dst: (te[j], 0, pinned(j, f, tv))),
                      pl.BlockSpec((None, tf, d), lambda j, f, te, tv, src, dst: (te[j], pinned(j, f, tv), 0))],
            out_specs=pl.BlockSpec(memory_space=pl.ANY),
            scratch_shapes=[pltpu.VMEM((2, tm, d), F32), pltpu.VMEM((tm, d), BF16), pltpu.VMEM((tm, d), F32),
                            pltpu.VMEM((tm, d), F32), pltpu.SemaphoreType.DMA((2,)),
                            pltpu.SemaphoreType.DMA((1,))]),
        out_shape=jax.ShapeDtypeStruct(((nt + 1) * tm, d), F32),
        compiler_params=pltpu.CompilerParams(dimension_semantics=("arbitrary", "arbitrary"),
                                             vmem_limit_bytes=56 * MIB, disable_bounds_checks=True),
        name="moe_experts",
    )(tile_exp, tile_valid, src_row, dst_row, v, sorted_gate.reshape(nt * tm, 1), wg, wu, wd)

    tc = _pick(lq, (256, 128))
    return pl.pallas_call(
        _combine_kernel, grid=(lq // tc,),
        in_specs=[pl.BlockSpec((tc, d), lambda i: (i, 0)),
                  pl.BlockSpec((tc, d), lambda i: (i + lq // tc, 0)),
                  pl.BlockSpec((tc, d), lambda i: (i, 0)),
                  pl.BlockSpec((1, d), lambda i: (0, 0)),
                  pl.BlockSpec((8, d), lambda i: (0, 0))],
        out_specs=pl.BlockSpec((tc, d), lambda i: (i, 0)),
        out_shape=jax.ShapeDtypeStruct((lq, d), F32),
        compiler_params=_params(("parallel",), 32),
        name="moe_combine",
    )(y, y, h_lat, nw3.reshape(1, d), mod)


def kernel(x, c, ctx, c_ctx, w_mod, b_mod, norms, e_w_in, e_pool_w, e_pool_scale, e_s5_lam_re, e_s5_lam_im, e_s5_log_dt, e_s5_b_re, e_s5_b_im, e_s5_c_re, e_s5_c_im, e_s5_d, e_s5_w_glu, e_w_out, e_ffn_gate, e_ffn_up, e_ffn_down, o_w_in, o_gla_w_a, o_gla_b_a, o_gla_norm, o_q_norm, o_k_norm, o_w_out, o_router, o_router_b, o_moe_gate, o_moe_up, o_moe_down):
    assert x.shape[0] == 1 and w_mod.shape[0] == 2 and e_w_in.shape[0] == 1 and o_w_in.shape[0] == 1
    d = x.shape[2]
    n_ctx = ctx.shape[1]
    h = jnp.concatenate([ctx[0], x[0]], axis=0)

    vecs = jnp.zeros((8, d), F32).at[0].set(c[0]).at[1].set(c_ctx)
    mods = _modulation(vecs, w_mod, b_mod)

    mod1, mod2 = _mod_rows(mods[0], 0), _mod_rows(mods[0], 1)
    z = _normed_matmul(h, norms[0, 0], mod1, e_w_in[0].astype(BF16), n_ctx)
    pool_out = _pool_mixer(z, e_pool_w[0].astype(BF16), e_pool_scale[0], n_ctx)
    m_op, qre, qim, pre, pim, at = _s5_params(e_s5_lam_re[0], e_s5_lam_im[0], e_s5_log_dt[0],
                                              e_s5_b_re[0], e_s5_b_im[0], e_s5_c_re[0], e_s5_c_im[0])
    y_s5 = _s5_mixer(z[:, W_POOL:], m_op, qre, qim, pre, pim, at, n_ctx)
    h = _even_out(pool_out, y_s5, z, e_s5_d[0], e_s5_w_glu[0].astype(BF16), e_w_out[0].astype(BF16),
                  h, norms[0, 1], mod1, n_ctx)
    h = _dense_ffn(h, norms[0, 2], norms[0, 3], mod2, e_ffn_gate[0].astype(BF16),
                   e_ffn_up[0].astype(BF16), e_ffn_down[0].astype(BF16), n_ctx)

    mod1, mod2 = _mod_rows(mods[1], 0), _mod_rows(mods[1], 1)
    w_in = o_w_in[0]
    a0 = 2 * W_GLA_K + W_GLA_V
    w_main = jnp.concatenate([w_in[:, :a0], w_in[:, a0 + GLA_RANK:]], axis=1).astype(BF16)
    w_aux = jnp.pad(w_in[:, a0:a0 + GLA_RANK], ((0, 0), (0, 128 - GLA_RANK))).astype(BF16)
    zmain, a_aux = _normed_matmul(h, norms[1, 0], mod1, w_main, n_ctx, w_aux=w_aux)
    o_f, o_b = _gla_mixer(zmain, a_aux, o_gla_w_a[0], o_gla_b_a[0], n_ctx)
    qx, kx, vx, bound_log2 = _qk_prep(zmain, o_q_norm[0], o_k_norm[0], n_ctx)
    att = _attention(qx, kx, vx, bound_log2)
    h_lat = _odd_out(o_f, o_b, zmain, att, o_gla_norm[0], o_w_out[0].astype(BF16), h, norms[1, 1], mod1, n_ctx)
    out = _moe(h_lat, norms[1, 2], norms[1, 3], mod2, o_router[0], o_router_b[0],
               o_moe_gate[0].astype(BF16), o_moe_up[0].astype(BF16), o_moe_down[0].astype(BF16))
    return out[None]
```

```python
import functools
import math

import jax
import jax.numpy as jnp
from jax import lax
from jax.experimental import pallas as pl
from jax.experimental.pallas import tpu as pltpu

F32 = jnp.float32
BF16 = jnp.bfloat16
HI = lax.Precision.HIGHEST
EPS = 1e-6

D_MODEL = 2048
GRID_W = 64
N_MOD = 6

POOL_WINDOWS = (2, 4, 8, 16)
POOL_GROUP = 384
W_POOL = 1536
W_S5 = 512
S5_CH = 16
S5_STATE = 64
S5_GROUPS = 32
S5_T = 32

GLA_HEADS = 4
GLA_DK = 128
GLA_DV = 256
GLA_RANK = 16
GLA_TAU = 16.0
GLA_CHUNK = 64
W_GLA_K = 512
W_GLA_V = 1024
ATT_HEAD_DIM = 128
ATT_HEADS = 8
ATT_KV_HEADS = 2
ATT_GROUP = 4
W_ATT = 1024
W_ATT_KV = 256
ROPE_THETA = 10000.0

D_FF = 7168
N_EXPERTS = 8

OQ, OK_, OV, OR, OAQ, OAK, OAV = 0, 512, 1024, 2048, 3072, 4096, 4352
W_ODD_MAIN = 4608

MIB = 2 ** 20


def _params(sem, vmem_mib):
    return pltpu.CompilerParams(dimension_semantics=sem, vmem_limit_bytes=vmem_mib * MIB)


def _pick(n, cands):
    for c in cands:
        if n % c == 0:
            return c
    raise ValueError(f"no tile for {n} in {cands}")


def _sigmoid(x):
    return 1.0 / (1.0 + jnp.exp(-x))


def _rms(x, w):
    return x * lax.rsqrt(jnp.mean(x * x, axis=-1, keepdims=True) + EPS) * w


def _norm_mod(x, nw, mod_ref, row0, n_ctx):
    xn = _rms(x, nw)
    rows = row0 + lax.broadcasted_iota(jnp.int32, (x.shape[0], 1), 0)
    is_ctx = rows < n_ctx
    sc = jnp.where(is_ctx, mod_ref[2:3, :], mod_ref[0:1, :])
    sh = jnp.where(is_ctx, mod_ref[3:4, :], mod_ref[1:2, :])
    return xn * (1.0 + sc) + sh


def _gate_rows(mod_ref, row0, tm, n_ctx):
    rows = row0 + lax.broadcasted_iota(jnp.int32, (tm, 1), 0)
    return jnp.where(rows < n_ctx, mod_ref[5:6, :], mod_ref[4:5, :])


def _mod_kernel(v_ref, w_ref, b_ref, o_ref):
    v = v_ref[...]
    s = v * _sigmoid(v)
    o_ref[...] = jnp.dot(s, w_ref[...], precision=HI, preferred_element_type=F32) + b_ref[...]


def _modulation(vecs, w_mod, b_mod):
    depth, d, n6 = w_mod.shape
    tn = 1024
    return pl.pallas_call(
        _mod_kernel,
        grid=(depth, n6 // tn),
        in_specs=[pl.BlockSpec((8, d), lambda l, j: (0, 0)),
                  pl.BlockSpec((None, d, tn), lambda l, j: (l, 0, j)),
                  pl.BlockSpec((None, 1, tn), lambda l, j: (l, 0, j))],
        out_specs=pl.BlockSpec((None, 8, tn), lambda l, j: (l, 0, j)),
        out_shape=jax.ShapeDtypeStruct((depth, 8, n6), F32),
        compiler_params=_params(("parallel", "parallel"), 40),
        name="modulation",
    )(vecs, w_mod, b_mod.reshape(depth, 1, n6))


def _mod_rows(m, sub):
    d = m.shape[1] // N_MOD
    m6 = m.reshape(8, N_MOD, d)
    sh, sc, g = m6[:, 3 * sub + 0], m6[:, 3 * sub + 1], m6[:, 3 * sub + 2]
    z = jnp.zeros((d,), F32)
    return jnp.stack([sc[0], sh[0], sc[1], sh[1], g[0], g[1], z, z])


def _nmm_kernel(h_ref, nw_ref, mod_ref, w_ref, o_ref, u_ref, *, n_ctx, tm):
    @pl.when(pl.program_id(1) == 0)
    def _():
        u = _norm_mod(h_ref[...], nw_ref[...], mod_ref, pl.program_id(0) * tm, n_ctx)
        u_ref[...] = u.astype(BF16)
    o_ref[...] = jnp.dot(u_ref[...], w_ref[...], preferred_element_type=F32).astype(o_ref.dtype)


def _nmm_aux_kernel(h_ref, nw_ref, mod_ref, w_ref, wa_ref, o_ref, oa_ref, u_ref, *, n_ctx, tm):
    @pl.when(pl.program_id(1) == 0)
    def _():
        u = _norm_mod(h_ref[...], nw_ref[...], mod_ref, pl.program_id(0) * tm, n_ctx)
        ub = u.astype(BF16)
        u_ref[...] = ub
        oa_ref[...] = jnp.dot(ub, wa_ref[...], preferred_element_type=F32)
    o_ref[...] = jnp.dot(u_ref[...], w_ref[...], preferred_element_type=F32).astype(o_ref.dtype)


def _normed_matmul(h, nw, mod, w, n_ctx, w_aux=None):
    n, d = h.shape
    nout = w.shape[1]
    tm = _pick(n, (640, 512, 256, 128))
    tn = _pick(nout, (1536, 1024, 512, 256, 128))
    in_specs = [pl.BlockSpec((tm, d), lambda i, j: (i, 0)),
                pl.BlockSpec((1, d), lambda i, j: (0, 0)),
                pl.BlockSpec((8, d), lambda i, j: (0, 0)),
                pl.BlockSpec((d, tn), lambda i, j: (0, j))]
    out_specs = pl.BlockSpec((tm, tn), lambda i, j: (i, j))
    out_shape = jax.ShapeDtypeStruct((n, nout), BF16)
    args = [h, nw.reshape(1, d), mod, w]
    if w_aux is None:
        kern = functools.partial(_nmm_kernel, n_ctx=n_ctx, tm=tm)
    else:
        na = w_aux.shape[1]
        kern = functools.partial(_nmm_aux_kernel, n_ctx=n_ctx, tm=tm)
        in_specs.append(pl.BlockSpec((d, na), lambda i, j: (0, 0)))
        out_specs = [out_specs, pl.BlockSpec((tm, na), lambda i, j: (i, 0))]
        out_shape = [out_shape, jax.ShapeDtypeStruct((n, na), F32)]
        args.append(w_aux)
    return pl.pallas_call(
        kern, grid=(n // tm, nout // tn), in_specs=in_specs, out_specs=out_specs, out_shape=out_shape,
        scratch_shapes=[pltpu.VMEM((tm, d), BF16)],
        compiler_params=_params(("parallel", "arbitrary"), 48),
        name="normed_proj",
    )(*args)


_POOL_HALO = 16


def _pool_kernel(z_ref, zp_ref, zn_ref, pw_ref, ps_ref, o_ref, ext_ref, *, n_ctx, n_all, tm):
    i = pl.program_id(0)
    row0 = i * tm
    in_ctx = row0 < n_ctx
    seq_start = jnp.where(in_ctx, 0, n_ctx)
    seq_end = jnp.where(in_ctx, n_ctx, n_all)
    has_prev = row0 > seq_start
    has_next = row0 + tm < seq_end
    hl = _POOL_HALO
    ext_ref[0:hl, :] = jnp.where(has_prev, zp_ref[...].astype(F32), 0.0)
    ext_ref[hl:hl + tm, :] = z_ref[...].astype(F32)
    ext_ref[hl + tm:hl + tm + hl, :] = jnp.where(has_next, zn_ref[...].astype(F32), 0.0)
    t = row0 - seq_start + lax.broadcasted_iota(jnp.int32, (tm, 1), 0)
    seq_len = seq_end - seq_start
    for g, w in enumerate(POOL_WINDOWS):
        c0 = g * POOL_GROUP
        acc = None
        for k in range(-(w // 2), w - w // 2):
            v = ext_ref[hl + k:hl + k + tm, c0:c0 + POOL_GROUP]
            acc = v if acc is None else acc + v
        lo = jnp.maximum(t - w // 2, 0)
        hi = jnp.minimum(t + (w - w // 2), seq_len)
        cnt = (hi - lo).astype(F32)
        dlt = acc / cnt - ext_ref[hl:hl + tm, c0:c0 + POOL_GROUP]
        y = jnp.dot(dlt.astype(BF16), pw_ref[g], preferred_element_type=F32)
        o_ref[:, c0:c0 + POOL_GROUP] = (y * ps_ref[:, c0:c0 + POOL_GROUP]).astype(o_ref.dtype)


def _pool_mixer(z, pool_w, pool_scale, n_ctx):
    n = z.shape[0]
    tm = 256
    assert n_ctx % tm == 0 and n % tm == 0
    hl = _POOL_HALO
    nh = n // hl
    per = tm // hl
    kern = functools.partial(_pool_kernel, n_ctx=n_ctx, n_all=n, tm=tm)
    return pl.pallas_call(
        kern, grid=(n // tm,),
        in_specs=[pl.BlockSpec((tm, W_POOL), lambda i: (i, 0)),
                  pl.BlockSpec((hl, W_POOL), lambda i: (jnp.maximum(i * per - 1, 0), 0)),
                  pl.BlockSpec((hl, W_POOL), lambda i: (jnp.minimum((i + 1) * per, nh - 1), 0)),
                  pl.BlockSpec((4, POOL_GROUP, POOL_GROUP), lambda i: (0, 0, 0)),
                  pl.BlockSpec((1, W_POOL), lambda i: (0, 0))],
        out_specs=pl.BlockSpec((tm, W_POOL), lambda i: (i, 0)),
        out_shape=jax.ShapeDtypeStruct((n, W_POOL), BF16),
        scratch_shapes=[pltpu.VMEM((tm + 2 * hl, W_POOL), F32)],
        compiler_params=_params(("parallel",), 32),
        name="pool_mixer",
    )(z, z, z, pool_w, pool_scale.reshape(1, W_POOL))


def _dot_hi(a, b):
    return jnp.dot(a, b, precision=HI, preferred_element_type=F32)


def _dot_t_hi(a, b):
    return lax.dot_general(a, b, (((0,), (0,)), ((), ())), precision=HI, preferred_element_type=F32)


def _s5_param_kernel(lr_ref, li_ref, lrc_ref, lic_ref, ldt_ref, br_ref, bi_ref, cr_ref, ci_ref,
                     m_ref, qre_ref, qim_ref, pre_ref, pim_ref, at_ref):
    t_len = S5_T
    tp = t_len + 8
    wid = t_len * S5_CH
    col = lax.broadcasted_iota(jnp.int32, (tp, wid), 1)
    kid = lax.broadcasted_iota(jnp.int32, (tp, wid), 0)
    cq = col >> 4
    rep_nat = (kid == cq).astype(F32)
    rep_rev = (kid == t_len - 1 - cq).astype(F32)
    til = (lax.broadcasted_iota(jnp.int32, (S5_CH, wid), 0)
           == (lax.broadcasted_iota(jnp.int32, (S5_CH, wid), 1) & (S5_CH - 1))).astype(F32)
    colb = lax.broadcasted_iota(jnp.int32, (S5_CH, wid), 1) >> 4
    kk = lax.broadcasted_iota(jnp.int32, (tp, S5_STATE), 0).astype(F32)
    rows = [None] * t_len
    for d in range(2):
        dt = jnp.exp(ldt_ref[d])
        lr, li = lr_ref[d], li_ref[d]
        mag = jnp.exp(kk * (lr * dt))
        ang = kk * (li * dt)
        pr, pi_ = mag * jnp.cos(ang), mag * jnp.sin(ang)
        lrc, lic = lrc_ref[d], lic_ref[d]
        magc = jnp.exp(lrc * dt)
        arc, aic = magc * jnp.cos(lic * dt), magc * jnp.sin(lic * dt)
        den = lrc * lrc + lic * lic
        nr, ni = arc - 1.0, aic
        kre = (nr * lrc + ni * lic) / den
        kim = (ni * lrc - nr * lic) / den
        bbr = kre * br_ref[d] - kim * bi_ref[d]
        bbi = kre * bi_ref[d] + kim * br_ref[d]
        repq = rep_rev if d == 0 else rep_nat
        repp = ((kid == cq + 1) if d == 0 else (kid == t_len - cq)).astype(F32)
        repm = rep_nat if d == 0 else rep_rev
        bt_r, bt_i = _dot_hi(bbr, til), _dot_hi(bbi, til)
        eqr, eqi = _dot_t_hi(pr, repq), _dot_t_hi(pi_, repq)
        qre_ref[d] = (eqr * bt_r - eqi * bt_i).astype(qre_ref.dtype)
        qim_ref[d] = (eqr * bt_i + eqi * bt_r).astype(qim_ref.dtype)
        epr, epi = _dot_t_hi(pr, repp), _dot_t_hi(pi_, repp)
        ct_r, ct_i = _dot_t_hi(cr_ref[d], til), _dot_t_hi(ci_ref[d], til)
        pre_ref[d] = (ct_r * epr - ct_i * epi).astype(pre_ref.dtype)
        pim_ref[d] = (-(ct_r * epi + ct_i * epr)).astype(pim_ref.dtype)
        at_ref[d, 0:1, :] = pr[t_len:t_len + 1, :]
        at_ref[d, 1:2, :] = pi_[t_len:t_len + 1, :]
        emr, emi = _dot_t_hi(pr, repm), _dot_t_hi(pi_, repm)
        y_re = emr * ct_r - emi * ct_i
        y_im = emr * ct_i + emi * ct_r
        r = _dot_t_hi(bbr, y_re) - _dot_t_hi(bbi, y_im)
        for s in range(t_len):
            if d == 0:
                blk = jnp.where(colb >= s, pltpu.roll(r, S5_CH * s, 1), 0.0)
            else:
                blk = jnp.where(colb <= s, pltpu.roll(r, (S5_CH * (s + 1)) % wid, 1), 0.0)
            rows[s] = blk if rows[s] is None else rows[s] + blk
    for s in range(t_len):
        m_ref[s * S5_CH:(s + 1) * S5_CH, :] = rows[s].astype(m_ref.dtype)


def _s5_params(lam_re, lam_im, log_dt, b_re, b_im, c_re, c_im):
    g, p, n = S5_GROUPS, S5_STATE, S5_CH
    wid = S5_T * S5_CH

    def spec(*shape):
        return pl.BlockSpec((2, None) + shape, lambda gi: (0, gi) + (0,) * len(shape))

    def ospec(*shape):
        return pl.BlockSpec((None,) + shape, lambda gi: (gi,) + (0,) * len(shape))
    return pl.pallas_call(
        _s5_param_kernel, grid=(g,),
        in_specs=[spec(1, p), spec(1, p), spec(p, 1), spec(p, 1), spec(1, 1),
                  spec(p, n), spec(p, n), spec(n, p), spec(n, p)],
        out_specs=[ospec(wid, wid), ospec(2, p, wid), ospec(2, p, wid), ospec(2, p, wid), ospec(2, p, wid),
                   ospec(2, 2, p)],
        out_shape=[jax.ShapeDtypeStruct((g, wid, wid), BF16),
                   jax.ShapeDtypeStruct((g, 2, p, wid), BF16),
                   jax.ShapeDtypeStruct((g, 2, p, wid), BF16),
                   jax.ShapeDtypeStruct((g, 2, p, wid), BF16),
                   jax.ShapeDtypeStruct((g, 2, p, wid), BF16),
                   jax.ShapeDtypeStruct((g, 2, 2, p), F32)],
        compiler_params=_params(("parallel",), 32),
        name="s5_params",
    )(lam_re.reshape(2, g, 1, p), lam_im.reshape(2, g, 1, p), lam_re.reshape(2, g, p, 1),
      lam_im.reshape(2, g, p, 1), log_dt.reshape(2, g, 1, 1), b_re, b_im, c_re, c_im)


def _s5_state_kernel(u_ref, qre_ref, qim_ref, s_ref):
    u = u_ref[...]
    dn = (((1,), (1,)), ((), ()))
    s_ref[:, 0:128] = lax.dot_general(u, qre_ref[...], dn, preferred_element_type=F32)
    s_ref[:, 128:256] = lax.dot_general(u, qim_ref[...], dn, preferred_element_type=F32)


def _s5_scan_kernel(s_ref, are_ref, aim_ref, h_ref, *, nc, ncc):
    ar, ai = are_ref[...], aim_ref[...]
    is_f = lax.broadcasted_iota(jnp.int32, ar.shape, 1) < S5_STATE

    def body(i, carry):
        hr, hi = carry
        cf = i
        cb = jnp.where(i < ncc, ncc - 1 - i, nc + ncc - 1 - i)
        h_ref[cf, :, 0:64] = hr[:, 0:64]
        h_ref[cb, :, 64:128] = hr[:, 64:128]
        h_ref[cf, :, 128:192] = hi[:, 0:64]
        h_ref[cb, :, 192:256] = hi[:, 64:128]
        sf, sb = s_ref[cf], s_ref[cb]
        sr = jnp.where(is_f, sf[:, 0:128], sb[:, 0:128])
        si = jnp.where(is_f, sf[:, 128:256], sb[:, 128:256])
        return ar * hr - ai * hi + sr, ar * hi + ai * hr + si

    zero = jnp.zeros(ar.shape, F32)
    lax.fori_loop(0, nc, body, (zero, zero))


def _s5_out_kernel(u_ref, m_ref, h_ref, p_ref, y_ref):
    y = jnp.dot(u_ref[...], m_ref[...], preferred_element_type=F32)
    y_ref[...] = y + jnp.dot(h_ref[...].astype(BF16), p_ref[...], preferred_element_type=F32)


def _s5_mixer(s_all, m, qre, qim, pre, pim, at, n_ctx):
    n = s_all.shape[0]
    g, t_len, wid = S5_GROUPS, S5_T, S5_T * S5_CH
    nc, ncc = n // t_len, n_ctx // t_len
    assert n % t_len == 0 and n_ctx % t_len == 0
    u = s_all.reshape(nc, t_len, g, S5_CH).transpose(2, 0, 1, 3).reshape(g, nc, wid)
    p_all = jnp.concatenate([pre.reshape(g, 128, wid), pim.reshape(g, 128, wid)], axis=1)
    a_re = at[:, :, 0, :].reshape(g, 128)
    a_im = at[:, :, 1, :].reshape(g, 128)
    s_t = pl.pallas_call(
        _s5_state_kernel, grid=(g,),
        in_specs=[pl.BlockSpec((None, nc, wid), lambda i: (i, 0, 0)),
                  pl.BlockSpec((None, 128, wid), lambda i: (i, 0, 0)),
                  pl.BlockSpec((None, 128, wid), lambda i: (i, 0, 0))],
        out_specs=pl.BlockSpec((nc, 256), lambda i: (0, i)),
        out_shape=jax.ShapeDtypeStruct((nc, g * 256), F32),
        compiler_params=_params(("parallel",), 32),
        name="s5_chunk_state",
    )(u, qre.reshape(g, 128, wid), qim.reshape(g, 128, wid))
    h_t = pl.pallas_call(
        functools.partial(_s5_scan_kernel, nc=nc, ncc=ncc),
        out_shape=jax.ShapeDtypeStruct((nc, g, 256), F32),
        compiler_params=pltpu.CompilerParams(vmem_limit_bytes=48 * MIB),
        name="s5_chunk_scan",
    )(s_t.reshape(nc, g, 256), a_re, a_im).reshape(nc, g * 256)
    y = pl.pallas_call(
        _s5_out_kernel, grid=(g,),
        in_specs=[pl.BlockSpec((None, nc, wid), lambda i: (i, 0, 0)),
                  pl.BlockSpec((None, wid, wid), lambda i: (i, 0, 0)),
                  pl.BlockSpec((nc, 256), lambda i: (0, i)),
                  pl.BlockSpec((None, 256, wid), lambda i: (i, 0, 0))],
        out_specs=pl.BlockSpec((None, nc, wid), lambda i: (i, 0, 0)),
        out_shape=jax.ShapeDtypeStruct((g, nc, wid), F32),
        compiler_params=_params(("parallel",), 32),
        name="s5_chunk_out",
    )(u, m, h_t, p_all)
    return y.reshape(g, nc, t_len, S5_CH).transpose(1, 2, 0, 3).reshape(n, W_S5)


def _gelu_tanh(x):
    return 0.5 * x * (1.0 + jnp.tanh(math.sqrt(2.0 / math.pi) * (x + 0.044715 * (x * x * x))))


def _even_out_kernel(pool_ref, y_ref, s_ref, dsk_ref, wglu_ref, wo_ref, h_ref, nw_ref, mod_ref, o_ref,
                     *, n_ctx, tm):
    y = _gelu_tanh(y_ref[...] + s_ref[...].astype(F32) * dsk_ref[...])
    gate = jnp.dot(y.astype(BF16), wglu_ref[...], preferred_element_type=F32)
    s5 = (y * _sigmoid(gate)).astype(BF16)
    mix = jnp.dot(pool_ref[...], wo_ref[0:W_POOL, :], preferred_element_type=F32)
    mix = mix + jnp.dot(s5, wo_ref[W_POOL:W_POOL + W_S5, :], preferred_element_type=F32)
    g = _gate_rows(mod_ref, pl.program_id(0) * tm, tm, n_ctx)
    o_ref[...] = h_ref[...] + g * _rms(mix, nw_ref[...])


def _even_out(pool_out, y_s5, z, dsk, w_glu, w_out, h, nw, mod, n_ctx):
    n, d = h.shape
    tm = _pick(n, (320, 256, 128))
    return pl.pallas_call(
        functools.partial(_even_out_kernel, n_ctx=n_ctx, tm=tm), grid=(n // tm,),
        in_specs=[pl.BlockSpec((tm, W_POOL), lambda i: (i, 0)),
                  pl.BlockSpec((tm, W_S5), lambda i: (i, 0)),
                  pl.BlockSpec((tm, W_S5), lambda i: (i, W_POOL // W_S5)),
                  pl.BlockSpec((1, W_S5), lambda i: (0, 0)),
                  pl.BlockSpec((W_S5, W_S5), lambda i: (0, 0)),
                  pl.BlockSpec((d, d), lambda i: (0, 0)),
                  pl.BlockSpec((tm, d), lambda i: (i, 0)),
                  pl.BlockSpec((1, d), lambda i: (0, 0)),
                  pl.BlockSpec((8, d), lambda i: (0, 0))],
        out_specs=pl.BlockSpec((tm, d), lambda i: (i, 0)),
        out_shape=jax.ShapeDtypeStruct((n, d), F32),
        compiler_params=_params(("parallel",), 48),
        name="even_mixer_out",
    )(pool_out, y_s5, z, dsk.reshape(1, W_S5), w_glu, w_out, h, nw.reshape(1, d), mod)


def _ffn_kernel(h_ref, nw2_ref, nw3_ref, mod_ref, wg_ref, wu_ref, wd_ref, o_ref, v_ref, *, n_ctx, tm):
    f = pl.program_id(1)

    @pl.when(f == 0)
    def _():
        v = _norm_mod(h_ref[...], nw2_ref[...], mod_ref, pl.program_id(0) * tm, n_ctx)
        v_ref[...] = v.astype(BF16)
        o_ref[...] = jnp.zeros_like(o_ref)
    v = v_ref[...]
    a = jnp.dot(v, wg_ref[...], preferred_element_type=F32)
    b = jnp.dot(v, wu_ref[...], preferred_element_type=F32)
    act = (a * _sigmoid(a) * b).astype(BF16)
    o_ref[...] += jnp.dot(act, wd_ref[...], preferred_element_type=F32)

    @pl.when(f == pl.num_programs(1) - 1)
    def _():
        g = _gate_rows(mod_ref, pl.program_id(0) * tm, tm, n_ctx)
        o_ref[...] = h_ref[...] + g * _rms(o_ref[...], nw3_ref[...])


def _dense_ffn(h, nw2, nw3, mod, wg, wu, wd, n_ctx):
    n, d = h.shape
    dff = wg.shape[1]
    tm = _pick(n, (640, 512, 256, 128))
    tf = 512
    return pl.pallas_call(
        functools.partial(_ffn_kernel, n_ctx=n_ctx, tm=tm), grid=(n // tm, dff // tf),
        in_specs=[pl.BlockSpec((tm, d), lambda i, f: (i, 0)),
                  pl.BlockSpec((1, d), lambda i, f: (0, 0)),
                  pl.BlockSpec((1, d), lambda i, f: (0, 0)),
                  pl.BlockSpec((8, d), lambda i, f: (0, 0)),
                  pl.BlockSpec((d, tf), lambda i, f: (0, f)),
                  pl.BlockSpec((d, tf), lambda i, f: (0, f)),
                  pl.BlockSpec((tf, d), lambda i, f: (f, 0))],
        out_specs=pl.BlockSpec((tm, d), lambda i, f: (i, 0)),
        out_shape=jax.ShapeDtypeStruct((n, d), F32),
        scratch_shapes=[pltpu.VMEM((tm, d), BF16)],
        compiler_params=_params(("parallel", "arbitrary"), 56),
        name="dense_swiglu",
    )(h, nw2.reshape(1, d), nw3.reshape(1, d), mod, wg, wu, wd)


def _split_bf16(x):
    hi = x.astype(BF16)
    return hi, (x - hi.astype(F32)).astype(BF16)


def _gla_kernel(qf_ref, kf_ref, vf_ref, af_ref, qb_ref, kb_ref, vb_ref, ab_ref, wa_ref, ba_ref,
                of_ref, ob_ref, st_ref, *, tb):
    c = GLA_CHUNK
    nsub = tb // c

    @pl.when(pl.program_id(0) == 0)
    def _():
        st_ref[...] = jnp.zeros_like(st_ref)
    ri = lax.broadcasted_iota(jnp.int32, (c, c), 0)
    ci = lax.broadcasted_iota(jnp.int32, (c, c), 1)
    scale = GLA_DK ** -0.5
    nt = (((1,), (1,)), ((), ()))
    tn = (((0,), (0,)), ((), ()))
    dirs = ((qf_ref, kf_ref, vf_ref, af_ref, of_ref), (qb_ref, kb_ref, vb_ref, ab_ref, ob_ref))
    for d, (q_ref, k_ref, v_ref, a_ref, o_ref) in enumerate(dirs):
        mask = (ri >= ci) if d == 0 else (ci >= ri)
        cum = mask.astype(BF16)
        a_hi, a_lo = _split_bf16(a_ref[...])
        w_hi, w_lo = _split_bf16(wa_ref[d])
        z = (jnp.dot(a_hi, w_hi, preferred_element_type=F32) + jnp.dot(a_hi, w_lo, preferred_element_type=F32)
             + jnp.dot(a_lo, w_hi, preferred_element_type=F32)) + ba_ref[d]
        glog = (jnp.minimum(z, 0.0) - jnp.log(1.0 + jnp.exp(-jnp.abs(z)))) * (1.0 / GLA_TAU)
        order = range(nsub) if d == 0 else range(nsub - 1, -1, -1)
        for sc in order:
            r0 = sc * c
            g_hi, g_lo = _split_bf16(glog[r0:r0 + c, :])
            b = jnp.dot(cum, g_hi, preferred_element_type=F32) + jnp.dot(cum, g_lo, preferred_element_type=F32)
            b_last = b[c - 1:c, :] if d == 0 else b[0:1, :]
            dec = jnp.exp(b_last)
            q = q_ref[r0:r0 + c, :].astype(F32) * scale
            k = k_ref[r0:r0 + c, :].astype(F32)
            qe = (q * jnp.exp(b)).astype(BF16)
            ke = (k * jnp.exp(-b)).astype(BF16)
            kd = (k * jnp.exp(b_last - b)).astype(BF16)
            for h in range(GLA_HEADS):
                ks = slice(h * GLA_DK, (h + 1) * GLA_DK)
                vs = slice(h * GLA_DV, (h + 1) * GLA_DV)
                att = lax.dot_general(qe[:, ks], ke[:, ks], nt, preferred_element_type=F32)
                att = jnp.where(mask, att, 0.0).astype(BF16)
                vh = v_ref[r0:r0 + c, vs]
                s_old = st_ref[d, h]
                o = jnp.dot(att, vh, preferred_element_type=F32)
                o = o + lax.dot_general(qe[:, ks], s_old.astype(BF16), nt, preferred_element_type=F32)
                o_ref[r0:r0 + c, vs] = o
                upd = lax.dot_general(vh, kd[:, ks], tn, preferred_element_type=F32)
                st_ref[d, h] = s_old * dec[:, ks] + upd


def _gla_mixer(zmain, a_aux, w_a, b_a, n_ctx):
    n = zmain.shape[0]
    tb = 256
    assert n % tb == 0 and n_ctx % tb == 0
    nb, ncb = n // tb, n_ctx // tb

    def bwd(s):
        return jnp.where(s < ncb, ncb - 1 - s, nb + ncb - 1 - s)
    wa = jnp.pad(w_a, ((0, 0), (0, 128 - GLA_RANK), (0, 0)))
    in_specs = []
    for order in (lambda s: s, bwd):
        in_specs += [pl.BlockSpec((tb, W_GLA_K), lambda s, o=order: (o(s), OQ // W_GLA_K)),
                     pl.BlockSpec((tb, W_GLA_K), lambda s, o=order: (o(s), OK_ // W_GLA_K)),
                     pl.BlockSpec((tb, W_GLA_V), lambda s, o=order: (o(s), OV // W_GLA_V)),
                     pl.BlockSpec((tb, 128), lambda s, o=order: (o(s), 0))]
    in_specs += [pl.BlockSpec((2, 128, W_GLA_K), lambda s: (0, 0, 0)),
                 pl.BlockSpec((2, 1, W_GLA_K), lambda s: (0, 0, 0))]
    return pl.pallas_call(
        functools.partial(_gla_kernel, tb=tb), grid=(nb,),
        in_specs=in_specs,
        out_specs=[pl.BlockSpec((tb, W_GLA_V), lambda s: (s, 0)),
                   pl.BlockSpec((tb, W_GLA_V), lambda s: (bwd(s), 0))],
        out_shape=[jax.ShapeDtypeStruct((n, W_GLA_V), F32)] * 2,
        scratch_shapes=[pltpu.VMEM((2, GLA_HEADS, GLA_DV, GLA_DK), F32)],
        compiler_params=_params(("arbitrary",), 32),
        name="gla_chunked",
    )(zmain, zmain, zmain, a_aux, zmain, zmain, zmain, a_aux, wa, b_a.reshape(2, 1, W_GLA_K))


_ATT_PAD = 2 * ATT_HEAD_DIM
_LOG2E = 1.4426950408889634
_FIXED_SHIFT_MAX_BOUND = 40.0


def _rope_fn(row0, tm, n_ctx):
    hd = ATT_HEAD_DIM
    row = row0 + lax.broadcasted_iota(jnp.int32, (tm, hd), 0)
    lane = lax.broadcasted_iota(jnp.int32, (tm, hd), 1)
    t = row - n_ctx
    pos = jnp.where(lane < hd // 2, t // GRID_W, t % GRID_W).astype(F32)
    quarter = hd // 4
    freq = jnp.exp((lane % quarter).astype(F32) * (-math.log(ROPE_THETA) / quarter))
    ang = pos * freq
    first = (lane & quarter) == 0
    cosv = jnp.cos(ang)
    sin_s = jnp.where(first, -jnp.sin(ang), jnp.sin(ang))

    def rope(x):
        sw = jnp.where(first, pltpu.roll(x, hd - quarter, 1), pltpu.roll(x, quarter, 1))
        return x * cosv + sw * sin_s
    return rope, row >= n_ctx


def _k_prep_kernel(k_ref, v_ref, kw_ref, kx_ref, vx_ref, kmax_ref, *, n_ctx, tm):
    hd = ATT_HEAD_DIM
    rope, is_lat = _rope_fn(pl.program_id(0) * tm, tm, n_ctx)
    lane = lax.broadcasted_iota(jnp.int32, (tm, hd), 1)
    one_col = jnp.where(lane == 0, 1.0, 0.0).astype(BF16)
    ones = jnp.ones((tm, hd), BF16)
    nmax = jnp.zeros((tm, 1), F32)
    for h in range(ATT_KV_HEADS):
        sl = slice(h * hd, (h + 1) * hd)
        xn = _rms(k_ref[:, sl].astype(F32), kw_ref[...])
        kr = jnp.where(is_lat, rope(xn), xn).astype(BF16)
        kx_ref[:, h * _ATT_PAD:h * _ATT_PAD + hd] = kr
        kx_ref[:, h * _ATT_PAD + hd:(h + 1) * _ATT_PAD] = one_col
        vx_ref[:, h * _ATT_PAD:h * _ATT_PAD + hd] = v_ref[:, sl]
        vx_ref[:, h * _ATT_PAD + hd:(h + 1) * _ATT_PAD] = ones
        krf = kr.astype(F32)
        nmax = jnp.maximum(nmax, jnp.sum(krf * krf, axis=-1, keepdims=True))
    kmax_ref[...] = jnp.broadcast_to(jnp.max(nmax, axis=0, keepdims=True), kmax_ref.shape)


def _q_prep_kernel(q_ref, qw_ref, kn_ref, qx_ref, bmax_ref, *, n_ctx, tm):
    hd = ATT_HEAD_DIM
    rope, _ = _rope_fn(n_ctx + pl.program_id(0) * tm, tm, n_ctx)
    lane = lax.broadcasted_iota(jnp.int32, (tm, hd), 1)
    scale = hd ** -0.5 * _LOG2E
    bmax = jnp.zeros((tm, 1), F32)
    for h in range(ATT_HEADS):
        sl = slice(h * hd, (h + 1) * hd)
        xn = _rms(q_ref[:, sl].astype(F32), qw_ref[...])
        qr = (rope(xn) * scale).astype(BF16)
        qf = qr.astype(F32)
        bound = jnp.sqrt(jnp.sum(qf * qf, axis=-1, keepdims=True)) * kn_ref[...]
        qx_ref[:, h * _ATT_PAD:h * _ATT_PAD + hd] = qr
        qx_ref[:, h * _ATT_PAD + hd:(h + 1) * _ATT_PAD] = jnp.where(lane == 0, -bound, 0.0).astype(BF16)
        bmax = jnp.maximum(bmax, bound)
    bmax_ref[...] = jnp.broadcast_to(jnp.max(bmax, axis=0, keepdims=True), bmax_ref.shape)


def _qk_prep(zmain, q_norm, k_norm, n_ctx):
    n = zmain.shape[0]
    lq = n - n_ctx
    tm = 256
    hd = ATT_HEAD_DIM
    assert n_ctx % tm == 0
    ncb = n_ctx // tm
    kx, vx, kmax = pl.pallas_call(
        functools.partial(_k_prep_kernel, n_ctx=n_ctx, tm=tm), grid=(n // tm,),
        in_specs=[pl.BlockSpec((tm, W_ATT_KV), lambda i: (i, OAK // W_ATT_KV)),
                  pl.BlockSpec((tm, W_ATT_KV), lambda i: (i, OAV // W_ATT_KV)),
                  pl.BlockSpec((1, hd), lambda i: (0, 0))],
        out_specs=[pl.BlockSpec((tm, ATT_KV_HEADS * _ATT_PAD), lambda i: (i, 0)),
                   pl.BlockSpec((tm, ATT_KV_HEADS * _ATT_PAD), lambda i: (i, 0)),
                   pl.BlockSpec((None, 8, 128), lambda i: (i, 0, 0))],
        out_shape=[jax.ShapeDtypeStruct((n, ATT_KV_HEADS * _ATT_PAD), BF16),
                   jax.ShapeDtypeStruct((n, ATT_KV_HEADS * _ATT_PAD), BF16),
                   jax.ShapeDtypeStruct((n // tm, 8, 128), F32)],
        compiler_params=_params(("parallel",), 32),
        name="k_norm_rope",
    )(zmain, zmain, k_norm.reshape(1, hd))
    knorm = jnp.sqrt(jnp.max(kmax)).reshape(1, 1)
    qx, bmax = pl.pallas_call(
        functools.partial(_q_prep_kernel, n_ctx=n_ctx, tm=tm), grid=(lq // tm,),
        in_specs=[pl.BlockSpec((tm, W_ATT), lambda i: (i + ncb, OAQ // W_ATT)),
                  pl.BlockSpec((1, hd), lambda i: (0, 0)),
                  pl.BlockSpec((1, 1), lambda i: (0, 0))],
        out_specs=[pl.BlockSpec((tm, ATT_HEADS * _ATT_PAD), lambda i: (i, 0)),
                   pl.BlockSpec((None, 8, 128), lambda i: (i, 0, 0))],
        out_shape=[jax.ShapeDtypeStruct((lq, ATT_HEADS * _ATT_PAD), BF16),
                   jax.ShapeDtypeStruct((lq // tm, 8, 128), F32)],
        compiler_params=_params(("parallel",), 32),
        name="q_norm_rope",
    )(zmain, q_norm.reshape(1, hd), knorm)
    return qx, kx, vx, jnp.max(bmax)


def _flash_fixed_kernel(q_ref, kt_ref, v_ref, o_ref, acc_ref):
    c = pl.program_id(2)
    hd = ATT_HEAD_DIM

    @pl.when(c == 0)
    def _():
        acc_ref[...] = jnp.zeros_like(acc_ref)
    kt = kt_ref[...]
    v = v_ref[...]
    for g in range(ATT_GROUP):
        s = jnp.dot(q_ref[:, g * _ATT_PAD:(g + 1) * _ATT_PAD], kt, preferred_element_type=F32)
        acc_ref[g] += jnp.dot(jnp.exp2(s).astype(BF16), v, preferred_element_type=F32)

    @pl.when(c == pl.num_programs(2) - 1)
    def _():
        for g in range(ATT_GROUP):
            a = acc_ref[g]
            o_ref[:, g * hd:(g + 1) * hd] = (a[:, 0:hd] / a[:, hd:2 * hd]).astype(o_ref.dtype)


def _flash_online_kernel(q_ref, kt_ref, v_ref, o_ref, m_ref, acc_ref):
    c = pl.program_id(2)
    hd = ATT_HEAD_DIM

    @pl.when(c == 0)
    def _():
        m_ref[...] = jnp.full(m_ref.shape, -jnp.inf, F32)
        acc_ref[...] = jnp.zeros_like(acc_ref)
    kt = kt_ref[...]
    v = v_ref[...]
    for g in range(ATT_GROUP):
        s = jnp.dot(q_ref[:, g * _ATT_PAD:(g + 1) * _ATT_PAD], kt, preferred_element_type=F32)
        m_prev = m_ref[g]
        m_new = jnp.maximum(m_prev, jnp.max(s, axis=-1, keepdims=True))
        p = jnp.exp2(s - m_new).astype(BF16)
        acc_ref[g] = jnp.exp2(m_prev - m_new) * acc_ref[g] + jnp.dot(p, v, preferred_element_type=F32)
        m_ref[g] = m_new

    @pl.when(c == pl.num_programs(2) - 1)
    def _():
        for g in range(ATT_GROUP):
            a = acc_ref[g]
            o_ref[:, g * hd:(g + 1) * hd] = (a[:, 0:hd] / a[:, hd:2 * hd]).astype(o_ref.dtype)


def _attention(qx, kx, vx, bound_log2):
    lq = qx.shape[0]
    n = kx.shape[0]
    kt = kx.T
    gw = ATT_GROUP * _ATT_PAD
    ow = ATT_GROUP * ATT_HEAD_DIM
    tq = _pick(lq, (512, 256, 128))

    def call(kern, tk, scratch, name):
        return pl.pallas_call(
            kern, grid=(ATT_KV_HEADS, lq // tq, n // tk),
            in_specs=[pl.BlockSpec((tq, gw), lambda h, i, j: (i, h)),
                      pl.BlockSpec((_ATT_PAD, tk), lambda h, i, j: (h, j)),
                      pl.BlockSpec((tk, _ATT_PAD), lambda h, i, j: (j, h))],
            out_specs=pl.BlockSpec((tq, ow), lambda h, i, j: (i, h)),
            out_shape=jax.ShapeDtypeStruct((lq, W_ATT), BF16),
            scratch_shapes=scratch + [pltpu.VMEM((ATT_GROUP, tq, _ATT_PAD), F32)],
            compiler_params=_params(("parallel", "parallel", "arbitrary"), 48),
            name=name,
        )(qx, kt, vx)

    def fixed(_):
        return call(_flash_fixed_kernel, _pick(n, (3328, 1280, 512, 256)), [], "gqa_flash_fixed_shift")

    def online(_):
        return call(_flash_online_kernel, _pick(n, (640, 512, 256, 128)),
                    [pltpu.VMEM((ATT_GROUP, tq, 1), F32)], "gqa_flash_online")
    return lax.cond(bound_log2 <= _FIXED_SHIFT_MAX_BOUND * _LOG2E, fixed, online, None)


def _odd_out_kernel(of_ref, ob_ref, r_ref, att_ref, gn_ref, wo_ref, h_ref, nw_ref, mod_ref, o_ref):
    o = of_ref[...] + ob_ref[...]
    r = r_ref[...].astype(F32)
    silu_r = r * _sigmoid(r)
    parts = []
    for h in range(GLA_HEADS):
        vs = slice(h * GLA_DV, (h + 1) * GLA_DV)
        parts.append((_rms(o[:, vs], gn_ref[:, vs]) * silu_r[:, vs]).astype(BF16))
    gla = jnp.concatenate(parts, axis=1)
    mix = jnp.dot(gla, wo_ref[0:W_GLA_V, :], preferred_element_type=F32)
    mix = mix + jnp.dot(att_ref[...], wo_ref[W_GLA_V:W_GLA_V + W_ATT, :], preferred_element_type=F32)
    o_ref[...] = h_ref[...] + mod_ref[4:5, :] * _rms(mix, nw_ref[...])


def _odd_out(o_f, o_b, zmain, att, gla_norm, w_out, h_all, nw, mod, n_ctx):
    n, d = h_all.shape
    lq = n - n_ctx
    tm = _pick(lq, (256, 128))
    assert n_ctx % tm == 0
    off = n_ctx // tm
    return pl.pallas_call(
        _odd_out_kernel, grid=(lq // tm,),
        in_specs=[pl.BlockSpec((tm, W_GLA_V), lambda i: (i + off, 0)),
                  pl.BlockSpec((tm, W_GLA_V), lambda i: (i + off, 0)),
                  pl.BlockSpec((tm, W_GLA_V), lambda i: (i + off, OR // W_GLA_V)),
                  pl.BlockSpec((tm, W_ATT), lambda i: (i, 0)),
                  pl.BlockSpec((1, W_GLA_V), lambda i: (0, 0)),
                  pl.BlockSpec((d, d), lambda i: (0, 0)),
                  pl.BlockSpec((tm, d), lambda i: (i + off, 0)),
                  pl.BlockSpec((1, d), lambda i: (0, 0)),
                  pl.BlockSpec((8, d), lambda i: (0, 0))],
        out_specs=pl.BlockSpec((tm, d), lambda i: (i, 0)),
        out_shape=jax.ShapeDtypeStruct((lq, d), F32),
        compiler_params=_params(("parallel",), 48),
        name="odd_mixer_out",
    )(o_f, o_b, zmain, att, gla_norm.reshape(1, W_GLA_V), w_out, h_all, nw.reshape(1, d), mod)


def _router_kernel(h_ref, nw_ref, mod_ref, wr_ref, br_ref, v_ref, r_ref):
    x = h_ref[...]
    v = _rms(x, nw_ref[...]) * (1.0 + mod_ref[0:1, :]) + mod_ref[1:2, :]
    v_ref[...] = v
    logits = _dot_hi(v, wr_ref[...]) + br_ref[...]
    lane = lax.broadcasted_iota(jnp.int32, logits.shape, 1)
    m1 = jnp.max(logits, axis=-1, keepdims=True)
    i1 = jnp.min(jnp.where(logits == m1, lane, 128), axis=-1, keepdims=True)
    rest = jnp.where(lane == i1, -jnp.inf, logits)
    m2 = jnp.max(rest, axis=-1, keepdims=True)
    i2 = jnp.min(jnp.where(rest == m2, lane, 128), axis=-1, keepdims=True)
    e2 = jnp.exp(m2 - m1)
    g1 = 1.0 / (1.0 + e2)
    g2 = e2 / (1.0 + e2)
    out = jnp.where(lane == 0, i1.astype(F32), 0.0)
    out = jnp.where(lane == 1, i2.astype(F32), out)
    out = jnp.where(lane == 2, g1, out)
    out = jnp.where(lane == 3, g2, out)
    r_ref[...] = out


def _router(h_lat, nw, mod, w_router, b_router):
    lq, d = h_lat.shape
    tm = _pick(lq, (512, 256, 128))
    wr = jnp.pad(w_router, ((0, 0), (0, 128 - N_EXPERTS)))
    br = jnp.pad(b_router, (0, 128 - N_EXPERTS), constant_values=-1e30).reshape(1, 128)
    return pl.pallas_call(
        _router_kernel, grid=(lq // tm,),
        in_specs=[pl.BlockSpec((tm, d), lambda i: (i, 0)),
                  pl.BlockSpec((1, d), lambda i: (0, 0)),
                  pl.BlockSpec((8, d), lambda i: (0, 0)),
                  pl.BlockSpec((d, 128), lambda i: (0, 0)),
                  pl.BlockSpec((1, 128), lambda i: (0, 0))],
        out_specs=[pl.BlockSpec((tm, d), lambda i: (i, 0)), pl.BlockSpec((tm, 128), lambda i: (i, 0))],
        out_shape=[jax.ShapeDtypeStruct((lq, d), F32), jax.ShapeDtypeStruct((lq, 128), F32)],
        compiler_params=_params(("parallel",), 40),
        name="router_top2",
    )(h_lat, nw.reshape(1, d), mod, wr, br)


def _expert_kernel(te_ref, tv_ref, src_ref, dst_ref, v_hbm, gt_ref, wg_ref, wu_ref, wd_ref, y_hbm,
                   xg_ref, xb_ref, ys_ref, acc_ref, gsem, ssem, *, tm, nt, nf):
    j = pl.program_id(0)
    f = pl.program_id(1)
    slot = j % 2
    per_step = tm // nf
    lo = f * per_step
    valid = tv_ref[j] == 1

    def start_gather(tile, slot_, row):
        tok = src_ref[tile * tm + row]
        pltpu.make_async_copy(v_hbm.at[pl.ds(tok, 1)], xg_ref.at[slot_, pl.ds(row, 1)], gsem.at[slot_]).start()

    def start_scatter(row):
        dst = dst_ref[j * tm + row]
        pltpu.make_async_copy(ys_ref.at[pl.ds(row, 1)], y_hbm.at[pl.ds(dst, 1)], ssem.at[0]).start()

    def gather_loop(tile, slot_, first, count):
        def body(r, carry):
            start_gather(tile, slot_, first + r)
            return carry
        lax.fori_loop(0, count, body, 0, unroll=8)

    @pl.when((j == 0) & (f == 0))
    def _():
        gather_loop(0, 0, 0, tm)
        ys_ref[...] = jnp.zeros_like(ys_ref)

    @pl.when(f == 0)
    def _():
        pltpu.make_async_copy(v_hbm.at[pl.ds(0, tm)], xg_ref.at[slot], gsem.at[slot]).wait()
        xb_ref[...] = xg_ref[slot].astype(BF16)
        acc_ref[...] = jnp.zeros_like(acc_ref)

    @pl.when(valid)
    def _():
        groups = 4
        per_group = per_step // groups

        def start_group(g, after):
            for r in range(g * per_group, (g + 1) * per_group):
                start_gather(j + 1, 1 - slot, lo + r + after)
                start_scatter(lo + r + after)

        def zero_after(val):
            bits = pltpu.bitcast(val[0:8, 0:128], jnp.int32)
            return ((bits & 1) >> 1)[0, 0]

        start_group(0, 0)
        x = xb_ref[...]
        a = jnp.dot(x, wg_ref[...], preferred_element_type=F32)
        start_group(1, zero_after(a))
        b = jnp.dot(x, wu_ref[...], preferred_element_type=F32)
        start_group(2, zero_after(b))
        act = (a * _sigmoid(a) * b).astype(BF16)
        down = jnp.dot(act, wd_ref[...], preferred_element_type=F32)
        start_group(3, zero_after(down))
        acc_ref[...] += down

    @pl.when(jnp.logical_not(valid))
    def _():
        @pl.when(j + 1 < nt)
        def _():
            gather_loop(j + 1, 1 - slot, lo, per_step)

        def body(r, carry):
            start_scatter(lo + r)
            return carry
        lax.fori_loop(0, per_step, body, 0, unroll=8)

    @pl.when(f == nf - 1)
    def _():
        pltpu.make_async_copy(ys_ref, y_hbm.at[pl.ds(0, tm)], ssem.at[0]).wait()
        ys_ref[...] = acc_ref[...] * gt_ref[...]


def _combine_kernel(y0_ref, y1_ref, h_ref, nw_ref, mod_ref, o_ref):
    o_ref[...] = h_ref[...] + mod_ref[4:5, :] * _rms(y0_ref[...] + y1_ref[...], nw_ref[...])


def _count_le(sorted_vals, queries):
    return jnp.sum((sorted_vals[None, :] <= queries[:, None]).astype(jnp.int32), axis=1)


def _moe(h_lat, nw2, nw3, mod, w_router, b_router, wg, wu, wd):
    lq, d = h_lat.shape
    e = N_EXPERTS
    dff = wg.shape[2]
    tm = _pick(lq, (512, 256))
    tf = 896
    v, route = _router(h_lat, nw2, mod, w_router, b_router)
    idx = route[:, 0:2].astype(jnp.int32)
    gates = route[:, 2:4]

    npairs = 2 * lq
    nt = npairs // tm + e + 1
    flat_e = idx.reshape(-1)
    pair_id = jnp.arange(npairs, dtype=jnp.int32)
    onehot = (flat_e[:, None] == jnp.arange(e, dtype=jnp.int32)[None, :]).astype(jnp.int32)
    csum = jnp.cumsum(onehot, axis=0)
    rank = jnp.sum(csum * onehot, axis=1) - 1
    counts = csum[-1]
    padded = ((counts + tm - 1) // tm) * tm
    ends = jnp.cumsum(padded)
    starts = ends - padded
    pos = jnp.sum(onehot * starts[None, :], axis=1) + rank
    pair = jnp.stack([pair_id, lax.bitcast_convert_type(gates.reshape(-1), jnp.int32)], axis=1)
    empty = jnp.broadcast_to(jnp.array([-1, 0], jnp.int32), (nt * tm, 2))
    slots = empty.at[pos].set(pair)
    sorted_pair = slots[:, 0]
    sorted_gate = lax.bitcast_convert_type(slots[:, 1], F32)
    is_pad = sorted_pair < 0
    src_row = jnp.where(is_pad, 0, sorted_pair >> 1).astype(jnp.int32)
    pad_rank = jnp.cumsum(is_pad.astype(jnp.int32)) - 1
    dst_row = jnp.where(is_pad, npairs + pad_rank, (sorted_pair & 1) * lq + (sorted_pair >> 1)).astype(jnp.int32)
    dst_row = jnp.concatenate([(nt - 1) * tm + jnp.arange(tm, dtype=jnp.int32), dst_row])
    n_used = ends[-1] // tm
    tile_ids = jnp.arange(nt, dtype=jnp.int32)
    tile_valid = (tile_ids < n_used).astype(jnp.int32)
    tile_exp = jnp.minimum(_count_le(ends, jnp.minimum(tile_ids, n_used - 1) * tm), e - 1)

    nf = dff // tf
    assert tm % nf == 0
    pinned = lambda j, f, tv: jnp.where(tv[j] == 1, f, nf - 1)
    y = pl.pallas_call(
        functools.partial(_expert_kernel, tm=tm, nt=nt, nf=nf),
        grid_spec=pltpu.PrefetchScalarGridSpec(
            num_scalar_prefetch=4, grid=(nt, nf),
            in_specs=[pl.BlockSpec(memory_space=pl.ANY),
                      pl.BlockSpec((tm, 1), lambda j, f, te, tv, src, dst: (j, 0)),
                      pl.BlockSpec((None, d, tf), lambda j, f, te, tv, src, dst: (te[j], 0, pinned(j, f, tv))),
                      pl.BlockSpec((None, d, tf), lambda j, f, te, tv, src, dst: (te[j], 0, pinned(j, f, tv))),
                      pl.BlockSpec((None, tf, d), lambda j, f, te, tv, src, dst: (te[j], pinned(j, f, tv), 0))],
            out_specs=pl.BlockSpec(memory_space=pl.ANY),
            scratch_shapes=[pltpu.VMEM((2, tm, d), F32), pltpu.VMEM((tm, d), BF16), pltpu.VMEM((tm, d), F32),
                            pltpu.VMEM((tm, d), F32), pltpu.SemaphoreType.DMA((2,)),
                            pltpu.SemaphoreType.DMA((1,))]),
        out_shape=jax.ShapeDtypeStruct((nt * tm, d), F32),
        compiler_params=pltpu.CompilerParams(dimension_semantics=("arbitrary", "arbitrary"),
                                             vmem_limit_bytes=56 * MIB, disable_bounds_checks=True),
        name="moe_experts",
    )(tile_exp, tile_valid, src_row, dst_row, v, sorted_gate.reshape(nt * tm, 1), wg, wu, wd)

    tc = _pick(lq, (256, 128))
    return pl.pallas_call(
        _combine_kernel, grid=(lq // tc,),
        in_specs=[pl.BlockSpec((tc, d), lambda i: (i, 0)),
                  pl.BlockSpec((tc, d), lambda i: (i + lq // tc, 0)),
                  pl.BlockSpec((tc, d), lambda i: (i, 0)),
                  pl.BlockSpec((1, d), lambda i: (0, 0)),
                  pl.BlockSpec((8, d), lambda i: (0, 0))],
        out_specs=pl.BlockSpec((tc, d), lambda i: (i, 0)),
        out_shape=jax.ShapeDtypeStruct((lq, d), F32),
        compiler_params=_params(("parallel",), 32),
        name="moe_combine",
    )(y, y, h_lat, nw3.reshape(1, d), mod)


def kernel(x, c, ctx, c_ctx, w_mod, b_mod, norms, e_w_in, e_pool_w, e_pool_scale, e_s5_lam_re, e_s5_lam_im, e_s5_log_dt, e_s5_b_re, e_s5_b_im, e_s5_c_re, e_s5_c_im, e_s5_d, e_s5_w_glu, e_w_out, e_ffn_gate, e_ffn_up, e_ffn_down, o_w_in, o_gla_w_a, o_gla_b_a, o_gla_norm, o_q_norm, o_k_norm, o_w_out, o_router, o_router_b, o_moe_gate, o_moe_up, o_moe_down):
    assert x.shape[0] == 1 and w_mod.shape[0] == 2 and e_w_in.shape[0] == 1 and o_w_in.shape[0] == 1
    d = x.shape[2]
    n_ctx = ctx.shape[1]
    h = jnp.concatenate([ctx[0], x[0]], axis=0)

    vecs = jnp.zeros((8, d), F32).at[0].set(c[0]).at[1].set(c_ctx)
    mods = _modulation(vecs, w_mod, b_mod)

    mod1, mod2 = _mod_rows(mods[0], 0), _mod_rows(mods[0], 1)
    z = _normed_matmul(h, norms[0, 0], mod1, e_w_in[0].astype(BF16), n_ctx)
    pool_out = _pool_mixer(z, e_pool_w[0].astype(BF16), e_pool_scale[0], n_ctx)
    m_op, qre, qim, pre, pim, at = _s5_params(e_s5_lam_re[0], e_s5_lam_im[0], e_s5_log_dt[0],
                                              e_s5_b_re[0], e_s5_b_im[0], e_s5_c_re[0], e_s5_c_im[0])
    y_s5 = _s5_mixer(z[:, W_POOL:], m_op, qre, qim, pre, pim, at, n_ctx)
    h = _even_out(pool_out, y_s5, z, e_s5_d[0], e_s5_w_glu[0].astype(BF16), e_w_out[0].astype(BF16),
                  h, norms[0, 1], mod1, n_ctx)
    h = _dense_ffn(h, norms[0, 2], norms[0, 3], mod2, e_ffn_gate[0].astype(BF16),
                   e_ffn_up[0].astype(BF16), e_ffn_down[0].astype(BF16), n_ctx)

    mod1, mod2 = _mod_rows(mods[1], 0), _mod_rows(mods[1], 1)
    w_in = o_w_in[0]
    a0 = 2 * W_GLA_K + W_GLA_V
    w_main = jnp.concatenate([w_in[:, :a0], w_in[:, a0 + GLA_RANK:]], axis=1).astype(BF16)
    w_aux = jnp.pad(w_in[:, a0:a0 + GLA_RANK], ((0, 0), (0, 128 - GLA_RANK))).astype(BF16)
    zmain, a_aux = _normed_matmul(h, norms[1, 0], mod1, w_main, n_ctx, w_aux=w_aux)
    o_f, o_b = _gla_mixer(zmain, a_aux, o_gla_w_a[0], o_gla_b_a[0], n_ctx)
    qx, kx, vx, bound_log2 = _qk_prep(zmain, o_q_norm[0], o_k_norm[0], n_ctx)
    att = _attention(qx, kx, vx, bound_log2)
    h_lat = _odd_out(o_f, o_b, zmain, att, o_gla_norm[0], o_w_out[0].astype(BF16), h, norms[1, 1], mod1, n_ctx)
    out = _moe(h_lat, norms[1, 2], norms[1, 3], mod2, o_router[0], o_router_b[0],
               o_moe_gate[0].astype(BF16), o_moe_up[0].astype(BF16), o_moe_down[0].astype(BF16))
    return out[None]
```

```python
import functools
import math

import jax
import jax.numpy as jnp
from jax import lax
from jax.experimental import pallas as pl
from jax.experimental.pallas import tpu as pltpu

F32 = jnp.float32
BF16 = jnp.bfloat16
HI = lax.Precision.HIGHEST
EPS = 1e-6

D_MODEL = 2048
GRID_W = 64
N_MOD = 6

POOL_WINDOWS = (2, 4, 8, 16)
POOL_GROUP = 384
W_POOL = 1536
W_S5 = 512
S5_CH = 16
S5_STATE = 64
S5_GROUPS = 32
S5_T = 32

GLA_HEADS = 4
GLA_DK = 128
GLA_DV = 256
GLA_RANK = 16
GLA_TAU = 16.0
GLA_CHUNK = 64
W_GLA_K = 512
W_GLA_V = 1024
ATT_HEAD_DIM = 128
ATT_HEADS = 8
ATT_KV_HEADS = 2
ATT_GROUP = 4
W_ATT = 1024
W_ATT_KV = 256
ROPE_THETA = 10000.0

D_FF = 7168
N_EXPERTS = 8

OQ, OK_, OV, OR, OAQ, OAK, OAV = 0, 512, 1024, 2048, 3072, 4096, 4352
W_ODD_MAIN = 4608

MIB = 2 ** 20


def _params(sem, vmem_mib):
    return pltpu.CompilerParams(dimension_semantics=sem, vmem_limit_bytes=vmem_mib * MIB)


def _pick(n, cands):
    for c in cands:
        if n % c == 0:
            return c
    raise ValueError(f"no tile for {n} in {cands}")


def _sigmoid(x):
    return 1.0 / (1.0 + jnp.exp(-x))


def _rms(x, w):
    return x * lax.rsqrt(jnp.mean(x * x, axis=-1, keepdims=True) + EPS) * w


def _norm_mod(x, nw, mod_ref, row0, n_ctx):
    xn = _rms(x, nw)
    rows = row0 + lax.broadcasted_iota(jnp.int32, (x.shape[0], 1), 0)
    is_ctx = rows < n_ctx
    sc = jnp.where(is_ctx, mod_ref[2:3, :], mod_ref[0:1, :])
    sh = jnp.where(is_ctx, mod_ref[3:4, :], mod_ref[1:2, :])
    return xn * (1.0 + sc) + sh


def _gate_rows(mod_ref, row0, tm, n_ctx):
    rows = row0 + lax.broadcasted_iota(jnp.int32, (tm, 1), 0)
    return jnp.where(rows < n_ctx, mod_ref[5:6, :], mod_ref[4:5, :])


def _mod_kernel(v_ref, w_ref, b_ref, o_ref):
    v = v_ref[...]
    s = v * _sigmoid(v)
    o_ref[...] = jnp.dot(s, w_ref[...], precision=HI, preferred_element_type=F32) + b_ref[...]


def _modulation(vecs, w_mod, b_mod):
    depth, d, n6 = w_mod.shape
    tn = 1024
    return pl.pallas_call(
        _mod_kernel,
        grid=(depth, n6 // tn),
        in_specs=[pl.BlockSpec((8, d), lambda l, j: (0, 0)),
                  pl.BlockSpec((None, d, tn), lambda l, j: (l, 0, j)),
                  pl.BlockSpec((None, 1, tn), lambda l, j: (l, 0, j))],
        out_specs=pl.BlockSpec((None, 8, tn), lambda l, j: (l, 0, j)),
        out_shape=jax.ShapeDtypeStruct((depth, 8, n6), F32),
        compiler_params=_params(("parallel", "parallel"), 40),
        name="modulation",
    )(vecs, w_mod, b_mod.reshape(depth, 1, n6))


def _mod_rows(m, sub):
    d = m.shape[1] // N_MOD
    m6 = m.reshape(8, N_MOD, d)
    sh, sc, g = m6[:, 3 * sub + 0], m6[:, 3 * sub + 1], m6[:, 3 * sub + 2]
    z = jnp.zeros((d,), F32)
    return jnp.stack([sc[0], sh[0], sc[1], sh[1], g[0], g[1], z, z])


def _nmm_kernel(h_ref, nw_ref, mod_ref, w_ref, o_ref, u_ref, *, n_ctx, tm):
    @pl.when(pl.program_id(1) == 0)
    def _():
        u = _norm_mod(h_ref[...], nw_ref[...], mod_ref, pl.program_id(0) * tm, n_ctx)
        u_ref[...] = u.astype(BF16)
    o_ref[...] = jnp.dot(u_ref[...], w_ref[...], preferred_element_type=F32).astype(o_ref.dtype)


def _nmm_aux_kernel(h_ref, nw_ref, mod_ref, w_ref, wa_ref, o_ref, oa_ref, u_ref, *, n_ctx, tm):
    @pl.when(pl.program_id(1) == 0)
    def _():
        u = _norm_mod(h_ref[...], nw_ref[...], mod_ref, pl.program_id(0) * tm, n_ctx)
        ub = u.astype(BF16)
        u_ref[...] = ub
        oa_ref[...] = jnp.dot(ub, wa_ref[...], preferred_element_type=F32)
    o_ref[...] = jnp.dot(u_ref[...], w_ref[...], preferred_element_type=F32).astype(o_ref.dtype)


def _normed_matmul(h, nw, mod, w, n_ctx, w_aux=None):
    n, d = h.shape
    nout = w.shape[1]
    tm = _pick(n, (640, 512, 256, 128))
    tn = _pick(nout, (1536, 1024, 512, 256, 128))
    in_specs = [pl.BlockSpec((tm, d), lambda i, j: (i, 0)),
                pl.BlockSpec((1, d), lambda i, j: (0, 0)),
                pl.BlockSpec((8, d), lambda i, j: (0, 0)),
                pl.BlockSpec((d, tn), lambda i, j: (0, j))]
    out_specs = pl.BlockSpec((tm, tn), lambda i, j: (i, j))
    out_shape = jax.ShapeDtypeStruct((n, nout), BF16)
    args = [h, nw.reshape(1, d), mod, w]
    if w_aux is None:
        kern = functools.partial(_nmm_kernel, n_ctx=n_ctx, tm=tm)
    else:
        na = w_aux.shape[1]
        kern = functools.partial(_nmm_aux_kernel, n_ctx=n_ctx, tm=tm)
        in_specs.append(pl.BlockSpec((d, na), lambda i, j: (0, 0)))
        out_specs = [out_specs, pl.BlockSpec((tm, na), lambda i, j: (i, 0))]
        out_shape = [out_shape, jax.ShapeDtypeStruct((n, na), F32)]
        args.append(w_aux)
    return pl.pallas_call(
        kern, grid=(n // tm, nout // tn), in_specs=in_specs, out_specs=out_specs, out_shape=out_shape,
        scratch_shapes=[pltpu.VMEM((tm, d), BF16)],
        compiler_params=_params(("parallel", "arbitrary"), 48),
        name="normed_proj",
    )(*args)


_POOL_HALO = 16


def _pool_kernel(z_ref, zp_ref, zn_ref, pw_ref, ps_ref, o_ref, ext_ref, *, n_ctx, n_all, tm):
    i = pl.program_id(0)
    row0 = i * tm
    in_ctx = row0 < n_ctx
    seq_start = jnp.where(in_ctx, 0, n_ctx)
    seq_end = jnp.where(in_ctx, n_ctx, n_all)
    has_prev = row0 > seq_start
    has_next = row0 + tm < seq_end
    hl = _POOL_HALO
    ext_ref[0:hl, :] = jnp.where(has_prev, zp_ref[...].astype(F32), 0.0)
    ext_ref[hl:hl + tm, :] = z_ref[...].astype(F32)
    ext_ref[hl + tm:hl + tm + hl, :] = jnp.where(has_next, zn_ref[...].astype(F32), 0.0)
    t = row0 - seq_start + lax.broadcasted_iota(jnp.int32, (tm, 1), 0)
    seq_len = seq_end - seq_start
    for g, w in enumerate(POOL_WINDOWS):
        c0 = g * POOL_GROUP
        acc = None
        for k in range(-(w // 2), w - w // 2):
            v = ext_ref[hl + k:hl + k + tm, c0:c0 + POOL_GROUP]
            acc = v if acc is None else acc + v
        lo = jnp.maximum(t - w // 2, 0)
        hi = jnp.minimum(t + (w - w // 2), seq_len)
        cnt = (hi - lo).astype(F32)
        dlt = acc / cnt - ext_ref[hl:hl + tm, c0:c0 + POOL_GROUP]
        y = jnp.dot(dlt.astype(BF16), pw_ref[g], preferred_element_type=F32)
        o_ref[:, c0:c0 + POOL_GROUP] = (y * ps_ref[:, c0:c0 + POOL_GROUP]).astype(o_ref.dtype)


def _pool_mixer(z, pool_w, pool_scale, n_ctx):
    n = z.shape[0]
    tm = 256
    assert n_ctx % tm == 0 and n % tm == 0
    hl = _POOL_HALO
    nh = n // hl
    per = tm // hl
    kern = functools.partial(_pool_kernel, n_ctx=n_ctx, n_all=n, tm=tm)
    return pl.pallas_call(
        kern, grid=(n // tm,),
        in_specs=[pl.BlockSpec((tm, W_POOL), lambda i: (i, 0)),
                  pl.BlockSpec((hl, W_POOL), lambda i: (jnp.maximum(i * per - 1, 0), 0)),
                  pl.BlockSpec((hl, W_POOL), lambda i: (jnp.minimum((i + 1) * per, nh - 1), 0)),
                  pl.BlockSpec((4, POOL_GROUP, POOL_GROUP), lambda i: (0, 0, 0)),
                  pl.BlockSpec((1, W_POOL), lambda i: (0, 0))],
        out_specs=pl.BlockSpec((tm, W_POOL), lambda i: (i, 0)),
        out_shape=jax.ShapeDtypeStruct((n, W_POOL), BF16),
        scratch_shapes=[pltpu.VMEM((tm + 2 * hl, W_POOL), F32)],
        compiler_params=_params(("parallel",), 32),
        name="pool_mixer",
    )(z, z, z, pool_w, pool_scale.reshape(1, W_POOL))


def _dot_hi(a, b):
    return jnp.dot(a, b, precision=HI, preferred_element_type=F32)


def _dot_t_hi(a, b):
    return lax.dot_general(a, b, (((0,), (0,)), ((), ())), precision=HI, preferred_element_type=F32)


def _s5_param_kernel(lr_ref, li_ref, lrc_ref, lic_ref, ldt_ref, br_ref, bi_ref, cr_ref, ci_ref,
                     m_ref, qre_ref, qim_ref, pre_ref, pim_ref, at_ref):
    t_len = S5_T
    tp = t_len + 8
    wid = t_len * S5_CH
    col = lax.broadcasted_iota(jnp.int32, (tp, wid), 1)
    kid = lax.broadcasted_iota(jnp.int32, (tp, wid), 0)
    cq = col >> 4
    rep_nat = (kid == cq).astype(F32)
    rep_rev = (kid == t_len - 1 - cq).astype(F32)
    til = (lax.broadcasted_iota(jnp.int32, (S5_CH, wid), 0)
           == (lax.broadcasted_iota(jnp.int32, (S5_CH, wid), 1) & (S5_CH - 1))).astype(F32)
    colb = lax.broadcasted_iota(jnp.int32, (S5_CH, wid), 1) >> 4
    kk = lax.broadcasted_iota(jnp.int32, (tp, S5_STATE), 0).astype(F32)
    rows = [None] * t_len
    for d in range(2):
        dt = jnp.exp(ldt_ref[d])
        lr, li = lr_ref[d], li_ref[d]
        mag = jnp.exp(kk * (lr * dt))
        ang = kk * (li * dt)
        pr, pi_ = mag * jnp.cos(ang), mag * jnp.sin(ang)
        lrc, lic = lrc_ref[d], lic_ref[d]
        magc = jnp.exp(lrc * dt)
        arc, aic = magc * jnp.cos(lic * dt), magc * jnp.sin(lic * dt)
        den = lrc * lrc + lic * lic
        nr, ni = arc - 1.0, aic
        kre = (nr * lrc + ni * lic) / den
        kim = (ni * lrc - nr * lic) / den
        bbr = kre * br_ref[d] - kim * bi_ref[d]
        bbi = kre * bi_ref[d] + kim * br_ref[d]
        repq = rep_rev if d == 0 else rep_nat
        repp = ((kid == cq + 1) if d == 0 else (kid == t_len - cq)).astype(F32)
        repm = rep_nat if d == 0 else rep_rev
        bt_r, bt_i = _dot_hi(bbr, til), _dot_hi(bbi, til)
        eqr, eqi = _dot_t_hi(pr, repq), _dot_t_hi(pi_, repq)
        qre_ref[d] = (eqr * bt_r - eqi * bt_i).astype(qre_ref.dtype)
        qim_ref[d] = (eqr * bt_i + eqi * bt_r).astype(qim_ref.dtype)
        epr, epi = _dot_t_hi(pr, repp), _dot_t_hi(pi_, repp)
        ct_r, ct_i = _dot_t_hi(cr_ref[d], til), _dot_t_hi(ci_ref[d], til)
        pre_ref[d] = (ct_r * epr - ct_i * epi).astype(pre_ref.dtype)
        pim_ref[d] = (-(ct_r * epi + ct_i * epr)).astype(pim_ref.dtype)
        at_ref[d, 0:1, :] = pr[t_len:t_len + 1, :]
        at_ref[d, 1:2, :] = pi_[t_len:t_len + 1, :]
        emr, emi = _dot_t_hi(pr, repm), _dot_t_hi(pi_, repm)
        y_re = emr * ct_r - emi * ct_i
        y_im = emr * ct_i + emi * ct_r
        r = _dot_t_hi(bbr, y_re) - _dot_t_hi(bbi, y_im)
        for s in range(t_len):
            if d == 0:
                blk = jnp.where(colb >= s, pltpu.roll(r, S5_CH * s, 1), 0.0)
            else:
                blk = jnp.where(colb <= s, pltpu.roll(r, (S5_CH * (s + 1)) % wid, 1), 0.0)
            rows[s] = blk if rows[s] is None else rows[s] + blk
    for s in range(t_len):
        m_ref[s * S5_CH:(s + 1) * S5_CH, :] = rows[s].astype(m_ref.dtype)


def _s5_params(lam_re, lam_im, log_dt, b_re, b_im, c_re, c_im):
    g, p, n = S5_GROUPS, S5_STATE, S5_CH
    wid = S5_T * S5_CH

    def spec(*shape):
        return pl.BlockSpec((2, None) + shape, lambda gi: (0, gi) + (0,) * len(shape))

    def ospec(*shape):
        return pl.BlockSpec((None,) + shape, lambda gi: (gi,) + (0,) * len(shape))
    return pl.pallas_call(
        _s5_param_kernel, grid=(g,),
        in_specs=[spec(1, p), spec(1, p), spec(p, 1), spec(p, 1), spec(1, 1),
                  spec(p, n), spec(p, n), spec(n, p), spec(n, p)],
        out_specs=[ospec(wid, wid), ospec(2, p, wid), ospec(2, p, wid), ospec(2, p, wid), ospec(2, p, wid),
                   ospec(2, 2, p)],
        out_shape=[jax.ShapeDtypeStruct((g, wid, wid), BF16),
                   jax.ShapeDtypeStruct((g, 2, p, wid), BF16),
                   jax.ShapeDtypeStruct((g, 2, p, wid), BF16),
                   jax.ShapeDtypeStruct((g, 2, p, wid), BF16),
                   jax.ShapeDtypeStruct((g, 2, p, wid), BF16),
                   jax.ShapeDtypeStruct((g, 2, 2, p), F32)],
        compiler_params=_params(("parallel",), 32),
        name="s5_params",
    )(lam_re.reshape(2, g, 1, p), lam_im.reshape(2, g, 1, p), lam_re.reshape(2, g, p, 1),
      lam_im.reshape(2, g, p, 1), log_dt.reshape(2, g, 1, 1), b_re, b_im, c_re, c_im)


def _s5_state_kernel(u_ref, qre_ref, qim_ref, s_ref):
    u = u_ref[...]
    dn = (((1,), (1,)), ((), ()))
    s_ref[:, 0:128] = lax.dot_general(u, qre_ref[...], dn, preferred_element_type=F32)
    s_ref[:, 128:256] = lax.dot_general(u, qim_ref[...], dn, preferred_element_type=F32)


def _s5_scan_kernel(s_ref, are_ref, aim_ref, h_ref, *, nc, ncc):
    ar, ai = are_ref[...], aim_ref[...]
    is_f = lax.broadcasted_iota(jnp.int32, ar.shape, 1) < S5_STATE

    def body(i, carry):
        hr, hi = carry
        cf = i
        cb = jnp.where(i < ncc, ncc - 1 - i, nc + ncc - 1 - i)
        h_ref[cf, :, 0:64] = hr[:, 0:64]
        h_ref[cb, :, 64:128] = hr[:, 64:128]
        h_ref[cf, :, 128:192] = hi[:, 0:64]
        h_ref[cb, :, 192:256] = hi[:, 64:128]
        sf, sb = s_ref[cf], s_ref[cb]
        sr = jnp.where(is_f, sf[:, 0:128], sb[:, 0:128])
        si = jnp.where(is_f, sf[:, 128:256], sb[:, 128:256])
        return ar * hr - ai * hi + sr, ar * hi + ai * hr + si

    zero = jnp.zeros(ar.shape, F32)
    lax.fori_loop(0, nc, body, (zero, zero))


def _s5_out_kernel(u_ref, m_ref, h_ref, p_ref, y_ref):
    y = jnp.dot(u_ref[...], m_ref[...], preferred_element_type=F32)
    y_ref[...] = y + jnp.dot(h_ref[...].astype(BF16), p_ref[...], preferred_element_type=F32)


def _s5_mixer(s_all, m, qre, qim, pre, pim, at, n_ctx):
    n = s_all.shape[0]
    g, t_len, wid = S5_GROUPS, S5_T, S5_T * S5_CH
    nc, ncc = n // t_len, n_ctx // t_len
    assert n % t_len == 0 and n_ctx % t_len == 0
    u = s_all.reshape(nc, t_len, g, S5_CH).transpose(2, 0, 1, 3).reshape(g, nc, wid)
    p_all = jnp.concatenate([pre.reshape(g, 128, wid), pim.reshape(g, 128, wid)], axis=1)
    a_re = at[:, :, 0, :].reshape(g, 128)
    a_im = at[:, :, 1, :].reshape(g, 128)
    s_t = pl.pallas_call(
        _s5_state_kernel, grid=(g,),
        in_specs=[pl.BlockSpec((None, nc, wid), lambda i: (i, 0, 0)),
                  pl.BlockSpec((None, 128, wid), lambda i: (i, 0, 0)),
                  pl.BlockSpec((None, 128, wid), lambda i: (i, 0, 0))],
        out_specs=pl.BlockSpec((nc, 256), lambda i: (0, i)),
        out_shape=jax.ShapeDtypeStruct((nc, g * 256), F32),
        compiler_params=_params(("parallel",), 32),
        name="s5_chunk_state",
    )(u, qre.reshape(g, 128, wid), qim.reshape(g, 128, wid))
    h_t = pl.pallas_call(
        functools.partial(_s5_scan_kernel, nc=nc, ncc=ncc),
        out_shape=jax.ShapeDtypeStruct((nc, g, 256), F32),
        compiler_params=pltpu.CompilerParams(vmem_limit_bytes=48 * MIB),
        name="s5_chunk_scan",
    )(s_t.reshape(nc, g, 256), a_re, a_im).reshape(nc, g * 256)
    y = pl.pallas_call(
        _s5_out_kernel, grid=(g,),
        in_specs=[pl.BlockSpec((None, nc, wid), lambda i: (i, 0, 0)),
                  pl.BlockSpec((None, wid, wid), lambda i: (i, 0, 0)),
                  pl.BlockSpec((nc, 256), lambda i: (0, i)),
                  pl.BlockSpec((None, 256, wid), lambda i: (i, 0, 0))],
        out_specs=pl.BlockSpec((None, nc, wid), lambda i: (i, 0, 0)),
        out_shape=jax.ShapeDtypeStruct((g, nc, wid), F32),
        compiler_params=_params(("parallel",), 32),
        name="s5_chunk_out",
    )(u, m, h_t, p_all)
    return y.reshape(g, nc, t_len, S5_CH).transpose(1, 2, 0, 3).reshape(n, W_S5)


def _gelu_tanh(x):
    return 0.5 * x * (1.0 + jnp.tanh(math.sqrt(2.0 / math.pi) * (x + 0.044715 * (x * x * x))))


def _even_out_kernel(pool_ref, y_ref, s_ref, dsk_ref, wglu_ref, wo_ref, h_ref, nw_ref, mod_ref, o_ref,
                     *, n_ctx, tm):
    y = _gelu_tanh(y_ref[...] + s_ref[...].astype(F32) * dsk_ref[...])
    gate = jnp.dot(y.astype(BF16), wglu_ref[...], preferred_element_type=F32)
    s5 = (y * _sigmoid(gate)).astype(BF16)
    mix = jnp.dot(pool_ref[...], wo_ref[0:W_POOL, :], preferred_element_type=F32)
    mix = mix + jnp.dot(s5, wo_ref[W_POOL:W_POOL + W_S5, :], preferred_element_type=F32)
    g = _gate_rows(mod_ref, pl.program_id(0) * tm, tm, n_ctx)
    o_ref[...] = h_ref[...] + g * _rms(mix, nw_ref[...])


def _even_out(pool_out, y_s5, z, dsk, w_glu, w_out, h, nw, mod, n_ctx):
    n, d = h.shape
    tm = _pick(n, (320, 256, 128))
    return pl.pallas_call(
        functools.partial(_even_out_kernel, n_ctx=n_ctx, tm=tm), grid=(n // tm,),
        in_specs=[pl.BlockSpec((tm, W_POOL), lambda i: (i, 0)),
                  pl.BlockSpec((tm, W_S5), lambda i: (i, 0)),
                  pl.BlockSpec((tm, W_S5), lambda i: (i, W_POOL // W_S5)),
                  pl.BlockSpec((1, W_S5), lambda i: (0, 0)),
                  pl.BlockSpec((W_S5, W_S5), lambda i: (0, 0)),
                  pl.BlockSpec((d, d), lambda i: (0, 0)),
                  pl.BlockSpec((tm, d), lambda i: (i, 0)),
                  pl.BlockSpec((1, d), lambda i: (0, 0)),
                  pl.BlockSpec((8, d), lambda i: (0, 0))],
        out_specs=pl.BlockSpec((tm, d), lambda i: (i, 0)),
        out_shape=jax.ShapeDtypeStruct((n, d), F32),
        compiler_params=_params(("parallel",), 48),
        name="even_mixer_out",
    )(pool_out, y_s5, z, dsk.reshape(1, W_S5), w_glu, w_out, h, nw.reshape(1, d), mod)


def _ffn_kernel(h_ref, nw2_ref, nw3_ref, mod_ref, wg_ref, wu_ref, wd_ref, o_ref, v_ref, *, n_ctx, tm):
    f = pl.program_id(1)

    @pl.when(f == 0)
    def _():
        v = _norm_mod(h_ref[...], nw2_ref[...], mod_ref, pl.program_id(0) * tm, n_ctx)
        v_ref[...] = v.astype(BF16)
        o_ref[...] = jnp.zeros_like(o_ref)
    v = v_ref[...]
    a = jnp.dot(v, wg_ref[...], preferred_element_type=F32)
    b = jnp.dot(v, wu_ref[...], preferred_element_type=F32)
    act = (a * _sigmoid(a) * b).astype(BF16)
    o_ref[...] += jnp.dot(act, wd_ref[...], preferred_element_type=F32)

    @pl.when(f == pl.num_programs(1) - 1)
    def _():
        g = _gate_rows(mod_ref, pl.program_id(0) * tm, tm, n_ctx)
        o_ref[...] = h_ref[...] + g * _rms(o_ref[...], nw3_ref[...])


def _dense_ffn(h, nw2, nw3, mod, wg, wu, wd, n_ctx):
    n, d = h.shape
    dff = wg.shape[1]
    tm = _pick(n, (640, 512, 256, 128))
    tf = 512
    return pl.pallas_call(
        functools.partial(_ffn_kernel, n_ctx=n_ctx, tm=tm), grid=(n // tm, dff // tf),
        in_specs=[pl.BlockSpec((tm, d), lambda i, f: (i, 0)),
                  pl.BlockSpec((1, d), lambda i, f: (0, 0)),
                  pl.BlockSpec((1, d), lambda i, f: (0, 0)),
                  pl.BlockSpec((8, d), lambda i, f: (0, 0)),
                  pl.BlockSpec((d, tf), lambda i, f: (0, f)),
                  pl.BlockSpec((d, tf), lambda i, f: (0, f)),
                  pl.BlockSpec((tf, d), lambda i, f: (f, 0))],
        out_specs=pl.BlockSpec((tm, d), lambda i, f: (i, 0)),
        out_shape=jax.ShapeDtypeStruct((n, d), F32),
        scratch_shapes=[pltpu.VMEM((tm, d), BF16)],
        compiler_params=_params(("parallel", "arbitrary"), 56),
        name="dense_swiglu",
    )(h, nw2.reshape(1, d), nw3.reshape(1, d), mod, wg, wu, wd)


def _split_bf16(x):
    hi = x.astype(BF16)
    return hi, (x - hi.astype(F32)).astype(BF16)


def _gla_kernel(qf_ref, kf_ref, vf_ref, af_ref, qb_ref, kb_ref, vb_ref, ab_ref, wa_ref, ba_ref,
                of_ref, ob_ref, st_ref, *, tb):
    c = GLA_CHUNK
    nsub = tb // c

    @pl.when(pl.program_id(0) == 0)
    def _():
        st_ref[...] = jnp.zeros_like(st_ref)
    ri = lax.broadcasted_iota(jnp.int32, (c, c), 0)
    ci = lax.broadcasted_iota(jnp.int32, (c, c), 1)
    scale = GLA_DK ** -0.5
    nt = (((1,), (1,)), ((), ()))
    tn = (((0,), (0,)), ((), ()))
    dirs = ((qf_ref, kf_ref, vf_ref, af_ref, of_ref), (qb_ref, kb_ref, vb_ref, ab_ref, ob_ref))
    for d, (q_ref, k_ref, v_ref, a_ref, o_ref) in enumerate(dirs):
        mask = (ri >= ci) if d == 0 else (ci >= ri)
        cum = mask.astype(BF16)
        a_hi, a_lo = _split_bf16(a_ref[...])
        w_hi, w_lo = _split_bf16(wa_ref[d])
        z = (jnp.dot(a_hi, w_hi, preferred_element_type=F32) + jnp.dot(a_hi, w_lo, preferred_element_type=F32)
             + jnp.dot(a_lo, w_hi, preferred_element_type=F32)) + ba_ref[d]
        glog = (jnp.minimum(z, 0.0) - jnp.log(1.0 + jnp.exp(-jnp.abs(z)))) * (1.0 / GLA_TAU)
        order = range(nsub) if d == 0 else range(nsub - 1, -1, -1)
        for sc in order:
            r0 = sc * c
            g_hi, g_lo = _split_bf16(glog[r0:r0 + c, :])
            b = jnp.dot(cum, g_hi, preferred_element_type=F32) + jnp.dot(cum, g_lo, preferred_element_type=F32)
            b_last = b[c - 1:c, :] if d == 0 else b[0:1, :]
            dec = jnp.exp(b_last)
            q = q_ref[r0:r0 + c, :].astype(F32) * scale
            k = k_ref[r0:r0 + c, :].astype(F32)
            qe = (q * jnp.exp(b)).astype(BF16)
            ke = (k * jnp.exp(-b)).astype(BF16)
            kd = (k * jnp.exp(b_last - b)).astype(BF16)
            for h in range(GLA_HEADS):
                ks = slice(h * GLA_DK, (h + 1) * GLA_DK)
                vs = slice(h * GLA_DV, (h + 1) * GLA_DV)
                att = lax.dot_general(qe[:, ks], ke[:, ks], nt, preferred_element_type=F32)
                att = jnp.where(mask, att, 0.0).astype(BF16)
                vh = v_ref[r0:r0 + c, vs]
                s_old = st_ref[d, h]
                o = jnp.dot(att, vh, preferred_element_type=F32)
                o = o + lax.dot_general(qe[:, ks], s_old.astype(BF16), nt, preferred_element_type=F32)
                o_ref[r0:r0 + c, vs] = o
                upd = lax.dot_general(vh, kd[:, ks], tn, preferred_element_type=F32)
                st_ref[d, h] = s_old * dec[:, ks] + upd


def _gla_mixer(zmain, a_aux, w_a, b_a, n_ctx):
    n = zmain.shape[0]
    tb = 256
    assert n % tb == 0 and n_ctx % tb == 0
    nb, ncb = n // tb, n_ctx // tb

    def bwd(s):
        return jnp.where(s < ncb, ncb - 1 - s, nb + ncb - 1 - s)
    wa = jnp.pad(w_a, ((0, 0), (0, 128 - GLA_RANK), (0, 0)))
    in_specs = []
    for order in (lambda s: s, bwd):
        in_specs += [pl.BlockSpec((tb, W_GLA_K), lambda s, o=order: (o(s), OQ // W_GLA_K)),
                     pl.BlockSpec((tb, W_GLA_K), lambda s, o=order: (o(s), OK_ // W_GLA_K)),
                     pl.BlockSpec((tb, W_GLA_V), lambda s, o=order: (o(s), OV // W_GLA_V)),
                     pl.BlockSpec((tb, 128), lambda s, o=order: (o(s), 0))]
    in_specs += [pl.BlockSpec((2, 128, W_GLA_K), lambda s: (0, 0, 0)),
                 pl.BlockSpec((2, 1, W_GLA_K), lambda s: (0, 0, 0))]
    return pl.pallas_call(
        functools.partial(_gla_kernel, tb=tb), grid=(nb,),
        in_specs=in_specs,
        out_specs=[pl.BlockSpec((tb, W_GLA_V), lambda s: (s, 0)),
                   pl.BlockSpec((tb, W_GLA_V), lambda s: (bwd(s), 0))],
        out_shape=[jax.ShapeDtypeStruct((n, W_GLA_V), F32)] * 2,
        scratch_shapes=[pltpu.VMEM((2, GLA_HEADS, GLA_DV, GLA_DK), F32)],
        compiler_params=_params(("arbitrary",), 32),
        name="gla_chunked",
    )(zmain, zmain, zmain, a_aux, zmain, zmain, zmain, a_aux, wa, b_a.reshape(2, 1, W_GLA_K))


_ATT_PAD = 2 * ATT_HEAD_DIM
_LOG2E = 1.4426950408889634
_FIXED_SHIFT_MAX_BOUND = 40.0


def _rope_fn(row0, tm, n_ctx):
    hd = ATT_HEAD_DIM
    row = row0 + lax.broadcasted_iota(jnp.int32, (tm, hd), 0)
    lane = lax.broadcasted_iota(jnp.int32, (tm, hd), 1)
    t = row - n_ctx
    pos = jnp.where(lane < hd // 2, t // GRID_W, t % GRID_W).astype(F32)
    quarter = hd // 4
    freq = jnp.exp((lane % quarter).astype(F32) * (-math.log(ROPE_THETA) / quarter))
    ang = pos * freq
    first = (lane & quarter) == 0
    cosv = jnp.cos(ang)
    sin_s = jnp.where(first, -jnp.sin(ang), jnp.sin(ang))

    def rope(x):
        sw = jnp.where(first, pltpu.roll(x, hd - quarter, 1), pltpu.roll(x, quarter, 1))
        return x * cosv + sw * sin_s
    return rope, row >= n_ctx


def _k_prep_kernel(k_ref, v_ref, kw_ref, kx_ref, vx_ref, kmax_ref, *, n_ctx, tm):
    hd = ATT_HEAD_DIM
    rope, is_lat = _rope_fn(pl.program_id(0) * tm, tm, n_ctx)
    lane = lax.broadcasted_iota(jnp.int32, (tm, hd), 1)
    one_col = jnp.where(lane == 0, 1.0, 0.0).astype(BF16)
    ones = jnp.ones((tm, hd), BF16)
    nmax = jnp.zeros((tm, 1), F32)
    for h in range(ATT_KV_HEADS):
        sl = slice(h * hd, (h + 1) * hd)
        xn = _rms(k_ref[:, sl].astype(F32), kw_ref[...])
        kr = jnp.where(is_lat, rope(xn), xn).astype(BF16)
        kx_ref[:, h * _ATT_PAD:h * _ATT_PAD + hd] = kr
        kx_ref[:, h * _ATT_PAD + hd:(h + 1) * _ATT_PAD] = one_col
        vx_ref[:, h * _ATT_PAD:h * _ATT_PAD + hd] = v_ref[:, sl]
        vx_ref[:, h * _ATT_PAD + hd:(h + 1) * _ATT_PAD] = ones
        krf = kr.astype(F32)
        nmax = jnp.maximum(nmax, jnp.sum(krf * krf, axis=-1, keepdims=True))
    kmax_ref[...] = jnp.broadcast_to(jnp.max(nmax, axis=0, keepdims=True), kmax_ref.shape)


def _q_prep_kernel(q_ref, qw_ref, kn_ref, qx_ref, bmax_ref, *, n_ctx, tm):
    hd = ATT_HEAD_DIM
    rope, _ = _rope_fn(n_ctx + pl.program_id(0) * tm, tm, n_ctx)
    lane = lax.broadcasted_iota(jnp.int32, (tm, hd), 1)
    scale = hd ** -0.5 * _LOG2E
    bmax = jnp.zeros((tm, 1), F32)
    for h in range(ATT_HEADS):
        sl = slice(h * hd, (h + 1) * hd)
        xn = _rms(q_ref[:, sl].astype(F32), qw_ref[...])
        qr = (rope(xn) * scale).astype(BF16)
        qf = qr.astype(F32)
        bound = jnp.sqrt(jnp.sum(qf * qf, axis=-1, keepdims=True)) * kn_ref[...]
        qx_ref[:, h * _ATT_PAD:h * _ATT_PAD + hd] = qr
        qx_ref[:, h * _ATT_PAD + hd:(h + 1) * _ATT_PAD] = jnp.where(lane == 0, -bound, 0.0).astype(BF16)
        bmax = jnp.maximum(bmax, bound)
    bmax_ref[...] = jnp.broadcast_to(jnp.max(bmax, axis=0, keepdims=True), bmax_ref.shape)


def _qk_prep(zmain, q_norm, k_norm, n_ctx):
    n = zmain.shape[0]
    lq = n - n_ctx
    tm = 256
    hd = ATT_HEAD_DIM
    assert n_ctx % tm == 0
    ncb = n_ctx // tm
    kx, vx, kmax = pl.pallas_call(
        functools.partial(_k_prep_kernel, n_ctx=n_ctx, tm=tm), grid=(n // tm,),
        in_specs=[pl.BlockSpec((tm, W_ATT_KV), lambda i: (i, OAK // W_ATT_KV)),
                  pl.BlockSpec((tm, W_ATT_KV), lambda i: (i, OAV // W_ATT_KV)),
                  pl.BlockSpec((1, hd), lambda i: (0, 0))],
        out_specs=[pl.BlockSpec((tm, ATT_KV_HEADS * _ATT_PAD), lambda i: (i, 0)),
                   pl.BlockSpec((tm, ATT_KV_HEADS * _ATT_PAD), lambda i: (i, 0)),
                   pl.BlockSpec((None, 8, 128), lambda i: (i, 0, 0))],
        out_shape=[jax.ShapeDtypeStruct((n, ATT_KV_HEADS * _ATT_PAD), BF16),
                   jax.ShapeDtypeStruct((n, ATT_KV_HEADS * _ATT_PAD), BF16),
                   jax.ShapeDtypeStruct((n // tm, 8, 128), F32)],
        compiler_params=_params(("parallel",), 32),
        name="k_norm_rope",
    )(zmain, zmain, k_norm.reshape(1, hd))
    knorm = jnp.sqrt(jnp.max(kmax)).reshape(1, 1)
    qx, bmax = pl.pallas_call(
        functools.partial(_q_prep_kernel, n_ctx=n_ctx, tm=tm), grid=(lq // tm,),
        in_specs=[pl.BlockSpec((tm, W_ATT), lambda i: (i + ncb, OAQ // W_ATT)),
                  pl.BlockSpec((1, hd), lambda i: (0, 0)),
                  pl.BlockSpec((1, 1), lambda i: (0, 0))],
        out_specs=[pl.BlockSpec((tm, ATT_HEADS * _ATT_PAD), lambda i: (i, 0)),
                   pl.BlockSpec((None, 8, 128), lambda i: (i, 0, 0))],
        out_shape=[jax.ShapeDtypeStruct((lq, ATT_HEADS * _ATT_PAD), BF16),
                   jax.ShapeDtypeStruct((lq // tm, 8, 128), F32)],
        compiler_params=_params(("parallel",), 32),
        name="q_norm_rope",
    )(zmain, q_norm.reshape(1, hd), knorm)
    return qx, kx, vx, jnp.max(bmax)


def _flash_fixed_kernel(q_ref, kt_ref, v_ref, o_ref, acc_ref):
    c = pl.program_id(2)
    hd = ATT_HEAD_DIM

    @pl.when(c == 0)
    def _():
        acc_ref[...] = jnp.zeros_like(acc_ref)
    kt = kt_ref[...]
    v = v_ref[...]
    for g in range(ATT_GROUP):
        s = jnp.dot(q_ref[:, g * _ATT_PAD:(g + 1) * _ATT_PAD], kt, preferred_element_type=F32)
        acc_ref[g] += jnp.dot(jnp.exp2(s).astype(BF16), v, preferred_element_type=F32)

    @pl.when(c == pl.num_programs(2) - 1)
    def _():
        for g in range(ATT_GROUP):
            a = acc_ref[g]
            o_ref[:, g * hd:(g + 1) * hd] = (a[:, 0:hd] / a[:, hd:2 * hd]).astype(o_ref.dtype)


def _flash_online_kernel(q_ref, kt_ref, v_ref, o_ref, m_ref, acc_ref):
    c = pl.program_id(2)
    hd = ATT_HEAD_DIM

    @pl.when(c == 0)
    def _():
        m_ref[...] = jnp.full(m_ref.shape, -jnp.inf, F32)
        acc_ref[...] = jnp.zeros_like(acc_ref)
    kt = kt_ref[...]
    v = v_ref[...]
    for g in range(ATT_GROUP):
        s = jnp.dot(q_ref[:, g * _ATT_PAD:(g + 1) * _ATT_PAD], kt, preferred_element_type=F32)
        m_prev = m_ref[g]
        m_new = jnp.maximum(m_prev, jnp.max(s, axis=-1, keepdims=True))
        p = jnp.exp2(s - m_new).astype(BF16)
        acc_ref[g] = jnp.exp2(m_prev - m_new) * acc_ref[g] + jnp.dot(p, v, preferred_element_type=F32)
        m_ref[g] = m_new

    @pl.when(c == pl.num_programs(2) - 1)
    def _():
        for g in range(ATT_GROUP):
            a = acc_ref[g]
            o_ref[:, g * hd:(g + 1) * hd] = (a[:, 0:hd] / a[:, hd:2 * hd]).astype(o_ref.dtype)


def _attention(qx, kx, vx, bound_log2):
    lq = qx.shape[0]
    n = kx.shape[0]
    kt = kx.T
    gw = ATT_GROUP * _ATT_PAD
    ow = ATT_GROUP * ATT_HEAD_DIM
    tq = _pick(lq, (512, 256, 128))

    def call(kern, tk, scratch, name):
        return pl.pallas_call(
            kern, grid=(ATT_KV_HEADS, lq // tq, n // tk),
            in_specs=[pl.BlockSpec((tq, gw), lambda h, i, j: (i, h)),
                      pl.BlockSpec((_ATT_PAD, tk), lambda h, i, j: (h, j)),
                      pl.BlockSpec((tk, _ATT_PAD), lambda h, i, j: (j, h))],
            out_specs=pl.BlockSpec((tq, ow), lambda h, i, j: (i, h)),
            out_shape=jax.ShapeDtypeStruct((lq, W_ATT), BF16),
            scratch_shapes=scratch + [pltpu.VMEM((ATT_GROUP, tq, _ATT_PAD), F32)],
            compiler_params=_params(("parallel", "parallel", "arbitrary"), 48),
            name=name,
        )(qx, kt, vx)

    def fixed(_):
        return call(_flash_fixed_kernel, _pick(n, (3328, 1280, 512, 256)), [], "gqa_flash_fixed_shift")

    def online(_):
        return call(_flash_online_kernel, _pick(n, (640, 512, 256, 128)),
                    [pltpu.VMEM((ATT_GROUP, tq, 1), F32)], "gqa_flash_online")
    return lax.cond(bound_log2 <= _FIXED_SHIFT_MAX_BOUND * _LOG2E, fixed, online, None)


def _odd_out_kernel(of_ref, ob_ref, r_ref, att_ref, gn_ref, wo_ref, h_ref, nw_ref, mod_ref, o_ref):
    o = of_ref[...] + ob_ref[...]
    r = r_ref[...].astype(F32)
    silu_r = r * _sigmoid(r)
    parts = []
    for h in range(GLA_HEADS):
        vs = slice(h * GLA_DV, (h + 1) * GLA_DV)
        parts.append((_rms(o[:, vs], gn_ref[:, vs]) * silu_r[:, vs]).astype(BF16))
    gla = jnp.concatenate(parts, axis=1)
    mix = jnp.dot(gla, wo_ref[0:W_GLA_V, :], preferred_element_type=F32)
    mix = mix + jnp.dot(att_ref[...], wo_ref[W_GLA_V:W_GLA_V + W_ATT, :], preferred_element_type=F32)
    o_ref[...] = h_ref[...] + mod_ref[4:5, :] * _rms(mix, nw_ref[...])


def _odd_out(o_f, o_b, zmain, att, gla_norm, w_out, h_all, nw, mod, n_ctx):
    n, d = h_all.shape
    lq = n - n_ctx
    tm = _pick(lq, (256, 128))
    assert n_ctx % tm == 0
    off = n_ctx // tm
    return pl.pallas_call(
        _odd_out_kernel, grid=(lq // tm,),
        in_specs=[pl.BlockSpec((tm, W_GLA_V), lambda i: (i + off, 0)),
                  pl.BlockSpec((tm, W_GLA_V), lambda i: (i + off, 0)),
                  pl.BlockSpec((tm, W_GLA_V), lambda i: (i + off, OR // W_GLA_V)),
                  pl.BlockSpec((tm, W_ATT), lambda i: (i, 0)),
                  pl.BlockSpec((1, W_GLA_V), lambda i: (0, 0)),
                  pl.BlockSpec((d, d), lambda i: (0, 0)),
                  pl.BlockSpec((tm, d), lambda i: (i + off, 0)),
                  pl.BlockSpec((1, d), lambda i: (0, 0)),
                  pl.BlockSpec((8, d), lambda i: (0, 0))],
        out_specs=pl.BlockSpec((tm, d), lambda i: (i, 0)),
        out_shape=jax.ShapeDtypeStruct((lq, d), F32),
        compiler_params=_params(("parallel",), 48),
        name="odd_mixer_out",
    )(o_f, o_b, zmain, att, gla_norm.reshape(1, W_GLA_V), w_out, h_all, nw.reshape(1, d), mod)


def _router_kernel(h_ref, nw_ref, mod_ref, wr_ref, br_ref, v_ref, r_ref):
    x = h_ref[...]
    v = _rms(x, nw_ref[...]) * (1.0 + mod_ref[0:1, :]) + mod_ref[1:2, :]
    v_ref[...] = v
    logits = _dot_hi(v, wr_ref[...]) + br_ref[...]
    lane = lax.broadcasted_iota(jnp.int32, logits.shape, 1)
    m1 = jnp.max(logits, axis=-1, keepdims=True)
    i1 = jnp.min(jnp.where(logits == m1, lane, 128), axis=-1, keepdims=True)
    rest = jnp.where(lane == i1, -jnp.inf, logits)
    m2 = jnp.max(rest, axis=-1, keepdims=True)
    i2 = jnp.min(jnp.where(rest == m2, lane, 128), axis=-1, keepdims=True)
    e2 = jnp.exp(m2 - m1)
    g1 = 1.0 / (1.0 + e2)
    g2 = e2 / (1.0 + e2)
    out = jnp.where(lane == 0, i1.astype(F32), 0.0)
    out = jnp.where(lane == 1, i2.astype(F32), out)
    out = jnp.where(lane == 2, g1, out)
    out = jnp.where(lane == 3, g2, out)
    r_ref[...] = out


def _router(h_lat, nw, mod, w_router, b_router):
    lq, d = h_lat.shape
    tm = _pick(lq, (512, 256, 128))
    wr = jnp.pad(w_router, ((0, 0), (0, 128 - N_EXPERTS)))
    br = jnp.pad(b_router, (0, 128 - N_EXPERTS), constant_values=-1e30).reshape(1, 128)
    return pl.pallas_call(
        _router_kernel, grid=(lq // tm,),
        in_specs=[pl.BlockSpec((tm, d), lambda i: (i, 0)),
                  pl.BlockSpec((1, d), lambda i: (0, 0)),
                  pl.BlockSpec((8, d), lambda i: (0, 0)),
                  pl.BlockSpec((d, 128), lambda i: (0, 0)),
                  pl.BlockSpec((1, 128), lambda i: (0, 0))],
        out_specs=[pl.BlockSpec((tm, d), lambda i: (i, 0)), pl.BlockSpec((tm, 128), lambda i: (i, 0))],
        out_shape=[jax.ShapeDtypeStruct((lq, d), F32), jax.ShapeDtypeStruct((lq, 128), F32)],
        compiler_params=_params(("parallel",), 40),
        name="router_top2",
    )(h_lat, nw.reshape(1, d), mod, wr, br)


def _expert_kernel(te_ref, tv_ref, src_ref, dst_ref, v_hbm, gt_ref, wg_ref, wu_ref, wd_ref, y_hbm,
                   xg_ref, xb_ref, ys_ref, acc_ref, gsem, ssem, *, tm, nt, nf):
    j = pl.program_id(0)
    f = pl.program_id(1)
    slot = j % 2
    per_step = tm // nf
    lo = f * per_step
    valid = tv_ref[j] == 1

    def start_gather(tile, slot_, row):
        tok = src_ref[tile * tm + row]
        pltpu.make_async_copy(v_hbm.at[pl.ds(tok, 1)], xg_ref.at[slot_, pl.ds(row, 1)], gsem.at[slot_]).start()

    def start_scatter(row):
        dst = dst_ref[j * tm + row]
        pltpu.make_async_copy(ys_ref.at[pl.ds(row, 1)], y_hbm.at[pl.ds(dst, 1)], ssem.at[0]).start()

    def gather_loop(tile, slot_, first, count):
        def body(r, carry):
            start_gather(tile, slot_, first + r)
            return carry
        lax.fori_loop(0, count, body, 0, unroll=8)

    @pl.when((j == 0) & (f == 0))
    def _():
        gather_loop(0, 0, 0, tm)
        ys_ref[...] = jnp.zeros_like(ys_ref)

    @pl.when(f == 0)
    def _():
        pltpu.make_async_copy(v_hbm.at[pl.ds(0, tm)], xg_ref.at[slot], gsem.at[slot]).wait()
        xb_ref[...] = xg_ref[slot].astype(BF16)
        acc_ref[...] = jnp.zeros_like(acc_ref)
        for row in range(nf * per_step, tm):
            start_scatter(row)

        @pl.when(j + 1 < nt)
        def _():
            for row in range(nf * per_step, tm):
                start_gather(j + 1, 1 - slot, row)

    @pl.when(valid)
    def _():
        groups = 4

        def start_group(g, after):
            for r in range(g * per_step // groups, (g + 1) * per_step // groups):
                start_gather(j + 1, 1 - slot, lo + r + after)
                start_scatter(lo + r + after)

        def zero_after(val):
            bits = pltpu.bitcast(val[0:8, 0:128], jnp.int32)
            return ((bits & 1) >> 1)[0, 0]

        start_group(0, 0)
        x = xb_ref[...]
        a = jnp.dot(x, wg_ref[...], preferred_element_type=F32)
        start_group(1, zero_after(a))
        b = jnp.dot(x, wu_ref[...], preferred_element_type=F32)
        start_group(2, zero_after(b))
        act = (a * _sigmoid(a) * b).astype(BF16)
        down = jnp.dot(act, wd_ref[...], preferred_element_type=F32)
        start_group(3, zero_after(down))
        acc_ref[...] += down

    @pl.when(jnp.logical_not(valid))
    def _():
        @pl.when(j + 1 < nt)
        def _():
            gather_loop(j + 1, 1 - slot, lo, per_step)

        def body(r, carry):
            start_scatter(lo + r)
            return carry
        lax.fori_loop(0, per_step, body, 0, unroll=8)

    @pl.when(f == nf - 1)
    def _():
        pltpu.make_async_copy(ys_ref, y_hbm.at[pl.ds(0, tm)], ssem.at[0]).wait()
        ys_ref[...] = acc_ref[...] * gt_ref[...]


def _combine_kernel(y0_ref, y1_ref, h_ref, nw_ref, mod_ref, o_ref):
    o_ref[...] = h_ref[...] + mod_ref[4:5, :] * _rms(y0_ref[...] + y1_ref[...], nw_ref[...])


def _count_le(sorted_vals, queries):
    return jnp.sum((sorted_vals[None, :] <= queries[:, None]).astype(jnp.int32), axis=1)


def _moe(h_lat, nw2, nw3, mod, w_router, b_router, wg, wu, wd):
    lq, d = h_lat.shape
    e = N_EXPERTS
    dff = wg.shape[2]
    tm = _pick(lq, (512, 256))
    tf = 1024
    v, route = _router(h_lat, nw2, mod, w_router, b_router)
    idx = route[:, 0:2].astype(jnp.int32)
    gates = route[:, 2:4]

    npairs = 2 * lq
    nt = npairs // tm + e + 1
    flat_e = idx.reshape(-1)
    pair_id = jnp.arange(npairs, dtype=jnp.int32)
    onehot = (flat_e[:, None] == jnp.arange(e, dtype=jnp.int32)[None, :]).astype(jnp.int32)
    csum = jnp.cumsum(onehot, axis=0)
    rank = jnp.sum(csum * onehot, axis=1) - 1
    counts = csum[-1]
    padded = ((counts + tm - 1) // tm) * tm
    ends = jnp.cumsum(padded)
    starts = ends - padded
    pos = jnp.sum(onehot * starts[None, :], axis=1) + rank
    pair = jnp.stack([pair_id, lax.bitcast_convert_type(gates.reshape(-1), jnp.int32)], axis=1)
    empty = jnp.broadcast_to(jnp.array([-1, 0], jnp.int32), (nt * tm, 2))
    slots = empty.at[pos].set(pair)
    sorted_pair = slots[:, 0]
    sorted_gate = lax.bitcast_convert_type(slots[:, 1], F32)
    is_pad = sorted_pair < 0
    src_row = jnp.where(is_pad, 0, sorted_pair >> 1).astype(jnp.int32)
    pad_rank = jnp.cumsum(is_pad.astype(jnp.int32)) - 1
    dst_row = jnp.where(is_pad, npairs + pad_rank, (sorted_pair & 1) * lq + (sorted_pair >> 1)).astype(jnp.int32)
    dst_row = jnp.concatenate([(nt - 1) * tm + jnp.arange(tm, dtype=jnp.int32), dst_row])
    n_used = ends[-1] // tm
    tile_ids = jnp.arange(nt, dtype=jnp.int32)
    tile_valid = (tile_ids < n_used).astype(jnp.int32)
    tile_exp = jnp.minimum(_count_le(ends, jnp.minimum(tile_ids, n_used - 1) * tm), e - 1)

    nf = dff // tf
    pinned = lambda j, f, tv: jnp.where(tv[j] == 1, f, nf - 1)
    y = pl.pallas_call(
        functools.partial(_expert_kernel, tm=tm, nt=nt, nf=nf),
        grid_spec=pltpu.PrefetchScalarGridSpec(
            num_scalar_prefetch=4, grid=(nt, nf),
            in_specs=[pl.BlockSpec(memory_space=pl.ANY),
                      pl.BlockSpec((tm, 1), lambda j, f, te, tv, src, dst: (j, 0)),
                      pl.BlockSpec((None, d, tf), lambda j, f, te, tv, src, dst: (te[j], 0, pinned(j, f, tv))),
                      pl.BlockSpec((None, d, tf), lambda j, f, te, tv, src, dst: (te[j], 0, pinned(j, f, tv))),
                      pl.BlockSpec((None, tf, d), lambda j, f, te, tv, src, dst: (te[j], pinned(j, f, tv), 0))],
            out_specs=pl.BlockSpec(memory_space=pl.ANY),
            scratch_shapes=[pltpu.VMEM((2, tm, d), F32), pltpu.VMEM((tm, d), BF16), pltpu.VMEM((tm, d), F32),
                            pltpu.VMEM((tm, d), F32), pltpu.SemaphoreType.DMA((2,)),
                            pltpu.SemaphoreType.DMA((1,))]),
        out_shape=jax.ShapeDtypeStruct((nt * tm, d), F32),
        compiler_params=pltpu.CompilerParams(dimension_semantics=("arbitrary", "arbitrary"),
                                             vmem_limit_bytes=56 * MIB, disable_bounds_checks=True),
        name="moe_experts",
    )(tile_exp, tile_valid, src_row, dst_row, v, sorted_gate.reshape(nt * tm, 1), wg, wu, wd)

    tc = _pick(lq, (256, 128))
    return pl.pallas_call(
        _combine_kernel, grid=(lq // tc,),
        in_specs=[pl.BlockSpec((tc, d), lambda i: (i, 0)),
                  pl.BlockSpec((tc, d), lambda i: (i + lq // tc, 0)),
                  pl.BlockSpec((tc, d), lambda i: (i, 0)),
                  pl.BlockSpec((1, d), lambda i: (0, 0)),
                  pl.BlockSpec((8, d), lambda i: (0, 0))],
        out_specs=pl.BlockSpec((tc, d), lambda i: (i, 0)),
        out_shape=jax.ShapeDtypeStruct((lq, d), F32),
        compiler_params=_params(("parallel",), 32),
        name="moe_combine",
    )(y, y, h_lat, nw3.reshape(1, d), mod)


def kernel(x, c, ctx, c_ctx, w_mod, b_mod, norms, e_w_in, e_pool_w, e_pool_scale, e_s5_lam_re, e_s5_lam_im, e_s5_log_dt, e_s5_b_re, e_s5_b_im, e_s5_c_re, e_s5_c_im, e_s5_d, e_s5_w_glu, e_w_out, e_ffn_gate, e_ffn_up, e_ffn_down, o_w_in, o_gla_w_a, o_gla_b_a, o_gla_norm, o_q_norm, o_k_norm, o_w_out, o_router, o_router_b, o_moe_gate, o_moe_up, o_moe_down):
    assert x.shape[0] == 1 and w_mod.shape[0] == 2 and e_w_in.shape[0] == 1 and o_w_in.shape[0] == 1
    d = x.shape[2]
    n_ctx = ctx.shape[1]
    h = jnp.concatenate([ctx[0], x[0]], axis=0)

    vecs = jnp.zeros((8, d), F32).at[0].set(c[0]).at[1].set(c_ctx)
    mods = _modulation(vecs, w_mod, b_mod)

    mod1, mod2 = _mod_rows(mods[0], 0), _mod_rows(mods[0], 1)
    z = _normed_matmul(h, norms[0, 0], mod1, e_w_in[0].astype(BF16), n_ctx)
    pool_out = _pool_mixer(z, e_pool_w[0].astype(BF16), e_pool_scale[0], n_ctx)
    m_op, qre, qim, pre, pim, at = _s5_params(e_s5_lam_re[0], e_s5_lam_im[0], e_s5_log_dt[0],
                                              e_s5_b_re[0], e_s5_b_im[0], e_s5_c_re[0], e_s5_c_im[0])
    y_s5 = _s5_mixer(z[:, W_POOL:], m_op, qre, qim, pre, pim, at, n_ctx)
    h = _even_out(pool_out, y_s5, z, e_s5_d[0], e_s5_w_glu[0].astype(BF16), e_w_out[0].astype(BF16),
                  h, norms[0, 1], mod1, n_ctx)
    h = _dense_ffn(h, norms[0, 2], norms[0, 3], mod2, e_ffn_gate[0].astype(BF16),
                   e_ffn_up[0].astype(BF16), e_ffn_down[0].astype(BF16), n_ctx)

    mod1, mod2 = _mod_rows(mods[1], 0), _mod_rows(mods[1], 1)
    w_in = o_w_in[0]
    a0 = 2 * W_GLA_K + W_GLA_V
    w_main = jnp.concatenate([w_in[:, :a0], w_in[:, a0 + GLA_RANK:]], axis=1).astype(BF16)
    w_aux = jnp.pad(w_in[:, a0:a0 + GLA_RANK], ((0, 0), (0, 128 - GLA_RANK))).astype(BF16)
    zmain, a_aux = _normed_matmul(h, norms[1, 0], mod1, w_main, n_ctx, w_aux=w_aux)
    o_f, o_b = _gla_mixer(zmain, a_aux, o_gla_w_a[0], o_gla_b_a[0], n_ctx)
    qx, kx, vx, bound_log2 = _qk_prep(zmain, o_q_norm[0], o_k_norm[0], n_ctx)
    att = _attention(qx, kx, vx, bound_log2)
    h_lat = _odd_out(o_f, o_b, zmain, att, o_gla_norm[0], o_w_out[0].astype(BF16), h, norms[1, 1], mod1, n_ctx)
    out = _moe(h_lat, norms[1, 2], norms[1, 3], mod2, o_router[0], o_router_b[0],
               o_moe_gate[0].astype(BF16), o_moe_up[0].astype(BF16), o_moe_down[0].astype(BF16))
    return out[None]
```

```python
import functools
import math

import jax
import jax.numpy as jnp
from jax import lax
from jax.experimental import pallas as pl
from jax.experimental.pallas import tpu as pltpu

F32 = jnp.float32
BF16 = jnp.bfloat16
HI = lax.Precision.HIGHEST
EPS = 1e-6

D_MODEL = 2048
GRID_W = 64
N_MOD = 6

POOL_WINDOWS = (2, 4, 8, 16)
POOL_GROUP = 384
W_POOL = 1536
W_S5 = 512
S5_CH = 16
S5_STATE = 64
S5_GROUPS = 32
S5_T = 32

GLA_HEADS = 4
GLA_DK = 128
GLA_DV = 256
GLA_RANK = 16
GLA_TAU = 16.0
GLA_CHUNK = 64
W_GLA_K = 512
W_GLA_V = 1024
ATT_HEAD_DIM = 128
ATT_HEADS = 8
ATT_KV_HEADS = 2
ATT_GROUP = 4
W_ATT = 1024
W_ATT_KV = 256
ROPE_THETA = 10000.0

D_FF = 7168
N_EXPERTS = 8

OQ, OK_, OV, OR, OAQ, OAK, OAV = 0, 512, 1024, 2048, 3072, 4096, 4352
W_ODD_MAIN = 4608

MIB = 2 ** 20


def _params(sem, vmem_mib):
    return pltpu.CompilerParams(dimension_semantics=sem, vmem_limit_bytes=vmem_mib * MIB)


def _pick(n, cands):
    for c in cands:
        if n % c == 0:
            return c
    raise ValueError(f"no tile for {n} in {cands}")


def _sigmoid(x):
    return 1.0 / (1.0 + jnp.exp(-x))


def _rms(x, w):
    return x * lax.rsqrt(jnp.mean(x * x, axis=-1, keepdims=True) + EPS) * w


_NORM_CHUNK = 16


def _store_norm_mod(dst_ref, h_ref, nw_ref, mod_ref, row0, n_ctx):
    tm = h_ref.shape[0]
    nw = nw_ref[...]
    w_lat = nw * (1.0 + mod_ref[0:1, :])
    sh_lat = mod_ref[1:2, :]

    def rows(c):
        r0 = pl.multiple_of(c * _NORM_CHUNK, _NORM_CHUNK)
        x = h_ref[pl.ds(r0, _NORM_CHUNK), :]
        return r0, x * lax.rsqrt(jnp.mean(x * x, axis=-1, keepdims=True) + EPS)

    @pl.when(row0 >= n_ctx)
    def _():
        def body(c, carry):
            r0, xr = rows(c)
            dst_ref[pl.ds(r0, _NORM_CHUNK), :] = (xr * w_lat + sh_lat).astype(dst_ref.dtype)
            return carry
        lax.fori_loop(0, tm // _NORM_CHUNK, body, 0, unroll=4)

    @pl.when(row0 < n_ctx)
    def _():
        w_ctx = nw * (1.0 + mod_ref[2:3, :])
        sh_ctx = mod_ref[3:4, :]

        def body(c, carry):
            r0, xr = rows(c)
            is_ctx = row0 + r0 + lax.broadcasted_iota(jnp.int32, (_NORM_CHUNK, 1), 0) < n_ctx
            u = xr * jnp.where(is_ctx, w_ctx, w_lat) + jnp.where(is_ctx, sh_ctx, sh_lat)
            dst_ref[pl.ds(r0, _NORM_CHUNK), :] = u.astype(dst_ref.dtype)
            return carry
        lax.fori_loop(0, tm // _NORM_CHUNK, body, 0, unroll=4)


def _gate_rows(mod_ref, row0, tm, n_ctx):
    rows = row0 + lax.broadcasted_iota(jnp.int32, (tm, 1), 0)
    return jnp.where(rows < n_ctx, mod_ref[5:6, :], mod_ref[4:5, :])


def _mod_kernel(v_ref, w_ref, b_ref, o_ref):
    v = v_ref[...]
    s = v * _sigmoid(v)
    o_ref[...] = jnp.dot(s, w_ref[...], precision=HI, preferred_element_type=F32) + b_ref[...]


def _modulation(vecs, w_mod, b_mod):
    depth, d, n6 = w_mod.shape
    tn = 1024
    return pl.pallas_call(
        _mod_kernel,
        grid=(depth, n6 // tn),
        in_specs=[pl.BlockSpec((8, d), lambda l, j: (0, 0)),
                  pl.BlockSpec((None, d, tn), lambda l, j: (l, 0, j)),
                  pl.BlockSpec((None, 1, tn), lambda l, j: (l, 0, j))],
        out_specs=pl.BlockSpec((None, 8, tn), lambda l, j: (l, 0, j)),
        out_shape=jax.ShapeDtypeStruct((depth, 8, n6), F32),
        compiler_params=_params(("parallel", "parallel"), 40),
        name="modulation",
    )(vecs, w_mod, b_mod.reshape(depth, 1, n6))


def _mod_rows(m, sub):
    d = m.shape[1] // N_MOD
    m6 = m.reshape(8, N_MOD, d)
    sh, sc, g = m6[:, 3 * sub + 0], m6[:, 3 * sub + 1], m6[:, 3 * sub + 2]
    z = jnp.zeros((d,), F32)
    return jnp.stack([sc[0], sh[0], sc[1], sh[1], g[0], g[1], z, z])


def _nmm_kernel(h_ref, nw_ref, mod_ref, w_ref, o_ref, u_ref, *, n_ctx, tm):
    @pl.when(pl.program_id(1) == 0)
    def _():
        _store_norm_mod(u_ref, h_ref, nw_ref, mod_ref, pl.program_id(0) * tm, n_ctx)
    o_ref[...] = jnp.dot(u_ref[...], w_ref[...], preferred_element_type=F32).astype(o_ref.dtype)


def _nmm_aux_kernel(h_ref, nw_ref, mod_ref, w_ref, wa_ref, o_ref, oa_ref, u_ref, *, n_ctx, tm):
    @pl.when(pl.program_id(1) == 0)
    def _():
        _store_norm_mod(u_ref, h_ref, nw_ref, mod_ref, pl.program_id(0) * tm, n_ctx)
        oa_ref[...] = jnp.dot(u_ref[...], wa_ref[...], preferred_element_type=F32)
    o_ref[...] = jnp.dot(u_ref[...], w_ref[...], preferred_element_type=F32).astype(o_ref.dtype)


def _normed_matmul(h, nw, mod, w, n_ctx, w_aux=None):
    n, d = h.shape
    nout = w.shape[1]
    tm = _pick(n, (640, 512, 256, 128))
    tn = _pick(nout, (1536, 1024, 512, 256, 128))
    in_specs = [pl.BlockSpec((tm, d), lambda i, j: (i, 0)),
                pl.BlockSpec((1, d), lambda i, j: (0, 0)),
                pl.BlockSpec((8, d), lambda i, j: (0, 0)),
                pl.BlockSpec((d, tn), lambda i, j: (0, j))]
    out_specs = pl.BlockSpec((tm, tn), lambda i, j: (i, j))
    out_shape = jax.ShapeDtypeStruct((n, nout), BF16)
    args = [h, nw.reshape(1, d), mod, w]
    if w_aux is None:
        kern = functools.partial(_nmm_kernel, n_ctx=n_ctx, tm=tm)
    else:
        na = w_aux.shape[1]
        kern = functools.partial(_nmm_aux_kernel, n_ctx=n_ctx, tm=tm)
        in_specs.append(pl.BlockSpec((d, na), lambda i, j: (0, 0)))
        out_specs = [out_specs, pl.BlockSpec((tm, na), lambda i, j: (i, 0))]
        out_shape = [out_shape, jax.ShapeDtypeStruct((n, na), F32)]
        args.append(w_aux)
    return pl.pallas_call(
        kern, grid=(n // tm, nout // tn), in_specs=in_specs, out_specs=out_specs, out_shape=out_shape,
        scratch_shapes=[pltpu.VMEM((tm, d), BF16)],
        compiler_params=_params(("parallel", "arbitrary"), 48),
        name="normed_proj",
    )(*args)


_POOL_HALO = 16


def _pool_kernel(z_ref, zp_ref, zn_ref, pw_ref, ps_ref, o_ref, ext_ref, *, n_ctx, n_all, tm):
    i = pl.program_id(0)
    row0 = i * tm
    in_ctx = row0 < n_ctx
    seq_start = jnp.where(in_ctx, 0, n_ctx)
    seq_end = jnp.where(in_ctx, n_ctx, n_all)
    has_prev = row0 > seq_start
    has_next = row0 + tm < seq_end
    hl = _POOL_HALO
    ext_ref[0:hl, :] = jnp.where(has_prev, zp_ref[...].astype(F32), 0.0)
    ext_ref[hl:hl + tm, :] = z_ref[...].astype(F32)
    ext_ref[hl + tm:hl + tm + hl, :] = jnp.where(has_next, zn_ref[...].astype(F32), 0.0)
    t = row0 - seq_start + lax.broadcasted_iota(jnp.int32, (tm, 1), 0)
    seq_len = seq_end - seq_start
    for g, w in enumerate(POOL_WINDOWS):
        c0 = g * POOL_GROUP
        acc = None
        for k in range(-(w // 2), w - w // 2):
            v = ext_ref[hl + k:hl + k + tm, c0:c0 + POOL_GROUP]
            acc = v if acc is None else acc + v
        lo = jnp.maximum(t - w // 2, 0)
        hi = jnp.minimum(t + (w - w // 2), seq_len)
        cnt = (hi - lo).astype(F32)
        dlt = acc / cnt - ext_ref[hl:hl + tm, c0:c0 + POOL_GROUP]
        y = jnp.dot(dlt.astype(BF16), pw_ref[g], preferred_element_type=F32)
        o_ref[:, c0:c0 + POOL_GROUP] = (y * ps_ref[:, c0:c0 + POOL_GROUP]).astype(o_ref.dtype)


def _pool_mixer(z, pool_w, pool_scale, n_ctx):
    n = z.shape[0]
    tm = 256
    assert n_ctx % tm == 0 and n % tm == 0
    hl = _POOL_HALO
    nh = n // hl
    per = tm // hl
    kern = functools.partial(_pool_kernel, n_ctx=n_ctx, n_all=n, tm=tm)
    return pl.pallas_call(
        kern, grid=(n // tm,),
        in_specs=[pl.BlockSpec((tm, W_POOL), lambda i: (i, 0)),
                  pl.BlockSpec((hl, W_POOL), lambda i: (jnp.maximum(i * per - 1, 0), 0)),
                  pl.BlockSpec((hl, W_POOL), lambda i: (jnp.minimum((i + 1) * per, nh - 1), 0)),
                  pl.BlockSpec((4, POOL_GROUP, POOL_GROUP), lambda i: (0, 0, 0)),
                  pl.BlockSpec((1, W_POOL), lambda i: (0, 0))],
        out_specs=pl.BlockSpec((tm, W_POOL), lambda i: (i, 0)),
        out_shape=jax.ShapeDtypeStruct((n, W_POOL), BF16),
        scratch_shapes=[pltpu.VMEM((tm + 2 * hl, W_POOL), F32)],
        compiler_params=_params(("parallel",), 32),
        name="pool_mixer",
    )(z, z, z, pool_w, pool_scale.reshape(1, W_POOL))


def _dot_hi(a, b):
    return jnp.dot(a, b, precision=HI, preferred_element_type=F32)


def _dot_t_hi(a, b):
    return lax.dot_general(a, b, (((0,), (0,)), ((), ())), precision=HI, preferred_element_type=F32)


def _s5_param_kernel(lr_ref, li_ref, lrc_ref, lic_ref, ldt_ref, br_ref, bi_ref, cr_ref, ci_ref,
                     m_ref, qre_ref, qim_ref, pre_ref, pim_ref, at_ref):
    t_len = S5_T
    tp = t_len + 8
    wid = t_len * S5_CH
    col = lax.broadcasted_iota(jnp.int32, (tp, wid), 1)
    kid = lax.broadcasted_iota(jnp.int32, (tp, wid), 0)
    cq = col >> 4
    rep_nat = (kid == cq).astype(F32)
    rep_rev = (kid == t_len - 1 - cq).astype(F32)
    til = (lax.broadcasted_iota(jnp.int32, (S5_CH, wid), 0)
           == (lax.broadcasted_iota(jnp.int32, (S5_CH, wid), 1) & (S5_CH - 1))).astype(F32)
    colb = lax.broadcasted_iota(jnp.int32, (S5_CH, wid), 1) >> 4
    kk = lax.broadcasted_iota(jnp.int32, (tp, S5_STATE), 0).astype(F32)
    rows = [None] * t_len
    for d in range(2):
        dt = jnp.exp(ldt_ref[d])
        lr, li = lr_ref[d], li_ref[d]
        mag = jnp.exp(kk * (lr * dt))
        ang = kk * (li * dt)
        pr, pi_ = mag * jnp.cos(ang), mag * jnp.sin(ang)
        lrc, lic = lrc_ref[d], lic_ref[d]
        magc = jnp.exp(lrc * dt)
        arc, aic = magc * jnp.cos(lic * dt), magc * jnp.sin(lic * dt)
        den = lrc * lrc + lic * lic
        nr, ni = arc - 1.0, aic
        kre = (nr * lrc + ni * lic) / den
        kim = (ni * lrc - nr * lic) / den
        bbr = kre * br_ref[d] - kim * bi_ref[d]
        bbi = kre * bi_ref[d] + kim * br_ref[d]
        repq = rep_rev if d == 0 else rep_nat
        repp = ((kid == cq + 1) if d == 0 else (kid == t_len - cq)).astype(F32)
        repm = rep_nat if d == 0 else rep_rev
        bt_r, bt_i = _dot_hi(bbr, til), _dot_hi(bbi, til)
        eqr, eqi = _dot_t_hi(pr, repq), _dot_t_hi(pi_, repq)
        qre_ref[d] = (eqr * bt_r - eqi * bt_i).astype(qre_ref.dtype)
        qim_ref[d] = (eqr * bt_i + eqi * bt_r).astype(qim_ref.dtype)
        epr, epi = _dot_t_hi(pr, repp), _dot_t_hi(pi_, repp)
        ct_r, ct_i = _dot_t_hi(cr_ref[d], til), _dot_t_hi(ci_ref[d], til)
        pre_ref[d] = (ct_r * epr - ct_i * epi).astype(pre_ref.dtype)
        pim_ref[d] = (-(ct_r * epi + ct_i * epr)).astype(pim_ref.dtype)
        at_ref[d, 0:1, :] = pr[t_len:t_len + 1, :]
        at_ref[d, 1:2, :] = pi_[t_len:t_len + 1, :]
        emr, emi = _dot_t_hi(pr, repm), _dot_t_hi(pi_, repm)
        y_re = emr * ct_r - emi * ct_i
        y_im = emr * ct_i + emi * ct_r
        r = _dot_t_hi(bbr, y_re) - _dot_t_hi(bbi, y_im)
        for s in range(t_len):
            if d == 0:
                blk = jnp.where(colb >= s, pltpu.roll(r, S5_CH * s, 1), 0.0)
            else:
                blk = jnp.where(colb <= s, pltpu.roll(r, (S5_CH * (s + 1)) % wid, 1), 0.0)
            rows[s] = blk if rows[s] is None else rows[s] + blk
    for s in range(t_len):
        m_ref[s * S5_CH:(s + 1) * S5_CH, :] = rows[s].astype(m_ref.dtype)


def _s5_params(lam_re, lam_im, log_dt, b_re, b_im, c_re, c_im):
    g, p, n = S5_GROUPS, S5_STATE, S5_CH
    wid = S5_T * S5_CH

    def spec(*shape):
        return pl.BlockSpec((2, None) + shape, lambda gi: (0, gi) + (0,) * len(shape))

    def ospec(*shape):
        return pl.BlockSpec((None,) + shape, lambda gi: (gi,) + (0,) * len(shape))
    return pl.pallas_call(
        _s5_param_kernel, grid=(g,),
        in_specs=[spec(1, p), spec(1, p), spec(p, 1), spec(p, 1), spec(1, 1),
                  spec(p, n), spec(p, n), spec(n, p), spec(n, p)],
        out_specs=[ospec(wid, wid), ospec(2, p, wid), ospec(2, p, wid), ospec(2, p, wid), ospec(2, p, wid),
                   ospec(2, 2, p)],
        out_shape=[jax.ShapeDtypeStruct((g, wid, wid), BF16),
                   jax.ShapeDtypeStruct((g, 2, p, wid), BF16),
                   jax.ShapeDtypeStruct((g, 2, p, wid), BF16),
                   jax.ShapeDtypeStruct((g, 2, p, wid), BF16),
                   jax.ShapeDtypeStruct((g, 2, p, wid), BF16),
                   jax.ShapeDtypeStruct((g, 2, 2, p), F32)],
        compiler_params=_params(("parallel",), 32),
        name="s5_params",
    )(lam_re.reshape(2, g, 1, p), lam_im.reshape(2, g, 1, p), lam_re.reshape(2, g, p, 1),
      lam_im.reshape(2, g, p, 1), log_dt.reshape(2, g, 1, 1), b_re, b_im, c_re, c_im)


def _s5_state_kernel(u_ref, qre_ref, qim_ref, s_ref):
    u = u_ref[...]
    dn = (((1,), (1,)), ((), ()))
    s_ref[:, 0:128] = lax.dot_general(u, qre_ref[...], dn, preferred_element_type=F32)
    s_ref[:, 128:256] = lax.dot_general(u, qim_ref[...], dn, preferred_element_type=F32)


def _s5_scan_kernel(s_ref, are_ref, aim_ref, h_ref, *, nc, ncc):
    ar, ai = are_ref[...], aim_ref[...]
    is_f = lax.broadcasted_iota(jnp.int32, ar.shape, 1) < S5_STATE

    def body(i, carry):
        hr, hi = carry
        cf = i
        cb = jnp.where(i < ncc, ncc - 1 - i, nc + ncc - 1 - i)
        h_ref[cf, :, 0:64] = hr[:, 0:64]
        h_ref[cb, :, 64:128] = hr[:, 64:128]
        h_ref[cf, :, 128:192] = hi[:, 0:64]
        h_ref[cb, :, 192:256] = hi[:, 64:128]
        sf, sb = s_ref[cf], s_ref[cb]
        sr = jnp.where(is_f, sf[:, 0:128], sb[:, 0:128])
        si = jnp.where(is_f, sf[:, 128:256], sb[:, 128:256])
        return ar * hr - ai * hi + sr, ar * hi + ai * hr + si

    zero = jnp.zeros(ar.shape, F32)
    lax.fori_loop(0, nc, body, (zero, zero))


def _s5_out_kernel(u_ref, m_ref, h_ref, p_ref, y_ref):
    y = jnp.dot(u_ref[...], m_ref[...], preferred_element_type=F32)
    y_ref[...] = y + jnp.dot(h_ref[...].astype(BF16), p_ref[...], preferred_element_type=F32)


def _s5_mixer(s_all, m, qre, qim, pre, pim, at, n_ctx):
    n = s_all.shape[0]
    g, t_len, wid = S5_GROUPS, S5_T, S5_T * S5_CH
    nc, ncc = n // t_len, n_ctx // t_len
    assert n % t_len == 0 and n_ctx % t_len == 0
    u = s_all.reshape(nc, t_len, g, S5_CH).transpose(2, 0, 1, 3).reshape(g, nc, wid)
    p_all = jnp.concatenate([pre.reshape(g, 128, wid), pim.reshape(g, 128, wid)], axis=1)
    a_re = at[:, :, 0, :].reshape(g, 128)
    a_im = at[:, :, 1, :].reshape(g, 128)
    s_t = pl.pallas_call(
        _s5_state_kernel, grid=(g,),
        in_specs=[pl.BlockSpec((None, nc, wid), lambda i: (i, 0, 0)),
                  pl.BlockSpec((None, 128, wid), lambda i: (i, 0, 0)),
                  pl.BlockSpec((None, 128, wid), lambda i: (i, 0, 0))],
        out_specs=pl.BlockSpec((nc, 256), lambda i: (0, i)),
        out_shape=jax.ShapeDtypeStruct((nc, g * 256), F32),
        compiler_params=_params(("parallel",), 32),
        name="s5_chunk_state",
    )(u, qre.reshape(g, 128, wid), qim.reshape(g, 128, wid))
    h_t = pl.pallas_call(
        functools.partial(_s5_scan_kernel, nc=nc, ncc=ncc),
        out_shape=jax.ShapeDtypeStruct((nc, g, 256), F32),
        compiler_params=pltpu.CompilerParams(vmem_limit_bytes=48 * MIB),
        name="s5_chunk_scan",
    )(s_t.reshape(nc, g, 256), a_re, a_im).reshape(nc, g * 256)
    y = pl.pallas_call(
        _s5_out_kernel, grid=(g,),
        in_specs=[pl.BlockSpec((None, nc, wid), lambda i: (i, 0, 0)),
                  pl.BlockSpec((None, wid, wid), lambda i: (i, 0, 0)),
                  pl.BlockSpec((nc, 256), lambda i: (0, i)),
                  pl.BlockSpec((None, 256, wid), lambda i: (i, 0, 0))],
        out_specs=pl.BlockSpec((None, nc, wid), lambda i: (i, 0, 0)),
        out_shape=jax.ShapeDtypeStruct((g, nc, wid), F32),
        compiler_params=_params(("parallel",), 32),
        name="s5_chunk_out",
    )(u, m, h_t, p_all)
    return y.reshape(g, nc, t_len, S5_CH).transpose(1, 2, 0, 3).reshape(n, W_S5)


def _gelu_tanh(x):
    return 0.5 * x * (1.0 + jnp.tanh(math.sqrt(2.0 / math.pi) * (x + 0.044715 * (x * x * x))))


def _even_out_kernel(pool_ref, y_ref, s_ref, dsk_ref, wglu_ref, wo_ref, h_ref, nw_ref, mod_ref, o_ref,
                     *, n_ctx, tm):
    y = _gelu_tanh(y_ref[...] + s_ref[...].astype(F32) * dsk_ref[...])
    gate = jnp.dot(y.astype(BF16), wglu_ref[...], preferred_element_type=F32)
    s5 = (y * _sigmoid(gate)).astype(BF16)
    mix = jnp.dot(pool_ref[...], wo_ref[0:W_POOL, :], preferred_element_type=F32)
    mix = mix + jnp.dot(s5, wo_ref[W_POOL:W_POOL + W_S5, :], preferred_element_type=F32)
    g = _gate_rows(mod_ref, pl.program_id(0) * tm, tm, n_ctx)
    o_ref[...] = h_ref[...] + g * _rms(mix, nw_ref[...])


def _even_out(pool_out, y_s5, z, dsk, w_glu, w_out, h, nw, mod, n_ctx):
    n, d = h.shape
    tm = _pick(n, (320, 256, 128))
    return pl.pallas_call(
        functools.partial(_even_out_kernel, n_ctx=n_ctx, tm=tm), grid=(n // tm,),
        in_specs=[pl.BlockSpec((tm, W_POOL), lambda i: (i, 0)),
                  pl.BlockSpec((tm, W_S5), lambda i: (i, 0)),
                  pl.BlockSpec((tm, W_S5), lambda i: (i, W_POOL // W_S5)),
                  pl.BlockSpec((1, W_S5), lambda i: (0, 0)),
                  pl.BlockSpec((W_S5, W_S5), lambda i: (0, 0)),
                  pl.BlockSpec((d, d), lambda i: (0, 0)),
                  pl.BlockSpec((tm, d), lambda i: (i, 0)),
                  pl.BlockSpec((1, d), lambda i: (0, 0)),
                  pl.BlockSpec((8, d), lambda i: (0, 0))],
        out_specs=pl.BlockSpec((tm, d), lambda i: (i, 0)),
        out_shape=jax.ShapeDtypeStruct((n, d), F32),
        compiler_params=_params(("parallel",), 48),
        name="even_mixer_out",
    )(pool_out, y_s5, z, dsk.reshape(1, W_S5), w_glu, w_out, h, nw.reshape(1, d), mod)


def _ffn_kernel(h_ref, nw2_ref, nw3_ref, mod_ref, wg_ref, wu_ref, wd_ref, o_ref, v_ref, *, n_ctx, tm):
    f = pl.program_id(1)

    @pl.when(f == 0)
    def _():
        _store_norm_mod(v_ref, h_ref, nw2_ref, mod_ref, pl.program_id(0) * tm, n_ctx)
        o_ref[...] = jnp.zeros_like(o_ref)
    v = v_ref[...]
    a = jnp.dot(v, wg_ref[...], preferred_element_type=F32)
    b = jnp.dot(v, wu_ref[...], preferred_element_type=F32)
    act = (a * _sigmoid(a) * b).astype(BF16)
    o_ref[...] += jnp.dot(act, wd_ref[...], preferred_element_type=F32)

    @pl.when(f == pl.num_programs(1) - 1)
    def _():
        g = _gate_rows(mod_ref, pl.program_id(0) * tm, tm, n_ctx)
        o_ref[...] = h_ref[...] + g * _rms(o_ref[...], nw3_ref[...])


def _dense_ffn(h, nw2, nw3, mod, wg, wu, wd, n_ctx):
    n, d = h.shape
    dff = wg.shape[1]
    tm = _pick(n, (640, 512, 256, 128))
    tf = 512
    return pl.pallas_call(
        functools.partial(_ffn_kernel, n_ctx=n_ctx, tm=tm), grid=(n // tm, dff // tf),
        in_specs=[pl.BlockSpec((tm, d), lambda i, f: (i, 0)),
                  pl.BlockSpec((1, d), lambda i, f: (0, 0)),
                  pl.BlockSpec((1, d), lambda i, f: (0, 0)),
                  pl.BlockSpec((8, d), lambda i, f: (0, 0)),
                  pl.BlockSpec((d, tf), lambda i, f: (0, f)),
                  pl.BlockSpec((d, tf), lambda i, f: (0, f)),
                  pl.BlockSpec((tf, d), lambda i, f: (f, 0))],
        out_specs=pl.BlockSpec((tm, d), lambda i, f: (i, 0)),
        out_shape=jax.ShapeDtypeStruct((n, d), F32),
        scratch_shapes=[pltpu.VMEM((tm, d), BF16)],
        compiler_params=_params(("parallel", "arbitrary"), 56),
        name="dense_swiglu",
    )(h, nw2.reshape(1, d), nw3.reshape(1, d), mod, wg, wu, wd)


def _split_bf16(x):
    hi = x.astype(BF16)
    return hi, (x - hi.astype(F32)).astype(BF16)


def _gla_kernel(qf_ref, kf_ref, vf_ref, af_ref, qb_ref, kb_ref, vb_ref, ab_ref, wa_ref, ba_ref,
                of_ref, ob_ref, st_ref, *, tb):
    c = GLA_CHUNK
    nsub = tb // c

    @pl.when(pl.program_id(0) == 0)
    def _():
        st_ref[...] = jnp.zeros_like(st_ref)
    ri = lax.broadcasted_iota(jnp.int32, (c, c), 0)
    ci = lax.broadcasted_iota(jnp.int32, (c, c), 1)
    scale = GLA_DK ** -0.5
    nt = (((1,), (1,)), ((), ()))
    tn = (((0,), (0,)), ((), ()))
    dirs = ((qf_ref, kf_ref, vf_ref, af_ref, of_ref), (qb_ref, kb_ref, vb_ref, ab_ref, ob_ref))
    for d, (q_ref, k_ref, v_ref, a_ref, o_ref) in enumerate(dirs):
        mask = (ri >= ci) if d == 0 else (ci >= ri)
        cum = mask.astype(BF16)
        a_hi, a_lo = _split_bf16(a_ref[...])
        w_hi, w_lo = _split_bf16(wa_ref[d])
        z = (jnp.dot(a_hi, w_hi, preferred_element_type=F32) + jnp.dot(a_hi, w_lo, preferred_element_type=F32)
             + jnp.dot(a_lo, w_hi, preferred_element_type=F32)) + ba_ref[d]
        glog = (jnp.minimum(z, 0.0) - jnp.log(1.0 + jnp.exp(-jnp.abs(z)))) * (1.0 / GLA_TAU)
        order = range(nsub) if d == 0 else range(nsub - 1, -1, -1)
        for sc in order:
            r0 = sc * c
            g_hi, g_lo = _split_bf16(glog[r0:r0 + c, :])
            b = jnp.dot(cum, g_hi, preferred_element_type=F32) + jnp.dot(cum, g_lo, preferred_element_type=F32)
            b_last = b[c - 1:c, :] if d == 0 else b[0:1, :]
            dec = jnp.exp(b_last)
            q = q_ref[r0:r0 + c, :].astype(F32) * scale
            k = k_ref[r0:r0 + c, :].astype(F32)
            qe = (q * jnp.exp(b)).astype(BF16)
            ke = (k * jnp.exp(-b)).astype(BF16)
            kd = (k * jnp.exp(b_last - b)).astype(BF16)
            for h in range(GLA_HEADS):
                ks = slice(h * GLA_DK, (h + 1) * GLA_DK)
                vs = slice(h * GLA_DV, (h + 1) * GLA_DV)
                att = lax.dot_general(qe[:, ks], ke[:, ks], nt, preferred_element_type=F32)
                att = jnp.where(mask, att, 0.0).astype(BF16)
                vh = v_ref[r0:r0 + c, vs]
                s_old = st_ref[d, h]
                o = jnp.dot(att, vh, preferred_element_type=F32)
                o = o + lax.dot_general(qe[:, ks], s_old.astype(BF16), nt, preferred_element_type=F32)
                o_ref[r0:r0 + c, vs] = o
                upd = lax.dot_general(vh, kd[:, ks], tn, preferred_element_type=F32)
                st_ref[d, h] = s_old * dec[:, ks] + upd


def _gla_mixer(zmain, a_aux, w_a, b_a, n_ctx):
    n = zmain.shape[0]
    tb = 256
    assert n % tb == 0 and n_ctx % tb == 0
    nb, ncb = n // tb, n_ctx // tb

    def bwd(s):
        return jnp.where(s < ncb, ncb - 1 - s, nb + ncb - 1 - s)
    wa = jnp.pad(w_a, ((0, 0), (0, 128 - GLA_RANK), (0, 0)))
    in_specs = []
    for order in (lambda s: s, bwd):
        in_specs += [pl.BlockSpec((tb, W_GLA_K), lambda s, o=order: (o(s), OQ // W_GLA_K)),
                     pl.BlockSpec((tb, W_GLA_K), lambda s, o=order: (o(s), OK_ // W_GLA_K)),
                     pl.BlockSpec((tb, W_GLA_V), lambda s, o=order: (o(s), OV // W_GLA_V)),
                     pl.BlockSpec((tb, 128), lambda s, o=order: (o(s), 0))]
    in_specs += [pl.BlockSpec((2, 128, W_GLA_K), lambda s: (0, 0, 0)),
                 pl.BlockSpec((2, 1, W_GLA_K), lambda s: (0, 0, 0))]
    return pl.pallas_call(
        functools.partial(_gla_kernel, tb=tb), grid=(nb,),
        in_specs=in_specs,
        out_specs=[pl.BlockSpec((tb, W_GLA_V), lambda s: (s, 0)),
                   pl.BlockSpec((tb, W_GLA_V), lambda s: (bwd(s), 0))],
        out_shape=[jax.ShapeDtypeStruct((n, W_GLA_V), F32)] * 2,
        scratch_shapes=[pltpu.VMEM((2, GLA_HEADS, GLA_DV, GLA_DK), F32)],
        compiler_params=_params(("arbitrary",), 32),
        name="gla_chunked",
    )(zmain, zmain, zmain, a_aux, zmain, zmain, zmain, a_aux, wa, b_a.reshape(2, 1, W_GLA_K))


_ATT_PAD = 2 * ATT_HEAD_DIM
_LOG2E = 1.4426950408889634
_FIXED_SHIFT_MAX_BOUND = 40.0


def _rope_fn(row0, tm, n_ctx):
    hd = ATT_HEAD_DIM
    row = row0 + lax.broadcasted_iota(jnp.int32, (tm, hd), 0)
    lane = lax.broadcasted_iota(jnp.int32, (tm, hd), 1)
    t = row - n_ctx
    pos = jnp.where(lane < hd // 2, t // GRID_W, t % GRID_W).astype(F32)
    quarter = hd // 4
    freq = jnp.exp((lane % quarter).astype(F32) * (-math.log(ROPE_THETA) / quarter))
    ang = pos * freq
    first = (lane & quarter) == 0
    cosv = jnp.cos(ang)
    sin_s = jnp.where(first, -jnp.sin(ang), jnp.sin(ang))

    def rope(x):
        sw = jnp.where(first, pltpu.roll(x, hd - quarter, 1), pltpu.roll(x, quarter, 1))
        return x * cosv + sw * sin_s
    return rope, row >= n_ctx


def _k_prep_kernel(k_ref, v_ref, kw_ref, kx_ref, vx_ref, kmax_ref, *, n_ctx, tm):
    hd = ATT_HEAD_DIM
    rope, is_lat = _rope_fn(pl.program_id(0) * tm, tm, n_ctx)
    lane = lax.broadcasted_iota(jnp.int32, (tm, hd), 1)
    one_col = jnp.where(lane == 0, 1.0, 0.0).astype(BF16)
    ones = jnp.ones((tm, hd), BF16)
    nmax = jnp.zeros((tm, 1), F32)
    for h in range(ATT_KV_HEADS):
        sl = slice(h * hd, (h + 1) * hd)
        xn = _rms(k_ref[:, sl].astype(F32), kw_ref[...])
        kr = jnp.where(is_lat, rope(xn), xn).astype(BF16)
        kx_ref[:, h * _ATT_PAD:h * _ATT_PAD + hd] = kr
        kx_ref[:, h * _ATT_PAD + hd:(h + 1) * _ATT_PAD] = one_col
        vx_ref[:, h * _ATT_PAD:h * _ATT_PAD + hd] = v_ref[:, sl]
        vx_ref[:, h * _ATT_PAD + hd:(h + 1) * _ATT_PAD] = ones
        krf = kr.astype(F32)
        nmax = jnp.maximum(nmax, jnp.sum(krf * krf, axis=-1, keepdims=True))
    kmax_ref[...] = jnp.broadcast_to(jnp.max(nmax, axis=0, keepdims=True), kmax_ref.shape)


def _q_prep_kernel(q_ref, qw_ref, kn_ref, qx_ref, bmax_ref, *, n_ctx, tm):
    hd = ATT_HEAD_DIM
    rope, _ = _rope_fn(n_ctx + pl.program_id(0) * tm, tm, n_ctx)
    lane = lax.broadcasted_iota(jnp.int32, (tm, hd), 1)
    scale = hd ** -0.5 * _LOG2E
    bmax = jnp.zeros((tm, 1), F32)
    for h in range(ATT_HEADS):
        sl = slice(h * hd, (h + 1) * hd)
        xn = _rms(q_ref[:, sl].astype(F32), qw_ref[...])
        qr = (rope(xn) * scale).astype(BF16)
        qf = qr.astype(F32)
        bound = jnp.sqrt(jnp.sum(qf * qf, axis=-1, keepdims=True)) * kn_ref[...]
        qx_ref[:, h * _ATT_PAD:h * _ATT_PAD + hd] = qr
        qx_ref[:, h * _ATT_PAD + hd:(h + 1) * _ATT_PAD] = jnp.where(lane == 0, -bound, 0.0).astype(BF16)
        bmax = jnp.maximum(bmax, bound)
    bmax_ref[...] = jnp.broadcast_to(jnp.max(bmax, axis=0, keepdims=True), bmax_ref.shape)


def _qk_prep(zmain, q_norm, k_norm, n_ctx):
    n = zmain.shape[0]
    lq = n - n_ctx
    tm = 256
    hd = ATT_HEAD_DIM
    assert n_ctx % tm == 0
    ncb = n_ctx // tm
    kx, vx, kmax = pl.pallas_call(
        functools.partial(_k_prep_kernel, n_ctx=n_ctx, tm=tm), grid=(n // tm,),
        in_specs=[pl.BlockSpec((tm, W_ATT_KV), lambda i: (i, OAK // W_ATT_KV)),
                  pl.BlockSpec((tm, W_ATT_KV), lambda i: (i, OAV // W_ATT_KV)),
                  pl.BlockSpec((1, hd), lambda i: (0, 0))],
        out_specs=[pl.BlockSpec((tm, ATT_KV_HEADS * _ATT_PAD), lambda i: (i, 0)),
                   pl.BlockSpec((tm, ATT_KV_HEADS * _ATT_PAD), lambda i: (i, 0)),
                   pl.BlockSpec((None, 8, 128), lambda i: (i, 0, 0))],
        out_shape=[jax.ShapeDtypeStruct((n, ATT_KV_HEADS * _ATT_PAD), BF16),
                   jax.ShapeDtypeStruct((n, ATT_KV_HEADS * _ATT_PAD), BF16),
                   jax.ShapeDtypeStruct((n // tm, 8, 128), F32)],
        compiler_params=_params(("parallel",), 32),
        name="k_norm_rope",
    )(zmain, zmain, k_norm.reshape(1, hd))
    knorm = jnp.sqrt(jnp.max(kmax)).reshape(1, 1)
    qx, bmax = pl.pallas_call(
        functools.partial(_q_prep_kernel, n_ctx=n_ctx, tm=tm), grid=(lq // tm,),
        in_specs=[pl.BlockSpec((tm, W_ATT), lambda i: (i + ncb, OAQ // W_ATT)),
                  pl.BlockSpec((1, hd), lambda i: (0, 0)),
                  pl.BlockSpec((1, 1), lambda i: (0, 0))],
        out_specs=[pl.BlockSpec((tm, ATT_HEADS * _ATT_PAD), lambda i: (i, 0)),
                   pl.BlockSpec((None, 8, 128), lambda i: (i, 0, 0))],
        out_shape=[jax.ShapeDtypeStruct((lq, ATT_HEADS * _ATT_PAD), BF16),
                   jax.ShapeDtypeStruct((lq // tm, 8, 128), F32)],
        compiler_params=_params(("parallel",), 32),
        name="q_norm_rope",
    )(zmain, q_norm.reshape(1, hd), knorm)
    return qx, kx, vx, jnp.max(bmax)


def _flash_fixed_kernel(q_ref, kt_ref, v_ref, o_ref, acc_ref):
    c = pl.program_id(2)
    hd = ATT_HEAD_DIM

    @pl.when(c == 0)
    def _():
        acc_ref[...] = jnp.zeros_like(acc_ref)
    kt = kt_ref[...]
    v = v_ref[...]
    for g in range(ATT_GROUP):
        s = jnp.dot(q_ref[:, g * _ATT_PAD:(g + 1) * _ATT_PAD], kt, preferred_element_type=F32)
        acc_ref[g] += jnp.dot(jnp.exp2(s).astype(BF16), v, preferred_element_type=F32)

    @pl.when(c == pl.num_programs(2) - 1)
    def _():
        for g in range(ATT_GROUP):
            a = acc_ref[g]
            o_ref[:, g * hd:(g + 1) * hd] = (a[:, 0:hd] / a[:, hd:2 * hd]).astype(o_ref.dtype)


def _flash_online_kernel(q_ref, kt_ref, v_ref, o_ref, m_ref, acc_ref):
    c = pl.program_id(2)
    hd = ATT_HEAD_DIM

    @pl.when(c == 0)
    def _():
        m_ref[...] = jnp.full(m_ref.shape, -jnp.inf, F32)
        acc_ref[...] = jnp.zeros_like(acc_ref)
    kt = kt_ref[...]
    v = v_ref[...]
    for g in range(ATT_GROUP):
        s = jnp.dot(q_ref[:, g * _ATT_PAD:(g + 1) * _ATT_PAD], kt, preferred_element_type=F32)
        m_prev = m_ref[g]
        m_new = jnp.maximum(m_prev, jnp.max(s, axis=-1, keepdims=True))
        p = jnp.exp2(s - m_new).astype(BF16)
        acc_ref[g] = jnp.exp2(m_prev - m_new) * acc_ref[g] + jnp.dot(p, v, preferred_element_type=F32)
        m_ref[g] = m_new

    @pl.when(c == pl.num_programs(2) - 1)
    def _():
        for g in range(ATT_GROUP):
            a = acc_ref[g]
            o_ref[:, g * hd:(g + 1) * hd] = (a[:, 0:hd] / a[:, hd:2 * hd]).astype(o_ref.dtype)


def _attention(qx, kx, vx, bound_log2):
    lq = qx.shape[0]
    n = kx.shape[0]
    kt = kx.T
    gw = ATT_GROUP * _ATT_PAD
    ow = ATT_GROUP * ATT_HEAD_DIM
    def call(kern, tq, tk, scratch, name):
        return pl.pallas_call(
            kern, grid=(ATT_KV_HEADS, lq // tq, n // tk),
            in_specs=[pl.BlockSpec((tq, gw), lambda h, i, j: (i, h)),
                      pl.BlockSpec((_ATT_PAD, tk), lambda h, i, j: (h, j)),
                      pl.BlockSpec((tk, _ATT_PAD), lambda h, i, j: (j, h))],
            out_specs=pl.BlockSpec((tq, ow), lambda h, i, j: (i, h)),
            out_shape=jax.ShapeDtypeStruct((lq, W_ATT), BF16),
            scratch_shapes=scratch(tq) + [pltpu.VMEM((ATT_GROUP, tq, _ATT_PAD), F32)],
            compiler_params=_params(("parallel", "parallel", "arbitrary"), 56),
            name=name,
        )(qx, kt, vx)

    def fixed(_):
        return call(_flash_fixed_kernel, _pick(lq, (1024, 512, 256, 128)), _pick(n, (3328, 1280, 512, 256)),
                    lambda tq: [], "gqa_flash_fixed_shift")

    def online(_):
        return call(_flash_online_kernel, _pick(lq, (512, 256, 128)), _pick(n, (640, 512, 256, 128)),
                    lambda tq: [pltpu.VMEM((ATT_GROUP, tq, 1), F32)], "gqa_flash_online")
    return lax.cond(bound_log2 <= _FIXED_SHIFT_MAX_BOUND * _LOG2E, fixed, online, None)


def _odd_out_kernel(of_ref, ob_ref, r_ref, att_ref, gn_ref, wo_ref, h_ref, nw_ref, mod_ref, o_ref):
    o = of_ref[...] + ob_ref[...]
    r = r_ref[...].astype(F32)
    silu_r = r * _sigmoid(r)
    parts = []
    for h in range(GLA_HEADS):
        vs = slice(h * GLA_DV, (h + 1) * GLA_DV)
        parts.append((_rms(o[:, vs], gn_ref[:, vs]) * silu_r[:, vs]).astype(BF16))
    gla = jnp.concatenate(parts, axis=1)
    mix = jnp.dot(gla, wo_ref[0:W_GLA_V, :], preferred_element_type=F32)
    mix = mix + jnp.dot(att_ref[...], wo_ref[W_GLA_V:W_GLA_V + W_ATT, :], preferred_element_type=F32)
    o_ref[...] = h_ref[...] + mod_ref[4:5, :] * _rms(mix, nw_ref[...])


def _odd_out(o_f, o_b, zmain, att, gla_norm, w_out, h_all, nw, mod, n_ctx):
    n, d = h_all.shape
    lq = n - n_ctx
    tm = _pick(lq, (256, 128))
    assert n_ctx % tm == 0
    off = n_ctx // tm
    return pl.pallas_call(
        _odd_out_kernel, grid=(lq // tm,),
        in_specs=[pl.BlockSpec((tm, W_GLA_V), lambda i: (i + off, 0)),
                  pl.BlockSpec((tm, W_GLA_V), lambda i: (i + off, 0)),
                  pl.BlockSpec((tm, W_GLA_V), lambda i: (i + off, OR // W_GLA_V)),
                  pl.BlockSpec((tm, W_ATT), lambda i: (i, 0)),
                  pl.BlockSpec((1, W_GLA_V), lambda i: (0, 0)),
                  pl.BlockSpec((d, d), lambda i: (0, 0)),
                  pl.BlockSpec((tm, d), lambda i: (i + off, 0)),
                  pl.BlockSpec((1, d), lambda i: (0, 0)),
                  pl.BlockSpec((8, d), lambda i: (0, 0))],
        out_specs=pl.BlockSpec((tm, d), lambda i: (i, 0)),
        out_shape=jax.ShapeDtypeStruct((lq, d), F32),
        compiler_params=_params(("parallel",), 48),
        name="odd_mixer_out",
    )(o_f, o_b, zmain, att, gla_norm.reshape(1, W_GLA_V), w_out, h_all, nw.reshape(1, d), mod)


def _router_kernel(h_ref, nw_ref, mod_ref, wr_ref, br_ref, v_ref, r_ref):
    x = h_ref[...]
    v = _rms(x, nw_ref[...]) * (1.0 + mod_ref[0:1, :]) + mod_ref[1:2, :]
    v_ref[...] = v
    logits = _dot_hi(v, wr_ref[...]) + br_ref[...]
    lane = lax.broadcasted_iota(jnp.int32, logits.shape, 1)
    m1 = jnp.max(logits, axis=-1, keepdims=True)
    i1 = jnp.min(jnp.where(logits == m1, lane, 128), axis=-1, keepdims=True)
    rest = jnp.where(lane == i1, -jnp.inf, logits)
    m2 = jnp.max(rest, axis=-1, keepdims=True)
    i2 = jnp.min(jnp.where(rest == m2, lane, 128), axis=-1, keepdims=True)
    e2 = jnp.exp(m2 - m1)
    g1 = 1.0 / (1.0 + e2)
    g2 = e2 / (1.0 + e2)
    out = jnp.where(lane == 0, i1.astype(F32), 0.0)
    out = jnp.where(lane == 1, i2.astype(F32), out)
    out = jnp.where(lane == 2, g1, out)
    out = jnp.where(lane == 3, g2, out)
    r_ref[...] = out


def _router(h_lat, nw, mod, w_router, b_router):
    lq, d = h_lat.shape
    tm = _pick(lq, (512, 256, 128))
    wr = jnp.pad(w_router, ((0, 0), (0, 128 - N_EXPERTS)))
    br = jnp.pad(b_router, (0, 128 - N_EXPERTS), constant_values=-1e30).reshape(1, 128)
    return pl.pallas_call(
        _router_kernel, grid=(lq // tm,),
        in_specs=[pl.BlockSpec((tm, d), lambda i: (i, 0)),
                  pl.BlockSpec((1, d), lambda i: (0, 0)),
                  pl.BlockSpec((8, d), lambda i: (0, 0)),
                  pl.BlockSpec((d, 128), lambda i: (0, 0)),
                  pl.BlockSpec((1, 128), lambda i: (0, 0))],
        out_specs=[pl.BlockSpec((tm, d), lambda i: (i, 0)), pl.BlockSpec((tm, 128), lambda i: (i, 0))],
        out_shape=[jax.ShapeDtypeStruct((lq, d), F32), jax.ShapeDtypeStruct((lq, 128), F32)],
        compiler_params=_params(("parallel",), 40),
        name="router_top2",
    )(h_lat, nw.reshape(1, d), mod, wr, br)


def _expert_kernel(te_ref, tv_ref, src_ref, dst_ref, v_hbm, gt_ref, wg_ref, wu_ref, wd_ref, y_hbm,
                   xg_ref, xb_ref, ys_ref, acc_ref, gsem, ssem, *, tm, nt, nf):
    j = pl.program_id(0)
    f = pl.program_id(1)
    slot = j % 2
    per_step = tm // nf
    lo = f * per_step
    valid = tv_ref[j] == 1

    def start_gather(tile, slot_, row):
        tok = src_ref[tile * tm + row]
        pltpu.make_async_copy(v_hbm.at[pl.ds(tok, 1)], xg_ref.at[slot_, pl.ds(row, 1)], gsem.at[slot_]).start()

    def start_scatter(row):
        dst = dst_ref[j * tm + row]
        pltpu.make_async_copy(ys_ref.at[pl.ds(row, 1)], y_hbm.at[pl.ds(dst, 1)], ssem.at[0]).start()

    def gather_loop(tile, slot_, first, count):
        def body(r, carry):
            start_gather(tile, slot_, first + r)
            return carry
        lax.fori_loop(0, count, body, 0, unroll=8)

    @pl.when((j == 0) & (f == 0))
    def _():
        gather_loop(0, 0, 0, tm)
        ys_ref[...] = jnp.zeros_like(ys_ref)

    @pl.when(f == 0)
    def _():
        pltpu.make_async_copy(v_hbm.at[pl.ds(0, tm)], xg_ref.at[slot], gsem.at[slot]).wait()
        xb_ref[...] = xg_ref[slot].astype(BF16)
        acc_ref[...] = jnp.zeros_like(acc_ref)
        for row in range(nf * per_step, tm):
            start_scatter(row)

        @pl.when(j + 1 < nt)
        def _():
            for row in range(nf * per_step, tm):
                start_gather(j + 1, 1 - slot, row)

    @pl.when(valid)
    def _():
        groups = 4

        def start_group(g, after):
            for r in range(g * per_step // groups, (g + 1) * per_step // groups):
                start_gather(j + 1, 1 - slot, lo + r + after)
                start_scatter(lo + r + after)

        def zero_after(val):
            bits = pltpu.bitcast(val[0:8, 0:128], jnp.int32)
            return ((bits & 1) >> 1)[0, 0]

        start_group(0, 0)
        x = xb_ref[...]
        a = jnp.dot(x, wg_ref[...], preferred_element_type=F32)
        start_group(1, zero_after(a))
        b = jnp.dot(x, wu_ref[...], preferred_element_type=F32)
        start_group(2, zero_after(b))
        act = (a * _sigmoid(a) * b).astype(BF16)
        down = jnp.dot(act, wd_ref[...], preferred_element_type=F32)
        start_group(3, zero_after(down))
        acc_ref[...] += down

    @pl.when(jnp.logical_not(valid))
    def _():
        @pl.when(j + 1 < nt)
        def _():
            gather_loop(j + 1, 1 - slot, lo, per_step)

        def body(r, carry):
            start_scatter(lo + r)
            return carry
        lax.fori_loop(0, per_step, body, 0, unroll=8)

    @pl.when(f == nf - 1)
    def _():
        pltpu.make_async_copy(ys_ref, y_hbm.at[pl.ds(0, tm)], ssem.at[0]).wait()
        ys_ref[...] = acc_ref[...] * gt_ref[...]


def _combine_kernel(y0_ref, y1_ref, h_ref, nw_ref, mod_ref, o_ref):
    o_ref[...] = h_ref[...] + mod_ref[4:5, :] * _rms(y0_ref[...] + y1_ref[...], nw_ref[...])


def _count_le(sorted_vals, queries):
    return jnp.sum((sorted_vals[None, :] <= queries[:, None]).astype(jnp.int32), axis=1)


def _moe(h_lat, nw2, nw3, mod, w_router, b_router, wg, wu, wd):
    lq, d = h_lat.shape
    e = N_EXPERTS
    dff = wg.shape[2]
    tm = _pick(lq, (512, 256))
    tf = 1024
    v, route = _router(h_lat, nw2, mod, w_router, b_router)
    idx = route[:, 0:2].astype(jnp.int32)
    gates = route[:, 2:4]

    npairs = 2 * lq
    nt = npairs // tm + e + 1
    flat_e = idx.reshape(-1)
    pair_id = jnp.arange(npairs, dtype=jnp.int32)
    onehot = (flat_e[:, None] == jnp.arange(e, dtype=jnp.int32)[None, :]).astype(jnp.int32)
    csum = jnp.cumsum(onehot, axis=0)
    rank = jnp.sum(csum * onehot, axis=1) - 1
    counts = csum[-1]
    padded = ((counts + tm - 1) // tm) * tm
    ends = jnp.cumsum(padded)
    starts = ends - padded
    pos = jnp.sum(onehot * starts[None, :], axis=1) + rank
    pair = jnp.stack([pair_id, lax.bitcast_convert_type(gates.reshape(-1), jnp.int32)], axis=1)
    empty = jnp.broadcast_to(jnp.array([-1, 0], jnp.int32), (nt * tm, 2))
    slots = empty.at[pos].set(pair)
    sorted_pair = slots[:, 0]
    sorted_gate = lax.bitcast_convert_type(slots[:, 1], F32)
    is_pad = sorted_pair < 0
    src_row = jnp.where(is_pad, 0, sorted_pair >> 1).astype(jnp.int32)
    pad_rank = jnp.cumsum(is_pad.astype(jnp.int32)) - 1
    dst_row = jnp.where(is_pad, npairs + pad_rank, (sorted_pair & 1) * lq + (sorted_pair >> 1)).astype(jnp.int32)
    dst_row = jnp.concatenate([(nt - 1) * tm + jnp.arange(tm, dtype=jnp.int32), dst_row])
    n_used = ends[-1] // tm
    tile_ids = jnp.arange(nt, dtype=jnp.int32)
    tile_valid = (tile_ids < n_used).astype(jnp.int32)
    tile_exp = jnp.minimum(_count_le(ends, jnp.minimum(tile_ids, n_used - 1) * tm), e - 1)

    nf = dff // tf
    pinned = lambda j, f, tv: jnp.where(tv[j] == 1, f, nf - 1)
    y = pl.pallas_call(
        functools.partial(_expert_kernel, tm=tm, nt=nt, nf=nf),
        grid_spec=pltpu.PrefetchScalarGridSpec(
            num_scalar_prefetch=4, grid=(nt, nf),
            in_specs=[pl.BlockSpec(memory_space=pl.ANY),
                      pl.BlockSpec((tm, 1), lambda j, f, te, tv, src, dst: (j, 0)),
                      pl.BlockSpec((None, d, tf), lambda j, f, te, tv, src, dst: (te[j], 0, pinned(j, f, tv))),
                      pl.BlockSpec((None, d, tf), lambda j, f, te, tv, src, dst: (te[j], 0, pinned(j, f, tv))),
                      pl.BlockSpec((None, tf, d), lambda j, f, te, tv, src, dst: (te[j], pinned(j, f, tv), 0))],
            out_specs=pl.BlockSpec(memory_space=pl.ANY),
            scratch_shapes=[pltpu.VMEM((2, tm, d), F32), pltpu.VMEM((tm, d), BF16), pltpu.VMEM((tm, d), F32),
                            pltpu.VMEM((tm, d), F32), pltpu.SemaphoreType.DMA((2,)),
                            pltpu.SemaphoreType.DMA((1,))]),
        out_shape=jax.ShapeDtypeStruct((nt * tm, d), F32),
        compiler_params=pltpu.CompilerParams(dimension_semantics=("arbitrary", "arbitrary"),
                                             vmem_limit_bytes=56 * MIB, disable_bounds_checks=True),
        name="moe_experts",
    )(tile_exp, tile_valid, src_row, dst_row, v, sorted_gate.reshape(nt * tm, 1), wg, wu, wd)

    tc = _pick(lq, (256, 128))
    return pl.pallas_call(
        _combine_kernel, grid=(lq // tc,),
        in_specs=[pl.BlockSpec((tc, d), lambda i: (i, 0)),
                  pl.BlockSpec((tc, d), lambda i: (i + lq // tc, 0)),
                  pl.BlockSpec((tc, d), lambda i: (i, 0)),
                  pl.BlockSpec((1, d), lambda i: (0, 0)),
                  pl.BlockSpec((8, d), lambda i: (0, 0))],
        out_specs=pl.BlockSpec((tc, d), lambda i: (i, 0)),
        out_shape=jax.ShapeDtypeStruct((lq, d), F32),
        compiler_params=_params(("parallel",), 32),
        name="moe_combine",
    )(y, y, h_lat, nw3.reshape(1, d), mod)


def kernel(x, c, ctx, c_ctx, w_mod, b_mod, norms, e_w_in, e_pool_w, e_pool_scale, e_s5_lam_re, e_s5_lam_im, e_s5_log_dt, e_s5_b_re, e_s5_b_im, e_s5_c_re, e_s5_c_im, e_s5_d, e_s5_w_glu, e_w_out, e_ffn_gate, e_ffn_up, e_ffn_down, o_w_in, o_gla_w_a, o_gla_b_a, o_gla_norm, o_q_norm, o_k_norm, o_w_out, o_router, o_router_b, o_moe_gate, o_moe_up, o_moe_down):
    assert x.shape[0] == 1 and w_mod.shape[0] == 2 and e_w_in.shape[0] == 1 and o_w_in.shape[0] == 1
    d = x.shape[2]
    n_ctx = ctx.shape[1]
    h = jnp.concatenate([ctx[0], x[0]], axis=0)

    vecs = jnp.zeros((8, d), F32).at[0].set(c[0]).at[1].set(c_ctx)
    mods = _modulation(vecs, w_mod, b_mod)

    mod1, mod2 = _mod_rows(mods[0], 0), _mod_rows(mods[0], 1)
    z = _normed_matmul(h, norms[0, 0], mod1, e_w_in[0].astype(BF16), n_ctx)
    pool_out = _pool_mixer(z, e_pool_w[0].astype(BF16), e_pool_scale[0], n_ctx)
    m_op, qre, qim, pre, pim, at = _s5_params(e_s5_lam_re[0], e_s5_lam_im[0], e_s5_log_dt[0],
                                              e_s5_b_re[0], e_s5_b_im[0], e_s5_c_re[0], e_s5_c_im[0])
    y_s5 = _s5_mixer(z[:, W_POOL:], m_op, qre, qim, pre, pim, at, n_ctx)
    h = _even_out(pool_out, y_s5, z, e_s5_d[0], e_s5_w_glu[0].astype(BF16), e_w_out[0].astype(BF16),
                  h, norms[0, 1], mod1, n_ctx)
    h = _dense_ffn(h, norms[0, 2], norms[0, 3], mod2, e_ffn_gate[0].astype(BF16),
                   e_ffn_up[0].astype(BF16), e_ffn_down[0].astype(BF16), n_ctx)

    mod1, mod2 = _mod_rows(mods[1], 0), _mod_rows(mods[1], 1)
    w_in = o_w_in[0]
    a0 = 2 * W_GLA_K + W_GLA_V
    w_main = jnp.concatenate([w_in[:, :a0], w_in[:, a0 + GLA_RANK:]], axis=1).astype(BF16)
    w_aux = jnp.pad(w_in[:, a0:a0 + GLA_RANK], ((0, 0), (0, 128 - GLA_RANK))).astype(BF16)
    zmain, a_aux = _normed_matmul(h, norms[1, 0], mod1, w_main, n_ctx, w_aux=w_aux)
    o_f, o_b = _gla_mixer(zmain, a_aux, o_gla_w_a[0], o_gla_b_a[0], n_ctx)
    qx, kx, vx, bound_log2 = _qk_prep(zmain, o_q_norm[0], o_k_norm[0], n_ctx)
    att = _attention(qx, kx, vx, bound_log2)
    h_lat = _odd_out(o_f, o_b, zmain, att, o_gla_norm[0], o_w_out[0].astype(BF16), h, norms[1, 1], mod1, n_ctx)
    out = _moe(h_lat, norms[1, 2], norms[1, 3], mod2, o_router[0], o_router_b[0],
               o_moe_gate[0].astype(BF16), o_moe_up[0].astype(BF16), o_moe_down[0].astype(BF16))
    return out[None]
```

```python
import functools
import math

import jax
import jax.numpy as jnp
from jax import lax
from jax.experimental import pallas as pl
from jax.experimental.pallas import tpu as pltpu

F32 = jnp.float32
BF16 = jnp.bfloat16
HI = lax.Precision.HIGHEST
EPS = 1e-6

D_MODEL = 2048
GRID_W = 64
N_MOD = 6

POOL_WINDOWS = (2, 4, 8, 16)
POOL_GROUP = 384
W_POOL = 1536
W_S5 = 512
S5_CH = 16
S5_STATE = 64
S5_GROUPS = 32
S5_T = 32

GLA_HEADS = 4
GLA_DK = 128
GLA_DV = 256
GLA_RANK = 16
GLA_TAU = 16.0
GLA_CHUNK = 64
W_GLA_K = 512
W_GLA_V = 1024
ATT_HEAD_DIM = 128
ATT_HEADS = 8
ATT_KV_HEADS = 2
ATT_GROUP = 4
W_ATT = 1024
W_ATT_KV = 256
ROPE_THETA = 10000.0

D_FF = 7168
N_EXPERTS = 8

OQ, OK_, OV, OR, OAQ, OAK, OAV = 0, 512, 1024, 2048, 3072, 4096, 4352
W_ODD_MAIN = 4608

MIB = 2 ** 20


def _params(sem, vmem_mib):
    return pltpu.CompilerParams(dimension_semantics=sem, vmem_limit_bytes=vmem_mib * MIB)


def _pick(n, cands):
    for c in cands:
        if n % c == 0:
            return c
    raise ValueError(f"no tile for {n} in {cands}")


def _sigmoid(x):
    return 1.0 / (1.0 + jnp.exp(-x))


def _rms(x, w):
    return x * lax.rsqrt(jnp.mean(x * x, axis=-1, keepdims=True) + EPS) * w


_NORM_CHUNK = 16


def _store_norm_mod(dst_ref, h_ref, nw_ref, mod_ref, row0, n_ctx):
    tm = h_ref.shape[0]
    nw = nw_ref[...]
    w_lat = nw * (1.0 + mod_ref[0:1, :])
    sh_lat = mod_ref[1:2, :]

    def rows(c):
        r0 = pl.multiple_of(c * _NORM_CHUNK, _NORM_CHUNK)
        x = h_ref[pl.ds(r0, _NORM_CHUNK), :]
        return r0, x * lax.rsqrt(jnp.mean(x * x, axis=-1, keepdims=True) + EPS)

    @pl.when(row0 >= n_ctx)
    def _():
        def body(c, carry):
            r0, xr = rows(c)
            dst_ref[pl.ds(r0, _NORM_CHUNK), :] = (xr * w_lat + sh_lat).astype(dst_ref.dtype)
            return carry
        lax.fori_loop(0, tm // _NORM_CHUNK, body, 0, unroll=4)

    @pl.when(row0 < n_ctx)
    def _():
        w_ctx = nw * (1.0 + mod_ref[2:3, :])
        sh_ctx = mod_ref[3:4, :]

        def body(c, carry):
            r0, xr = rows(c)
            is_ctx = row0 + r0 + lax.broadcasted_iota(jnp.int32, (_NORM_CHUNK, 1), 0) < n_ctx
            u = xr * jnp.where(is_ctx, w_ctx, w_lat) + jnp.where(is_ctx, sh_ctx, sh_lat)
            dst_ref[pl.ds(r0, _NORM_CHUNK), :] = u.astype(dst_ref.dtype)
            return carry
        lax.fori_loop(0, tm // _NORM_CHUNK, body, 0, unroll=4)


def _gate_rows(mod_ref, row0, tm, n_ctx):
    rows = row0 + lax.broadcasted_iota(jnp.int32, (tm, 1), 0)
    return jnp.where(rows < n_ctx, mod_ref[5:6, :], mod_ref[4:5, :])


def _mod_kernel(v_ref, w_ref, b_ref, o_ref):
    v = v_ref[...]
    s = v * _sigmoid(v)
    o_ref[...] = jnp.dot(s, w_ref[...], precision=HI, preferred_element_type=F32) + b_ref[...]


def _modulation(vecs, w_mod, b_mod):
    depth, d, n6 = w_mod.shape
    tn = 1024
    return pl.pallas_call(
        _mod_kernel,
        grid=(depth, n6 // tn),
        in_specs=[pl.BlockSpec((8, d), lambda l, j: (0, 0)),
                  pl.BlockSpec((None, d, tn), lambda l, j: (l, 0, j)),
                  pl.BlockSpec((None, 1, tn), lambda l, j: (l, 0, j))],
        out_specs=pl.BlockSpec((None, 8, tn), lambda l, j: (l, 0, j)),
        out_shape=jax.ShapeDtypeStruct((depth, 8, n6), F32),
        compiler_params=_params(("parallel", "parallel"), 40),
        name="modulation",
    )(vecs, w_mod, b_mod.reshape(depth, 1, n6))


def _mod_rows(m, sub):
    d = m.shape[1] // N_MOD
    m6 = m.reshape(8, N_MOD, d)
    sh, sc, g = m6[:, 3 * sub + 0], m6[:, 3 * sub + 1], m6[:, 3 * sub + 2]
    z = jnp.zeros((d,), F32)
    return jnp.stack([sc[0], sh[0], sc[1], sh[1], g[0], g[1], z, z])


def _nmm_kernel(h_ref, nw_ref, mod_ref, w_ref, o_ref, u_ref, *, n_ctx, tm):
    @pl.when(pl.program_id(1) == 0)
    def _():
        _store_norm_mod(u_ref, h_ref, nw_ref, mod_ref, pl.program_id(0) * tm, n_ctx)
    o_ref[...] = jnp.dot(u_ref[...], w_ref[...], preferred_element_type=F32).astype(o_ref.dtype)


def _nmm_aux_kernel(h_ref, nw_ref, mod_ref, w_ref, wa_ref, o_ref, oa_ref, u_ref, *, n_ctx, tm):
    @pl.when(pl.program_id(1) == 0)
    def _():
        _store_norm_mod(u_ref, h_ref, nw_ref, mod_ref, pl.program_id(0) * tm, n_ctx)
        oa_ref[...] = jnp.dot(u_ref[...], wa_ref[...], preferred_element_type=F32)
    o_ref[...] = jnp.dot(u_ref[...], w_ref[...], preferred_element_type=F32).astype(o_ref.dtype)


def _normed_matmul(h, nw, mod, w, n_ctx, w_aux=None):
    n, d = h.shape
    nout = w.shape[1]
    tm = _pick(n, (640, 512, 256, 128))
    tn = _pick(nout, (1536, 1024, 512, 256, 128))
    in_specs = [pl.BlockSpec((tm, d), lambda i, j: (i, 0)),
                pl.BlockSpec((1, d), lambda i, j: (0, 0)),
                pl.BlockSpec((8, d), lambda i, j: (0, 0)),
                pl.BlockSpec((d, tn), lambda i, j: (0, j))]
    out_specs = pl.BlockSpec((tm, tn), lambda i, j: (i, j))
    out_shape = jax.ShapeDtypeStruct((n, nout), BF16)
    args = [h, nw.reshape(1, d), mod, w]
    if w_aux is None:
        kern = functools.partial(_nmm_kernel, n_ctx=n_ctx, tm=tm)
    else:
        na = w_aux.shape[1]
        kern = functools.partial(_nmm_aux_kernel, n_ctx=n_ctx, tm=tm)
        in_specs.append(pl.BlockSpec((d, na), lambda i, j: (0, 0)))
        out_specs = [out_specs, pl.BlockSpec((tm, na), lambda i, j: (i, 0))]
        out_shape = [out_shape, jax.ShapeDtypeStruct((n, na), F32)]
        args.append(w_aux)
    return pl.pallas_call(
        kern, grid=(n // tm, nout // tn), in_specs=in_specs, out_specs=out_specs, out_shape=out_shape,
        scratch_shapes=[pltpu.VMEM((tm, d), BF16)],
        compiler_params=_params(("parallel", "arbitrary"), 48),
        name="normed_proj",
    )(*args)


_POOL_HALO = 16


def _pool_kernel(z_ref, zp_ref, zn_ref, pw_ref, ps_ref, o_ref, ext_ref, *, n_ctx, n_all, tm):
    i = pl.program_id(0)
    row0 = i * tm
    in_ctx = row0 < n_ctx
    seq_start = jnp.where(in_ctx, 0, n_ctx)
    seq_end = jnp.where(in_ctx, n_ctx, n_all)
    has_prev = row0 > seq_start
    has_next = row0 + tm < seq_end
    hl = _POOL_HALO
    ext_ref[0:hl, :] = jnp.where(has_prev, zp_ref[...].astype(F32), 0.0)
    ext_ref[hl:hl + tm, :] = z_ref[...].astype(F32)
    ext_ref[hl + tm:hl + tm + hl, :] = jnp.where(has_next, zn_ref[...].astype(F32), 0.0)
    t = row0 - seq_start + lax.broadcasted_iota(jnp.int32, (tm, 1), 0)
    seq_len = seq_end - seq_start
    for g, w in enumerate(POOL_WINDOWS):
        c0 = g * POOL_GROUP
        acc = None
        for k in range(-(w // 2), w - w // 2):
            v = ext_ref[hl + k:hl + k + tm, c0:c0 + POOL_GROUP]
            acc = v if acc is None else acc + v
        lo = jnp.maximum(t - w // 2, 0)
        hi = jnp.minimum(t + (w - w // 2), seq_len)
        cnt = (hi - lo).astype(F32)
        dlt = acc / cnt - ext_ref[hl:hl + tm, c0:c0 + POOL_GROUP]
        y = jnp.dot(dlt.astype(BF16), pw_ref[g], preferred_element_type=F32)
        o_ref[:, c0:c0 + POOL_GROUP] = (y * ps_ref[:, c0:c0 + POOL_GROUP]).astype(o_ref.dtype)


def _pool_mixer(z, pool_w, pool_scale, n_ctx):
    n = z.shape[0]
    tm = 256
    assert n_ctx % tm == 0 and n % tm == 0
    hl = _POOL_HALO
    nh = n // hl
    per = tm // hl
    kern = functools.partial(_pool_kernel, n_ctx=n_ctx, n_all=n, tm=tm)
    return pl.pallas_call(
        kern, grid=(n // tm,),
        in_specs=[pl.BlockSpec((tm, W_POOL), lambda i: (i, 0)),
                  pl.BlockSpec((hl, W_POOL), lambda i: (jnp.maximum(i * per - 1, 0), 0)),
                  pl.BlockSpec((hl, W_POOL), lambda i: (jnp.minimum((i + 1) * per, nh - 1), 0)),
                  pl.BlockSpec((4, POOL_GROUP, POOL_GROUP), lambda i: (0, 0, 0)),
                  pl.BlockSpec((1, W_POOL), lambda i: (0, 0))],
        out_specs=pl.BlockSpec((tm, W_POOL), lambda i: (i, 0)),
        out_shape=jax.ShapeDtypeStruct((n, W_POOL), BF16),
        scratch_shapes=[pltpu.VMEM((tm + 2 * hl, W_POOL), F32)],
        compiler_params=_params(("parallel",), 32),
        name="pool_mixer",
    )(z, z, z, pool_w, pool_scale.reshape(1, W_POOL))


def _dot_hi(a, b):
    return jnp.dot(a, b, precision=HI, preferred_element_type=F32)


def _dot_t_hi(a, b):
    return lax.dot_general(a, b, (((0,), (0,)), ((), ())), precision=HI, preferred_element_type=F32)


def _s5_param_kernel(lr_ref, li_ref, lrc_ref, lic_ref, ldt_ref, br_ref, bi_ref, cr_ref, ci_ref,
                     m_ref, qre_ref, qim_ref, pre_ref, pim_ref, at_ref):
    t_len = S5_T
    tp = t_len + 8
    wid = t_len * S5_CH
    col = lax.broadcasted_iota(jnp.int32, (tp, wid), 1)
    kid = lax.broadcasted_iota(jnp.int32, (tp, wid), 0)
    cq = col >> 4
    rep_nat = (kid == cq).astype(F32)
    rep_rev = (kid == t_len - 1 - cq).astype(F32)
    til = (lax.broadcasted_iota(jnp.int32, (S5_CH, wid), 0)
           == (lax.broadcasted_iota(jnp.int32, (S5_CH, wid), 1) & (S5_CH - 1))).astype(F32)
    colb = lax.broadcasted_iota(jnp.int32, (S5_CH, wid), 1) >> 4
    kk = lax.broadcasted_iota(jnp.int32, (tp, S5_STATE), 0).astype(F32)
    rows = [None] * t_len
    for d in range(2):
        dt = jnp.exp(ldt_ref[d])
        lr, li = lr_ref[d], li_ref[d]
        mag = jnp.exp(kk * (lr * dt))
        ang = kk * (li * dt)
        pr, pi_ = mag * jnp.cos(ang), mag * jnp.sin(ang)
        lrc, lic = lrc_ref[d], lic_ref[d]
        magc = jnp.exp(lrc * dt)
        arc, aic = magc * jnp.cos(lic * dt), magc * jnp.sin(lic * dt)
        den = lrc * lrc + lic * lic
        nr, ni = arc - 1.0, aic
        kre = (nr * lrc + ni * lic) / den
        kim = (ni * lrc - nr * lic) / den
        bbr = kre * br_ref[d] - kim * bi_ref[d]
        bbi = kre * bi_ref[d] + kim * br_ref[d]
        repq = rep_rev if d == 0 else rep_nat
        repp = ((kid == cq + 1) if d == 0 else (kid == t_len - cq)).astype(F32)
        repm = rep_nat if d == 0 else rep_rev
        bt_r, bt_i = _dot_hi(bbr, til), _dot_hi(bbi, til)
        eqr, eqi = _dot_t_hi(pr, repq), _dot_t_hi(pi_, repq)
        qre_ref[d] = (eqr * bt_r - eqi * bt_i).astype(qre_ref.dtype)
        qim_ref[d] = (eqr * bt_i + eqi * bt_r).astype(qim_ref.dtype)
        epr, epi = _dot_t_hi(pr, repp), _dot_t_hi(pi_, repp)
        ct_r, ct_i = _dot_t_hi(cr_ref[d], til), _dot_t_hi(ci_ref[d], til)
        pre_ref[d] = (ct_r * epr - ct_i * epi).astype(pre_ref.dtype)
        pim_ref[d] = (-(ct_r * epi + ct_i * epr)).astype(pim_ref.dtype)
        at_ref[d, 0:1, :] = pr[t_len:t_len + 1, :]
        at_ref[d, 1:2, :] = pi_[t_len:t_len + 1, :]
        emr, emi = _dot_t_hi(pr, repm), _dot_t_hi(pi_, repm)
        y_re = emr * ct_r - emi * ct_i
        y_im = emr * ct_i + emi * ct_r
        r = _dot_t_hi(bbr, y_re) - _dot_t_hi(bbi, y_im)
        for s in range(t_len):
            if d == 0:
                blk = jnp.where(colb >= s, pltpu.roll(r, S5_CH * s, 1), 0.0)
            else:
                blk = jnp.where(colb <= s, pltpu.roll(r, (S5_CH * (s + 1)) % wid, 1), 0.0)
            rows[s] = blk if rows[s] is None else rows[s] + blk
    for s in range(t_len):
        m_ref[s * S5_CH:(s + 1) * S5_CH, :] = rows[s].astype(m_ref.dtype)


def _s5_params(lam_re, lam_im, log_dt, b_re, b_im, c_re, c_im):
    g, p, n = S5_GROUPS, S5_STATE, S5_CH
    wid = S5_T * S5_CH

    def spec(*shape):
        return pl.BlockSpec((2, None) + shape, lambda gi: (0, gi) + (0,) * len(shape))

    def ospec(*shape):
        return pl.BlockSpec((None,) + shape, lambda gi: (gi,) + (0,) * len(shape))
    return pl.pallas_call(
        _s5_param_kernel, grid=(g,),
        in_specs=[spec(1, p), spec(1, p), spec(p, 1), spec(p, 1), spec(1, 1),
                  spec(p, n), spec(p, n), spec(n, p), spec(n, p)],
        out_specs=[ospec(wid, wid), ospec(2, p, wid), ospec(2, p, wid), ospec(2, p, wid), ospec(2, p, wid),
                   ospec(2, 2, p)],
        out_shape=[jax.ShapeDtypeStruct((g, wid, wid), BF16),
                   jax.ShapeDtypeStruct((g, 2, p, wid), BF16),
                   jax.ShapeDtypeStruct((g, 2, p, wid), BF16),
                   jax.ShapeDtypeStruct((g, 2, p, wid), BF16),
                   jax.ShapeDtypeStruct((g, 2, p, wid), BF16),
                   jax.ShapeDtypeStruct((g, 2, 2, p), F32)],
        compiler_params=_params(("parallel",), 32),
        name="s5_params",
    )(lam_re.reshape(2, g, 1, p), lam_im.reshape(2, g, 1, p), lam_re.reshape(2, g, p, 1),
      lam_im.reshape(2, g, p, 1), log_dt.reshape(2, g, 1, 1), b_re, b_im, c_re, c_im)


def _s5_state_kernel(u_ref, qre_ref, qim_ref, s_ref):
    u = u_ref[...]
    dn = (((1,), (1,)), ((), ()))
    s_ref[:, 0:128] = lax.dot_general(u, qre_ref[...], dn, preferred_element_type=F32)
    s_ref[:, 128:256] = lax.dot_general(u, qim_ref[...], dn, preferred_element_type=F32)


def _s5_scan_kernel(s_ref, are_ref, aim_ref, h_ref, *, nc, ncc):
    ar, ai = are_ref[...], aim_ref[...]
    is_f = lax.broadcasted_iota(jnp.int32, ar.shape, 1) < S5_STATE

    def body(i, carry):
        hr, hi = carry
        cf = i
        cb = jnp.where(i < ncc, ncc - 1 - i, nc + ncc - 1 - i)
        h_ref[cf, :, 0:64] = hr[:, 0:64]
        h_ref[cb, :, 64:128] = hr[:, 64:128]
        h_ref[cf, :, 128:192] = hi[:, 0:64]
        h_ref[cb, :, 192:256] = hi[:, 64:128]
        sf, sb = s_ref[cf], s_ref[cb]
        sr = jnp.where(is_f, sf[:, 0:128], sb[:, 0:128])
        si = jnp.where(is_f, sf[:, 128:256], sb[:, 128:256])
        return ar * hr - ai * hi + sr, ar * hi + ai * hr + si

    zero = jnp.zeros(ar.shape, F32)
    lax.fori_loop(0, nc, body, (zero, zero))


def _s5_out_kernel(u_ref, m_ref, h_ref, p_ref, y_ref):
    y = jnp.dot(u_ref[...], m_ref[...], preferred_element_type=F32)
    y_ref[...] = y + jnp.dot(h_ref[...].astype(BF16), p_ref[...], preferred_element_type=F32)


def _s5_mixer(s_all, m, qre, qim, pre, pim, at, n_ctx):
    n = s_all.shape[0]
    g, t_len, wid = S5_GROUPS, S5_T, S5_T * S5_CH
    nc, ncc = n // t_len, n_ctx // t_len
    assert n % t_len == 0 and n_ctx % t_len == 0
    u = s_all.reshape(nc, t_len, g, S5_CH).transpose(2, 0, 1, 3).reshape(g, nc, wid)
    p_all = jnp.concatenate([pre.reshape(g, 128, wid), pim.reshape(g, 128, wid)], axis=1)
    a_re = at[:, :, 0, :].reshape(g, 128)
    a_im = at[:, :, 1, :].reshape(g, 128)
    s_t = pl.pallas_call(
        _s5_state_kernel, grid=(g,),
        in_specs=[pl.BlockSpec((None, nc, wid), lambda i: (i, 0, 0)),
                  pl.BlockSpec((None, 128, wid), lambda i: (i, 0, 0)),
                  pl.BlockSpec((None, 128, wid), lambda i: (i, 0, 0))],
        out_specs=pl.BlockSpec((nc, 256), lambda i: (0, i)),
        out_shape=jax.ShapeDtypeStruct((nc, g * 256), F32),
        compiler_params=_params(("parallel",), 32),
        name="s5_chunk_state",
    )(u, qre.reshape(g, 128, wid), qim.reshape(g, 128, wid))
    h_t = pl.pallas_call(
        functools.partial(_s5_scan_kernel, nc=nc, ncc=ncc),
        out_shape=jax.ShapeDtypeStruct((nc, g, 256), F32),
        compiler_params=pltpu.CompilerParams(vmem_limit_bytes=48 * MIB),
        name="s5_chunk_scan",
    )(s_t.reshape(nc, g, 256), a_re, a_im).reshape(nc, g * 256)
    y = pl.pallas_call(
        _s5_out_kernel, grid=(g,),
        in_specs=[pl.BlockSpec((None, nc, wid), lambda i: (i, 0, 0)),
                  pl.BlockSpec((None, wid, wid), lambda i: (i, 0, 0)),
                  pl.BlockSpec((nc, 256), lambda i: (0, i)),
                  pl.BlockSpec((None, 256, wid), lambda i: (i, 0, 0))],
        out_specs=pl.BlockSpec((None, nc, wid), lambda i: (i, 0, 0)),
        out_shape=jax.ShapeDtypeStruct((g, nc, wid), F32),
        compiler_params=_params(("parallel",), 32),
        name="s5_chunk_out",
    )(u, m, h_t, p_all)
    return y.reshape(g, nc, t_len, S5_CH).transpose(1, 2, 0, 3).reshape(n, W_S5)


def _gelu_tanh(x):
    return 0.5 * x * (1.0 + jnp.tanh(math.sqrt(2.0 / math.pi) * (x + 0.044715 * (x * x * x))))


def _even_out_kernel(pool_ref, y_ref, s_ref, dsk_ref, wglu_ref, wo_ref, h_ref, nw_ref, mod_ref, o_ref,
                     *, n_ctx, tm):
    y = _gelu_tanh(y_ref[...] + s_ref[...].astype(F32) * dsk_ref[...])
    gate = jnp.dot(y.astype(BF16), wglu_ref[...], preferred_element_type=F32)
    s5 = (y * _sigmoid(gate)).astype(BF16)
    mix = jnp.dot(pool_ref[...], wo_ref[0:W_POOL, :], preferred_element_type=F32)
    mix = mix + jnp.dot(s5, wo_ref[W_POOL:W_POOL + W_S5, :], preferred_element_type=F32)
    g = _gate_rows(mod_ref, pl.program_id(0) * tm, tm, n_ctx)
    o_ref[...] = h_ref[...] + g * _rms(mix, nw_ref[...])


def _even_out(pool_out, y_s5, z, dsk, w_glu, w_out, h, nw, mod, n_ctx):
    n, d = h.shape
    tm = _pick(n, (320, 256, 128))
    return pl.pallas_call(
        functools.partial(_even_out_kernel, n_ctx=n_ctx, tm=tm), grid=(n // tm,),
        in_specs=[pl.BlockSpec((tm, W_POOL), lambda i: (i, 0)),
                  pl.BlockSpec((tm, W_S5), lambda i: (i, 0)),
                  pl.BlockSpec((tm, W_S5), lambda i: (i, W_POOL // W_S5)),
                  pl.BlockSpec((1, W_S5), lambda i: (0, 0)),
                  pl.BlockSpec((W_S5, W_S5), lambda i: (0, 0)),
                  pl.BlockSpec((d, d), lambda i: (0, 0)),
                  pl.BlockSpec((tm, d), lambda i: (i, 0)),
                  pl.BlockSpec((1, d), lambda i: (0, 0)),
                  pl.BlockSpec((8, d), lambda i: (0, 0))],
        out_specs=pl.BlockSpec((tm, d), lambda i: (i, 0)),
        out_shape=jax.ShapeDtypeStruct((n, d), F32),
        compiler_params=_params(("parallel",), 48),
        name="even_mixer_out",
    )(pool_out, y_s5, z, dsk.reshape(1, W_S5), w_glu, w_out, h, nw.reshape(1, d), mod)


def _ffn_kernel(h_ref, nw2_ref, nw3_ref, mod_ref, wg_ref, wu_ref, wd_ref, o_ref, v_ref, *, n_ctx, tm):
    f = pl.program_id(1)

    @pl.when(f == 0)
    def _():
        _store_norm_mod(v_ref, h_ref, nw2_ref, mod_ref, pl.program_id(0) * tm, n_ctx)
        o_ref[...] = jnp.zeros_like(o_ref)
    v = v_ref[...]
    a = jnp.dot(v, wg_ref[...], preferred_element_type=F32)
    b = jnp.dot(v, wu_ref[...], preferred_element_type=F32)
    act = (a * _sigmoid(a) * b).astype(BF16)
    o_ref[...] += jnp.dot(act, wd_ref[...], preferred_element_type=F32)

    @pl.when(f == pl.num_programs(1) - 1)
    def _():
        g = _gate_rows(mod_ref, pl.program_id(0) * tm, tm, n_ctx)
        o_ref[...] = h_ref[...] + g * _rms(o_ref[...], nw3_ref[...])


def _dense_ffn(h, nw2, nw3, mod, wg, wu, wd, n_ctx):
    n, d = h.shape
    dff = wg.shape[1]
    tm = _pick(n, (640, 512, 256, 128))
    tf = 512
    return pl.pallas_call(
        functools.partial(_ffn_kernel, n_ctx=n_ctx, tm=tm), grid=(n // tm, dff // tf),
        in_specs=[pl.BlockSpec((tm, d), lambda i, f: (i, 0)),
                  pl.BlockSpec((1, d), lambda i, f: (0, 0)),
                  pl.BlockSpec((1, d), lambda i, f: (0, 0)),
                  pl.BlockSpec((8, d), lambda i, f: (0, 0)),
                  pl.BlockSpec((d, tf), lambda i, f: (0, f)),
                  pl.BlockSpec((d, tf), lambda i, f: (0, f)),
                  pl.BlockSpec((tf, d), lambda i, f: (f, 0))],
        out_specs=pl.BlockSpec((tm, d), lambda i, f: (i, 0)),
        out_shape=jax.ShapeDtypeStruct((n, d), F32),
        scratch_shapes=[pltpu.VMEM((tm, d), BF16)],
        compiler_params=_params(("parallel", "arbitrary"), 56),
        name="dense_swiglu",
    )(h, nw2.reshape(1, d), nw3.reshape(1, d), mod, wg, wu, wd)


def _split_bf16(x):
    hi = x.astype(BF16)
    return hi, (x - hi.astype(F32)).astype(BF16)


def _gla_kernel(qf_ref, kf_ref, vf_ref, af_ref, qb_ref, kb_ref, vb_ref, ab_ref, wa_ref, ba_ref,
                of_ref, ob_ref, st_ref, *, tb):
    c = GLA_CHUNK
    nsub = tb // c

    @pl.when(pl.program_id(0) == 0)
    def _():
        st_ref[...] = jnp.zeros_like(st_ref)
    ri = lax.broadcasted_iota(jnp.int32, (c, c), 0)
    ci = lax.broadcasted_iota(jnp.int32, (c, c), 1)
    scale = GLA_DK ** -0.5
    nt = (((1,), (1,)), ((), ()))
    tn = (((0,), (0,)), ((), ()))
    dirs = ((qf_ref, kf_ref, vf_ref, af_ref, of_ref), (qb_ref, kb_ref, vb_ref, ab_ref, ob_ref))
    for d, (q_ref, k_ref, v_ref, a_ref, o_ref) in enumerate(dirs):
        mask = (ri >= ci) if d == 0 else (ci >= ri)
        cum = mask.astype(BF16)
        a_hi, a_lo = _split_bf16(a_ref[...])
        w_hi, w_lo = _split_bf16(wa_ref[d])
        z = (jnp.dot(a_hi, w_hi, preferred_element_type=F32) + jnp.dot(a_hi, w_lo, preferred_element_type=F32)
             + jnp.dot(a_lo, w_hi, preferred_element_type=F32)) + ba_ref[d]
        glog = (jnp.minimum(z, 0.0) - jnp.log(1.0 + jnp.exp(-jnp.abs(z)))) * (1.0 / GLA_TAU)
        order = range(nsub) if d == 0 else range(nsub - 1, -1, -1)
        for sc in order:
            r0 = sc * c
            g_hi, g_lo = _split_bf16(glog[r0:r0 + c, :])
            b = jnp.dot(cum, g_hi, preferred_element_type=F32) + jnp.dot(cum, g_lo, preferred_element_type=F32)
            b_last = b[c - 1:c, :] if d == 0 else b[0:1, :]
            dec = jnp.exp(b_last)
            q = q_ref[r0:r0 + c, :].astype(F32) * scale
            k = k_ref[r0:r0 + c, :].astype(F32)
            qe = (q * jnp.exp(b)).astype(BF16)
            ke = (k * jnp.exp(-b)).astype(BF16)
            kd = (k * jnp.exp(b_last - b)).astype(BF16)
            for h in range(GLA_HEADS):
                ks = slice(h * GLA_DK, (h + 1) * GLA_DK)
                vs = slice(h * GLA_DV, (h + 1) * GLA_DV)
                att = lax.dot_general(qe[:, ks], ke[:, ks], nt, preferred_element_type=F32)
                att = jnp.where(mask, att, 0.0).astype(BF16)
                vh = v_ref[r0:r0 + c, vs]
                s_old = st_ref[d, h]
                o = jnp.dot(att, vh, preferred_element_type=F32)
                o = o + lax.dot_general(qe[:, ks], s_old.astype(BF16), nt, preferred_element_type=F32)
                o_ref[r0:r0 + c, vs] = o
                upd = lax.dot_general(vh, kd[:, ks], tn, preferred_element_type=F32)
                st_ref[d, h] = s_old * dec[:, ks] + upd


def _gla_mixer(zmain, a_aux, w_a, b_a, n_ctx):
    n = zmain.shape[0]
    tb = 256
    assert n % tb == 0 and n_ctx % tb == 0
    nb, ncb = n // tb, n_ctx // tb

    def bwd(s):
        return jnp.where(s < ncb, ncb - 1 - s, nb + ncb - 1 - s)
    wa = jnp.pad(w_a, ((0, 0), (0, 128 - GLA_RANK), (0, 0)))
    in_specs = []
    for order in (lambda s: s, bwd):
        in_specs += [pl.BlockSpec((tb, W_GLA_K), lambda s, o=order: (o(s), OQ // W_GLA_K)),
                     pl.BlockSpec((tb, W_GLA_K), lambda s, o=order: (o(s), OK_ // W_GLA_K)),
                     pl.BlockSpec((tb, W_GLA_V), lambda s, o=order: (o(s), OV // W_GLA_V)),
                     pl.BlockSpec((tb, 128), lambda s, o=order: (o(s), 0))]
    in_specs += [pl.BlockSpec((2, 128, W_GLA_K), lambda s: (0, 0, 0)),
                 pl.BlockSpec((2, 1, W_GLA_K), lambda s: (0, 0, 0))]
    return pl.pallas_call(
        functools.partial(_gla_kernel, tb=tb), grid=(nb,),
        in_specs=in_specs,
        out_specs=[pl.BlockSpec((tb, W_GLA_V), lambda s: (s, 0)),
                   pl.BlockSpec((tb, W_GLA_V), lambda s: (bwd(s), 0))],
        out_shape=[jax.ShapeDtypeStruct((n, W_GLA_V), F32)] * 2,
        scratch_shapes=[pltpu.VMEM((2, GLA_HEADS, GLA_DV, GLA_DK), F32)],
        compiler_params=_params(("arbitrary",), 32),
        name="gla_chunked",
    )(zmain, zmain, zmain, a_aux, zmain, zmain, zmain, a_aux, wa, b_a.reshape(2, 1, W_GLA_K))


_ATT_PAD = 2 * ATT_HEAD_DIM
_LOG2E = 1.4426950408889634
_FIXED_SHIFT_MAX_BOUND = 40.0


def _rope_fn(row0, tm, n_ctx):
    hd = ATT_HEAD_DIM
    row = row0 + lax.broadcasted_iota(jnp.int32, (tm, hd), 0)
    lane = lax.broadcasted_iota(jnp.int32, (tm, hd), 1)
    t = row - n_ctx
    pos = jnp.where(lane < hd // 2, t // GRID_W, t % GRID_W).astype(F32)
    quarter = hd // 4
    freq = jnp.exp((lane % quarter).astype(F32) * (-math.log(ROPE_THETA) / quarter))
    ang = pos * freq
    first = (lane & quarter) == 0
    cosv = jnp.cos(ang)
    sin_s = jnp.where(first, -jnp.sin(ang), jnp.sin(ang))

    def rope(x):
        sw = jnp.where(first, pltpu.roll(x, hd - quarter, 1), pltpu.roll(x, quarter, 1))
        return x * cosv + sw * sin_s
    return rope, row >= n_ctx


def _k_prep_kernel(k_ref, v_ref, kw_ref, kx_ref, vx_ref, kmax_ref, *, n_ctx, tm):
    hd = ATT_HEAD_DIM
    rope, is_lat = _rope_fn(pl.program_id(0) * tm, tm, n_ctx)
    lane = lax.broadcasted_iota(jnp.int32, (tm, hd), 1)
    one_col = jnp.where(lane == 0, 1.0, 0.0).astype(BF16)
    ones = jnp.ones((tm, hd), BF16)
    nmax = jnp.zeros((tm, 1), F32)
    for h in range(ATT_KV_HEADS):
        sl = slice(h * hd, (h + 1) * hd)
        xn = _rms(k_ref[:, sl].astype(F32), kw_ref[...])
        kr = jnp.where(is_lat, rope(xn), xn).astype(BF16)
        kx_ref[:, h * _ATT_PAD:h * _ATT_PAD + hd] = kr
        kx_ref[:, h * _ATT_PAD + hd:(h + 1) * _ATT_PAD] = one_col
        vx_ref[:, h * _ATT_PAD:h * _ATT_PAD + hd] = v_ref[:, sl]
        vx_ref[:, h * _ATT_PAD + hd:(h + 1) * _ATT_PAD] = ones
        krf = kr.astype(F32)
        nmax = jnp.maximum(nmax, jnp.sum(krf * krf, axis=-1, keepdims=True))
    kmax_ref[...] = jnp.broadcast_to(jnp.max(nmax, axis=0, keepdims=True), kmax_ref.shape)


def _q_prep_kernel(q_ref, qw_ref, kn_ref, qx_ref, bmax_ref, *, n_ctx, tm):
    hd = ATT_HEAD_DIM
    rope, _ = _rope_fn(n_ctx + pl.program_id(0) * tm, tm, n_ctx)
    lane = lax.broadcasted_iota(jnp.int32, (tm, hd), 1)
    scale = hd ** -0.5 * _LOG2E
    bmax = jnp.zeros((tm, 1), F32)
    for h in range(ATT_HEADS):
        sl = slice(h * hd, (h + 1) * hd)
        xn = _rms(q_ref[:, sl].astype(F32), qw_ref[...])
        qr = (rope(xn) * scale).astype(BF16)
        qf = qr.astype(F32)
        bound = jnp.sqrt(jnp.sum(qf * qf, axis=-1, keepdims=True)) * kn_ref[...]
        qx_ref[:, h * _ATT_PAD:h * _ATT_PAD + hd] = qr
        qx_ref[:, h * _ATT_PAD + hd:(h + 1) * _ATT_PAD] = jnp.where(lane == 0, -bound, 0.0).astype(BF16)
        bmax = jnp.maximum(bmax, bound)
    bmax_ref[...] = jnp.broadcast_to(jnp.max(bmax, axis=0, keepdims=True), bmax_ref.shape)


def _qk_prep(zmain, q_norm, k_norm, n_ctx):
    n = zmain.shape[0]
    lq = n - n_ctx
    tm = 256
    hd = ATT_HEAD_DIM
    assert n_ctx % tm == 0
    ncb = n_ctx // tm
    kx, vx, kmax = pl.pallas_call(
        functools.partial(_k_prep_kernel, n_ctx=n_ctx, tm=tm), grid=(n // tm,),
        in_specs=[pl.BlockSpec((tm, W_ATT_KV), lambda i: (i, OAK // W_ATT_KV)),
                  pl.BlockSpec((tm, W_ATT_KV), lambda i: (i, OAV // W_ATT_KV)),
                  pl.BlockSpec((1, hd), lambda i: (0, 0))],
        out_specs=[pl.BlockSpec((tm, ATT_KV_HEADS * _ATT_PAD), lambda i: (i, 0)),
                   pl.BlockSpec((tm, ATT_KV_HEADS * _ATT_PAD), lambda i: (i, 0)),
                   pl.BlockSpec((None, 8, 128), lambda i: (i, 0, 0))],
        out_shape=[jax.ShapeDtypeStruct((n, ATT_KV_HEADS * _ATT_PAD), BF16),
                   jax.ShapeDtypeStruct((n, ATT_KV_HEADS * _ATT_PAD), BF16),
                   jax.ShapeDtypeStruct((n // tm, 8, 128), F32)],
        compiler_params=_params(("parallel",), 32),
        name="k_norm_rope",
    )(zmain, zmain, k_norm.reshape(1, hd))
    knorm = jnp.sqrt(jnp.max(kmax)).reshape(1, 1)
    qx, bmax = pl.pallas_call(
        functools.partial(_q_prep_kernel, n_ctx=n_ctx, tm=tm), grid=(lq // tm,),
        in_specs=[pl.BlockSpec((tm, W_ATT), lambda i: (i + ncb, OAQ // W_ATT)),
                  pl.BlockSpec((1, hd), lambda i: (0, 0)),
                  pl.BlockSpec((1, 1), lambda i: (0, 0))],
        out_specs=[pl.BlockSpec((tm, ATT_HEADS * _ATT_PAD), lambda i: (i, 0)),
                   pl.BlockSpec((None, 8, 128), lambda i: (i, 0, 0))],
        out_shape=[jax.ShapeDtypeStruct((lq, ATT_HEADS * _ATT_PAD), BF16),
                   jax.ShapeDtypeStruct((lq // tm, 8, 128), F32)],
        compiler_params=_params(("parallel",), 32),
        name="q_norm_rope",
    )(zmain, q_norm.reshape(1, hd), knorm)
    return qx, kx, vx, jnp.max(bmax)


def _flash_fixed_kernel(q_ref, kt_ref, v_ref, o_ref, acc_ref):
    c = pl.program_id(2)
    hd = ATT_HEAD_DIM

    @pl.when(c == 0)
    def _():
        acc_ref[...] = jnp.zeros_like(acc_ref)
    kt = kt_ref[...]
    v = v_ref[...]
    for g in range(ATT_GROUP):
        s = jnp.dot(q_ref[:, g * _ATT_PAD:(g + 1) * _ATT_PAD], kt, preferred_element_type=F32)
        acc_ref[g] += jnp.dot(jnp.exp2(s).astype(BF16), v, preferred_element_type=F32)

    @pl.when(c == pl.num_programs(2) - 1)
    def _():
        for g in range(ATT_GROUP):
            a = acc_ref[g]
            o_ref[:, g * hd:(g + 1) * hd] = (a[:, 0:hd] / a[:, hd:2 * hd]).astype(o_ref.dtype)


def _flash_online_kernel(q_ref, kt_ref, v_ref, o_ref, m_ref, acc_ref):
    c = pl.program_id(2)
    hd = ATT_HEAD_DIM

    @pl.when(c == 0)
    def _():
        m_ref[...] = jnp.full(m_ref.shape, -jnp.inf, F32)
        acc_ref[...] = jnp.zeros_like(acc_ref)
    kt = kt_ref[...]
    v = v_ref[...]
    for g in range(ATT_GROUP):
        s = jnp.dot(q_ref[:, g * _ATT_PAD:(g + 1) * _ATT_PAD], kt, preferred_element_type=F32)
        m_prev = m_ref[g]
        m_new = jnp.maximum(m_prev, jnp.max(s, axis=-1, keepdims=True))
        p = jnp.exp2(s - m_new).astype(BF16)
        acc_ref[g] = jnp.exp2(m_prev - m_new) * acc_ref[g] + jnp.dot(p, v, preferred_element_type=F32)
        m_ref[g] = m_new

    @pl.when(c == pl.num_programs(2) - 1)
    def _():
        for g in range(ATT_GROUP):
            a = acc_ref[g]
            o_ref[:, g * hd:(g + 1) * hd] = (a[:, 0:hd] / a[:, hd:2 * hd]).astype(o_ref.dtype)


def _attention(qx, kx, vx, bound_log2):
    lq = qx.shape[0]
    n = kx.shape[0]
    kt = kx.T
    gw = ATT_GROUP * _ATT_PAD
    ow = ATT_GROUP * ATT_HEAD_DIM
    def call(kern, tq, tk, scratch, name):
        return pl.pallas_call(
            kern, grid=(ATT_KV_HEADS, lq // tq, n // tk),
            in_specs=[pl.BlockSpec((tq, gw), lambda h, i, j: (i, h)),
                      pl.BlockSpec((_ATT_PAD, tk), lambda h, i, j: (h, j)),
                      pl.BlockSpec((tk, _ATT_PAD), lambda h, i, j: (j, h))],
            out_specs=pl.BlockSpec((tq, ow), lambda h, i, j: (i, h)),
            out_shape=jax.ShapeDtypeStruct((lq, W_ATT), BF16),
            scratch_shapes=scratch(tq) + [pltpu.VMEM((ATT_GROUP, tq, _ATT_PAD), F32)],
            compiler_params=_params(("parallel", "parallel", "arbitrary"), 56),
            name=name,
        )(qx, kt, vx)

    def fixed(_):
        return call(_flash_fixed_kernel, _pick(lq, (1024, 512, 256, 128)), _pick(n, (3328, 1280, 512, 256)),
                    lambda tq: [], "gqa_flash_fixed_shift")

    def online(_):
        return call(_flash_online_kernel, _pick(lq, (512, 256, 128)), _pick(n, (640, 512, 256, 128)),
                    lambda tq: [pltpu.VMEM((ATT_GROUP, tq, 1), F32)], "gqa_flash_online")
    return lax.cond(bound_log2 <= _FIXED_SHIFT_MAX_BOUND * _LOG2E, fixed, online, None)


def _odd_out_kernel(of_ref, ob_ref, r_ref, att_ref, gn_ref, wo_ref, h_ref, nw_ref, mod_ref, o_ref):
    o = of_ref[...] + ob_ref[...]
    r = r_ref[...].astype(F32)
    silu_r = r * _sigmoid(r)
    parts = []
    for h in range(GLA_HEADS):
        vs = slice(h * GLA_DV, (h + 1) * GLA_DV)
        parts.append((_rms(o[:, vs], gn_ref[:, vs]) * silu_r[:, vs]).astype(BF16))
    gla = jnp.concatenate(parts, axis=1)
    mix = jnp.dot(gla, wo_ref[0:W_GLA_V, :], preferred_element_type=F32)
    mix = mix + jnp.dot(att_ref[...], wo_ref[W_GLA_V:W_GLA_V + W_ATT, :], preferred_element_type=F32)
    o_ref[...] = h_ref[...] + mod_ref[4:5, :] * _rms(mix, nw_ref[...])


def _odd_out(o_f, o_b, zmain, att, gla_norm, w_out, h_all, nw, mod, n_ctx):
    n, d = h_all.shape
    lq = n - n_ctx
    tm = _pick(lq, (256, 128))
    assert n_ctx % tm == 0
    off = n_ctx // tm
    return pl.pallas_call(
        _odd_out_kernel, grid=(lq // tm,),
        in_specs=[pl.BlockSpec((tm, W_GLA_V), lambda i: (i + off, 0)),
                  pl.BlockSpec((tm, W_GLA_V), lambda i: (i + off, 0)),
                  pl.BlockSpec((tm, W_GLA_V), lambda i: (i + off, OR // W_GLA_V)),
                  pl.BlockSpec((tm, W_ATT), lambda i: (i, 0)),
                  pl.BlockSpec((1, W_GLA_V), lambda i: (0, 0)),
                  pl.BlockSpec((d, d), lambda i: (0, 0)),
                  pl.BlockSpec((tm, d), lambda i: (i + off, 0)),
                  pl.BlockSpec((1, d), lambda i: (0, 0)),
                  pl.BlockSpec((8, d), lambda i: (0, 0))],
        out_specs=pl.BlockSpec((tm, d), lambda i: (i, 0)),
        out_shape=jax.ShapeDtypeStruct((lq, d), F32),
        compiler_params=_params(("parallel",), 48),
        name="odd_mixer_out",
    )(o_f, o_b, zmain, att, gla_norm.reshape(1, W_GLA_V), w_out, h_all, nw.reshape(1, d), mod)


def _router_kernel(h_ref, nw_ref, mod_ref, wr_ref, br_ref, v_ref, r_ref):
    x = h_ref[...]
    v = _rms(x, nw_ref[...]) * (1.0 + mod_ref[0:1, :]) + mod_ref[1:2, :]
    v_ref[...] = v
    logits = _dot_hi(v, wr_ref[...]) + br_ref[...]
    lane = lax.broadcasted_iota(jnp.int32, logits.shape, 1)
    m1 = jnp.max(logits, axis=-1, keepdims=True)
    i1 = jnp.min(jnp.where(logits == m1, lane, 128), axis=-1, keepdims=True)
    rest = jnp.where(lane == i1, -jnp.inf, logits)
    m2 = jnp.max(rest, axis=-1, keepdims=True)
    i2 = jnp.min(jnp.where(rest == m2, lane, 128), axis=-1, keepdims=True)
    e2 = jnp.exp(m2 - m1)
    g1 = 1.0 / (1.0 + e2)
    g2 = e2 / (1.0 + e2)
    out = jnp.where(lane == 0, i1.astype(F32), 0.0)
    out = jnp.where(lane == 1, i2.astype(F32), out)
    out = jnp.where(lane == 2, g1, out)
    out = jnp.where(lane == 3, g2, out)
    r_ref[...] = out


def _router(h_lat, nw, mod, w_router, b_router):
    lq, d = h_lat.shape
    tm = _pick(lq, (512, 256, 128))
    wr = jnp.pad(w_router, ((0, 0), (0, 128 - N_EXPERTS)))
    br = jnp.pad(b_router, (0, 128 - N_EXPERTS), constant_values=-1e30).reshape(1, 128)
    return pl.pallas_call(
        _router_kernel, grid=(lq // tm,),
        in_specs=[pl.BlockSpec((tm, d), lambda i: (i, 0)),
                  pl.BlockSpec((1, d), lambda i: (0, 0)),
                  pl.BlockSpec((8, d), lambda i: (0, 0)),
                  pl.BlockSpec((d, 128), lambda i: (0, 0)),
                  pl.BlockSpec((1, 128), lambda i: (0, 0))],
        out_specs=[pl.BlockSpec((tm, d), lambda i: (i, 0)), pl.BlockSpec((tm, 128), lambda i: (i, 0))],
        out_shape=[jax.ShapeDtypeStruct((lq, d), F32), jax.ShapeDtypeStruct((lq, 128), F32)],
        compiler_params=_params(("parallel",), 40),
        name="router_top2",
    )(h_lat, nw.reshape(1, d), mod, wr, br)


def _expert_kernel(te_ref, tv_ref, src_ref, dst_ref, v_hbm, gt_ref, wg_ref, wu_ref, wd_ref, y_hbm,
                   xg_ref, xb_ref, ys_ref, acc_ref, gsem, ssem, *, tm, nt, nf):
    j = pl.program_id(0)
    f = pl.program_id(1)
    slot = j % 2
    per_step = tm // nf
    lo = f * per_step
    valid = tv_ref[j] == 1

    def start_gather(tile, slot_, row):
        tok = src_ref[tile * tm + row]
        pltpu.make_async_copy(v_hbm.at[pl.ds(tok, 1)], xg_ref.at[slot_, pl.ds(row, 1)], gsem.at[slot_]).start()

    def start_scatter(row):
        dst = dst_ref[j * tm + row]
        pltpu.make_async_copy(ys_ref.at[1 - slot, pl.ds(row, 1)], y_hbm.at[pl.ds(dst, 1)],
                              ssem.at[1 - slot]).start()

    def wait_scatter(slot_):
        pltpu.make_async_copy(ys_ref.at[slot_], y_hbm.at[pl.ds(0, tm)], ssem.at[slot_]).wait()

    def gather_loop(tile, slot_, first, count):
        def body(r, carry):
            start_gather(tile, slot_, first + r)
            return carry
        lax.fori_loop(0, count, body, 0, unroll=8)

    @pl.when((j == 0) & (f == 0))
    def _():
        gather_loop(0, 0, 0, tm)
        ys_ref[1] = jnp.zeros((tm, ys_ref.shape[2]), F32)

    @pl.when(f == 0)
    def _():
        pltpu.make_async_copy(v_hbm.at[pl.ds(0, tm)], xg_ref.at[slot], gsem.at[slot]).wait()
        xb_ref[...] = xg_ref[slot].astype(BF16)
        acc_ref[...] = jnp.zeros_like(acc_ref)
        for row in range(nf * per_step, tm):
            start_scatter(row)

        @pl.when(j + 1 < nt)
        def _():
            for row in range(nf * per_step, tm):
                start_gather(j + 1, 1 - slot, row)

    @pl.when(valid)
    def _():
        half = per_step // 2

        def gathers(first, last, after):
            for r in range(first, last):
                start_gather(j + 1, 1 - slot, lo + r + after)

        def scatters(first, last, after):
            for r in range(first, last):
                start_scatter(lo + r + after)

        def zero_after(val):
            bits = pltpu.bitcast(val[0:8, 0:128], jnp.int32)
            return ((bits & 1) >> 1)[0, 0]

        gathers(0, half, 0)
        x = xb_ref[...]
        a = jnp.dot(x, wg_ref[...], preferred_element_type=F32)
        gathers(half, per_step, zero_after(a))
        b = jnp.dot(x, wu_ref[...], preferred_element_type=F32)
        scatters(0, half, zero_after(b))
        act = (a * _sigmoid(a) * b).astype(BF16)
        down = jnp.dot(act, wd_ref[...], preferred_element_type=F32)
        scatters(half, per_step, zero_after(down))
        acc_ref[...] += down

    @pl.when(jnp.logical_not(valid))
    def _():
        @pl.when(j + 1 < nt)
        def _():
            gather_loop(j + 1, 1 - slot, lo, per_step)

        def body(r, carry):
            start_scatter(lo + r)
            return carry
        lax.fori_loop(0, per_step, body, 0, unroll=8)

    @pl.when(f == nf - 1)
    def _():
        @pl.when(j > 0)
        def _():
            wait_scatter(slot)
        ys_ref[slot] = acc_ref[...] * gt_ref[...]

        @pl.when(j == nt - 1)
        def _():
            wait_scatter(1 - slot)


def _combine_kernel(y0_ref, y1_ref, h_ref, nw_ref, mod_ref, o_ref):
    o_ref[...] = h_ref[...] + mod_ref[4:5, :] * _rms(y0_ref[...] + y1_ref[...], nw_ref[...])


def _count_le(sorted_vals, queries):
    return jnp.sum((sorted_vals[None, :] <= queries[:, None]).astype(jnp.int32), axis=1)


def _moe(h_lat, nw2, nw3, mod, w_router, b_router, wg, wu, wd):
    lq, d = h_lat.shape
    e = N_EXPERTS
    dff = wg.shape[2]
    tm = _pick(lq, (512, 256))
    tf = 1024
    v, route = _router(h_lat, nw2, mod, w_router, b_router)
    idx = route[:, 0:2].astype(jnp.int32)
    gates = route[:, 2:4]

    npairs = 2 * lq
    nt = npairs // tm + e + 1
    flat_e = idx.reshape(-1)
    pair_id = jnp.arange(npairs, dtype=jnp.int32)
    onehot = (flat_e[:, None] == jnp.arange(e, dtype=jnp.int32)[None, :]).astype(jnp.int32)
    csum = jnp.cumsum(onehot, axis=0)
    rank = jnp.sum(csum * onehot, axis=1) - 1
    counts = csum[-1]
    padded = ((counts + tm - 1) // tm) * tm
    ends = jnp.cumsum(padded)
    starts = ends - padded
    pos = jnp.sum(onehot * starts[None, :], axis=1) + rank
    pair = jnp.stack([pair_id, lax.bitcast_convert_type(gates.reshape(-1), jnp.int32)], axis=1)
    empty = jnp.broadcast_to(jnp.array([-1, 0], jnp.int32), (nt * tm, 2))
    slots = empty.at[pos].set(pair)
    sorted_pair = slots[:, 0]
    sorted_gate = lax.bitcast_convert_type(slots[:, 1], F32)
    is_pad = sorted_pair < 0
    src_row = jnp.where(is_pad, 0, sorted_pair >> 1).astype(jnp.int32)
    pad_rank = jnp.cumsum(is_pad.astype(jnp.int32)) - 1
    dst_row = jnp.where(is_pad, npairs + pad_rank, (sorted_pair & 1) * lq + (sorted_pair >> 1)).astype(jnp.int32)
    dst_row = jnp.concatenate([(nt - 1) * tm + jnp.arange(tm, dtype=jnp.int32), dst_row])
    n_used = ends[-1] // tm
    tile_ids = jnp.arange(nt, dtype=jnp.int32)
    tile_valid = (tile_ids < n_used).astype(jnp.int32)
    tile_exp = jnp.minimum(_count_le(ends, jnp.minimum(tile_ids, n_used - 1) * tm), e - 1)

    nf = dff // tf
    pinned = lambda j, f, tv: jnp.where(tv[j] == 1, f, nf - 1)
    y = pl.pallas_call(
        functools.partial(_expert_kernel, tm=tm, nt=nt, nf=nf),
        grid_spec=pltpu.PrefetchScalarGridSpec(
            num_scalar_prefetch=4, grid=(nt, nf),
            in_specs=[pl.BlockSpec(memory_space=pl.ANY),
                      pl.BlockSpec((tm, 1), lambda j, f, te, tv, src, dst: (j, 0)),
                      pl.BlockSpec((None, d, tf), lambda j, f, te, tv, src, dst: (te[j], 0, pinned(j, f, tv))),
                      pl.BlockSpec((None, d, tf), lambda j, f, te, tv, src, dst: (te[j], 0, pinned(j, f, tv))),
                      pl.BlockSpec((None, tf, d), lambda j, f, te, tv, src, dst: (te[j], pinned(j, f, tv), 0))],
            out_specs=pl.BlockSpec(memory_space=pl.ANY),
            scratch_shapes=[pltpu.VMEM((2, tm, d), F32), pltpu.VMEM((tm, d), BF16), pltpu.VMEM((2, tm, d), F32),
                            pltpu.VMEM((tm, d), F32), pltpu.SemaphoreType.DMA((2,)),
                            pltpu.SemaphoreType.DMA((2,))]),
        out_shape=jax.ShapeDtypeStruct((nt * tm, d), F32),
        compiler_params=pltpu.CompilerParams(dimension_semantics=("arbitrary", "arbitrary"),
                                             vmem_limit_bytes=56 * MIB, disable_bounds_checks=True),
        name="moe_experts",
    )(tile_exp, tile_valid, src_row, dst_row, v, sorted_gate.reshape(nt * tm, 1), wg, wu, wd)

    tc = _pick(lq, (256, 128))
    return pl.pallas_call(
        _combine_kernel, grid=(lq // tc,),
        in_specs=[pl.BlockSpec((tc, d), lambda i: (i, 0)),
                  pl.BlockSpec((tc, d), lambda i: (i + lq // tc, 0)),
                  pl.BlockSpec((tc, d), lambda i: (i, 0)),
                  pl.BlockSpec((1, d), lambda i: (0, 0)),
                  pl.BlockSpec((8, d), lambda i: (0, 0))],
        out_specs=pl.BlockSpec((tc, d), lambda i: (i, 0)),
        out_shape=jax.ShapeDtypeStruct((lq, d), F32),
        compiler_params=_params(("parallel",), 32),
        name="moe_combine",
    )(y, y, h_lat, nw3.reshape(1, d), mod)


def kernel(x, c, ctx, c_ctx, w_mod, b_mod, norms, e_w_in, e_pool_w, e_pool_scale, e_s5_lam_re, e_s5_lam_im, e_s5_log_dt, e_s5_b_re, e_s5_b_im, e_s5_c_re, e_s5_c_im, e_s5_d, e_s5_w_glu, e_w_out, e_ffn_gate, e_ffn_up, e_ffn_down, o_w_in, o_gla_w_a, o_gla_b_a, o_gla_norm, o_q_norm, o_k_norm, o_w_out, o_router, o_router_b, o_moe_gate, o_moe_up, o_moe_down):
    assert x.shape[0] == 1 and w_mod.shape[0] == 2 and e_w_in.shape[0] == 1 and o_w_in.shape[0] == 1
    d = x.shape[2]
    n_ctx = ctx.shape[1]
    h = jnp.concatenate([ctx[0], x[0]], axis=0)

    vecs = jnp.zeros((8, d), F32).at[0].set(c[0]).at[1].set(c_ctx)
    mods = _modulation(vecs, w_mod, b_mod)

    mod1, mod2 = _mod_rows(mods[0], 0), _mod_rows(mods[0], 1)
    z = _normed_matmul(h, norms[0, 0], mod1, e_w_in[0].astype(BF16), n_ctx)
    pool_out = _pool_mixer(z, e_pool_w[0].astype(BF16), e_pool_scale[0], n_ctx)
    m_op, qre, qim, pre, pim, at = _s5_params(e_s5_lam_re[0], e_s5_lam_im[0], e_s5_log_dt[0],
                                              e_s5_b_re[0], e_s5_b_im[0], e_s5_c_re[0], e_s5_c_im[0])
    y_s5 = _s5_mixer(z[:, W_POOL:], m_op, qre, qim, pre, pim, at, n_ctx)
    h = _even_out(pool_out, y_s5, z, e_s5_d[0], e_s5_w_glu[0].astype(BF16), e_w_out[0].astype(BF16),
                  h, norms[0, 1], mod1, n_ctx)
    h = _dense_ffn(h, norms[0, 2], norms[0, 3], mod2, e_ffn_gate[0].astype(BF16),
                   e_ffn_up[0].astype(BF16), e_ffn_down[0].astype(BF16), n_ctx)

    mod1, mod2 = _mod_rows(mods[1], 0), _mod_rows(mods[1], 1)
    w_in = o_w_in[0]
    a0 = 2 * W_GLA_K + W_GLA_V
    w_main = jnp.concatenate([w_in[:, :a0], w_in[:, a0 + GLA_RANK:]], axis=1).astype(BF16)
    w_aux = jnp.pad(w_in[:, a0:a0 + GLA_RANK], ((0, 0), (0, 128 - GLA_RANK))).astype(BF16)
    zmain, a_aux = _normed_matmul(h, norms[1, 0], mod1, w_main, n_ctx, w_aux=w_aux)
    o_f, o_b = _gla_mixer(zmain, a_aux, o_gla_w_a[0], o_gla_b_a[0], n_ctx)
    qx, kx, vx, bound_log2 = _qk_prep(zmain, o_q_norm[0], o_k_norm[0], n_ctx)
    att = _attention(qx, kx, vx, bound_log2)
    h_lat = _odd_out(o_f, o_b, zmain, att, o_gla_norm[0], o_w_out[0].astype(BF16), h, norms[1, 1], mod1, n_ctx)
    out = _moe(h_lat, norms[1, 2], norms[1, 3], mod2, o_router[0], o_router_b[0],
               o_moe_gate[0].astype(BF16), o_moe_up[0].astype(BF16), o_moe_down[0].astype(BF16))
    return out[None]
```

```python
import functools
import math

import jax
import jax.numpy as jnp
from jax import lax
from jax.experimental import pallas as pl
from jax.experimental.pallas import tpu as pltpu

F32 = jnp.float32
BF16 = jnp.bfloat16
HI = lax.Precision.HIGHEST
EPS = 1e-6

D_MODEL = 2048
GRID_W = 64
N_MOD = 6

POOL_WINDOWS = (2, 4, 8, 16)
POOL_GROUP = 384
W_POOL = 1536
W_S5 = 512
S5_CH = 16
S5_STATE = 64
S5_GROUPS = 32
S5_T = 32

GLA_HEADS = 4
GLA_DK = 128
GLA_DV = 256
GLA_RANK = 16
GLA_TAU = 16.0
GLA_CHUNK = 64
W_GLA_K = 512
W_GLA_V = 1024
ATT_HEAD_DIM = 128
ATT_HEADS = 8
ATT_KV_HEADS = 2
ATT_GROUP = 4
W_ATT = 1024
W_ATT_KV = 256
ROPE_THETA = 10000.0

D_FF = 7168
N_EXPERTS = 8

OQ, OK_, OV, OR, OAQ, OAK, OAV = 0, 512, 1024, 2048, 3072, 4096, 4352
W_ODD_MAIN = 4608

MIB = 2 ** 20


def _params(sem, vmem_mib):
    return pltpu.CompilerParams(dimension_semantics=sem, vmem_limit_bytes=vmem_mib * MIB)


def _pick(n, cands):
    for c in cands:
        if n % c == 0:
            return c
    raise ValueError(f"no tile for {n} in {cands}")


def _sigmoid(x):
    return 1.0 / (1.0 + jnp.exp(-x))


def _rms(x, w):
    return x * lax.rsqrt(jnp.mean(x * x, axis=-1, keepdims=True) + EPS) * w


_NORM_CHUNK = 16


def _store_norm_mod(dst_ref, h_ref, nw_ref, mod_ref, row0, n_ctx):
    tm = h_ref.shape[0]
    nw = nw_ref[...]
    w_lat = nw * (1.0 + mod_ref[0:1, :])
    sh_lat = mod_ref[1:2, :]

    def rows(c):
        r0 = pl.multiple_of(c * _NORM_CHUNK, _NORM_CHUNK)
        x = h_ref[pl.ds(r0, _NORM_CHUNK), :]
        return r0, x * lax.rsqrt(jnp.mean(x * x, axis=-1, keepdims=True) + EPS)

    @pl.when(row0 >= n_ctx)
    def _():
        def body(c, carry):
            r0, xr = rows(c)
            dst_ref[pl.ds(r0, _NORM_CHUNK), :] = (xr * w_lat + sh_lat).astype(dst_ref.dtype)
            return carry
        lax.fori_loop(0, tm // _NORM_CHUNK, body, 0, unroll=4)

    @pl.when(row0 < n_ctx)
    def _():
        w_ctx = nw * (1.0 + mod_ref[2:3, :])
        sh_ctx = mod_ref[3:4, :]

        def body(c, carry):
            r0, xr = rows(c)
            is_ctx = row0 + r0 + lax.broadcasted_iota(jnp.int32, (_NORM_CHUNK, 1), 0) < n_ctx
            u = xr * jnp.where(is_ctx, w_ctx, w_lat) + jnp.where(is_ctx, sh_ctx, sh_lat)
            dst_ref[pl.ds(r0, _NORM_CHUNK), :] = u.astype(dst_ref.dtype)
            return carry
        lax.fori_loop(0, tm // _NORM_CHUNK, body, 0, unroll=4)


def _add_gated_norm(o_ref, h_ref, nw_ref, mod_ref, row0, n_ctx):
    tm = o_ref.shape[0]
    nw = nw_ref[...]
    w_lat = nw * mod_ref[4:5, :]

    def rows(c):
        r0 = pl.multiple_of(c * _NORM_CHUNK, _NORM_CHUNK)
        x = o_ref[pl.ds(r0, _NORM_CHUNK), :]
        return r0, x * lax.rsqrt(jnp.mean(x * x, axis=-1, keepdims=True) + EPS)

    def latent_tile():
        def body(c, carry):
            r0, xr = rows(c)
            o_ref[pl.ds(r0, _NORM_CHUNK), :] = h_ref[pl.ds(r0, _NORM_CHUNK), :] + xr * w_lat
            return carry
        lax.fori_loop(0, tm // _NORM_CHUNK, body, 0, unroll=4)

    if n_ctx == 0:
        latent_tile()
        return
    pl.when(row0 >= n_ctx)(latent_tile)

    @pl.when(row0 < n_ctx)
    def _():
        w_ctx = nw * mod_ref[5:6, :]

        def body(c, carry):
            r0, xr = rows(c)
            is_ctx = row0 + r0 + lax.broadcasted_iota(jnp.int32, (_NORM_CHUNK, 1), 0) < n_ctx
            o_ref[pl.ds(r0, _NORM_CHUNK), :] = (h_ref[pl.ds(r0, _NORM_CHUNK), :]
                                                 + xr * jnp.where(is_ctx, w_ctx, w_lat))
            return carry
        lax.fori_loop(0, tm // _NORM_CHUNK, body, 0, unroll=4)


def _mod_kernel(v_ref, w_ref, b_ref, o_ref):
    v = v_ref[...]
    s = v * _sigmoid(v)
    o_ref[...] = jnp.dot(s, w_ref[...], precision=HI, preferred_element_type=F32) + b_ref[...]


def _modulation(vecs, w_mod, b_mod):
    depth, d, n6 = w_mod.shape
    tn = 1024
    return pl.pallas_call(
        _mod_kernel,
        grid=(depth, n6 // tn),
        in_specs=[pl.BlockSpec((8, d), lambda l, j: (0, 0)),
                  pl.BlockSpec((None, d, tn), lambda l, j: (l, 0, j)),
                  pl.BlockSpec((None, 1, tn), lambda l, j: (l, 0, j))],
        out_specs=pl.BlockSpec((None, 8, tn), lambda l, j: (l, 0, j)),
        out_shape=jax.ShapeDtypeStruct((depth, 8, n6), F32),
        compiler_params=_params(("parallel", "parallel"), 40),
        name="modulation",
    )(vecs, w_mod, b_mod.reshape(depth, 1, n6))


def _mod_rows(m, sub):
    d = m.shape[1] // N_MOD
    m6 = m.reshape(8, N_MOD, d)
    sh, sc, g = m6[:, 3 * sub + 0], m6[:, 3 * sub + 1], m6[:, 3 * sub + 2]
    z = jnp.zeros((d,), F32)
    return jnp.stack([sc[0], sh[0], sc[1], sh[1], g[0], g[1], z, z])


def _nmm_kernel(h_ref, nw_ref, mod_ref, w_ref, o_ref, u_ref, *, n_ctx, tm):
    @pl.when(pl.program_id(1) == 0)
    def _():
        _store_norm_mod(u_ref, h_ref, nw_ref, mod_ref, pl.program_id(0) * tm, n_ctx)
    o_ref[...] = jnp.dot(u_ref[...], w_ref[...], preferred_element_type=F32).astype(o_ref.dtype)


def _nmm_aux_kernel(h_ref, nw_ref, mod_ref, w_ref, wa_ref, o_ref, oa_ref, u_ref, *, n_ctx, tm):
    @pl.when(pl.program_id(1) == 0)
    def _():
        _store_norm_mod(u_ref, h_ref, nw_ref, mod_ref, pl.program_id(0) * tm, n_ctx)
        oa_ref[...] = jnp.dot(u_ref[...], wa_ref[...], preferred_element_type=F32)
    o_ref[...] = jnp.dot(u_ref[...], w_ref[...], preferred_element_type=F32).astype(o_ref.dtype)


def _normed_matmul(h, nw, mod, w, n_ctx, w_aux=None):
    n, d = h.shape
    nout = w.shape[1]
    tm = _pick(n, (640, 512, 256, 128))
    tn = _pick(nout, (1536, 1024, 512, 256, 128))
    in_specs = [pl.BlockSpec((tm, d), lambda i, j: (i, 0)),
                pl.BlockSpec((1, d), lambda i, j: (0, 0)),
                pl.BlockSpec((8, d), lambda i, j: (0, 0)),
                pl.BlockSpec((d, tn), lambda i, j: (0, j))]
    out_specs = pl.BlockSpec((tm, tn), lambda i, j: (i, j))
    out_shape = jax.ShapeDtypeStruct((n, nout), BF16)
    args = [h, nw.reshape(1, d), mod, w]
    if w_aux is None:
        kern = functools.partial(_nmm_kernel, n_ctx=n_ctx, tm=tm)
    else:
        na = w_aux.shape[1]
        kern = functools.partial(_nmm_aux_kernel, n_ctx=n_ctx, tm=tm)
        in_specs.append(pl.BlockSpec((d, na), lambda i, j: (0, 0)))
        out_specs = [out_specs, pl.BlockSpec((tm, na), lambda i, j: (i, 0))]
        out_shape = [out_shape, jax.ShapeDtypeStruct((n, na), F32)]
        args.append(w_aux)
    return pl.pallas_call(
        kern, grid=(n // tm, nout // tn), in_specs=in_specs, out_specs=out_specs, out_shape=out_shape,
        scratch_shapes=[pltpu.VMEM((tm, d), BF16)],
        compiler_params=_params(("parallel", "arbitrary"), 48),
        name="normed_proj",
    )(*args)


_POOL_HALO = 16


def _pool_kernel(z_ref, zp_ref, zn_ref, pw_ref, ps_ref, o_ref, ext_ref, *, n_ctx, n_all, tm):
    i = pl.program_id(0)
    row0 = i * tm
    in_ctx = row0 < n_ctx
    seq_start = jnp.where(in_ctx, 0, n_ctx)
    seq_end = jnp.where(in_ctx, n_ctx, n_all)
    has_prev = row0 > seq_start
    has_next = row0 + tm < seq_end
    hl = _POOL_HALO
    ext_ref[0:hl, :] = jnp.where(has_prev, zp_ref[...].astype(F32), 0.0)
    ext_ref[hl:hl + tm, :] = z_ref[...].astype(F32)
    ext_ref[hl + tm:hl + tm + hl, :] = jnp.where(has_next, zn_ref[...].astype(F32), 0.0)
    t = row0 - seq_start + lax.broadcasted_iota(jnp.int32, (tm, 1), 0)
    seq_len = seq_end - seq_start
    for g, w in enumerate(POOL_WINDOWS):
        c0 = g * POOL_GROUP
        acc = None
        for k in range(-(w // 2), w - w // 2):
            v = ext_ref[hl + k:hl + k + tm, c0:c0 + POOL_GROUP]
            acc = v if acc is None else acc + v
        lo = jnp.maximum(t - w // 2, 0)
        hi = jnp.minimum(t + (w - w // 2), seq_len)
        cnt = (hi - lo).astype(F32)
        dlt = acc / cnt - ext_ref[hl:hl + tm, c0:c0 + POOL_GROUP]
        y = jnp.dot(dlt.astype(BF16), pw_ref[g], preferred_element_type=F32)
        o_ref[:, c0:c0 + POOL_GROUP] = (y * ps_ref[:, c0:c0 + POOL_GROUP]).astype(o_ref.dtype)


def _pool_mixer(z, pool_w, pool_scale, n_ctx):
    n = z.shape[0]
    tm = 256
    assert n_ctx % tm == 0 and n % tm == 0
    hl = _POOL_HALO
    nh = n // hl
    per = tm // hl
    kern = functools.partial(_pool_kernel, n_ctx=n_ctx, n_all=n, tm=tm)
    return pl.pallas_call(
        kern, grid=(n // tm,),
        in_specs=[pl.BlockSpec((tm, W_POOL), lambda i: (i, 0)),
                  pl.BlockSpec((hl, W_POOL), lambda i: (jnp.maximum(i * per - 1, 0), 0)),
                  pl.BlockSpec((hl, W_POOL), lambda i: (jnp.minimum((i + 1) * per, nh - 1), 0)),
                  pl.BlockSpec((4, POOL_GROUP, POOL_GROUP), lambda i: (0, 0, 0)),
                  pl.BlockSpec((1, W_POOL), lambda i: (0, 0))],
        out_specs=pl.BlockSpec((tm, W_POOL), lambda i: (i, 0)),
        out_shape=jax.ShapeDtypeStruct((n, W_POOL), BF16),
        scratch_shapes=[pltpu.VMEM((tm + 2 * hl, W_POOL), F32)],
        compiler_params=_params(("parallel",), 32),
        name="pool_mixer",
    )(z, z, z, pool_w, pool_scale.reshape(1, W_POOL))


def _dot_hi(a, b):
    return jnp.dot(a, b, precision=HI, preferred_element_type=F32)


def _dot_t_hi(a, b):
    return lax.dot_general(a, b, (((0,), (0,)), ((), ())), precision=HI, preferred_element_type=F32)


def _s5_param_kernel(lr_ref, li_ref, lrc_ref, lic_ref, ldt_ref, br_ref, bi_ref, cr_ref, ci_ref,
                     m_ref, qre_ref, qim_ref, pre_ref, pim_ref, at_ref):
    t_len = S5_T
    tp = t_len + 8
    wid = t_len * S5_CH
    col = lax.broadcasted_iota(jnp.int32, (tp, wid), 1)
    kid = lax.broadcasted_iota(jnp.int32, (tp, wid), 0)
    cq = col >> 4
    rep_nat = (kid == cq).astype(F32)
    rep_rev = (kid == t_len - 1 - cq).astype(F32)
    til = (lax.broadcasted_iota(jnp.int32, (S5_CH, wid), 0)
           == (lax.broadcasted_iota(jnp.int32, (S5_CH, wid), 1) & (S5_CH - 1))).astype(F32)
    colb = lax.broadcasted_iota(jnp.int32, (S5_CH, wid), 1) >> 4
    kk = lax.broadcasted_iota(jnp.int32, (tp, S5_STATE), 0).astype(F32)
    rows = [None] * t_len
    for d in range(2):
        dt = jnp.exp(ldt_ref[d])
        lr, li = lr_ref[d], li_ref[d]
        mag = jnp.exp(kk * (lr * dt))
        ang = kk * (li * dt)
        pr, pi_ = mag * jnp.cos(ang), mag * jnp.sin(ang)
        lrc, lic = lrc_ref[d], lic_ref[d]
        magc = jnp.exp(lrc * dt)
        arc, aic = magc * jnp.cos(lic * dt), magc * jnp.sin(lic * dt)
        den = lrc * lrc + lic * lic
        nr, ni = arc - 1.0, aic
        kre = (nr * lrc + ni * lic) / den
        kim = (ni * lrc - nr * lic) / den
        bbr = kre * br_ref[d] - kim * bi_ref[d]
        bbi = kre * bi_ref[d] + kim * br_ref[d]
        repq = rep_rev if d == 0 else rep_nat
        repp = ((kid == cq + 1) if d == 0 else (kid == t_len - cq)).astype(F32)
        repm = rep_nat if d == 0 else rep_rev
        bt_r, bt_i = _dot_hi(bbr, til), _dot_hi(bbi, til)
        eqr, eqi = _dot_t_hi(pr, repq), _dot_t_hi(pi_, repq)
        qre_ref[d] = (eqr * bt_r - eqi * bt_i).astype(qre_ref.dtype)
        qim_ref[d] = (eqr * bt_i + eqi * bt_r).astype(qim_ref.dtype)
        epr, epi = _dot_t_hi(pr, repp), _dot_t_hi(pi_, repp)
        ct_r, ct_i = _dot_t_hi(cr_ref[d], til), _dot_t_hi(ci_ref[d], til)
        pre_ref[d] = (ct_r * epr - ct_i * epi).astype(pre_ref.dtype)
        pim_ref[d] = (-(ct_r * epi + ct_i * epr)).astype(pim_ref.dtype)
        at_ref[d, 0:1, :] = pr[t_len:t_len + 1, :]
        at_ref[d, 1:2, :] = pi_[t_len:t_len + 1, :]
        emr, emi = _dot_t_hi(pr, repm), _dot_t_hi(pi_, repm)
        y_re = emr * ct_r - emi * ct_i
        y_im = emr * ct_i + emi * ct_r
        r = _dot_t_hi(bbr, y_re) - _dot_t_hi(bbi, y_im)
        for s in range(t_len):
            if d == 0:
                blk = jnp.where(colb >= s, pltpu.roll(r, S5_CH * s, 1), 0.0)
            else:
                blk = jnp.where(colb <= s, pltpu.roll(r, (S5_CH * (s + 1)) % wid, 1), 0.0)
            rows[s] = blk if rows[s] is None else rows[s] + blk
    for s in range(t_len):
        m_ref[s * S5_CH:(s + 1) * S5_CH, :] = rows[s].astype(m_ref.dtype)


def _s5_params(lam_re, lam_im, log_dt, b_re, b_im, c_re, c_im):
    g, p, n = S5_GROUPS, S5_STATE, S5_CH
    wid = S5_T * S5_CH

    def spec(*shape):
        return pl.BlockSpec((2, None) + shape, lambda gi: (0, gi) + (0,) * len(shape))

    def ospec(*shape):
        return pl.BlockSpec((None,) + shape, lambda gi: (gi,) + (0,) * len(shape))
    return pl.pallas_call(
        _s5_param_kernel, grid=(g,),
        in_specs=[spec(1, p), spec(1, p), spec(p, 1), spec(p, 1), spec(1, 1),
                  spec(p, n), spec(p, n), spec(n, p), spec(n, p)],
        out_specs=[ospec(wid, wid), ospec(2, p, wid), ospec(2, p, wid), ospec(2, p, wid), ospec(2, p, wid),
                   ospec(2, 2, p)],
        out_shape=[jax.ShapeDtypeStruct((g, wid, wid), BF16),
                   jax.ShapeDtypeStruct((g, 2, p, wid), BF16),
                   jax.ShapeDtypeStruct((g, 2, p, wid), BF16),
                   jax.ShapeDtypeStruct((g, 2, p, wid), BF16),
                   jax.ShapeDtypeStruct((g, 2, p, wid), BF16),
                   jax.ShapeDtypeStruct((g, 2, 2, p), F32)],
        compiler_params=_params(("parallel",), 32),
        name="s5_params",
    )(lam_re.reshape(2, g, 1, p), lam_im.reshape(2, g, 1, p), lam_re.reshape(2, g, p, 1),
      lam_im.reshape(2, g, p, 1), log_dt.reshape(2, g, 1, 1), b_re, b_im, c_re, c_im)


def _s5_state_kernel(u_ref, qre_ref, qim_ref, s_ref):
    u = u_ref[...]
    dn = (((1,), (1,)), ((), ()))
    s_ref[:, 0:128] = lax.dot_general(u, qre_ref[...], dn, preferred_element_type=F32)
    s_ref[:, 128:256] = lax.dot_general(u, qim_ref[...], dn, preferred_element_type=F32)


def _s5_scan_kernel(s_ref, are_ref, aim_ref, h_ref, *, nc, ncc):
    ar, ai = are_ref[...], aim_ref[...]
    is_f = lax.broadcasted_iota(jnp.int32, ar.shape, 1) < S5_STATE

    def body(i, carry):
        hr, hi = carry
        cf = i
        cb = jnp.where(i < ncc, ncc - 1 - i, nc + ncc - 1 - i)
        h_ref[cf, :, 0:64] = hr[:, 0:64]
        h_ref[cb, :, 64:128] = hr[:, 64:128]
        h_ref[cf, :, 128:192] = hi[:, 0:64]
        h_ref[cb, :, 192:256] = hi[:, 64:128]
        sf, sb = s_ref[cf], s_ref[cb]
        sr = jnp.where(is_f, sf[:, 0:128], sb[:, 0:128])
        si = jnp.where(is_f, sf[:, 128:256], sb[:, 128:256])
        return ar * hr - ai * hi + sr, ar * hi + ai * hr + si

    zero = jnp.zeros(ar.shape, F32)
    lax.fori_loop(0, nc, body, (zero, zero))


def _s5_out_kernel(u_ref, m_ref, h_ref, p_ref, y_ref):
    y = jnp.dot(u_ref[...], m_ref[...], preferred_element_type=F32)
    y_ref[...] = y + jnp.dot(h_ref[...].astype(BF16), p_ref[...], preferred_element_type=F32)


def _s5_mixer(s_all, m, qre, qim, pre, pim, at, n_ctx):
    n = s_all.shape[0]
    g, t_len, wid = S5_GROUPS, S5_T, S5_T * S5_CH
    nc, ncc = n // t_len, n_ctx // t_len
    assert n % t_len == 0 and n_ctx % t_len == 0
    u = s_all.reshape(nc, t_len, g, S5_CH).transpose(2, 0, 1, 3).reshape(g, nc, wid)
    p_all = jnp.concatenate([pre.reshape(g, 128, wid), pim.reshape(g, 128, wid)], axis=1)
    a_re = at[:, :, 0, :].reshape(g, 128)
    a_im = at[:, :, 1, :].reshape(g, 128)
    s_t = pl.pallas_call(
        _s5_state_kernel, grid=(g,),
        in_specs=[pl.BlockSpec((None, nc, wid), lambda i: (i, 0, 0)),
                  pl.BlockSpec((None, 128, wid), lambda i: (i, 0, 0)),
                  pl.BlockSpec((None, 128, wid), lambda i: (i, 0, 0))],
        out_specs=pl.BlockSpec((nc, 256), lambda i: (0, i)),
        out_shape=jax.ShapeDtypeStruct((nc, g * 256), F32),
        compiler_params=_params(("parallel",), 32),
        name="s5_chunk_state",
    )(u, qre.reshape(g, 128, wid), qim.reshape(g, 128, wid))
    h_t = pl.pallas_call(
        functools.partial(_s5_scan_kernel, nc=nc, ncc=ncc),
        out_shape=jax.ShapeDtypeStruct((nc, g, 256), F32),
        compiler_params=pltpu.CompilerParams(vmem_limit_bytes=48 * MIB),
        name="s5_chunk_scan",
    )(s_t.reshape(nc, g, 256), a_re, a_im).reshape(nc, g * 256)
    y = pl.pallas_call(
        _s5_out_kernel, grid=(g,),
        in_specs=[pl.BlockSpec((None, nc, wid), lambda i: (i, 0, 0)),
                  pl.BlockSpec((None, wid, wid), lambda i: (i, 0, 0)),
                  pl.BlockSpec((nc, 256), lambda i: (0, i)),
                  pl.BlockSpec((None, 256, wid), lambda i: (i, 0, 0))],
        out_specs=pl.BlockSpec((None, nc, wid), lambda i: (i, 0, 0)),
        out_shape=jax.ShapeDtypeStruct((g, nc, wid), F32),
        compiler_params=_params(("parallel",), 32),
        name="s5_chunk_out",
    )(u, m, h_t, p_all)
    return y.reshape(g, nc, t_len, S5_CH).transpose(1, 2, 0, 3).reshape(n, W_S5)


def _gelu_tanh(x):
    return 0.5 * x * (1.0 + jnp.tanh(math.sqrt(2.0 / math.pi) * (x + 0.044715 * (x * x * x))))


def _even_out_kernel(pool_ref, y_ref, s_ref, dsk_ref, wglu_ref, wo_ref, h_ref, nw_ref, mod_ref, o_ref,
                     *, n_ctx, tm):
    y = _gelu_tanh(y_ref[...] + s_ref[...].astype(F32) * dsk_ref[...])
    gate = jnp.dot(y.astype(BF16), wglu_ref[...], preferred_element_type=F32)
    s5 = (y * _sigmoid(gate)).astype(BF16)
    mix = jnp.dot(pool_ref[...], wo_ref[0:W_POOL, :], preferred_element_type=F32)
    mix = mix + jnp.dot(s5, wo_ref[W_POOL:W_POOL + W_S5, :], preferred_element_type=F32)
    o_ref[...] = mix
    _add_gated_norm(o_ref, h_ref, nw_ref, mod_ref, pl.program_id(0) * tm, n_ctx)


def _even_out(pool_out, y_s5, z, dsk, w_glu, w_out, h, nw, mod, n_ctx):
    n, d = h.shape
    tm = _pick(n, (320, 256, 128))
    return pl.pallas_call(
        functools.partial(_even_out_kernel, n_ctx=n_ctx, tm=tm), grid=(n // tm,),
        in_specs=[pl.BlockSpec((tm, W_POOL), lambda i: (i, 0)),
                  pl.BlockSpec((tm, W_S5), lambda i: (i, 0)),
                  pl.BlockSpec((tm, W_S5), lambda i: (i, W_POOL // W_S5)),
                  pl.BlockSpec((1, W_S5), lambda i: (0, 0)),
                  pl.BlockSpec((W_S5, W_S5), lambda i: (0, 0)),
                  pl.BlockSpec((d, d), lambda i: (0, 0)),
                  pl.BlockSpec((tm, d), lambda i: (i, 0)),
                  pl.BlockSpec((1, d), lambda i: (0, 0)),
                  pl.BlockSpec((8, d), lambda i: (0, 0))],
        out_specs=pl.BlockSpec((tm, d), lambda i: (i, 0)),
        out_shape=jax.ShapeDtypeStruct((n, d), F32),
        compiler_params=_params(("parallel",), 48),
        name="even_mixer_out",
    )(pool_out, y_s5, z, dsk.reshape(1, W_S5), w_glu, w_out, h, nw.reshape(1, d), mod)


def _ffn_kernel(h_ref, nw2_ref, nw3_ref, mod_ref, wg_ref, wu_ref, wd_ref, o_ref, v_ref, *, n_ctx, tm):
    f = pl.program_id(1)

    @pl.when(f == 0)
    def _():
        _store_norm_mod(v_ref, h_ref, nw2_ref, mod_ref, pl.program_id(0) * tm, n_ctx)
        o_ref[...] = jnp.zeros_like(o_ref)
    v = v_ref[...]
    a = jnp.dot(v, wg_ref[...], preferred_element_type=F32)
    b = jnp.dot(v, wu_ref[...], preferred_element_type=F32)
    act = (a * _sigmoid(a) * b).astype(BF16)
    o_ref[...] += jnp.dot(act, wd_ref[...], preferred_element_type=F32)

    @pl.when(f == pl.num_programs(1) - 1)
    def _():
        _add_gated_norm(o_ref, h_ref, nw3_ref, mod_ref, pl.program_id(0) * tm, n_ctx)


def _dense_ffn(h, nw2, nw3, mod, wg, wu, wd, n_ctx):
    n, d = h.shape
    dff = wg.shape[1]
    tm = _pick(n, (640, 512, 256, 128))
    tf = 512
    return pl.pallas_call(
        functools.partial(_ffn_kernel, n_ctx=n_ctx, tm=tm), grid=(n // tm, dff // tf),
        in_specs=[pl.BlockSpec((tm, d), lambda i, f: (i, 0)),
                  pl.BlockSpec((1, d), lambda i, f: (0, 0)),
                  pl.BlockSpec((1, d), lambda i, f: (0, 0)),
                  pl.BlockSpec((8, d), lambda i, f: (0, 0)),
                  pl.BlockSpec((d, tf), lambda i, f: (0, f)),
                  pl.BlockSpec((d, tf), lambda i, f: (0, f)),
                  pl.BlockSpec((tf, d), lambda i, f: (f, 0))],
        out_specs=pl.BlockSpec((tm, d), lambda i, f: (i, 0)),
        out_shape=jax.ShapeDtypeStruct((n, d), F32),
        scratch_shapes=[pltpu.VMEM((tm, d), BF16)],
        compiler_params=_params(("parallel", "arbitrary"), 56),
        name="dense_swiglu",
    )(h, nw2.reshape(1, d), nw3.reshape(1, d), mod, wg, wu, wd)


def _split_bf16(x):
    hi = x.astype(BF16)
    return hi, (x - hi.astype(F32)).astype(BF16)


def _gla_kernel(qf_ref, kf_ref, vf_ref, af_ref, qb_ref, kb_ref, vb_ref, ab_ref, wa_ref, ba_ref,
                of_ref, ob_ref, st_ref, *, tb):
    c = GLA_CHUNK
    nsub = tb // c

    @pl.when(pl.program_id(0) == 0)
    def _():
        st_ref[...] = jnp.zeros_like(st_ref)
    ri = lax.broadcasted_iota(jnp.int32, (c, c), 0)
    ci = lax.broadcasted_iota(jnp.int32, (c, c), 1)
    scale = GLA_DK ** -0.5
    nt = (((1,), (1,)), ((), ()))
    tn = (((0,), (0,)), ((), ()))
    dirs = ((qf_ref, kf_ref, vf_ref, af_ref, of_ref), (qb_ref, kb_ref, vb_ref, ab_ref, ob_ref))
    for d, (q_ref, k_ref, v_ref, a_ref, o_ref) in enumerate(dirs):
        mask = (ri >= ci) if d == 0 else (ci >= ri)
        cum = mask.astype(BF16)
        a_hi, a_lo = _split_bf16(a_ref[...])
        w_hi, w_lo = _split_bf16(wa_ref[d])
        z = (jnp.dot(a_hi, w_hi, preferred_element_type=F32) + jnp.dot(a_hi, w_lo, preferred_element_type=F32)
             + jnp.dot(a_lo, w_hi, preferred_element_type=F32)) + ba_ref[d]
        glog = (jnp.minimum(z, 0.0) - jnp.log(1.0 + jnp.exp(-jnp.abs(z)))) * (1.0 / GLA_TAU)
        order = range(nsub) if d == 0 else range(nsub - 1, -1, -1)
        for sc in order:
            r0 = sc * c
            g_hi, g_lo = _split_bf16(glog[r0:r0 + c, :])
            b = jnp.dot(cum, g_hi, preferred_element_type=F32) + jnp.dot(cum, g_lo, preferred_element_type=F32)
            b_last = b[c - 1:c, :] if d == 0 else b[0:1, :]
            dec = jnp.exp(b_last)
            q = q_ref[r0:r0 + c, :].astype(F32) * scale
            k = k_ref[r0:r0 + c, :].astype(F32)
            qe = (q * jnp.exp(b)).astype(BF16)
            ke = (k * jnp.exp(-b)).astype(BF16)
            kd = (k * jnp.exp(b_last - b)).astype(BF16)
            for h in range(GLA_HEADS):
                ks = slice(h * GLA_DK, (h + 1) * GLA_DK)
                vs = slice(h * GLA_DV, (h + 1) * GLA_DV)
                att = lax.dot_general(qe[:, ks], ke[:, ks], nt, preferred_element_type=F32)
                att = jnp.where(mask, att, 0.0).astype(BF16)
                vh = v_ref[r0:r0 + c, vs]
                s_old = st_ref[d, h]
                o = jnp.dot(att, vh, preferred_element_type=F32)
                o = o + lax.dot_general(qe[:, ks], s_old.astype(BF16), nt, preferred_element_type=F32)
                o_ref[r0:r0 + c, vs] = o
                upd = lax.dot_general(vh, kd[:, ks], tn, preferred_element_type=F32)
                st_ref[d, h] = s_old * dec[:, ks] + upd


def _gla_mixer(zmain, a_aux, w_a, b_a, n_ctx):
    n = zmain.shape[0]
    tb = 256
    assert n % tb == 0 and n_ctx % tb == 0
    nb, ncb = n // tb, n_ctx // tb

    def bwd(s):
        return jnp.where(s < ncb, ncb - 1 - s, nb + ncb - 1 - s)
    wa = jnp.pad(w_a, ((0, 0), (0, 128 - GLA_RANK), (0, 0)))
    in_specs = []
    for order in (lambda s: s, bwd):
        in_specs += [pl.BlockSpec((tb, W_GLA_K), lambda s, o=order: (o(s), OQ // W_GLA_K)),
                     pl.BlockSpec((tb, W_GLA_K), lambda s, o=order: (o(s), OK_ // W_GLA_K)),
                     pl.BlockSpec((tb, W_GLA_V), lambda s, o=order: (o(s), OV // W_GLA_V)),
                     pl.BlockSpec((tb, 128), lambda s, o=order: (o(s), 0))]
    in_specs += [pl.BlockSpec((2, 128, W_GLA_K), lambda s: (0, 0, 0)),
                 pl.BlockSpec((2, 1, W_GLA_K), lambda s: (0, 0, 0))]
    return pl.pallas_call(
        functools.partial(_gla_kernel, tb=tb), grid=(nb,),
        in_specs=in_specs,
        out_specs=[pl.BlockSpec((tb, W_GLA_V), lambda s: (s, 0)),
                   pl.BlockSpec((tb, W_GLA_V), lambda s: (bwd(s), 0))],
        out_shape=[jax.ShapeDtypeStruct((n, W_GLA_V), F32)] * 2,
        scratch_shapes=[pltpu.VMEM((2, GLA_HEADS, GLA_DV, GLA_DK), F32)],
        compiler_params=_params(("arbitrary",), 32),
        name="gla_chunked",
    )(zmain, zmain, zmain, a_aux, zmain, zmain, zmain, a_aux, wa, b_a.reshape(2, 1, W_GLA_K))


_ATT_PAD = 2 * ATT_HEAD_DIM
_LOG2E = 1.4426950408889634
_FIXED_SHIFT_MAX_BOUND = 40.0


def _rope_fn(row0, tm, n_ctx):
    hd = ATT_HEAD_DIM
    row = row0 + lax.broadcasted_iota(jnp.int32, (tm, hd), 0)
    lane = lax.broadcasted_iota(jnp.int32, (tm, hd), 1)
    t = row - n_ctx
    pos = jnp.where(lane < hd // 2, t // GRID_W, t % GRID_W).astype(F32)
    quarter = hd // 4
    freq = jnp.exp((lane % quarter).astype(F32) * (-math.log(ROPE_THETA) / quarter))
    ang = pos * freq
    first = (lane & quarter) == 0
    cosv = jnp.cos(ang)
    sin_s = jnp.where(first, -jnp.sin(ang), jnp.sin(ang))

    def rope(x):
        sw = jnp.where(first, pltpu.roll(x, hd - quarter, 1), pltpu.roll(x, quarter, 1))
        return x * cosv + sw * sin_s
    return rope, row >= n_ctx


def _k_prep_kernel(k_ref, v_ref, kw_ref, kx_ref, vx_ref, kmax_ref, *, n_ctx, tm):
    hd = ATT_HEAD_DIM
    rope, is_lat = _rope_fn(pl.program_id(0) * tm, tm, n_ctx)
    lane = lax.broadcasted_iota(jnp.int32, (tm, hd), 1)
    one_col = jnp.where(lane == 0, 1.0, 0.0).astype(BF16)
    ones = jnp.ones((tm, hd), BF16)
    nmax = jnp.zeros((tm, 1), F32)
    for h in range(ATT_KV_HEADS):
        sl = slice(h * hd, (h + 1) * hd)
        xn = _rms(k_ref[:, sl].astype(F32), kw_ref[...])
        kr = jnp.where(is_lat, rope(xn), xn).astype(BF16)
        kx_ref[:, h * _ATT_PAD:h * _ATT_PAD + hd] = kr
        kx_ref[:, h * _ATT_PAD + hd:(h + 1) * _ATT_PAD] = one_col
        vx_ref[:, h * _ATT_PAD:h * _ATT_PAD + hd] = v_ref[:, sl]
        vx_ref[:, h * _ATT_PAD + hd:(h + 1) * _ATT_PAD] = ones
        krf = kr.astype(F32)
        nmax = jnp.maximum(nmax, jnp.sum(krf * krf, axis=-1, keepdims=True))
    kmax_ref[...] = jnp.broadcast_to(jnp.max(nmax, axis=0, keepdims=True), kmax_ref.shape)


def _q_prep_kernel(q_ref, qw_ref, kn_ref, qx_ref, bmax_ref, *, n_ctx, tm):
    hd = ATT_HEAD_DIM
    rope, _ = _rope_fn(n_ctx + pl.program_id(0) * tm, tm, n_ctx)
    lane = lax.broadcasted_iota(jnp.int32, (tm, hd), 1)
    scale = hd ** -0.5 * _LOG2E
    bmax = jnp.zeros((tm, 1), F32)
    for h in range(ATT_HEADS):
        sl = slice(h * hd, (h + 1) * hd)
        xn = _rms(q_ref[:, sl].astype(F32), qw_ref[...])
        qr = (rope(xn) * scale).astype(BF16)
        qf = qr.astype(F32)
        bound = jnp.sqrt(jnp.sum(qf * qf, axis=-1, keepdims=True)) * kn_ref[...]
        qx_ref[:, h * _ATT_PAD:h * _ATT_PAD + hd] = qr
        qx_ref[:, h * _ATT_PAD + hd:(h + 1) * _ATT_PAD] = jnp.where(lane == 0, -bound, 0.0).astype(BF16)
        bmax = jnp.maximum(bmax, bound)
    bmax_ref[...] = jnp.broadcast_to(jnp.max(bmax, axis=0, keepdims=True), bmax_ref.shape)


def _qk_prep(zmain, q_norm, k_norm, n_ctx):
    n = zmain.shape[0]
    lq = n - n_ctx
    tm = 256
    hd = ATT_HEAD_DIM
    assert n_ctx % tm == 0
    ncb = n_ctx // tm
    kx, vx, kmax = pl.pallas_call(
        functools.partial(_k_prep_kernel, n_ctx=n_ctx, tm=tm), grid=(n // tm,),
        in_specs=[pl.BlockSpec((tm, W_ATT_KV), lambda i: (i, OAK // W_ATT_KV)),
                  pl.BlockSpec((tm, W_ATT_KV), lambda i: (i, OAV // W_ATT_KV)),
                  pl.BlockSpec((1, hd), lambda i: (0, 0))],
        out_specs=[pl.BlockSpec((tm, ATT_KV_HEADS * _ATT_PAD), lambda i: (i, 0)),
                   pl.BlockSpec((tm, ATT_KV_HEADS * _ATT_PAD), lambda i: (i, 0)),
                   pl.BlockSpec((None, 8, 128), lambda i: (i, 0, 0))],
        out_shape=[jax.ShapeDtypeStruct((n, ATT_KV_HEADS * _ATT_PAD), BF16),
                   jax.ShapeDtypeStruct((n, ATT_KV_HEADS * _ATT_PAD), BF16),
                   jax.ShapeDtypeStruct((n // tm, 8, 128), F32)],
        compiler_params=_params(("parallel",), 32),
        name="k_norm_rope",
    )(zmain, zmain, k_norm.reshape(1, hd))
    knorm = jnp.sqrt(jnp.max(kmax)).reshape(1, 1)
    qx, bmax = pl.pallas_call(
        functools.partial(_q_prep_kernel, n_ctx=n_ctx, tm=tm), grid=(lq // tm,),
        in_specs=[pl.BlockSpec((tm, W_ATT), lambda i: (i + ncb, OAQ // W_ATT)),
                  pl.BlockSpec((1, hd), lambda i: (0, 0)),
                  pl.BlockSpec((1, 1), lambda i: (0, 0))],
        out_specs=[pl.BlockSpec((tm, ATT_HEADS * _ATT_PAD), lambda i: (i, 0)),
                   pl.BlockSpec((None, 8, 128), lambda i: (i, 0, 0))],
        out_shape=[jax.ShapeDtypeStruct((lq, ATT_HEADS * _ATT_PAD), BF16),
                   jax.ShapeDtypeStruct((lq // tm, 8, 128), F32)],
        compiler_params=_params(("parallel",), 32),
        name="q_norm_rope",
    )(zmain, q_norm.reshape(1, hd), knorm)
    return qx, kx, vx, jnp.max(bmax)


def _flash_fixed_kernel(q_ref, kt_ref, v_ref, o_ref, acc_ref):
    c = pl.program_id(2)
    hd = ATT_HEAD_DIM

    @pl.when(c == 0)
    def _():
        acc_ref[...] = jnp.zeros_like(acc_ref)
    kt = kt_ref[...]
    v = v_ref[...]
    for g in range(ATT_GROUP):
        s = jnp.dot(q_ref[:, g * _ATT_PAD:(g + 1) * _ATT_PAD], kt, preferred_element_type=F32)
        acc_ref[g] += jnp.dot(jnp.exp2(s).astype(BF16), v, preferred_element_type=F32)

    @pl.when(c == pl.num_programs(2) - 1)
    def _():
        for g in range(ATT_GROUP):
            a = acc_ref[g]
            o_ref[:, g * hd:(g + 1) * hd] = (a[:, 0:hd] / a[:, hd:2 * hd]).astype(o_ref.dtype)


def _flash_online_kernel(q_ref, kt_ref, v_ref, o_ref, m_ref, acc_ref):
    c = pl.program_id(2)
    hd = ATT_HEAD_DIM

    @pl.when(c == 0)
    def _():
        m_ref[...] = jnp.full(m_ref.shape, -jnp.inf, F32)
        acc_ref[...] = jnp.zeros_like(acc_ref)
    kt = kt_ref[...]
    v = v_ref[...]
    for g in range(ATT_GROUP):
        s = jnp.dot(q_ref[:, g * _ATT_PAD:(g + 1) * _ATT_PAD], kt, preferred_element_type=F32)
        m_prev = m_ref[g]
        m_new = jnp.maximum(m_prev, jnp.max(s, axis=-1, keepdims=True))
        p = jnp.exp2(s - m_new).astype(BF16)
        acc_ref[g] = jnp.exp2(m_prev - m_new) * acc_ref[g] + jnp.dot(p, v, preferred_element_type=F32)
        m_ref[g] = m_new

    @pl.when(c == pl.num_programs(2) - 1)
    def _():
        for g in range(ATT_GROUP):
            a = acc_ref[g]
            o_ref[:, g * hd:(g + 1) * hd] = (a[:, 0:hd] / a[:, hd:2 * hd]).astype(o_ref.dtype)


def _attention(qx, kx, vx, bound_log2):
    lq = qx.shape[0]
    n = kx.shape[0]
    kt = kx.T
    gw = ATT_GROUP * _ATT_PAD
    ow = ATT_GROUP * ATT_HEAD_DIM
    def call(kern, tq, tk, scratch, name):
        return pl.pallas_call(
            kern, grid=(ATT_KV_HEADS, lq // tq, n // tk),
            in_specs=[pl.BlockSpec((tq, gw), lambda h, i, j: (i, h)),
                      pl.BlockSpec((_ATT_PAD, tk), lambda h, i, j: (h, j)),
                      pl.BlockSpec((tk, _ATT_PAD), lambda h, i, j: (j, h))],
            out_specs=pl.BlockSpec((tq, ow), lambda h, i, j: (i, h)),
            out_shape=jax.ShapeDtypeStruct((lq, W_ATT), BF16),
            scratch_shapes=scratch(tq) + [pltpu.VMEM((ATT_GROUP, tq, _ATT_PAD), F32)],
            compiler_params=_params(("parallel", "parallel", "arbitrary"), 56),
            name=name,
        )(qx, kt, vx)

    def fixed(_):
        return call(_flash_fixed_kernel, _pick(lq, (1024, 512, 256, 128)), _pick(n, (3328, 1280, 512, 256)),
                    lambda tq: [], "gqa_flash_fixed_shift")

    def online(_):
        return call(_flash_online_kernel, _pick(lq, (512, 256, 128)), _pick(n, (640, 512, 256, 128)),
                    lambda tq: [pltpu.VMEM((ATT_GROUP, tq, 1), F32)], "gqa_flash_online")
    return lax.cond(bound_log2 <= _FIXED_SHIFT_MAX_BOUND * _LOG2E, fixed, online, None)


def _odd_out_kernel(of_ref, ob_ref, r_ref, att_ref, gn_ref, wo_ref, h_ref, nw_ref, mod_ref, o_ref):
    o = of_ref[...] + ob_ref[...]
    r = r_ref[...].astype(F32)
    silu_r = r * _sigmoid(r)
    parts = []
    for h in range(GLA_HEADS):
        vs = slice(h * GLA_DV, (h + 1) * GLA_DV)
        parts.append((_rms(o[:, vs], gn_ref[:, vs]) * silu_r[:, vs]).astype(BF16))
    gla = jnp.concatenate(parts, axis=1)
    mix = jnp.dot(gla, wo_ref[0:W_GLA_V, :], preferred_element_type=F32)
    mix = mix + jnp.dot(att_ref[...], wo_ref[W_GLA_V:W_GLA_V + W_ATT, :], preferred_element_type=F32)
    o_ref[...] = mix
    _add_gated_norm(o_ref, h_ref, nw_ref, mod_ref, 0, 0)


def _odd_out(o_f, o_b, zmain, att, gla_norm, w_out, h_all, nw, mod, n_ctx):
    n, d = h_all.shape
    lq = n - n_ctx
    tm = _pick(lq, (256, 128))
    assert n_ctx % tm == 0
    off = n_ctx // tm
    return pl.pallas_call(
        _odd_out_kernel, grid=(lq // tm,),
        in_specs=[pl.BlockSpec((tm, W_GLA_V), lambda i: (i + off, 0)),
                  pl.BlockSpec((tm, W_GLA_V), lambda i: (i + off, 0)),
                  pl.BlockSpec((tm, W_GLA_V), lambda i: (i + off, OR // W_GLA_V)),
                  pl.BlockSpec((tm, W_ATT), lambda i: (i, 0)),
                  pl.BlockSpec((1, W_GLA_V), lambda i: (0, 0)),
                  pl.BlockSpec((d, d), lambda i: (0, 0)),
                  pl.BlockSpec((tm, d), lambda i: (i + off, 0)),
                  pl.BlockSpec((1, d), lambda i: (0, 0)),
                  pl.BlockSpec((8, d), lambda i: (0, 0))],
        out_specs=pl.BlockSpec((tm, d), lambda i: (i, 0)),
        out_shape=jax.ShapeDtypeStruct((lq, d), F32),
        compiler_params=_params(("parallel",), 48),
        name="odd_mixer_out",
    )(o_f, o_b, zmain, att, gla_norm.reshape(1, W_GLA_V), w_out, h_all, nw.reshape(1, d), mod)


def _router_kernel(h_ref, nw_ref, mod_ref, wr_ref, br_ref, v_ref, r_ref):
    x = h_ref[...]
    v = _rms(x, nw_ref[...]) * (1.0 + mod_ref[0:1, :]) + mod_ref[1:2, :]
    v_ref[...] = v
    logits = _dot_hi(v, wr_ref[...]) + br_ref[...]
    lane = lax.broadcasted_iota(jnp.int32, logits.shape, 1)
    m1 = jnp.max(logits, axis=-1, keepdims=True)
    i1 = jnp.min(jnp.where(logits == m1, lane, 128), axis=-1, keepdims=True)
    rest = jnp.where(lane == i1, -jnp.inf, logits)
    m2 = jnp.max(rest, axis=-1, keepdims=True)
    i2 = jnp.min(jnp.where(rest == m2, lane, 128), axis=-1, keepdims=True)
    e2 = jnp.exp(m2 - m1)
    g1 = 1.0 / (1.0 + e2)
    g2 = e2 / (1.0 + e2)
    out = jnp.where(lane == 0, i1.astype(F32), 0.0)
    out = jnp.where(lane == 1, i2.astype(F32), out)
    out = jnp.where(lane == 2, g1, out)
    out = jnp.where(lane == 3, g2, out)
    r_ref[...] = out


def _router(h_lat, nw, mod, w_router, b_router):
    lq, d = h_lat.shape
    tm = _pick(lq, (512, 256, 128))
    wr = jnp.pad(w_router, ((0, 0), (0, 128 - N_EXPERTS)))
    br = jnp.pad(b_router, (0, 128 - N_EXPERTS), constant_values=-1e30).reshape(1, 128)
    return pl.pallas_call(
        _router_kernel, grid=(lq // tm,),
        in_specs=[pl.BlockSpec((tm, d), lambda i: (i, 0)),
                  pl.BlockSpec((1, d), lambda i: (0, 0)),
                  pl.BlockSpec((8, d), lambda i: (0, 0)),
                  pl.BlockSpec((d, 128), lambda i: (0, 0)),
                  pl.BlockSpec((1, 128), lambda i: (0, 0))],
        out_specs=[pl.BlockSpec((tm, d), lambda i: (i, 0)), pl.BlockSpec((tm, 128), lambda i: (i, 0))],
        out_shape=[jax.ShapeDtypeStruct((lq, d), F32), jax.ShapeDtypeStruct((lq, 128), F32)],
        compiler_params=_params(("parallel",), 40),
        name="router_top2",
    )(h_lat, nw.reshape(1, d), mod, wr, br)


_ROW_DMA_PRIORITY = 1


def _expert_kernel(te_ref, tv_ref, src_ref, dst_ref, v_hbm, gt_ref, wg_ref, wu_ref, wd_ref, y_hbm,
                   xg_ref, xb_ref, ys_ref, acc_ref, gsem, ssem, *, tm, nt, nf):
    j = pl.program_id(0)
    f = pl.program_id(1)
    slot = j % 2
    per_step = tm // nf
    lo = f * per_step
    valid = tv_ref[j] == 1

    def start_gather(tile, slot_, row):
        tok = src_ref[tile * tm + row]
        pltpu.make_async_copy(v_hbm.at[pl.ds(tok, 1)], xg_ref.at[slot_, pl.ds(row, 1)],
                              gsem.at[slot_]).start(priority=_ROW_DMA_PRIORITY)

    def start_scatter(row):
        dst = dst_ref[j * tm + row]
        pltpu.make_async_copy(ys_ref.at[1 - slot, pl.ds(row, 1)], y_hbm.at[pl.ds(dst, 1)],
                              ssem.at[1 - slot]).start(priority=_ROW_DMA_PRIORITY)

    def wait_scatter(slot_):
        pltpu.make_async_copy(ys_ref.at[slot_], y_hbm.at[pl.ds(0, tm)], ssem.at[slot_]).wait()

    def gather_loop(tile, slot_, first, count):
        def body(r, carry):
            start_gather(tile, slot_, first + r)
            return carry
        lax.fori_loop(0, count, body, 0, unroll=8)

    @pl.when((j == 0) & (f == 0))
    def _():
        gather_loop(0, 0, 0, tm)
        ys_ref[1] = jnp.zeros((tm, ys_ref.shape[2]), F32)

    @pl.when(f == 0)
    def _():
        pltpu.make_async_copy(v_hbm.at[pl.ds(0, tm)], xg_ref.at[slot], gsem.at[slot]).wait()
        xb_ref[...] = xg_ref[slot].astype(BF16)
        acc_ref[...] = jnp.zeros_like(acc_ref)
        for row in range(nf * per_step, tm):
            start_scatter(row)

        @pl.when(j + 1 < nt)
        def _():
            for row in range(nf * per_step, tm):
                start_gather(j + 1, 1 - slot, row)

    @pl.when(valid)
    def _():
        half = per_step // 2

        def gathers(first, last, after):
            for r in range(first, last):
                start_gather(j + 1, 1 - slot, lo + r + after)

        def scatters(first, last, after):
            for r in range(first, last):
                start_scatter(lo + r + after)

        def zero_after(val):
            bits = pltpu.bitcast(val[0:8, 0:128], jnp.int32)
            return ((bits & 1) >> 1)[0, 0]

        gathers(0, half, 0)
        x = xb_ref[...]
        a = jnp.dot(x, wg_ref[...], preferred_element_type=F32)
        gathers(half, per_step, zero_after(a))
        b = jnp.dot(x, wu_ref[...], preferred_element_type=F32)
        scatters(0, half, zero_after(b))
        act = (a * _sigmoid(a) * b).astype(BF16)
        down = jnp.dot(act, wd_ref[...], preferred_element_type=F32)
        scatters(half, per_step, zero_after(down))
        acc_ref[...] += down

    @pl.when(jnp.logical_not(valid))
    def _():
        @pl.when(j + 1 < nt)
        def _():
            gather_loop(j + 1, 1 - slot, lo, per_step)

        def body(r, carry):
            start_scatter(lo + r)
            return carry
        lax.fori_loop(0, per_step, body, 0, unroll=8)

    @pl.when(f == nf - 1)
    def _():
        @pl.when(j > 0)
        def _():
            wait_scatter(slot)
        ys_ref[slot] = acc_ref[...] * gt_ref[...]

        @pl.when(j == nt - 1)
        def _():
            wait_scatter(1 - slot)


def _combine_kernel(y0_ref, y1_ref, h_ref, nw_ref, mod_ref, o_ref):
    o_ref[...] = h_ref[...] + mod_ref[4:5, :] * _rms(y0_ref[...] + y1_ref[...], nw_ref[...])


def _count_le(sorted_vals, queries):
    return jnp.sum((sorted_vals[None, :] <= queries[:, None]).astype(jnp.int32), axis=1)


def _moe(h_lat, nw2, nw3, mod, w_router, b_router, wg, wu, wd):
    lq, d = h_lat.shape
    e = N_EXPERTS
    dff = wg.shape[2]
    tm = _pick(lq, (512, 256))
    tf = 1024
    v, route = _router(h_lat, nw2, mod, w_router, b_router)
    idx = route[:, 0:2].astype(jnp.int32)
    gates = route[:, 2:4]

    npairs = 2 * lq
    nt = npairs // tm + e + 1
    flat_e = idx.reshape(-1)
    pair_id = jnp.arange(npairs, dtype=jnp.int32)
    onehot = (flat_e[:, None] == jnp.arange(e, dtype=jnp.int32)[None, :]).astype(jnp.int32)
    csum = jnp.cumsum(onehot, axis=0)
    rank = jnp.sum(csum * onehot, axis=1) - 1
    counts = csum[-1]
    padded = ((counts + tm - 1) // tm) * tm
    ends = jnp.cumsum(padded)
    starts = ends - padded
    pos = jnp.sum(onehot * starts[None, :], axis=1) + rank
    pair = jnp.stack([pair_id, lax.bitcast_convert_type(gates.reshape(-1), jnp.int32)], axis=1)
    empty = jnp.broadcast_to(jnp.array([-1, 0], jnp.int32), (nt * tm, 2))
    slots = empty.at[pos].set(pair)
    sorted_pair = slots[:, 0]
    sorted_gate = lax.bitcast_convert_type(slots[:, 1], F32)
    is_pad = sorted_pair < 0
    src_row = jnp.where(is_pad, 0, sorted_pair >> 1).astype(jnp.int32)
    pad_rank = jnp.cumsum(is_pad.astype(jnp.int32)) - 1
    dst_row = jnp.where(is_pad, npairs + pad_rank, (sorted_pair & 1) * lq + (sorted_pair >> 1)).astype(jnp.int32)
    dst_row = jnp.concatenate([(nt - 1) * tm + jnp.arange(tm, dtype=jnp.int32), dst_row])
    n_used = ends[-1] // tm
    tile_ids = jnp.arange(nt, dtype=jnp.int32)
    tile_valid = (tile_ids < n_used).astype(jnp.int32)
    tile_exp = jnp.minimum(_count_le(ends, jnp.minimum(tile_ids, n_used - 1) * tm), e - 1)

    nf = dff // tf
    pinned = lambda j, f, tv: jnp.where(tv[j] == 1, f, nf - 1)
    y = pl.pallas_call(
        functools.partial(_expert_kernel, tm=tm, nt=nt, nf=nf),
        grid_spec=pltpu.PrefetchScalarGridSpec(
            num_scalar_prefetch=4, grid=(nt, nf),
            in_specs=[pl.BlockSpec(memory_space=pl.ANY),
                      pl.BlockSpec((tm, 1), lambda j, f, te, tv, src, dst: (j, 0)),
                      pl.BlockSpec((None, d, tf), lambda j, f, te, tv, src, dst: (te[j], 0, pinned(j, f, tv))),
                      pl.BlockSpec((None, d, tf), lambda j, f, te, tv, src, dst: (te[j], 0, pinned(j, f, tv))),
                      pl.BlockSpec((None, tf, d), lambda j, f, te, tv, src, dst: (te[j], pinned(j, f, tv), 0))],
            out_specs=pl.BlockSpec(memory_space=pl.ANY),
            scratch_shapes=[pltpu.VMEM((2, tm, d), F32), pltpu.VMEM((tm, d), BF16), pltpu.VMEM((2, tm, d), F32),
                            pltpu.VMEM((tm, d), F32), pltpu.SemaphoreType.DMA((2,)),
                            pltpu.SemaphoreType.DMA((2,))]),
        out_shape=jax.ShapeDtypeStruct((nt * tm, d), F32),
        compiler_params=pltpu.CompilerParams(dimension_semantics=("arbitrary", "arbitrary"),
                                             vmem_limit_bytes=56 * MIB, disable_bounds_checks=True),
        name="moe_experts",
    )(tile_exp, tile_valid, src_row, dst_row, v, sorted_gate.reshape(nt * tm, 1), wg, wu, wd)

    tc = _pick(lq, (256, 128))
    return pl.pallas_call(
        _combine_kernel, grid=(lq // tc,),
        in_specs=[pl.BlockSpec((tc, d), lambda i: (i, 0)),
                  pl.BlockSpec((tc, d), lambda i: (i + lq // tc, 0)),
                  pl.BlockSpec((tc, d), lambda i: (i, 0)),
                  pl.BlockSpec((1, d), lambda i: (0, 0)),
                  pl.BlockSpec((8, d), lambda i: (0, 0))],
        out_specs=pl.BlockSpec((tc, d), lambda i: (i, 0)),
        out_shape=jax.ShapeDtypeStruct((lq, d), F32),
        compiler_params=_params(("parallel",), 32),
        name="moe_combine",
    )(y, y, h_lat, nw3.reshape(1, d), mod)


def kernel(x, c, ctx, c_ctx, w_mod, b_mod, norms, e_w_in, e_pool_w, e_pool_scale, e_s5_lam_re, e_s5_lam_im, e_s5_log_dt, e_s5_b_re, e_s5_b_im, e_s5_c_re, e_s5_c_im, e_s5_d, e_s5_w_glu, e_w_out, e_ffn_gate, e_ffn_up, e_ffn_down, o_w_in, o_gla_w_a, o_gla_b_a, o_gla_norm, o_q_norm, o_k_norm, o_w_out, o_router, o_router_b, o_moe_gate, o_moe_up, o_moe_down):
    assert x.shape[0] == 1 and w_mod.shape[0] == 2 and e_w_in.shape[0] == 1 and o_w_in.shape[0] == 1
    d = x.shape[2]
    n_ctx = ctx.shape[1]
    h = jnp.concatenate([ctx[0], x[0]], axis=0)

    vecs = jnp.zeros((8, d), F32).at[0].set(c[0]).at[1].set(c_ctx)
    mods = _modulation(vecs, w_mod, b_mod)

    mod1, mod2 = _mod_rows(mods[0], 0), _mod_rows(mods[0], 1)
    z = _normed_matmul(h, norms[0, 0], mod1, e_w_in[0].astype(BF16), n_ctx)
    pool_out = _pool_mixer(z, e_pool_w[0].astype(BF16), e_pool_scale[0], n_ctx)
    m_op, qre, qim, pre, pim, at = _s5_params(e_s5_lam_re[0], e_s5_lam_im[0], e_s5_log_dt[0],
                                              e_s5_b_re[0], e_s5_b_im[0], e_s5_c_re[0], e_s5_c_im[0])
    y_s5 = _s5_mixer(z[:, W_POOL:], m_op, qre, qim, pre, pim, at, n_ctx)
    h = _even_out(pool_out, y_s5, z, e_s5_d[0], e_s5_w_glu[0].astype(BF16), e_w_out[0].astype(BF16),
                  h, norms[0, 1], mod1, n_ctx)
    h = _dense_ffn(h, norms[0, 2], norms[0, 3], mod2, e_ffn_gate[0].astype(BF16),
                   e_ffn_up[0].astype(BF16), e_ffn_down[0].astype(BF16), n_ctx)

    mod1, mod2 = _mod_rows(mods[1], 0), _mod_rows(mods[1], 1)
    w_in = o_w_in[0]
    a0 = 2 * W_GLA_K + W_GLA_V
    w_main = jnp.concatenate([w_in[:, :a0], w_in[:, a0 + GLA_RANK:]], axis=1).astype(BF16)
    w_aux = jnp.pad(w_in[:, a0:a0 + GLA_RANK], ((0, 0), (0, 128 - GLA_RANK))).astype(BF16)
    zmain, a_aux = _normed_matmul(h, norms[1, 0], mod1, w_main, n_ctx, w_aux=w_aux)
    o_f, o_b = _gla_mixer(zmain, a_aux, o_gla_w_a[0], o_gla_b_a[0], n_ctx)
    qx, kx, vx, bound_log2 = _qk_prep(zmain, o_q_norm[0], o_k_norm[0], n_ctx)
    att = _attention(qx, kx, vx, bound_log2)
    h_lat = _odd_out(o_f, o_b, zmain, att, o_gla_norm[0], o_w_out[0].astype(BF16), h, norms[1, 1], mod1, n_ctx)
    out = _moe(h_lat, norms[1, 2], norms[1, 3], mod2, o_router[0], o_router_b[0],
               o_moe_gate[0].astype(BF16), o_moe_up[0].astype(BF16), o_moe_down[0].astype(BF16))
    return out[None]
```

```python
import functools
import math

import jax
import jax.numpy as jnp
from jax import lax
from jax.experimental import pallas as pl
from jax.experimental.pallas import tpu as pltpu

F32 = jnp.float32
BF16 = jnp.bfloat16
HI = lax.Precision.HIGHEST
EPS = 1e-6

D_MODEL = 2048
GRID_W = 64
N_MOD = 6

POOL_WINDOWS = (2, 4, 8, 16)
POOL_GROUP = 384
W_POOL = 1536
W_S5 = 512
S5_CH = 16
S5_STATE = 64
S5_GROUPS = 32
S5_T = 32

GLA_HEADS = 4
GLA_DK = 128
GLA_DV = 256
GLA_RANK = 16
GLA_TAU = 16.0
GLA_CHUNK = 64
W_GLA_K = 512
W_GLA_V = 1024
ATT_HEAD_DIM = 128
ATT_HEADS = 8
ATT_KV_HEADS = 2
ATT_GROUP = 4
W_ATT = 1024
W_ATT_KV = 256
ROPE_THETA = 10000.0

D_FF = 7168
N_EXPERTS = 8

OQ, OK_, OV, OR, OAQ, OAK, OAV = 0, 512, 1024, 2048, 3072, 4096, 4352
W_ODD_MAIN = 4608

MIB = 2 ** 20


def _params(sem, vmem_mib):
    return pltpu.CompilerParams(dimension_semantics=sem, vmem_limit_bytes=vmem_mib * MIB)


def _pick(n, cands):
    for c in cands:
        if n % c == 0:
            return c
    raise ValueError(f"no tile for {n} in {cands}")


def _sigmoid(x):
    return 1.0 / (1.0 + jnp.exp(-x))


def _rms(x, w):
    return x * lax.rsqrt(jnp.mean(x * x, axis=-1, keepdims=True) + EPS) * w


_NORM_CHUNK = 16


def _store_norm_mod(dst_ref, h_ref, nw_ref, mod_ref, row0, n_ctx):
    tm = h_ref.shape[0]
    nw = nw_ref[...]
    w_lat = nw * (1.0 + mod_ref[0:1, :])
    sh_lat = mod_ref[1:2, :]

    def rows(c):
        r0 = pl.multiple_of(c * _NORM_CHUNK, _NORM_CHUNK)
        x = h_ref[pl.ds(r0, _NORM_CHUNK), :]
        return r0, x * lax.rsqrt(jnp.mean(x * x, axis=-1, keepdims=True) + EPS)

    @pl.when(row0 >= n_ctx)
    def _():
        def body(c, carry):
            r0, xr = rows(c)
            dst_ref[pl.ds(r0, _NORM_CHUNK), :] = (xr * w_lat + sh_lat).astype(dst_ref.dtype)
            return carry
        lax.fori_loop(0, tm // _NORM_CHUNK, body, 0, unroll=4)

    @pl.when(row0 < n_ctx)
    def _():
        w_ctx = nw * (1.0 + mod_ref[2:3, :])
        sh_ctx = mod_ref[3:4, :]

        def body(c, carry):
            r0, xr = rows(c)
            is_ctx = row0 + r0 + lax.broadcasted_iota(jnp.int32, (_NORM_CHUNK, 1), 0) < n_ctx
            u = xr * jnp.where(is_ctx, w_ctx, w_lat) + jnp.where(is_ctx, sh_ctx, sh_lat)
            dst_ref[pl.ds(r0, _NORM_CHUNK), :] = u.astype(dst_ref.dtype)
            return carry
        lax.fori_loop(0, tm // _NORM_CHUNK, body, 0, unroll=4)


def _gate_rows(mod_ref, row0, tm, n_ctx):
    rows = row0 + lax.broadcasted_iota(jnp.int32, (tm, 1), 0)
    return jnp.where(rows < n_ctx, mod_ref[5:6, :], mod_ref[4:5, :])


def _mod_kernel(v_ref, w_ref, b_ref, o_ref):
    v = v_ref[...]
    s = v * _sigmoid(v)
    o_ref[...] = jnp.dot(s, w_ref[...], precision=HI, preferred_element_type=F32) + b_ref[...]


def _modulation(vecs, w_mod, b_mod):
    depth, d, n6 = w_mod.shape
    tn = 1024
    return pl.pallas_call(
        _mod_kernel,
        grid=(depth, n6 // tn),
        in_specs=[pl.BlockSpec((8, d), lambda l, j: (0, 0)),
                  pl.BlockSpec((None, d, tn), lambda l, j: (l, 0, j)),
                  pl.BlockSpec((None, 1, tn), lambda l, j: (l, 0, j))],
        out_specs=pl.BlockSpec((None, 8, tn), lambda l, j: (l, 0, j)),
        out_shape=jax.ShapeDtypeStruct((depth, 8, n6), F32),
        compiler_params=_params(("parallel", "parallel"), 40),
        name="modulation",
    )(vecs, w_mod, b_mod.reshape(depth, 1, n6))


def _mod_rows(m, sub):
    d = m.shape[1] // N_MOD
    m6 = m.reshape(8, N_MOD, d)
    sh, sc, g = m6[:, 3 * sub + 0], m6[:, 3 * sub + 1], m6[:, 3 * sub + 2]
    z = jnp.zeros((d,), F32)
    return jnp.stack([sc[0], sh[0], sc[1], sh[1], g[0], g[1], z, z])


def _nmm_kernel(h_ref, nw_ref, mod_ref, w_ref, o_ref, u_ref, *, n_ctx, tm):
    @pl.when(pl.program_id(1) == 0)
    def _():
        _store_norm_mod(u_ref, h_ref, nw_ref, mod_ref, pl.program_id(0) * tm, n_ctx)
    o_ref[...] = jnp.dot(u_ref[...], w_ref[...], preferred_element_type=F32).astype(o_ref.dtype)


def _nmm_aux_kernel(h_ref, nw_ref, mod_ref, w_ref, wa_ref, o_ref, oa_ref, u_ref, *, n_ctx, tm):
    @pl.when(pl.program_id(1) == 0)
    def _():
        _store_norm_mod(u_ref, h_ref, nw_ref, mod_ref, pl.program_id(0) * tm, n_ctx)
        oa_ref[...] = jnp.dot(u_ref[...], wa_ref[...], preferred_element_type=F32)
    o_ref[...] = jnp.dot(u_ref[...], w_ref[...], preferred_element_type=F32).astype(o_ref.dtype)


def _normed_matmul(h, nw, mod, w, n_ctx, w_aux=None):
    n, d = h.shape
    nout = w.shape[1]
    tm = _pick(n, (640, 512, 256, 128))
    tn = _pick(nout, (1536, 1024, 512, 256, 128))
    in_specs = [pl.BlockSpec((tm, d), lambda i, j: (i, 0)),
                pl.BlockSpec((1, d), lambda i, j: (0, 0)),
                pl.BlockSpec((8, d), lambda i, j: (0, 0)),
                pl.BlockSpec((d, tn), lambda i, j: (0, j))]
    out_specs = pl.BlockSpec((tm, tn), lambda i, j: (i, j))
    out_shape = jax.ShapeDtypeStruct((n, nout), BF16)
    args = [h, nw.reshape(1, d), mod, w]
    if w_aux is None:
        kern = functools.partial(_nmm_kernel, n_ctx=n_ctx, tm=tm)
    else:
        na = w_aux.shape[1]
        kern = functools.partial(_nmm_aux_kernel, n_ctx=n_ctx, tm=tm)
        in_specs.append(pl.BlockSpec((d, na), lambda i, j: (0, 0)))
        out_specs = [out_specs, pl.BlockSpec((tm, na), lambda i, j: (i, 0))]
        out_shape = [out_shape, jax.ShapeDtypeStruct((n, na), F32)]
        args.append(w_aux)
    return pl.pallas_call(
        kern, grid=(n // tm, nout // tn), in_specs=in_specs, out_specs=out_specs, out_shape=out_shape,
        scratch_shapes=[pltpu.VMEM((tm, d), BF16)],
        compiler_params=_params(("parallel", "arbitrary"), 48),
        name="normed_proj",
    )(*args)


_POOL_HALO = 16


def _pool_kernel(z_ref, zp_ref, zn_ref, pw_ref, ps_ref, o_ref, ext_ref, *, n_ctx, n_all, tm):
    i = pl.program_id(0)
    row0 = i * tm
    in_ctx = row0 < n_ctx
    seq_start = jnp.where(in_ctx, 0, n_ctx)
    seq_end = jnp.where(in_ctx, n_ctx, n_all)
    has_prev = row0 > seq_start
    has_next = row0 + tm < seq_end
    hl = _POOL_HALO
    ext_ref[0:hl, :] = jnp.where(has_prev, zp_ref[...].astype(F32), 0.0)
    ext_ref[hl:hl + tm, :] = z_ref[...].astype(F32)
    ext_ref[hl + tm:hl + tm + hl, :] = jnp.where(has_next, zn_ref[...].astype(F32), 0.0)
    t = row0 - seq_start + lax.broadcasted_iota(jnp.int32, (tm, 1), 0)
    seq_len = seq_end - seq_start
    for g, w in enumerate(POOL_WINDOWS):
        c0 = g * POOL_GROUP
        acc = None
        for k in range(-(w // 2), w - w // 2):
            v = ext_ref[hl + k:hl + k + tm, c0:c0 + POOL_GROUP]
            acc = v if acc is None else acc + v
        lo = jnp.maximum(t - w // 2, 0)
        hi = jnp.minimum(t + (w - w // 2), seq_len)
        cnt = (hi - lo).astype(F32)
        dlt = acc / cnt - ext_ref[hl:hl + tm, c0:c0 + POOL_GROUP]
        y = jnp.dot(dlt.astype(BF16), pw_ref[g], preferred_element_type=F32)
        o_ref[:, c0:c0 + POOL_GROUP] = (y * ps_ref[:, c0:c0 + POOL_GROUP]).astype(o_ref.dtype)


def _pool_mixer(z, pool_w, pool_scale, n_ctx):
    n = z.shape[0]
    tm = 256
    assert n_ctx % tm == 0 and n % tm == 0
    hl = _POOL_HALO
    nh = n // hl
    per = tm // hl
    kern = functools.partial(_pool_kernel, n_ctx=n_ctx, n_all=n, tm=tm)
    return pl.pallas_call(
        kern, grid=(n // tm,),
        in_specs=[pl.BlockSpec((tm, W_POOL), lambda i: (i, 0)),
                  pl.BlockSpec((hl, W_POOL), lambda i: (jnp.maximum(i * per - 1, 0), 0)),
                  pl.BlockSpec((hl, W_POOL), lambda i: (jnp.minimum((i + 1) * per, nh - 1), 0)),
                  pl.BlockSpec((4, POOL_GROUP, POOL_GROUP), lambda i: (0, 0, 0)),
                  pl.BlockSpec((1, W_POOL), lambda i: (0, 0))],
        out_specs=pl.BlockSpec((tm, W_POOL), lambda i: (i, 0)),
        out_shape=jax.ShapeDtypeStruct((n, W_POOL), BF16),
        scratch_shapes=[pltpu.VMEM((tm + 2 * hl, W_POOL), F32)],
        compiler_params=_params(("parallel",), 32),
        name="pool_mixer",
    )(z, z, z, pool_w, pool_scale.reshape(1, W_POOL))


def _dot_hi(a, b):
    return jnp.dot(a, b, precision=HI, preferred_element_type=F32)


def _dot_t_hi(a, b):
    return lax.dot_general(a, b, (((0,), (0,)), ((), ())), precision=HI, preferred_element_type=F32)


def _s5_param_kernel(lr_ref, li_ref, lrc_ref, lic_ref, ldt_ref, br_ref, bi_ref, cr_ref, ci_ref,
                     m_ref, qre_ref, qim_ref, pre_ref, pim_ref, at_ref):
    t_len = S5_T
    tp = t_len + 8
    wid = t_len * S5_CH
    col = lax.broadcasted_iota(jnp.int32, (tp, wid), 1)
    kid = lax.broadcasted_iota(jnp.int32, (tp, wid), 0)
    cq = col >> 4
    rep_nat = (kid == cq).astype(F32)
    rep_rev = (kid == t_len - 1 - cq).astype(F32)
    til = (lax.broadcasted_iota(jnp.int32, (S5_CH, wid), 0)
           == (lax.broadcasted_iota(jnp.int32, (S5_CH, wid), 1) & (S5_CH - 1))).astype(F32)
    colb = lax.broadcasted_iota(jnp.int32, (S5_CH, wid), 1) >> 4
    kk = lax.broadcasted_iota(jnp.int32, (tp, S5_STATE), 0).astype(F32)
    rows = [None] * t_len
    for d in range(2):
        dt = jnp.exp(ldt_ref[d])
        lr, li = lr_ref[d], li_ref[d]
        mag = jnp.exp(kk * (lr * dt))
        ang = kk * (li * dt)
        pr, pi_ = mag * jnp.cos(ang), mag * jnp.sin(ang)
        lrc, lic = lrc_ref[d], lic_ref[d]
        magc = jnp.exp(lrc * dt)
        arc, aic = magc * jnp.cos(lic * dt), magc * jnp.sin(lic * dt)
        den = lrc * lrc + lic * lic
        nr, ni = arc - 1.0, aic
        kre = (nr * lrc + ni * lic) / den
        kim = (ni * lrc - nr * lic) / den
        bbr = kre * br_ref[d] - kim * bi_ref[d]
        bbi = kre * bi_ref[d] + kim * br_ref[d]
        repq = rep_rev if d == 0 else rep_nat
        repp = ((kid == cq + 1) if d == 0 else (kid == t_len - cq)).astype(F32)
        repm = rep_nat if d == 0 else rep_rev
        bt_r, bt_i = _dot_hi(bbr, til), _dot_hi(bbi, til)
        eqr, eqi = _dot_t_hi(pr, repq), _dot_t_hi(pi_, repq)
        qre_ref[d] = (eqr * bt_r - eqi * bt_i).astype(qre_ref.dtype)
        qim_ref[d] = (eqr * bt_i + eqi * bt_r).astype(qim_ref.dtype)
        epr, epi = _dot_t_hi(pr, repp), _dot_t_hi(pi_, repp)
        ct_r, ct_i = _dot_t_hi(cr_ref[d], til), _dot_t_hi(ci_ref[d], til)
        pre_ref[d] = (ct_r * epr - ct_i * epi).astype(pre_ref.dtype)
        pim_ref[d] = (-(ct_r * epi + ct_i * epr)).astype(pim_ref.dtype)
        at_ref[d, 0:1, :] = pr[t_len:t_len + 1, :]
        at_ref[d, 1:2, :] = pi_[t_len:t_len + 1, :]
        emr, emi = _dot_t_hi(pr, repm), _dot_t_hi(pi_, repm)
        y_re = emr * ct_r - emi * ct_i
        y_im = emr * ct_i + emi * ct_r
        r = _dot_t_hi(bbr, y_re) - _dot_t_hi(bbi, y_im)
        for s in range(t_len):
            if d == 0:
                blk = jnp.where(colb >= s, pltpu.roll(r, S5_CH * s, 1), 0.0)
            else:
                blk = jnp.where(colb <= s, pltpu.roll(r, (S5_CH * (s + 1)) % wid, 1), 0.0)
            rows[s] = blk if rows[s] is None else rows[s] + blk
    for s in range(t_len):
        m_ref[s * S5_CH:(s + 1) * S5_CH, :] = rows[s].astype(m_ref.dtype)


def _s5_params(lam_re, lam_im, log_dt, b_re, b_im, c_re, c_im):
    g, p, n = S5_GROUPS, S5_STATE, S5_CH
    wid = S5_T * S5_CH

    def spec(*shape):
        return pl.BlockSpec((2, None) + shape, lambda gi: (0, gi) + (0,) * len(shape))

    def ospec(*shape):
        return pl.BlockSpec((None,) + shape, lambda gi: (gi,) + (0,) * len(shape))
    return pl.pallas_call(
        _s5_param_kernel, grid=(g,),
        in_specs=[spec(1, p), spec(1, p), spec(p, 1), spec(p, 1), spec(1, 1),
                  spec(p, n), spec(p, n), spec(n, p), spec(n, p)],
        out_specs=[ospec(wid, wid), ospec(2, p, wid), ospec(2, p, wid), ospec(2, p, wid), ospec(2, p, wid),
                   ospec(2, 2, p)],
        out_shape=[jax.ShapeDtypeStruct((g, wid, wid), BF16),
                   jax.ShapeDtypeStruct((g, 2, p, wid), BF16),
                   jax.ShapeDtypeStruct((g, 2, p, wid), BF16),
                   jax.ShapeDtypeStruct((g, 2, p, wid), BF16),
                   jax.ShapeDtypeStruct((g, 2, p, wid), BF16),
                   jax.ShapeDtypeStruct((g, 2, 2, p), F32)],
        compiler_params=_params(("parallel",), 32),
        name="s5_params",
    )(lam_re.reshape(2, g, 1, p), lam_im.reshape(2, g, 1, p), lam_re.reshape(2, g, p, 1),
      lam_im.reshape(2, g, p, 1), log_dt.reshape(2, g, 1, 1), b_re, b_im, c_re, c_im)


def _s5_state_kernel(u_ref, qre_ref, qim_ref, s_ref):
    u = u_ref[...]
    dn = (((1,), (1,)), ((), ()))
    s_ref[:, 0:128] = lax.dot_general(u, qre_ref[...], dn, preferred_element_type=F32)
    s_ref[:, 128:256] = lax.dot_general(u, qim_ref[...], dn, preferred_element_type=F32)


def _s5_scan_kernel(s_ref, are_ref, aim_ref, h_ref, *, nc, ncc):
    ar, ai = are_ref[...], aim_ref[...]
    is_f = lax.broadcasted_iota(jnp.int32, ar.shape, 1) < S5_STATE

    def body(i, carry):
        hr, hi = carry
        cf = i
        cb = jnp.where(i < ncc, ncc - 1 - i, nc + ncc - 1 - i)
        h_ref[cf, :, 0:64] = hr[:, 0:64]
        h_ref[cb, :, 64:128] = hr[:, 64:128]
        h_ref[cf, :, 128:192] = hi[:, 0:64]
        h_ref[cb, :, 192:256] = hi[:, 64:128]
        sf, sb = s_ref[cf], s_ref[cb]
        sr = jnp.where(is_f, sf[:, 0:128], sb[:, 0:128])
        si = jnp.where(is_f, sf[:, 128:256], sb[:, 128:256])
        return ar * hr - ai * hi + sr, ar * hi + ai * hr + si

    zero = jnp.zeros(ar.shape, F32)
    lax.fori_loop(0, nc, body, (zero, zero))


def _s5_out_kernel(u_ref, m_ref, h_ref, p_ref, y_ref):
    y = jnp.dot(u_ref[...], m_ref[...], preferred_element_type=F32)
    y_ref[...] = y + jnp.dot(h_ref[...].astype(BF16), p_ref[...], preferred_element_type=F32)


def _s5_mixer(s_all, m, qre, qim, pre, pim, at, n_ctx):
    n = s_all.shape[0]
    g, t_len, wid = S5_GROUPS, S5_T, S5_T * S5_CH
    nc, ncc = n // t_len, n_ctx // t_len
    assert n % t_len == 0 and n_ctx % t_len == 0
    u = s_all.reshape(nc, t_len, g, S5_CH).transpose(2, 0, 1, 3).reshape(g, nc, wid)
    p_all = jnp.concatenate([pre.reshape(g, 128, wid), pim.reshape(g, 128, wid)], axis=1)
    a_re = at[:, :, 0, :].reshape(g, 128)
    a_im = at[:, :, 1, :].reshape(g, 128)
    s_t = pl.pallas_call(
        _s5_state_kernel, grid=(g,),
        in_specs=[pl.BlockSpec((None, nc, wid), lambda i: (i, 0, 0)),
                  pl.BlockSpec((None, 128, wid), lambda i: (i, 0, 0)),
                  pl.BlockSpec((None, 128, wid), lambda i: (i, 0, 0))],
        out_specs=pl.BlockSpec((nc, 256), lambda i: (0, i)),
        out_shape=jax.ShapeDtypeStruct((nc, g * 256), F32),
        compiler_params=_params(("parallel",), 32),
        name="s5_chunk_state",
    )(u, qre.reshape(g, 128, wid), qim.reshape(g, 128, wid))
    h_t = pl.pallas_call(
        functools.partial(_s5_scan_kernel, nc=nc, ncc=ncc),
        out_shape=jax.ShapeDtypeStruct((nc, g, 256), F32),
        compiler_params=pltpu.CompilerParams(vmem_limit_bytes=48 * MIB),
        name="s5_chunk_scan",
    )(s_t.reshape(nc, g, 256), a_re, a_im).reshape(nc, g * 256)
    y = pl.pallas_call(
        _s5_out_kernel, grid=(g,),
        in_specs=[pl.BlockSpec((None, nc, wid), lambda i: (i, 0, 0)),
                  pl.BlockSpec((None, wid, wid), lambda i: (i, 0, 0)),
                  pl.BlockSpec((nc, 256), lambda i: (0, i)),
                  pl.BlockSpec((None, 256, wid), lambda i: (i, 0, 0))],
        out_specs=pl.BlockSpec((None, nc, wid), lambda i: (i, 0, 0)),
        out_shape=jax.ShapeDtypeStruct((g, nc, wid), F32),
        compiler_params=_params(("parallel",), 32),
        name="s5_chunk_out",
    )(u, m, h_t, p_all)
    return y.reshape(g, nc, t_len, S5_CH).transpose(1, 2, 0, 3).reshape(n, W_S5)


def _gelu_tanh(x):
    return 0.5 * x * (1.0 + jnp.tanh(math.sqrt(2.0 / math.pi) * (x + 0.044715 * (x * x * x))))


def _even_out_kernel(pool_ref, y_ref, s_ref, dsk_ref, wglu_ref, wo_ref, h_ref, nw_ref, mod_ref, o_ref,
                     *, n_ctx, tm):
    y = _gelu_tanh(y_ref[...] + s_ref[...].astype(F32) * dsk_ref[...])
    gate = jnp.dot(y.astype(BF16), wglu_ref[...], preferred_element_type=F32)
    s5 = (y * _sigmoid(gate)).astype(BF16)
    mix = jnp.dot(pool_ref[...], wo_ref[0:W_POOL, :], preferred_element_type=F32)
    mix = mix + jnp.dot(s5, wo_ref[W_POOL:W_POOL + W_S5, :], preferred_element_type=F32)
    g = _gate_rows(mod_ref, pl.program_id(0) * tm, tm, n_ctx)
    o_ref[...] = h_ref[...] + g * _rms(mix, nw_ref[...])


def _even_out(pool_out, y_s5, z, dsk, w_glu, w_out, h, nw, mod, n_ctx):
    n, d = h.shape
    tm = _pick(n, (320, 256, 128))
    return pl.pallas_call(
        functools.partial(_even_out_kernel, n_ctx=n_ctx, tm=tm), grid=(n // tm,),
        in_specs=[pl.BlockSpec((tm, W_POOL), lambda i: (i, 0)),
                  pl.BlockSpec((tm, W_S5), lambda i: (i, 0)),
                  pl.BlockSpec((tm, W_S5), lambda i: (i, W_POOL // W_S5)),
                  pl.BlockSpec((1, W_S5), lambda i: (0, 0)),
                  pl.BlockSpec((W_S5, W_S5), lambda i: (0, 0)),
                  pl.BlockSpec((d, d), lambda i: (0, 0)),
                  pl.BlockSpec((tm, d), lambda i: (i, 0)),
                  pl.BlockSpec((1, d), lambda i: (0, 0)),
                  pl.BlockSpec((8, d), lambda i: (0, 0))],
        out_specs=pl.BlockSpec((tm, d), lambda i: (i, 0)),
        out_shape=jax.ShapeDtypeStruct((n, d), F32),
        compiler_params=_params(("parallel",), 48),
        name="even_mixer_out",
    )(pool_out, y_s5, z, dsk.reshape(1, W_S5), w_glu, w_out, h, nw.reshape(1, d), mod)


def _ffn_kernel(h_ref, nw2_ref, nw3_ref, mod_ref, wg_ref, wu_ref, wd_ref, o_ref, v_ref, *, n_ctx, tm):
    f = pl.program_id(1)

    @pl.when(f == 0)
    def _():
        _store_norm_mod(v_ref, h_ref, nw2_ref, mod_ref, pl.program_id(0) * tm, n_ctx)
        o_ref[...] = jnp.zeros_like(o_ref)
    v = v_ref[...]
    a = jnp.dot(v, wg_ref[...], preferred_element_type=F32)
    b = jnp.dot(v, wu_ref[...], preferred_element_type=F32)
    act = (a * _sigmoid(a) * b).astype(BF16)
    o_ref[...] += jnp.dot(act, wd_ref[...], preferred_element_type=F32)

    @pl.when(f == pl.num_programs(1) - 1)
    def _():
        g = _gate_rows(mod_ref, pl.program_id(0) * tm, tm, n_ctx)
        o_ref[...] = h_ref[...] + g * _rms(o_ref[...], nw3_ref[...])


def _dense_ffn(h, nw2, nw3, mod, wg, wu, wd, n_ctx):
    n, d = h.shape
    dff = wg.shape[1]
    tm = _pick(n, (640, 512, 256, 128))
    tf = 512
    return pl.pallas_call(
        functools.partial(_ffn_kernel, n_ctx=n_ctx, tm=tm), grid=(n // tm, dff // tf),
        in_specs=[pl.BlockSpec((tm, d), lambda i, f: (i, 0)),
                  pl.BlockSpec((1, d), lambda i, f: (0, 0)),
                  pl.BlockSpec((1, d), lambda i, f: (0, 0)),
                  pl.BlockSpec((8, d), lambda i, f: (0, 0)),
                  pl.BlockSpec((d, tf), lambda i, f: (0, f)),
                  pl.BlockSpec((d, tf), lambda i, f: (0, f)),
                  pl.BlockSpec((tf, d), lambda i, f: (f, 0))],
        out_specs=pl.BlockSpec((tm, d), lambda i, f: (i, 0)),
        out_shape=jax.ShapeDtypeStruct((n, d), F32),
        scratch_shapes=[pltpu.VMEM((tm, d), BF16)],
        compiler_params=_params(("parallel", "arbitrary"), 56),
        name="dense_swiglu",
    )(h, nw2.reshape(1, d), nw3.reshape(1, d), mod, wg, wu, wd)


def _split_bf16(x):
    hi = x.astype(BF16)
    return hi, (x - hi.astype(F32)).astype(BF16)


def _gla_kernel(qf_ref, kf_ref, vf_ref, af_ref, qb_ref, kb_ref, vb_ref, ab_ref, wa_ref, ba_ref,
                of_ref, ob_ref, st_ref, *, tb):
    c = GLA_CHUNK
    nsub = tb // c

    @pl.when(pl.program_id(0) == 0)
    def _():
        st_ref[...] = jnp.zeros_like(st_ref)
    ri = lax.broadcasted_iota(jnp.int32, (c, c), 0)
    ci = lax.broadcasted_iota(jnp.int32, (c, c), 1)
    scale = GLA_DK ** -0.5
    nt = (((1,), (1,)), ((), ()))
    tn = (((0,), (0,)), ((), ()))
    dirs = ((qf_ref, kf_ref, vf_ref, af_ref, of_ref), (qb_ref, kb_ref, vb_ref, ab_ref, ob_ref))
    for d, (q_ref, k_ref, v_ref, a_ref, o_ref) in enumerate(dirs):
        mask = (ri >= ci) if d == 0 else (ci >= ri)
        cum = mask.astype(BF16)
        a_hi, a_lo = _split_bf16(a_ref[...])
        w_hi, w_lo = _split_bf16(wa_ref[d])
        z = (jnp.dot(a_hi, w_hi, preferred_element_type=F32) + jnp.dot(a_hi, w_lo, preferred_element_type=F32)
             + jnp.dot(a_lo, w_hi, preferred_element_type=F32)) + ba_ref[d]
        glog = (jnp.minimum(z, 0.0) - jnp.log(1.0 + jnp.exp(-jnp.abs(z)))) * (1.0 / GLA_TAU)
        order = range(nsub) if d == 0 else range(nsub - 1, -1, -1)
        for sc in order:
            r0 = sc * c
            g_hi, g_lo = _split_bf16(glog[r0:r0 + c, :])
            b = jnp.dot(cum, g_hi, preferred_element_type=F32) + jnp.dot(cum, g_lo, preferred_element_type=F32)
            b_last = b[c - 1:c, :] if d == 0 else b[0:1, :]
            dec = jnp.exp(b_last)
            q = q_ref[r0:r0 + c, :].astype(F32) * scale
            k = k_ref[r0:r0 + c, :].astype(F32)
            qe = (q * jnp.exp(b)).astype(BF16)
            ke = (k * jnp.exp(-b)).astype(BF16)
            kd = (k * jnp.exp(b_last - b)).astype(BF16)
            for h in range(GLA_HEADS):
                ks = slice(h * GLA_DK, (h + 1) * GLA_DK)
                vs = slice(h * GLA_DV, (h + 1) * GLA_DV)
                att = lax.dot_general(qe[:, ks], ke[:, ks], nt, preferred_element_type=F32)
                att = jnp.where(mask, att, 0.0).astype(BF16)
                vh = v_ref[r0:r0 + c, vs]
                s_old = st_ref[d, h]
                o = jnp.dot(att, vh, preferred_element_type=F32)
                o = o + lax.dot_general(qe[:, ks], s_old.astype(BF16), nt, preferred_element_type=F32)
                o_ref[r0:r0 + c, vs] = o
                upd = lax.dot_general(vh, kd[:, ks], tn, preferred_element_type=F32)
                st_ref[d, h] = s_old * dec[:, ks] + upd


def _gla_mixer(zmain, a_aux, w_a, b_a, n_ctx):
    n = zmain.shape[0]
    tb = 256
    assert n % tb == 0 and n_ctx % tb == 0
    nb, ncb = n // tb, n_ctx // tb

    def bwd(s):
        return jnp.where(s < ncb, ncb - 1 - s, nb + ncb - 1 - s)
    wa = jnp.pad(w_a, ((0, 0), (0, 128 - GLA_RANK), (0, 0)))
    in_specs = []
    for order in (lambda s: s, bwd):
        in_specs += [pl.BlockSpec((tb, W_GLA_K), lambda s, o=order: (o(s), OQ // W_GLA_K)),
                     pl.BlockSpec((tb, W_GLA_K), lambda s, o=order: (o(s), OK_ // W_GLA_K)),
                     pl.BlockSpec((tb, W_GLA_V), lambda s, o=order: (o(s), OV // W_GLA_V)),
                     pl.BlockSpec((tb, 128), lambda s, o=order: (o(s), 0))]
    in_specs += [pl.BlockSpec((2, 128, W_GLA_K), lambda s: (0, 0, 0)),
                 pl.BlockSpec((2, 1, W_GLA_K), lambda s: (0, 0, 0))]
    return pl.pallas_call(
        functools.partial(_gla_kernel, tb=tb), grid=(nb,),
        in_specs=in_specs,
        out_specs=[pl.BlockSpec((tb, W_GLA_V), lambda s: (s, 0)),
                   pl.BlockSpec((tb, W_GLA_V), lambda s: (bwd(s), 0))],
        out_shape=[jax.ShapeDtypeStruct((n, W_GLA_V), F32)] * 2,
        scratch_shapes=[pltpu.VMEM((2, GLA_HEADS, GLA_DV, GLA_DK), F32)],
        compiler_params=_params(("arbitrary",), 32),
        name="gla_chunked",
    )(zmain, zmain, zmain, a_aux, zmain, zmain, zmain, a_aux, wa, b_a.reshape(2, 1, W_GLA_K))


_ATT_PAD = 2 * ATT_HEAD_DIM
_LOG2E = 1.4426950408889634
_FIXED_SHIFT_MAX_BOUND = 40.0


def _rope_fn(row0, tm, n_ctx):
    hd = ATT_HEAD_DIM
    row = row0 + lax.broadcasted_iota(jnp.int32, (tm, hd), 0)
    lane = lax.broadcasted_iota(jnp.int32, (tm, hd), 1)
    quarter = hd // 4
    first = (lane & quarter) == 0
    assert tm % GRID_W == 0 and n_ctx % GRID_W == 0
    reps = tm // GRID_W
    lane_s = lax.broadcasted_iota(jnp.int32, (GRID_W, hd), 1)
    freq = jnp.exp((lane_s % quarter).astype(F32) * (-math.log(ROPE_THETA) / quarter))
    ang_col = lax.broadcasted_iota(jnp.int32, (GRID_W, hd), 0).astype(F32) * freq
    grid_row0 = (row0 - n_ctx) // GRID_W
    ang_row = (grid_row0 + lax.broadcasted_iota(jnp.int32, (8, hd), 0)).astype(F32) * freq[0:8, :]
    cos_c, sin_c = jnp.cos(ang_col), jnp.sin(ang_col)
    cos_r, sin_r = jnp.cos(ang_row), jnp.sin(ang_row)

    def expand(tab_r, tab_c):
        by_row = jnp.concatenate([jnp.broadcast_to(tab_r[k:k + 1, :], (GRID_W, hd)) for k in range(reps)], axis=0)
        return jnp.where(lane < hd // 2, by_row, jnp.concatenate([tab_c] * reps, axis=0))
    cosv = expand(cos_r, cos_c)
    sinv = expand(sin_r, sin_c)
    sin_s = jnp.where(first, -sinv, sinv)

    def rope(x):
        sw = jnp.where(first, pltpu.roll(x, hd - quarter, 1), pltpu.roll(x, quarter, 1))
        return x * cosv + sw * sin_s
    return rope, row >= n_ctx


def _k_prep_kernel(k_ref, v_ref, kw_ref, kx_ref, vx_ref, kmax_ref, *, n_ctx, tm):
    hd = ATT_HEAD_DIM
    rope, is_lat = _rope_fn(pl.program_id(0) * tm, tm, n_ctx)
    lane = lax.broadcasted_iota(jnp.int32, (tm, hd), 1)
    one_col = jnp.where(lane == 0, 1.0, 0.0).astype(BF16)
    ones = jnp.ones((tm, hd), BF16)
    nmax = jnp.zeros((tm, 1), F32)
    for h in range(ATT_KV_HEADS):
        sl = slice(h * hd, (h + 1) * hd)
        xn = _rms(k_ref[:, sl].astype(F32), kw_ref[...])
        kr = jnp.where(is_lat, rope(xn), xn).astype(BF16)
        kx_ref[:, h * _ATT_PAD:h * _ATT_PAD + hd] = kr
        kx_ref[:, h * _ATT_PAD + hd:(h + 1) * _ATT_PAD] = one_col
        vx_ref[:, h * _ATT_PAD:h * _ATT_PAD + hd] = v_ref[:, sl]
        vx_ref[:, h * _ATT_PAD + hd:(h + 1) * _ATT_PAD] = ones
        krf = kr.astype(F32)
        nmax = jnp.maximum(nmax, jnp.sum(krf * krf, axis=-1, keepdims=True))
    kmax_ref[...] = jnp.broadcast_to(jnp.max(nmax, axis=0, keepdims=True), kmax_ref.shape)


def _q_prep_kernel(q_ref, qw_ref, kn_ref, qx_ref, bmax_ref, *, n_ctx, tm):
    hd = ATT_HEAD_DIM
    rope, _ = _rope_fn(n_ctx + pl.program_id(0) * tm, tm, n_ctx)
    lane = lax.broadcasted_iota(jnp.int32, (tm, hd), 1)
    scale = hd ** -0.5 * _LOG2E
    bmax = jnp.zeros((tm, 1), F32)
    for h in range(ATT_HEADS):
        sl = slice(h * hd, (h + 1) * hd)
        xn = _rms(q_ref[:, sl].astype(F32), qw_ref[...])
        qr = (rope(xn) * scale).astype(BF16)
        qf = qr.astype(F32)
        bound = jnp.sqrt(jnp.sum(qf * qf, axis=-1, keepdims=True)) * kn_ref[...]
        qx_ref[:, h * _ATT_PAD:h * _ATT_PAD + hd] = qr
        qx_ref[:, h * _ATT_PAD + hd:(h + 1) * _ATT_PAD] = jnp.where(lane == 0, -bound, 0.0).astype(BF16)
        bmax = jnp.maximum(bmax, bound)
    bmax_ref[...] = jnp.broadcast_to(jnp.max(bmax, axis=0, keepdims=True), bmax_ref.shape)


def _qk_prep(zmain, q_norm, k_norm, n_ctx):
    n = zmain.shape[0]
    lq = n - n_ctx
    tm = 256
    hd = ATT_HEAD_DIM
    assert n_ctx % tm == 0
    ncb = n_ctx // tm
    kx, vx, kmax = pl.pallas_call(
        functools.partial(_k_prep_kernel, n_ctx=n_ctx, tm=tm), grid=(n // tm,),
        in_specs=[pl.BlockSpec((tm, W_ATT_KV), lambda i: (i, OAK // W_ATT_KV)),
                  pl.BlockSpec((tm, W_ATT_KV), lambda i: (i, OAV // W_ATT_KV)),
                  pl.BlockSpec((1, hd), lambda i: (0, 0))],
        out_specs=[pl.BlockSpec((tm, ATT_KV_HEADS * _ATT_PAD), lambda i: (i, 0)),
                   pl.BlockSpec((tm, ATT_KV_HEADS * _ATT_PAD), lambda i: (i, 0)),
                   pl.BlockSpec((None, 8, 128), lambda i: (i, 0, 0))],
        out_shape=[jax.ShapeDtypeStruct((n, ATT_KV_HEADS * _ATT_PAD), BF16),
                   jax.ShapeDtypeStruct((n, ATT_KV_HEADS * _ATT_PAD), BF16),
                   jax.ShapeDtypeStruct((n // tm, 8, 128), F32)],
        compiler_params=_params(("parallel",), 32),
        name="k_norm_rope",
    )(zmain, zmain, k_norm.reshape(1, hd))
    knorm = jnp.sqrt(jnp.max(kmax)).reshape(1, 1)
    qx, bmax = pl.pallas_call(
        functools.partial(_q_prep_kernel, n_ctx=n_ctx, tm=tm), grid=(lq // tm,),
        in_specs=[pl.BlockSpec((tm, W_ATT), lambda i: (i + ncb, OAQ // W_ATT)),
                  pl.BlockSpec((1, hd), lambda i: (0, 0)),
                  pl.BlockSpec((1, 1), lambda i: (0, 0))],
        out_specs=[pl.BlockSpec((tm, ATT_HEADS * _ATT_PAD), lambda i: (i, 0)),
                   pl.BlockSpec((None, 8, 128), lambda i: (i, 0, 0))],
        out_shape=[jax.ShapeDtypeStruct((lq, ATT_HEADS * _ATT_PAD), BF16),
                   jax.ShapeDtypeStruct((lq // tm, 8, 128), F32)],
        compiler_params=_params(("parallel",), 32),
        name="q_norm_rope",
    )(zmain, q_norm.reshape(1, hd), knorm)
    return qx, kx, vx, jnp.max(bmax)


def _flash_fixed_kernel(q_ref, kt_ref, v_ref, o_ref, acc_ref):
    c = pl.program_id(2)
    hd = ATT_HEAD_DIM

    @pl.when(c == 0)
    def _():
        acc_ref[...] = jnp.zeros_like(acc_ref)
    kt = kt_ref[...]
    v = v_ref[...]
    for g in range(ATT_GROUP):
        s = jnp.dot(q_ref[:, g * _ATT_PAD:(g + 1) * _ATT_PAD], kt, preferred_element_type=F32)
        acc_ref[g] += jnp.dot(jnp.exp2(s).astype(BF16), v, preferred_element_type=F32)

    @pl.when(c == pl.num_programs(2) - 1)
    def _():
        for g in range(ATT_GROUP):
            a = acc_ref[g]
            o_ref[:, g * hd:(g + 1) * hd] = (a[:, 0:hd] / a[:, hd:2 * hd]).astype(o_ref.dtype)


def _flash_online_kernel(q_ref, kt_ref, v_ref, o_ref, m_ref, acc_ref):
    c = pl.program_id(2)
    hd = ATT_HEAD_DIM

    @pl.when(c == 0)
    def _():
        m_ref[...] = jnp.full(m_ref.shape, -jnp.inf, F32)
        acc_ref[...] = jnp.zeros_like(acc_ref)
    kt = kt_ref[...]
    v = v_ref[...]
    for g in range(ATT_GROUP):
        s = jnp.dot(q_ref[:, g * _ATT_PAD:(g + 1) * _ATT_PAD], kt, preferred_element_type=F32)
        m_prev = m_ref[g]
        m_new = jnp.maximum(m_prev, jnp.max(s, axis=-1, keepdims=True))
        p = jnp.exp2(s - m_new).astype(BF16)
        acc_ref[g] = jnp.exp2(m_prev - m_new) * acc_ref[g] + jnp.dot(p, v, preferred_element_type=F32)
        m_ref[g] = m_new

    @pl.when(c == pl.num_programs(2) - 1)
    def _():
        for g in range(ATT_GROUP):
            a = acc_ref[g]
            o_ref[:, g * hd:(g + 1) * hd] = (a[:, 0:hd] / a[:, hd:2 * hd]).astype(o_ref.dtype)


def _attention(qx, kx, vx, bound_log2):
    lq = qx.shape[0]
    n = kx.shape[0]
    kt = kx.T
    gw = ATT_GROUP * _ATT_PAD
    ow = ATT_GROUP * ATT_HEAD_DIM
    def call(kern, tq, tk, scratch, name):
        return pl.pallas_call(
            kern, grid=(ATT_KV_HEADS, lq // tq, n // tk),
            in_specs=[pl.BlockSpec((tq, gw), lambda h, i, j: (i, h)),
                      pl.BlockSpec((_ATT_PAD, tk), lambda h, i, j: (h, j)),
                      pl.BlockSpec((tk, _ATT_PAD), lambda h, i, j: (j, h))],
            out_specs=pl.BlockSpec((tq, ow), lambda h, i, j: (i, h)),
            out_shape=jax.ShapeDtypeStruct((lq, W_ATT), BF16),
            scratch_shapes=scratch(tq) + [pltpu.VMEM((ATT_GROUP, tq, _ATT_PAD), F32)],
            compiler_params=_params(("parallel", "parallel", "arbitrary"), 56),
            name=name,
        )(qx, kt, vx)

    def fixed(_):
        return call(_flash_fixed_kernel, _pick(lq, (1024, 512, 256, 128)), _pick(n, (3328, 1280, 512, 256)),
                    lambda tq: [], "gqa_flash_fixed_shift")

    def online(_):
        return call(_flash_online_kernel, _pick(lq, (512, 256, 128)), _pick(n, (640, 512, 256, 128)),
                    lambda tq: [pltpu.VMEM((ATT_GROUP, tq, 1), F32)], "gqa_flash_online")
    return lax.cond(bound_log2 <= _FIXED_SHIFT_MAX_BOUND * _LOG2E, fixed, online, None)


def _odd_out_kernel(of_ref, ob_ref, r_ref, att_ref, gn_ref, wo_ref, h_ref, nw_ref, mod_ref, o_ref):
    o = of_ref[...] + ob_ref[...]
    r = r_ref[...].astype(F32)
    silu_r = r * _sigmoid(r)
    parts = []
    for h in range(GLA_HEADS):
        vs = slice(h * GLA_DV, (h + 1) * GLA_DV)
        parts.append((_rms(o[:, vs], gn_ref[:, vs]) * silu_r[:, vs]).astype(BF16))
    gla = jnp.concatenate(parts, axis=1)
    mix = jnp.dot(gla, wo_ref[0:W_GLA_V, :], preferred_element_type=F32)
    mix = mix + jnp.dot(att_ref[...], wo_ref[W_GLA_V:W_GLA_V + W_ATT, :], preferred_element_type=F32)
    o_ref[...] = h_ref[...] + mod_ref[4:5, :] * _rms(mix, nw_ref[...])


def _odd_out(o_f, o_b, zmain, att, gla_norm, w_out, h_all, nw, mod, n_ctx):
    n, d = h_all.shape
    lq = n - n_ctx
    tm = _pick(lq, (256, 128))
    assert n_ctx % tm == 0
    off = n_ctx // tm
    return pl.pallas_call(
        _odd_out_kernel, grid=(lq // tm,),
        in_specs=[pl.BlockSpec((tm, W_GLA_V), lambda i: (i + off, 0)),
                  pl.BlockSpec((tm, W_GLA_V), lambda i: (i + off, 0)),
                  pl.BlockSpec((tm, W_GLA_V), lambda i: (i + off, OR // W_GLA_V)),
                  pl.BlockSpec((tm, W_ATT), lambda i: (i, 0)),
                  pl.BlockSpec((1, W_GLA_V), lambda i: (0, 0)),
                  pl.BlockSpec((d, d), lambda i: (0, 0)),
                  pl.BlockSpec((tm, d), lambda i: (i + off, 0)),
                  pl.BlockSpec((1, d), lambda i: (0, 0)),
                  pl.BlockSpec((8, d), lambda i: (0, 0))],
        out_specs=pl.BlockSpec((tm, d), lambda i: (i, 0)),
        out_shape=jax.ShapeDtypeStruct((lq, d), F32),
        compiler_params=_params(("parallel",), 48),
        name="odd_mixer_out",
    )(o_f, o_b, zmain, att, gla_norm.reshape(1, W_GLA_V), w_out, h_all, nw.reshape(1, d), mod)


def _router_kernel(h_ref, nw_ref, mod_ref, wr_ref, br_ref, v_ref, r_ref):
    x = h_ref[...]
    v = _rms(x, nw_ref[...]) * (1.0 + mod_ref[0:1, :]) + mod_ref[1:2, :]
    v_ref[...] = v
    logits = _dot_hi(v, wr_ref[...]) + br_ref[...]
    lane = lax.broadcasted_iota(jnp.int32, logits.shape, 1)
    m1 = jnp.max(logits, axis=-1, keepdims=True)
    i1 = jnp.min(jnp.where(logits == m1, lane, 128), axis=-1, keepdims=True)
    rest = jnp.where(lane == i1, -jnp.inf, logits)
    m2 = jnp.max(rest, axis=-1, keepdims=True)
    i2 = jnp.min(jnp.where(rest == m2, lane, 128), axis=-1, keepdims=True)
    e2 = jnp.exp(m2 - m1)
    g1 = 1.0 / (1.0 + e2)
    g2 = e2 / (1.0 + e2)
    out = jnp.where(lane == 0, i1.astype(F32), 0.0)
    out = jnp.where(lane == 1, i2.astype(F32), out)
    out = jnp.where(lane == 2, g1, out)
    out = jnp.where(lane == 3, g2, out)
    r_ref[...] = out


def _router(h_lat, nw, mod, w_router, b_router):
    lq, d = h_lat.shape
    tm = _pick(lq, (512, 256, 128))
    wr = jnp.pad(w_router, ((0, 0), (0, 128 - N_EXPERTS)))
    br = jnp.pad(b_router, (0, 128 - N_EXPERTS), constant_values=-1e30).reshape(1, 128)
    return pl.pallas_call(
        _router_kernel, grid=(lq // tm,),
        in_specs=[pl.BlockSpec((tm, d), lambda i: (i, 0)),
                  pl.BlockSpec((1, d), lambda i: (0, 0)),
                  pl.BlockSpec((8, d), lambda i: (0, 0)),
                  pl.BlockSpec((d, 128), lambda i: (0, 0)),
                  pl.BlockSpec((1, 128), lambda i: (0, 0))],
        out_specs=[pl.BlockSpec((tm, d), lambda i: (i, 0)), pl.BlockSpec((tm, 128), lambda i: (i, 0))],
        out_shape=[jax.ShapeDtypeStruct((lq, d), F32), jax.ShapeDtypeStruct((lq, 128), F32)],
        compiler_params=_params(("parallel",), 40),
        name="router_top2",
    )(h_lat, nw.reshape(1, d), mod, wr, br)


def _expert_kernel(te_ref, tv_ref, src_ref, dst_ref, v_hbm, gt_ref, wg_ref, wu_ref, wd_ref, y_hbm,
                   xg_ref, xb_ref, ys_ref, acc_ref, gsem, ssem, *, tm, nt, nf):
    j = pl.program_id(0)
    f = pl.program_id(1)
    slot = j % 2
    per_step = tm // nf
    lo = f * per_step
    valid = tv_ref[j] != 0

    def start_gather(tile, slot_, row):
        tok = src_ref[tile * tm + row]
        pltpu.make_async_copy(v_hbm.at[pl.ds(tok, 1)], xg_ref.at[slot_, pl.ds(row, 1)], gsem.at[slot_]).start()

    def start_scatter(row):
        dst = dst_ref[j * tm + row]
        pltpu.make_async_copy(ys_ref.at[1 - slot, pl.ds(row, 1)], y_hbm.at[pl.ds(dst, 1)],
                              ssem.at[1 - slot]).start()

    def wait_scatter(slot_):
        pltpu.make_async_copy(ys_ref.at[slot_], y_hbm.at[pl.ds(0, tm)], ssem.at[slot_]).wait()

    def gather_loop(tile, slot_, first, count):
        def body(r, carry):
            start_gather(tile, slot_, first + r)
            return carry
        lax.fori_loop(0, count, body, 0, unroll=8)

    @pl.when((j == 0) & (f == 0))
    def _():
        gather_loop(0, 0, 0, tm)
        ys_ref[1] = jnp.zeros((tm, ys_ref.shape[2]), F32)

    @pl.when(f == 0)
    def _():
        pltpu.make_async_copy(v_hbm.at[pl.ds(0, tm)], xg_ref.at[slot], gsem.at[slot]).wait()
        xb_ref[...] = xg_ref[slot].astype(BF16)
        acc_ref[...] = jnp.zeros_like(acc_ref)
        for row in range(nf * per_step, tm):
            start_scatter(row)

        @pl.when(j + 1 < nt)
        def _():
            for row in range(nf * per_step, tm):
                start_gather(j + 1, 1 - slot, row)

    def compute(rows_used):
        half = per_step // 2

        def gathers(first, last, after):
            for r in range(first, last):
                start_gather(j + 1, 1 - slot, lo + r + after)

        def scatters(first, last, after):
            for r in range(first, last):
                start_scatter(lo + r + after)

        def zero_after(val):
            bits = pltpu.bitcast(val[0:8, 0:128], jnp.int32)
            return ((bits & 1) >> 1)[0, 0]

        gathers(0, half, 0)
        x = xb_ref[0:rows_used, :]
        a = jnp.dot(x, wg_ref[...], preferred_element_type=F32)
        gathers(half, per_step, zero_after(a))
        b = jnp.dot(x, wu_ref[...], preferred_element_type=F32)
        scatters(0, half, zero_after(b))
        act = (a * _sigmoid(a) * b).astype(BF16)
        down = jnp.dot(act, wd_ref[...], preferred_element_type=F32)
        scatters(half, per_step, zero_after(down))
        acc_ref[0:rows_used, :] += down

    pl.when(tv_ref[j] == 1)(functools.partial(compute, tm))
    pl.when(tv_ref[j] == 2)(functools.partial(compute, tm // 2))

    @pl.when(jnp.logical_not(valid))
    def _():
        @pl.when(j + 1 < nt)
        def _():
            gather_loop(j + 1, 1 - slot, lo, per_step)

        def body(r, carry):
            start_scatter(lo + r)
            return carry
        lax.fori_loop(0, per_step, body, 0, unroll=8)

    @pl.when(f == nf - 1)
    def _():
        @pl.when(j > 0)
        def _():
            wait_scatter(slot)
        ys_ref[slot] = acc_ref[...] * gt_ref[...]

        @pl.when(j == nt - 1)
        def _():
            wait_scatter(1 - slot)


def _combine_kernel(y0_ref, y1_ref, h_ref, nw_ref, mod_ref, o_ref):
    o_ref[...] = h_ref[...] + mod_ref[4:5, :] * _rms(y0_ref[...] + y1_ref[...], nw_ref[...])


def _count_le(sorted_vals, queries):
    return jnp.sum((sorted_vals[None, :] <= queries[:, None]).astype(jnp.int32), axis=1)


def _moe(h_lat, nw2, nw3, mod, w_router, b_router, wg, wu, wd):
    lq, d = h_lat.shape
    e = N_EXPERTS
    dff = wg.shape[2]
    tm = _pick(lq, (512, 256))
    tf = 1024
    v, route = _router(h_lat, nw2, mod, w_router, b_router)
    idx = route[:, 0:2].astype(jnp.int32)
    gates = route[:, 2:4]

    npairs = 2 * lq
    nt = npairs // tm + e + 1
    flat_e = idx.reshape(-1)
    pair_id = jnp.arange(npairs, dtype=jnp.int32)
    onehot = (flat_e[:, None] == jnp.arange(e, dtype=jnp.int32)[None, :]).astype(jnp.int32)
    csum = jnp.cumsum(onehot, axis=0)
    rank = jnp.sum(csum * onehot, axis=1) - 1
    counts = csum[-1]
    padded = ((counts + tm - 1) // tm) * tm
    ends = jnp.cumsum(padded)
    starts = ends - padded
    pos = jnp.sum(onehot * starts[None, :], axis=1) + rank
    pair = jnp.stack([pair_id, lax.bitcast_convert_type(gates.reshape(-1), jnp.int32)], axis=1)
    empty = jnp.broadcast_to(jnp.array([-1, 0], jnp.int32), (nt * tm, 2))
    slots = empty.at[pos].set(pair)
    sorted_pair = slots[:, 0]
    sorted_gate = lax.bitcast_convert_type(slots[:, 1], F32)
    is_pad = sorted_pair < 0
    src_row = jnp.where(is_pad, 0, sorted_pair >> 1).astype(jnp.int32)
    pad_rank = jnp.cumsum(is_pad.astype(jnp.int32)) - 1
    dst_row = jnp.where(is_pad, npairs + pad_rank, (sorted_pair & 1) * lq + (sorted_pair >> 1)).astype(jnp.int32)
    dst_row = jnp.concatenate([(nt - 1) * tm + jnp.arange(tm, dtype=jnp.int32), dst_row])
    n_used = ends[-1] // tm
    tile_ids = jnp.arange(nt, dtype=jnp.int32)
    tile_exp = jnp.minimum(_count_le(ends, jnp.minimum(tile_ids, n_used - 1) * tm), e - 1)
    onehot_t = (tile_exp[:, None] == jnp.arange(e, dtype=jnp.int32)[None, :]).astype(jnp.int32)
    real_end = jnp.sum(onehot_t * (starts + counts)[None, :], axis=1)
    tile_rows = jnp.clip(real_end - tile_ids * tm, 0, tm)
    tile_valid = jnp.where(tile_ids < n_used, jnp.where(tile_rows <= tm // 2, 2, 1), 0).astype(jnp.int32)

    nf = dff // tf
    pinned = lambda j, f, tv: jnp.where(tv[j] != 0, f, nf - 1)
    y = pl.pallas_call(
        functools.partial(_expert_kernel, tm=tm, nt=nt, nf=nf),
        grid_spec=pltpu.PrefetchScalarGridSpec(
            num_scalar_prefetch=4, grid=(nt, nf),
            in_specs=[pl.BlockSpec(memory_space=pl.ANY),
                      pl.BlockSpec((tm, 1), lambda j, f, te, tv, src, dst: (j, 0)),
                      pl.BlockSpec((None, d, tf), lambda j, f, te, tv, src, dst: (te[j], 0, pinned(j, f, tv))),
                      pl.BlockSpec((None, d, tf), lambda j, f, te, tv, src, dst: (te[j], 0, pinned(j, f, tv))),
                      pl.BlockSpec((None, tf, d), lambda j, f, te, tv, src, dst: (te[j], pinned(j, f, tv), 0))],
            out_specs=pl.BlockSpec(memory_space=pl.ANY),
            scratch_shapes=[pltpu.VMEM((2, tm, d), F32), pltpu.VMEM((tm, d), BF16), pltpu.VMEM((2, tm, d), F32),
                            pltpu.VMEM((tm, d), F32), pltpu.SemaphoreType.DMA((2,)),
                            pltpu.SemaphoreType.DMA((2,))]),
        out_shape=jax.ShapeDtypeStruct((nt * tm, d), F32),
        compiler_params=pltpu.CompilerParams(dimension_semantics=("arbitrary", "arbitrary"),
                                             vmem_limit_bytes=56 * MIB, disable_bounds_checks=True),
        name="moe_experts",
    )(tile_exp, tile_valid, src_row, dst_row, v, sorted_gate.reshape(nt * tm, 1), wg, wu, wd)

    tc = _pick(lq, (256, 128))
    return pl.pallas_call(
        _combine_kernel, grid=(lq // tc,),
        in_specs=[pl.BlockSpec((tc, d), lambda i: (i, 0)),
                  pl.BlockSpec((tc, d), lambda i: (i + lq // tc, 0)),
                  pl.BlockSpec((tc, d), lambda i: (i, 0)),
                  pl.BlockSpec((1, d), lambda i: (0, 0)),
                  pl.BlockSpec((8, d), lambda i: (0, 0))],
        out_specs=pl.BlockSpec((tc, d), lambda i: (i, 0)),
        out_shape=jax.ShapeDtypeStruct((lq, d), F32),
        compiler_params=_params(("parallel",), 32),
        name="moe_combine",
    )(y, y, h_lat, nw3.reshape(1, d), mod)


def kernel(x, c, ctx, c_ctx, w_mod, b_mod, norms, e_w_in, e_pool_w, e_pool_scale, e_s5_lam_re, e_s5_lam_im, e_s5_log_dt, e_s5_b_re, e_s5_b_im, e_s5_c_re, e_s5_c_im, e_s5_d, e_s5_w_glu, e_w_out, e_ffn_gate, e_ffn_up, e_ffn_down, o_w_in, o_gla_w_a, o_gla_b_a, o_gla_norm, o_q_norm, o_k_norm, o_w_out, o_router, o_router_b, o_moe_gate, o_moe_up, o_moe_down):
    assert x.shape[0] == 1 and w_mod.shape[0] == 2 and e_w_in.shape[0] == 1 and o_w_in.shape[0] == 1
    d = x.shape[2]
    n_ctx = ctx.shape[1]
    h = jnp.concatenate([ctx[0], x[0]], axis=0)

    vecs = jnp.zeros((8, d), F32).at[0].set(c[0]).at[1].set(c_ctx)
    mods = _modulation(vecs, w_mod, b_mod)

    mod1, mod2 = _mod_rows(mods[0], 0), _mod_rows(mods[0], 1)
    z = _normed_matmul(h, norms[0, 0], mod1, e_w_in[0].astype(BF16), n_ctx)
    pool_out = _pool_mixer(z, e_pool_w[0].astype(BF16), e_pool_scale[0], n_ctx)
    m_op, qre, qim, pre, pim, at = _s5_params(e_s5_lam_re[0], e_s5_lam_im[0], e_s5_log_dt[0],
                                              e_s5_b_re[0], e_s5_b_im[0], e_s5_c_re[0], e_s5_c_im[0])
    y_s5 = _s5_mixer(z[:, W_POOL:], m_op, qre, qim, pre, pim, at, n_ctx)
    h = _even_out(pool_out, y_s5, z, e_s5_d[0], e_s5_w_glu[0].astype(BF16), e_w_out[0].astype(BF16),
                  h, norms[0, 1], mod1, n_ctx)
    h = _dense_ffn(h, norms[0, 2], norms[0, 3], mod2, e_ffn_gate[0].astype(BF16),
                   e_ffn_up[0].astype(BF16), e_ffn_down[0].astype(BF16), n_ctx)

    mod1, mod2 = _mod_rows(mods[1], 0), _mod_rows(mods[1], 1)
    w_in = o_w_in[0]
    a0 = 2 * W_GLA_K + W_GLA_V
    w_main = jnp.concatenate([w_in[:, :a0], w_in[:, a0 + GLA_RANK:]], axis=1).astype(BF16)
    w_aux = jnp.pad(w_in[:, a0:a0 + GLA_RANK], ((0, 0), (0, 128 - GLA_RANK))).astype(BF16)
    zmain, a_aux = _normed_matmul(h, norms[1, 0], mod1, w_main, n_ctx, w_aux=w_aux)
    o_f, o_b = _gla_mixer(zmain, a_aux, o_gla_w_a[0], o_gla_b_a[0], n_ctx)
    qx, kx, vx, bound_log2 = _qk_prep(zmain, o_q_norm[0], o_k_norm[0], n_ctx)
    att = _attention(qx, kx, vx, bound_log2)
    h_lat = _odd_out(o_f, o_b, zmain, att, o_gla_norm[0], o_w_out[0].astype(BF16), h, norms[1, 1], mod1, n_ctx)
    out = _moe(h_lat, norms[1, 2], norms[1, 3], mod2, o_router[0], o_router_b[0],
               o_moe_gate[0].astype(BF16), o_moe_up[0].astype(BF16), o_moe_down[0].astype(BF16))
    return out[None]
```

```python
import functools
import math

import jax
import jax.numpy as jnp
from jax import lax
from jax.experimental import pallas as pl
from jax.experimental.pallas import tpu as pltpu

F32 = jnp.float32
BF16 = jnp.bfloat16
HI = lax.Precision.HIGHEST
EPS = 1e-6

D_MODEL = 2048
GRID_W = 64
N_MOD = 6

POOL_WINDOWS = (2, 4, 8, 16)
POOL_GROUP = 384
W_POOL = 1536
W_S5 = 512
S5_CH = 16
S5_STATE = 64
S5_GROUPS = 32
S5_T = 32

GLA_HEADS = 4
GLA_DK = 128
GLA_DV = 256
GLA_RANK = 16
GLA_TAU = 16.0
GLA_CHUNK = 64
W_GLA_K = 512
W_GLA_V = 1024
ATT_HEAD_DIM = 128
ATT_HEADS = 8
ATT_KV_HEADS = 2
ATT_GROUP = 4
W_ATT = 1024
W_ATT_KV = 256
ROPE_THETA = 10000.0

D_FF = 7168
N_EXPERTS = 8

OQ, OK_, OV, OR, OAQ, OAK, OAV = 0, 512, 1024, 2048, 3072, 4096, 4352
W_ODD_MAIN = 4608

MIB = 2 ** 20


def _params(sem, vmem_mib):
    return pltpu.CompilerParams(dimension_semantics=sem, vmem_limit_bytes=vmem_mib * MIB)


def _pick(n, cands):
    for c in cands:
        if n % c == 0:
            return c
    raise ValueError(f"no tile for {n} in {cands}")


def _sigmoid(x):
    return 1.0 / (1.0 + jnp.exp(-x))


def _rms(x, w):
    return x * lax.rsqrt(jnp.mean(x * x, axis=-1, keepdims=True) + EPS) * w


_NORM_CHUNK = 16


def _store_norm_mod(dst_ref, h_ref, nw_ref, mod_ref, row0, n_ctx):
    tm = h_ref.shape[0]
    nw = nw_ref[...]
    w_lat = nw * (1.0 + mod_ref[0:1, :])
    sh_lat = mod_ref[1:2, :]

    def rows(c):
        r0 = pl.multiple_of(c * _NORM_CHUNK, _NORM_CHUNK)
        x = h_ref[pl.ds(r0, _NORM_CHUNK), :]
        return r0, x * lax.rsqrt(jnp.mean(x * x, axis=-1, keepdims=True) + EPS)

    @pl.when(row0 >= n_ctx)
    def _():
        def body(c, carry):
            r0, xr = rows(c)
            dst_ref[pl.ds(r0, _NORM_CHUNK), :] = (xr * w_lat + sh_lat).astype(dst_ref.dtype)
            return carry
        lax.fori_loop(0, tm // _NORM_CHUNK, body, 0, unroll=4)

    @pl.when(row0 < n_ctx)
    def _():
        w_ctx = nw * (1.0 + mod_ref[2:3, :])
        sh_ctx = mod_ref[3:4, :]

        def body(c, carry):
            r0, xr = rows(c)
            is_ctx = row0 + r0 + lax.broadcasted_iota(jnp.int32, (_NORM_CHUNK, 1), 0) < n_ctx
            u = xr * jnp.where(is_ctx, w_ctx, w_lat) + jnp.where(is_ctx, sh_ctx, sh_lat)
            dst_ref[pl.ds(r0, _NORM_CHUNK), :] = u.astype(dst_ref.dtype)
            return carry
        lax.fori_loop(0, tm // _NORM_CHUNK, body, 0, unroll=4)


def _gate_rows(mod_ref, row0, tm, n_ctx):
    rows = row0 + lax.broadcasted_iota(jnp.int32, (tm, 1), 0)
    return jnp.where(rows < n_ctx, mod_ref[5:6, :], mod_ref[4:5, :])


def _mod_kernel(v_ref, w_ref, b_ref, o_ref):
    v = v_ref[...]
    s = v * _sigmoid(v)
    o_ref[...] = jnp.dot(s, w_ref[...], precision=HI, preferred_element_type=F32) + b_ref[...]


def _modulation(vecs, w_mod, b_mod):
    depth, d, n6 = w_mod.shape
    tn = 1024
    return pl.pallas_call(
        _mod_kernel,
        grid=(depth, n6 // tn),
        in_specs=[pl.BlockSpec((8, d), lambda l, j: (0, 0)),
                  pl.BlockSpec((None, d, tn), lambda l, j: (l, 0, j)),
                  pl.BlockSpec((None, 1, tn), lambda l, j: (l, 0, j))],
        out_specs=pl.BlockSpec((None, 8, tn), lambda l, j: (l, 0, j)),
        out_shape=jax.ShapeDtypeStruct((depth, 8, n6), F32),
        compiler_params=_params(("parallel", "parallel"), 40),
        name="modulation",
    )(vecs, w_mod, b_mod.reshape(depth, 1, n6))


def _mod_rows(m, sub):
    d = m.shape[1] // N_MOD
    m6 = m.reshape(8, N_MOD, d)
    sh, sc, g = m6[:, 3 * sub + 0], m6[:, 3 * sub + 1], m6[:, 3 * sub + 2]
    z = jnp.zeros((d,), F32)
    return jnp.stack([sc[0], sh[0], sc[1], sh[1], g[0], g[1], z, z])


def _nmm_kernel(h_ref, nw_ref, mod_ref, w_ref, o_ref, u_ref, *, n_ctx, tm):
    @pl.when(pl.program_id(1) == 0)
    def _():
        _store_norm_mod(u_ref, h_ref, nw_ref, mod_ref, pl.program_id(0) * tm, n_ctx)
    o_ref[...] = jnp.dot(u_ref[...], w_ref[...], preferred_element_type=F32).astype(o_ref.dtype)


def _nmm_aux_kernel(h_ref, nw_ref, mod_ref, w_ref, wa_ref, o_ref, oa_ref, u_ref, *, n_ctx, tm):
    @pl.when(pl.program_id(1) == 0)
    def _():
        _store_norm_mod(u_ref, h_ref, nw_ref, mod_ref, pl.program_id(0) * tm, n_ctx)
        oa_ref[...] = jnp.dot(u_ref[...], wa_ref[...], preferred_element_type=F32)
    o_ref[...] = jnp.dot(u_ref[...], w_ref[...], preferred_element_type=F32).astype(o_ref.dtype)


def _normed_matmul(h, nw, mod, w, n_ctx, w_aux=None):
    n, d = h.shape
    nout = w.shape[1]
    tm = _pick(n, (640, 512, 256, 128))
    tn = _pick(nout, (1536, 1024, 512, 256, 128))
    in_specs = [pl.BlockSpec((tm, d), lambda i, j: (i, 0)),
                pl.BlockSpec((1, d), lambda i, j: (0, 0)),
                pl.BlockSpec((8, d), lambda i, j: (0, 0)),
                pl.BlockSpec((d, tn), lambda i, j: (0, j))]
    out_specs = pl.BlockSpec((tm, tn), lambda i, j: (i, j))
    out_shape = jax.ShapeDtypeStruct((n, nout), BF16)
    args = [h, nw.reshape(1, d), mod, w]
    if w_aux is None:
        kern = functools.partial(_nmm_kernel, n_ctx=n_ctx, tm=tm)
    else:
        na = w_aux.shape[1]
        kern = functools.partial(_nmm_aux_kernel, n_ctx=n_ctx, tm=tm)
        in_specs.append(pl.BlockSpec((d, na), lambda i, j: (0, 0)))
        out_specs = [out_specs, pl.BlockSpec((tm, na), lambda i, j: (i, 0))]
        out_shape = [out_shape, jax.ShapeDtypeStruct((n, na), F32)]
        args.append(w_aux)
    return pl.pallas_call(
        kern, grid=(n // tm, nout // tn), in_specs=in_specs, out_specs=out_specs, out_shape=out_shape,
        scratch_shapes=[pltpu.VMEM((tm, d), BF16)],
        compiler_params=_params(("parallel", "arbitrary"), 48),
        name="normed_proj",
    )(*args)


_POOL_HALO = 16


def _pool_kernel(z_ref, zp_ref, zn_ref, pw_ref, ps_ref, o_ref, ext_ref, *, n_ctx, n_all, tm):
    i = pl.program_id(0)
    row0 = i * tm
    in_ctx = row0 < n_ctx
    seq_start = jnp.where(in_ctx, 0, n_ctx)
    seq_end = jnp.where(in_ctx, n_ctx, n_all)
    has_prev = row0 > seq_start
    has_next = row0 + tm < seq_end
    hl = _POOL_HALO
    ext_ref[0:hl, :] = jnp.where(has_prev, zp_ref[...].astype(F32), 0.0)
    ext_ref[hl:hl + tm, :] = z_ref[...].astype(F32)
    ext_ref[hl + tm:hl + tm + hl, :] = jnp.where(has_next, zn_ref[...].astype(F32), 0.0)
    t = row0 - seq_start + lax.broadcasted_iota(jnp.int32, (tm, 1), 0)
    seq_len = seq_end - seq_start
    for g, w in enumerate(POOL_WINDOWS):
        c0 = g * POOL_GROUP
        acc = None
        for k in range(-(w // 2), w - w // 2):
            v = ext_ref[hl + k:hl + k + tm, c0:c0 + POOL_GROUP]
            acc = v if acc is None else acc + v
        lo = jnp.maximum(t - w // 2, 0)
        hi = jnp.minimum(t + (w - w // 2), seq_len)
        cnt = (hi - lo).astype(F32)
        dlt = acc / cnt - ext_ref[hl:hl + tm, c0:c0 + POOL_GROUP]
        y = jnp.dot(dlt.astype(BF16), pw_ref[g], preferred_element_type=F32)
        o_ref[:, c0:c0 + POOL_GROUP] = (y * ps_ref[:, c0:c0 + POOL_GROUP]).astype(o_ref.dtype)


def _pool_mixer(z, pool_w, pool_scale, n_ctx):
    n = z.shape[0]
    tm = 256
    assert n_ctx % tm == 0 and n % tm == 0
    hl = _POOL_HALO
    nh = n // hl
    per = tm // hl
    kern = functools.partial(_pool_kernel, n_ctx=n_ctx, n_all=n, tm=tm)
    return pl.pallas_call(
        kern, grid=(n // tm,),
        in_specs=[pl.BlockSpec((tm, W_POOL), lambda i: (i, 0)),
                  pl.BlockSpec((hl, W_POOL), lambda i: (jnp.maximum(i * per - 1, 0), 0)),
                  pl.BlockSpec((hl, W_POOL), lambda i: (jnp.minimum((i + 1) * per, nh - 1), 0)),
                  pl.BlockSpec((4, POOL_GROUP, POOL_GROUP), lambda i: (0, 0, 0)),
                  pl.BlockSpec((1, W_POOL), lambda i: (0, 0))],
        out_specs=pl.BlockSpec((tm, W_POOL), lambda i: (i, 0)),
        out_shape=jax.ShapeDtypeStruct((n, W_POOL), BF16),
        scratch_shapes=[pltpu.VMEM((tm + 2 * hl, W_POOL), F32)],
        compiler_params=_params(("parallel",), 32),
        name="pool_mixer",
    )(z, z, z, pool_w, pool_scale.reshape(1, W_POOL))


def _dot_hi(a, b):
    return jnp.dot(a, b, precision=HI, preferred_element_type=F32)


def _dot_t_hi(a, b):
    return lax.dot_general(a, b, (((0,), (0,)), ((), ())), precision=HI, preferred_element_type=F32)


def _s5_param_kernel(lr_ref, li_ref, lrc_ref, lic_ref, ldt_ref, br_ref, bi_ref, cr_ref, ci_ref,
                     m_ref, qre_ref, qim_ref, pre_ref, pim_ref, at_ref):
    t_len = S5_T
    tp = t_len + 8
    wid = t_len * S5_CH
    col = lax.broadcasted_iota(jnp.int32, (tp, wid), 1)
    kid = lax.broadcasted_iota(jnp.int32, (tp, wid), 0)
    cq = col >> 4
    rep_nat = (kid == cq).astype(F32)
    rep_rev = (kid == t_len - 1 - cq).astype(F32)
    til = (lax.broadcasted_iota(jnp.int32, (S5_CH, wid), 0)
           == (lax.broadcasted_iota(jnp.int32, (S5_CH, wid), 1) & (S5_CH - 1))).astype(F32)
    colb = lax.broadcasted_iota(jnp.int32, (S5_CH, wid), 1) >> 4
    kk = lax.broadcasted_iota(jnp.int32, (tp, S5_STATE), 0).astype(F32)
    rows = [None] * t_len
    for d in range(2):
        dt = jnp.exp(ldt_ref[d])
        lr, li = lr_ref[d], li_ref[d]
        mag = jnp.exp(kk * (lr * dt))
        ang = kk * (li * dt)
        pr, pi_ = mag * jnp.cos(ang), mag * jnp.sin(ang)
        lrc, lic = lrc_ref[d], lic_ref[d]
        magc = jnp.exp(lrc * dt)
        arc, aic = magc * jnp.cos(lic * dt), magc * jnp.sin(lic * dt)
        den = lrc * lrc + lic * lic
        nr, ni = arc - 1.0, aic
        kre = (nr * lrc + ni * lic) / den
        kim = (ni * lrc - nr * lic) / den
        bbr = kre * br_ref[d] - kim * bi_ref[d]
        bbi = kre * bi_ref[d] + kim * br_ref[d]
        repq = rep_rev if d == 0 else rep_nat
        repp = ((kid == cq + 1) if d == 0 else (kid == t_len - cq)).astype(F32)
        repm = rep_nat if d == 0 else rep_rev
        bt_r, bt_i = _dot_hi(bbr, til), _dot_hi(bbi, til)
        eqr, eqi = _dot_t_hi(pr, repq), _dot_t_hi(pi_, repq)
        qre_ref[d] = (eqr * bt_r - eqi * bt_i).astype(qre_ref.dtype)
        qim_ref[d] = (eqr * bt_i + eqi * bt_r).astype(qim_ref.dtype)
        epr, epi = _dot_t_hi(pr, repp), _dot_t_hi(pi_, repp)
        ct_r, ct_i = _dot_t_hi(cr_ref[d], til), _dot_t_hi(ci_ref[d], til)
        pre_ref[d] = (ct_r * epr - ct_i * epi).astype(pre_ref.dtype)
        pim_ref[d] = (-(ct_r * epi + ct_i * epr)).astype(pim_ref.dtype)
        at_ref[d, 0:1, :] = pr[t_len:t_len + 1, :]
        at_ref[d, 1:2, :] = pi_[t_len:t_len + 1, :]
        emr, emi = _dot_t_hi(pr, repm), _dot_t_hi(pi_, repm)
        y_re = emr * ct_r - emi * ct_i
        y_im = emr * ct_i + emi * ct_r
        r = _dot_t_hi(bbr, y_re) - _dot_t_hi(bbi, y_im)
        for s in range(t_len):
            if d == 0:
                blk = jnp.where(colb >= s, pltpu.roll(r, S5_CH * s, 1), 0.0)
            else:
                blk = jnp.where(colb <= s, pltpu.roll(r, (S5_CH * (s + 1)) % wid, 1), 0.0)
            rows[s] = blk if rows[s] is None else rows[s] + blk
    for s in range(t_len):
        m_ref[s * S5_CH:(s + 1) * S5_CH, :] = rows[s].astype(m_ref.dtype)


def _s5_params(lam_re, lam_im, log_dt, b_re, b_im, c_re, c_im):
    g, p, n = S5_GROUPS, S5_STATE, S5_CH
    wid = S5_T * S5_CH

    def spec(*shape):
        return pl.BlockSpec((2, None) + shape, lambda gi: (0, gi) + (0,) * len(shape))

    def ospec(*shape):
        return pl.BlockSpec((None,) + shape, lambda gi: (gi,) + (0,) * len(shape))
    return pl.pallas_call(
        _s5_param_kernel, grid=(g,),
        in_specs=[spec(1, p), spec(1, p), spec(p, 1), spec(p, 1), spec(1, 1),
                  spec(p, n), spec(p, n), spec(n, p), spec(n, p)],
        out_specs=[ospec(wid, wid), ospec(2, p, wid), ospec(2, p, wid), ospec(2, p, wid), ospec(2, p, wid),
                   ospec(2, 2, p)],
        out_shape=[jax.ShapeDtypeStruct((g, wid, wid), BF16),
                   jax.ShapeDtypeStruct((g, 2, p, wid), BF16),
                   jax.ShapeDtypeStruct((g, 2, p, wid), BF16),
                   jax.ShapeDtypeStruct((g, 2, p, wid), BF16),
                   jax.ShapeDtypeStruct((g, 2, p, wid), BF16),
                   jax.ShapeDtypeStruct((g, 2, 2, p), F32)],
        compiler_params=_params(("parallel",), 32),
        name="s5_params",
    )(lam_re.reshape(2, g, 1, p), lam_im.reshape(2, g, 1, p), lam_re.reshape(2, g, p, 1),
      lam_im.reshape(2, g, p, 1), log_dt.reshape(2, g, 1, 1), b_re, b_im, c_re, c_im)


def _s5_state_kernel(u_ref, qre_ref, qim_ref, s_ref):
    u = u_ref[...]
    dn = (((1,), (1,)), ((), ()))
    s_ref[:, 0:128] = lax.dot_general(u, qre_ref[...], dn, preferred_element_type=F32)
    s_ref[:, 128:256] = lax.dot_general(u, qim_ref[...], dn, preferred_element_type=F32)


def _s5_scan_kernel(s_ref, are_ref, aim_ref, h_ref, *, nc, ncc):
    ar, ai = are_ref[...], aim_ref[...]
    is_f = lax.broadcasted_iota(jnp.int32, ar.shape, 1) < S5_STATE

    def body(i, carry):
        hr, hi = carry
        cf = i
        cb = jnp.where(i < ncc, ncc - 1 - i, nc + ncc - 1 - i)
        h_ref[cf, :, 0:64] = hr[:, 0:64]
        h_ref[cb, :, 64:128] = hr[:, 64:128]
        h_ref[cf, :, 128:192] = hi[:, 0:64]
        h_ref[cb, :, 192:256] = hi[:, 64:128]
        sf, sb = s_ref[cf], s_ref[cb]
        sr = jnp.where(is_f, sf[:, 0:128], sb[:, 0:128])
        si = jnp.where(is_f, sf[:, 128:256], sb[:, 128:256])
        return ar * hr - ai * hi + sr, ar * hi + ai * hr + si

    zero = jnp.zeros(ar.shape, F32)
    lax.fori_loop(0, nc, body, (zero, zero))


def _s5_out_kernel(u_ref, m_ref, h_ref, p_ref, y_ref):
    y = jnp.dot(u_ref[...], m_ref[...], preferred_element_type=F32)
    y_ref[...] = y + jnp.dot(h_ref[...].astype(BF16), p_ref[...], preferred_element_type=F32)


def _s5_mixer(s_all, m, qre, qim, pre, pim, at, n_ctx):
    n = s_all.shape[0]
    g, t_len, wid = S5_GROUPS, S5_T, S5_T * S5_CH
    nc, ncc = n // t_len, n_ctx // t_len
    assert n % t_len == 0 and n_ctx % t_len == 0
    u = s_all.reshape(nc, t_len, g, S5_CH).transpose(2, 0, 1, 3).reshape(g, nc, wid)
    p_all = jnp.concatenate([pre.reshape(g, 128, wid), pim.reshape(g, 128, wid)], axis=1)
    a_re = at[:, :, 0, :].reshape(g, 128)
    a_im = at[:, :, 1, :].reshape(g, 128)
    s_t = pl.pallas_call(
        _s5_state_kernel, grid=(g,),
        in_specs=[pl.BlockSpec((None, nc, wid), lambda i: (i, 0, 0)),
                  pl.BlockSpec((None, 128, wid), lambda i: (i, 0, 0)),
                  pl.BlockSpec((None, 128, wid), lambda i: (i, 0, 0))],
        out_specs=pl.BlockSpec((nc, 256), lambda i: (0, i)),
        out_shape=jax.ShapeDtypeStruct((nc, g * 256), F32),
        compiler_params=_params(("parallel",), 32),
        name="s5_chunk_state",
    )(u, qre.reshape(g, 128, wid), qim.reshape(g, 128, wid))
    h_t = pl.pallas_call(
        functools.partial(_s5_scan_kernel, nc=nc, ncc=ncc),
        out_shape=jax.ShapeDtypeStruct((nc, g, 256), F32),
        compiler_params=pltpu.CompilerParams(vmem_limit_bytes=48 * MIB),
        name="s5_chunk_scan",
    )(s_t.reshape(nc, g, 256), a_re, a_im).reshape(nc, g * 256)
    y = pl.pallas_call(
        _s5_out_kernel, grid=(g,),
        in_specs=[pl.BlockSpec((None, nc, wid), lambda i: (i, 0, 0)),
                  pl.BlockSpec((None, wid, wid), lambda i: (i, 0, 0)),
                  pl.BlockSpec((nc, 256), lambda i: (0, i)),
                  pl.BlockSpec((None, 256, wid), lambda i: (i, 0, 0))],
        out_specs=pl.BlockSpec((None, nc, wid), lambda i: (i, 0, 0)),
        out_shape=jax.ShapeDtypeStruct((g, nc, wid), F32),
        compiler_params=_params(("parallel",), 32),
        name="s5_chunk_out",
    )(u, m, h_t, p_all)
    return y.reshape(g, nc, t_len, S5_CH).transpose(1, 2, 0, 3).reshape(n, W_S5)


def _gelu_tanh(x):
    return 0.5 * x * (1.0 + jnp.tanh(math.sqrt(2.0 / math.pi) * (x + 0.044715 * (x * x * x))))


def _even_out_kernel(pool_ref, y_ref, s_ref, dsk_ref, wglu_ref, wo_ref, h_ref, nw_ref, mod_ref, o_ref,
                     *, n_ctx, tm):
    y = _gelu_tanh(y_ref[...] + s_ref[...].astype(F32) * dsk_ref[...])
    gate = jnp.dot(y.astype(BF16), wglu_ref[...], preferred_element_type=F32)
    s5 = (y * _sigmoid(gate)).astype(BF16)
    mix = jnp.dot(pool_ref[...], wo_ref[0:W_POOL, :], preferred_element_type=F32)
    mix = mix + jnp.dot(s5, wo_ref[W_POOL:W_POOL + W_S5, :], preferred_element_type=F32)
    g = _gate_rows(mod_ref, pl.program_id(0) * tm, tm, n_ctx)
    o_ref[...] = h_ref[...] + g * _rms(mix, nw_ref[...])


def _even_out(pool_out, y_s5, z, dsk, w_glu, w_out, h, nw, mod, n_ctx):
    n, d = h.shape
    tm = _pick(n, (320, 256, 128))
    return pl.pallas_call(
        functools.partial(_even_out_kernel, n_ctx=n_ctx, tm=tm), grid=(n // tm,),
        in_specs=[pl.BlockSpec((tm, W_POOL), lambda i: (i, 0)),
                  pl.BlockSpec((tm, W_S5), lambda i: (i, 0)),
                  pl.BlockSpec((tm, W_S5), lambda i: (i, W_POOL // W_S5)),
                  pl.BlockSpec((1, W_S5), lambda i: (0, 0)),
                  pl.BlockSpec((W_S5, W_S5), lambda i: (0, 0)),
                  pl.BlockSpec((d, d), lambda i: (0, 0)),
                  pl.BlockSpec((tm, d), lambda i: (i, 0)),
                  pl.BlockSpec((1, d), lambda i: (0, 0)),
                  pl.BlockSpec((8, d), lambda i: (0, 0))],
        out_specs=pl.BlockSpec((tm, d), lambda i: (i, 0)),
        out_shape=jax.ShapeDtypeStruct((n, d), F32),
        compiler_params=_params(("parallel",), 48),
        name="even_mixer_out",
    )(pool_out, y_s5, z, dsk.reshape(1, W_S5), w_glu, w_out, h, nw.reshape(1, d), mod)


def _ffn_kernel(h_ref, nw2_ref, nw3_ref, mod_ref, wg_ref, wu_ref, wd_ref, o_ref, v_ref, *, n_ctx, tm):
    f = pl.program_id(1)

    @pl.when(f == 0)
    def _():
        _store_norm_mod(v_ref, h_ref, nw2_ref, mod_ref, pl.program_id(0) * tm, n_ctx)
        o_ref[...] = jnp.zeros_like(o_ref)
    v = v_ref[...]
    a = jnp.dot(v, wg_ref[...], preferred_element_type=F32)
    b = jnp.dot(v, wu_ref[...], preferred_element_type=F32)
    act = (a * _sigmoid(a) * b).astype(BF16)
    o_ref[...] += jnp.dot(act, wd_ref[...], preferred_element_type=F32)

    @pl.when(f == pl.num_programs(1) - 1)
    def _():
        g = _gate_rows(mod_ref, pl.program_id(0) * tm, tm, n_ctx)
        o_ref[...] = h_ref[...] + g * _rms(o_ref[...], nw3_ref[...])


def _dense_ffn(h, nw2, nw3, mod, wg, wu, wd, n_ctx):
    n, d = h.shape
    dff = wg.shape[1]
    tm = _pick(n, (640, 512, 256, 128))
    tf = 512
    return pl.pallas_call(
        functools.partial(_ffn_kernel, n_ctx=n_ctx, tm=tm), grid=(n // tm, dff // tf),
        in_specs=[pl.BlockSpec((tm, d), lambda i, f: (i, 0)),
                  pl.BlockSpec((1, d), lambda i, f: (0, 0)),
                  pl.BlockSpec((1, d), lambda i, f: (0, 0)),
                  pl.BlockSpec((8, d), lambda i, f: (0, 0)),
                  pl.BlockSpec((d, tf), lambda i, f: (0, f)),
                  pl.BlockSpec((d, tf), lambda i, f: (0, f)),
                  pl.BlockSpec((tf, d), lambda i, f: (f, 0))],
        out_specs=pl.BlockSpec((tm, d), lambda i, f: (i, 0)),
        out_shape=jax.ShapeDtypeStruct((n, d), F32),
        scratch_shapes=[pltpu.VMEM((tm, d), BF16)],
        compiler_params=_params(("parallel", "arbitrary"), 56),
        name="dense_swiglu",
    )(h, nw2.reshape(1, d), nw3.reshape(1, d), mod, wg, wu, wd)


def _split_bf16(x):
    hi = x.astype(BF16)
    return hi, (x - hi.astype(F32)).astype(BF16)


def _gla_kernel(qf_ref, kf_ref, vf_ref, af_ref, qb_ref, kb_ref, vb_ref, ab_ref, wa_ref, ba_ref,
                of_ref, ob_ref, st_ref, *, tb):
    c = GLA_CHUNK
    nsub = tb // c

    @pl.when(pl.program_id(0) == 0)
    def _():
        st_ref[...] = jnp.zeros_like(st_ref)
    ri = lax.broadcasted_iota(jnp.int32, (c, c), 0)
    ci = lax.broadcasted_iota(jnp.int32, (c, c), 1)
    scale = GLA_DK ** -0.5
    nt = (((1,), (1,)), ((), ()))
    tn = (((0,), (0,)), ((), ()))
    dirs = ((qf_ref, kf_ref, vf_ref, af_ref, of_ref), (qb_ref, kb_ref, vb_ref, ab_ref, ob_ref))
    for d, (q_ref, k_ref, v_ref, a_ref, o_ref) in enumerate(dirs):
        mask = (ri >= ci) if d == 0 else (ci >= ri)
        cum = mask.astype(BF16)
        a_hi, a_lo = _split_bf16(a_ref[...])
        w_hi, w_lo = _split_bf16(wa_ref[d])
        z = (jnp.dot(a_hi, w_hi, preferred_element_type=F32) + jnp.dot(a_hi, w_lo, preferred_element_type=F32)
             + jnp.dot(a_lo, w_hi, preferred_element_type=F32)) + ba_ref[d]
        glog = (jnp.minimum(z, 0.0) - jnp.log(1.0 + jnp.exp(-jnp.abs(z)))) * (1.0 / GLA_TAU)
        order = range(nsub) if d == 0 else range(nsub - 1, -1, -1)
        for sc in order:
            r0 = sc * c
            g_hi, g_lo = _split_bf16(glog[r0:r0 + c, :])
            b = jnp.dot(cum, g_hi, preferred_element_type=F32) + jnp.dot(cum, g_lo, preferred_element_type=F32)
            b_last = b[c - 1:c, :] if d == 0 else b[0:1, :]
            dec = jnp.exp(b_last)
            q = q_ref[r0:r0 + c, :].astype(F32) * scale
            k = k_ref[r0:r0 + c, :].astype(F32)
            qe = (q * jnp.exp(b)).astype(BF16)
            ke = (k * jnp.exp(-b)).astype(BF16)
            kd = (k * jnp.exp(b_last - b)).astype(BF16)
            for h in range(GLA_HEADS):
                ks = slice(h * GLA_DK, (h + 1) * GLA_DK)
                vs = slice(h * GLA_DV, (h + 1) * GLA_DV)
                att = lax.dot_general(qe[:, ks], ke[:, ks], nt, preferred_element_type=F32)
                att = jnp.where(mask, att, 0.0).astype(BF16)
                vh = v_ref[r0:r0 + c, vs]
                s_old = st_ref[d, h]
                o = jnp.dot(att, vh, preferred_element_type=F32)
                o = o + lax.dot_general(qe[:, ks], s_old.astype(BF16), nt, preferred_element_type=F32)
                o_ref[r0:r0 + c, vs] = o
                upd = lax.dot_general(vh, kd[:, ks], tn, preferred_element_type=F32)
                st_ref[d, h] = s_old * dec[:, ks] + upd


def _gla_mixer(zmain, a_aux, w_a, b_a, n_ctx):
    n = zmain.shape[0]
    tb = 256
    assert n % tb == 0 and n_ctx % tb == 0
    nb, ncb = n // tb, n_ctx // tb

    def bwd(s):
        return jnp.where(s < ncb, ncb - 1 - s, nb + ncb - 1 - s)
    wa = jnp.pad(w_a, ((0, 0), (0, 128 - GLA_RANK), (0, 0)))
    in_specs = []
    for order in (lambda s: s, bwd):
        in_specs += [pl.BlockSpec((tb, W_GLA_K), lambda s, o=order: (o(s), OQ // W_GLA_K)),
                     pl.BlockSpec((tb, W_GLA_K), lambda s, o=order: (o(s), OK_ // W_GLA_K)),
                     pl.BlockSpec((tb, W_GLA_V), lambda s, o=order: (o(s), OV // W_GLA_V)),
                     pl.BlockSpec((tb, 128), lambda s, o=order: (o(s), 0))]
    in_specs += [pl.BlockSpec((2, 128, W_GLA_K), lambda s: (0, 0, 0)),
                 pl.BlockSpec((2, 1, W_GLA_K), lambda s: (0, 0, 0))]
    return pl.pallas_call(
        functools.partial(_gla_kernel, tb=tb), grid=(nb,),
        in_specs=in_specs,
        out_specs=[pl.BlockSpec((tb, W_GLA_V), lambda s: (s, 0)),
                   pl.BlockSpec((tb, W_GLA_V), lambda s: (bwd(s), 0))],
        out_shape=[jax.ShapeDtypeStruct((n, W_GLA_V), F32)] * 2,
        scratch_shapes=[pltpu.VMEM((2, GLA_HEADS, GLA_DV, GLA_DK), F32)],
        compiler_params=_params(("arbitrary",), 32),
        name="gla_chunked",
    )(zmain, zmain, zmain, a_aux, zmain, zmain, zmain, a_aux, wa, b_a.reshape(2, 1, W_GLA_K))


_ATT_PAD = 2 * ATT_HEAD_DIM
_LOG2E = 1.4426950408889634
_FIXED_SHIFT_MAX_BOUND = 40.0


def _rope_fn(row0, tm, n_ctx):
    hd = ATT_HEAD_DIM
    row = row0 + lax.broadcasted_iota(jnp.int32, (tm, hd), 0)
    lane = lax.broadcasted_iota(jnp.int32, (tm, hd), 1)
    quarter = hd // 4
    first = (lane & quarter) == 0
    assert tm % GRID_W == 0 and n_ctx % GRID_W == 0
    reps = tm // GRID_W
    lane_s = lax.broadcasted_iota(jnp.int32, (GRID_W, hd), 1)
    freq = jnp.exp((lane_s % quarter).astype(F32) * (-math.log(ROPE_THETA) / quarter))
    ang_col = lax.broadcasted_iota(jnp.int32, (GRID_W, hd), 0).astype(F32) * freq
    grid_row0 = (row0 - n_ctx) // GRID_W
    ang_row = (grid_row0 + lax.broadcasted_iota(jnp.int32, (8, hd), 0)).astype(F32) * freq[0:8, :]
    cos_c, sin_c = jnp.cos(ang_col), jnp.sin(ang_col)
    cos_r, sin_r = jnp.cos(ang_row), jnp.sin(ang_row)

    def expand(tab_r, tab_c):
        by_row = jnp.concatenate([jnp.broadcast_to(tab_r[k:k + 1, :], (GRID_W, hd)) for k in range(reps)], axis=0)
        return jnp.where(lane < hd // 2, by_row, jnp.concatenate([tab_c] * reps, axis=0))
    cosv = expand(cos_r, cos_c)
    sinv = expand(sin_r, sin_c)
    sin_s = jnp.where(first, -sinv, sinv)

    def rope(x):
        sw = jnp.where(first, pltpu.roll(x, hd - quarter, 1), pltpu.roll(x, quarter, 1))
        return x * cosv + sw * sin_s
    return rope, row >= n_ctx


def _k_prep_kernel(k_ref, v_ref, kw_ref, kx_ref, vx_ref, kmax_ref, *, n_ctx, tm):
    hd = ATT_HEAD_DIM
    rope, is_lat = _rope_fn(pl.program_id(0) * tm, tm, n_ctx)
    lane = lax.broadcasted_iota(jnp.int32, (tm, hd), 1)
    one_col = jnp.where(lane == 0, 1.0, 0.0).astype(BF16)
    ones = jnp.ones((tm, hd), BF16)
    nmax = jnp.zeros((tm, 1), F32)
    for h in range(ATT_KV_HEADS):
        sl = slice(h * hd, (h + 1) * hd)
        xn = _rms(k_ref[:, sl].astype(F32), kw_ref[...])
        kr = jnp.where(is_lat, rope(xn), xn).astype(BF16)
        kx_ref[:, h * _ATT_PAD:h * _ATT_PAD + hd] = kr
        kx_ref[:, h * _ATT_PAD + hd:(h + 1) * _ATT_PAD] = one_col
        vx_ref[:, h * _ATT_PAD:h * _ATT_PAD + hd] = v_ref[:, sl]
        vx_ref[:, h * _ATT_PAD + hd:(h + 1) * _ATT_PAD] = ones
        krf = kr.astype(F32)
        nmax = jnp.maximum(nmax, jnp.sum(krf * krf, axis=-1, keepdims=True))
    kmax_ref[...] = jnp.broadcast_to(jnp.max(nmax, axis=0, keepdims=True), kmax_ref.shape)


def _q_prep_kernel(q_ref, qw_ref, kn_ref, qx_ref, bmax_ref, *, n_ctx, tm):
    hd = ATT_HEAD_DIM
    rope, _ = _rope_fn(n_ctx + pl.program_id(0) * tm, tm, n_ctx)
    lane = lax.broadcasted_iota(jnp.int32, (tm, hd), 1)
    scale = hd ** -0.5 * _LOG2E
    bmax = jnp.zeros((tm, 1), F32)
    for h in range(ATT_HEADS):
        sl = slice(h * hd, (h + 1) * hd)
        xn = _rms(q_ref[:, sl].astype(F32), qw_ref[...])
        qr = (rope(xn) * scale).astype(BF16)
        qf = qr.astype(F32)
        bound = jnp.sqrt(jnp.sum(qf * qf, axis=-1, keepdims=True)) * kn_ref[...]
        qx_ref[:, h * _ATT_PAD:h * _ATT_PAD + hd] = qr
        qx_ref[:, h * _ATT_PAD + hd:(h + 1) * _ATT_PAD] = jnp.where(lane == 0, -bound, 0.0).astype(BF16)
        bmax = jnp.maximum(bmax, bound)
    bmax_ref[...] = jnp.broadcast_to(jnp.max(bmax, axis=0, keepdims=True), bmax_ref.shape)


def _qk_prep(zmain, q_norm, k_norm, n_ctx):
    n = zmain.shape[0]
    lq = n - n_ctx
    tm = 256
    hd = ATT_HEAD_DIM
    assert n_ctx % tm == 0
    ncb = n_ctx // tm
    kx, vx, kmax = pl.pallas_call(
        functools.partial(_k_prep_kernel, n_ctx=n_ctx, tm=tm), grid=(n // tm,),
        in_specs=[pl.BlockSpec((tm, W_ATT_KV), lambda i: (i, OAK // W_ATT_KV)),
                  pl.BlockSpec((tm, W_ATT_KV), lambda i: (i, OAV // W_ATT_KV)),
                  pl.BlockSpec((1, hd), lambda i: (0, 0))],
        out_specs=[pl.BlockSpec((tm, ATT_KV_HEADS * _ATT_PAD), lambda i: (i, 0)),
                   pl.BlockSpec((tm, ATT_KV_HEADS * _ATT_PAD), lambda i: (i, 0)),
                   pl.BlockSpec((None, 8, 128), lambda i: (i, 0, 0))],
        out_shape=[jax.ShapeDtypeStruct((n, ATT_KV_HEADS * _ATT_PAD), BF16),
                   jax.ShapeDtypeStruct((n, ATT_KV_HEADS * _ATT_PAD), BF16),
                   jax.ShapeDtypeStruct((n // tm, 8, 128), F32)],
        compiler_params=_params(("parallel",), 32),
        name="k_norm_rope",
    )(zmain, zmain, k_norm.reshape(1, hd))
    knorm = jnp.sqrt(jnp.max(kmax)).reshape(1, 1)
    qx, bmax = pl.pallas_call(
        functools.partial(_q_prep_kernel, n_ctx=n_ctx, tm=tm), grid=(lq // tm,),
        in_specs=[pl.BlockSpec((tm, W_ATT), lambda i: (i + ncb, OAQ // W_ATT)),
                  pl.BlockSpec((1, hd), lambda i: (0, 0)),
                  pl.BlockSpec((1, 1), lambda i: (0, 0))],
        out_specs=[pl.BlockSpec((tm, ATT_HEADS * _ATT_PAD), lambda i: (i, 0)),
                   pl.BlockSpec((None, 8, 128), lambda i: (i, 0, 0))],
        out_shape=[jax.ShapeDtypeStruct((lq, ATT_HEADS * _ATT_PAD), BF16),
                   jax.ShapeDtypeStruct((lq // tm, 8, 128), F32)],
        compiler_params=_params(("parallel",), 32),
        name="q_norm_rope",
    )(zmain, q_norm.reshape(1, hd), knorm)
    return qx, kx, vx, jnp.max(bmax)


def _flash_fixed_kernel(q_ref, kt_ref, v_ref, o_ref, acc_ref):
    c = pl.program_id(2)
    hd = ATT_HEAD_DIM

    @pl.when(c == 0)
    def _():
        acc_ref[...] = jnp.zeros_like(acc_ref)
    kt = kt_ref[...]
    v = v_ref[...]
    for g in range(ATT_GROUP):
        s = jnp.dot(q_ref[:, g * _ATT_PAD:(g + 1) * _ATT_PAD], kt, preferred_element_type=F32)
        acc_ref[g] += jnp.dot(jnp.exp2(s).astype(BF16), v, preferred_element_type=F32)

    @pl.when(c == pl.num_programs(2) - 1)
    def _():
        for g in range(ATT_GROUP):
            a = acc_ref[g]
            o_ref[:, g * hd:(g + 1) * hd] = (a[:, 0:hd] / a[:, hd:2 * hd]).astype(o_ref.dtype)


def _flash_online_kernel(q_ref, kt_ref, v_ref, o_ref, m_ref, acc_ref):
    c = pl.program_id(2)
    hd = ATT_HEAD_DIM

    @pl.when(c == 0)
    def _():
        m_ref[...] = jnp.full(m_ref.shape, -jnp.inf, F32)
        acc_ref[...] = jnp.zeros_like(acc_ref)
    kt = kt_ref[...]
    v = v_ref[...]
    for g in range(ATT_GROUP):
        s = jnp.dot(q_ref[:, g * _ATT_PAD:(g + 1) * _ATT_PAD], kt, preferred_element_type=F32)
        m_prev = m_ref[g]
        m_new = jnp.maximum(m_prev, jnp.max(s, axis=-1, keepdims=True))
        p = jnp.exp2(s - m_new).astype(BF16)
        acc_ref[g] = jnp.exp2(m_prev - m_new) * acc_ref[g] + jnp.dot(p, v, preferred_element_type=F32)
        m_ref[g] = m_new

    @pl.when(c == pl.num_programs(2) - 1)
    def _():
        for g in range(ATT_GROUP):
            a = acc_ref[g]
            o_ref[:, g * hd:(g + 1) * hd] = (a[:, 0:hd] / a[:, hd:2 * hd]).astype(o_ref.dtype)


def _attention(qx, kx, vx, bound_log2):
    lq = qx.shape[0]
    n = kx.shape[0]
    kt = kx.T
    gw = ATT_GROUP * _ATT_PAD
    ow = ATT_GROUP * ATT_HEAD_DIM
    def call(kern, tq, tk, scratch, name):
        return pl.pallas_call(
            kern, grid=(ATT_KV_HEADS, lq // tq, n // tk),
            in_specs=[pl.BlockSpec((tq, gw), lambda h, i, j: (i, h)),
                      pl.BlockSpec((_ATT_PAD, tk), lambda h, i, j: (h, j)),
                      pl.BlockSpec((tk, _ATT_PAD), lambda h, i, j: (j, h))],
            out_specs=pl.BlockSpec((tq, ow), lambda h, i, j: (i, h)),
            out_shape=jax.ShapeDtypeStruct((lq, W_ATT), BF16),
            scratch_shapes=scratch(tq) + [pltpu.VMEM((ATT_GROUP, tq, _ATT_PAD), F32)],
            compiler_params=_params(("parallel", "parallel", "arbitrary"), 56),
            name=name,
        )(qx, kt, vx)

    def fixed(_):
        return call(_flash_fixed_kernel, _pick(lq, (1024, 512, 256, 128)), _pick(n, (3328, 1280, 512, 256)),
                    lambda tq: [], "gqa_flash_fixed_shift")

    def online(_):
        return call(_flash_online_kernel, _pick(lq, (512, 256, 128)), _pick(n, (640, 512, 256, 128)),
                    lambda tq: [pltpu.VMEM((ATT_GROUP, tq, 1), F32)], "gqa_flash_online")
    return lax.cond(bound_log2 <= _FIXED_SHIFT_MAX_BOUND * _LOG2E, fixed, online, None)


def _odd_out_kernel(of_ref, ob_ref, r_ref, att_ref, gn_ref, wo_ref, h_ref, nw_ref, mod_ref, o_ref):
    o = of_ref[...] + ob_ref[...]
    r = r_ref[...].astype(F32)
    silu_r = r * _sigmoid(r)
    parts = []
    for h in range(GLA_HEADS):
        vs = slice(h * GLA_DV, (h + 1) * GLA_DV)
        parts.append((_rms(o[:, vs], gn_ref[:, vs]) * silu_r[:, vs]).astype(BF16))
    gla = jnp.concatenate(parts, axis=1)
    mix = jnp.dot(gla, wo_ref[0:W_GLA_V, :], preferred_element_type=F32)
    mix = mix + jnp.dot(att_ref[...], wo_ref[W_GLA_V:W_GLA_V + W_ATT, :], preferred_element_type=F32)
    o_ref[...] = h_ref[...] + mod_ref[4:5, :] * _rms(mix, nw_ref[...])


def _odd_out(o_f, o_b, zmain, att, gla_norm, w_out, h_all, nw, mod, n_ctx):
    n, d = h_all.shape
    lq = n - n_ctx
    tm = _pick(lq, (256, 128))
    assert n_ctx % tm == 0
    off = n_ctx // tm
    return pl.pallas_call(
        _odd_out_kernel, grid=(lq // tm,),
        in_specs=[pl.BlockSpec((tm, W_GLA_V), lambda i: (i + off, 0)),
                  pl.BlockSpec((tm, W_GLA_V), lambda i: (i + off, 0)),
                  pl.BlockSpec((tm, W_GLA_V), lambda i: (i + off, OR // W_GLA_V)),
                  pl.BlockSpec((tm, W_ATT), lambda i: (i, 0)),
                  pl.BlockSpec((1, W_GLA_V), lambda i: (0, 0)),
                  pl.BlockSpec((d, d), lambda i: (0, 0)),
                  pl.BlockSpec((tm, d), lambda i: (i + off, 0)),
                  pl.BlockSpec((1, d), lambda i: (0, 0)),
                  pl.BlockSpec((8, d), lambda i: (0, 0))],
        out_specs=pl.BlockSpec((tm, d), lambda i: (i, 0)),
        out_shape=jax.ShapeDtypeStruct((lq, d), F32),
        compiler_params=_params(("parallel",), 48),
        name="odd_mixer_out",
    )(o_f, o_b, zmain, att, gla_norm.reshape(1, W_GLA_V), w_out, h_all, nw.reshape(1, d), mod)


def _router_kernel(h_ref, nw_ref, mod_ref, wr_ref, br_ref, v_ref, r_ref):
    x = h_ref[...]
    v = _rms(x, nw_ref[...]) * (1.0 + mod_ref[0:1, :]) + mod_ref[1:2, :]
    v_ref[...] = v
    logits = _dot_hi(v, wr_ref[...]) + br_ref[...]
    lane = lax.broadcasted_iota(jnp.int32, logits.shape, 1)
    m1 = jnp.max(logits, axis=-1, keepdims=True)
    i1 = jnp.min(jnp.where(logits == m1, lane, 128), axis=-1, keepdims=True)
    rest = jnp.where(lane == i1, -jnp.inf, logits)
    m2 = jnp.max(rest, axis=-1, keepdims=True)
    i2 = jnp.min(jnp.where(rest == m2, lane, 128), axis=-1, keepdims=True)
    e2 = jnp.exp(m2 - m1)
    g1 = 1.0 / (1.0 + e2)
    g2 = e2 / (1.0 + e2)
    out = jnp.where(lane == 0, i1.astype(F32), 0.0)
    out = jnp.where(lane == 1, i2.astype(F32), out)
    out = jnp.where(lane == 2, g1, out)
    out = jnp.where(lane == 3, g2, out)
    r_ref[...] = out


def _router(h_lat, nw, mod, w_router, b_router):
    lq, d = h_lat.shape
    tm = _pick(lq, (512, 256, 128))
    wr = jnp.pad(w_router, ((0, 0), (0, 128 - N_EXPERTS)))
    br = jnp.pad(b_router, (0, 128 - N_EXPERTS), constant_values=-1e30).reshape(1, 128)
    return pl.pallas_call(
        _router_kernel, grid=(lq // tm,),
        in_specs=[pl.BlockSpec((tm, d), lambda i: (i, 0)),
                  pl.BlockSpec((1, d), lambda i: (0, 0)),
                  pl.BlockSpec((8, d), lambda i: (0, 0)),
                  pl.BlockSpec((d, 128), lambda i: (0, 0)),
                  pl.BlockSpec((1, 128), lambda i: (0, 0))],
        out_specs=[pl.BlockSpec((tm, d), lambda i: (i, 0)), pl.BlockSpec((tm, 128), lambda i: (i, 0))],
        out_shape=[jax.ShapeDtypeStruct((lq, d), F32), jax.ShapeDtypeStruct((lq, 128), F32)],
        compiler_params=_params(("parallel",), 40),
        name="router_top2",
    )(h_lat, nw.reshape(1, d), mod, wr, br)


def _expert_kernel(te_ref, tv_ref, src_ref, dst_ref, v_hbm, wg_ref, wu_ref, wd_ref, y_hbm,
                   xg_ref, xb_ref, ys_ref, acc_ref, gsem, ssem, *, tm, nt, nf):
    j = pl.program_id(0)
    f = pl.program_id(1)
    slot = j % 2
    per_step = tm // nf
    lo = f * per_step
    valid = tv_ref[j] != 0

    def start_gather(tile, slot_, row):
        tok = src_ref[tile * tm + row]
        pltpu.make_async_copy(v_hbm.at[pl.ds(tok, 1)], xg_ref.at[slot_, pl.ds(row, 1)], gsem.at[slot_]).start()

    def start_scatter(row):
        dst = dst_ref[j * tm + row]
        pltpu.make_async_copy(ys_ref.at[1 - slot, pl.ds(row, 1)], y_hbm.at[pl.ds(dst, 1)],
                              ssem.at[1 - slot]).start()

    def wait_scatter(slot_):
        pltpu.make_async_copy(ys_ref.at[slot_], y_hbm.at[pl.ds(0, tm)], ssem.at[slot_]).wait()

    def gather_loop(tile, slot_, first, count):
        def body(r, carry):
            start_gather(tile, slot_, first + r)
            return carry
        lax.fori_loop(0, count, body, 0, unroll=8)

    @pl.when((j == 0) & (f == 0))
    def _():
        gather_loop(0, 0, 0, tm)
        ys_ref[1] = jnp.zeros((tm, ys_ref.shape[2]), F32)

    @pl.when(f == 0)
    def _():
        pltpu.make_async_copy(v_hbm.at[pl.ds(0, tm)], xg_ref.at[slot], gsem.at[slot]).wait()
        xb_ref[...] = xg_ref[slot].astype(BF16)
        acc_ref[...] = jnp.zeros_like(acc_ref)
        for row in range(nf * per_step, tm):
            start_scatter(row)

        @pl.when(j + 1 < nt)
        def _():
            for row in range(nf * per_step, tm):
                start_gather(j + 1, 1 - slot, row)

    def compute(rows_used):
        half = per_step // 2

        def gathers(first, last, after):
            for r in range(first, last):
                start_gather(j + 1, 1 - slot, lo + r + after)

        def scatters(first, last, after):
            for r in range(first, last):
                start_scatter(lo + r + after)

        def zero_after(val):
            bits = pltpu.bitcast(val[0:8, 0:128], jnp.int32)
            return ((bits & 1) >> 1)[0, 0]

        gathers(0, half, 0)
        x = xb_ref[0:rows_used, :]
        a = jnp.dot(x, wg_ref[...], preferred_element_type=F32)
        gathers(half, per_step, zero_after(a))
        b = jnp.dot(x, wu_ref[...], preferred_element_type=F32)
        scatters(0, half, zero_after(b))
        act = (a * _sigmoid(a) * b).astype(BF16)
        down = jnp.dot(act, wd_ref[...], preferred_element_type=F32)
        scatters(half, per_step, zero_after(down))
        acc_ref[0:rows_used, :] += down

    pl.when(tv_ref[j] == 1)(functools.partial(compute, tm))
    pl.when(tv_ref[j] == 2)(functools.partial(compute, tm // 2))

    @pl.when(jnp.logical_not(valid))
    def _():
        @pl.when(j + 1 < nt)
        def _():
            gather_loop(j + 1, 1 - slot, lo, per_step)

        def body(r, carry):
            start_scatter(lo + r)
            return carry
        lax.fori_loop(0, per_step, body, 0, unroll=8)

    @pl.when(f == nf - 1)
    def _():
        @pl.when(j > 0)
        def _():
            wait_scatter(slot)
        ys_ref[slot] = acc_ref[...]

        @pl.when(j == nt - 1)
        def _():
            wait_scatter(1 - slot)


def _combine_kernel(y0_ref, y1_ref, route_ref, h_ref, nw_ref, mod_ref, o_ref):
    route = route_ref[...]
    f = y0_ref[...] * route[:, 2:3] + y1_ref[...] * route[:, 3:4]
    o_ref[...] = h_ref[...] + mod_ref[4:5, :] * _rms(f, nw_ref[...])


def _count_le(sorted_vals, queries):
    return jnp.sum((sorted_vals[None, :] <= queries[:, None]).astype(jnp.int32), axis=1)


def _moe(h_lat, nw2, nw3, mod, w_router, b_router, wg, wu, wd):
    lq, d = h_lat.shape
    e = N_EXPERTS
    dff = wg.shape[2]
    tm = _pick(lq, (512, 256))
    tf = 1024
    v, route = _router(h_lat, nw2, mod, w_router, b_router)
    idx = route[:, 0:2].astype(jnp.int32)

    npairs = 2 * lq
    nt = npairs // tm + e + 1
    flat_e = idx.reshape(-1)
    pair_id = jnp.arange(npairs, dtype=jnp.int32)
    onehot = (flat_e[:, None] == jnp.arange(e, dtype=jnp.int32)[None, :]).astype(jnp.int32)
    csum = jnp.cumsum(onehot, axis=0)
    rank = jnp.sum(csum * onehot, axis=1) - 1
    counts = csum[-1]
    padded = ((counts + tm - 1) // tm) * tm
    ends = jnp.cumsum(padded)
    starts = ends - padded
    pos = jnp.sum(onehot * starts[None, :], axis=1) + rank
    sorted_pair = jnp.full((nt * tm,), -1, jnp.int32).at[pos].set(pair_id)
    is_pad = sorted_pair < 0
    src_row = jnp.where(is_pad, 0, sorted_pair >> 1).astype(jnp.int32)
    pad_rank = jnp.cumsum(is_pad.astype(jnp.int32)) - 1
    dst_row = jnp.where(is_pad, npairs + pad_rank, (sorted_pair & 1) * lq + (sorted_pair >> 1)).astype(jnp.int32)
    dst_row = jnp.concatenate([(nt - 1) * tm + jnp.arange(tm, dtype=jnp.int32), dst_row])
    n_used = ends[-1] // tm
    tile_ids = jnp.arange(nt, dtype=jnp.int32)
    tile_exp = jnp.minimum(_count_le(ends, jnp.minimum(tile_ids, n_used - 1) * tm), e - 1)
    onehot_t = (tile_exp[:, None] == jnp.arange(e, dtype=jnp.int32)[None, :]).astype(jnp.int32)
    real_end = jnp.sum(onehot_t * (starts + counts)[None, :], axis=1)
    tile_rows = jnp.clip(real_end - tile_ids * tm, 0, tm)
    tile_valid = jnp.where(tile_ids < n_used, jnp.where(tile_rows <= tm // 2, 2, 1), 0).astype(jnp.int32)

    nf = dff // tf
    pinned = lambda j, f, tv: jnp.where(tv[j] != 0, f, nf - 1)
    y = pl.pallas_call(
        functools.partial(_expert_kernel, tm=tm, nt=nt, nf=nf),
        grid_spec=pltpu.PrefetchScalarGridSpec(
            num_scalar_prefetch=4, grid=(nt, nf),
            in_specs=[pl.BlockSpec(memory_space=pl.ANY),
                      pl.BlockSpec((None, d, tf), lambda j, f, te, tv, src, dst: (te[j], 0, pinned(j, f, tv))),
                      pl.BlockSpec((None, d, tf), lambda j, f, te, tv, src, dst: (te[j], 0, pinned(j, f, tv))),
                      pl.BlockSpec((None, tf, d), lambda j, f, te, tv, src, dst: (te[j], pinned(j, f, tv), 0))],
            out_specs=pl.BlockSpec(memory_space=pl.ANY),
            scratch_shapes=[pltpu.VMEM((2, tm, d), F32), pltpu.VMEM((tm, d), BF16), pltpu.VMEM((2, tm, d), F32),
                            pltpu.VMEM((tm, d), F32), pltpu.SemaphoreType.DMA((2,)),
                            pltpu.SemaphoreType.DMA((2,))]),
        out_shape=jax.ShapeDtypeStruct((nt * tm, d), F32),
        compiler_params=pltpu.CompilerParams(dimension_semantics=("arbitrary", "arbitrary"),
                                             vmem_limit_bytes=56 * MIB, disable_bounds_checks=True),
        name="moe_experts",
    )(tile_exp, tile_valid, src_row, dst_row, v, wg, wu, wd)

    tc = _pick(lq, (256, 128))
    return pl.pallas_call(
        _combine_kernel, grid=(lq // tc,),
        in_specs=[pl.BlockSpec((tc, d), lambda i: (i, 0)),
                  pl.BlockSpec((tc, d), lambda i: (i + lq // tc, 0)),
                  pl.BlockSpec((tc, 128), lambda i: (i, 0)),
                  pl.BlockSpec((tc, d), lambda i: (i, 0)),
                  pl.BlockSpec((1, d), lambda i: (0, 0)),
                  pl.BlockSpec((8, d), lambda i: (0, 0))],
        out_specs=pl.BlockSpec((tc, d), lambda i: (i, 0)),
        out_shape=jax.ShapeDtypeStruct((lq, d), F32),
        compiler_params=_params(("parallel",), 32),
        name="moe_combine",
    )(y, y, route, h_lat, nw3.reshape(1, d), mod)


def kernel(x, c, ctx, c_ctx, w_mod, b_mod, norms, e_w_in, e_pool_w, e_pool_scale, e_s5_lam_re, e_s5_lam_im, e_s5_log_dt, e_s5_b_re, e_s5_b_im, e_s5_c_re, e_s5_c_im, e_s5_d, e_s5_w_glu, e_w_out, e_ffn_gate, e_ffn_up, e_ffn_down, o_w_in, o_gla_w_a, o_gla_b_a, o_gla_norm, o_q_norm, o_k_norm, o_w_out, o_router, o_router_b, o_moe_gate, o_moe_up, o_moe_down):
    assert x.shape[0] == 1 and w_mod.shape[0] == 2 and e_w_in.shape[0] == 1 and o_w_in.shape[0] == 1
    d = x.shape[2]
    n_ctx = ctx.shape[1]
    h = jnp.concatenate([ctx[0], x[0]], axis=0)

    vecs = jnp.zeros((8, d), F32).at[0].set(c[0]).at[1].set(c_ctx)
    mods = _modulation(vecs, w_mod, b_mod)

    mod1, mod2 = _mod_rows(mods[0], 0), _mod_rows(mods[0], 1)
    z = _normed_matmul(h, norms[0, 0], mod1, e_w_in[0].astype(BF16), n_ctx)
    pool_out = _pool_mixer(z, e_pool_w[0].astype(BF16), e_pool_scale[0], n_ctx)
    m_op, qre, qim, pre, pim, at = _s5_params(e_s5_lam_re[0], e_s5_lam_im[0], e_s5_log_dt[0],
                                              e_s5_b_re[0], e_s5_b_im[0], e_s5_c_re[0], e_s5_c_im[0])
    y_s5 = _s5_mixer(z[:, W_POOL:], m_op, qre, qim, pre, pim, at, n_ctx)
    h = _even_out(pool_out, y_s5, z, e_s5_d[0], e_s5_w_glu[0].astype(BF16), e_w_out[0].astype(BF16),
                  h, norms[0, 1], mod1, n_ctx)
    h = _dense_ffn(h, norms[0, 2], norms[0, 3], mod2, e_ffn_gate[0].astype(BF16),
                   e_ffn_up[0].astype(BF16), e_ffn_down[0].astype(BF16), n_ctx)

    mod1, mod2 = _mod_rows(mods[1], 0), _mod_rows(mods[1], 1)
    w_in = o_w_in[0]
    a0 = 2 * W_GLA_K + W_GLA_V
    w_main = jnp.concatenate([w_in[:, :a0], w_in[:, a0 + GLA_RANK:]], axis=1).astype(BF16)
    w_aux = jnp.pad(w_in[:, a0:a0 + GLA_RANK], ((0, 0), (0, 128 - GLA_RANK))).astype(BF16)
    zmain, a_aux = _normed_matmul(h, norms[1, 0], mod1, w_main, n_ctx, w_aux=w_aux)
    o_f, o_b = _gla_mixer(zmain, a_aux, o_gla_w_a[0], o_gla_b_a[0], n_ctx)
    qx, kx, vx, bound_log2 = _qk_prep(zmain, o_q_norm[0], o_k_norm[0], n_ctx)
    att = _attention(qx, kx, vx, bound_log2)
    h_lat = _odd_out(o_f, o_b, zmain, att, o_gla_norm[0], o_w_out[0].astype(BF16), h, norms[1, 1], mod1, n_ctx)
    out = _moe(h_lat, norms[1, 2], norms[1, 3], mod2, o_router[0], o_router_b[0],
               o_moe_gate[0].astype(BF16), o_moe_up[0].astype(BF16), o_moe_down[0].astype(BF16))
    return out[None]
```

```python
import functools
import math

import jax
import jax.numpy as jnp
from jax import lax
from jax.experimental import pallas as pl
from jax.experimental.pallas import tpu as pltpu

F32 = jnp.float32
BF16 = jnp.bfloat16
HI = lax.Precision.HIGHEST
EPS = 1e-6

D_MODEL = 2048
GRID_W = 64
N_MOD = 6

POOL_WINDOWS = (2, 4, 8, 16)
POOL_GROUP = 384
W_POOL = 1536
W_S5 = 512
S5_CH = 16
S5_STATE = 64
S5_GROUPS = 32
S5_T = 32

GLA_HEADS = 4
GLA_DK = 128
GLA_DV = 256
GLA_RANK = 16
GLA_TAU = 16.0
GLA_CHUNK = 64
W_GLA_K = 512
W_GLA_V = 1024
ATT_HEAD_DIM = 128
ATT_HEADS = 8
ATT_KV_HEADS = 2
ATT_GROUP = 4
W_ATT = 1024
W_ATT_KV = 256
ROPE_THETA = 10000.0

D_FF = 7168
N_EXPERTS = 8

OQ, OK_, OV, OR, OAQ, OAK, OAV = 0, 512, 1024, 2048, 3072, 4096, 4352
W_ODD_MAIN = 4608

MIB = 2 ** 20


def _params(sem, vmem_mib):
    return pltpu.CompilerParams(dimension_semantics=sem, vmem_limit_bytes=vmem_mib * MIB)


def _pick(n, cands):
    for c in cands:
        if n % c == 0:
            return c
    raise ValueError(f"no tile for {n} in {cands}")


def _sigmoid(x):
    return 1.0 / (1.0 + jnp.exp(-x))


def _rms(x, w):
    return x * lax.rsqrt(jnp.mean(x * x, axis=-1, keepdims=True) + EPS) * w


_NORM_CHUNK = 16


def _store_norm_mod(dst_ref, h_ref, nw_ref, mod_ref, row0, n_ctx):
    tm = h_ref.shape[0]
    nw = nw_ref[...]
    w_lat = nw * (1.0 + mod_ref[0:1, :])
    sh_lat = mod_ref[1:2, :]

    def rows(c):
        r0 = pl.multiple_of(c * _NORM_CHUNK, _NORM_CHUNK)
        x = h_ref[pl.ds(r0, _NORM_CHUNK), :]
        return r0, x * lax.rsqrt(jnp.mean(x * x, axis=-1, keepdims=True) + EPS)

    @pl.when(row0 >= n_ctx)
    def _():
        def body(c, carry):
            r0, xr = rows(c)
            dst_ref[pl.ds(r0, _NORM_CHUNK), :] = (xr * w_lat + sh_lat).astype(dst_ref.dtype)
            return carry
        lax.fori_loop(0, tm // _NORM_CHUNK, body, 0, unroll=4)

    @pl.when(row0 < n_ctx)
    def _():
        w_ctx = nw * (1.0 + mod_ref[2:3, :])
        sh_ctx = mod_ref[3:4, :]

        def body(c, carry):
            r0, xr = rows(c)
            is_ctx = row0 + r0 + lax.broadcasted_iota(jnp.int32, (_NORM_CHUNK, 1), 0) < n_ctx
            u = xr * jnp.where(is_ctx, w_ctx, w_lat) + jnp.where(is_ctx, sh_ctx, sh_lat)
            dst_ref[pl.ds(r0, _NORM_CHUNK), :] = u.astype(dst_ref.dtype)
            return carry
        lax.fori_loop(0, tm // _NORM_CHUNK, body, 0, unroll=4)


def _gate_rows(mod_ref, row0, tm, n_ctx):
    rows = row0 + lax.broadcasted_iota(jnp.int32, (tm, 1), 0)
    return jnp.where(rows < n_ctx, mod_ref[5:6, :], mod_ref[4:5, :])


def _mod_kernel(v_ref, w_ref, b_ref, o_ref):
    v = v_ref[...]
    s = v * _sigmoid(v)
    o_ref[...] = jnp.dot(s, w_ref[...], precision=HI, preferred_element_type=F32) + b_ref[...]


def _modulation(vecs, w_mod, b_mod):
    depth, d, n6 = w_mod.shape
    tn = 1024
    return pl.pallas_call(
        _mod_kernel,
        grid=(depth, n6 // tn),
        in_specs=[pl.BlockSpec((8, d), lambda l, j: (0, 0)),
                  pl.BlockSpec((None, d, tn), lambda l, j: (l, 0, j)),
                  pl.BlockSpec((None, 1, tn), lambda l, j: (l, 0, j))],
        out_specs=pl.BlockSpec((None, 8, tn), lambda l, j: (l, 0, j)),
        out_shape=jax.ShapeDtypeStruct((depth, 8, n6), F32),
        compiler_params=_params(("parallel", "parallel"), 40),
        name="modulation",
    )(vecs, w_mod, b_mod.reshape(depth, 1, n6))


def _mod_rows(m, sub):
    d = m.shape[1] // N_MOD
    m6 = m.reshape(8, N_MOD, d)
    sh, sc, g = m6[:, 3 * sub + 0], m6[:, 3 * sub + 1], m6[:, 3 * sub + 2]
    z = jnp.zeros((d,), F32)
    return jnp.stack([sc[0], sh[0], sc[1], sh[1], g[0], g[1], z, z])


def _nmm_kernel(h_ref, nw_ref, mod_ref, w_ref, o_ref, u_ref, *, n_ctx, tm):
    @pl.when(pl.program_id(1) == 0)
    def _():
        _store_norm_mod(u_ref, h_ref, nw_ref, mod_ref, pl.program_id(0) * tm, n_ctx)
    o_ref[...] = jnp.dot(u_ref[...], w_ref[...], preferred_element_type=F32).astype(o_ref.dtype)


def _nmm_aux_kernel(h_ref, nw_ref, mod_ref, w_ref, wa_ref, o_ref, oa_ref, u_ref, *, n_ctx, tm):
    @pl.when(pl.program_id(1) == 0)
    def _():
        _store_norm_mod(u_ref, h_ref, nw_ref, mod_ref, pl.program_id(0) * tm, n_ctx)
        oa_ref[...] = jnp.dot(u_ref[...], wa_ref[...], preferred_element_type=F32)
    o_ref[...] = jnp.dot(u_ref[...], w_ref[...], preferred_element_type=F32).astype(o_ref.dtype)


def _normed_matmul(h, nw, mod, w, n_ctx, w_aux=None):
    n, d = h.shape
    nout = w.shape[1]
    tm = _pick(n, (640, 512, 256, 128))
    tn = _pick(nout, (1536, 1024, 512, 256, 128))
    in_specs = [pl.BlockSpec((tm, d), lambda i, j: (i, 0)),
                pl.BlockSpec((1, d), lambda i, j: (0, 0)),
                pl.BlockSpec((8, d), lambda i, j: (0, 0)),
                pl.BlockSpec((d, tn), lambda i, j: (0, j))]
    out_specs = pl.BlockSpec((tm, tn), lambda i, j: (i, j))
    out_shape = jax.ShapeDtypeStruct((n, nout), BF16)
    args = [h, nw.reshape(1, d), mod, w]
    if w_aux is None:
        kern = functools.partial(_nmm_kernel, n_ctx=n_ctx, tm=tm)
    else:
        na = w_aux.shape[1]
        kern = functools.partial(_nmm_aux_kernel, n_ctx=n_ctx, tm=tm)
        in_specs.append(pl.BlockSpec((d, na), lambda i, j: (0, 0)))
        out_specs = [out_specs, pl.BlockSpec((tm, na), lambda i, j: (i, 0))]
        out_shape = [out_shape, jax.ShapeDtypeStruct((n, na), F32)]
        args.append(w_aux)
    return pl.pallas_call(
        kern, grid=(n // tm, nout // tn), in_specs=in_specs, out_specs=out_specs, out_shape=out_shape,
        scratch_shapes=[pltpu.VMEM((tm, d), BF16)],
        compiler_params=_params(("parallel", "arbitrary"), 48),
        name="normed_proj",
    )(*args)


_POOL_HALO = 16


def _pool_kernel(z_ref, zp_ref, zn_ref, pw_ref, ps_ref, o_ref, ext_ref, *, n_ctx, n_all, tm):
    i = pl.program_id(0)
    row0 = i * tm
    in_ctx = row0 < n_ctx
    seq_start = jnp.where(in_ctx, 0, n_ctx)
    seq_end = jnp.where(in_ctx, n_ctx, n_all)
    has_prev = row0 > seq_start
    has_next = row0 + tm < seq_end
    hl = _POOL_HALO
    ext_ref[0:hl, :] = jnp.where(has_prev, zp_ref[...].astype(F32), 0.0)
    ext_ref[hl:hl + tm, :] = z_ref[...].astype(F32)
    ext_ref[hl + tm:hl + tm + hl, :] = jnp.where(has_next, zn_ref[...].astype(F32), 0.0)
    t = row0 - seq_start + lax.broadcasted_iota(jnp.int32, (tm, 1), 0)
    seq_len = seq_end - seq_start
    for g, w in enumerate(POOL_WINDOWS):
        c0 = g * POOL_GROUP
        acc = None
        for k in range(-(w // 2), w - w // 2):
            v = ext_ref[hl + k:hl + k + tm, c0:c0 + POOL_GROUP]
            acc = v if acc is None else acc + v
        lo = jnp.maximum(t - w // 2, 0)
        hi = jnp.minimum(t + (w - w // 2), seq_len)
        cnt = (hi - lo).astype(F32)
        dlt = acc / cnt - ext_ref[hl:hl + tm, c0:c0 + POOL_GROUP]
        y = jnp.dot(dlt.astype(BF16), pw_ref[g], preferred_element_type=F32)
        o_ref[:, c0:c0 + POOL_GROUP] = (y * ps_ref[:, c0:c0 + POOL_GROUP]).astype(o_ref.dtype)


def _pool_mixer(z, pool_w, pool_scale, n_ctx):
    n = z.shape[0]
    tm = 256
    assert n_ctx % tm == 0 and n % tm == 0
    hl = _POOL_HALO
    nh = n // hl
    per = tm // hl
    kern = functools.partial(_pool_kernel, n_ctx=n_ctx, n_all=n, tm=tm)
    return pl.pallas_call(
        kern, grid=(n // tm,),
        in_specs=[pl.BlockSpec((tm, W_POOL), lambda i: (i, 0)),
                  pl.BlockSpec((hl, W_POOL), lambda i: (jnp.maximum(i * per - 1, 0), 0)),
                  pl.BlockSpec((hl, W_POOL), lambda i: (jnp.minimum((i + 1) * per, nh - 1), 0)),
                  pl.BlockSpec((4, POOL_GROUP, POOL_GROUP), lambda i: (0, 0, 0)),
                  pl.BlockSpec((1, W_POOL), lambda i: (0, 0))],
        out_specs=pl.BlockSpec((tm, W_POOL), lambda i: (i, 0)),
        out_shape=jax.ShapeDtypeStruct((n, W_POOL), BF16),
        scratch_shapes=[pltpu.VMEM((tm + 2 * hl, W_POOL), F32)],
        compiler_params=_params(("parallel",), 32),
        name="pool_mixer",
    )(z, z, z, pool_w, pool_scale.reshape(1, W_POOL))


def _dot_hi(a, b):
    return jnp.dot(a, b, precision=HI, preferred_element_type=F32)


def _dot_t_hi(a, b):
    return lax.dot_general(a, b, (((0,), (0,)), ((), ())), precision=HI, preferred_element_type=F32)


def _s5_param_kernel(lr_ref, li_ref, lrc_ref, lic_ref, ldt_ref, br_ref, bi_ref, cr_ref, ci_ref,
                     m_ref, qre_ref, qim_ref, pre_ref, pim_ref, at_ref):
    t_len = S5_T
    tp = t_len + 8
    wid = t_len * S5_CH
    col = lax.broadcasted_iota(jnp.int32, (tp, wid), 1)
    kid = lax.broadcasted_iota(jnp.int32, (tp, wid), 0)
    cq = col >> 4
    rep_nat = (kid == cq).astype(F32)
    rep_rev = (kid == t_len - 1 - cq).astype(F32)
    til = (lax.broadcasted_iota(jnp.int32, (S5_CH, wid), 0)
           == (lax.broadcasted_iota(jnp.int32, (S5_CH, wid), 1) & (S5_CH - 1))).astype(F32)
    colb = lax.broadcasted_iota(jnp.int32, (S5_CH, wid), 1) >> 4
    kk = lax.broadcasted_iota(jnp.int32, (tp, S5_STATE), 0).astype(F32)
    rows = [None] * t_len
    for d in range(2):
        dt = jnp.exp(ldt_ref[d])
        lr, li = lr_ref[d], li_ref[d]
        mag = jnp.exp(kk * (lr * dt))
        ang = kk * (li * dt)
        pr, pi_ = mag * jnp.cos(ang), mag * jnp.sin(ang)
        lrc, lic = lrc_ref[d], lic_ref[d]
        magc = jnp.exp(lrc * dt)
        arc, aic = magc * jnp.cos(lic * dt), magc * jnp.sin(lic * dt)
        den = lrc * lrc + lic * lic
        nr, ni = arc - 1.0, aic
        kre = (nr * lrc + ni * lic) / den
        kim = (ni * lrc - nr * lic) / den
        bbr = kre * br_ref[d] - kim * bi_ref[d]
        bbi = kre * bi_ref[d] + kim * br_ref[d]
        repq = rep_rev if d == 0 else rep_nat
        repp = ((kid == cq + 1) if d == 0 else (kid == t_len - cq)).astype(F32)
        repm = rep_nat if d == 0 else rep_rev
        bt_r, bt_i = _dot_hi(bbr, til), _dot_hi(bbi, til)
        eqr, eqi = _dot_t_hi(pr, repq), _dot_t_hi(pi_, repq)
        qre_ref[d] = (eqr * bt_r - eqi * bt_i).astype(qre_ref.dtype)
        qim_ref[d] = (eqr * bt_i + eqi * bt_r).astype(qim_ref.dtype)
        epr, epi = _dot_t_hi(pr, repp), _dot_t_hi(pi_, repp)
        ct_r, ct_i = _dot_t_hi(cr_ref[d], til), _dot_t_hi(ci_ref[d], til)
        pre_ref[d] = (ct_r * epr - ct_i * epi).astype(pre_ref.dtype)
        pim_ref[d] = (-(ct_r * epi + ct_i * epr)).astype(pim_ref.dtype)
        at_ref[d, 0:1, :] = pr[t_len:t_len + 1, :]
        at_ref[d, 1:2, :] = pi_[t_len:t_len + 1, :]
        emr, emi = _dot_t_hi(pr, repm), _dot_t_hi(pi_, repm)
        y_re = emr * ct_r - emi * ct_i
        y_im = emr * ct_i + emi * ct_r
        r = _dot_t_hi(bbr, y_re) - _dot_t_hi(bbi, y_im)
        for s in range(t_len):
            if d == 0:
                blk = jnp.where(colb >= s, pltpu.roll(r, S5_CH * s, 1), 0.0)
            else:
                blk = jnp.where(colb <= s, pltpu.roll(r, (S5_CH * (s + 1)) % wid, 1), 0.0)
            rows[s] = blk if rows[s] is None else rows[s] + blk
    for s in range(t_len):
        m_ref[s * S5_CH:(s + 1) * S5_CH, :] = rows[s].astype(m_ref.dtype)


def _s5_params(lam_re, lam_im, log_dt, b_re, b_im, c_re, c_im):
    g, p, n = S5_GROUPS, S5_STATE, S5_CH
    wid = S5_T * S5_CH

    def spec(*shape):
        return pl.BlockSpec((2, None) + shape, lambda gi: (0, gi) + (0,) * len(shape))

    def ospec(*shape):
        return pl.BlockSpec((None,) + shape, lambda gi: (gi,) + (0,) * len(shape))
    return pl.pallas_call(
        _s5_param_kernel, grid=(g,),
        in_specs=[spec(1, p), spec(1, p), spec(p, 1), spec(p, 1), spec(1, 1),
                  spec(p, n), spec(p, n), spec(n, p), spec(n, p)],
        out_specs=[ospec(wid, wid), ospec(2, p, wid), ospec(2, p, wid), ospec(2, p, wid), ospec(2, p, wid),
                   ospec(2, 2, p)],
        out_shape=[jax.ShapeDtypeStruct((g, wid, wid), BF16),
                   jax.ShapeDtypeStruct((g, 2, p, wid), BF16),
                   jax.ShapeDtypeStruct((g, 2, p, wid), BF16),
                   jax.ShapeDtypeStruct((g, 2, p, wid), BF16),
                   jax.ShapeDtypeStruct((g, 2, p, wid), BF16),
                   jax.ShapeDtypeStruct((g, 2, 2, p), F32)],
        compiler_params=_params(("parallel",), 32),
        name="s5_params",
    )(lam_re.reshape(2, g, 1, p), lam_im.reshape(2, g, 1, p), lam_re.reshape(2, g, p, 1),
      lam_im.reshape(2, g, p, 1), log_dt.reshape(2, g, 1, 1), b_re, b_im, c_re, c_im)


def _s5_state_kernel(u_ref, qre_ref, qim_ref, s_ref):
    u = u_ref[...]
    dn = (((1,), (1,)), ((), ()))
    s_ref[:, 0:128] = lax.dot_general(u, qre_ref[...], dn, preferred_element_type=F32)
    s_ref[:, 128:256] = lax.dot_general(u, qim_ref[...], dn, preferred_element_type=F32)


def _s5_scan_kernel(s_ref, are_ref, aim_ref, h_ref, *, nc, ncc):
    ar, ai = are_ref[...], aim_ref[...]
    is_f = lax.broadcasted_iota(jnp.int32, ar.shape, 1) < S5_STATE

    def body(i, carry):
        hr, hi = carry
        cf = i
        cb = jnp.where(i < ncc, ncc - 1 - i, nc + ncc - 1 - i)
        h_ref[cf, :, 0:64] = hr[:, 0:64]
        h_ref[cb, :, 64:128] = hr[:, 64:128]
        h_ref[cf, :, 128:192] = hi[:, 0:64]
        h_ref[cb, :, 192:256] = hi[:, 64:128]
        sf, sb = s_ref[cf], s_ref[cb]
        sr = jnp.where(is_f, sf[:, 0:128], sb[:, 0:128])
        si = jnp.where(is_f, sf[:, 128:256], sb[:, 128:256])
        return ar * hr - ai * hi + sr, ar * hi + ai * hr + si

    zero = jnp.zeros(ar.shape, F32)
    lax.fori_loop(0, nc, body, (zero, zero))


def _s5_out_kernel(u_ref, m_ref, h_ref, p_ref, y_ref):
    y = jnp.dot(u_ref[...], m_ref[...], preferred_element_type=F32)
    y_ref[...] = y + jnp.dot(h_ref[...].astype(BF16), p_ref[...], preferred_element_type=F32)


def _s5_mixer(s_all, m, qre, qim, pre, pim, at, n_ctx):
    n = s_all.shape[0]
    g, t_len, wid = S5_GROUPS, S5_T, S5_T * S5_CH
    nc, ncc = n // t_len, n_ctx // t_len
    assert n % t_len == 0 and n_ctx % t_len == 0
    u = s_all.reshape(nc, t_len, g, S5_CH).transpose(2, 0, 1, 3).reshape(g, nc, wid)
    p_all = jnp.concatenate([pre.reshape(g, 128, wid), pim.reshape(g, 128, wid)], axis=1)
    a_re = at[:, :, 0, :].reshape(g, 128)
    a_im = at[:, :, 1, :].reshape(g, 128)
    s_t = pl.pallas_call(
        _s5_state_kernel, grid=(g,),
        in_specs=[pl.BlockSpec((None, nc, wid), lambda i: (i, 0, 0)),
                  pl.BlockSpec((None, 128, wid), lambda i: (i, 0, 0)),
                  pl.BlockSpec((None, 128, wid), lambda i: (i, 0, 0))],
        out_specs=pl.BlockSpec((nc, 256), lambda i: (0, i)),
        out_shape=jax.ShapeDtypeStruct((nc, g * 256), F32),
        compiler_params=_params(("parallel",), 32),
        name="s5_chunk_state",
    )(u, qre.reshape(g, 128, wid), qim.reshape(g, 128, wid))
    h_t = pl.pallas_call(
        functools.partial(_s5_scan_kernel, nc=nc, ncc=ncc),
        out_shape=jax.ShapeDtypeStruct((nc, g, 256), F32),
        compiler_params=pltpu.CompilerParams(vmem_limit_bytes=48 * MIB),
        name="s5_chunk_scan",
    )(s_t.reshape(nc, g, 256), a_re, a_im).reshape(nc, g * 256)
    y = pl.pallas_call(
        _s5_out_kernel, grid=(g,),
        in_specs=[pl.BlockSpec((None, nc, wid), lambda i: (i, 0, 0)),
                  pl.BlockSpec((None, wid, wid), lambda i: (i, 0, 0)),
                  pl.BlockSpec((nc, 256), lambda i: (0, i)),
                  pl.BlockSpec((None, 256, wid), lambda i: (i, 0, 0))],
        out_specs=pl.BlockSpec((None, nc, wid), lambda i: (i, 0, 0)),
        out_shape=jax.ShapeDtypeStruct((g, nc, wid), F32),
        compiler_params=_params(("parallel",), 32),
        name="s5_chunk_out",
    )(u, m, h_t, p_all)
    return y.reshape(g, nc, t_len, S5_CH).transpose(1, 2, 0, 3).reshape(n, W_S5)


def _gelu_tanh(x):
    return 0.5 * x * (1.0 + jnp.tanh(math.sqrt(2.0 / math.pi) * (x + 0.044715 * (x * x * x))))


def _even_out_kernel(pool_ref, y_ref, s_ref, dsk_ref, wglu_ref, wo_ref, h_ref, nw_ref, mod_ref, o_ref,
                     *, n_ctx, tm):
    y = _gelu_tanh(y_ref[...] + s_ref[...].astype(F32) * dsk_ref[...])
    gate = jnp.dot(y.astype(BF16), wglu_ref[...], preferred_element_type=F32)
    s5 = (y * _sigmoid(gate)).astype(BF16)
    mix = jnp.dot(pool_ref[...], wo_ref[0:W_POOL, :], preferred_element_type=F32)
    mix = mix + jnp.dot(s5, wo_ref[W_POOL:W_POOL + W_S5, :], preferred_element_type=F32)
    g = _gate_rows(mod_ref, pl.program_id(0) * tm, tm, n_ctx)
    o_ref[...] = h_ref[...] + g * _rms(mix, nw_ref[...])


def _even_out(pool_out, y_s5, z, dsk, w_glu, w_out, h, nw, mod, n_ctx):
    n, d = h.shape
    tm = _pick(n, (320, 256, 128))
    return pl.pallas_call(
        functools.partial(_even_out_kernel, n_ctx=n_ctx, tm=tm), grid=(n // tm,),
        in_specs=[pl.BlockSpec((tm, W_POOL), lambda i: (i, 0)),
                  pl.BlockSpec((tm, W_S5), lambda i: (i, 0)),
                  pl.BlockSpec((tm, W_S5), lambda i: (i, W_POOL // W_S5)),
                  pl.BlockSpec((1, W_S5), lambda i: (0, 0)),
                  pl.BlockSpec((W_S5, W_S5), lambda i: (0, 0)),
                  pl.BlockSpec((d, d), lambda i: (0, 0)),
                  pl.BlockSpec((tm, d), lambda i: (i, 0)),
                  pl.BlockSpec((1, d), lambda i: (0, 0)),
                  pl.BlockSpec((8, d), lambda i: (0, 0))],
        out_specs=pl.BlockSpec((tm, d), lambda i: (i, 0)),
        out_shape=jax.ShapeDtypeStruct((n, d), F32),
        compiler_params=_params(("parallel",), 48),
        name="even_mixer_out",
    )(pool_out, y_s5, z, dsk.reshape(1, W_S5), w_glu, w_out, h, nw.reshape(1, d), mod)


def _ffn_kernel(h_ref, nw2_ref, nw3_ref, mod_ref, wg_ref, wu_ref, wd_ref, o_ref, v_ref, *, n_ctx, tm):
    f = pl.program_id(1)

    @pl.when(f == 0)
    def _():
        _store_norm_mod(v_ref, h_ref, nw2_ref, mod_ref, pl.program_id(0) * tm, n_ctx)
        o_ref[...] = jnp.zeros_like(o_ref)
    v = v_ref[...]
    a = jnp.dot(v, wg_ref[...], preferred_element_type=F32)
    b = jnp.dot(v, wu_ref[...], preferred_element_type=F32)
    act = (a * _sigmoid(a) * b).astype(BF16)
    o_ref[...] += jnp.dot(act, wd_ref[...], preferred_element_type=F32)

    @pl.when(f == pl.num_programs(1) - 1)
    def _():
        g = _gate_rows(mod_ref, pl.program_id(0) * tm, tm, n_ctx)
        o_ref[...] = h_ref[...] + g * _rms(o_ref[...], nw3_ref[...])


def _dense_ffn(h, nw2, nw3, mod, wg, wu, wd, n_ctx):
    n, d = h.shape
    dff = wg.shape[1]
    tm = _pick(n, (640, 512, 256, 128))
    tf = 512
    return pl.pallas_call(
        functools.partial(_ffn_kernel, n_ctx=n_ctx, tm=tm), grid=(n // tm, dff // tf),
        in_specs=[pl.BlockSpec((tm, d), lambda i, f: (i, 0)),
                  pl.BlockSpec((1, d), lambda i, f: (0, 0)),
                  pl.BlockSpec((1, d), lambda i, f: (0, 0)),
                  pl.BlockSpec((8, d), lambda i, f: (0, 0)),
                  pl.BlockSpec((d, tf), lambda i, f: (0, f)),
                  pl.BlockSpec((d, tf), lambda i, f: (0, f)),
                  pl.BlockSpec((tf, d), lambda i, f: (f, 0))],
        out_specs=pl.BlockSpec((tm, d), lambda i, f: (i, 0)),
        out_shape=jax.ShapeDtypeStruct((n, d), F32),
        scratch_shapes=[pltpu.VMEM((tm, d), BF16)],
        compiler_params=_params(("parallel", "arbitrary"), 56),
        name="dense_swiglu",
    )(h, nw2.reshape(1, d), nw3.reshape(1, d), mod, wg, wu, wd)


def _split_bf16(x):
    hi = x.astype(BF16)
    return hi, (x - hi.astype(F32)).astype(BF16)


def _gla_kernel(qf_ref, kf_ref, vf_ref, af_ref, qb_ref, kb_ref, vb_ref, ab_ref, wa_ref, ba_ref,
                of_ref, ob_ref, st_ref, *, tb):
    c = GLA_CHUNK
    nsub = tb // c

    @pl.when(pl.program_id(0) == 0)
    def _():
        st_ref[...] = jnp.zeros_like(st_ref)
    ri = lax.broadcasted_iota(jnp.int32, (c, c), 0)
    ci = lax.broadcasted_iota(jnp.int32, (c, c), 1)
    scale = GLA_DK ** -0.5
    nt = (((1,), (1,)), ((), ()))
    tn = (((0,), (0,)), ((), ()))
    dirs = ((qf_ref, kf_ref, vf_ref, af_ref, of_ref), (qb_ref, kb_ref, vb_ref, ab_ref, ob_ref))
    for d, (q_ref, k_ref, v_ref, a_ref, o_ref) in enumerate(dirs):
        mask = (ri >= ci) if d == 0 else (ci >= ri)
        cum = mask.astype(BF16)
        a_hi, a_lo = _split_bf16(a_ref[...])
        w_hi, w_lo = _split_bf16(wa_ref[d])
        z = (jnp.dot(a_hi, w_hi, preferred_element_type=F32) + jnp.dot(a_hi, w_lo, preferred_element_type=F32)
             + jnp.dot(a_lo, w_hi, preferred_element_type=F32)) + ba_ref[d]
        glog = (jnp.minimum(z, 0.0) - jnp.log(1.0 + jnp.exp(-jnp.abs(z)))) * (1.0 / GLA_TAU)
        order = range(nsub) if d == 0 else range(nsub - 1, -1, -1)
        for sc in order:
            r0 = sc * c
            g_hi, g_lo = _split_bf16(glog[r0:r0 + c, :])
            b = jnp.dot(cum, g_hi, preferred_element_type=F32) + jnp.dot(cum, g_lo, preferred_element_type=F32)
            b_last = b[c - 1:c, :] if d == 0 else b[0:1, :]
            dec = jnp.exp(b_last)
            q = q_ref[r0:r0 + c, :].astype(F32) * scale
            k = k_ref[r0:r0 + c, :].astype(F32)
            qe = (q * jnp.exp(b)).astype(BF16)
            ke = (k * jnp.exp(-b)).astype(BF16)
            kd = (k * jnp.exp(b_last - b)).astype(BF16)
            for h in range(GLA_HEADS):
                ks = slice(h * GLA_DK, (h + 1) * GLA_DK)
                vs = slice(h * GLA_DV, (h + 1) * GLA_DV)
                att = lax.dot_general(qe[:, ks], ke[:, ks], nt, preferred_element_type=F32)
                att = jnp.where(mask, att, 0.0).astype(BF16)
                vh = v_ref[r0:r0 + c, vs]
                s_old = st_ref[d, h]
                o = jnp.dot(att, vh, preferred_element_type=F32)
                o = o + lax.dot_general(qe[:, ks], s_old.astype(BF16), nt, preferred_element_type=F32)
                o_ref[r0:r0 + c, vs] = o
                upd = lax.dot_general(vh, kd[:, ks], tn, preferred_element_type=F32)
                st_ref[d, h] = s_old * dec[:, ks] + upd


def _gla_mixer(zmain, a_aux, w_a, b_a, n_ctx):
    n = zmain.shape[0]
    tb = 256
    assert n % tb == 0 and n_ctx % tb == 0
    nb, ncb = n // tb, n_ctx // tb

    def bwd(s):
        return jnp.where(s < ncb, ncb - 1 - s, nb + ncb - 1 - s)
    wa = jnp.pad(w_a, ((0, 0), (0, 128 - GLA_RANK), (0, 0)))
    in_specs = []
    for order in (lambda s: s, bwd):
        in_specs += [pl.BlockSpec((tb, W_GLA_K), lambda s, o=order: (o(s), OQ // W_GLA_K)),
                     pl.BlockSpec((tb, W_GLA_K), lambda s, o=order: (o(s), OK_ // W_GLA_K)),
                     pl.BlockSpec((tb, W_GLA_V), lambda s, o=order: (o(s), OV // W_GLA_V)),
                     pl.BlockSpec((tb, 128), lambda s, o=order: (o(s), 0))]
    in_specs += [pl.BlockSpec((2, 128, W_GLA_K), lambda s: (0, 0, 0)),
                 pl.BlockSpec((2, 1, W_GLA_K), lambda s: (0, 0, 0))]
    return pl.pallas_call(
        functools.partial(_gla_kernel, tb=tb), grid=(nb,),
        in_specs=in_specs,
        out_specs=[pl.BlockSpec((tb, W_GLA_V), lambda s: (s, 0)),
                   pl.BlockSpec((tb, W_GLA_V), lambda s: (bwd(s), 0))],
        out_shape=[jax.ShapeDtypeStruct((n, W_GLA_V), F32)] * 2,
        scratch_shapes=[pltpu.VMEM((2, GLA_HEADS, GLA_DV, GLA_DK), F32)],
        compiler_params=_params(("arbitrary",), 32),
        name="gla_chunked",
    )(zmain, zmain, zmain, a_aux, zmain, zmain, zmain, a_aux, wa, b_a.reshape(2, 1, W_GLA_K))


_ATT_PAD = 2 * ATT_HEAD_DIM
_LOG2E = 1.4426950408889634
_FIXED_SHIFT_MAX_BOUND = 40.0


def _rope_fn(row0, tm, n_ctx):
    hd = ATT_HEAD_DIM
    row = row0 + lax.broadcasted_iota(jnp.int32, (tm, hd), 0)
    lane = lax.broadcasted_iota(jnp.int32, (tm, hd), 1)
    quarter = hd // 4
    first = (lane & quarter) == 0
    assert tm % GRID_W == 0 and n_ctx % GRID_W == 0
    reps = tm // GRID_W
    lane_s = lax.broadcasted_iota(jnp.int32, (GRID_W, hd), 1)
    freq = jnp.exp((lane_s % quarter).astype(F32) * (-math.log(ROPE_THETA) / quarter))
    ang_col = lax.broadcasted_iota(jnp.int32, (GRID_W, hd), 0).astype(F32) * freq
    grid_row0 = (row0 - n_ctx) // GRID_W
    ang_row = (grid_row0 + lax.broadcasted_iota(jnp.int32, (8, hd), 0)).astype(F32) * freq[0:8, :]
    cos_c, sin_c = jnp.cos(ang_col), jnp.sin(ang_col)
    cos_r, sin_r = jnp.cos(ang_row), jnp.sin(ang_row)

    def expand(tab_r, tab_c):
        by_row = jnp.concatenate([jnp.broadcast_to(tab_r[k:k + 1, :], (GRID_W, hd)) for k in range(reps)], axis=0)
        return jnp.where(lane < hd // 2, by_row, jnp.concatenate([tab_c] * reps, axis=0))
    cosv = expand(cos_r, cos_c)
    sinv = expand(sin_r, sin_c)
    sin_s = jnp.where(first, -sinv, sinv)

    def rope(x):
        sw = jnp.where(first, pltpu.roll(x, hd - quarter, 1), pltpu.roll(x, quarter, 1))
        return x * cosv + sw * sin_s
    return rope, row >= n_ctx


def _k_prep_kernel(k_ref, v_ref, kw_ref, kx_ref, vx_ref, kmax_ref, *, n_ctx, tm):
    hd = ATT_HEAD_DIM
    rope, is_lat = _rope_fn(pl.program_id(0) * tm, tm, n_ctx)
    lane = lax.broadcasted_iota(jnp.int32, (tm, hd), 1)
    one_col = jnp.where(lane == 0, 1.0, 0.0).astype(BF16)
    ones = jnp.ones((tm, hd), BF16)
    nmax = jnp.zeros((tm, 1), F32)
    for h in range(ATT_KV_HEADS):
        sl = slice(h * hd, (h + 1) * hd)
        xn = _rms(k_ref[:, sl].astype(F32), kw_ref[...])
        kr = jnp.where(is_lat, rope(xn), xn).astype(BF16)
        kx_ref[:, h * _ATT_PAD:h * _ATT_PAD + hd] = kr
        kx_ref[:, h * _ATT_PAD + hd:(h + 1) * _ATT_PAD] = one_col
        vx_ref[:, h * _ATT_PAD:h * _ATT_PAD + hd] = v_ref[:, sl]
        vx_ref[:, h * _ATT_PAD + hd:(h + 1) * _ATT_PAD] = ones
        krf = kr.astype(F32)
        nmax = jnp.maximum(nmax, jnp.sum(krf * krf, axis=-1, keepdims=True))
    kmax_ref[...] = jnp.broadcast_to(jnp.max(nmax, axis=0, keepdims=True), kmax_ref.shape)


def _q_prep_kernel(q_ref, qw_ref, kn_ref, qx_ref, bmax_ref, *, n_ctx, tm):
    hd = ATT_HEAD_DIM
    rope, _ = _rope_fn(n_ctx + pl.program_id(0) * tm, tm, n_ctx)
    lane = lax.broadcasted_iota(jnp.int32, (tm, hd), 1)
    scale = hd ** -0.5 * _LOG2E
    bmax = jnp.zeros((tm, 1), F32)
    for h in range(ATT_HEADS):
        sl = slice(h * hd, (h + 1) * hd)
        xn = _rms(q_ref[:, sl].astype(F32), qw_ref[...])
        qr = (rope(xn) * scale).astype(BF16)
        qf = qr.astype(F32)
        bound = jnp.sqrt(jnp.sum(qf * qf, axis=-1, keepdims=True)) * kn_ref[...]
        qx_ref[:, h * _ATT_PAD:h * _ATT_PAD + hd] = qr
        qx_ref[:, h * _ATT_PAD + hd:(h + 1) * _ATT_PAD] = jnp.where(lane == 0, -bound, 0.0).astype(BF16)
        bmax = jnp.maximum(bmax, bound)
    bmax_ref[...] = jnp.broadcast_to(jnp.max(bmax, axis=0, keepdims=True), bmax_ref.shape)


def _qk_prep(zmain, q_norm, k_norm, n_ctx):
    n = zmain.shape[0]
    lq = n - n_ctx
    tm = 256
    hd = ATT_HEAD_DIM
    assert n_ctx % tm == 0
    ncb = n_ctx // tm
    kx, vx, kmax = pl.pallas_call(
        functools.partial(_k_prep_kernel, n_ctx=n_ctx, tm=tm), grid=(n // tm,),
        in_specs=[pl.BlockSpec((tm, W_ATT_KV), lambda i: (i, OAK // W_ATT_KV)),
                  pl.BlockSpec((tm, W_ATT_KV), lambda i: (i, OAV // W_ATT_KV)),
                  pl.BlockSpec((1, hd), lambda i: (0, 0))],
        out_specs=[pl.BlockSpec((tm, ATT_KV_HEADS * _ATT_PAD), lambda i: (i, 0)),
                   pl.BlockSpec((tm, ATT_KV_HEADS * _ATT_PAD), lambda i: (i, 0)),
                   pl.BlockSpec((None, 8, 128), lambda i: (i, 0, 0))],
        out_shape=[jax.ShapeDtypeStruct((n, ATT_KV_HEADS * _ATT_PAD), BF16),
                   jax.ShapeDtypeStruct((n, ATT_KV_HEADS * _ATT_PAD), BF16),
                   jax.ShapeDtypeStruct((n // tm, 8, 128), F32)],
        compiler_params=_params(("parallel",), 32),
        name="k_norm_rope",
    )(zmain, zmain, k_norm.reshape(1, hd))
    knorm = jnp.sqrt(jnp.max(kmax)).reshape(1, 1)
    qx, bmax = pl.pallas_call(
        functools.partial(_q_prep_kernel, n_ctx=n_ctx, tm=tm), grid=(lq // tm,),
        in_specs=[pl.BlockSpec((tm, W_ATT), lambda i: (i + ncb, OAQ // W_ATT)),
                  pl.BlockSpec((1, hd), lambda i: (0, 0)),
                  pl.BlockSpec((1, 1), lambda i: (0, 0))],
        out_specs=[pl.BlockSpec((tm, ATT_HEADS * _ATT_PAD), lambda i: (i, 0)),
                   pl.BlockSpec((None, 8, 128), lambda i: (i, 0, 0))],
        out_shape=[jax.ShapeDtypeStruct((lq, ATT_HEADS * _ATT_PAD), BF16),
                   jax.ShapeDtypeStruct((lq // tm, 8, 128), F32)],
        compiler_params=_params(("parallel",), 32),
        name="q_norm_rope",
    )(zmain, q_norm.reshape(1, hd), knorm)
    return qx, kx, vx, jnp.max(bmax)


def _flash_fixed_kernel(q_ref, kt_ref, v_ref, o_ref, acc_ref):
    c = pl.program_id(2)
    hd = ATT_HEAD_DIM

    @pl.when(c == 0)
    def _():
        acc_ref[...] = jnp.zeros_like(acc_ref)
    kt = kt_ref[...]
    v = v_ref[...]
    for g in range(ATT_GROUP):
        s = jnp.dot(q_ref[:, g * _ATT_PAD:(g + 1) * _ATT_PAD], kt, preferred_element_type=F32)
        acc_ref[g] += jnp.dot(jnp.exp2(s).astype(BF16), v, preferred_element_type=F32)

    @pl.when(c == pl.num_programs(2) - 1)
    def _():
        for g in range(ATT_GROUP):
            a = acc_ref[g]
            o_ref[:, g * hd:(g + 1) * hd] = (a[:, 0:hd] / a[:, hd:2 * hd]).astype(o_ref.dtype)


def _flash_online_kernel(q_ref, kt_ref, v_ref, o_ref, m_ref, acc_ref):
    c = pl.program_id(2)
    hd = ATT_HEAD_DIM

    @pl.when(c == 0)
    def _():
        m_ref[...] = jnp.full(m_ref.shape, -jnp.inf, F32)
        acc_ref[...] = jnp.zeros_like(acc_ref)
    kt = kt_ref[...]
    v = v_ref[...]
    for g in range(ATT_GROUP):
        s = jnp.dot(q_ref[:, g * _ATT_PAD:(g + 1) * _ATT_PAD], kt, preferred_element_type=F32)
        m_prev = m_ref[g]
        m_new = jnp.maximum(m_prev, jnp.max(s, axis=-1, keepdims=True))
        p = jnp.exp2(s - m_new).astype(BF16)
        acc_ref[g] = jnp.exp2(m_prev - m_new) * acc_ref[g] + jnp.dot(p, v, preferred_element_type=F32)
        m_ref[g] = m_new

    @pl.when(c == pl.num_programs(2) - 1)
    def _():
        for g in range(ATT_GROUP):
            a = acc_ref[g]
            o_ref[:, g * hd:(g + 1) * hd] = (a[:, 0:hd] / a[:, hd:2 * hd]).astype(o_ref.dtype)


def _attention(qx, kx, vx, bound_log2):
    lq = qx.shape[0]
    n = kx.shape[0]
    kt = kx.T
    gw = ATT_GROUP * _ATT_PAD
    ow = ATT_GROUP * ATT_HEAD_DIM
    def call(kern, tq, tk, scratch, name):
        return pl.pallas_call(
            kern, grid=(ATT_KV_HEADS, lq // tq, n // tk),
            in_specs=[pl.BlockSpec((tq, gw), lambda h, i, j: (i, h)),
                      pl.BlockSpec((_ATT_PAD, tk), lambda h, i, j: (h, j)),
                      pl.BlockSpec((tk, _ATT_PAD), lambda h, i, j: (j, h))],
            out_specs=pl.BlockSpec((tq, ow), lambda h, i, j: (i, h)),
            out_shape=jax.ShapeDtypeStruct((lq, W_ATT), BF16),
            scratch_shapes=scratch(tq) + [pltpu.VMEM((ATT_GROUP, tq, _ATT_PAD), F32)],
            compiler_params=_params(("parallel", "parallel", "arbitrary"), 56),
            name=name,
        )(qx, kt, vx)

    def fixed(_):
        return call(_flash_fixed_kernel, _pick(lq, (1024, 512, 256, 128)), _pick(n, (3328, 1280, 512, 256)),
                    lambda tq: [], "gqa_flash_fixed_shift")

    def online(_):
        return call(_flash_online_kernel, _pick(lq, (512, 256, 128)), _pick(n, (640, 512, 256, 128)),
                    lambda tq: [pltpu.VMEM((ATT_GROUP, tq, 1), F32)], "gqa_flash_online")
    return lax.cond(bound_log2 <= _FIXED_SHIFT_MAX_BOUND * _LOG2E, fixed, online, None)


def _odd_out_kernel(of_ref, ob_ref, r_ref, att_ref, gn_ref, wo_ref, h_ref, nw_ref, mod_ref, o_ref):
    o = of_ref[...] + ob_ref[...]
    r = r_ref[...].astype(F32)
    silu_r = r * _sigmoid(r)
    parts = []
    for h in range(GLA_HEADS):
        vs = slice(h * GLA_DV, (h + 1) * GLA_DV)
        parts.append((_rms(o[:, vs], gn_ref[:, vs]) * silu_r[:, vs]).astype(BF16))
    gla = jnp.concatenate(parts, axis=1)
    mix = jnp.dot(gla, wo_ref[0:W_GLA_V, :], preferred_element_type=F32)
    mix = mix + jnp.dot(att_ref[...], wo_ref[W_GLA_V:W_GLA_V + W_ATT, :], preferred_element_type=F32)
    o_ref[...] = h_ref[...] + mod_ref[4:5, :] * _rms(mix, nw_ref[...])


def _odd_out(o_f, o_b, zmain, att, gla_norm, w_out, h_all, nw, mod, n_ctx):
    n, d = h_all.shape
    lq = n - n_ctx
    tm = _pick(lq, (256, 128))
    assert n_ctx % tm == 0
    off = n_ctx // tm
    return pl.pallas_call(
        _odd_out_kernel, grid=(lq // tm,),
        in_specs=[pl.BlockSpec((tm, W_GLA_V), lambda i: (i + off, 0)),
                  pl.BlockSpec((tm, W_GLA_V), lambda i: (i + off, 0)),
                  pl.BlockSpec((tm, W_GLA_V), lambda i: (i + off, OR // W_GLA_V)),
                  pl.BlockSpec((tm, W_ATT), lambda i: (i, 0)),
                  pl.BlockSpec((1, W_GLA_V), lambda i: (0, 0)),
                  pl.BlockSpec((d, d), lambda i: (0, 0)),
                  pl.BlockSpec((tm, d), lambda i: (i + off, 0)),
                  pl.BlockSpec((1, d), lambda i: (0, 0)),
                  pl.BlockSpec((8, d), lambda i: (0, 0))],
        out_specs=pl.BlockSpec((tm, d), lambda i: (i, 0)),
        out_shape=jax.ShapeDtypeStruct((lq, d), F32),
        compiler_params=_params(("parallel",), 48),
        name="odd_mixer_out",
    )(o_f, o_b, zmain, att, gla_norm.reshape(1, W_GLA_V), w_out, h_all, nw.reshape(1, d), mod)


def _router_kernel(h_ref, nw_ref, mod_ref, wr_ref, br_ref, v_ref, r_ref):
    x = h_ref[...]
    v = _rms(x, nw_ref[...]) * (1.0 + mod_ref[0:1, :]) + mod_ref[1:2, :]
    v_ref[...] = v
    logits = _dot_hi(v, wr_ref[...]) + br_ref[...]
    lane = lax.broadcasted_iota(jnp.int32, logits.shape, 1)
    m1 = jnp.max(logits, axis=-1, keepdims=True)
    i1 = jnp.min(jnp.where(logits == m1, lane, 128), axis=-1, keepdims=True)
    rest = jnp.where(lane == i1, -jnp.inf, logits)
    m2 = jnp.max(rest, axis=-1, keepdims=True)
    i2 = jnp.min(jnp.where(rest == m2, lane, 128), axis=-1, keepdims=True)
    e2 = jnp.exp(m2 - m1)
    g1 = 1.0 / (1.0 + e2)
    g2 = e2 / (1.0 + e2)
    out = jnp.where(lane == 0, i1.astype(F32), 0.0)
    out = jnp.where(lane == 1, i2.astype(F32), out)
    out = jnp.where(lane == 2, g1, out)
    out = jnp.where(lane == 3, g2, out)
    r_ref[...] = out


def _router(h_lat, nw, mod, w_router, b_router):
    lq, d = h_lat.shape
    tm = _pick(lq, (512, 256, 128))
    wr = jnp.pad(w_router, ((0, 0), (0, 128 - N_EXPERTS)))
    br = jnp.pad(b_router, (0, 128 - N_EXPERTS), constant_values=-1e30).reshape(1, 128)
    return pl.pallas_call(
        _router_kernel, grid=(lq // tm,),
        in_specs=[pl.BlockSpec((tm, d), lambda i: (i, 0)),
                  pl.BlockSpec((1, d), lambda i: (0, 0)),
                  pl.BlockSpec((8, d), lambda i: (0, 0)),
                  pl.BlockSpec((d, 128), lambda i: (0, 0)),
                  pl.BlockSpec((1, 128), lambda i: (0, 0))],
        out_specs=[pl.BlockSpec((tm, d), lambda i: (i, 0)), pl.BlockSpec((tm, 128), lambda i: (i, 0))],
        out_shape=[jax.ShapeDtypeStruct((lq, d), F32), jax.ShapeDtypeStruct((lq, 128), F32)],
        compiler_params=_params(("parallel",), 40),
        name="router_top2",
    )(h_lat, nw.reshape(1, d), mod, wr, br)


_TILE_PARTS = 4


def _expert_kernel(te_ref, tv_ref, src_ref, dst_ref, v_hbm, wg_ref, wu_ref, wd_ref, y_hbm,
                   xg_ref, xb_ref, ys_ref, acc_ref, gsem, ssem, *, tm, nt, nf):
    j = pl.program_id(0)
    f = pl.program_id(1)
    slot = j % 2
    per_step = tm // nf
    lo = f * per_step
    valid = tv_ref[j] != 0

    def start_gather(tile, slot_, row):
        tok = src_ref[tile * tm + row]
        pltpu.make_async_copy(v_hbm.at[pl.ds(tok, 1)], xg_ref.at[slot_, pl.ds(row, 1)], gsem.at[slot_]).start()

    def start_scatter(row):
        dst = dst_ref[j * tm + row]
        pltpu.make_async_copy(ys_ref.at[1 - slot, pl.ds(row, 1)], y_hbm.at[pl.ds(dst, 1)],
                              ssem.at[1 - slot]).start()

    def wait_scatter(slot_):
        pltpu.make_async_copy(ys_ref.at[slot_], y_hbm.at[pl.ds(0, tm)], ssem.at[slot_]).wait()

    def gather_loop(tile, slot_, first, count):
        def body(r, carry):
            start_gather(tile, slot_, first + r)
            return carry
        lax.fori_loop(0, count, body, 0, unroll=8)

    @pl.when((j == 0) & (f == 0))
    def _():
        gather_loop(0, 0, 0, tm)
        ys_ref[1] = jnp.zeros((tm, ys_ref.shape[2]), F32)

    @pl.when(f == 0)
    def _():
        pltpu.make_async_copy(v_hbm.at[pl.ds(0, tm)], xg_ref.at[slot], gsem.at[slot]).wait()
        xb_ref[...] = xg_ref[slot].astype(BF16)
        acc_ref[...] = jnp.zeros_like(acc_ref)
        for row in range(nf * per_step, tm):
            start_scatter(row)

        @pl.when(j + 1 < nt)
        def _():
            for row in range(nf * per_step, tm):
                start_gather(j + 1, 1 - slot, row)

    def compute(rows_used):
        half = per_step // 2

        def gathers(first, last, after):
            for r in range(first, last):
                start_gather(j + 1, 1 - slot, lo + r + after)

        def scatters(first, last, after):
            for r in range(first, last):
                start_scatter(lo + r + after)

        def zero_after(val):
            bits = pltpu.bitcast(val[0:8, 0:128], jnp.int32)
            return ((bits & 1) >> 1)[0, 0]

        gathers(0, half, 0)
        x = xb_ref[0:rows_used, :]
        a = jnp.dot(x, wg_ref[...], preferred_element_type=F32)
        gathers(half, per_step, zero_after(a))
        b = jnp.dot(x, wu_ref[...], preferred_element_type=F32)
        scatters(0, half, zero_after(b))
        act = (a * _sigmoid(a) * b).astype(BF16)
        down = jnp.dot(act, wd_ref[...], preferred_element_type=F32)
        scatters(half, per_step, zero_after(down))
        acc_ref[0:rows_used, :] += down

    for quarters in range(1, _TILE_PARTS + 1):
        pl.when(tv_ref[j] == quarters)(functools.partial(compute, quarters * tm // _TILE_PARTS))

    @pl.when(jnp.logical_not(valid))
    def _():
        @pl.when(j + 1 < nt)
        def _():
            gather_loop(j + 1, 1 - slot, lo, per_step)

        def body(r, carry):
            start_scatter(lo + r)
            return carry
        lax.fori_loop(0, per_step, body, 0, unroll=8)

    @pl.when(f == nf - 1)
    def _():
        @pl.when(j > 0)
        def _():
            wait_scatter(slot)
        ys_ref[slot] = acc_ref[...]

        @pl.when(j == nt - 1)
        def _():
            wait_scatter(1 - slot)


def _combine_kernel(y0_ref, y1_ref, route_ref, h_ref, nw_ref, mod_ref, o_ref):
    route = route_ref[...]
    f = y0_ref[...] * route[:, 2:3] + y1_ref[...] * route[:, 3:4]
    o_ref[...] = h_ref[...] + mod_ref[4:5, :] * _rms(f, nw_ref[...])


def _count_le(sorted_vals, queries):
    return jnp.sum((sorted_vals[None, :] <= queries[:, None]).astype(jnp.int32), axis=1)


def _moe(h_lat, nw2, nw3, mod, w_router, b_router, wg, wu, wd):
    lq, d = h_lat.shape
    e = N_EXPERTS
    dff = wg.shape[2]
    tm = _pick(lq, (512, 256))
    tf = 1024
    v, route = _router(h_lat, nw2, mod, w_router, b_router)
    idx = route[:, 0:2].astype(jnp.int32)

    npairs = 2 * lq
    nt = npairs // tm + e + 1
    flat_e = idx.reshape(-1)
    pair_id = jnp.arange(npairs, dtype=jnp.int32)
    onehot = (flat_e[:, None] == jnp.arange(e, dtype=jnp.int32)[None, :]).astype(jnp.int32)
    csum = jnp.cumsum(onehot, axis=0)
    rank = jnp.sum(csum * onehot, axis=1) - 1
    counts = csum[-1]
    padded = ((counts + tm - 1) // tm) * tm
    ends = jnp.cumsum(padded)
    starts = ends - padded
    pos = jnp.sum(onehot * starts[None, :], axis=1) + rank
    sorted_pair = jnp.full((nt * tm,), -1, jnp.int32).at[pos].set(pair_id)
    is_pad = sorted_pair < 0
    src_row = jnp.where(is_pad, 0, sorted_pair >> 1).astype(jnp.int32)
    pad_rank = jnp.cumsum(is_pad.astype(jnp.int32)) - 1
    dst_row = jnp.where(is_pad, npairs + pad_rank, (sorted_pair & 1) * lq + (sorted_pair >> 1)).astype(jnp.int32)
    dst_row = jnp.concatenate([(nt - 1) * tm + jnp.arange(tm, dtype=jnp.int32), dst_row])
    n_used = ends[-1] // tm
    tile_ids = jnp.arange(nt, dtype=jnp.int32)
    tile_exp = jnp.minimum(_count_le(ends, jnp.minimum(tile_ids, n_used - 1) * tm), e - 1)
    onehot_t = (tile_exp[:, None] == jnp.arange(e, dtype=jnp.int32)[None, :]).astype(jnp.int32)
    real_end = jnp.sum(onehot_t * (starts + counts)[None, :], axis=1)
    tile_rows = jnp.clip(real_end - tile_ids * tm, 0, tm)
    part = tm // _TILE_PARTS
    tile_valid = jnp.where(tile_ids < n_used, (tile_rows + part - 1) // part, 0).astype(jnp.int32)

    nf = dff // tf
    pinned = lambda j, f, tv: jnp.where(tv[j] != 0, f, nf - 1)
    y = pl.pallas_call(
        functools.partial(_expert_kernel, tm=tm, nt=nt, nf=nf),
        grid_spec=pltpu.PrefetchScalarGridSpec(
            num_scalar_prefetch=4, grid=(nt, nf),
            in_specs=[pl.BlockSpec(memory_space=pl.ANY),
                      pl.BlockSpec((None, d, tf), lambda j, f, te, tv, src, dst: (te[j], 0, pinned(j, f, tv))),
                      pl.BlockSpec((None, d, tf), lambda j, f, te, tv, src, dst: (te[j], 0, pinned(j, f, tv))),
                      pl.BlockSpec((None, tf, d), lambda j, f, te, tv, src, dst: (te[j], pinned(j, f, tv), 0))],
            out_specs=pl.BlockSpec(memory_space=pl.ANY),
            scratch_shapes=[pltpu.VMEM((2, tm, d), F32), pltpu.VMEM((tm, d), BF16), pltpu.VMEM((2, tm, d), F32),
                            pltpu.VMEM((tm, d), F32), pltpu.SemaphoreType.DMA((2,)),
                            pltpu.SemaphoreType.DMA((2,))]),
        out_shape=jax.ShapeDtypeStruct((nt * tm, d), F32),
        compiler_params=pltpu.CompilerParams(dimension_semantics=("arbitrary", "arbitrary"),
                                             vmem_limit_bytes=56 * MIB, disable_bounds_checks=True),
        name="moe_experts",
    )(tile_exp, tile_valid, src_row, dst_row, v, wg, wu, wd)

    tc = _pick(lq, (256, 128))
    return pl.pallas_call(
        _combine_kernel, grid=(lq // tc,),
        in_specs=[pl.BlockSpec((tc, d), lambda i: (i, 0)),
                  pl.BlockSpec((tc, d), lambda i: (i + lq // tc, 0)),
                  pl.BlockSpec((tc, 128), lambda i: (i, 0)),
                  pl.BlockSpec((tc, d), lambda i: (i, 0)),
                  pl.BlockSpec((1, d), lambda i: (0, 0)),
                  pl.BlockSpec((8, d), lambda i: (0, 0))],
        out_specs=pl.BlockSpec((tc, d), lambda i: (i, 0)),
        out_shape=jax.ShapeDtypeStruct((lq, d), F32),
        compiler_params=_params(("parallel",), 32),
        name="moe_combine",
    )(y, y, route, h_lat, nw3.reshape(1, d), mod)


def kernel(x, c, ctx, c_ctx, w_mod, b_mod, norms, e_w_in, e_pool_w, e_pool_scale, e_s5_lam_re, e_s5_lam_im, e_s5_log_dt, e_s5_b_re, e_s5_b_im, e_s5_c_re, e_s5_c_im, e_s5_d, e_s5_w_glu, e_w_out, e_ffn_gate, e_ffn_up, e_ffn_down, o_w_in, o_gla_w_a, o_gla_b_a, o_gla_norm, o_q_norm, o_k_norm, o_w_out, o_router, o_router_b, o_moe_gate, o_moe_up, o_moe_down):
    assert x.shape[0] == 1 and w_mod.shape[0] == 2 and e_w_in.shape[0] == 1 and o_w_in.shape[0] == 1
    d = x.shape[2]
    n_ctx = ctx.shape[1]
    h = jnp.concatenate([ctx[0], x[0]], axis=0)

    vecs = jnp.zeros((8, d), F32).at[0].set(c[0]).at[1].set(c_ctx)
    mods = _modulation(vecs, w_mod, b_mod)

    mod1, mod2 = _mod_rows(mods[0], 0), _mod_rows(mods[0], 1)
    z = _normed_matmul(h, norms[0, 0], mod1, e_w_in[0].astype(BF16), n_ctx)
    pool_out = _pool_mixer(z, e_pool_w[0].astype(BF16), e_pool_scale[0], n_ctx)
    m_op, qre, qim, pre, pim, at = _s5_params(e_s5_lam_re[0], e_s5_lam_im[0], e_s5_log_dt[0],
                                              e_s5_b_re[0], e_s5_b_im[0], e_s5_c_re[0], e_s5_c_im[0])
    y_s5 = _s5_mixer(z[:, W_POOL:], m_op, qre, qim, pre, pim, at, n_ctx)
    h = _even_out(pool_out, y_s5, z, e_s5_d[0], e_s5_w_glu[0].astype(BF16), e_w_out[0].astype(BF16),
                  h, norms[0, 1], mod1, n_ctx)
    h = _dense_ffn(h, norms[0, 2], norms[0, 3], mod2, e_ffn_gate[0].astype(BF16),
                   e_ffn_up[0].astype(BF16), e_ffn_down[0].astype(BF16), n_ctx)

    mod1, mod2 = _mod_rows(mods[1], 0), _mod_rows(mods[1], 1)
    w_in = o_w_in[0]
    a0 = 2 * W_GLA_K + W_GLA_V
    w_main = jnp.concatenate([w_in[:, :a0], w_in[:, a0 + GLA_RANK:]], axis=1).astype(BF16)
    w_aux = jnp.pad(w_in[:, a0:a0 + GLA_RANK], ((0, 0), (0, 128 - GLA_RANK))).astype(BF16)
    zmain, a_aux = _normed_matmul(h, norms[1, 0], mod1, w_main, n_ctx, w_aux=w_aux)
    o_f, o_b = _gla_mixer(zmain, a_aux, o_gla_w_a[0], o_gla_b_a[0], n_ctx)
    qx, kx, vx, bound_log2 = _qk_prep(zmain, o_q_norm[0], o_k_norm[0], n_ctx)
    att = _attention(qx, kx, vx, bound_log2)
    h_lat = _odd_out(o_f, o_b, zmain, att, o_gla_norm[0], o_w_out[0].astype(BF16), h, norms[1, 1], mod1, n_ctx)
    out = _moe(h_lat, norms[1, 2], norms[1, 3], mod2, o_router[0], o_router_b[0],
               o_moe_gate[0].astype(BF16), o_moe_up[0].astype(BF16), o_moe_down[0].astype(BF16))
    return out[None]
```

```python
import functools
import math

import jax
import jax.numpy as jnp
from jax import lax
from jax.experimental import pallas as pl
from jax.experimental.pallas import tpu as pltpu

F32 = jnp.float32
BF16 = jnp.bfloat16
HI = lax.Precision.HIGHEST
EPS = 1e-6

D_MODEL = 2048
GRID_W = 64
N_MOD = 6

POOL_WINDOWS = (2, 4, 8, 16)
POOL_GROUP = 384
W_POOL = 1536
W_S5 = 512
S5_CH = 16
S5_STATE = 64
S5_GROUPS = 32
S5_T = 32

GLA_HEADS = 4
GLA_DK = 128
GLA_DV = 256
GLA_RANK = 16
GLA_TAU = 16.0
GLA_CHUNK = 64
W_GLA_K = 512
W_GLA_V = 1024
ATT_HEAD_DIM = 128
ATT_HEADS = 8
ATT_KV_HEADS = 2
ATT_GROUP = 4
W_ATT = 1024
W_ATT_KV = 256
ROPE_THETA = 10000.0

D_FF = 7168
N_EXPERTS = 8

OQ, OK_, OV, OR, OAQ, OAK, OAV = 0, 512, 1024, 2048, 3072, 4096, 4352
W_ODD_MAIN = 4608

MIB = 2 ** 20


def _params(sem, vmem_mib):
    return pltpu.CompilerParams(dimension_semantics=sem, vmem_limit_bytes=vmem_mib * MIB)


def _pick(n, cands):
    for c in cands:
        if n % c == 0:
            return c
    raise ValueError(f"no tile for {n} in {cands}")


def _sigmoid(x):
    return 1.0 / (1.0 + jnp.exp(-x))


def _rms(x, w):
    return x * lax.rsqrt(jnp.mean(x * x, axis=-1, keepdims=True) + EPS) * w


_NORM_CHUNK = 16


def _store_norm_mod(dst_ref, h_ref, nw_ref, mod_ref, row0, n_ctx):
    tm = h_ref.shape[0]
    nw = nw_ref[...]
    w_lat = nw * (1.0 + mod_ref[0:1, :])
    sh_lat = mod_ref[1:2, :]

    def rows(c):
        r0 = pl.multiple_of(c * _NORM_CHUNK, _NORM_CHUNK)
        x = h_ref[pl.ds(r0, _NORM_CHUNK), :]
        return r0, x * lax.rsqrt(jnp.mean(x * x, axis=-1, keepdims=True) + EPS)

    @pl.when(row0 >= n_ctx)
    def _():
        def body(c, carry):
            r0, xr = rows(c)
            dst_ref[pl.ds(r0, _NORM_CHUNK), :] = (xr * w_lat + sh_lat).astype(dst_ref.dtype)
            return carry
        lax.fori_loop(0, tm // _NORM_CHUNK, body, 0, unroll=4)

    @pl.when(row0 < n_ctx)
    def _():
        w_ctx = nw * (1.0 + mod_ref[2:3, :])
        sh_ctx = mod_ref[3:4, :]

        def body(c, carry):
            r0, xr = rows(c)
            is_ctx = row0 + r0 + lax.broadcasted_iota(jnp.int32, (_NORM_CHUNK, 1), 0) < n_ctx
            u = xr * jnp.where(is_ctx, w_ctx, w_lat) + jnp.where(is_ctx, sh_ctx, sh_lat)
            dst_ref[pl.ds(r0, _NORM_CHUNK), :] = u.astype(dst_ref.dtype)
            return carry
        lax.fori_loop(0, tm // _NORM_CHUNK, body, 0, unroll=4)


def _gate_rows(mod_ref, row0, tm, n_ctx):
    rows = row0 + lax.broadcasted_iota(jnp.int32, (tm, 1), 0)
    return jnp.where(rows < n_ctx, mod_ref[5:6, :], mod_ref[4:5, :])


def _mod_kernel(v_ref, w_ref, b_ref, o_ref):
    v = v_ref[...]
    s = v * _sigmoid(v)
    o_ref[...] = jnp.dot(s, w_ref[...], precision=HI, preferred_element_type=F32) + b_ref[...]


def _modulation(vecs, w_mod, b_mod):
    depth, d, n6 = w_mod.shape
    tn = 1024
    return pl.pallas_call(
        _mod_kernel,
        grid=(depth, n6 // tn),
        in_specs=[pl.BlockSpec((8, d), lambda l, j: (0, 0)),
                  pl.BlockSpec((None, d, tn), lambda l, j: (l, 0, j)),
                  pl.BlockSpec((None, 1, tn), lambda l, j: (l, 0, j))],
        out_specs=pl.BlockSpec((None, 8, tn), lambda l, j: (l, 0, j)),
        out_shape=jax.ShapeDtypeStruct((depth, 8, n6), F32),
        compiler_params=_params(("parallel", "parallel"), 40),
        name="modulation",
    )(vecs, w_mod, b_mod.reshape(depth, 1, n6))


def _mod_rows(m, sub):
    d = m.shape[1] // N_MOD
    m6 = m.reshape(8, N_MOD, d)
    sh, sc, g = m6[:, 3 * sub + 0], m6[:, 3 * sub + 1], m6[:, 3 * sub + 2]
    z = jnp.zeros((d,), F32)
    return jnp.stack([sc[0], sh[0], sc[1], sh[1], g[0], g[1], z, z])


def _nmm_kernel(h_ref, nw_ref, mod_ref, w_ref, o_ref, u_ref, *, n_ctx, tm):
    @pl.when(pl.program_id(1) == 0)
    def _():
        _store_norm_mod(u_ref, h_ref, nw_ref, mod_ref, pl.program_id(0) * tm, n_ctx)
    o_ref[...] = jnp.dot(u_ref[...], w_ref[...], preferred_element_type=F32).astype(o_ref.dtype)


def _nmm_aux_kernel(h_ref, nw_ref, mod_ref, w_ref, wa_ref, o_ref, oa_ref, u_ref, *, n_ctx, tm):
    @pl.when(pl.program_id(1) == 0)
    def _():
        _store_norm_mod(u_ref, h_ref, nw_ref, mod_ref, pl.program_id(0) * tm, n_ctx)
        oa_ref[...] = jnp.dot(u_ref[...], wa_ref[...], preferred_element_type=F32)
    o_ref[...] = jnp.dot(u_ref[...], w_ref[...], preferred_element_type=F32).astype(o_ref.dtype)


def _normed_matmul(h, nw, mod, w, n_ctx, w_aux=None):
    n, d = h.shape
    nout = w.shape[1]
    tm = _pick(n, (640, 512, 256, 128))
    tn = _pick(nout, (1536, 1024, 512, 256, 128))
    in_specs = [pl.BlockSpec((tm, d), lambda i, j: (i, 0)),
                pl.BlockSpec((1, d), lambda i, j: (0, 0)),
                pl.BlockSpec((8, d), lambda i, j: (0, 0)),
                pl.BlockSpec((d, tn), lambda i, j: (0, j))]
    out_specs = pl.BlockSpec((tm, tn), lambda i, j: (i, j))
    out_shape = jax.ShapeDtypeStruct((n, nout), BF16)
    args = [h, nw.reshape(1, d), mod, w]
    if w_aux is None:
        kern = functools.partial(_nmm_kernel, n_ctx=n_ctx, tm=tm)
    else:
        na = w_aux.shape[1]
        kern = functools.partial(_nmm_aux_kernel, n_ctx=n_ctx, tm=tm)
        in_specs.append(pl.BlockSpec((d, na), lambda i, j: (0, 0)))
        out_specs = [out_specs, pl.BlockSpec((tm, na), lambda i, j: (i, 0))]
        out_shape = [out_shape, jax.ShapeDtypeStruct((n, na), F32)]
        args.append(w_aux)
    return pl.pallas_call(
        kern, grid=(n // tm, nout // tn), in_specs=in_specs, out_specs=out_specs, out_shape=out_shape,
        scratch_shapes=[pltpu.VMEM((tm, d), BF16)],
        compiler_params=_params(("parallel", "arbitrary"), 48),
        name="normed_proj",
    )(*args)


_POOL_HALO = 16


def _pool_kernel(z_ref, zp_ref, zn_ref, pw_ref, ps_ref, o_ref, ext_ref, *, n_ctx, n_all, tm):
    i = pl.program_id(0)
    row0 = i * tm
    in_ctx = row0 < n_ctx
    seq_start = jnp.where(in_ctx, 0, n_ctx)
    seq_end = jnp.where(in_ctx, n_ctx, n_all)
    has_prev = row0 > seq_start
    has_next = row0 + tm < seq_end
    hl = _POOL_HALO
    ext_ref[0:hl, :] = jnp.where(has_prev, zp_ref[...].astype(F32), 0.0)
    ext_ref[hl:hl + tm, :] = z_ref[...].astype(F32)
    ext_ref[hl + tm:hl + tm + hl, :] = jnp.where(has_next, zn_ref[...].astype(F32), 0.0)
    t = row0 - seq_start + lax.broadcasted_iota(jnp.int32, (tm, 1), 0)
    seq_len = seq_end - seq_start
    for g, w in enumerate(POOL_WINDOWS):
        c0 = g * POOL_GROUP
        acc = None
        for k in range(-(w // 2), w - w // 2):
            v = ext_ref[hl + k:hl + k + tm, c0:c0 + POOL_GROUP]
            acc = v if acc is None else acc + v
        lo = jnp.maximum(t - w // 2, 0)
        hi = jnp.minimum(t + (w - w // 2), seq_len)
        cnt = (hi - lo).astype(F32)
        dlt = acc / cnt - ext_ref[hl:hl + tm, c0:c0 + POOL_GROUP]
        y = jnp.dot(dlt.astype(BF16), pw_ref[g], preferred_element_type=F32)
        o_ref[:, c0:c0 + POOL_GROUP] = (y * ps_ref[:, c0:c0 + POOL_GROUP]).astype(o_ref.dtype)


def _pool_mixer(z, pool_w, pool_scale, n_ctx):
    n = z.shape[0]
    tm = 256
    assert n_ctx % tm == 0 and n % tm == 0
    hl = _POOL_HALO
    nh = n // hl
    per = tm // hl
    kern = functools.partial(_pool_kernel, n_ctx=n_ctx, n_all=n, tm=tm)
    return pl.pallas_call(
        kern, grid=(n // tm,),
        in_specs=[pl.BlockSpec((tm, W_POOL), lambda i: (i, 0)),
                  pl.BlockSpec((hl, W_POOL), lambda i: (jnp.maximum(i * per - 1, 0), 0)),
                  pl.BlockSpec((hl, W_POOL), lambda i: (jnp.minimum((i + 1) * per, nh - 1), 0)),
                  pl.BlockSpec((4, POOL_GROUP, POOL_GROUP), lambda i: (0, 0, 0)),
                  pl.BlockSpec((1, W_POOL), lambda i: (0, 0))],
        out_specs=pl.BlockSpec((tm, W_POOL), lambda i: (i, 0)),
        out_shape=jax.ShapeDtypeStruct((n, W_POOL), BF16),
        scratch_shapes=[pltpu.VMEM((tm + 2 * hl, W_POOL), F32)],
        compiler_params=_params(("parallel",), 32),
        name="pool_mixer",
    )(z, z, z, pool_w, pool_scale.reshape(1, W_POOL))


def _dot_hi(a, b):
    return jnp.dot(a, b, precision=HI, preferred_element_type=F32)


def _dot_t_hi(a, b):
    return lax.dot_general(a, b, (((0,), (0,)), ((), ())), precision=HI, preferred_element_type=F32)


def _s5_param_kernel(lr_ref, li_ref, lrc_ref, lic_ref, ldt_ref, br_ref, bi_ref, cr_ref, ci_ref,
                     m_ref, qre_ref, qim_ref, pre_ref, pim_ref, at_ref):
    t_len = S5_T
    tp = t_len + 8
    wid = t_len * S5_CH
    col = lax.broadcasted_iota(jnp.int32, (tp, wid), 1)
    kid = lax.broadcasted_iota(jnp.int32, (tp, wid), 0)
    cq = col >> 4
    rep_nat = (kid == cq).astype(F32)
    rep_rev = (kid == t_len - 1 - cq).astype(F32)
    til = (lax.broadcasted_iota(jnp.int32, (S5_CH, wid), 0)
           == (lax.broadcasted_iota(jnp.int32, (S5_CH, wid), 1) & (S5_CH - 1))).astype(F32)
    colb = lax.broadcasted_iota(jnp.int32, (S5_CH, wid), 1) >> 4
    kk = lax.broadcasted_iota(jnp.int32, (tp, S5_STATE), 0).astype(F32)
    rows = [None] * t_len
    for d in range(2):
        dt = jnp.exp(ldt_ref[d])
        lr, li = lr_ref[d], li_ref[d]
        mag = jnp.exp(kk * (lr * dt))
        ang = kk * (li * dt)
        pr, pi_ = mag * jnp.cos(ang), mag * jnp.sin(ang)
        lrc, lic = lrc_ref[d], lic_ref[d]
        magc = jnp.exp(lrc * dt)
        arc, aic = magc * jnp.cos(lic * dt), magc * jnp.sin(lic * dt)
        den = lrc * lrc + lic * lic
        nr, ni = arc - 1.0, aic
        kre = (nr * lrc + ni * lic) / den
        kim = (ni * lrc - nr * lic) / den
        bbr = kre * br_ref[d] - kim * bi_ref[d]
        bbi = kre * bi_ref[d] + kim * br_ref[d]
        repq = rep_rev if d == 0 else rep_nat
        repp = ((kid == cq + 1) if d == 0 else (kid == t_len - cq)).astype(F32)
        repm = rep_nat if d == 0 else rep_rev
        bt_r, bt_i = _dot_hi(bbr, til), _dot_hi(bbi, til)
        eqr, eqi = _dot_t_hi(pr, repq), _dot_t_hi(pi_, repq)
        qre_ref[d] = (eqr * bt_r - eqi * bt_i).astype(qre_ref.dtype)
        qim_ref[d] = (eqr * bt_i + eqi * bt_r).astype(qim_ref.dtype)
        epr, epi = _dot_t_hi(pr, repp), _dot_t_hi(pi_, repp)
        ct_r, ct_i = _dot_t_hi(cr_ref[d], til), _dot_t_hi(ci_ref[d], til)
        pre_ref[d] = (ct_r * epr - ct_i * epi).astype(pre_ref.dtype)
        pim_ref[d] = (-(ct_r * epi + ct_i * epr)).astype(pim_ref.dtype)
        at_ref[d, 0:1, :] = pr[t_len:t_len + 1, :]
        at_ref[d, 1:2, :] = pi_[t_len:t_len + 1, :]
        emr, emi = _dot_t_hi(pr, repm), _dot_t_hi(pi_, repm)
        y_re = emr * ct_r - emi * ct_i
        y_im = emr * ct_i + emi * ct_r
        r = _dot_t_hi(bbr, y_re) - _dot_t_hi(bbi, y_im)
        for s in range(t_len):
            if d == 0:
                blk = jnp.where(colb >= s, pltpu.roll(r, S5_CH * s, 1), 0.0)
            else:
                blk = jnp.where(colb <= s, pltpu.roll(r, (S5_CH * (s + 1)) % wid, 1), 0.0)
            rows[s] = blk if rows[s] is None else rows[s] + blk
    for s in range(t_len):
        m_ref[s * S5_CH:(s + 1) * S5_CH, :] = rows[s].astype(m_ref.dtype)


def _s5_params(lam_re, lam_im, log_dt, b_re, b_im, c_re, c_im):
    g, p, n = S5_GROUPS, S5_STATE, S5_CH
    wid = S5_T * S5_CH

    def spec(*shape):
        return pl.BlockSpec((2, None) + shape, lambda gi: (0, gi) + (0,) * len(shape))

    def ospec(*shape):
        return pl.BlockSpec((None,) + shape, lambda gi: (gi,) + (0,) * len(shape))
    return pl.pallas_call(
        _s5_param_kernel, grid=(g,),
        in_specs=[spec(1, p), spec(1, p), spec(p, 1), spec(p, 1), spec(1, 1),
                  spec(p, n), spec(p, n), spec(n, p), spec(n, p)],
        out_specs=[ospec(wid, wid), ospec(2, p, wid), ospec(2, p, wid), ospec(2, p, wid), ospec(2, p, wid),
                   ospec(2, 2, p)],
        out_shape=[jax.ShapeDtypeStruct((g, wid, wid), BF16),
                   jax.ShapeDtypeStruct((g, 2, p, wid), BF16),
                   jax.ShapeDtypeStruct((g, 2, p, wid), BF16),
                   jax.ShapeDtypeStruct((g, 2, p, wid), BF16),
                   jax.ShapeDtypeStruct((g, 2, p, wid), BF16),
                   jax.ShapeDtypeStruct((g, 2, 2, p), F32)],
        compiler_params=_params(("parallel",), 32),
        name="s5_params",
    )(lam_re.reshape(2, g, 1, p), lam_im.reshape(2, g, 1, p), lam_re.reshape(2, g, p, 1),
      lam_im.reshape(2, g, p, 1), log_dt.reshape(2, g, 1, 1), b_re, b_im, c_re, c_im)


def _s5_state_kernel(u_ref, qre_ref, qim_ref, s_ref):
    u = u_ref[...]
    dn = (((1,), (1,)), ((), ()))
    s_ref[:, 0:128] = lax.dot_general(u, qre_ref[...], dn, preferred_element_type=F32)
    s_ref[:, 128:256] = lax.dot_general(u, qim_ref[...], dn, preferred_element_type=F32)


def _s5_scan_kernel(s_ref, are_ref, aim_ref, h_ref, *, nc, ncc):
    ar, ai = are_ref[...], aim_ref[...]
    is_f = lax.broadcasted_iota(jnp.int32, ar.shape, 1) < S5_STATE

    def body(i, carry):
        hr, hi = carry
        cf = i
        cb = jnp.where(i < ncc, ncc - 1 - i, nc + ncc - 1 - i)
        h_ref[cf, :, 0:64] = hr[:, 0:64]
        h_ref[cb, :, 64:128] = hr[:, 64:128]
        h_ref[cf, :, 128:192] = hi[:, 0:64]
        h_ref[cb, :, 192:256] = hi[:, 64:128]
        sf, sb = s_ref[cf], s_ref[cb]
        sr = jnp.where(is_f, sf[:, 0:128], sb[:, 0:128])
        si = jnp.where(is_f, sf[:, 128:256], sb[:, 128:256])
        return ar * hr - ai * hi + sr, ar * hi + ai * hr + si

    zero = jnp.zeros(ar.shape, F32)
    lax.fori_loop(0, nc, body, (zero, zero))


def _s5_out_kernel(u_ref, m_ref, h_ref, p_ref, y_ref):
    y = jnp.dot(u_ref[...], m_ref[...], preferred_element_type=F32)
    y_ref[...] = y + jnp.dot(h_ref[...].astype(BF16), p_ref[...], preferred_element_type=F32)


def _s5_mixer(s_all, m, qre, qim, pre, pim, at, n_ctx):
    n = s_all.shape[0]
    g, t_len, wid = S5_GROUPS, S5_T, S5_T * S5_CH
    nc, ncc = n // t_len, n_ctx // t_len
    assert n % t_len == 0 and n_ctx % t_len == 0
    u = s_all.reshape(nc, t_len, g, S5_CH).transpose(2, 0, 1, 3).reshape(g, nc, wid)
    p_all = jnp.concatenate([pre.reshape(g, 128, wid), pim.reshape(g, 128, wid)], axis=1)
    a_re = at[:, :, 0, :].reshape(g, 128)
    a_im = at[:, :, 1, :].reshape(g, 128)
    s_t = pl.pallas_call(
        _s5_state_kernel, grid=(g,),
        in_specs=[pl.BlockSpec((None, nc, wid), lambda i: (i, 0, 0)),
                  pl.BlockSpec((None, 128, wid), lambda i: (i, 0, 0)),
                  pl.BlockSpec((None, 128, wid), lambda i: (i, 0, 0))],
        out_specs=pl.BlockSpec((nc, 256), lambda i: (0, i)),
        out_shape=jax.ShapeDtypeStruct((nc, g * 256), F32),
        compiler_params=_params(("parallel",), 32),
        name="s5_chunk_state",
    )(u, qre.reshape(g, 128, wid), qim.reshape(g, 128, wid))
    h_t = pl.pallas_call(
        functools.partial(_s5_scan_kernel, nc=nc, ncc=ncc),
        out_shape=jax.ShapeDtypeStruct((nc, g, 256), F32),
        compiler_params=pltpu.CompilerParams(vmem_limit_bytes=48 * MIB),
        name="s5_chunk_scan",
    )(s_t.reshape(nc, g, 256), a_re, a_im).reshape(nc, g * 256)
    y = pl.pallas_call(
        _s5_out_kernel, grid=(g,),
        in_specs=[pl.BlockSpec((None, nc, wid), lambda i: (i, 0, 0)),
                  pl.BlockSpec((None, wid, wid), lambda i: (i, 0, 0)),
                  pl.BlockSpec((nc, 256), lambda i: (0, i)),
                  pl.BlockSpec((None, 256, wid), lambda i: (i, 0, 0))],
        out_specs=pl.BlockSpec((None, nc, wid), lambda i: (i, 0, 0)),
        out_shape=jax.ShapeDtypeStruct((g, nc, wid), F32),
        compiler_params=_params(("parallel",), 32),
        name="s5_chunk_out",
    )(u, m, h_t, p_all)
    return y.reshape(g, nc, t_len, S5_CH).transpose(1, 2, 0, 3).reshape(n, W_S5)


def _gelu_tanh(x):
    return 0.5 * x * (1.0 + jnp.tanh(math.sqrt(2.0 / math.pi) * (x + 0.044715 * (x * x * x))))


def _even_out_kernel(pool_ref, y_ref, s_ref, dsk_ref, wglu_ref, wo_ref, h_ref, nw_ref, mod_ref, o_ref,
                     *, n_ctx, tm):
    y = _gelu_tanh(y_ref[...] + s_ref[...].astype(F32) * dsk_ref[...])
    gate = jnp.dot(y.astype(BF16), wglu_ref[...], preferred_element_type=F32)
    s5 = (y * _sigmoid(gate)).astype(BF16)
    mix = jnp.dot(pool_ref[...], wo_ref[0:W_POOL, :], preferred_element_type=F32)
    mix = mix + jnp.dot(s5, wo_ref[W_POOL:W_POOL + W_S5, :], preferred_element_type=F32)
    g = _gate_rows(mod_ref, pl.program_id(0) * tm, tm, n_ctx)
    o_ref[...] = h_ref[...] + g * _rms(mix, nw_ref[...])


def _even_out(pool_out, y_s5, z, dsk, w_glu, w_out, h, nw, mod, n_ctx):
    n, d = h.shape
    tm = _pick(n, (320, 256, 128))
    return pl.pallas_call(
        functools.partial(_even_out_kernel, n_ctx=n_ctx, tm=tm), grid=(n // tm,),
        in_specs=[pl.BlockSpec((tm, W_POOL), lambda i: (i, 0)),
                  pl.BlockSpec((tm, W_S5), lambda i: (i, 0)),
                  pl.BlockSpec((tm, W_S5), lambda i: (i, W_POOL // W_S5)),
                  pl.BlockSpec((1, W_S5), lambda i: (0, 0)),
                  pl.BlockSpec((W_S5, W_S5), lambda i: (0, 0)),
                  pl.BlockSpec((d, d), lambda i: (0, 0)),
                  pl.BlockSpec((tm, d), lambda i: (i, 0)),
                  pl.BlockSpec((1, d), lambda i: (0, 0)),
                  pl.BlockSpec((8, d), lambda i: (0, 0))],
        out_specs=pl.BlockSpec((tm, d), lambda i: (i, 0)),
        out_shape=jax.ShapeDtypeStruct((n, d), F32),
        compiler_params=_params(("parallel",), 48),
        name="even_mixer_out",
    )(pool_out, y_s5, z, dsk.reshape(1, W_S5), w_glu, w_out, h, nw.reshape(1, d), mod)


def _ffn_kernel(h_ref, nw2_ref, nw3_ref, mod_ref, wg_ref, wu_ref, wd_ref, o_ref, v_ref, *, n_ctx, tm):
    f = pl.program_id(1)

    @pl.when(f == 0)
    def _():
        _store_norm_mod(v_ref, h_ref, nw2_ref, mod_ref, pl.program_id(0) * tm, n_ctx)
        o_ref[...] = jnp.zeros_like(o_ref)
    v = v_ref[...]
    a = jnp.dot(v, wg_ref[...], preferred_element_type=F32)
    b = jnp.dot(v, wu_ref[...], preferred_element_type=F32)
    act = (a * _sigmoid(a) * b).astype(BF16)
    o_ref[...] += jnp.dot(act, wd_ref[...], preferred_element_type=F32)

    @pl.when(f == pl.num_programs(1) - 1)
    def _():
        g = _gate_rows(mod_ref, pl.program_id(0) * tm, tm, n_ctx)
        o_ref[...] = h_ref[...] + g * _rms(o_ref[...], nw3_ref[...])


def _dense_ffn(h, nw2, nw3, mod, wg, wu, wd, n_ctx):
    n, d = h.shape
    dff = wg.shape[1]
    tm = _pick(n, (640, 512, 256, 128))
    tf = 512
    return pl.pallas_call(
        functools.partial(_ffn_kernel, n_ctx=n_ctx, tm=tm), grid=(n // tm, dff // tf),
        in_specs=[pl.BlockSpec((tm, d), lambda i, f: (i, 0)),
                  pl.BlockSpec((1, d), lambda i, f: (0, 0)),
                  pl.BlockSpec((1, d), lambda i, f: (0, 0)),
                  pl.BlockSpec((8, d), lambda i, f: (0, 0)),
                  pl.BlockSpec((d, tf), lambda i, f: (0, f)),
                  pl.BlockSpec((d, tf), lambda i, f: (0, f)),
                  pl.BlockSpec((tf, d), lambda i, f: (f, 0))],
        out_specs=pl.BlockSpec((tm, d), lambda i, f: (i, 0)),
        out_shape=jax.ShapeDtypeStruct((n, d), F32),
        scratch_shapes=[pltpu.VMEM((tm, d), BF16)],
        compiler_params=_params(("parallel", "arbitrary"), 56),
        name="dense_swiglu",
    )(h, nw2.reshape(1, d), nw3.reshape(1, d), mod, wg, wu, wd)


def _split_bf16(x):
    hi = x.astype(BF16)
    return hi, (x - hi.astype(F32)).astype(BF16)


def _gla_kernel(qf_ref, kf_ref, vf_ref, af_ref, qb_ref, kb_ref, vb_ref, ab_ref, wa_ref, ba_ref,
                of_ref, ob_ref, st_ref, *, tb):
    c = GLA_CHUNK
    nsub = tb // c

    @pl.when(pl.program_id(0) == 0)
    def _():
        st_ref[...] = jnp.zeros_like(st_ref)
    ri = lax.broadcasted_iota(jnp.int32, (c, c), 0)
    ci = lax.broadcasted_iota(jnp.int32, (c, c), 1)
    scale = GLA_DK ** -0.5
    nt = (((1,), (1,)), ((), ()))
    tn = (((0,), (0,)), ((), ()))
    dirs = ((qf_ref, kf_ref, vf_ref, af_ref, of_ref), (qb_ref, kb_ref, vb_ref, ab_ref, ob_ref))
    for d, (q_ref, k_ref, v_ref, a_ref, o_ref) in enumerate(dirs):
        mask = (ri >= ci) if d == 0 else (ci >= ri)
        cum = mask.astype(BF16)
        a_hi, a_lo = _split_bf16(a_ref[...])
        w_hi, w_lo = _split_bf16(wa_ref[d])
        z = (jnp.dot(a_hi, w_hi, preferred_element_type=F32) + jnp.dot(a_hi, w_lo, preferred_element_type=F32)
             + jnp.dot(a_lo, w_hi, preferred_element_type=F32)) + ba_ref[d]
        glog = (jnp.minimum(z, 0.0) - jnp.log(1.0 + jnp.exp(-jnp.abs(z)))) * (1.0 / GLA_TAU)
        order = range(nsub) if d == 0 else range(nsub - 1, -1, -1)
        for sc in order:
            r0 = sc * c
            g_hi, g_lo = _split_bf16(glog[r0:r0 + c, :])
            b = jnp.dot(cum, g_hi, preferred_element_type=F32) + jnp.dot(cum, g_lo, preferred_element_type=F32)
            b_last = b[c - 1:c, :] if d == 0 else b[0:1, :]
            dec = jnp.exp(b_last)
            q = q_ref[r0:r0 + c, :].astype(F32) * scale
            k = k_ref[r0:r0 + c, :].astype(F32)
            qe = (q * jnp.exp(b)).astype(BF16)
            ke = (k * jnp.exp(-b)).astype(BF16)
            kd = (k * jnp.exp(b_last - b)).astype(BF16)
            for h in range(GLA_HEADS):
                ks = slice(h * GLA_DK, (h + 1) * GLA_DK)
                vs = slice(h * GLA_DV, (h + 1) * GLA_DV)
                att = lax.dot_general(qe[:, ks], ke[:, ks], nt, preferred_element_type=F32)
                att = jnp.where(mask, att, 0.0).astype(BF16)
                vh = v_ref[r0:r0 + c, vs]
                s_old = st_ref[d, h]
                o = jnp.dot(att, vh, preferred_element_type=F32)
                o = o + lax.dot_general(qe[:, ks], s_old.astype(BF16), nt, preferred_element_type=F32)
                o_ref[r0:r0 + c, vs] = o
                upd = lax.dot_general(vh, kd[:, ks], tn, preferred_element_type=F32)
                st_ref[d, h] = s_old * dec[:, ks] + upd


def _gla_mixer(zmain, a_aux, w_a, b_a, n_ctx):
    n = zmain.shape[0]
    tb = 256
    assert n % tb == 0 and n_ctx % tb == 0
    nb, ncb = n // tb, n_ctx // tb

    def bwd(s):
        return jnp.where(s < ncb, ncb - 1 - s, nb + ncb - 1 - s)
    wa = jnp.pad(w_a, ((0, 0), (0, 128 - GLA_RANK), (0, 0)))
    in_specs = []
    for order in (lambda s: s, bwd):
        in_specs += [pl.BlockSpec((tb, W_GLA_K), lambda s, o=order: (o(s), OQ // W_GLA_K)),
                     pl.BlockSpec((tb, W_GLA_K), lambda s, o=order: (o(s), OK_ // W_GLA_K)),
                     pl.BlockSpec((tb, W_GLA_V), lambda s, o=order: (o(s), OV // W_GLA_V)),
                     pl.BlockSpec((tb, 128), lambda s, o=order: (o(s), 0))]
    in_specs += [pl.BlockSpec((2, 128, W_GLA_K), lambda s: (0, 0, 0)),
                 pl.BlockSpec((2, 1, W_GLA_K), lambda s: (0, 0, 0))]
    return pl.pallas_call(
        functools.partial(_gla_kernel, tb=tb), grid=(nb,),
        in_specs=in_specs,
        out_specs=[pl.BlockSpec((tb, W_GLA_V), lambda s: (s, 0)),
                   pl.BlockSpec((tb, W_GLA_V), lambda s: (bwd(s), 0))],
        out_shape=[jax.ShapeDtypeStruct((n, W_GLA_V), F32)] * 2,
        scratch_shapes=[pltpu.VMEM((2, GLA_HEADS, GLA_DV, GLA_DK), F32)],
        compiler_params=_params(("arbitrary",), 32),
        name="gla_chunked",
    )(zmain, zmain, zmain, a_aux, zmain, zmain, zmain, a_aux, wa, b_a.reshape(2, 1, W_GLA_K))


_ATT_PAD = 2 * ATT_HEAD_DIM
_LOG2E = 1.4426950408889634
_FIXED_SHIFT_MAX_BOUND = 40.0


def _rope_fn(row0, tm, n_ctx):
    hd = ATT_HEAD_DIM
    row = row0 + lax.broadcasted_iota(jnp.int32, (tm, hd), 0)
    lane = lax.broadcasted_iota(jnp.int32, (tm, hd), 1)
    quarter = hd // 4
    first = (lane & quarter) == 0
    assert tm % GRID_W == 0 and n_ctx % GRID_W == 0
    reps = tm // GRID_W
    lane_s = lax.broadcasted_iota(jnp.int32, (GRID_W, hd), 1)
    freq = jnp.exp((lane_s % quarter).astype(F32) * (-math.log(ROPE_THETA) / quarter))
    ang_col = lax.broadcasted_iota(jnp.int32, (GRID_W, hd), 0).astype(F32) * freq
    grid_row0 = (row0 - n_ctx) // GRID_W
    ang_row = (grid_row0 + lax.broadcasted_iota(jnp.int32, (8, hd), 0)).astype(F32) * freq[0:8, :]
    cos_c, sin_c = jnp.cos(ang_col), jnp.sin(ang_col)
    cos_r, sin_r = jnp.cos(ang_row), jnp.sin(ang_row)

    def expand(tab_r, tab_c):
        by_row = jnp.concatenate([jnp.broadcast_to(tab_r[k:k + 1, :], (GRID_W, hd)) for k in range(reps)], axis=0)
        return jnp.where(lane < hd // 2, by_row, jnp.concatenate([tab_c] * reps, axis=0))
    cosv = expand(cos_r, cos_c)
    sinv = expand(sin_r, sin_c)
    sin_s = jnp.where(first, -sinv, sinv)

    def rope(x):
        sw = jnp.where(first, pltpu.roll(x, hd - quarter, 1), pltpu.roll(x, quarter, 1))
        return x * cosv + sw * sin_s
    return rope, row >= n_ctx


def _k_prep_kernel(k_ref, v_ref, kw_ref, kx_ref, vx_ref, kmax_ref, *, n_ctx, tm):
    hd = ATT_HEAD_DIM
    rope, is_lat = _rope_fn(pl.program_id(0) * tm, tm, n_ctx)
    lane = lax.broadcasted_iota(jnp.int32, (tm, hd), 1)
    one_col = jnp.where(lane == 0, 1.0, 0.0).astype(BF16)
    ones = jnp.ones((tm, hd), BF16)
    nmax = jnp.zeros((tm, 1), F32)
    for h in range(ATT_KV_HEADS):
        sl = slice(h * hd, (h + 1) * hd)
        xn = _rms(k_ref[:, sl].astype(F32), kw_ref[...])
        kr = jnp.where(is_lat, rope(xn), xn).astype(BF16)
        kx_ref[:, h * _ATT_PAD:h * _ATT_PAD + hd] = kr
        kx_ref[:, h * _ATT_PAD + hd:(h + 1) * _ATT_PAD] = one_col
        vx_ref[:, h * _ATT_PAD:h * _ATT_PAD + hd] = v_ref[:, sl]
        vx_ref[:, h * _ATT_PAD + hd:(h + 1) * _ATT_PAD] = ones
        krf = kr.astype(F32)
        nmax = jnp.maximum(nmax, jnp.sum(krf * krf, axis=-1, keepdims=True))
    kmax_ref[...] = jnp.broadcast_to(jnp.max(nmax, axis=0, keepdims=True), kmax_ref.shape)


def _q_prep_kernel(q_ref, qw_ref, kn_ref, qx_ref, bmax_ref, *, n_ctx, tm):
    hd = ATT_HEAD_DIM
    rope, _ = _rope_fn(n_ctx + pl.program_id(0) * tm, tm, n_ctx)
    lane = lax.broadcasted_iota(jnp.int32, (tm, hd), 1)
    scale = hd ** -0.5 * _LOG2E
    bmax = jnp.zeros((tm, 1), F32)
    for h in range(ATT_HEADS):
        sl = slice(h * hd, (h + 1) * hd)
        xn = _rms(q_ref[:, sl].astype(F32), qw_ref[...])
        qr = (rope(xn) * scale).astype(BF16)
        qf = qr.astype(F32)
        bound = jnp.sqrt(jnp.sum(qf * qf, axis=-1, keepdims=True)) * kn_ref[...]
        qx_ref[:, h * _ATT_PAD:h * _ATT_PAD + hd] = qr
        qx_ref[:, h * _ATT_PAD + hd:(h + 1) * _ATT_PAD] = jnp.where(lane == 0, -bound, 0.0).astype(BF16)
        bmax = jnp.maximum(bmax, bound)
    bmax_ref[...] = jnp.broadcast_to(jnp.max(bmax, axis=0, keepdims=True), bmax_ref.shape)


def _qk_prep(zmain, q_norm, k_norm, n_ctx):
    n = zmain.shape[0]
    lq = n - n_ctx
    tm = 256
    hd = ATT_HEAD_DIM
    assert n_ctx % tm == 0
    ncb = n_ctx // tm
    kx, vx, kmax = pl.pallas_call(
        functools.partial(_k_prep_kernel, n_ctx=n_ctx, tm=tm), grid=(n // tm,),
        in_specs=[pl.BlockSpec((tm, W_ATT_KV), lambda i: (i, OAK // W_ATT_KV)),
                  pl.BlockSpec((tm, W_ATT_KV), lambda i: (i, OAV // W_ATT_KV)),
                  pl.BlockSpec((1, hd), lambda i: (0, 0))],
        out_specs=[pl.BlockSpec((tm, ATT_KV_HEADS * _ATT_PAD), lambda i: (i, 0)),
                   pl.BlockSpec((tm, ATT_KV_HEADS * _ATT_PAD), lambda i: (i, 0)),
                   pl.BlockSpec((None, 8, 128), lambda i: (i, 0, 0))],
        out_shape=[jax.ShapeDtypeStruct((n, ATT_KV_HEADS * _ATT_PAD), BF16),
                   jax.ShapeDtypeStruct((n, ATT_KV_HEADS * _ATT_PAD), BF16),
                   jax.ShapeDtypeStruct((n // tm, 8, 128), F32)],
        compiler_params=_params(("parallel",), 32),
        name="k_norm_rope",
    )(zmain, zmain, k_norm.reshape(1, hd))
    knorm = jnp.sqrt(jnp.max(kmax)).reshape(1, 1)
    qx, bmax = pl.pallas_call(
        functools.partial(_q_prep_kernel, n_ctx=n_ctx, tm=tm), grid=(lq // tm,),
        in_specs=[pl.BlockSpec((tm, W_ATT), lambda i: (i + ncb, OAQ // W_ATT)),
                  pl.BlockSpec((1, hd), lambda i: (0, 0)),
                  pl.BlockSpec((1, 1), lambda i: (0, 0))],
        out_specs=[pl.BlockSpec((tm, ATT_HEADS * _ATT_PAD), lambda i: (i, 0)),
                   pl.BlockSpec((None, 8, 128), lambda i: (i, 0, 0))],
        out_shape=[jax.ShapeDtypeStruct((lq, ATT_HEADS * _ATT_PAD), BF16),
                   jax.ShapeDtypeStruct((lq // tm, 8, 128), F32)],
        compiler_params=_params(("parallel",), 32),
        name="q_norm_rope",
    )(zmain, q_norm.reshape(1, hd), knorm)
    return qx, kx, vx, jnp.max(bmax)


def _flash_fixed_kernel(q_ref, kt_ref, v_ref, o_ref, acc_ref):
    c = pl.program_id(2)
    hd = ATT_HEAD_DIM

    @pl.when(c == 0)
    def _():
        acc_ref[...] = jnp.zeros_like(acc_ref)
    kt = kt_ref[...]
    v = v_ref[...]
    for g in range(ATT_GROUP):
        s = jnp.dot(q_ref[:, g * _ATT_PAD:(g + 1) * _ATT_PAD], kt, preferred_element_type=F32)
        acc_ref[g] += jnp.dot(jnp.exp2(s).astype(BF16), v, preferred_element_type=F32)

    @pl.when(c == pl.num_programs(2) - 1)
    def _():
        for g in range(ATT_GROUP):
            a = acc_ref[g]
            o_ref[:, g * hd:(g + 1) * hd] = (a[:, 0:hd] / a[:, hd:2 * hd]).astype(o_ref.dtype)


def _flash_online_kernel(q_ref, kt_ref, v_ref, o_ref, m_ref, acc_ref):
    c = pl.program_id(2)
    hd = ATT_HEAD_DIM

    @pl.when(c == 0)
    def _():
        m_ref[...] = jnp.full(m_ref.shape, -jnp.inf, F32)
        acc_ref[...] = jnp.zeros_like(acc_ref)
    kt = kt_ref[...]
    v = v_ref[...]
    for g in range(ATT_GROUP):
        s = jnp.dot(q_ref[:, g * _ATT_PAD:(g + 1) * _ATT_PAD], kt, preferred_element_type=F32)
        m_prev = m_ref[g]
        m_new = jnp.maximum(m_prev, jnp.max(s, axis=-1, keepdims=True))
        p = jnp.exp2(s - m_new).astype(BF16)
        acc_ref[g] = jnp.exp2(m_prev - m_new) * acc_ref[g] + jnp.dot(p, v, preferred_element_type=F32)
        m_ref[g] = m_new

    @pl.when(c == pl.num_programs(2) - 1)
    def _():
        for g in range(ATT_GROUP):
            a = acc_ref[g]
            o_ref[:, g * hd:(g + 1) * hd] = (a[:, 0:hd] / a[:, hd:2 * hd]).astype(o_ref.dtype)


def _attention(qx, kx, vx, bound_log2):
    lq = qx.shape[0]
    n = kx.shape[0]
    kt = kx.T
    gw = ATT_GROUP * _ATT_PAD
    ow = ATT_GROUP * ATT_HEAD_DIM
    def call(kern, tq, tk, scratch, name):
        return pl.pallas_call(
            kern, grid=(ATT_KV_HEADS, lq // tq, n // tk),
            in_specs=[pl.BlockSpec((tq, gw), lambda h, i, j: (i, h)),
                      pl.BlockSpec((_ATT_PAD, tk), lambda h, i, j: (h, j)),
                      pl.BlockSpec((tk, _ATT_PAD), lambda h, i, j: (j, h))],
            out_specs=pl.BlockSpec((tq, ow), lambda h, i, j: (i, h)),
            out_shape=jax.ShapeDtypeStruct((lq, W_ATT), BF16),
            scratch_shapes=scratch(tq) + [pltpu.VMEM((ATT_GROUP, tq, _ATT_PAD), F32)],
            compiler_params=_params(("parallel", "parallel", "arbitrary"), 56),
            name=name,
        )(qx, kt, vx)

    def fixed(_):
        return call(_flash_fixed_kernel, _pick(lq, (1024, 512, 256, 128)), _pick(n, (3328, 1280, 512, 256)),
                    lambda tq: [], "gqa_flash_fixed_shift")

    def online(_):
        return call(_flash_online_kernel, _pick(lq, (512, 256, 128)), _pick(n, (640, 512, 256, 128)),
                    lambda tq: [pltpu.VMEM((ATT_GROUP, tq, 1), F32)], "gqa_flash_online")
    return lax.cond(bound_log2 <= _FIXED_SHIFT_MAX_BOUND * _LOG2E, fixed, online, None)


def _odd_out_kernel(of_ref, ob_ref, r_ref, att_ref, gn_ref, wo_ref, h_ref, nw_ref, mod_ref, o_ref):
    o = of_ref[...] + ob_ref[...]
    r = r_ref[...].astype(F32)
    silu_r = r * _sigmoid(r)
    parts = []
    for h in range(GLA_HEADS):
        vs = slice(h * GLA_DV, (h + 1) * GLA_DV)
        parts.append((_rms(o[:, vs], gn_ref[:, vs]) * silu_r[:, vs]).astype(BF16))
    gla = jnp.concatenate(parts, axis=1)
    mix = jnp.dot(gla, wo_ref[0:W_GLA_V, :], preferred_element_type=F32)
    mix = mix + jnp.dot(att_ref[...], wo_ref[W_GLA_V:W_GLA_V + W_ATT, :], preferred_element_type=F32)
    o_ref[...] = h_ref[...] + mod_ref[4:5, :] * _rms(mix, nw_ref[...])


def _odd_out(o_f, o_b, zmain, att, gla_norm, w_out, h_all, nw, mod, n_ctx):
    n, d = h_all.shape
    lq = n - n_ctx
    tm = _pick(lq, (256, 128))
    assert n_ctx % tm == 0
    off = n_ctx // tm
    return pl.pallas_call(
        _odd_out_kernel, grid=(lq // tm,),
        in_specs=[pl.BlockSpec((tm, W_GLA_V), lambda i: (i + off, 0)),
                  pl.BlockSpec((tm, W_GLA_V), lambda i: (i + off, 0)),
                  pl.BlockSpec((tm, W_GLA_V), lambda i: (i + off, OR // W_GLA_V)),
                  pl.BlockSpec((tm, W_ATT), lambda i: (i, 0)),
                  pl.BlockSpec((1, W_GLA_V), lambda i: (0, 0)),
                  pl.BlockSpec((d, d), lambda i: (0, 0)),
                  pl.BlockSpec((tm, d), lambda i: (i + off, 0)),
                  pl.BlockSpec((1, d), lambda i: (0, 0)),
                  pl.BlockSpec((8, d), lambda i: (0, 0))],
        out_specs=pl.BlockSpec((tm, d), lambda i: (i, 0)),
        out_shape=jax.ShapeDtypeStruct((lq, d), F32),
        compiler_params=_params(("parallel",), 48),
        name="odd_mixer_out",
    )(o_f, o_b, zmain, att, gla_norm.reshape(1, W_GLA_V), w_out, h_all, nw.reshape(1, d), mod)


def _router_kernel(h_ref, nw_ref, mod_ref, wr_ref, br_ref, v_ref, r_ref):
    x = h_ref[...]
    v = _rms(x, nw_ref[...]) * (1.0 + mod_ref[0:1, :]) + mod_ref[1:2, :]
    v_ref[...] = v
    logits = _dot_hi(v, wr_ref[...]) + br_ref[...]
    lane = lax.broadcasted_iota(jnp.int32, logits.shape, 1)
    m1 = jnp.max(logits, axis=-1, keepdims=True)
    i1 = jnp.min(jnp.where(logits == m1, lane, 128), axis=-1, keepdims=True)
    rest = jnp.where(lane == i1, -jnp.inf, logits)
    m2 = jnp.max(rest, axis=-1, keepdims=True)
    i2 = jnp.min(jnp.where(rest == m2, lane, 128), axis=-1, keepdims=True)
    e2 = jnp.exp(m2 - m1)
    g1 = 1.0 / (1.0 + e2)
    g2 = e2 / (1.0 + e2)
    out = jnp.where(lane == 0, i1.astype(F32), 0.0)
    out = jnp.where(lane == 1, i2.astype(F32), out)
    out = jnp.where(lane == 2, g1, out)
    out = jnp.where(lane == 3, g2, out)
    r_ref[...] = out


def _router(h_lat, nw, mod, w_router, b_router):
    lq, d = h_lat.shape
    tm = _pick(lq, (512, 256, 128))
    wr = jnp.pad(w_router, ((0, 0), (0, 128 - N_EXPERTS)))
    br = jnp.pad(b_router, (0, 128 - N_EXPERTS), constant_values=-1e30).reshape(1, 128)
    return pl.pallas_call(
        _router_kernel, grid=(lq // tm,),
        in_specs=[pl.BlockSpec((tm, d), lambda i: (i, 0)),
                  pl.BlockSpec((1, d), lambda i: (0, 0)),
                  pl.BlockSpec((8, d), lambda i: (0, 0)),
                  pl.BlockSpec((d, 128), lambda i: (0, 0)),
                  pl.BlockSpec((1, 128), lambda i: (0, 0))],
        out_specs=[pl.BlockSpec((tm, d), lambda i: (i, 0)), pl.BlockSpec((tm, 128), lambda i: (i, 0))],
        out_shape=[jax.ShapeDtypeStruct((lq, d), F32), jax.ShapeDtypeStruct((lq, 128), F32)],
        compiler_params=_params(("parallel",), 40),
        name="router_top2",
    )(h_lat, nw.reshape(1, d), mod, wr, br)


_TILE_PARTS = 4


def _expert_kernel(te_ref, tv_ref, src_ref, dst_ref, v_hbm, wg_ref, wu_ref, wd_ref, y_hbm,
                   xg_ref, xb_ref, ys_ref, acc_ref, gsem, ssem, *, tm, nt, nf):
    j = pl.program_id(0)
    f = pl.program_id(1)
    slot = j % 2
    per_step = tm // nf
    lo = f * per_step
    valid = tv_ref[j] != 0

    def start_gather(tile, slot_, row, queue=0):
        tok = src_ref[tile * tm + row]
        pltpu.make_async_copy(v_hbm.at[pl.ds(tok, 1)], xg_ref.at[slot_, pl.ds(row, 1)],
                              gsem.at[slot_]).start(priority=queue)

    def start_scatter(row, queue=0):
        dst = dst_ref[j * tm + row]
        pltpu.make_async_copy(ys_ref.at[1 - slot, pl.ds(row, 1)], y_hbm.at[pl.ds(dst, 1)],
                              ssem.at[1 - slot]).start(priority=queue)

    def wait_scatter(slot_):
        pltpu.make_async_copy(ys_ref.at[slot_], y_hbm.at[pl.ds(0, tm)], ssem.at[slot_]).wait()

    def gather_loop(tile, slot_, first, count):
        def body(r, carry):
            start_gather(tile, slot_, first + r)
            return carry
        lax.fori_loop(0, count, body, 0, unroll=8)

    @pl.when((j == 0) & (f == 0))
    def _():
        gather_loop(0, 0, 0, tm)
        ys_ref[1] = jnp.zeros((tm, ys_ref.shape[2]), F32)

    @pl.when(f == 0)
    def _():
        pltpu.make_async_copy(v_hbm.at[pl.ds(0, tm)], xg_ref.at[slot], gsem.at[slot]).wait()
        xb_ref[...] = xg_ref[slot].astype(BF16)
        acc_ref[...] = jnp.zeros_like(acc_ref)
        for row in range(nf * per_step, tm):
            start_scatter(row)

        @pl.when(j + 1 < nt)
        def _():
            for row in range(nf * per_step, tm):
                start_gather(j + 1, 1 - slot, row)

    def compute(rows_used):
        half = per_step // 2

        def gathers(first, last, after):
            for r in range(first, last):
                start_gather(j + 1, 1 - slot, lo + r + after, queue=r % 2)

        def scatters(first, last, after):
            for r in range(first, last):
                start_scatter(lo + r + after, queue=r % 2)

        def zero_after(val):
            bits = pltpu.bitcast(val[0:8, 0:128], jnp.int32)
            return ((bits & 1) >> 1)[0, 0]

        gathers(0, half, 0)
        x = xb_ref[0:rows_used, :]
        a = jnp.dot(x, wg_ref[...], preferred_element_type=F32)
        gathers(half, per_step, zero_after(a))
        b = jnp.dot(x, wu_ref[...], preferred_element_type=F32)
        scatters(0, half, zero_after(b))
        act = (a * _sigmoid(a) * b).astype(BF16)
        down = jnp.dot(act, wd_ref[...], preferred_element_type=F32)
        scatters(half, per_step, zero_after(down))
        acc_ref[0:rows_used, :] += down

    for quarters in range(1, _TILE_PARTS + 1):
        pl.when(tv_ref[j] == quarters)(functools.partial(compute, quarters * tm // _TILE_PARTS))

    @pl.when(jnp.logical_not(valid))
    def _():
        @pl.when(j + 1 < nt)
        def _():
            gather_loop(j + 1, 1 - slot, lo, per_step)

        def body(r, carry):
            start_scatter(lo + r)
            return carry
        lax.fori_loop(0, per_step, body, 0, unroll=8)

    @pl.when(f == nf - 1)
    def _():
        @pl.when(j > 0)
        def _():
            wait_scatter(slot)
        ys_ref[slot] = acc_ref[...]

        @pl.when(j == nt - 1)
        def _():
            wait_scatter(1 - slot)


def _combine_kernel(y0_ref, y1_ref, route_ref, h_ref, nw_ref, mod_ref, o_ref):
    route = route_ref[...]
    f = y0_ref[...] * route[:, 2:3] + y1_ref[...] * route[:, 3:4]
    o_ref[...] = h_ref[...] + mod_ref[4:5, :] * _rms(f, nw_ref[...])


def _count_le(sorted_vals, queries):
    return jnp.sum((sorted_vals[None, :] <= queries[:, None]).astype(jnp.int32), axis=1)


def _moe(h_lat, nw2, nw3, mod, w_router, b_router, wg, wu, wd):
    lq, d = h_lat.shape
    e = N_EXPERTS
    dff = wg.shape[2]
    tm = _pick(lq, (512, 256))
    tf = 1024
    v, route = _router(h_lat, nw2, mod, w_router, b_router)
    idx = route[:, 0:2].astype(jnp.int32)

    npairs = 2 * lq
    nt = npairs // tm + e + 1
    flat_e = idx.reshape(-1)
    pair_id = jnp.arange(npairs, dtype=jnp.int32)
    onehot = (flat_e[:, None] == jnp.arange(e, dtype=jnp.int32)[None, :]).astype(jnp.int32)
    csum = jnp.cumsum(onehot, axis=0)
    rank = jnp.sum(csum * onehot, axis=1) - 1
    counts = csum[-1]
    padded = ((counts + tm - 1) // tm) * tm
    ends = jnp.cumsum(padded)
    starts = ends - padded
    pos = jnp.sum(onehot * starts[None, :], axis=1) + rank
    sorted_pair = jnp.full((nt * tm,), -1, jnp.int32).at[pos].set(pair_id)
    is_pad = sorted_pair < 0
    src_row = jnp.where(is_pad, 0, sorted_pair >> 1).astype(jnp.int32)
    pad_rank = jnp.cumsum(is_pad.astype(jnp.int32)) - 1
    dst_row = jnp.where(is_pad, npairs + pad_rank, (sorted_pair & 1) * lq + (sorted_pair >> 1)).astype(jnp.int32)
    dst_row = jnp.concatenate([(nt - 1) * tm + jnp.arange(tm, dtype=jnp.int32), dst_row])
    n_used = ends[-1] // tm
    tile_ids = jnp.arange(nt, dtype=jnp.int32)
    tile_exp = jnp.minimum(_count_le(ends, jnp.minimum(tile_ids, n_used - 1) * tm), e - 1)
    onehot_t = (tile_exp[:, None] == jnp.arange(e, dtype=jnp.int32)[None, :]).astype(jnp.int32)
    real_end = jnp.sum(onehot_t * (starts + counts)[None, :], axis=1)
    tile_rows = jnp.clip(real_end - tile_ids * tm, 0, tm)
    part = tm // _TILE_PARTS
    tile_valid = jnp.where(tile_ids < n_used, (tile_rows + part - 1) // part, 0).astype(jnp.int32)

    nf = dff // tf
    pinned = lambda j, f, tv: jnp.where(tv[j] != 0, f, nf - 1)
    y = pl.pallas_call(
        functools.partial(_expert_kernel, tm=tm, nt=nt, nf=nf),
        grid_spec=pltpu.PrefetchScalarGridSpec(
            num_scalar_prefetch=4, grid=(nt, nf),
            in_specs=[pl.BlockSpec(memory_space=pl.ANY),
                      pl.BlockSpec((None, d, tf), lambda j, f, te, tv, src, dst: (te[j], 0, pinned(j, f, tv))),
                      pl.BlockSpec((None, d, tf), lambda j, f, te, tv, src, dst: (te[j], 0, pinned(j, f, tv))),
                      pl.BlockSpec((None, tf, d), lambda j, f, te, tv, src, dst: (te[j], pinned(j, f, tv), 0))],
            out_specs=pl.BlockSpec(memory_space=pl.ANY),
            scratch_shapes=[pltpu.VMEM((2, tm, d), F32), pltpu.VMEM((tm, d), BF16), pltpu.VMEM((2, tm, d), F32),
                            pltpu.VMEM((tm, d), F32), pltpu.SemaphoreType.DMA((2,)),
                            pltpu.SemaphoreType.DMA((2,))]),
        out_shape=jax.ShapeDtypeStruct((nt * tm, d), F32),
        compiler_params=pltpu.CompilerParams(dimension_semantics=("arbitrary", "arbitrary"),
                                             vmem_limit_bytes=56 * MIB, disable_bounds_checks=True),
        name="moe_experts",
    )(tile_exp, tile_valid, src_row, dst_row, v, wg, wu, wd)

    tc = _pick(lq, (256, 128))
    return pl.pallas_call(
        _combine_kernel, grid=(lq // tc,),
        in_specs=[pl.BlockSpec((tc, d), lambda i: (i, 0)),
                  pl.BlockSpec((tc, d), lambda i: (i + lq // tc, 0)),
                  pl.BlockSpec((tc, 128), lambda i: (i, 0)),
                  pl.BlockSpec((tc, d), lambda i: (i, 0)),
                  pl.BlockSpec((1, d), lambda i: (0, 0)),
                  pl.BlockSpec((8, d), lambda i: (0, 0))],
        out_specs=pl.BlockSpec((tc, d), lambda i: (i, 0)),
        out_shape=jax.ShapeDtypeStruct((lq, d), F32),
        compiler_params=_params(("parallel",), 32),
        name="moe_combine",
    )(y, y, route, h_lat, nw3.reshape(1, d), mod)


def kernel(x, c, ctx, c_ctx, w_mod, b_mod, norms, e_w_in, e_pool_w, e_pool_scale, e_s5_lam_re, e_s5_lam_im, e_s5_log_dt, e_s5_b_re, e_s5_b_im, e_s5_c_re, e_s5_c_im, e_s5_d, e_s5_w_glu, e_w_out, e_ffn_gate, e_ffn_up, e_ffn_down, o_w_in, o_gla_w_a, o_gla_b_a, o_gla_norm, o_q_norm, o_k_norm, o_w_out, o_router, o_router_b, o_moe_gate, o_moe_up, o_moe_down):
    assert x.shape[0] == 1 and w_mod.shape[0] == 2 and e_w_in.shape[0] == 1 and o_w_in.shape[0] == 1
    d = x.shape[2]
    n_ctx = ctx.shape[1]
    h = jnp.concatenate([ctx[0], x[0]], axis=0)

    vecs = jnp.zeros((8, d), F32).at[0].set(c[0]).at[1].set(c_ctx)
    mods = _modulation(vecs, w_mod, b_mod)

    mod1, mod2 = _mod_rows(mods[0], 0), _mod_rows(mods[0], 1)
    z = _normed_matmul(h, norms[0, 0], mod1, e_w_in[0].astype(BF16), n_ctx)
    pool_out = _pool_mixer(z, e_pool_w[0].astype(BF16), e_pool_scale[0], n_ctx)
    m_op, qre, qim, pre, pim, at = _s5_params(e_s5_lam_re[0], e_s5_lam_im[0], e_s5_log_dt[0],
                                              e_s5_b_re[0], e_s5_b_im[0], e_s5_c_re[0], e_s5_c_im[0])
    y_s5 = _s5_mixer(z[:, W_POOL:], m_op, qre, qim, pre, pim, at, n_ctx)
    h = _even_out(pool_out, y_s5, z, e_s5_d[0], e_s5_w_glu[0].astype(BF16), e_w_out[0].astype(BF16),
                  h, norms[0, 1], mod1, n_ctx)
    h = _dense_ffn(h, norms[0, 2], norms[0, 3], mod2, e_ffn_gate[0].astype(BF16),
                   e_ffn_up[0].astype(BF16), e_ffn_down[0].astype(BF16), n_ctx)

    mod1, mod2 = _mod_rows(mods[1], 0), _mod_rows(mods[1], 1)
    w_in = o_w_in[0]
    a0 = 2 * W_GLA_K + W_GLA_V
    w_main = jnp.concatenate([w_in[:, :a0], w_in[:, a0 + GLA_RANK:]], axis=1).astype(BF16)
    w_aux = jnp.pad(w_in[:, a0:a0 + GLA_RANK], ((0, 0), (0, 128 - GLA_RANK))).astype(BF16)
    zmain, a_aux = _normed_matmul(h, norms[1, 0], mod1, w_main, n_ctx, w_aux=w_aux)
    o_f, o_b = _gla_mixer(zmain, a_aux, o_gla_w_a[0], o_gla_b_a[0], n_ctx)
    qx, kx, vx, bound_log2 = _qk_prep(zmain, o_q_norm[0], o_k_norm[0], n_ctx)
    att = _attention(qx, kx, vx, bound_log2)
    h_lat = _odd_out(o_f, o_b, zmain, att, o_gla_norm[0], o_w_out[0].astype(BF16), h, norms[1, 1], mod1, n_ctx)
    out = _moe(h_lat, norms[1, 2], norms[1, 3], mod2, o_router[0], o_router_b[0],
               o_moe_gate[0].astype(BF16), o_moe_up[0].astype(BF16), o_moe_down[0].astype(BF16))
    return out[None]
```
